```python
import math
import jax, jax.numpy as jnp
from jax import lax
import numpy as np

D_MODEL = 1024
BATCH = 8
SEQ = 4096
DEPTH = 4

CHUNK = 64
MIX_W = 3 * D_MODEL // 4
N_BRANCH = 3
CONV_K = 4
HG_DK = 128
HG_DV = 128
HG_HEADS = MIX_W // HG_DV
F_MIN = 1e-30
SSM_HEADDIM = 64
SSM_HEADS = MIX_W // SSM_HEADDIM
SSM_STATE = 128
SSM_GROUPS = 2
SSM_CONV_CH = MIX_W + 2 * SSM_GROUPS * SSM_STATE
GDN_DK = 128
GDN_DV = 128
GDN_HEADS = MIX_W // GDN_DV
GDN_QKV = GDN_HEADS * (2 * GDN_DK + GDN_DV)
FFN_HIDDEN = (((8 * D_MODEL + 2) // 3 + 255) // 256) * 256
NORM_EPS = 1e-6

IN_SIZES = (
    HG_HEADS * HG_DK,
    HG_HEADS * HG_DK,
    HG_HEADS * HG_DV,
    HG_HEADS * HG_DV,
    MIX_W,
    SSM_CONV_CH,
    SSM_HEADS,
    GDN_QKV,
    GDN_HEADS * GDN_DV,
    GDN_HEADS,
    GDN_HEADS,
    N_BRANCH * D_MODEL,
)
IN_WIDTH = sum(IN_SIZES)
IN_SPLITS = tuple(int(v) for v in np.cumsum(IN_SIZES)[:-1])

kernel_name = 'hybrid_hgrn2_ssd_gdn_streaming_trunk'


def rms_norm(x, w):
    xf = x.astype(jnp.float32)
    y = xf * lax.rsqrt(jnp.mean(xf * xf, axis=-1, keepdims=True) + NORM_EPS)
    return y.astype(x.dtype) * w


def l2_normalize(x):
    xf = x.astype(jnp.float32)
    return (xf * lax.rsqrt(jnp.sum(xf * xf, axis=-1, keepdims=True) + NORM_EPS)).astype(x.dtype)


def causal_depthwise_conv(x, w, b=None):
    k, ch = w.shape
    y = lax.conv_general_dilated(x, w[:, None, :], window_strides=(1,), padding=((k - 1, 0),),
                                 dimension_numbers=('NWC', 'WIO', 'NWC'), feature_group_count=ch)
    return y if b is None else y + b


def to_chunks(t):
    bsz, s, h, d = t.shape
    return t.reshape(bsz, s // CHUNK, CHUNK, h, d).transpose(1, 0, 3, 2, 4)


def from_chunks(t):
    nc, bsz, h, c, d = t.shape
    return t.transpose(1, 0, 3, 2, 4).reshape(bsz, nc * c, h, d)


def causal_mask(strict=False):
    return jnp.tril(jnp.ones((CHUNK, CHUNK), dtype=bool), -1 if strict else 0)


def masked_exp(diff, mask):
    return jnp.where(mask, jnp.exp(jnp.where(mask, diff, 0.0)), 0.0)


def gla_chunked(q, k, v, logf):
    bsz, _, heads, dk = q.shape
    dv = v.shape[-1]
    mask = causal_mask()[:, :, None]

    def step(state, inp):
        q_c, k_c, v_c, b_c = inp
        o_inter = jnp.einsum('bhtk,bhkv->bhtv', q_c * jnp.exp(b_c), state)
        pair = masked_exp(b_c[:, :, :, None, :] - b_c[:, :, None, :, :], mask)
        scores = jnp.einsum('bhtk,bhsk,bhtsk->bhts', q_c, k_c, pair)
        o = o_inter + jnp.einsum('bhts,bhsv->bhtv', scores, v_c)
        b_end = b_c[:, :, -1:, :]
        new_state = (jnp.exp(b_end[:, :, 0, :, None]) * state
                     + jnp.einsum('bhsk,bhsv->bhkv', k_c * jnp.exp(b_end - b_c), v_c))
        return new_state, o

    b = jnp.cumsum(to_chunks(logf), axis=3)
    init = jnp.zeros((bsz, heads, dk, dv), v.dtype)
    _, o = lax.scan(step, init, (to_chunks(q), to_chunks(k), to_chunks(v), b))
    return from_chunks(o)


def ssd_chunked(cm, bm, xdt, loga):
    qc, kc, vc = to_chunks(cm), to_chunks(bm), to_chunks(xdt)
    cum = jnp.cumsum(to_chunks(loga[..., None])[..., 0], axis=-1)
    seg = masked_exp(cum[..., :, None] - cum[..., None, :], causal_mask())
    y_intra = jnp.einsum('nbhts,nbhsp->nbhtp', jnp.einsum('nbhtk,nbhsk->nbhts', qc, kc) * seg, vc)
    chunk_states = jnp.einsum('nbhsk,nbhsp->nbhkp', kc * jnp.exp(cum[..., -1:] - cum)[..., None], vc)
    chunk_decay = jnp.exp(cum[..., -1])

    def carry(state, inp):
        local, dec = inp
        return dec[..., None, None] * state + local, state

    init = jnp.zeros(chunk_states.shape[1:], chunk_states.dtype)
    _, starts = lax.scan(carry, init, (chunk_states, chunk_decay))
    y_inter = jnp.einsum('nbhtk,nbhkp->nbhtp', qc * jnp.exp(cum)[..., None], starts)
    return from_chunks(y_intra + y_inter)


def gated_delta_chunked(q, k, v, logg, beta):
    bsz, _, heads, dk = q.shape
    dv = v.shape[-1]
    qc, kc, vc = to_chunks(q), to_chunks(k), to_chunks(v)
    bc = to_chunks(beta[..., None])[..., 0]
    cum = jnp.cumsum(to_chunks(logg[..., None])[..., 0], axis=-1)
    decay = masked_exp(cum[..., :, None] - cum[..., None, :], causal_mask())
    kk = jnp.einsum('nbhtk,nbhsk->nbhts', kc, kc)
    a_low = jnp.where(causal_mask(strict=True), bc[..., :, None] * kk * decay, 0.0)
    rhs = jnp.concatenate([vc * bc[..., None], kc * (bc * jnp.exp(cum))[..., None]], axis=-1)
    eye = jnp.eye(CHUNK, dtype=jnp.float32)
    sol = lax.linalg.triangular_solve(eye + a_low.astype(jnp.float32), rhs.astype(jnp.float32),
                                      left_side=True, lower=True, unit_diagonal=True).astype(v.dtype)
    u_base, w_corr = sol[..., :dv], sol[..., dv:]
    qk = jnp.einsum('nbhtk,nbhsk->nbhts', qc, kc) * decay

    def step(state, inp):
        q_c, k_c, u_b, w_c, qk_c, cum_c = inp
        u = u_b - jnp.einsum('bhtk,bhkv->bhtv', w_c, state)
        o = (jnp.einsum('bhtk,bhkv->bhtv', q_c * jnp.exp(cum_c)[..., None], state)
             + jnp.einsum('bhts,bhsv->bhtv', qk_c, u))
        new_state = (jnp.exp(cum_c[..., -1])[..., None, None] * state
                     + jnp.einsum('bhsk,bhsv->bhkv', k_c * jnp.exp(cum_c[..., -1:] - cum_c)[..., None], u))
        return new_state, o

    init = jnp.zeros((bsz, heads, dk, dv), v.dtype)
    _, o = lax.scan(step, init, (qc, kc, u_base, w_corr, qk, cum))
    return from_chunks(o)


def hgrn2_branch(q_raw, f_raw, i_raw, g_raw, lower_bound, norm_w):
    bsz, s, _ = q_raw.shape
    q = jax.nn.silu(q_raw)
    f = lower_bound + (1 - lower_bound) * jax.nn.sigmoid(f_raw)
    logf = jnp.log(jnp.maximum(f, F_MIN))
    k = (1 - lower_bound) * jax.nn.sigmoid(-f_raw)
    o = gla_chunked(q.reshape(bsz, s, HG_HEADS, HG_DK), k.reshape(bsz, s, HG_HEADS, HG_DK),
                    i_raw.reshape(bsz, s, HG_HEADS, HG_DV), logf.reshape(bsz, s, HG_HEADS, HG_DK))
    o = rms_norm(o, norm_w) * jax.nn.silu(g_raw.reshape(bsz, s, HG_HEADS, HG_DV))
    return o.reshape(bsz, s, MIX_W)


def mamba2_branch(z, xbc, dt_raw, conv_w, conv_b, dt_bias, a_log, d_skip, norm_w):
    bsz, s, _ = z.shape
    xbc = jax.nn.silu(causal_depthwise_conv(xbc, conv_w, conv_b))
    xs, bm, cm = jnp.split(xbc, [MIX_W, MIX_W + SSM_GROUPS * SSM_STATE], axis=-1)
    xs = xs.reshape(bsz, s, SSM_HEADS, SSM_HEADDIM)
    rep = SSM_HEADS // SSM_GROUPS
    bm = jnp.repeat(bm.reshape(bsz, s, SSM_GROUPS, SSM_STATE), rep, axis=2)
    cm = jnp.repeat(cm.reshape(bsz, s, SSM_GROUPS, SSM_STATE), rep, axis=2)
    dt = jax.nn.softplus(dt_raw + dt_bias)
    loga = -jnp.exp(a_log) * dt
    y = ssd_chunked(cm, bm, xs * dt[..., None], loga) + d_skip[:, None] * xs
    y = y.reshape(bsz, s, MIX_W) * jax.nn.silu(z)
    gw = MIX_W // SSM_GROUPS
    y = rms_norm(y.reshape(bsz, s, SSM_GROUPS, gw), norm_w.reshape(SSM_GROUPS, gw))
    return y.reshape(bsz, s, MIX_W)


def gdn_branch(qkv, z, b_raw, a_raw, conv_w, dt_bias, a_log, norm_w):
    bsz, s, _ = z.shape
    qkv = jax.nn.silu(causal_depthwise_conv(qkv, conv_w))
    q, k, v = jnp.split(qkv, [GDN_HEADS * GDN_DK, 2 * GDN_HEADS * GDN_DK], axis=-1)
    q = l2_normalize(q.reshape(bsz, s, GDN_HEADS, GDN_DK)) * (GDN_DK ** -0.5)
    k = l2_normalize(k.reshape(bsz, s, GDN_HEADS, GDN_DK))
    v = v.reshape(bsz, s, GDN_HEADS, GDN_DV)
    beta = jax.nn.sigmoid(b_raw)
    logg = -jnp.exp(a_log) * jax.nn.softplus(a_raw + dt_bias)
    o = gated_delta_chunked(q, k, v, logg, beta)
    o = rms_norm(o, norm_w) * jax.nn.silu(z.reshape(bsz, s, GDN_HEADS, GDN_DV))
    return o.reshape(bsz, s, MIX_W)


def _fwd_setup_inputs(seed: int = 0) -> dict:
    key = jax.random.key(seed)
    ks = iter(jax.random.split(key, 40))

    def normal(shape, scale):
        return scale * jax.random.normal(next(ks), shape, jnp.float32)

    def uniform(shape, lo, hi):
        return jax.random.uniform(next(ks), shape, jnp.float32, lo, hi)

    def dt_bias_init(shape):
        dt = jnp.exp(uniform(shape, math.log(1e-3), math.log(1e-1)))
        return dt + jnp.log(-jnp.expm1(-dt))

    L = DEPTH
    return {
        'x': normal((BATCH, SEQ, D_MODEL), 1.0),
        'c': normal((BATCH, D_MODEL), 1.0),
        'w_ada': normal((L, D_MODEL, 6 * D_MODEL), 0.1 * D_MODEL ** -0.5),
        'b_ada': normal((L, 6 * D_MODEL), 0.01),
        'norm_mix': 1.0 + normal((L, D_MODEL), 0.02),
        'norm_ffn': 1.0 + normal((L, D_MODEL), 0.02),
        'w_in': normal((L, D_MODEL, IN_WIDTH), D_MODEL ** -0.5),
        'b_merge': normal((L, N_BRANCH * D_MODEL), 0.01),
        'hgrn_lb_logits': normal((L, HG_HEADS * HG_DK), 0.1),
        'hgrn_norm': 1.0 + normal((L, HG_DV), 0.02),
        'ssm_conv_w': normal((L, CONV_K, SSM_CONV_CH), CONV_K ** -0.5),
        'ssm_conv_b': normal((L, SSM_CONV_CH), 0.01),
        'ssm_dt_bias': dt_bias_init((L, SSM_HEADS)),
        'ssm_a_log': jnp.log(uniform((L, SSM_HEADS), 1.0, 16.0)),
        'ssm_d': 1.0 + normal((L, SSM_HEADS), 0.02),
        'ssm_norm': 1.0 + normal((L, MIX_W), 0.02),
        'gdn_conv_w': normal((L, CONV_K, GDN_QKV), CONV_K ** -0.5),
        'gdn_dt_bias': dt_bias_init((L, GDN_HEADS)),
        'gdn_a_log': jnp.log(uniform((L, GDN_HEADS), 1.0, 16.0)),
        'gdn_norm': 1.0 + normal((L, GDN_DV), 0.02),
        'w_branch': normal((L, N_BRANCH, MIX_W, D_MODEL), MIX_W ** -0.5),
        'w_out': normal((L, D_MODEL, D_MODEL), D_MODEL ** -0.5),
        'w_ffn_in': normal((L, D_MODEL, 2 * FFN_HIDDEN), D_MODEL ** -0.5),
        'w_ffn_out': normal((L, FFN_HIDDEN, D_MODEL), FFN_HIDDEN ** -0.5),
        'norm_final': 1.0 + normal((D_MODEL,), 0.02),
    }


def _fwd_reference(x, c, w_ada, b_ada, norm_mix, norm_ffn, w_in, b_merge, hgrn_lb_logits, hgrn_norm,
              ssm_conv_w, ssm_conv_b, ssm_dt_bias, ssm_a_log, ssm_d, ssm_norm,
              gdn_conv_w, gdn_dt_bias, gdn_a_log, gdn_norm,
              w_branch, w_out, w_ffn_in, w_ffn_out, norm_final):
    bsz, s, _ = x.shape
    p = jax.nn.softmax(hgrn_lb_logits.astype(jnp.float32), axis=0)
    lower_bounds = (jnp.cumsum(p, axis=0) - p[0]).astype(x.dtype)
    c_act = jax.nn.silu(c)
    for layer in range(DEPTH):
        mod = c_act @ w_ada[layer] + b_ada[layer]
        sh1, sc1, g1, sh2, sc2, g2 = [m[:, None, :] for m in jnp.split(mod, 6, axis=-1)]
        h = rms_norm(x, norm_mix[layer]) * (1 + sc1) + sh1
        (hq, hf, hi, hg, sz, sxbc, sdt, gqkv, gz, gb, ga, gate_logits) = jnp.split(
            h @ w_in[layer], IN_SPLITS, axis=-1)
        y_hgrn = hgrn2_branch(hq, hf, hi, hg, lower_bounds[layer], hgrn_norm[layer])
        y_ssm = mamba2_branch(sz, sxbc, sdt, ssm_conv_w[layer], ssm_conv_b[layer], ssm_dt_bias[layer],
                              ssm_a_log[layer], ssm_d[layer], ssm_norm[layer])
        y_gdn = gdn_branch(gqkv, gz, gb, ga, gdn_conv_w[layer], gdn_dt_bias[layer], gdn_a_log[layer],
                           gdn_norm[layer])
        branch_out = jnp.einsum('nbsw,nwd->nbsd', jnp.stack([y_hgrn, y_ssm, y_gdn], axis=0), w_branch[layer])
        gates = jax.nn.sigmoid(gate_logits + b_merge[layer]).reshape(bsz, s, N_BRANCH, D_MODEL)
        merged = jnp.einsum('bsnd,nbsd->bsd', gates, branch_out)
        x = x + (1 + g1) * (merged @ w_out[layer])
        h = rms_norm(x, norm_ffn[layer]) * (1 + sc2) + sh2
        gate, up = jnp.split(h @ w_ffn_in[layer], 2, axis=-1)
        x = x + (1 + g2) * ((jax.nn.silu(gate) * up) @ w_ffn_out[layer])
    return rms_norm(x, norm_final)


import jax as _jax
import jax.numpy as _jnp

TWIN_FORMAT = 'train_step'
FWD_PARAMS = ['x', 'c', 'w_ada', 'b_ada', 'norm_mix', 'norm_ffn', 'w_in', 'b_merge', 'hgrn_lb_logits', 'hgrn_norm', 'ssm_conv_w', 'ssm_conv_b', 'ssm_dt_bias', 'ssm_a_log', 'ssm_d', 'ssm_norm', 'gdn_conv_w', 'gdn_dt_bias', 'gdn_a_log', 'gdn_norm', 'w_branch', 'w_out', 'w_ffn_in', 'w_ffn_out', 'norm_final']
TWIN_WEIGHTS = ['w_ada', 'b_ada', 'norm_mix', 'norm_ffn', 'w_in', 'b_merge', 'hgrn_lb_logits', 'hgrn_norm', 'ssm_conv_w', 'ssm_conv_b', 'ssm_dt_bias', 'ssm_a_log', 'ssm_d', 'ssm_norm', 'gdn_conv_w', 'gdn_dt_bias', 'gdn_a_log', 'gdn_norm', 'w_branch', 'w_out', 'w_ffn_in', 'w_ffn_out', 'norm_final']
TWIN_DIFF_INPUT = 'x'
TWIN_INPUTS = ['x', 'c', 'w_ada', 'b_ada', 'norm_mix', 'norm_ffn', 'w_in', 'b_merge', 'hgrn_lb_logits', 'hgrn_norm', 'ssm_conv_w', 'ssm_conv_b', 'ssm_dt_bias', 'ssm_a_log', 'ssm_d', 'ssm_norm', 'gdn_conv_w', 'gdn_dt_bias', 'gdn_a_log', 'gdn_norm', 'w_branch', 'w_out', 'w_ffn_in', 'w_ffn_out', 'norm_final', 'loss_target', 'm_w_ada', 'm_b_ada', 'm_norm_mix', 'm_norm_ffn', 'm_w_in', 'm_b_merge', 'm_hgrn_lb_logits', 'm_hgrn_norm', 'm_ssm_conv_w', 'm_ssm_conv_b', 'm_ssm_dt_bias', 'm_ssm_a_log', 'm_ssm_d', 'm_ssm_norm', 'm_gdn_conv_w', 'm_gdn_dt_bias', 'm_gdn_a_log', 'm_gdn_norm', 'm_w_branch', 'm_w_out', 'm_w_ffn_in', 'm_w_ffn_out', 'm_norm_final', 'v_w_ada', 'v_b_ada', 'v_norm_mix', 'v_norm_ffn', 'v_w_in', 'v_b_merge', 'v_hgrn_lb_logits', 'v_hgrn_norm', 'v_ssm_conv_w', 'v_ssm_conv_b', 'v_ssm_dt_bias', 'v_ssm_a_log', 'v_ssm_d', 'v_ssm_norm', 'v_gdn_conv_w', 'v_gdn_dt_bias', 'v_gdn_a_log', 'v_gdn_norm', 'v_w_branch', 'v_w_out', 'v_w_ffn_in', 'v_w_ffn_out', 'v_norm_final']
TWIN_OUTPUTS = ['loss', 'grad_x', 'grad_w_ada', 'grad_b_ada', 'grad_norm_mix', 'grad_norm_ffn', 'grad_w_in', 'grad_b_merge', 'grad_hgrn_lb_logits', 'grad_hgrn_norm', 'grad_ssm_conv_w', 'grad_ssm_conv_b', 'grad_ssm_dt_bias', 'grad_ssm_a_log', 'grad_ssm_d', 'grad_ssm_norm', 'grad_gdn_conv_w', 'grad_gdn_dt_bias', 'grad_gdn_a_log', 'grad_gdn_norm', 'grad_w_branch', 'grad_w_out', 'grad_w_ffn_in', 'grad_w_ffn_out', 'grad_norm_final', 'delta_w_ada', 'delta_b_ada', 'delta_norm_mix', 'delta_norm_ffn', 'delta_w_in', 'delta_b_merge', 'delta_hgrn_lb_logits', 'delta_hgrn_norm', 'delta_ssm_conv_w', 'delta_ssm_conv_b', 'delta_ssm_dt_bias', 'delta_ssm_a_log', 'delta_ssm_d', 'delta_ssm_norm', 'delta_gdn_conv_w', 'delta_gdn_dt_bias', 'delta_gdn_a_log', 'delta_gdn_norm', 'delta_w_branch', 'delta_w_out', 'delta_w_ffn_in', 'delta_w_ffn_out', 'delta_norm_final', 'new_m_w_ada', 'new_m_b_ada', 'new_m_norm_mix', 'new_m_norm_ffn', 'new_m_w_in', 'new_m_b_merge', 'new_m_hgrn_lb_logits', 'new_m_hgrn_norm', 'new_m_ssm_conv_w', 'new_m_ssm_conv_b', 'new_m_ssm_dt_bias', 'new_m_ssm_a_log', 'new_m_ssm_d', 'new_m_ssm_norm', 'new_m_gdn_conv_w', 'new_m_gdn_dt_bias', 'new_m_gdn_a_log', 'new_m_gdn_norm', 'new_m_w_branch', 'new_m_w_out', 'new_m_w_ffn_in', 'new_m_w_ffn_out', 'new_m_norm_final', 'new_v_w_ada', 'new_v_b_ada', 'new_v_norm_mix', 'new_v_norm_ffn', 'new_v_w_in', 'new_v_b_merge', 'new_v_hgrn_lb_logits', 'new_v_hgrn_norm', 'new_v_ssm_conv_w', 'new_v_ssm_conv_b', 'new_v_ssm_dt_bias', 'new_v_ssm_a_log', 'new_v_ssm_d', 'new_v_ssm_norm', 'new_v_gdn_conv_w', 'new_v_gdn_dt_bias', 'new_v_gdn_a_log', 'new_v_gdn_norm', 'new_v_w_branch', 'new_v_w_out', 'new_v_w_ffn_in', 'new_v_w_ffn_out', 'new_v_norm_final']
TWIN_LEAF_KINDS = {'loss': 'loss', 'grad_x': 'grad_x', 'grad_w_ada': 'grad_w', 'grad_b_ada': 'grad_w', 'grad_norm_mix': 'grad_w', 'grad_norm_ffn': 'grad_w', 'grad_w_in': 'grad_w', 'grad_b_merge': 'grad_w', 'grad_hgrn_lb_logits': 'grad_w', 'grad_hgrn_norm': 'grad_w', 'grad_ssm_conv_w': 'grad_w', 'grad_ssm_conv_b': 'grad_w', 'grad_ssm_dt_bias': 'grad_w', 'grad_ssm_a_log': 'grad_w', 'grad_ssm_d': 'grad_w', 'grad_ssm_norm': 'grad_w', 'grad_gdn_conv_w': 'grad_w', 'grad_gdn_dt_bias': 'grad_w', 'grad_gdn_a_log': 'grad_w', 'grad_gdn_norm': 'grad_w', 'grad_w_branch': 'grad_w', 'grad_w_out': 'grad_w', 'grad_w_ffn_in': 'grad_w', 'grad_w_ffn_out': 'grad_w', 'grad_norm_final': 'grad_w', 'delta_w_ada': 'delta_w', 'delta_b_ada': 'delta_w', 'delta_norm_mix': 'delta_w', 'delta_norm_ffn': 'delta_w', 'delta_w_in': 'delta_w', 'delta_b_merge': 'delta_w', 'delta_hgrn_lb_logits': 'delta_w', 'delta_hgrn_norm': 'delta_w', 'delta_ssm_conv_w': 'delta_w', 'delta_ssm_conv_b': 'delta_w', 'delta_ssm_dt_bias': 'delta_w', 'delta_ssm_a_log': 'delta_w', 'delta_ssm_d': 'delta_w', 'delta_ssm_norm': 'delta_w', 'delta_gdn_conv_w': 'delta_w', 'delta_gdn_dt_bias': 'delta_w', 'delta_gdn_a_log': 'delta_w', 'delta_gdn_norm': 'delta_w', 'delta_w_branch': 'delta_w', 'delta_w_out': 'delta_w', 'delta_w_ffn_in': 'delta_w', 'delta_w_ffn_out': 'delta_w', 'delta_norm_final': 'delta_w', 'new_m_w_ada': 'new_m', 'new_m_b_ada': 'new_m', 'new_m_norm_mix': 'new_m', 'new_m_norm_ffn': 'new_m', 'new_m_w_in': 'new_m', 'new_m_b_merge': 'new_m', 'new_m_hgrn_lb_logits': 'new_m', 'new_m_hgrn_norm': 'new_m', 'new_m_ssm_conv_w': 'new_m', 'new_m_ssm_conv_b': 'new_m', 'new_m_ssm_dt_bias': 'new_m', 'new_m_ssm_a_log': 'new_m', 'new_m_ssm_d': 'new_m', 'new_m_ssm_norm': 'new_m', 'new_m_gdn_conv_w': 'new_m', 'new_m_gdn_dt_bias': 'new_m', 'new_m_gdn_a_log': 'new_m', 'new_m_gdn_norm': 'new_m', 'new_m_w_branch': 'new_m', 'new_m_w_out': 'new_m', 'new_m_w_ffn_in': 'new_m', 'new_m_w_ffn_out': 'new_m', 'new_m_norm_final': 'new_m', 'new_v_w_ada': 'new_v', 'new_v_b_ada': 'new_v', 'new_v_norm_mix': 'new_v', 'new_v_norm_ffn': 'new_v', 'new_v_w_in': 'new_v', 'new_v_b_merge': 'new_v', 'new_v_hgrn_lb_logits': 'new_v', 'new_v_hgrn_norm': 'new_v', 'new_v_ssm_conv_w': 'new_v', 'new_v_ssm_conv_b': 'new_v', 'new_v_ssm_dt_bias': 'new_v', 'new_v_ssm_a_log': 'new_v', 'new_v_ssm_d': 'new_v', 'new_v_ssm_norm': 'new_v', 'new_v_gdn_conv_w': 'new_v', 'new_v_gdn_dt_bias': 'new_v', 'new_v_gdn_a_log': 'new_v', 'new_v_gdn_norm': 'new_v', 'new_v_w_branch': 'new_v', 'new_v_w_out': 'new_v', 'new_v_w_ffn_in': 'new_v', 'new_v_w_ffn_out': 'new_v', 'new_v_norm_final': 'new_v'}


def _forward(args):
    return _fwd_reference(*[args[k] for k in FWD_PARAMS])


def _output_shape():
    out = _jax.eval_shape(lambda: _forward(_fwd_setup_inputs(0)))
    return out.shape, out.dtype

N_MICROBATCH = 1
ADAM_LR = 0.001
ADAM_B1 = 0.9
ADAM_B2 = 0.999
ADAM_EPS = 1e-08
ADAM_WD = 0.01
ADAM_STEP = 10
PER_EXAMPLE_BATCH_AXIS = {'x': 0, 'c': 0, 'loss_target': 0}
SHARED_INPUTS = []
_WEIGHT_DTYPES = {'w_ada': _jnp.float32, 'b_ada': _jnp.float32, 'norm_mix': _jnp.float32, 'norm_ffn': _jnp.float32, 'w_in': _jnp.float32, 'b_merge': _jnp.float32, 'hgrn_lb_logits': _jnp.float32, 'hgrn_norm': _jnp.float32, 'ssm_conv_w': _jnp.float32, 'ssm_conv_b': _jnp.float32, 'ssm_dt_bias': _jnp.float32, 'ssm_a_log': _jnp.float32, 'ssm_d': _jnp.float32, 'ssm_norm': _jnp.float32, 'gdn_conv_w': _jnp.float32, 'gdn_dt_bias': _jnp.float32, 'gdn_a_log': _jnp.float32, 'gdn_norm': _jnp.float32, 'w_branch': _jnp.float32, 'w_out': _jnp.float32, 'w_ffn_in': _jnp.float32, 'w_ffn_out': _jnp.float32, 'norm_final': _jnp.float32}
MOMENT_SCALE = {'w_ada': 9.874010e-02, 'b_ada': 1.622246e-01, 'norm_mix': 1.930271e-01, 'norm_ffn': 1.226247e-01, 'w_in': 5.733375e-02, 'b_merge': 2.755267e-02, 'hgrn_lb_logits': 3.635036e-03, 'hgrn_norm': 1.563921e-01, 'ssm_conv_w': 8.888824e-02, 'ssm_conv_b': 1.337139e-01, 'ssm_dt_bias': 2.110786e-01, 'ssm_a_log': 4.936626e-01, 'ssm_d': 7.218538e-01, 'ssm_norm': 1.113652e-01, 'gdn_conv_w': 4.819280e-02, 'gdn_dt_bias': 6.186268e-01, 'gdn_a_log': 6.502706e-01, 'gdn_norm': 1.669809e-01, 'w_branch': 6.943514e-02, 'w_out': 1.201461e-01, 'w_ffn_in': 5.308060e-02, 'w_ffn_out': 8.665950e-02, 'norm_final': 3.201502e+01}


def _to_microbatches(a, axis):
    t = _jnp.moveaxis(a, axis, 0)
    t = t.reshape((N_MICROBATCH, t.shape[0] // N_MICROBATCH) + t.shape[1:])
    return _jnp.moveaxis(t, 1, axis + 1)


def setup_inputs(seed: int = 0) -> dict:
    inp = _fwd_setup_inputs(seed)
    key = _jax.random.fold_in(_jax.random.key(seed), 7919)
    shape, _ = _output_shape()
    out = dict(inp)
    out["loss_target"] = _jax.random.normal(_jax.random.fold_in(key, 0), shape, _jnp.float32)
    for i, name in enumerate(TWIN_WEIGHTS):
        w = inp[name].astype(_jnp.float32)
        if MOMENT_SCALE is None:
            s = _jnp.sqrt(_jnp.mean(_jnp.square(w)) + 1e-30)
        else:
            s = MOMENT_SCALE[name]
        km, kv = _jax.random.split(_jax.random.fold_in(key, i + 1))
        out[name] = w
        out["m_" + name] = s * _jax.random.normal(km, w.shape, _jnp.float32)
        out["v_" + name] = (s * s) * _jax.random.uniform(kv, w.shape, _jnp.float32, 0.5, 1.5)
    if N_MICROBATCH > 1:
        for name, axis in PER_EXAMPLE_BATCH_AXIS.items():
            out[name] = _to_microbatches(out[name], axis)
    return {'x': out['x'], 'c': out['c'], 'w_ada': out['w_ada'], 'b_ada': out['b_ada'], 'norm_mix': out['norm_mix'], 'norm_ffn': out['norm_ffn'], 'w_in': out['w_in'], 'b_merge': out['b_merge'], 'hgrn_lb_logits': out['hgrn_lb_logits'], 'hgrn_norm': out['hgrn_norm'], 'ssm_conv_w': out['ssm_conv_w'], 'ssm_conv_b': out['ssm_conv_b'], 'ssm_dt_bias': out['ssm_dt_bias'], 'ssm_a_log': out['ssm_a_log'], 'ssm_d': out['ssm_d'], 'ssm_norm': out['ssm_norm'], 'gdn_conv_w': out['gdn_conv_w'], 'gdn_dt_bias': out['gdn_dt_bias'], 'gdn_a_log': out['gdn_a_log'], 'gdn_norm': out['gdn_norm'], 'w_branch': out['w_branch'], 'w_out': out['w_out'], 'w_ffn_in': out['w_ffn_in'], 'w_ffn_out': out['w_ffn_out'], 'norm_final': out['norm_final'], 'loss_target': out['loss_target'], 'm_w_ada': out['m_w_ada'], 'm_b_ada': out['m_b_ada'], 'm_norm_mix': out['m_norm_mix'], 'm_norm_ffn': out['m_norm_ffn'], 'm_w_in': out['m_w_in'], 'm_b_merge': out['m_b_merge'], 'm_hgrn_lb_logits': out['m_hgrn_lb_logits'], 'm_hgrn_norm': out['m_hgrn_norm'], 'm_ssm_conv_w': out['m_ssm_conv_w'], 'm_ssm_conv_b': out['m_ssm_conv_b'], 'm_ssm_dt_bias': out['m_ssm_dt_bias'], 'm_ssm_a_log': out['m_ssm_a_log'], 'm_ssm_d': out['m_ssm_d'], 'm_ssm_norm': out['m_ssm_norm'], 'm_gdn_conv_w': out['m_gdn_conv_w'], 'm_gdn_dt_bias': out['m_gdn_dt_bias'], 'm_gdn_a_log': out['m_gdn_a_log'], 'm_gdn_norm': out['m_gdn_norm'], 'm_w_branch': out['m_w_branch'], 'm_w_out': out['m_w_out'], 'm_w_ffn_in': out['m_w_ffn_in'], 'm_w_ffn_out': out['m_w_ffn_out'], 'm_norm_final': out['m_norm_final'], 'v_w_ada': out['v_w_ada'], 'v_b_ada': out['v_b_ada'], 'v_norm_mix': out['v_norm_mix'], 'v_norm_ffn': out['v_norm_ffn'], 'v_w_in': out['v_w_in'], 'v_b_merge': out['v_b_merge'], 'v_hgrn_lb_logits': out['v_hgrn_lb_logits'], 'v_hgrn_norm': out['v_hgrn_norm'], 'v_ssm_conv_w': out['v_ssm_conv_w'], 'v_ssm_conv_b': out['v_ssm_conv_b'], 'v_ssm_dt_bias': out['v_ssm_dt_bias'], 'v_ssm_a_log': out['v_ssm_a_log'], 'v_ssm_d': out['v_ssm_d'], 'v_ssm_norm': out['v_ssm_norm'], 'v_gdn_conv_w': out['v_gdn_conv_w'], 'v_gdn_dt_bias': out['v_gdn_dt_bias'], 'v_gdn_a_log': out['v_gdn_a_log'], 'v_gdn_norm': out['v_gdn_norm'], 'v_w_branch': out['v_w_branch'], 'v_w_out': out['v_w_out'], 'v_w_ffn_in': out['v_w_ffn_in'], 'v_w_ffn_out': out['v_w_ffn_out'], 'v_norm_final': out['v_norm_final']}


def _loss(weights, diff, rest, loss_target):
    with _jax.named_scope("forward"):
        args = {**rest, TWIN_DIFF_INPUT: diff, **{k: w.astype(_WEIGHT_DTYPES[k]) for k, w in weights.items()}}
        y = _forward(args)
    with _jax.named_scope("loss_head"):
        err = _jnp.square(y.astype(_jnp.float32) - loss_target)
        return 0.5 * _jnp.sum(_jnp.mean(err, axis=-1)) if err.ndim else 0.5 * err


def _adamw(w, g, m, v):
    m = ADAM_B1 * m + (1.0 - ADAM_B1) * g
    v = ADAM_B2 * v + (1.0 - ADAM_B2) * _jnp.square(g)
    m_hat = m / (1.0 - ADAM_B1 ** ADAM_STEP)
    v_hat = v / (1.0 - ADAM_B2 ** ADAM_STEP)
    delta = -ADAM_LR * (m_hat / (_jnp.sqrt(v_hat) + ADAM_EPS) + ADAM_WD * w)
    return delta, m, v


def reference(x, c, w_ada, b_ada, norm_mix, norm_ffn, w_in, b_merge, hgrn_lb_logits, hgrn_norm, ssm_conv_w, ssm_conv_b, ssm_dt_bias, ssm_a_log, ssm_d, ssm_norm, gdn_conv_w, gdn_dt_bias, gdn_a_log, gdn_norm, w_branch, w_out, w_ffn_in, w_ffn_out, norm_final, loss_target, m_w_ada, m_b_ada, m_norm_mix, m_norm_ffn, m_w_in, m_b_merge, m_hgrn_lb_logits, m_hgrn_norm, m_ssm_conv_w, m_ssm_conv_b, m_ssm_dt_bias, m_ssm_a_log, m_ssm_d, m_ssm_norm, m_gdn_conv_w, m_gdn_dt_bias, m_gdn_a_log, m_gdn_norm, m_w_branch, m_w_out, m_w_ffn_in, m_w_ffn_out, m_norm_final, v_w_ada, v_b_ada, v_norm_mix, v_norm_ffn, v_w_in, v_b_merge, v_hgrn_lb_logits, v_hgrn_norm, v_ssm_conv_w, v_ssm_conv_b, v_ssm_dt_bias, v_ssm_a_log, v_ssm_d, v_ssm_norm, v_gdn_conv_w, v_gdn_dt_bias, v_gdn_a_log, v_gdn_norm, v_w_branch, v_w_out, v_w_ffn_in, v_w_ffn_out, v_norm_final):
    given = dict(x=x, c=c, w_ada=w_ada, b_ada=b_ada, norm_mix=norm_mix, norm_ffn=norm_ffn, w_in=w_in, b_merge=b_merge, hgrn_lb_logits=hgrn_lb_logits, hgrn_norm=hgrn_norm, ssm_conv_w=ssm_conv_w, ssm_conv_b=ssm_conv_b, ssm_dt_bias=ssm_dt_bias, ssm_a_log=ssm_a_log, ssm_d=ssm_d, ssm_norm=ssm_norm, gdn_conv_w=gdn_conv_w, gdn_dt_bias=gdn_dt_bias, gdn_a_log=gdn_a_log, gdn_norm=gdn_norm, w_branch=w_branch, w_out=w_out, w_ffn_in=w_ffn_in, w_ffn_out=w_ffn_out, norm_final=norm_final, loss_target=loss_target, m_w_ada=m_w_ada, m_b_ada=m_b_ada, m_norm_mix=m_norm_mix, m_norm_ffn=m_norm_ffn, m_w_in=m_w_in, m_b_merge=m_b_merge, m_hgrn_lb_logits=m_hgrn_lb_logits, m_hgrn_norm=m_hgrn_norm, m_ssm_conv_w=m_ssm_conv_w, m_ssm_conv_b=m_ssm_conv_b, m_ssm_dt_bias=m_ssm_dt_bias, m_ssm_a_log=m_ssm_a_log, m_ssm_d=m_ssm_d, m_ssm_norm=m_ssm_norm, m_gdn_conv_w=m_gdn_conv_w, m_gdn_dt_bias=m_gdn_dt_bias, m_gdn_a_log=m_gdn_a_log, m_gdn_norm=m_gdn_norm, m_w_branch=m_w_branch, m_w_out=m_w_out, m_w_ffn_in=m_w_ffn_in, m_w_ffn_out=m_w_ffn_out, m_norm_final=m_norm_final, v_w_ada=v_w_ada, v_b_ada=v_b_ada, v_norm_mix=v_norm_mix, v_norm_ffn=v_norm_ffn, v_w_in=v_w_in, v_b_merge=v_b_merge, v_hgrn_lb_logits=v_hgrn_lb_logits, v_hgrn_norm=v_hgrn_norm, v_ssm_conv_w=v_ssm_conv_w, v_ssm_conv_b=v_ssm_conv_b, v_ssm_dt_bias=v_ssm_dt_bias, v_ssm_a_log=v_ssm_a_log, v_ssm_d=v_ssm_d, v_ssm_norm=v_ssm_norm, v_gdn_conv_w=v_gdn_conv_w, v_gdn_dt_bias=v_gdn_dt_bias, v_gdn_a_log=v_gdn_a_log, v_gdn_norm=v_gdn_norm, v_w_branch=v_w_branch, v_w_out=v_w_out, v_w_ffn_in=v_w_ffn_in, v_w_ffn_out=v_w_ffn_out, v_norm_final=v_norm_final)
    weights = {n: given[n] for n in TWIN_WEIGHTS}
    shared = {n: given[n] for n in SHARED_INPUTS}
    per_example = {n: given[n] for n in ['x', 'c']}
    grad_fn = _jax.value_and_grad(_loss, argnums=(0, 1))

    def one_microbatch(ex, loss_target):
        ex = dict(ex)
        diff = ex.pop(TWIN_DIFF_INPUT)
        return grad_fn(weights, diff, {**shared, **ex}, loss_target)

    if N_MICROBATCH == 1:
        loss, (grad_w, grad_x) = one_microbatch(per_example, given["loss_target"])
    else:
        def body(carry, xs):
            loss_sum, grad_sum = carry
            l_k, (gw_k, gx_k) = one_microbatch(xs[0], xs[1])
            with _jax.named_scope("update"):
                return (loss_sum + l_k, _jax.tree.map(_jnp.add, grad_sum, gw_k)), gx_k

        init = (_jnp.zeros((), _jnp.float32), _jax.tree.map(_jnp.zeros_like, weights))
        (loss, grad_w), grad_x = _jax.lax.scan(body, init, (per_example, given["loss_target"]))
    with _jax.named_scope("update"):
        delta_w, new_m, new_v = {}, {}, {}
        for n in TWIN_WEIGHTS:
            delta_w[n], new_m[n], new_v[n] = _adamw(weights[n], grad_w[n], given["m_" + n], given["v_" + n])
    return (loss, grad_x, *[grad_w[n] for n in TWIN_WEIGHTS], *[delta_w[n] for n in TWIN_WEIGHTS],
            *[new_m[n] for n in TWIN_WEIGHTS], *[new_v[n] for n in TWIN_WEIGHTS])
```

```python
import functools
import math

import numpy as np
import jax
import jax.numpy as jnp
from jax import lax
from jax.experimental import pallas as pl
from jax.experimental.pallas import tpu as pltpu

F32 = jnp.float32
BF16 = jnp.bfloat16

N_DEV = 8
CHUNK = 64
SUB = 16
HEAD = 128
SSM_P = 64
CONV_K = 4
F_MIN = 1e-30
NORM_EPS = 1e-6
LANES = 128
VMEM_LIMIT = 56 * 1024 * 1024

ADAM_LR = 0.001
ADAM_B1 = 0.9
ADAM_B2 = 0.999
ADAM_EPS = 1e-08
ADAM_WD = 0.01
ADAM_STEP = 10


def _dg(a, b, ca, cb):
    return lax.dot_general(a.astype(BF16), b.astype(BF16), (((ca,), (cb,)), ((), ())),
                           preferred_element_type=F32)


def _split3(x):
    x1 = x.astype(BF16)
    r = x - x1.astype(F32)
    x2 = r.astype(BF16)
    x3 = (r - x2.astype(F32)).astype(BF16)
    return x1, x2, x3


def _hdg(a, b, ca, cb):
    a1, a2, a3 = _split3(a)
    b1, b2, b3 = _split3(b)
    dn = (((ca,), (cb,)), ((), ()))
    d = lambda p, q: lax.dot_general(p, q, dn, preferred_element_type=F32)
    return ((d(a3, b1) + d(a1, b3)) + d(a2, b2)) + ((d(a2, b1) + d(a1, b2)) + d(a1, b1))


def _xdg(m, x, cm):
    mb = m.astype(BF16)
    dn = (((cm,), (0,)), ((), ()))
    x1, x2, x3 = _split3(x)
    d = lambda q: lax.dot_general(mb, q, dn, preferred_element_type=F32)
    return (d(x3) + d(x2)) + d(x1)


def _dot_family(prim):
    @jax.custom_vjp
    def nn(a, b):
        return prim(a, b, 1, 0)

    @jax.custom_vjp
    def nt(a, b):
        return prim(a, b, 1, 1)

    @jax.custom_vjp
    def tn(a, b):
        return prim(a, b, 0, 0)

    nn.defvjp(lambda a, b: (nn(a, b), (a, b)), lambda r, g: (nt(g, r[1]), tn(r[0], g)))
    nt.defvjp(lambda a, b: (nt(a, b), (a, b)), lambda r, g: (nn(g, r[1]), tn(g, r[0])))
    tn.defvjp(lambda a, b: (tn(a, b), (a, b)), lambda r, g: (nt(r[1], g), nn(r[0], g)))
    return nn, nt, tn


mm_nn, mm_nt, mm_tn = _dot_family(_dg)
hd_nn, hd_nt, hd_tn = _dot_family(_hdg)


@jax.custom_vjp
def xdot_n(m, x):
    return _xdg(m, x, 1)


@jax.custom_vjp
def xdot_t(m, x):
    return _xdg(m, x, 0)


xdot_n.defvjp(lambda m, x: (xdot_n(m, x), m), lambda m, g: (jnp.zeros_like(m), xdot_t(m, g)))
xdot_t.defvjp(lambda m, x: (xdot_t(m, x), m), lambda m, g: (jnp.zeros_like(m), xdot_n(m, g)))


def _iota(shape, dim):
    return lax.broadcasted_iota(jnp.int32, shape, dim)


def _sigmoid(x):
    return jax.nn.sigmoid(x)


def _silu(x):
    return x * jax.nn.sigmoid(x)


def _softplus(x):
    e = jnp.exp(-jnp.abs(x))
    small = e * (1.0 - e * (0.5 - e * (1.0 / 3.0)))
    return jnp.maximum(x, 0.0) + jnp.where(e < 1e-3, small, jnp.log(1.0 + e))


def _masked_exp(diff, mask):
    return jnp.where(mask, jnp.exp(jnp.where(mask, diff, 0.0)), 0.0)


def _rms(x, w):
    return x * lax.rsqrt(jnp.mean(x * x, axis=-1, keepdims=True) + NORM_EPS) * w


def _cum_col_row(lg_col, lg_row):
    c = lg_col.shape[0]
    r, s = _iota((c, c), 0), _iota((c, c), 1)
    cum_col = jnp.sum(jnp.where(s <= r, jnp.broadcast_to(lg_row, (c, c)), 0.0), axis=1, keepdims=True)
    cum_row = jnp.sum(jnp.where(r <= s, jnp.broadcast_to(lg_col, (c, c)), 0.0), axis=0, keepdims=True)
    total = jnp.sum(lg_col, axis=0, keepdims=True)
    return cum_col, cum_row, total


def hgrn_chunk(seq, hp, sp, st):
    (blk,), (lb,), (nw,) = seq, hp, sp
    c = blk.shape[0]
    q_raw, f_raw, v, g_raw = (blk[:, i * HEAD:(i + 1) * HEAD] for i in range(4))
    q = _silu(q_raw)
    f = lb + (1.0 - lb) * _sigmoid(f_raw)
    logf = jnp.log(jnp.maximum(f, F_MIN))
    k = (1.0 - lb) * _sigmoid(-f_raw)
    r, s = _iota((c, c), 0), _iota((c, c), 1)
    b = xdot_n((s <= r).astype(F32), logf)
    o_inter = mm_nt(q * jnp.exp(b), st)
    nsub = c // SUB
    er = _iota((SUB * SUB, SUB), 0)
    ec = _iota((SUB * SUB, SUB), 1)
    e_t = (er // SUB == ec).astype(F32)
    e_s = (er % SUB == ec).astype(F32)
    pr = _iota((SUB * SUB, 1), 0)
    pmask = (pr % SUB) <= (pr // SUB)
    er64 = _iota((SUB * SUB, c), 0)
    ec64 = _iota((SUB * SUB, c), 1)
    rows_c = _iota((c, 1), 0)
    parts = []
    for i in range(nsub):
        sl = slice(SUB * i, SUB * (i + 1))
        qi, ki, bi = q[sl], k[sl], b[sl]
        qb = xdot_n(e_t, qi)
        kb = xdot_n(e_s, ki)
        bd = xdot_n(e_t - e_s, bi)
        p = _masked_exp(bd, pmask)
        sc_col = jnp.sum(qb * kb * p, axis=1, keepdims=True)
        place = (ec64 == (er64 % SUB) + SUB * i).astype(F32)
        sc = xdot_t(e_t, sc_col * place)
        if i > 0:
            bref = xdot_n((s == SUB * i).astype(F32), b)
            valid = rows_c < SUB * i
            qt = qi * jnp.exp(bi - bref[sl])
            kt = k * _masked_exp(bref - b, valid)
            sc = sc + mm_nt(qt, kt)
        parts.append(mm_nn(sc, v))
    o = o_inter + jnp.concatenate(parts, axis=0)
    bend = xdot_n((s == c - 1).astype(F32), b)
    dk = st.shape[1]
    bend_st = xdot_n((_iota((st.shape[0], c), 1) == c - 1).astype(F32), b)
    st_new = st * jnp.exp(bend_st) + mm_tn(v, k * jnp.exp(bend - b))
    y = _rms(o, nw) * _silu(g_raw)
    del dk
    return (y,), st_new


def ssd_chunk(seq, hp, sp, st):
    xs, bm, cm, dtc, dtr = seq
    dt_bias, a_log = hp
    c = xs.shape[0]
    lane = _iota((1, 2 * SSM_P), 1)
    first = lane < SSM_P
    r, s = _iota((c, c), 0), _iota((c, c), 1)
    g = mm_nt(cm, bm)
    dts, cums, ends, segs = [], [], [], []
    for i in range(2):
        neg_a = -jnp.exp(a_log[i])
        dt_col = _softplus(dtc[i] + dt_bias[i])
        dt_row = _softplus(dtr[i] + dt_bias[i])
        cum_col, cum_row, total = _cum_col_row(neg_a * dt_col, neg_a * dt_row)
        dts.append(dt_col)
        cums.append(cum_col)
        ends.append(total)
        segs.append(_masked_exp(cum_col - cum_row, s <= r))
    dt_l = jnp.where(first, dts[0], dts[1])
    cum_l = jnp.where(first, cums[0], cums[1])
    end_l = jnp.where(first, ends[0], ends[1])
    xdt = xs * dt_l
    y_intra = (mm_nn(g * segs[0], jnp.where(first, xdt, 0.0))
               + mm_nn(g * segs[1], jnp.where(first, 0.0, xdt)))
    y_inter = mm_nn(cm, st) * jnp.exp(cum_l)
    st_new = st * jnp.exp(end_l) + mm_tn(bm, xdt * jnp.exp(end_l - cum_l))
    return (y_intra + y_inter,), st_new


def _neumann_inverse(a):
    n = a.shape[0]
    eye = (_iota((n, n), 0) == _iota((n, n), 1)).astype(F32)
    p = -a
    t = eye + p
    for _ in range(int(math.log2(n)) - 1):
        p = hd_nn(p, p)
        t = t + hd_nn(t, p)
    return t


def gdn_chunk(seq, hp, sp, st):
    q_raw, k_raw, v, z, gbc, gac, gar = seq
    dt_bias, a_log = hp
    (nw,) = sp
    c = v.shape[0]
    r, s = _iota((c, c), 0), _iota((c, c), 1)
    q = q_raw * lax.rsqrt(jnp.sum(q_raw * q_raw, axis=-1, keepdims=True) + NORM_EPS) * (HEAD ** -0.5)
    k = k_raw * lax.rsqrt(jnp.sum(k_raw * k_raw, axis=-1, keepdims=True) + NORM_EPS)
    beta = _sigmoid(gbc)
    neg_a = -jnp.exp(a_log)
    cum, cum_row, total = _cum_col_row(neg_a * _softplus(gac + dt_bias), neg_a * _softplus(gar + dt_bias))
    decay = _masked_exp(cum - cum_row, s <= r)
    kk = mm_nt(k, k)
    a_low = jnp.where(s < r, beta * kk * decay, 0.0)
    t = _neumann_inverse(a_low)
    u_base = hd_nn(t, v * beta)
    w_corr = hd_nn(t, k * (beta * jnp.exp(cum)))
    qk = mm_nt(q, k) * decay
    u = u_base - mm_nn(w_corr, st)
    o = mm_nn(q * jnp.exp(cum), st) + mm_nn(qk, u)
    st_new = jnp.exp(total) * st + mm_tn(k * jnp.exp(total - cum), u)
    y = _rms(o, nw) * _silu(z)
    return (y,), st_new


def normmod_fn(rows, params):
    (x,), (nw, sc, sh) = rows, params
    return (_rms(x, nw) * (1.0 + sc) + sh,)


def ssmpost_fn(rows, params):
    (y, xs, z), (d_exp, nw) = rows, params
    y = (y + d_exp * xs) * _silu(z)
    gw = y.shape[1] // 2
    return (jnp.concatenate([_rms(y[:, :gw], nw[:, :gw]), _rms(y[:, gw:], nw[:, gw:])], axis=1),)


def merge_fn(rows, params):
    (yh, ys, yg, gl), (bm, wb) = rows, params
    d = wb.shape[2]
    gates = _sigmoid(gl + bm)
    out = 0.0
    for n, y in enumerate((yh, ys, yg)):
        out = out + gates[:, n * d:(n + 1) * d] * mm_nn(y, wb[n])
    return (out,)


def outproj_fn(rows, params):
    (m, x), (g1, w) = rows, params
    return (x + (1.0 + g1) * mm_nn(m, w),)


def swiglu_fn(rows, params):
    (gu,), _ = rows, params
    hid = gu.shape[1] // 2
    return (_silu(gu[:, :hid]) * gu[:, hid:],)


def resid_fn(rows, params):
    (x, o), (g2,) = rows, params
    return (x + (1.0 + g2) * o,)


def loss_fn(rows, params):
    (x, tgt), (nw,) = rows, params
    err = _rms(x, nw) - tgt
    per_row = 0.5 * jnp.mean(err * err, axis=-1, keepdims=True)
    return (jnp.broadcast_to(jnp.sum(per_row, axis=0, keepdims=True), (8, LANES)),)


def _params(sem):
    return pltpu.CompilerParams(dimension_semantics=sem, vmem_limit_bytes=VMEM_LIMIT)


def _whole(a):
    nd = a.ndim
    return pl.BlockSpec(a.shape, lambda *_: (0,) * nd)


def _pick(n, cands):
    for c in cands:
        if n % c == 0:
            return c
    return n


def matmul(name, a, b, mode, out_dtype):
    if mode == "nn":
        (m, k), n = a.shape, b.shape[1]
    elif mode == "nt":
        (m, k), n = a.shape, b.shape[0]
    else:
        (k, m), n = a.shape, b.shape[1]
    tm = _pick(m, (512, 256, 128))
    tn = _pick(n, (1280, 1024, 1408, 768, 512, 384, 256, 128))
    tk = _pick(k, (1024, 1280, 1408, 768, 512, 256, 128))
    if mode == "tn":
        tm = _pick(m, (1024, 768, 512, 256, 128))
        tk = _pick(k, (512, 256, 128))
    nk = k // tk
    ca, cb = {"nn": (1, 0), "nt": (1, 1), "tn": (0, 0)}[mode]

    def body(a_ref, b_ref, o_ref, acc_ref):
        kk = pl.program_id(2)

        @pl.when(kk == 0)
        def _():
            acc_ref[...] = jnp.zeros_like(acc_ref)

        acc_ref[...] += _dg(a_ref[...], b_ref[...], ca, cb)

        @pl.when(kk == nk - 1)
        def _():
            o_ref[...] = acc_ref[...].astype(o_ref.dtype)

    a_spec = (pl.BlockSpec((tk, tm), lambda i, j, q: (q, i)) if mode == "tn"
              else pl.BlockSpec((tm, tk), lambda i, j, q: (i, q)))
    b_spec = (pl.BlockSpec((tn, tk), lambda i, j, q: (j, q)) if mode == "nt"
              else pl.BlockSpec((tk, tn), lambda i, j, q: (q, j)))
    return pl.pallas_call(
        body, name=name, grid=(m // tm, n // tn, nk),
        in_specs=[a_spec, b_spec],
        out_specs=pl.BlockSpec((tm, tn), lambda i, j, q: (i, j)),
        out_shape=jax.ShapeDtypeStruct((m, n), out_dtype),
        scratch_shapes=[pltpu.VMEM((tm, tn), F32)],
        compiler_params=_params(("parallel", "parallel", "arbitrary")),
    )(a, b)


def _row_specs(rows, tm):
    return [pl.BlockSpec((tm, w), lambda i, _c=c: (i, _c)) for (_, w, c) in rows]


def rowstage_fwd(name, fn, rows, params, outs, tm):
    s = rows[0][0].shape[0]
    nr, npar = len(rows), len(params)

    def body(*refs):
        r = [x[...].astype(F32) for x in refs[:nr]]
        p = [x[...].astype(F32) for x in refs[nr:nr + npar]]
        for ref, val in zip(refs[nr + npar:], fn(r, p)):
            ref[...] = val.astype(ref.dtype)

    res = pl.pallas_call(
        body, name=name, grid=(s // tm,),
        in_specs=_row_specs(rows, tm) + [_whole(p) for p in params],
        out_specs=[pl.BlockSpec((tm, w), lambda i: (i, 0)) for (w, _) in outs],
        out_shape=[jax.ShapeDtypeStruct((s, w), dt) for (w, dt) in outs],
        compiler_params=_params(("parallel",)),
    )(*[r[0] for r in rows], *params)
    return res


def rowstage_bwd(name, fn, rows, params, douts, drow_dtypes, tm, adds=None):
    s = rows[0][0].shape[0]
    nr, npar, no = len(rows), len(params), len(douts)
    adds = adds or {}
    add_idx = sorted(adds)
    na = len(add_idx)

    def body(*refs):
        r = [x[...].astype(F32) for x in refs[:nr]]
        p = [x[...].astype(F32) for x in refs[nr:nr + npar]]
        g = [x[...].astype(F32) for x in refs[nr + npar:nr + npar + no]]
        a_refs = refs[nr + npar + no:nr + npar + no + na]
        dr_refs = refs[nr + npar + no + na:nr + npar + no + na + nr]
        dp_refs = refs[nr + npar + no + na + nr:]
        _, vjp = jax.vjp(lambda r_, p_: tuple(fn(r_, p_)), r, p)
        dr, dp = vjp(tuple(g))
        for j, (ref, val) in enumerate(zip(dr_refs, dr)):
            if j in adds:
                val = val + a_refs[add_idx.index(j)][...].astype(F32)
            ref[...] = val.astype(ref.dtype)

        @pl.when(pl.program_id(0) == 0)
        def _():
            for ref in dp_refs:
                ref[...] = jnp.zeros_like(ref)

        for ref, val in zip(dp_refs, dp):
            ref[...] += val

    res = pl.pallas_call(
        body, name=name, grid=(s // tm,),
        in_specs=(_row_specs(rows, tm) + [_whole(p) for p in params]
                  + [pl.BlockSpec((tm, d.shape[1]), lambda i: (i, 0)) for d in douts]
                  + [pl.BlockSpec((tm, rows[j][1]), lambda i: (i, 0)) for j in add_idx]),
        out_specs=([pl.BlockSpec((tm, w), lambda i: (i, 0)) for (_, w, _) in rows] + [_whole(p) for p in params]),
        out_shape=([jax.ShapeDtypeStruct((s, w), dt) for (_, w, _), dt in zip(rows, drow_dtypes)]
                   + [jax.ShapeDtypeStruct(p.shape, F32) for p in params]),
        compiler_params=_params(("arbitrary",)),
    )(*[r[0] for r in rows], *params, *douts, *[adds[j] for j in add_idx])
    return res[:nr], res[nr:]


def _flip(index_map, nc):
    return lambda h, n: index_map(h, nc - 1 - n)


def scan_fwd(name, chunk_fn, nh, nc, seqs, hparams, sparams, state_shape, outs):
    ns, nhp, nsp, no = len(seqs), len(hparams), len(sparams), len(outs)

    def body(*refs):
        seq_r, hp_r, sp_r = refs[:ns], refs[ns:ns + nhp], refs[ns + nhp:ns + nhp + nsp]
        out_r = refs[ns + nhp + nsp:ns + nhp + nsp + no]
        st_out, st_scr = refs[-2], refs[-1]

        @pl.when(pl.program_id(1) == 0)
        def _():
            st_scr[...] = jnp.zeros_like(st_scr)

        st = st_scr[...]
        st_out[...] = st
        o, st_new = chunk_fn([x[...].astype(F32) for x in seq_r], [x[...] for x in hp_r], [x[...] for x in sp_r], st)
        for ref, val in zip(out_r, o):
            ref[...] = val.astype(ref.dtype)
        st_scr[...] = st_new

    nst = len(state_shape)
    res = pl.pallas_call(
        body, name=name, grid=(nh, nc),
        in_specs=([pl.BlockSpec(bs, im) for (_, bs, im) in seqs]
                  + [pl.BlockSpec(bs, lambda h, n, _im=im: _im(h)) for (_, bs, im) in hparams]
                  + [_whole(p) for p in sparams]),
        out_specs=([pl.BlockSpec(bs, im) for (_, _, bs, im) in outs]
                   + [pl.BlockSpec((None, None) + tuple(state_shape), lambda h, n: (h, n) + (0,) * nst)]),
        out_shape=([jax.ShapeDtypeStruct(fs, dt) for (fs, dt, _, _) in outs]
                   + [jax.ShapeDtypeStruct((nh, nc) + tuple(state_shape), F32)]),
        scratch_shapes=[pltpu.VMEM(tuple(state_shape), F32)],
        compiler_params=_params(("parallel", "arbitrary")),
    )(*[x[0] for x in seqs], *[x[0] for x in hparams], *sparams)
    return res[:no], res[no]


def scan_bwd(name, chunk_fn, nh, nc, seqs, hparams, sparams, state_shape, states, douts, dseqs):
    ns, nhp, nsp, no = len(seqs), len(hparams), len(sparams), len(douts)
    nst = len(state_shape)

    def body(*refs):
        seq_r, hp_r, sp_r = refs[:ns], refs[ns:ns + nhp], refs[ns + nhp:ns + nhp + nsp]
        base = ns + nhp + nsp
        st_r = refs[base]
        do_r = refs[base + 1:base + 1 + no]
        base += 1 + no
        ds_r, dhp_r, dsp_r = refs[base:base + ns], refs[base + ns:base + ns + nhp], refs[base + ns + nhp:base + ns + nhp + nsp]
        dst_scr = refs[-1]
        h, n = pl.program_id(0), pl.program_id(1)

        @pl.when(n == 0)
        def _():
            dst_scr[...] = jnp.zeros_like(dst_scr)
            for ref in dhp_r:
                ref[...] = jnp.zeros_like(ref)

        @pl.when((n == 0) & (h == 0))
        def _():
            for ref in dsp_r:
                ref[...] = jnp.zeros_like(ref)

        prim = ([x[...].astype(F32) for x in seq_r], [x[...] for x in hp_r], [x[...] for x in sp_r], st_r[...])
        _, vjp = jax.vjp(lambda a, b, c, d: chunk_fn(a, b, c, d), *prim)
        cot = (tuple(x[...].astype(F32) for x in do_r), dst_scr[...])
        ds, dhp, dsp, dst = vjp(cot)
        for ref, val in zip(ds_r, ds):
            ref[...] = val.astype(ref.dtype)
        for ref, val in zip(dhp_r, dhp):
            ref[...] += val
        for ref, val in zip(dsp_r, dsp):
            ref[...] += val
        dst_scr[...] = dst

    res = pl.pallas_call(
        body, name=name, grid=(nh, nc),
        in_specs=([pl.BlockSpec(bs, _flip(im, nc)) for (_, bs, im) in seqs]
                  + [pl.BlockSpec(bs, lambda h, n, _im=im: _im(h)) for (_, bs, im) in hparams]
                  + [_whole(p) for p in sparams]
                  + [pl.BlockSpec((None, None) + tuple(state_shape), lambda h, n: (h, nc - 1 - n) + (0,) * nst)]
                  + [pl.BlockSpec(bs, _flip(im, nc)) for (_, bs, im) in douts]),
        out_specs=([pl.BlockSpec(bs, _flip(im, nc)) for (_, _, bs, im) in dseqs]
                   + [pl.BlockSpec(bs, lambda h, n, _im=im: _im(h)) for (_, bs, im) in hparams]
                   + [_whole(p) for p in sparams]),
        out_shape=([jax.ShapeDtypeStruct(fs, dt) for (fs, dt, _, _) in dseqs]
                   + [jax.ShapeDtypeStruct(x[0].shape, F32) for x in hparams]
                   + [jax.ShapeDtypeStruct(p.shape, F32) for p in sparams]),
        scratch_shapes=[pltpu.VMEM(tuple(state_shape), F32)],
        compiler_params=_params(("arbitrary", "arbitrary")),
    )(*[x[0] for x in seqs], *[x[0] for x in hparams], *sparams, states, *[x[0] for x in douts])
    return res[:ns], res[ns:ns + nhp], res[ns + nhp:]


def _shift_down(x, n, rows):
    if n == 0:
        return x
    return jnp.where(rows >= n, pltpu.roll(x, n, 0), 0.0)


def _shift_up(x, n, rows):
    if n == 0:
        return x
    s = x.shape[0]
    return jnp.where(rows < s - n, pltpu.roll(x, s - n, 0), 0.0)


def conv_fwd(name, x, col0, w, b):
    s, cw = x.shape[0], w.shape[1]

    def body(x_ref, w_ref, b_ref, o_ref):
        xv = x_ref[...]
        rows = _iota(xv.shape, 0)
        u = jnp.broadcast_to(b_ref[...], xv.shape)
        for j in range(CONV_K):
            u = u + w_ref[j:j + 1, :] * _shift_down(xv, CONV_K - 1 - j, rows)
        o_ref[...] = _silu(u)

    return pl.pallas_call(
        body, name=name, grid=(cw // LANES,),
        in_specs=[pl.BlockSpec((s, LANES), lambda j: (0, col0 + j)),
                  pl.BlockSpec((CONV_K, LANES), lambda j: (0, j)),
                  pl.BlockSpec((1, LANES), lambda j: (0, j))],
        out_specs=pl.BlockSpec((s, LANES), lambda j: (0, j)),
        out_shape=jax.ShapeDtypeStruct((s, cw), F32),
        compiler_params=_params(("parallel",)),
    )(x, w, b)


def conv_bwd(name, x, col0, w, b, dout):
    s, cw = x.shape[0], w.shape[1]

    def body(x_ref, w_ref, b_ref, g_ref, dx_ref, dw_ref, db_ref):
        xv = x_ref[...]
        rows = _iota(xv.shape, 0)
        sh = [_shift_down(xv, CONV_K - 1 - j, rows) for j in range(CONV_K)]
        u = jnp.broadcast_to(b_ref[...], xv.shape)
        for j in range(CONV_K):
            u = u + w_ref[j:j + 1, :] * sh[j]
        sg = _sigmoid(u)
        du = g_ref[...] * (sg * (1.0 + u * (1.0 - sg)))
        dx = jnp.zeros_like(xv)
        for j in range(CONV_K):
            dx = dx + w_ref[j:j + 1, :] * _shift_up(du, CONV_K - 1 - j, rows)
            dw_ref[j:j + 1, :] = jnp.sum(du * sh[j], axis=0, keepdims=True)
        dx_ref[...] = dx.astype(dx_ref.dtype)
        db_ref[...] = jnp.sum(du, axis=0, keepdims=True)

    return pl.pallas_call(
        body, name=name, grid=(cw // LANES,),
        in_specs=[pl.BlockSpec((s, LANES), lambda j: (0, col0 + j)),
                  pl.BlockSpec((CONV_K, LANES), lambda j: (0, j)),
                  pl.BlockSpec((1, LANES), lambda j: (0, j)),
                  pl.BlockSpec((s, LANES), lambda j: (0, j))],
        out_specs=[pl.BlockSpec((s, LANES), lambda j: (0, j)),
                   pl.BlockSpec((CONV_K, LANES), lambda j: (0, j)),
                   pl.BlockSpec((1, LANES), lambda j: (0, j))],
        out_shape=[jax.ShapeDtypeStruct((s, cw), BF16), jax.ShapeDtypeStruct((CONV_K, cw), F32),
                   jax.ShapeDtypeStruct((1, cw), F32)],
        compiler_params=_params(("parallel",)),
    )(x, w, b, dout)


def exchange(name, send, broadcast):
    slab = send.shape if broadcast else send.shape[1:]

    def body(send_ref, recv_ref, send_sems, recv_sems, local_sem):
        x, y, c = lax.axis_index("x"), lax.axis_index("y"), lax.axis_index("c")
        me = 4 * x + 2 * y + c

        def src(peer):
            return send_ref if broadcast else send_ref.at[peer]

        local = pltpu.make_async_copy(src(me), recv_ref.at[me], local_sem)
        local.start()
        peers = []
        for k in range(1, N_DEV):
            px = 1 - x if (k >> 2) & 1 else x
            py = 1 - y if (k >> 1) & 1 else y
            pc = 1 - c if k & 1 else c
            peers.append(((px, py, pc), 4 * px + 2 * py + pc))
        sends = []
        for k, (dev, peer) in enumerate(peers):
            cp = pltpu.make_async_remote_copy(
                src_ref=src(peer), dst_ref=recv_ref.at[me], send_sem=send_sems.at[k], recv_sem=recv_sems.at[k],
                device_id=dev, device_id_type=pl.DeviceIdType.MESH)
            cp.start()
            sends.append(cp)
        for k, (dev, peer) in enumerate(peers):
            pltpu.make_async_remote_copy(
                src_ref=src(peer), dst_ref=recv_ref.at[peer], send_sem=send_sems.at[k], recv_sem=recv_sems.at[k],
                device_id=dev, device_id_type=pl.DeviceIdType.MESH).wait_recv()
        for cp in sends:
            cp.wait_send()
        local.wait()

    return pl.pallas_call(
        body, name=name,
        in_specs=[pl.BlockSpec(memory_space=pltpu.HBM)],
        out_specs=pl.BlockSpec(memory_space=pltpu.HBM),
        out_shape=jax.ShapeDtypeStruct((N_DEV,) + tuple(slab), send.dtype),
        scratch_shapes=[pltpu.SemaphoreType.DMA((N_DEV - 1,)), pltpu.SemaphoreType.DMA((N_DEV - 1,)),
                        pltpu.SemaphoreType.DMA],
        compiler_params=pltpu.CompilerParams(has_side_effects=True),
    )(send)


def adamw_sum(name, parts, w, m, v):
    rws = w.shape[0]
    nsum = parts.shape[0]
    tr = _pick(rws, (744, 512, 256, 128, 64, 32, 16, 8))
    c1 = 1.0 / (1.0 - ADAM_B1 ** ADAM_STEP)
    c2 = 1.0 / (1.0 - ADAM_B2 ** ADAM_STEP)

    def body(p_ref, w_ref, m_ref, v_ref, g_ref, d_ref, nm_ref, nv_ref):
        g = p_ref[0]
        for j in range(1, nsum):
            g = g + p_ref[j]
        nm = ADAM_B1 * m_ref[...] + (1.0 - ADAM_B1) * g
        nv = ADAM_B2 * v_ref[...] + (1.0 - ADAM_B2) * (g * g)
        g_ref[...] = g
        nm_ref[...] = nm
        nv_ref[...] = nv
        d_ref[...] = -ADAM_LR * ((nm * c1) / (jnp.sqrt(nv * c2) + ADAM_EPS) + ADAM_WD * w_ref[...])

    blk = pl.BlockSpec((tr, LANES), lambda i: (i, 0))
    return pl.pallas_call(
        body, name=name, grid=(rws // tr,),
        in_specs=[pl.BlockSpec((nsum, tr, LANES), lambda i: (0, i, 0)), blk, blk, blk],
        out_specs=[blk, blk, blk, blk],
        out_shape=[jax.ShapeDtypeStruct(w.shape, F32)] * 4,
        compiler_params=_params(("parallel",)),
    )(parts, w, m, v)


def ada_fwd(name, c_all, w, b):
    nl = w.shape[0]

    def body(c_ref, w_ref, b_ref, o_ref):
        ca = _silu(c_ref[...])
        for l in range(nl):
            o_ref[l] = mm_nn(ca, w_ref[l]) + b_ref[l]

    return pl.pallas_call(
        body, name=name,
        out_shape=jax.ShapeDtypeStruct((nl, c_all.shape[0], w.shape[2]), F32),
        compiler_params=pltpu.CompilerParams(vmem_limit_bytes=VMEM_LIMIT),
    )(c_all, w, b)


def ada_bwd(name, c_all, dmod):
    nl = dmod.shape[0]

    def body(c_ref, g_ref, o_ref):
        ca = _silu(c_ref[...])
        for l in range(nl):
            o_ref[l] = mm_tn(ca, g_ref[l])

    return pl.pallas_call(
        body, name=name,
        out_shape=jax.ShapeDtypeStruct((nl, c_all.shape[1], dmod.shape[2]), F32),
        compiler_params=pltpu.CompilerParams(vmem_limit_bytes=VMEM_LIMIT),
    )(c_all, dmod)


def lower_bounds_fn(rows, params):
    (lg,), _ = rows, params
    nl = lg.shape[0]
    mx = jnp.max(lg, axis=0, keepdims=True)
    e = jnp.exp(lg - mx)
    p = e / jnp.sum(e, axis=0, keepdims=True)
    layer = _iota((nl, 1), 0)
    acc = jnp.zeros_like(p)
    for j in range(1, nl):
        pj = jnp.sum(jnp.where(layer == j, p, 0.0), axis=0, keepdims=True)
        acc = acc + jnp.where(layer >= j, 1.0, 0.0) * pj
    return (acc,)


def loss_call(name, x, tgt, nw, tm):
    s, d = x.shape

    def body(x_ref, t_ref, w_ref, l_ref, dx_ref, dw_ref):
        def f(xv, wv):
            err = _rms(xv, wv) - t_ref[...]
            return jnp.sum(0.5 * jnp.mean(err * err, axis=-1, keepdims=True), axis=0, keepdims=True)

        val, vjp = jax.vjp(f, x_ref[...], w_ref[...])
        dx, dw = vjp(jnp.ones_like(val))

        @pl.when(pl.program_id(0) == 0)
        def _():
            l_ref[...] = jnp.zeros_like(l_ref)
            dw_ref[...] = jnp.zeros_like(dw_ref)

        l_ref[...] += jnp.broadcast_to(val, l_ref.shape)
        dw_ref[...] += dw
        dx_ref[...] = dx

    row = pl.BlockSpec((tm, d), lambda i: (i, 0))
    return pl.pallas_call(
        body, name=name, grid=(s // tm,),
        in_specs=[row, row, _whole(nw)],
        out_specs=[pl.BlockSpec((8, LANES), lambda i: (0, 0)), row, _whole(nw)],
        out_shape=[jax.ShapeDtypeStruct((8, LANES), F32), jax.ShapeDtypeStruct((s, d), F32),
                   jax.ShapeDtypeStruct(nw.shape, F32)],
        compiler_params=_params(("arbitrary",)),
    )(x, tgt, nw)


class Dims:
    def __init__(self, s, d, ffn):
        self.s, self.d, self.ffn = s, d, ffn
        self.mix = 3 * d // 4
        self.nh = self.mix // HEAD
        self.ssm_heads = self.mix // SSM_P
        self.pairs = self.mix // (2 * SSM_P)
        self.nc = s // CHUNK
        self.conv_ssm = self.mix + 4 * HEAD
        self.conv_w = self.conv_ssm + 3 * self.mix
        self.o_gates = 4 * self.mix
        self.o_sz = self.o_gates + 3 * d
        self.o_gz = self.o_sz + self.mix
        self.o_conv = self.o_gz + self.mix
        self.o_small = self.o_conv + self.conv_w
        used = self.o_small + LANES
        self.np = -(-used // 1280) * 1280
        self.tm = _pick(s, (256, 128, 64))
        mix, nh = self.mix, self.nh
        self.in_sizes = (mix, mix, mix, mix, mix, self.conv_ssm, self.ssm_heads, 3 * mix, mix, nh, nh, 3 * d)
        self.in_width = sum(self.in_sizes)


def permute_w_in(dm, w):
    off = np.cumsum((0,) + dm.in_sizes)
    g = [w[:, off[i]:off[i + 1]] for i in range(12)]
    hq, hf, hi, hg, sz, sxbc, sdt, gqkv, gz, gb, ga, gates = g
    hgrn = jnp.stack([t.reshape(dm.d, dm.nh, HEAD) for t in (hq, hf, hi, hg)], axis=2).reshape(dm.d, 4 * dm.mix)
    small = jnp.concatenate([sdt, gb, ga], axis=1)
    pad = jnp.zeros((dm.d, dm.np - dm.o_small - small.shape[1]), w.dtype)
    return jnp.concatenate([hgrn, gates, sz, gz, sxbc, gqkv, small, pad], axis=1)


def unpermute_w_in(dm, wp):
    mix, nh = dm.mix, dm.nh
    hgrn = wp[:, :4 * mix].reshape(dm.d, nh, 4, HEAD)
    hq, hf, hi, hg = (hgrn[:, :, i, :].reshape(dm.d, mix) for i in range(4))
    gates = wp[:, dm.o_gates:dm.o_sz]
    sz = wp[:, dm.o_sz:dm.o_gz]
    gz = wp[:, dm.o_gz:dm.o_conv]
    sxbc = wp[:, dm.o_conv:dm.o_conv + dm.conv_ssm]
    gqkv = wp[:, dm.o_conv + dm.conv_ssm:dm.o_small]
    sm = wp[:, dm.o_small:]
    sdt, gb, ga = sm[:, :dm.ssm_heads], sm[:, dm.ssm_heads:dm.ssm_heads + nh], sm[:, dm.ssm_heads + nh:dm.ssm_heads + 2 * nh]
    return jnp.concatenate([hq, hf, hi, hg, sz, sxbc, sdt, gqkv, gz, gb, ga, gates], axis=1)


def _small_views(dm, small):
    t = small.T
    col = lambda a: a[:, :, None]
    row = lambda a: a.reshape(a.shape[0], dm.nc, 1, CHUNK)
    a, b = dm.ssm_heads, dm.ssm_heads + dm.nh
    sdt, gb, ga = t[:a], t[a:b], t[b:b + dm.nh]
    return col(sdt), row(sdt), col(gb), col(ga), row(ga)


def _scan_specs(dm, proj, conv_out, views, lp):
    dt_col, dt_row, gb_col, ga_col, ga_row = views
    mixb, nh, ppg = dm.mix // LANES, dm.nh, dm.pairs // 2
    c128 = (CHUNK, LANES)
    hgrn = dict(
        nh=nh, fn=hgrn_chunk,
        seqs=[(proj, (CHUNK, 4 * HEAD), lambda h, n: (n, h))],
        hparams=[(lp["lb"], (1, HEAD), lambda h: (0, h))],
        sparams=[lp["hgrn_norm"]],
        dseqs=[((dm.s, 4 * dm.mix), BF16, (CHUNK, 4 * HEAD), lambda h, n: (n, h))])
    ssd = dict(
        nh=dm.pairs, fn=ssd_chunk,
        seqs=[(conv_out, c128, lambda j, n: (n, j)),
              (conv_out, c128, lambda j, n: (n, mixb + j // ppg)),
              (conv_out, c128, lambda j, n: (n, mixb + 2 + j // ppg)),
              (dt_col, (2, CHUNK, 1), lambda j, n: (j, n, 0)),
              (dt_row, (2, None, 1, CHUNK), lambda j, n: (j, n, 0, 0))],
        hparams=[(lp["ssm_dt_bias"], (2, 1, 1), lambda j: (j, 0, 0)), (lp["ssm_a_log"], (2, 1, 1), lambda j: (j, 0, 0))],
        sparams=[],
        dseqs=[((dm.s, dm.mix), F32, c128, lambda j, n: (n, j)),
               ((dm.s, dm.mix), F32, c128, lambda j, n: (n, j)),
               ((dm.s, dm.mix), F32, c128, lambda j, n: (n, j)),
               (dt_col.shape, F32, (2, CHUNK, 1), lambda j, n: (j, n, 0)),
               (dt_row.shape, F32, (2, None, 1, CHUNK), lambda j, n: (j, n, 0, 0))])
    cq = dm.conv_ssm // LANES
    cgz = dm.o_gz // LANES
    hcol = ((None, CHUNK, 1), lambda h, n: (h, n, 0))
    hrow = ((None, None, 1, CHUNK), lambda h, n: (h, n, 0, 0))
    hblk = (c128, lambda h, n: (n, h))
    gdn = dict(
        nh=nh, fn=gdn_chunk,
        seqs=[(conv_out, c128, lambda h, n: (n, cq + h)),
              (conv_out, c128, lambda h, n: (n, cq + nh + h)),
              (conv_out, c128, lambda h, n: (n, cq + 2 * nh + h)),
              (proj, c128, lambda h, n: (n, cgz + h)),
              (gb_col,) + hcol, (ga_col,) + hcol, (ga_row,) + hrow],
        hparams=[(lp["gdn_dt_bias"], (None, 1, 1), lambda h: (h, 0, 0)), (lp["gdn_a_log"], (None, 1, 1), lambda h: (h, 0, 0))],
        sparams=[lp["gdn_norm"]],
        dseqs=[((dm.s, dm.mix), F32) + hblk, ((dm.s, dm.mix), F32) + hblk, ((dm.s, dm.mix), F32) + hblk,
               ((dm.s, dm.mix), BF16) + hblk,
               (gb_col.shape, F32) + hcol, (ga_col.shape, F32) + hcol, (ga_row.shape, F32) + hrow])
    return hgrn, ssd, gdn


def _run_scan_fwd(dm, name, sp):
    out = ((dm.s, dm.mix), F32, (CHUNK, LANES), lambda h, n: (n, h))
    (y,), states = scan_fwd(name, sp["fn"], sp["nh"], dm.nc, sp["seqs"], sp["hparams"], sp["sparams"],
                            (HEAD, HEAD), [out])
    return y, states


def _run_scan_bwd(dm, name, sp, states, dy):
    dout = (dy, (CHUNK, LANES), lambda h, n: (n, h))
    return scan_bwd(name, sp["fn"], sp["nh"], dm.nc, sp["seqs"], sp["hparams"], sp["sparams"], (HEAD, HEAD),
                    states, [dout], sp["dseqs"])


def layer_fwd(dm, l, x, lp):
    tm, d, mix = dm.tm, dm.d, dm.mix
    tag = f"l{l}_"
    (h,) = rowstage_fwd(tag + "norm1", normmod_fn, [(x, d, 0)], [lp["norm_mix"], lp["sc1"], lp["sh1"]], [(d, BF16)], tm)
    proj = matmul(tag + "proj", h, lp["w_in"], "nn", F32)
    conv_out = conv_fwd(tag + "conv", proj, dm.o_conv // LANES, lp["conv_w"], lp["conv_b"])
    small = proj[:, dm.o_small:dm.o_small + LANES]
    views = _small_views(dm, small)
    hg, sd, gd = _scan_specs(dm, proj, conv_out, views, lp)
    yh, st_h = _run_scan_fwd(dm, tag + "hgrn", hg)
    y_ssd, st_s = _run_scan_fwd(dm, tag + "ssd", sd)
    yg, st_g = _run_scan_fwd(dm, tag + "gdn", gd)
    (ys,) = rowstage_fwd(tag + "ssmpost", ssmpost_fn,
                         [(y_ssd, mix, 0), (conv_out, mix, 0), (proj, mix, dm.o_sz // mix)],
                         [lp["ssm_d_exp"], lp["ssm_norm"]], [(mix, F32)], tm)
    (merged,) = rowstage_fwd(tag + "merge", merge_fn, [(yh, mix, 0), (ys, mix, 0), (yg, mix, 0), (proj, 3 * d, 1)],
                             [lp["b_merge"], lp["w_branch"]], [(d, BF16)], tm)
    (x1,) = rowstage_fwd(tag + "outproj", outproj_fn, [(merged, d, 0), (x, d, 0)], [lp["g1"], lp["w_out"]], [(d, F32)], tm)
    (h2,) = rowstage_fwd(tag + "norm2", normmod_fn, [(x1, d, 0)], [lp["norm_ffn"], lp["sc2"], lp["sh2"]], [(d, BF16)], tm)
    gu = matmul(tag + "ffn_in", h2, lp["w_ffn_in"], "nn", F32)
    (act,) = rowstage_fwd(tag + "swiglu", swiglu_fn, [(gu, 2 * dm.ffn, 0)], [], [(dm.ffn, BF16)], tm)
    o2 = matmul(tag + "ffn_out", act, lp["w_ffn_out"], "nn", F32)
    (x2,) = rowstage_fwd(tag + "resid", resid_fn, [(x1, d, 0), (o2, d, 0)], [lp["g2"]], [(d, F32)], tm)
    saved = dict(x=x, h=h, proj=proj, conv_out=conv_out, views=views, yh=yh, y_ssd=y_ssd, yg=yg, ys=ys,
                 st_h=st_h, st_s=st_s, st_g=st_g, merged=merged, x1=x1, h2=h2, gu=gu, act=act, o2=o2)
    return x2, saved


def layer_bwd(dm, l, dx2, lp, sv):
    tm, d, mix, s = dm.tm, dm.d, dm.mix, dm.s
    tag = f"l{l}_b_"
    g = {}
    (dx1_a, do2), (g["g2"],) = rowstage_bwd(tag + "resid", resid_fn, [(sv["x1"], d, 0), (sv["o2"], d, 0)], [lp["g2"]],
                                            [dx2], [F32, BF16], tm)
    dact = matmul(tag + "ffn_out_dx", do2, lp["w_ffn_out"], "nt", BF16)
    g["w_ffn_out"] = matmul(tag + "ffn_out_dw", sv["act"], do2, "tn", F32)
    (dgu,), _ = rowstage_bwd(tag + "swiglu", swiglu_fn, [(sv["gu"], 2 * dm.ffn, 0)], [], [dact], [BF16], tm)
    dh2 = matmul(tag + "ffn_in_dx", dgu, lp["w_ffn_in"], "nt", BF16)
    g["w_ffn_in"] = matmul(tag + "ffn_in_dw", sv["h2"], dgu, "tn", F32)
    (dx1,), (g["norm_ffn"], g["sc2"], g["sh2"]) = rowstage_bwd(
        tag + "norm2", normmod_fn, [(sv["x1"], d, 0)], [lp["norm_ffn"], lp["sc2"], lp["sh2"]], [dh2], [F32], tm,
        adds={0: dx1_a})
    (dmerged, dx_a), (g["g1"], g["w_out"]) = rowstage_bwd(
        tag + "outproj", outproj_fn, [(sv["merged"], d, 0), (sv["x"], d, 0)], [lp["g1"], lp["w_out"]], [dx1],
        [BF16, F32], tm)
    proj, conv_out = sv["proj"], sv["conv_out"]
    (dyh, dys, dyg, dgates), (g["b_merge"], g["w_branch"]) = rowstage_bwd(
        tag + "merge", merge_fn, [(sv["yh"], mix, 0), (sv["ys"], mix, 0), (sv["yg"], mix, 0), (proj, 3 * d, 1)],
        [lp["b_merge"], lp["w_branch"]], [dmerged], [F32, F32, F32, BF16], tm)
    (dy_ssd, dxs_a, dsz), (g["ssm_d_exp"], g["ssm_norm"]) = rowstage_bwd(
        tag + "ssmpost", ssmpost_fn, [(sv["y_ssd"], mix, 0), (conv_out, mix, 0), (proj, mix, dm.o_sz // mix)],
        [lp["ssm_d_exp"], lp["ssm_norm"]], [dys], [F32, F32, BF16], tm)
    hg, sd, gd = _scan_specs(dm, proj, conv_out, sv["views"], lp)
    (dhgrn,), (g["lb"],), (g["hgrn_norm"],) = _run_scan_bwd(dm, tag + "hgrn", hg, sv["st_h"], dyh)
    (dxs_b, dbp, dcp, d_dt_col, d_dt_row), (g["ssm_dt_bias"], g["ssm_a_log"]), _ = _run_scan_bwd(
        dm, tag + "ssd", sd, sv["st_s"], dy_ssd)
    (dq, dk, dv, dgz, d_gb_col, d_ga_col, d_ga_row), (g["gdn_dt_bias"], g["gdn_a_log"]), (g["gdn_norm"],) = _run_scan_bwd(
        dm, tag + "gdn", gd, sv["st_g"], dyg)
    group = lambda t: t.reshape(s, 2, dm.pairs // 2, LANES).sum(axis=2).reshape(s, 2 * LANES)
    dconv = jnp.concatenate([dxs_a + dxs_b, group(dbp), group(dcp), dq, dk, dv], axis=1)
    dpc, g["conv_w"], g["conv_b"] = conv_bwd(tag + "conv", proj, dm.o_conv // LANES, lp["conv_w"], lp["conv_b"], dconv)
    unrow = lambda t: t.reshape(t.shape[0], s).T
    dsmall = jnp.concatenate([d_dt_col[:, :, 0].T + unrow(d_dt_row), d_gb_col[:, :, 0].T,
                              d_ga_col[:, :, 0].T + unrow(d_ga_row)], axis=1)
    pad = jnp.zeros((s, dm.np - dm.o_small - dsmall.shape[1]), BF16)
    dproj = jnp.concatenate([dhgrn, dgates, dsz, dgz, dpc, dsmall.astype(BF16), pad], axis=1)
    dh = matmul(tag + "proj_dx", dproj, lp["w_in"], "nt", BF16)
    g["w_in"] = matmul(tag + "proj_dw", sv["h"], dproj, "tn", F32)
    (dx,), (g["norm_mix"], g["sc1"], g["sh1"]) = rowstage_bwd(
        tag + "norm1", normmod_fn, [(sv["x"], d, 0)], [lp["norm_mix"], lp["sc1"], lp["sh1"]], [dh], [F32], tm,
        adds={0: dx_a})
    return dx, g


WEIGHTS = ("w_ada", "b_ada", "norm_mix", "norm_ffn", "w_in", "b_merge", "hgrn_lb_logits", "hgrn_norm", "ssm_conv_w",
           "ssm_conv_b", "ssm_dt_bias", "ssm_a_log", "ssm_d", "ssm_norm", "gdn_conv_w", "gdn_dt_bias", "gdn_a_log",
           "gdn_norm", "w_branch", "w_out", "w_ffn_in", "w_ffn_out", "norm_final")
GATHERED = ("w_in", "w_branch", "w_out", "w_ffn_in", "w_ffn_out")
PACKET = ("b_ada", "norm_mix", "norm_ffn", "b_merge", "hgrn_norm", "ssm_conv_b", "ssm_dt_bias", "ssm_a_log", "ssm_d",
          "ssm_norm", "gdn_dt_bias", "gdn_a_log", "gdn_norm", "norm_final")
MISC = ("hgrn_lb_logits", "ssm_conv_w", "gdn_conv_w", "w_ada")


def _pack(arrs, dtype, row_mult, lead=0):
    flat = jnp.concatenate([t.reshape(t.shape[:lead] + (-1,)).astype(dtype) for t in arrs], axis=lead)
    n = flat.shape[-1]
    unit = row_mult * LANES
    tot = -(-n // unit) * unit
    flat = jnp.pad(flat, [(0, 0)] * lead + [(0, tot - n)])
    return flat.reshape(flat.shape[:lead] + (tot // LANES, LANES))


def _unpack(packed, shapes, lead=0):
    flat = packed.reshape(packed.shape[:lead] + (-1,))
    out, off = [], 0
    for shp in shapes:
        n = int(np.prod(shp))
        out.append(flat[..., off:off + n].reshape(flat.shape[:lead] + tuple(shp)))
        off += n
    return out


def _join(name, g):
    if name in ("w_in", "w_ffn_in"):
        return g.transpose(1, 0, 2).reshape(g.shape[1], -1)
    if name == "w_branch":
        return g.transpose(1, 2, 0, 3).reshape(g.shape[1], g.shape[2], -1)
    return g.reshape((-1,) + g.shape[2:])


def _split(name, w):
    if name in ("w_in", "w_ffn_in"):
        return w.reshape(w.shape[0], N_DEV, -1).transpose(1, 0, 2)
    if name == "w_branch":
        return w.reshape(w.shape[0], w.shape[1], N_DEV, -1).transpose(2, 0, 1, 3)
    return w.reshape((N_DEV, -1) + w.shape[1:])


def layer_params(dm, l, full, small, mod_l, lb_l):
    d, mix = dm.d, dm.mix
    row = lambda t: t.reshape(1, -1)
    head = lambda t: t.reshape(-1, 1, 1)
    sh1, sc1, g1, sh2, sc2, g2 = (row(mod_l[i * d:(i + 1) * d]) for i in range(6))
    conv_b = jnp.concatenate([small["ssm_conv_b"][l], jnp.zeros((3 * mix,), F32)])
    return dict(
        w_in=permute_w_in(dm, full["w_in"]), w_branch=full["w_branch"], w_out=full["w_out"],
        w_ffn_in=full["w_ffn_in"], w_ffn_out=full["w_ffn_out"],
        norm_mix=row(small["norm_mix"][l]), norm_ffn=row(small["norm_ffn"][l]), b_merge=row(small["b_merge"][l]),
        hgrn_norm=row(small["hgrn_norm"][l]), lb=row(lb_l),
        conv_w=jnp.concatenate([small["ssm_conv_w"][l], small["gdn_conv_w"][l]], axis=1), conv_b=row(conv_b),
        ssm_dt_bias=head(small["ssm_dt_bias"][l]), ssm_a_log=head(small["ssm_a_log"][l]),
        ssm_d_exp=row(jnp.repeat(small["ssm_d"][l], SSM_P)), ssm_norm=row(small["ssm_norm"][l]),
        gdn_dt_bias=head(small["gdn_dt_bias"][l]), gdn_a_log=head(small["gdn_a_log"][l]), gdn_norm=row(small["gdn_norm"][l]),
        sh1=sh1, sc1=sc1, g1=g1, sh2=sh2, sc2=sc2, g2=g2)


def layer_grads(dm, g):
    cs = dm.conv_ssm
    out = dict(
        w_in=unpermute_w_in(dm, g["w_in"]), w_branch=g["w_branch"], w_out=g["w_out"], w_ffn_in=g["w_ffn_in"],
        w_ffn_out=g["w_ffn_out"], norm_mix=g["norm_mix"][0], norm_ffn=g["norm_ffn"][0], b_merge=g["b_merge"][0],
        hgrn_norm=g["hgrn_norm"][0], ssm_conv_w=g["conv_w"][:, :cs], gdn_conv_w=g["conv_w"][:, cs:],
        ssm_conv_b=g["conv_b"][0, :cs], ssm_dt_bias=g["ssm_dt_bias"][:, 0, 0], ssm_a_log=g["ssm_a_log"][:, 0, 0],
        ssm_d=g["ssm_d_exp"].reshape(dm.ssm_heads, SSM_P).sum(axis=1), ssm_norm=g["ssm_norm"][0],
        gdn_dt_bias=g["gdn_dt_bias"][:, 0, 0], gdn_a_log=g["gdn_a_log"][:, 0, 0], gdn_norm=g["gdn_norm"][0])
    dmod = jnp.concatenate([g[k][0] for k in ("sh1", "sc1", "g1", "sh2", "sc2", "g2")])
    return out, dmod, g["lb"][0]


def local_step(dm, x, tgt, lps, norm_final):
    saved = []
    for l, lp in enumerate(lps):
        x, sv = layer_fwd(dm, l, x, lp)
        saved.append(sv)
    loss, dx, dnf = loss_call("loss", x, tgt, norm_final, dm.tm)
    grads = [None] * len(lps)
    for l in reversed(range(len(lps))):
        dx, grads[l] = layer_bwd(dm, l, dx, lps[l], saved[l])
    return loss, dx, dnf, grads


def kernel(x, c, w_ada, b_ada, norm_mix, norm_ffn, w_in, b_merge, hgrn_lb_logits, hgrn_norm, ssm_conv_w, ssm_conv_b, ssm_dt_bias, ssm_a_log, ssm_d, ssm_norm, gdn_conv_w, gdn_dt_bias, gdn_a_log, gdn_norm, w_branch, w_out, w_ffn_in, w_ffn_out, norm_final, loss_target, m_w_ada, m_b_ada, m_norm_mix, m_norm_ffn, m_w_in, m_b_merge, m_hgrn_lb_logits, m_hgrn_norm, m_ssm_conv_w, m_ssm_conv_b, m_ssm_dt_bias, m_ssm_a_log, m_ssm_d, m_ssm_norm, m_gdn_conv_w, m_gdn_dt_bias, m_gdn_a_log, m_gdn_norm, m_w_branch, m_w_out, m_w_ffn_in, m_w_ffn_out, m_norm_final, v_w_ada, v_b_ada, v_norm_mix, v_norm_ffn, v_w_in, v_b_merge, v_hgrn_lb_logits, v_hgrn_norm, v_ssm_conv_w, v_ssm_conv_b, v_ssm_dt_bias, v_ssm_a_log, v_ssm_d, v_ssm_norm, v_gdn_conv_w, v_gdn_dt_bias, v_gdn_a_log, v_gdn_norm, v_w_branch, v_w_out, v_w_ffn_in, v_w_ffn_out, v_norm_final):
    a = dict(locals())
    x, tgt = a["x"][0], a["loss_target"][0]
    s, d = x.shape
    nl = a["w_ada"].shape[0]
    dm = Dims(s, d, a["w_ffn_out"].shape[1] * N_DEV)
    me = 4 * lax.axis_index("x") + 2 * lax.axis_index("y") + lax.axis_index("c")

    first = [a["c"], a["ssm_conv_w"], a["gdn_conv_w"]]
    c_all, scw, gcw = _unpack(exchange("gather_c", _pack(first, F32, 8), True), [t.shape for t in first], lead=1)
    small = dict(a, ssm_conv_w=scw.transpose(1, 2, 0, 3).reshape(scw.shape[1:3] + (-1,)),
                 gdn_conv_w=gcw.transpose(1, 2, 0, 3).reshape(gcw.shape[1:3] + (-1,)))
    c_pad = jnp.zeros((LANES, d), F32).at[:N_DEV].set(c_all.reshape(N_DEV, d))
    ncol = a["w_ada"].shape[2]
    b_mine = lax.dynamic_slice(a["b_ada"], (0, me * ncol), (nl, ncol))[:, None, :]
    mod_part = ada_fwd("ada_fwd", c_pad, a["w_ada"], b_mine)[:, :N_DEV, :]
    mod = exchange("a2a_mod", _pack([mod_part.transpose(1, 0, 2)], F32, 8, lead=1), False)
    mod = _unpack(mod, [(nl, ncol)], lead=1)[0].transpose(1, 0, 2).reshape(nl, N_DEV * ncol)
    (lb,) = rowstage_fwd("lower_bounds", lower_bounds_fn, [(a["hgrn_lb_logits"], dm.mix, 0)], [], [(dm.mix, F32)], nl)

    shard_shapes = [a[n].shape[1:] for n in GATHERED]
    lps = []
    for l in range(nl):
        got = exchange(f"gather_w{l}", _pack([a[n][l] for n in GATHERED], BF16, 16), True)
        full = {n: _join(n, t) for n, t in zip(GATHERED, _unpack(got, shard_shapes, lead=1))}
        lps.append(layer_params(dm, l, full, small, mod[l], lb[l]))

    loss, dx, dnf, grads = local_step(dm, x, tgt, lps, a["norm_final"].reshape(1, d))

    per_layer = [layer_grads(dm, g) for g in grads]
    res = {}
    for l in range(nl):
        lg = per_layer[l][0]
        parts = exchange(f"scatter_g{l}", _pack([_split(n, lg[n]) for n in GATHERED], F32, 8, lead=1), False)
        outs = adamw_sum(f"adamw_l{l}", parts, *[_pack([a[p + n][l] for n in GATHERED], F32, 8) for p in ("", "m_", "v_")])
        for kind, o in zip(("grad", "delta", "new_m", "new_v"), outs):
            for n, t in zip(GATHERED, _unpack(o, shard_shapes)):
                res.setdefault((kind, n), []).append(t)
    for key in list(res):
        res[key] = jnp.stack(res[key])

    stackg = lambda n: jnp.stack([pl_[0][n] for pl_ in per_layer])
    dmod = jnp.stack([pl_[1] for pl_ in per_layer])
    dlb = jnp.stack([pl_[2] for pl_ in per_layer])
    pk_g = [dmod if n == "b_ada" else dnf if n == "norm_final" else stackg(n) for n in PACKET]
    extra = [dlb, stackg("ssm_conv_w"), stackg("gdn_conv_w"), loss[0, :1]]
    pk_shapes = [t.shape for t in pk_g + extra]
    zeros = [jnp.zeros(t.shape, F32) for t in extra]
    parts = exchange("gather_small", _pack(pk_g + extra, F32, 8), True)
    outs = adamw_sum("adamw_small", parts, *[_pack([a[p + n] for n in PACKET] + zeros, F32, 8) for p in ("", "m_", "v_")])
    for kind, o in zip(("grad", "delta", "new_m", "new_v"), outs):
        un = _unpack(o, pk_shapes)
        for n, t in zip(PACKET, un):
            res[(kind, n)] = t.reshape(a[n].shape)
        if kind == "grad":
            dlb_sum, g_scw, g_gcw, loss_sum = un[len(PACKET):]

    (g_lb,), _ = rowstage_bwd("lower_bounds_b", lower_bounds_fn, [(a["hgrn_lb_logits"], dm.mix, 0)], [], [dlb_sum], [F32], nl)
    mine = lambda t, n: lax.dynamic_slice_in_dim(t, me * a[n].shape[-1], a[n].shape[-1], axis=t.ndim - 1)
    dmod_cols = exchange("a2a_dmod", _pack([_split("w_in", dmod)], F32, 8, lead=1), False)
    dmod_cols = _unpack(dmod_cols, [(nl, ncol)], lead=1)[0].transpose(1, 0, 2)
    dmod_pad = jnp.zeros((nl, LANES, ncol), F32).at[:, :N_DEV].set(dmod_cols)
    g_misc = [g_lb, mine(g_scw, "ssm_conv_w"), mine(g_gcw, "gdn_conv_w"), ada_bwd("ada_bwd", c_pad, dmod_pad)]
    outs = adamw_sum("adamw_misc", _pack(g_misc, F32, 8)[None], *[_pack([a[p + n] for n in MISC], F32, 8) for p in ("", "m_", "v_")])
    for kind, o in zip(("grad", "delta", "new_m", "new_v"), outs):
        for n, t in zip(MISC, _unpack(o, [a[n].shape for n in MISC])):
            res[(kind, n)] = t

    out = [loss_sum.reshape(()), dx[None]]
    for kind in ("grad", "delta", "new_m", "new_v"):
        out += [res[(kind, n)] for n in WEIGHTS]
    return tuple(out)
```

```python
import functools
import math

import numpy as np
import jax
import jax.numpy as jnp
from jax import lax
from jax.experimental import pallas as pl
from jax.experimental.pallas import tpu as pltpu

F32 = jnp.float32
BF16 = jnp.bfloat16

N_DEV = 8
CHUNK = 64
SUB = 16
HEADS_PER_STEP = 2
HEAD = 128
SSM_P = 64
CONV_K = 4
F_MIN = 1e-30
NORM_EPS = 1e-6
LANES = 128
VMEM_LIMIT = 56 * 1024 * 1024

ADAM_LR = 0.001
ADAM_B1 = 0.9
ADAM_B2 = 0.999
ADAM_EPS = 1e-08
ADAM_WD = 0.01
ADAM_STEP = 10


def _dg(a, b, ca, cb):
    return lax.dot_general(a.astype(BF16), b.astype(BF16), (((ca,), (cb,)), ((), ())),
                           preferred_element_type=F32)


def _split3(x):
    x1 = x.astype(BF16)
    r = x - x1.astype(F32)
    x2 = r.astype(BF16)
    x3 = (r - x2.astype(F32)).astype(BF16)
    return x1, x2, x3


def _hdg(a, b, ca, cb):
    a1, a2, a3 = _split3(a)
    b1, b2, b3 = _split3(b)
    dn = (((ca,), (cb,)), ((), ()))
    d = lambda p, q: lax.dot_general(p, q, dn, preferred_element_type=F32)
    return ((d(a3, b1) + d(a1, b3)) + d(a2, b2)) + ((d(a2, b1) + d(a1, b2)) + d(a1, b1))


def _xdg(m, x, cm):
    mb = m.astype(BF16)
    dn = (((cm,), (0,)), ((), ()))
    x1, x2, x3 = _split3(x)
    d = lambda q: lax.dot_general(mb, q, dn, preferred_element_type=F32)
    return (d(x3) + d(x2)) + d(x1)


def _dot_family(prim):
    @jax.custom_vjp
    def nn(a, b):
        return prim(a, b, 1, 0)

    @jax.custom_vjp
    def nt(a, b):
        return prim(a, b, 1, 1)

    @jax.custom_vjp
    def tn(a, b):
        return prim(a, b, 0, 0)

    nn.defvjp(lambda a, b: (nn(a, b), (a, b)), lambda r, g: (nt(g, r[1]), tn(r[0], g)))
    nt.defvjp(lambda a, b: (nt(a, b), (a, b)), lambda r, g: (nn(g, r[1]), tn(g, r[0])))
    tn.defvjp(lambda a, b: (tn(a, b), (a, b)), lambda r, g: (nt(r[1], g), nn(r[0], g)))
    return nn, nt, tn


mm_nn, mm_nt, mm_tn = _dot_family(_dg)
hd_nn, hd_nt, hd_tn = _dot_family(_hdg)


@jax.custom_vjp
def xdot_n(m, x):
    return _xdg(m, x, 1)


@jax.custom_vjp
def xdot_t(m, x):
    return _xdg(m, x, 0)


xdot_n.defvjp(lambda m, x: (xdot_n(m, x), m), lambda m, g: (jnp.zeros_like(m), xdot_t(m, g)))
xdot_t.defvjp(lambda m, x: (xdot_t(m, x), m), lambda m, g: (jnp.zeros_like(m), xdot_n(m, g)))


def _iota(shape, dim):
    return lax.broadcasted_iota(jnp.int32, shape, dim)


def _sigmoid(x):
    return jax.nn.sigmoid(x)


def _silu(x):
    return x * jax.nn.sigmoid(x)


def _softplus(x):
    e = jnp.exp(-jnp.abs(x))
    small = e * (1.0 - e * (0.5 - e * (1.0 / 3.0)))
    return jnp.maximum(x, 0.0) + jnp.where(e < 1e-3, small, jnp.log(1.0 + e))


def _masked_exp(diff, mask):
    return jnp.where(mask, jnp.exp(jnp.where(mask, diff, 0.0)), 0.0)


def _rms(x, w):
    return x * lax.rsqrt(jnp.mean(x * x, axis=-1, keepdims=True) + NORM_EPS) * w


def _cum_col_row(lg_col, lg_row):
    c = lg_col.shape[0]
    r, s = _iota((c, c), 0), _iota((c, c), 1)
    cum_col = jnp.sum(jnp.where(s <= r, jnp.broadcast_to(lg_row, (c, c)), 0.0), axis=1, keepdims=True)
    cum_row = jnp.sum(jnp.where(r <= s, jnp.broadcast_to(lg_col, (c, c)), 0.0), axis=0, keepdims=True)
    total = jnp.sum(lg_col, axis=0, keepdims=True)
    return cum_col, cum_row, total


def hgrn_chunk(seq, hp, sp, st):
    (blk,), (lb,), (nw,) = seq, hp, sp
    c = blk.shape[0]
    q_raw, f_raw, v, g_raw = (blk[:, i * HEAD:(i + 1) * HEAD] for i in range(4))
    q = _silu(q_raw)
    f = lb + (1.0 - lb) * _sigmoid(f_raw)
    logf = jnp.log(jnp.maximum(f, F_MIN))
    k = (1.0 - lb) * _sigmoid(-f_raw)
    r, s = _iota((c, c), 0), _iota((c, c), 1)
    b = xdot_n((s <= r).astype(F32), logf)
    o_inter = mm_nt(q * jnp.exp(b), st)
    nsub = c // SUB
    er = _iota((SUB * SUB, SUB), 0)
    ec = _iota((SUB * SUB, SUB), 1)
    e_t = (er // SUB == ec).astype(F32)
    e_s = (er % SUB == ec).astype(F32)
    pr = _iota((SUB * SUB, 1), 0)
    pmask = (pr % SUB) <= (pr // SUB)
    er64 = _iota((SUB * SUB, c), 0)
    ec64 = _iota((SUB * SUB, c), 1)
    rows_c = _iota((c, 1), 0)
    parts = []
    for i in range(nsub):
        sl = slice(SUB * i, SUB * (i + 1))
        qi, ki, bi = q[sl], k[sl], b[sl]
        qb = xdot_n(e_t, qi)
        kb = xdot_n(e_s, ki)
        bd = xdot_n(e_t - e_s, bi)
        p = _masked_exp(bd, pmask)
        sc_col = jnp.sum(qb * kb * p, axis=1, keepdims=True)
        place = (ec64 == (er64 % SUB) + SUB * i).astype(F32)
        sc = xdot_t(e_t, sc_col * place)
        if i > 0:
            bref = xdot_n((s == SUB * i).astype(F32), b)
            valid = rows_c < SUB * i
            qt = qi * jnp.exp(bi - bref[sl])
            kt = k * _masked_exp(bref - b, valid)
            sc = sc + mm_nt(qt, kt)
        parts.append(mm_nn(sc, v))
    o = o_inter + jnp.concatenate(parts, axis=0)
    bend = xdot_n((s == c - 1).astype(F32), b)
    dk = st.shape[1]
    bend_st = xdot_n((_iota((st.shape[0], c), 1) == c - 1).astype(F32), b)
    st_new = st * jnp.exp(bend_st) + mm_tn(v, k * jnp.exp(bend - b))
    y = _rms(o, nw) * _silu(g_raw)
    del dk
    return (y,), st_new


def ssd_chunk(seq, hp, sp, st):
    xs, bm, cm, dtc, dtr = seq
    dt_bias, a_log = hp
    c = xs.shape[0]
    lane = _iota((1, 2 * SSM_P), 1)
    first = lane < SSM_P
    r, s = _iota((c, c), 0), _iota((c, c), 1)
    g = mm_nt(cm, bm)
    dts, cums, ends, segs = [], [], [], []
    for i in range(2):
        neg_a = -jnp.exp(a_log[i])
        dt_col = _softplus(dtc[i] + dt_bias[i])
        dt_row = _softplus(dtr[i] + dt_bias[i])
        cum_col, cum_row, total = _cum_col_row(neg_a * dt_col, neg_a * dt_row)
        dts.append(dt_col)
        cums.append(cum_col)
        ends.append(total)
        segs.append(_masked_exp(cum_col - cum_row, s <= r))
    dt_l = jnp.where(first, dts[0], dts[1])
    cum_l = jnp.where(first, cums[0], cums[1])
    end_l = jnp.where(first, ends[0], ends[1])
    xdt = xs * dt_l
    y_intra = (mm_nn(g * segs[0], jnp.where(first, xdt, 0.0))
               + mm_nn(g * segs[1], jnp.where(first, 0.0, xdt)))
    y_inter = mm_nn(cm, st) * jnp.exp(cum_l)
    st_new = st * jnp.exp(end_l) + mm_tn(bm, xdt * jnp.exp(end_l - cum_l))
    return (y_intra + y_inter,), st_new


def _neumann_inverse(a):
    n = a.shape[0]
    eye = (_iota((n, n), 0) == _iota((n, n), 1)).astype(F32)
    p = -a
    t = eye + p
    for _ in range(int(math.log2(n)) - 1):
        p = hd_nn(p, p)
        t = t + hd_nn(t, p)
    return t


def gdn_chunk(seq, hp, sp, st):
    q_raw, k_raw, v, z, gbc, gac, gar = seq
    dt_bias, a_log = hp
    (nw,) = sp
    c = v.shape[0]
    r, s = _iota((c, c), 0), _iota((c, c), 1)
    q = q_raw * lax.rsqrt(jnp.sum(q_raw * q_raw, axis=-1, keepdims=True) + NORM_EPS) * (HEAD ** -0.5)
    k = k_raw * lax.rsqrt(jnp.sum(k_raw * k_raw, axis=-1, keepdims=True) + NORM_EPS)
    beta = _sigmoid(gbc)
    neg_a = -jnp.exp(a_log)
    cum, cum_row, total = _cum_col_row(neg_a * _softplus(gac + dt_bias), neg_a * _softplus(gar + dt_bias))
    decay = _masked_exp(cum - cum_row, s <= r)
    kk = mm_nt(k, k)
    a_low = jnp.where(s < r, beta * kk * decay, 0.0)
    t = _neumann_inverse(a_low)
    u_base = hd_nn(t, v * beta)
    w_corr = hd_nn(t, k * (beta * jnp.exp(cum)))
    qk = mm_nt(q, k) * decay
    u = u_base - mm_nn(w_corr, st)
    o = mm_nn(q * jnp.exp(cum), st) + mm_nn(qk, u)
    st_new = jnp.exp(total) * st + mm_tn(k * jnp.exp(total - cum), u)
    y = _rms(o, nw) * _silu(z)
    return (y,), st_new


def normmod_fn(rows, params):
    (x,), (nw, sc, sh) = rows, params
    return (_rms(x, nw) * (1.0 + sc) + sh,)


def ssmpost_fn(rows, params):
    (y, xs, z), (d_exp, nw) = rows, params
    y = (y + d_exp * xs) * _silu(z)
    gw = y.shape[1] // 2
    return (jnp.concatenate([_rms(y[:, :gw], nw[:, :gw]), _rms(y[:, gw:], nw[:, gw:])], axis=1),)


def merge_fn(rows, params):
    (yh, ys, yg, gl), (bm, wb) = rows, params
    d = wb.shape[2]
    gates = _sigmoid(gl + bm)
    out = 0.0
    for n, y in enumerate((yh, ys, yg)):
        out = out + gates[:, n * d:(n + 1) * d] * mm_nn(y, wb[n])
    return (out,)


def outproj_fn(rows, params):
    (m, x), (g1, w) = rows, params
    return (x + (1.0 + g1) * mm_nn(m, w),)


def resid_fn(rows, params):
    (x, o), (g2,) = rows, params
    return (x + (1.0 + g2) * o,)


def _params(sem):
    return pltpu.CompilerParams(dimension_semantics=sem, vmem_limit_bytes=VMEM_LIMIT)


def _whole(a):
    nd = a.ndim
    return pl.BlockSpec(a.shape, lambda *_: (0,) * nd)


def _pick(n, cands):
    for c in cands:
        if n % c == 0:
            return c
    return n


def matmul(name, a, b, mode, out_dtype):
    if mode == "nn":
        (m, k), n = a.shape, b.shape[1]
    elif mode == "nt":
        (m, k), n = a.shape, b.shape[0]
    else:
        (k, m), n = a.shape, b.shape[1]
    tm = _pick(m, (512, 256, 128))
    tn = _pick(n, (1280, 1024, 1408, 768, 512, 384, 256, 128))
    tk = _pick(k, (1024, 1280, 1408, 768, 512, 256, 128))
    if mode == "tn":
        tm = _pick(m, (1024, 768, 512, 256, 128))
        tk = _pick(k, (512, 256, 128))
    nk = k // tk
    ca, cb = {"nn": (1, 0), "nt": (1, 1), "tn": (0, 0)}[mode]

    def body(a_ref, b_ref, o_ref, acc_ref):
        kk = pl.program_id(2)

        @pl.when(kk == 0)
        def _():
            acc_ref[...] = jnp.zeros_like(acc_ref)

        acc_ref[...] += _dg(a_ref[...], b_ref[...], ca, cb)

        @pl.when(kk == nk - 1)
        def _():
            o_ref[...] = acc_ref[...].astype(o_ref.dtype)

    a_spec = (pl.BlockSpec((tk, tm), lambda i, j, q: (q, i)) if mode == "tn"
              else pl.BlockSpec((tm, tk), lambda i, j, q: (i, q)))
    b_spec = (pl.BlockSpec((tn, tk), lambda i, j, q: (j, q)) if mode == "nt"
              else pl.BlockSpec((tk, tn), lambda i, j, q: (q, j)))
    return pl.pallas_call(
        body, name=name, grid=(m // tm, n // tn, nk),
        in_specs=[a_spec, b_spec],
        out_specs=pl.BlockSpec((tm, tn), lambda i, j, q: (i, j)),
        out_shape=jax.ShapeDtypeStruct((m, n), out_dtype),
        scratch_shapes=[pltpu.VMEM((tm, tn), F32)],
        compiler_params=_params(("parallel", "parallel", "arbitrary")),
    )(a, b)


def bmatmul(name, a, b, mode, out_dtype, out_batched):
    ab, bb = a.ndim == 3, b.ndim == 3
    nb = a.shape[0] if ab else b.shape[0]
    a2, b2 = a.shape[-2:], b.shape[-2:]
    if mode == "nn":
        (m, k), n = a2, b2[1]
    elif mode == "nt":
        (m, k), n = a2, b2[0]
    else:
        (k, m), n = a2, b2[1]
    tm = _pick(m, (1024, 512, 256, 128) if mode == "tn" else (512, 256, 128))
    tn = _pick(n, (1024, 512, 256, 128))
    tk = _pick(k, (512, 256, 128) if mode == "tn" else (1024, 512, 256, 128))
    nk = k // tk
    ca, cb = {"nn": (1, 0), "nt": (1, 1), "tn": (0, 0)}[mode]
    ids = (lambda g: g) if out_batched else (lambda g: (g[2], g[0], g[1], g[3]))
    grid = (nb, m // tm, n // tn, nk) if out_batched else (m // tm, n // tn, nb, nk)

    def a_map(*g):
        bi, i, j, q = ids(g)
        idx = (q, i) if mode == "tn" else (i, q)
        return (bi,) + idx if ab else idx

    def b_map(*g):
        bi, i, j, q = ids(g)
        idx = (j, q) if mode == "nt" else (q, j)
        return (bi,) + idx if bb else idx

    def o_map(*g):
        bi, i, j, q = ids(g)
        return (bi, i, j) if out_batched else (i, j)

    def body(a_ref, b_ref, o_ref, acc_ref):
        bi, _, _, q = ids(tuple(pl.program_id(d) for d in range(4)))
        first = (q == 0) if out_batched else (q == 0) & (bi == 0)
        last = (q == nk - 1) if out_batched else (q == nk - 1) & (bi == nb - 1)

        @pl.when(first)
        def _():
            acc_ref[...] = jnp.zeros_like(acc_ref)

        acc_ref[...] += _dg(a_ref[...], b_ref[...], ca, cb)

        @pl.when(last)
        def _():
            o_ref[...] = acc_ref[...].astype(o_ref.dtype)

    a_blk = (tk, tm) if mode == "tn" else (tm, tk)
    b_blk = (tn, tk) if mode == "nt" else (tk, tn)
    return pl.pallas_call(
        body, name=name, grid=grid,
        in_specs=[pl.BlockSpec(((None,) if ab else ()) + a_blk, a_map), pl.BlockSpec(((None,) if bb else ()) + b_blk, b_map)],
        out_specs=pl.BlockSpec(((None,) if out_batched else ()) + (tm, tn), o_map),
        out_shape=jax.ShapeDtypeStruct(((nb,) if out_batched else ()) + (m, n), out_dtype),
        scratch_shapes=[pltpu.VMEM((tm, tn), F32)],
        compiler_params=_params(("parallel", "parallel", "arbitrary", "arbitrary")),
    )(a, b)


def colgather(name, src, idx, dst_w, out_dtype):
    nsrc, rows, w = src.shape
    nbs = -(-w // LANES)
    ne = idx.shape[0]
    nbd = idx.shape[1] // LANES
    tiles = [sorted(set((idx[e, t * LANES:(t + 1) * LANES][idx[e, t * LANES:(t + 1) * LANES] >= 0] // LANES).tolist()))
             for e in range(ne) for t in range(nbd)]
    nslot = max(1, max(len(t) for t in tiles))
    tbl = np.full((ne * nbd, nslot), -1, np.int32)
    for i, t in enumerate(tiles):
        tbl[i, :len(t)] = t
    exact3 = src.dtype == F32

    def body(tbl_ref, idx_ref, src_ref, o_ref, acc_ref):
        ti, si = pl.program_id(0), pl.program_id(1)

        @pl.when(si == 0)
        def _():
            acc_ref[...] = jnp.zeros_like(acc_ref)

        t = tbl_ref[ti * nslot + si]

        @pl.when(t >= 0)
        def _():
            onehot = ((_iota((LANES, LANES), 0) + t * LANES) == idx_ref[...]).astype(BF16)
            col = _iota((1, LANES), 1) + (t % nbs) * LANES
            xv = jnp.where(col < w, src_ref[...], jnp.zeros((), src_ref.dtype))
            d = lambda p: lax.dot_general(p, onehot, (((1,), (0,)), ((), ())), preferred_element_type=F32)
            if exact3:
                x1, x2, x3 = _split3(xv)
                acc_ref[...] += (d(x3) + d(x2)) + d(x1)
            else:
                acc_ref[...] += d(xv)

        @pl.when(si == nslot - 1)
        def _():
            o_ref[...] = acc_ref[...].astype(o_ref.dtype)

    def src_map(ti, si, tbl_ref):
        t = jnp.maximum(tbl_ref[ti * nslot + si], 0)
        return (t // nbs, 0, t % nbs)

    grid_spec = pltpu.PrefetchScalarGridSpec(
        num_scalar_prefetch=1, grid=(ne * nbd, nslot),
        in_specs=[pl.BlockSpec((None, 1, LANES), lambda ti, si, tbl_ref: (ti // nbd, 0, ti % nbd)),
                  pl.BlockSpec((None, rows, LANES), src_map)],
        out_specs=pl.BlockSpec((None, rows, LANES), lambda ti, si, tbl_ref: (ti // nbd, 0, ti % nbd)),
        scratch_shapes=[pltpu.VMEM((rows, LANES), F32)])
    return pl.pallas_call(
        body, name=name, grid_spec=grid_spec,
        out_shape=jax.ShapeDtypeStruct((ne, rows, dst_w), out_dtype),
        compiler_params=_params(("parallel", "arbitrary")),
    )(jnp.asarray(tbl.reshape(-1)), jnp.asarray(idx.reshape(ne, 1, nbd * LANES).astype(np.int32)), src)


def swiglu3_fwd(name, gu, tm):
    _, nb, s, w = gu.shape

    def body(x_ref, o_ref):
        o_ref[...] = (_silu(x_ref[0]) * x_ref[1]).astype(o_ref.dtype)

    return pl.pallas_call(
        body, name=name, grid=(nb, s // tm),
        in_specs=[pl.BlockSpec((2, None, tm, w), lambda b, i: (0, b, i, 0))],
        out_specs=pl.BlockSpec((None, tm, w), lambda b, i: (b, i, 0)),
        out_shape=jax.ShapeDtypeStruct((nb, s, w), BF16),
        compiler_params=_params(("parallel", "parallel")),
    )(gu)


def swiglu3_bwd(name, gu, dact, tm):
    _, nb, s, w = gu.shape

    def body(x_ref, g_ref, o_ref):
        _, vjp = jax.vjp(lambda a, b: _silu(a) * b, x_ref[0], x_ref[1])
        dg, du = vjp(g_ref[...].astype(F32))
        o_ref[0] = dg.astype(o_ref.dtype)
        o_ref[1] = du.astype(o_ref.dtype)

    return pl.pallas_call(
        body, name=name, grid=(nb, s // tm),
        in_specs=[pl.BlockSpec((2, None, tm, w), lambda b, i: (0, b, i, 0)),
                  pl.BlockSpec((None, tm, w), lambda b, i: (b, i, 0))],
        out_specs=pl.BlockSpec((2, None, tm, w), lambda b, i: (0, b, i, 0)),
        out_shape=jax.ShapeDtypeStruct(gu.shape, BF16),
        compiler_params=_params(("parallel", "parallel")),
    )(gu, dact)


def _row_specs(rows, tm):
    return [pl.BlockSpec((tm, w), lambda i, _c=c: (i, _c)) for (_, w, c) in rows]


def rowstage_fwd(name, fn, rows, params, outs, tm):
    s = rows[0][0].shape[0]
    nr, npar = len(rows), len(params)

    def body(*refs):
        r = [x[...].astype(F32) for x in refs[:nr]]
        p = [x[...].astype(F32) for x in refs[nr:nr + npar]]
        for ref, val in zip(refs[nr + npar:], fn(r, p)):
            ref[...] = val.astype(ref.dtype)

    res = pl.pallas_call(
        body, name=name, grid=(s // tm,),
        in_specs=_row_specs(rows, tm) + [_whole(p) for p in params],
        out_specs=[pl.BlockSpec((tm, w), lambda i: (i, 0)) for (w, _) in outs],
        out_shape=[jax.ShapeDtypeStruct((s, w), dt) for (w, dt) in outs],
        compiler_params=_params(("parallel",)),
    )(*[r[0] for r in rows], *params)
    return res


def rowstage_bwd(name, fn, rows, params, douts, drow_dtypes, tm, adds=None):
    s = rows[0][0].shape[0]
    nr, npar, no = len(rows), len(params), len(douts)
    adds = adds or {}
    add_idx = sorted(adds)
    na = len(add_idx)

    def body(*refs):
        r = [x[...].astype(F32) for x in refs[:nr]]
        p = [x[...].astype(F32) for x in refs[nr:nr + npar]]
        g = [x[...].astype(F32) for x in refs[nr + npar:nr + npar + no]]
        a_refs = refs[nr + npar + no:nr + npar + no + na]
        dr_refs = refs[nr + npar + no + na:nr + npar + no + na + nr]
        dp_refs = refs[nr + npar + no + na + nr:]
        _, vjp = jax.vjp(lambda r_, p_: tuple(fn(r_, p_)), r, p)
        dr, dp = vjp(tuple(g))
        for j, (ref, val) in enumerate(zip(dr_refs, dr)):
            if j in adds:
                val = val + a_refs[add_idx.index(j)][...].astype(F32)
            ref[...] = val.astype(ref.dtype)

        @pl.when(pl.program_id(0) == 0)
        def _():
            for ref in dp_refs:
                ref[...] = jnp.zeros_like(ref)

        for ref, val in zip(dp_refs, dp):
            ref[...] += val

    res = pl.pallas_call(
        body, name=name, grid=(s // tm,),
        in_specs=(_row_specs(rows, tm) + [_whole(p) for p in params]
                  + [pl.BlockSpec((tm, d.shape[1]), lambda i: (i, 0)) for d in douts]
                  + [pl.BlockSpec((tm, rows[j][1]), lambda i: (i, 0)) for j in add_idx]),
        out_specs=([pl.BlockSpec((tm, w), lambda i: (i, 0)) for (_, w, _) in rows] + [_whole(p) for p in params]),
        out_shape=([jax.ShapeDtypeStruct((s, w), dt) for (_, w, _), dt in zip(rows, drow_dtypes)]
                   + [jax.ShapeDtypeStruct(p.shape, F32) for p in params]),
        compiler_params=_params(("arbitrary",)),
    )(*[r[0] for r in rows], *params, *douts, *[adds[j] for j in add_idx])
    return res[:nr], res[nr:]


def _flip(index_map, nc):
    return lambda h, n: index_map(h, nc - 1 - n)


def _take(v, split, j):
    if split is None:
        return v
    if split[0] == "lane":
        return v[:, j * split[1]:(j + 1) * split[1]]
    if split[0] == "lead":
        return v[j * split[1]:(j + 1) * split[1]]
    return v[j]


def _where(split, j):
    if split[0] == "lane":
        return (slice(None), slice(j * split[1], (j + 1) * split[1]))
    if split[0] == "lead":
        return (slice(j * split[1], (j + 1) * split[1]),)
    return (j,)


def scan_fwd(name, chunk_fn, nblk, hb, nc, seqs, hparams, sparams, state_shape, outs):
    ns, nhp, nsp, no = len(seqs), len(hparams), len(sparams), len(outs)

    def body(*refs):
        seq_r, hp_r, sp_r = refs[:ns], refs[ns:ns + nhp], refs[ns + nhp:ns + nhp + nsp]
        out_r = refs[ns + nhp + nsp:ns + nhp + nsp + no]
        st_out, st_scr = refs[-2], refs[-1]

        @pl.when(pl.program_id(1) == 0)
        def _():
            st_scr[...] = jnp.zeros_like(st_scr)

        seq_v = [x[...].astype(F32) for x in seq_r]
        hp_v = [x[...] for x in hp_r]
        sp_v = [x[...] for x in sp_r]
        for j in range(hb):
            st = st_scr[j]
            st_out[j] = st
            o, st_new = chunk_fn([_take(v, s[3], j) for v, s in zip(seq_v, seqs)],
                                 [_take(v, s[3], j) for v, s in zip(hp_v, hparams)], sp_v, st)
            for ref, spec, val in zip(out_r, outs, o):
                ref[_where(spec[4], j)] = val.astype(ref.dtype)
            st_scr[j] = st_new

    nst = len(state_shape)
    res = pl.pallas_call(
        body, name=name, grid=(nblk, nc),
        in_specs=([pl.BlockSpec(bs, im) for (_, bs, im, _) in seqs]
                  + [pl.BlockSpec(bs, lambda h, n, _im=im: _im(h)) for (_, bs, im, _) in hparams]
                  + [_whole(p) for p in sparams]),
        out_specs=([pl.BlockSpec(bs, im) for (_, _, bs, im, _) in outs]
                   + [pl.BlockSpec((hb, None) + tuple(state_shape), lambda h, n: (h, n) + (0,) * nst)]),
        out_shape=([jax.ShapeDtypeStruct(fs, dt) for (fs, dt, _, _, _) in outs]
                   + [jax.ShapeDtypeStruct((nblk * hb, nc) + tuple(state_shape), F32)]),
        scratch_shapes=[pltpu.VMEM((hb,) + tuple(state_shape), F32)],
        compiler_params=_params(("parallel", "arbitrary")),
    )(*[x[0] for x in seqs], *[x[0] for x in hparams], *sparams)
    return res[:no], res[no]


def scan_bwd(name, chunk_fn, nblk, hb, nc, seqs, hparams, sparams, state_shape, states, douts, dseqs):
    ns, nhp, nsp, no = len(seqs), len(hparams), len(sparams), len(douts)
    nst = len(state_shape)

    def body(*refs):
        seq_r, hp_r, sp_r = refs[:ns], refs[ns:ns + nhp], refs[ns + nhp:ns + nhp + nsp]
        base = ns + nhp + nsp
        st_r = refs[base]
        do_r = refs[base + 1:base + 1 + no]
        base += 1 + no
        ds_r, dhp_r, dsp_r = refs[base:base + ns], refs[base + ns:base + ns + nhp], refs[base + ns + nhp:base + ns + nhp + nsp]
        dst_scr = refs[-1]
        h, n = pl.program_id(0), pl.program_id(1)

        @pl.when(n == 0)
        def _():
            dst_scr[...] = jnp.zeros_like(dst_scr)
            for ref in dhp_r:
                ref[...] = jnp.zeros_like(ref)

        @pl.when((n == 0) & (h == 0))
        def _():
            for ref in dsp_r:
                ref[...] = jnp.zeros_like(ref)

        seq_v = [x[...].astype(F32) for x in seq_r]
        hp_v = [x[...] for x in hp_r]
        sp_v = [x[...] for x in sp_r]
        do_v = [x[...].astype(F32) for x in do_r]
        shared = [None] * ns
        dsp_sum = [None] * nsp
        for j in range(hb):
            prim = ([_take(v, s[3], j) for v, s in zip(seq_v, seqs)],
                    [_take(v, s[3], j) for v, s in zip(hp_v, hparams)], sp_v, st_r[j])
            _, vjp = jax.vjp(lambda a, b, c, d: chunk_fn(a, b, c, d), *prim)
            cot = (tuple(_take(v, s[3], j) for v, s in zip(do_v, douts)), dst_scr[j])
            ds, dhp, dsp, dst = vjp(cot)
            for i, (ref, spec, val) in enumerate(zip(ds_r, dseqs, ds)):
                if spec[4] is None:
                    shared[i] = val if shared[i] is None else shared[i] + val
                else:
                    ref[_where(spec[4], j)] = val.astype(ref.dtype)
            for ref, spec, val in zip(dhp_r, hparams, dhp):
                ref[_where(spec[3], j)] += val
            dsp_sum = [val if acc is None else acc + val for acc, val in zip(dsp_sum, dsp)]
            dst_scr[j] = dst
        for ref, val in zip(ds_r, shared):
            if val is not None:
                ref[...] = val.astype(ref.dtype)
        for ref, val in zip(dsp_r, dsp_sum):
            ref[...] += val

    res = pl.pallas_call(
        body, name=name, grid=(nblk, nc),
        in_specs=([pl.BlockSpec(bs, _flip(im, nc)) for (_, bs, im, _) in seqs]
                  + [pl.BlockSpec(bs, lambda h, n, _im=im: _im(h)) for (_, bs, im, _) in hparams]
                  + [_whole(p) for p in sparams]
                  + [pl.BlockSpec((hb, None) + tuple(state_shape), lambda h, n: (h, nc - 1 - n) + (0,) * nst)]
                  + [pl.BlockSpec(bs, _flip(im, nc)) for (_, bs, im, _) in douts]),
        out_specs=([pl.BlockSpec(bs, _flip(im, nc)) for (_, _, bs, im, _) in dseqs]
                   + [pl.BlockSpec(bs, lambda h, n, _im=im: _im(h)) for (_, bs, im, _) in hparams]
                   + [_whole(p) for p in sparams]),
        out_shape=([jax.ShapeDtypeStruct(fs, dt) for (fs, dt, _, _, _) in dseqs]
                   + [jax.ShapeDtypeStruct(x[0].shape, F32) for x in hparams]
                   + [jax.ShapeDtypeStruct(p.shape, F32) for p in sparams]),
        scratch_shapes=[pltpu.VMEM((hb,) + tuple(state_shape), F32)],
        compiler_params=_params(("arbitrary", "arbitrary")),
    )(*[x[0] for x in seqs], *[x[0] for x in hparams], *sparams, states, *[x[0] for x in douts])
    return res[:ns], res[ns:ns + nhp], res[ns + nhp:]


def _shift_down(x, n, rows):
    if n == 0:
        return x
    return jnp.where(rows >= n, pltpu.roll(x, n, 0), 0.0)


def _shift_up(x, n, rows):
    if n == 0:
        return x
    s = x.shape[0]
    return jnp.where(rows < s - n, pltpu.roll(x, s - n, 0), 0.0)


def conv_fwd(name, x, col0, w, b):
    s, cw = x.shape[0], w.shape[1]

    def body(x_ref, w_ref, b_ref, o_ref):
        xv = x_ref[...]
        rows = _iota(xv.shape, 0)
        u = jnp.broadcast_to(b_ref[...], xv.shape)
        for j in range(CONV_K):
            u = u + w_ref[j:j + 1, :] * _shift_down(xv, CONV_K - 1 - j, rows)
        o_ref[...] = _silu(u)

    return pl.pallas_call(
        body, name=name, grid=(cw // LANES,),
        in_specs=[pl.BlockSpec((s, LANES), lambda j: (0, col0 + j)),
                  pl.BlockSpec((CONV_K, LANES), lambda j: (0, j)),
                  pl.BlockSpec((1, LANES), lambda j: (0, j))],
        out_specs=pl.BlockSpec((s, LANES), lambda j: (0, j)),
        out_shape=jax.ShapeDtypeStruct((s, cw), F32),
        compiler_params=_params(("parallel",)),
    )(x, w, b)


def conv_bwd(name, x, col0, w, b, dout):
    s, cw = x.shape[0], w.shape[1]

    def body(x_ref, w_ref, b_ref, g_ref, dx_ref, dw_ref, db_ref):
        xv = x_ref[...]
        rows = _iota(xv.shape, 0)
        sh = [_shift_down(xv, CONV_K - 1 - j, rows) for j in range(CONV_K)]
        u = jnp.broadcast_to(b_ref[...], xv.shape)
        for j in range(CONV_K):
            u = u + w_ref[j:j + 1, :] * sh[j]
        sg = _sigmoid(u)
        du = g_ref[...] * (sg * (1.0 + u * (1.0 - sg)))
        dx = jnp.zeros_like(xv)
        for j in range(CONV_K):
            dx = dx + w_ref[j:j + 1, :] * _shift_up(du, CONV_K - 1 - j, rows)
            dw_ref[j:j + 1, :] = jnp.sum(du * sh[j], axis=0, keepdims=True)
        dx_ref[...] = dx.astype(dx_ref.dtype)
        db_ref[...] = jnp.sum(du, axis=0, keepdims=True)

    return pl.pallas_call(
        body, name=name, grid=(cw // LANES,),
        in_specs=[pl.BlockSpec((s, LANES), lambda j: (0, col0 + j)),
                  pl.BlockSpec((CONV_K, LANES), lambda j: (0, j)),
                  pl.BlockSpec((1, LANES), lambda j: (0, j)),
                  pl.BlockSpec((s, LANES), lambda j: (0, j))],
        out_specs=[pl.BlockSpec((s, LANES), lambda j: (0, j)),
                   pl.BlockSpec((CONV_K, LANES), lambda j: (0, j)),
                   pl.BlockSpec((1, LANES), lambda j: (0, j))],
        out_shape=[jax.ShapeDtypeStruct((s, cw), BF16), jax.ShapeDtypeStruct((CONV_K, cw), F32),
                   jax.ShapeDtypeStruct((1, cw), F32)],
        compiler_params=_params(("parallel",)),
    )(x, w, b, dout)


def exchange(name, sends, broadcast):
    nop = len(sends)

    def body(*refs):
        send_refs, recv_refs = refs[:nop], refs[nop:2 * nop]
        send_sems, recv_sems, local_sems = refs[2 * nop:]
        x, y, c = lax.axis_index("x"), lax.axis_index("y"), lax.axis_index("c")
        me = 4 * x + 2 * y + c

        def src(i, peer):
            return send_refs[i] if broadcast else send_refs[i].at[peer]

        def remote(i, k, dev, peer, landing):
            return pltpu.make_async_remote_copy(
                src_ref=src(i, peer), dst_ref=recv_refs[i].at[landing], send_sem=send_sems.at[i * (N_DEV - 1) + k],
                recv_sem=recv_sems.at[i * (N_DEV - 1) + k], device_id=dev, device_id_type=pl.DeviceIdType.MESH)

        local = [pltpu.make_async_copy(src(i, me), recv_refs[i].at[me], local_sems.at[i]) for i in range(nop)]
        for cp in local:
            cp.start()
        peers = []
        for k in range(1, N_DEV):
            px = 1 - x if (k >> 2) & 1 else x
            py = 1 - y if (k >> 1) & 1 else y
            pc = 1 - c if k & 1 else c
            peers.append(((px, py, pc), 4 * px + 2 * py + pc))
        sent = []
        for k, (dev, peer) in enumerate(peers):
            for i in range(nop):
                cp = remote(i, k, dev, peer, me)
                cp.start()
                sent.append(cp)
        for k, (dev, peer) in enumerate(peers):
            for i in range(nop):
                remote(i, k, dev, peer, peer).wait_recv()
        for cp in sent:
            cp.wait_send()
        for cp in local:
            cp.wait()

    hbm = pl.BlockSpec(memory_space=pltpu.HBM)
    return pl.pallas_call(
        body, name=name,
        in_specs=[hbm] * nop, out_specs=[hbm] * nop,
        out_shape=[jax.ShapeDtypeStruct((N_DEV,) + tuple(t.shape if broadcast else t.shape[1:]), t.dtype) for t in sends],
        scratch_shapes=[pltpu.SemaphoreType.DMA((nop * (N_DEV - 1),)), pltpu.SemaphoreType.DMA((nop * (N_DEV - 1),)),
                        pltpu.SemaphoreType.DMA((nop,))],
        compiler_params=pltpu.CompilerParams(has_side_effects=True),
    )(*sends)


def adamw_sum(name, parts, w, m, v):
    rws, cols = w.shape
    nsum = parts.shape[0]
    tr = _pick(rws, (256, 128, 64, 32, 16, 8))
    c1 = 1.0 / (1.0 - ADAM_B1 ** ADAM_STEP)
    c2 = 1.0 / (1.0 - ADAM_B2 ** ADAM_STEP)

    def body(p_ref, w_ref, m_ref, v_ref, g_ref, d_ref, nm_ref, nv_ref):
        g = p_ref[0]
        for j in range(1, nsum):
            g = g + p_ref[j]
        nm = ADAM_B1 * m_ref[...] + (1.0 - ADAM_B1) * g
        nv = ADAM_B2 * v_ref[...] + (1.0 - ADAM_B2) * (g * g)
        g_ref[...] = g
        nm_ref[...] = nm
        nv_ref[...] = nv
        d_ref[...] = -ADAM_LR * ((nm * c1) / (jnp.sqrt(nv * c2) + ADAM_EPS) + ADAM_WD * w_ref[...])

    blk = pl.BlockSpec((tr, cols), lambda i: (i, 0))
    return pl.pallas_call(
        body, name=name, grid=(rws // tr,),
        in_specs=[pl.BlockSpec((nsum, tr, cols), lambda i: (0, i, 0)), blk, blk, blk],
        out_specs=[blk, blk, blk, blk],
        out_shape=[jax.ShapeDtypeStruct(w.shape, F32)] * 4,
        compiler_params=_params(("parallel",)),
    )(parts, w, m, v)


def ada_fwd(name, c_all, w, b):
    nl = w.shape[0]

    def body(c_ref, w_ref, b_ref, o_ref):
        ca = _silu(c_ref[...])
        for l in range(nl):
            o_ref[l] = mm_nn(ca, w_ref[l]) + b_ref[l]

    return pl.pallas_call(
        body, name=name,
        out_shape=jax.ShapeDtypeStruct((nl, c_all.shape[0], w.shape[2]), F32),
        compiler_params=pltpu.CompilerParams(vmem_limit_bytes=VMEM_LIMIT),
    )(c_all, w, b)


def ada_bwd(name, c_all, dmod):
    nl = dmod.shape[0]

    def body(c_ref, g_ref, o_ref):
        ca = _silu(c_ref[...])
        for l in range(nl):
            o_ref[l] = mm_tn(ca, g_ref[l])

    return pl.pallas_call(
        body, name=name,
        out_shape=jax.ShapeDtypeStruct((nl, c_all.shape[1], dmod.shape[2]), F32),
        compiler_params=pltpu.CompilerParams(vmem_limit_bytes=VMEM_LIMIT),
    )(c_all, dmod)


def lower_bounds_fn(rows, params):
    (lg,), _ = rows, params
    nl = lg.shape[0]
    mx = jnp.max(lg, axis=0, keepdims=True)
    e = jnp.exp(lg - mx)
    p = e / jnp.sum(e, axis=0, keepdims=True)
    layer = _iota((nl, 1), 0)
    acc = jnp.zeros_like(p)
    for j in range(1, nl):
        pj = jnp.sum(jnp.where(layer == j, p, 0.0), axis=0, keepdims=True)
        acc = acc + jnp.where(layer >= j, 1.0, 0.0) * pj
    return (acc,)


def loss_call(name, x, tgt, nw, tm):
    s, d = x.shape

    def body(x_ref, t_ref, w_ref, l_ref, dx_ref, dw_ref):
        def f(xv, wv):
            err = _rms(xv, wv) - t_ref[...]
            return jnp.sum(0.5 * jnp.mean(err * err, axis=-1, keepdims=True), axis=0, keepdims=True)

        val, vjp = jax.vjp(f, x_ref[...], w_ref[...])
        dx, dw = vjp(jnp.ones_like(val))

        @pl.when(pl.program_id(0) == 0)
        def _():
            l_ref[...] = jnp.zeros_like(l_ref)
            dw_ref[...] = jnp.zeros_like(dw_ref)

        l_ref[...] += jnp.broadcast_to(val, l_ref.shape)
        dw_ref[...] += dw
        dx_ref[...] = dx

    row = pl.BlockSpec((tm, d), lambda i: (i, 0))
    return pl.pallas_call(
        body, name=name, grid=(s // tm,),
        in_specs=[row, row, _whole(nw)],
        out_specs=[pl.BlockSpec((8, LANES), lambda i: (0, 0)), row, _whole(nw)],
        out_shape=[jax.ShapeDtypeStruct((8, LANES), F32), jax.ShapeDtypeStruct((s, d), F32),
                   jax.ShapeDtypeStruct(nw.shape, F32)],
        compiler_params=_params(("arbitrary",)),
    )(x, tgt, nw)


class Dims:
    def __init__(self, s, d, ffn):
        self.s, self.d, self.ffn = s, d, ffn
        self.mix = 3 * d // 4
        self.nh = self.mix // HEAD
        self.ssm_heads = self.mix // SSM_P
        self.pairs = self.mix // (2 * SSM_P)
        self.nc = s // CHUNK
        self.conv_ssm = self.mix + 4 * HEAD
        self.conv_w = self.conv_ssm + 3 * self.mix
        self.o_gates = 4 * self.mix
        self.o_sz = self.o_gates + 3 * d
        self.o_gz = self.o_sz + self.mix
        self.o_conv = self.o_gz + self.mix
        self.o_small = self.o_conv + self.conv_w
        used = self.o_small + LANES
        self.np = -(-used // 1280) * 1280
        self.tm = _pick(s, (256, 128, 64))
        mix, nh = self.mix, self.nh
        self.in_sizes = (mix, mix, mix, mix, mix, self.conv_ssm, self.ssm_heads, 3 * mix, mix, nh, nh, 3 * d)
        self.in_width = sum(self.in_sizes)


def w_in_tables(dm, nshard):
    off = np.cumsum((0,) + dm.in_sizes)
    hq, hf, hi, hg, sz, sxbc, sdt, gqkv, gz, gb, ga, gates = (np.arange(off[i], off[i + 1]) for i in range(12))
    hgrn = np.stack([t.reshape(dm.nh, HEAD) for t in (hq, hf, hi, hg)], axis=1).reshape(-1)
    perm = np.concatenate([hgrn, gates, sz, gz, sxbc, gqkv, sdt, gb, ga])
    perm = np.concatenate([perm, np.full(dm.np - perm.size, -1)])
    shard = dm.in_width // nshard
    wpad = -(-shard // LANES) * LANES
    fwd = np.where(perm >= 0, (perm // shard) * wpad + perm % shard, -1)[None]
    inv = np.zeros(dm.in_width, np.int64)
    inv[perm[perm >= 0]] = np.nonzero(perm >= 0)[0]
    bwd = np.full((nshard, wpad), -1)
    bwd[:, :shard] = inv.reshape(nshard, shard)
    return fwd.astype(np.int32), bwd.astype(np.int32)


def _small_views(dm, small):
    t = small.T
    col = lambda a: a[:, :, None]
    row = lambda a: a.reshape(a.shape[0], dm.nc, 1, CHUNK)
    a, b = dm.ssm_heads, dm.ssm_heads + dm.nh
    sdt, gb, ga = t[:a], t[a:b], t[b:b + dm.nh]
    return col(sdt), row(sdt), col(gb), col(ga), row(ga)


def _scan_specs(dm, proj, conv_out, views, lp):
    dt_col, dt_row, gb_col, ga_col, ga_row = views
    mixb, nh = dm.mix // LANES, dm.nh
    s, mix = dm.s, dm.mix
    lane = ("lane", LANES)
    hb = HEADS_PER_STEP
    hw = (CHUNK, hb * LANES)
    hgrn = dict(
        nblk=nh // hb, hb=hb, fn=hgrn_chunk,
        seqs=[(proj, (CHUNK, hb * 4 * HEAD), lambda h, n: (n, h), ("lane", 4 * HEAD))],
        hparams=[(lp["lb"], (1, hb * HEAD), lambda h: (0, h), lane)],
        sparams=[lp["hgrn_norm"]],
        dseqs=[((s, 4 * mix), BF16, (CHUNK, hb * 4 * HEAD), lambda h, n: (n, h), ("lane", 4 * HEAD))],
        io=(hw, lambda h, n: (n, h), lane))
    ppg = dm.pairs // 2
    gw = (CHUNK, ppg * LANES)
    pcol = ((2 * ppg, CHUNK, 1), lambda g, n: (g, n, 0), ("lead", 2))
    prow = ((2 * ppg, None, 1, CHUNK), lambda g, n: (g, n, 0, 0), ("lead", 2))
    ppar = ((2 * ppg, 1, 1), lambda g: (g, 0, 0), ("lead", 2))
    bc = lambda first: ((CHUNK, LANES), lambda g, n: (n, first + g), None)
    ssd = dict(
        nblk=2, hb=ppg, fn=ssd_chunk,
        seqs=[(conv_out, gw, lambda g, n: (n, g), lane), (conv_out,) + bc(mixb), (conv_out,) + bc(mixb + 2),
              (dt_col,) + pcol, (dt_row,) + prow],
        hparams=[(lp["ssm_dt_bias"],) + ppar, (lp["ssm_a_log"],) + ppar],
        sparams=[],
        dseqs=[((s, mix), F32, gw, lambda g, n: (n, g), lane), ((s, 2 * LANES), F32) + bc(0), ((s, 2 * LANES), F32) + bc(0),
               (dt_col.shape, F32) + pcol, (dt_row.shape, F32) + prow],
        io=(gw, lambda g, n: (n, g), lane))
    cq, cgz = dm.conv_ssm // LANES, dm.o_gz // LANES
    assert cq % hb == 0 and nh % hb == 0 and cgz % hb == 0
    hcol = ((hb, CHUNK, 1), lambda h, n: (h, n, 0), ("idx",))
    hrow = ((hb, None, 1, CHUNK), lambda h, n: (h, n, 0, 0), ("idx",))
    hpar = ((hb, 1, 1), lambda h: (h, 0, 0), ("idx",))
    at = lambda first: (hw, lambda h, n: (n, first // hb + h), lane)
    gdn = dict(
        nblk=nh // hb, hb=hb, fn=gdn_chunk,
        seqs=[(conv_out,) + at(cq), (conv_out,) + at(cq + nh), (conv_out,) + at(cq + 2 * nh), (proj,) + at(cgz),
              (gb_col,) + hcol, (ga_col,) + hcol, (ga_row,) + hrow],
        hparams=[(lp["gdn_dt_bias"],) + hpar, (lp["gdn_a_log"],) + hpar],
        sparams=[lp["gdn_norm"]],
        dseqs=[((s, mix), F32) + at(0), ((s, mix), F32) + at(0), ((s, mix), F32) + at(0), ((s, mix), BF16) + at(0),
               (gb_col.shape, F32) + hcol, (ga_col.shape, F32) + hcol, (ga_row.shape, F32) + hrow],
        io=(hw, lambda h, n: (n, h), lane))
    return hgrn, ssd, gdn


def _run_scan_fwd(dm, name, sp):
    out = ((dm.s, dm.mix), F32) + sp["io"]
    (y,), states = scan_fwd(name, sp["fn"], sp["nblk"], sp["hb"], dm.nc, sp["seqs"], sp["hparams"], sp["sparams"],
                            (HEAD, HEAD), [out])
    return y, states


def _run_scan_bwd(dm, name, sp, states, dy):
    return scan_bwd(name, sp["fn"], sp["nblk"], sp["hb"], dm.nc, sp["seqs"], sp["hparams"], sp["sparams"], (HEAD, HEAD),
                    states, [(dy,) + sp["io"]], sp["dseqs"])


def layer_fwd(dm, l, x, lp):
    tm, d, mix = dm.tm, dm.d, dm.mix
    tag = f"l{l}_"
    (h,) = rowstage_fwd(tag + "norm1", normmod_fn, [(x, d, 0)], [lp["norm_mix"], lp["sc1"], lp["sh1"]], [(d, BF16)], tm)
    proj = matmul(tag + "proj", h, lp["w_in"], "nn", F32)
    conv_out = conv_fwd(tag + "conv", proj, dm.o_conv // LANES, lp["conv_w"], lp["conv_b"])
    small = proj[:, dm.o_small:dm.o_small + LANES]
    views = _small_views(dm, small)
    hg, sd, gd = _scan_specs(dm, proj, conv_out, views, lp)
    yh, st_h = _run_scan_fwd(dm, tag + "hgrn", hg)
    y_ssd, st_s = _run_scan_fwd(dm, tag + "ssd", sd)
    yg, st_g = _run_scan_fwd(dm, tag + "gdn", gd)
    (ys,) = rowstage_fwd(tag + "ssmpost", ssmpost_fn,
                         [(y_ssd, mix, 0), (conv_out, mix, 0), (proj, mix, dm.o_sz // mix)],
                         [lp["ssm_d_exp"], lp["ssm_norm"]], [(mix, F32)], tm)
    (merged,) = rowstage_fwd(tag + "merge", merge_fn, [(yh, mix, 0), (ys, mix, 0), (yg, mix, 0), (proj, 3 * d, 1)],
                             [lp["b_merge"], lp["w_branch"]], [(d, BF16)], tm)
    (x1,) = rowstage_fwd(tag + "outproj", outproj_fn, [(merged, d, 0), (x, d, 0)], [lp["g1"], lp["w_out"]], [(d, F32)], tm)
    (h2,) = rowstage_fwd(tag + "norm2", normmod_fn, [(x1, d, 0)], [lp["norm_ffn"], lp["sc2"], lp["sh2"]], [(d, BF16)], tm)
    gu = bmatmul(tag + "ffn_in", h2, lp["w_ffn_in"], "nn", F32, True)
    gu = gu.reshape((2, gu.shape[0] // 2) + gu.shape[1:])
    act = swiglu3_fwd(tag + "swiglu", gu, tm)
    o2 = bmatmul(tag + "ffn_out", act, lp["w_ffn_out"], "nn", F32, False)
    (x2,) = rowstage_fwd(tag + "resid", resid_fn, [(x1, d, 0), (o2, d, 0)], [lp["g2"]], [(d, F32)], tm)
    saved = dict(x=x, h=h, proj=proj, conv_out=conv_out, views=views, yh=yh, y_ssd=y_ssd, yg=yg, ys=ys,
                 st_h=st_h, st_s=st_s, st_g=st_g, merged=merged, x1=x1, h2=h2, gu=gu, act=act, o2=o2)
    return x2, saved


def layer_bwd(dm, l, dx2, lp, sv):
    tm, d, mix, s = dm.tm, dm.d, dm.mix, dm.s
    tag = f"l{l}_b_"
    g = {}
    (dx1_a, do2), (g["g2"],) = rowstage_bwd(tag + "resid", resid_fn, [(sv["x1"], d, 0), (sv["o2"], d, 0)], [lp["g2"]],
                                            [dx2], [F32, BF16], tm)
    dact = bmatmul(tag + "ffn_out_dx", do2, lp["w_ffn_out"], "nt", BF16, True)
    g["w_ffn_out"] = bmatmul(tag + "ffn_out_dw", sv["act"], do2, "tn", F32, True)
    dgu = swiglu3_bwd(tag + "swiglu", sv["gu"], dact, tm)
    dgu = dgu.reshape((-1,) + dgu.shape[2:])
    dh2 = bmatmul(tag + "ffn_in_dx", dgu, lp["w_ffn_in"], "nt", BF16, False)
    g["w_ffn_in"] = bmatmul(tag + "ffn_in_dw", sv["h2"], dgu, "tn", F32, True)
    (dx1,), (g["norm_ffn"], g["sc2"], g["sh2"]) = rowstage_bwd(
        tag + "norm2", normmod_fn, [(sv["x1"], d, 0)], [lp["norm_ffn"], lp["sc2"], lp["sh2"]], [dh2], [F32], tm,
        adds={0: dx1_a})
    (dmerged, dx_a), (g["g1"], g["w_out"]) = rowstage_bwd(
        tag + "outproj", outproj_fn, [(sv["merged"], d, 0), (sv["x"], d, 0)], [lp["g1"], lp["w_out"]], [dx1],
        [BF16, F32], tm)
    proj, conv_out = sv["proj"], sv["conv_out"]
    (dyh, dys, dyg, dgates), (g["b_merge"], g["w_branch"]) = rowstage_bwd(
        tag + "merge", merge_fn, [(sv["yh"], mix, 0), (sv["ys"], mix, 0), (sv["yg"], mix, 0), (proj, 3 * d, 1)],
        [lp["b_merge"], lp["w_branch"]], [dmerged], [F32, F32, F32, BF16], tm)
    (dy_ssd, dxs_a, dsz), (g["ssm_d_exp"], g["ssm_norm"]) = rowstage_bwd(
        tag + "ssmpost", ssmpost_fn, [(sv["y_ssd"], mix, 0), (conv_out, mix, 0), (proj, mix, dm.o_sz // mix)],
        [lp["ssm_d_exp"], lp["ssm_norm"]], [dys], [F32, F32, BF16], tm)
    hg, sd, gd = _scan_specs(dm, proj, conv_out, sv["views"], lp)
    (dhgrn,), (g["lb"],), (g["hgrn_norm"],) = _run_scan_bwd(dm, tag + "hgrn", hg, sv["st_h"], dyh)
    (dxs_b, dbp, dcp, d_dt_col, d_dt_row), (g["ssm_dt_bias"], g["ssm_a_log"]), _ = _run_scan_bwd(
        dm, tag + "ssd", sd, sv["st_s"], dy_ssd)
    (dq, dk, dv, dgz, d_gb_col, d_ga_col, d_ga_row), (g["gdn_dt_bias"], g["gdn_a_log"]), (g["gdn_norm"],) = _run_scan_bwd(
        dm, tag + "gdn", gd, sv["st_g"], dyg)
    dconv = jnp.concatenate([dxs_a + dxs_b, dbp, dcp, dq, dk, dv], axis=1)
    dpc, g["conv_w"], g["conv_b"] = conv_bwd(tag + "conv", proj, dm.o_conv // LANES, lp["conv_w"], lp["conv_b"], dconv)
    unrow = lambda t: t.reshape(t.shape[0], s).T
    dsmall = jnp.concatenate([d_dt_col[:, :, 0].T + unrow(d_dt_row), d_gb_col[:, :, 0].T,
                              d_ga_col[:, :, 0].T + unrow(d_ga_row)], axis=1)
    pad = jnp.zeros((s, dm.np - dm.o_small - dsmall.shape[1]), BF16)
    dproj = jnp.concatenate([dhgrn, dgates, dsz, dgz, dpc, dsmall.astype(BF16), pad], axis=1)
    dh = matmul(tag + "proj_dx", dproj, lp["w_in"], "nt", BF16)
    g["w_in"] = matmul(tag + "proj_dw", sv["h"], dproj, "tn", F32)
    (dx,), (g["norm_mix"], g["sc1"], g["sh1"]) = rowstage_bwd(
        tag + "norm1", normmod_fn, [(sv["x"], d, 0)], [lp["norm_mix"], lp["sc1"], lp["sh1"]], [dh], [F32], tm,
        adds={0: dx_a})
    return dx, g


WEIGHTS = ("w_ada", "b_ada", "norm_mix", "norm_ffn", "w_in", "b_merge", "hgrn_lb_logits", "hgrn_norm", "ssm_conv_w",
           "ssm_conv_b", "ssm_dt_bias", "ssm_a_log", "ssm_d", "ssm_norm", "gdn_conv_w", "gdn_dt_bias", "gdn_a_log",
           "gdn_norm", "w_branch", "w_out", "w_ffn_in", "w_ffn_out", "norm_final")
GATHERED = ("w_in", "w_branch", "w_out", "w_ffn_in", "w_ffn_out")
PACKET = ("b_ada", "norm_mix", "norm_ffn", "b_merge", "hgrn_norm", "ssm_conv_b", "ssm_dt_bias", "ssm_a_log", "ssm_d",
          "ssm_norm", "gdn_dt_bias", "gdn_a_log", "gdn_norm", "norm_final")
MISC = ("hgrn_lb_logits", "ssm_conv_w", "gdn_conv_w")


def _pack(arrs, dtype, row_mult, lead=0):
    flat = jnp.concatenate([t.reshape(t.shape[:lead] + (-1,)).astype(dtype) for t in arrs], axis=lead)
    n = flat.shape[-1]
    unit = row_mult * LANES
    tot = -(-n // unit) * unit
    flat = jnp.pad(flat, [(0, 0)] * lead + [(0, tot - n)])
    return flat.reshape(flat.shape[:lead] + (tot // LANES, LANES))


def _unpack(packed, shapes, lead=0):
    flat = packed.reshape(packed.shape[:lead] + (-1,))
    out, off = [], 0
    for shp in shapes:
        n = int(np.prod(shp))
        out.append(flat[..., off:off + n].reshape(flat.shape[:lead] + tuple(shp)))
        off += n
    return out


def _shard2d(t):
    return t.reshape((-1, t.shape[-1]))


def weights_from_shards(dm, l, got, idx):
    w_in, wb, w_out, wf, wfo = got
    d, mix = dm.d, dm.mix
    return dict(
        w_in=colgather(f"l{l}_w_in", w_in, idx, dm.np, BF16)[0],
        w_branch=wb.reshape(N_DEV, 3, mix, d // N_DEV).transpose(1, 2, 0, 3).reshape(3, mix, d),
        w_out=w_out.reshape(d, d), w_ffn_in=wf, w_ffn_out=wfo.reshape(N_DEV // 2, -1, d))


def shards_of_grads(dm, l, g, idx):
    d, mix = dm.d, dm.mix
    return [colgather(f"l{l}_g_w_in", g["w_in"][None], idx, dm.in_width // N_DEV, F32),
            g["w_branch"].reshape(3, mix, N_DEV, d // N_DEV).transpose(2, 0, 1, 3).reshape(N_DEV, 3 * mix, d // N_DEV),
            g["w_out"].reshape(N_DEV, d // N_DEV, d), g["w_ffn_in"], g["w_ffn_out"].reshape(N_DEV, -1, d)]


def layer_params(dm, l, full, small, mod_l, lb_l):
    d, mix = dm.d, dm.mix
    row = lambda t: t.reshape(1, -1)
    head = lambda t: t.reshape(-1, 1, 1)
    sh1, sc1, g1, sh2, sc2, g2 = (row(mod_l[i * d:(i + 1) * d]) for i in range(6))
    conv_b = jnp.concatenate([small["ssm_conv_b"][l], jnp.zeros((3 * mix,), F32)])
    return dict(
        w_in=full["w_in"], w_branch=full["w_branch"], w_out=full["w_out"],
        w_ffn_in=full["w_ffn_in"], w_ffn_out=full["w_ffn_out"],
        norm_mix=row(small["norm_mix"][l]), norm_ffn=row(small["norm_ffn"][l]), b_merge=row(small["b_merge"][l]),
        hgrn_norm=row(small["hgrn_norm"][l]), lb=row(lb_l),
        conv_w=jnp.concatenate([small["ssm_conv_w"][l], small["gdn_conv_w"][l]], axis=1), conv_b=row(conv_b),
        ssm_dt_bias=head(small["ssm_dt_bias"][l]), ssm_a_log=head(small["ssm_a_log"][l]),
        ssm_d_exp=row(jnp.repeat(small["ssm_d"][l], SSM_P)), ssm_norm=row(small["ssm_norm"][l]),
        gdn_dt_bias=head(small["gdn_dt_bias"][l]), gdn_a_log=head(small["gdn_a_log"][l]), gdn_norm=row(small["gdn_norm"][l]),
        sh1=sh1, sc1=sc1, g1=g1, sh2=sh2, sc2=sc2, g2=g2)


def layer_grads(dm, g):
    cs = dm.conv_ssm
    out = dict(
        w_in=g["w_in"], w_branch=g["w_branch"], w_out=g["w_out"], w_ffn_in=g["w_ffn_in"],
        w_ffn_out=g["w_ffn_out"], norm_mix=g["norm_mix"][0], norm_ffn=g["norm_ffn"][0], b_merge=g["b_merge"][0],
        hgrn_norm=g["hgrn_norm"][0], ssm_conv_w=g["conv_w"][:, :cs], gdn_conv_w=g["conv_w"][:, cs:],
        ssm_conv_b=g["conv_b"][0, :cs], ssm_dt_bias=g["ssm_dt_bias"][:, 0, 0], ssm_a_log=g["ssm_a_log"][:, 0, 0],
        ssm_d=g["ssm_d_exp"].reshape(dm.ssm_heads, SSM_P).sum(axis=1), ssm_norm=g["ssm_norm"][0],
        gdn_dt_bias=g["gdn_dt_bias"][:, 0, 0], gdn_a_log=g["gdn_a_log"][:, 0, 0], gdn_norm=g["gdn_norm"][0])
    dmod = jnp.concatenate([g[k][0] for k in ("sh1", "sc1", "g1", "sh2", "sc2", "g2")])
    return out, dmod, g["lb"][0]


def local_step(dm, x, tgt, lps, norm_final):
    saved = []
    for l, lp in enumerate(lps):
        x, sv = layer_fwd(dm, l, x, lp)
        saved.append(sv)
    loss, dx, dnf = loss_call("loss", x, tgt, norm_final, dm.tm)
    grads = [None] * len(lps)
    for l in reversed(range(len(lps))):
        dx, grads[l] = layer_bwd(dm, l, dx, lps[l], saved[l])
    return loss, dx, dnf, grads


def kernel(x, c, w_ada, b_ada, norm_mix, norm_ffn, w_in, b_merge, hgrn_lb_logits, hgrn_norm, ssm_conv_w, ssm_conv_b, ssm_dt_bias, ssm_a_log, ssm_d, ssm_norm, gdn_conv_w, gdn_dt_bias, gdn_a_log, gdn_norm, w_branch, w_out, w_ffn_in, w_ffn_out, norm_final, loss_target, m_w_ada, m_b_ada, m_norm_mix, m_norm_ffn, m_w_in, m_b_merge, m_hgrn_lb_logits, m_hgrn_norm, m_ssm_conv_w, m_ssm_conv_b, m_ssm_dt_bias, m_ssm_a_log, m_ssm_d, m_ssm_norm, m_gdn_conv_w, m_gdn_dt_bias, m_gdn_a_log, m_gdn_norm, m_w_branch, m_w_out, m_w_ffn_in, m_w_ffn_out, m_norm_final, v_w_ada, v_b_ada, v_norm_mix, v_norm_ffn, v_w_in, v_b_merge, v_hgrn_lb_logits, v_hgrn_norm, v_ssm_conv_w, v_ssm_conv_b, v_ssm_dt_bias, v_ssm_a_log, v_ssm_d, v_ssm_norm, v_gdn_conv_w, v_gdn_dt_bias, v_gdn_a_log, v_gdn_norm, v_w_branch, v_w_out, v_w_ffn_in, v_w_ffn_out, v_norm_final):
    a = dict(locals())
    x, tgt = a["x"][0], a["loss_target"][0]
    s, d = x.shape
    nl = a["w_ada"].shape[0]
    dm = Dims(s, d, a["w_ffn_out"].shape[1] * N_DEV)
    me = 4 * lax.axis_index("x") + 2 * lax.axis_index("y") + lax.axis_index("c")

    first = [a["c"], a["ssm_conv_w"], a["gdn_conv_w"]]
    c_all, scw, gcw = _unpack(exchange("gather_c", [_pack(first, F32, 8)], True)[0], [t.shape for t in first], lead=1)
    small = dict(a, ssm_conv_w=scw.transpose(1, 2, 0, 3).reshape(scw.shape[1:3] + (-1,)),
                 gdn_conv_w=gcw.transpose(1, 2, 0, 3).reshape(gcw.shape[1:3] + (-1,)))
    c_pad = jnp.zeros((LANES, d), F32).at[:N_DEV].set(c_all.reshape(N_DEV, d))
    ncol = a["w_ada"].shape[2]
    b_mine = lax.dynamic_slice(a["b_ada"], (0, me * ncol), (nl, ncol))[:, None, :]
    mod_part = ada_fwd("ada_fwd", c_pad, a["w_ada"], b_mine)[:, :N_DEV, :]
    (mod,) = exchange("a2a_mod", [mod_part.transpose(1, 0, 2)], False)
    mod = mod.transpose(1, 0, 2).reshape(nl, N_DEV * ncol)
    (lb,) = rowstage_fwd("lower_bounds", lower_bounds_fn, [(a["hgrn_lb_logits"], dm.mix, 0)], [], [(dm.mix, F32)], nl)

    idx_fwd, idx_bwd = w_in_tables(dm, N_DEV)
    lps = []
    for l in range(nl):
        got = exchange(f"gather_w{l}", [_shard2d(a[n][l]).astype(BF16) for n in GATHERED], True)
        lps.append(layer_params(dm, l, weights_from_shards(dm, l, got, idx_fwd), small, mod[l], lb[l]))

    loss, dx, dnf, grads = local_step(dm, x, tgt, lps, a["norm_final"].reshape(1, d))

    per_layer = [layer_grads(dm, g) for g in grads]
    res = {}
    for l in range(nl):
        parts = exchange(f"scatter_g{l}", shards_of_grads(dm, l, per_layer[l][0], idx_bwd), False)
        for n, p in zip(GATHERED, parts):
            outs = adamw_sum(f"adamw_l{l}_{n}", p, *[_shard2d(a[q + n][l]) for q in ("", "m_", "v_")])
            for kind, o in zip(("grad", "delta", "new_m", "new_v"), outs):
                res.setdefault((kind, n), []).append(o.reshape(a[n].shape[1:]))
    for key in list(res):
        res[key] = jnp.stack(res[key])

    stackg = lambda n: jnp.stack([pl_[0][n] for pl_ in per_layer])
    dmod = jnp.stack([pl_[1] for pl_ in per_layer])
    dlb = jnp.stack([pl_[2] for pl_ in per_layer])
    pk_g = [dmod if n == "b_ada" else dnf if n == "norm_final" else stackg(n) for n in PACKET]
    extra = [dlb, stackg("ssm_conv_w"), stackg("gdn_conv_w"), loss[0, :1]]
    pk_shapes = [t.shape for t in pk_g + extra]
    zeros = [jnp.zeros(t.shape, F32) for t in extra]
    (parts,) = exchange("gather_small", [_pack(pk_g + extra, F32, 8)], True)
    outs = adamw_sum("adamw_small", parts, *[_pack([a[p + n] for n in PACKET] + zeros, F32, 8) for p in ("", "m_", "v_")])
    for kind, o in zip(("grad", "delta", "new_m", "new_v"), outs):
        un = _unpack(o, pk_shapes)
        for n, t in zip(PACKET, un):
            res[(kind, n)] = t.reshape(a[n].shape)
        if kind == "grad":
            dlb_sum, g_scw, g_gcw, loss_sum = un[len(PACKET):]

    (g_lb,), _ = rowstage_bwd("lower_bounds_b", lower_bounds_fn, [(a["hgrn_lb_logits"], dm.mix, 0)], [], [dlb_sum], [F32], nl)
    mine = lambda t, n: lax.dynamic_slice_in_dim(t, me * a[n].shape[-1], a[n].shape[-1], axis=t.ndim - 1)
    (dmod_cols,) = exchange("a2a_dmod", [dmod.reshape(nl, N_DEV, ncol).transpose(1, 0, 2)], False)
    dmod_pad = jnp.zeros((nl, LANES, ncol), F32).at[:, :N_DEV].set(dmod_cols.transpose(1, 0, 2))
    g_w_ada = ada_bwd("ada_bwd", c_pad, dmod_pad)
    outs = adamw_sum("adamw_w_ada", g_w_ada.reshape(1, nl * d, ncol), *[a[q + "w_ada"].reshape(nl * d, ncol) for q in ("", "m_", "v_")])
    for kind, o in zip(("grad", "delta", "new_m", "new_v"), outs):
        res[(kind, "w_ada")] = o.reshape(nl, d, ncol)
    g_misc = [g_lb, mine(g_scw, "ssm_conv_w"), mine(g_gcw, "gdn_conv_w")]
    outs = adamw_sum("adamw_misc", _pack(g_misc, F32, 8)[None], *[_pack([a[q + n] for n in MISC], F32, 8) for q in ("", "m_", "v_")])
    for kind, o in zip(("grad", "delta", "new_m", "new_v"), outs):
        for n, t in zip(MISC, _unpack(o, [a[n].shape for n in MISC])):
            res[(kind, n)] = t

    out = [loss_sum.reshape(()), dx[None]]
    for kind in ("grad", "delta", "new_m", "new_v"):
        out += [res[(kind, n)] for n in WEIGHTS]
    return tuple(out)
```

```python
import functools
import math

import numpy as np
import jax
import jax.numpy as jnp
from jax import lax
from jax.experimental import pallas as pl
from jax.experimental.pallas import tpu as pltpu

F32 = jnp.float32
BF16 = jnp.bfloat16

N_DEV = 8
CHUNK = 64
SUB = 16
HEADS_PER_STEP = 2
HEAD = 128
SSM_P = 64
CONV_K = 4
F_MIN = 1e-30
NORM_EPS = 1e-6
LANES = 128
VMEM_LIMIT = 56 * 1024 * 1024

ADAM_LR = 0.001
ADAM_B1 = 0.9
ADAM_B2 = 0.999
ADAM_EPS = 1e-08
ADAM_WD = 0.01
ADAM_STEP = 10


def _dg(a, b, ca, cb):
    return lax.dot_general(a.astype(BF16), b.astype(BF16), (((ca,), (cb,)), ((), ())),
                           preferred_element_type=F32)


def _split3(x):
    x1 = x.astype(BF16)
    r = x - x1.astype(F32)
    x2 = r.astype(BF16)
    x3 = (r - x2.astype(F32)).astype(BF16)
    return x1, x2, x3


def _hdg(a, b, ca, cb):
    a1, a2, _ = _split3(a)
    b1, b2, _ = _split3(b)
    dn = (((ca,), (cb,)), ((), ()))
    d = lambda p, q: lax.dot_general(p, q, dn, preferred_element_type=F32)
    return (d(a2, b1) + d(a1, b2)) + d(a1, b1)


def _dot_family(prim):
    @jax.custom_vjp
    def nn(a, b):
        return prim(a, b, 1, 0)

    @jax.custom_vjp
    def nt(a, b):
        return prim(a, b, 1, 1)

    @jax.custom_vjp
    def tn(a, b):
        return prim(a, b, 0, 0)

    nn.defvjp(lambda a, b: (nn(a, b), (a, b)), lambda r, g: (nt(g, r[1]), tn(r[0], g)))
    nt.defvjp(lambda a, b: (nt(a, b), (a, b)), lambda r, g: (nn(g, r[1]), tn(g, r[0])))
    tn.defvjp(lambda a, b: (tn(a, b), (a, b)), lambda r, g: (nt(r[1], g), nn(r[0], g)))
    return nn, nt, tn


mm_nn, mm_nt, mm_tn = _dot_family(_dg)
hd_nn, hd_nt, hd_tn = _dot_family(_hdg)


def _iota(shape, dim):
    return lax.broadcasted_iota(jnp.int32, shape, dim)


def _scan_rows(x, reverse):
    n = x.shape[0]
    rows = _iota(x.shape, 0)
    k = 1
    while k < n:
        if reverse:
            x = x + jnp.where(rows < n - k, pltpu.roll(x, n - k, 0), 0.0)
        else:
            x = x + jnp.where(rows >= k, pltpu.roll(x, k, 0), 0.0)
        k *= 2
    return x


@jax.custom_vjp
def cumsum_rows(x):
    return _scan_rows(x, False)


cumsum_rows.defvjp(lambda x: (_scan_rows(x, False), None), lambda _, g: (_scan_rows(g, True),))


def _sigmoid(x):
    return jax.nn.sigmoid(x)


def _silu(x):
    return x * jax.nn.sigmoid(x)


def _softplus(x):
    e = jnp.exp(-jnp.abs(x))
    small = e * (1.0 - e * (0.5 - e * (1.0 / 3.0)))
    return jnp.maximum(x, 0.0) + jnp.where(e < 1e-3, small, jnp.log(1.0 + e))


def _masked_exp(diff, mask):
    return jnp.where(mask, jnp.exp(jnp.where(mask, diff, 0.0)), 0.0)


def _rms(x, w):
    return x * lax.rsqrt(jnp.mean(x * x, axis=-1, keepdims=True) + NORM_EPS) * w


def _cum_col_row(lg_col, lg_row):
    c = lg_col.shape[0]
    r, s = _iota((c, c), 0), _iota((c, c), 1)
    cum_col = jnp.sum(jnp.where(s <= r, jnp.broadcast_to(lg_row, (c, c)), 0.0), axis=1, keepdims=True)
    cum_row = jnp.sum(jnp.where(r <= s, jnp.broadcast_to(lg_col, (c, c)), 0.0), axis=0, keepdims=True)
    total = jnp.sum(lg_col, axis=0, keepdims=True)
    return cum_col, cum_row, total


def hgrn_chunk(seq, hp, sp, st):
    (blk,), (lb,), (nw,) = seq, hp, sp
    c = blk.shape[0]
    q_raw, f_raw, v, g_raw = (blk[:, i * HEAD:(i + 1) * HEAD] for i in range(4))
    q = _silu(q_raw)
    f = lb + (1.0 - lb) * _sigmoid(f_raw)
    logf = jnp.log(jnp.maximum(f, F_MIN))
    k = (1.0 - lb) * _sigmoid(-f_raw)
    b = cumsum_rows(logf)
    o_inter = mm_nt(q * jnp.exp(b), st)
    nsub = c // SUB
    wide = (SUB, SUB, HEAD)
    er = _iota((SUB * SUB, SUB), 0)
    e_t = (er // SUB == _iota((SUB * SUB, SUB), 1)).astype(F32)
    pr = _iota((SUB * SUB, 1), 0)
    pmask = (pr % SUB) <= (pr // SUB)
    er64 = _iota((SUB * SUB, c), 0)
    ec64 = _iota((SUB * SUB, c), 1)
    rows_c = _iota((c, 1), 0)
    row = lambda a, i: jnp.sum(jnp.where(rows_c == i, a, 0.0), axis=0, keepdims=True)
    parts = []
    for i in range(nsub):
        sl = slice(SUB * i, SUB * (i + 1))
        qi, ki, bi = q[sl], k[sl], b[sl]
        qb = jnp.broadcast_to(qi[:, None, :], wide).reshape(SUB * SUB, HEAD)
        kb = jnp.broadcast_to(ki[None, :, :], wide).reshape(SUB * SUB, HEAD)
        bd = (bi[:, None, :] - bi[None, :, :]).reshape(SUB * SUB, HEAD)
        sc_col = jnp.sum(qb * kb * _masked_exp(bd, pmask), axis=1, keepdims=True)
        place = (ec64 == (er64 % SUB) + SUB * i).astype(F32)
        sc = mm_tn(e_t, sc_col * place)
        if i > 0:
            bref = row(b, SUB * i)
            qt = qi * jnp.exp(bi - bref)
            kt = k * _masked_exp(bref - b, rows_c < SUB * i)
            sc = sc + mm_nt(qt, kt)
        parts.append(mm_nn(sc, v))
    o = o_inter + jnp.concatenate(parts, axis=0)
    bend = row(b, c - 1)
    st_new = st * jnp.exp(bend) + mm_tn(v, k * jnp.exp(bend - b))
    y = _rms(o, nw) * _silu(g_raw)
    return (y,), st_new


def ssd_chunk(seq, hp, sp, st):
    xs, bm, cm, dtc, dtr = seq
    dt_bias, a_log = hp
    c = xs.shape[0]
    lane = _iota((1, 2 * SSM_P), 1)
    first = lane < SSM_P
    r, s = _iota((c, c), 0), _iota((c, c), 1)
    g = mm_nt(cm, bm)
    dts, cums, ends, segs = [], [], [], []
    for i in range(2):
        neg_a = -jnp.exp(a_log[i])
        dt_col = _softplus(dtc[i] + dt_bias[i])
        dt_row = _softplus(dtr[i] + dt_bias[i])
        cum_col, cum_row, total = _cum_col_row(neg_a * dt_col, neg_a * dt_row)
        dts.append(dt_col)
        cums.append(cum_col)
        ends.append(total)
        segs.append(_masked_exp(cum_col - cum_row, s <= r))
    dt_l = jnp.where(first, dts[0], dts[1])
    cum_l = jnp.where(first, cums[0], cums[1])
    end_l = jnp.where(first, ends[0], ends[1])
    xdt = xs * dt_l
    y_intra = (mm_nn(g * segs[0], jnp.where(first, xdt, 0.0))
               + mm_nn(g * segs[1], jnp.where(first, 0.0, xdt)))
    y_inter = mm_nn(cm, st) * jnp.exp(cum_l)
    st_new = st * jnp.exp(end_l) + mm_tn(bm, xdt * jnp.exp(end_l - cum_l))
    return (y_intra + y_inter,), st_new


def _neumann_inverse(a):
    n = a.shape[0]
    eye = (_iota((n, n), 0) == _iota((n, n), 1)).astype(F32)
    p = -a
    t = eye + p
    for _ in range(int(math.log2(n)) - 1):
        p = _hdg(p, p, 1, 0)
        t = t + _hdg(t, p, 1, 0)
    return t


@jax.custom_vjp
def inv_unit_lower(a):
    return _neumann_inverse(a)


def _inv_fwd(a):
    t = _neumann_inverse(a)
    return t, t


inv_unit_lower.defvjp(_inv_fwd, lambda t, g: (-hd_nt(hd_tn(t, g), t),))


def gdn_chunk(seq, hp, sp, st):
    q_raw, k_raw, v, z, gbc, gac, gar = seq
    dt_bias, a_log = hp
    (nw,) = sp
    c = v.shape[0]
    r, s = _iota((c, c), 0), _iota((c, c), 1)
    q = q_raw * lax.rsqrt(jnp.sum(q_raw * q_raw, axis=-1, keepdims=True) + NORM_EPS) * (HEAD ** -0.5)
    k = k_raw * lax.rsqrt(jnp.sum(k_raw * k_raw, axis=-1, keepdims=True) + NORM_EPS)
    beta = _sigmoid(gbc)
    neg_a = -jnp.exp(a_log)
    cum, cum_row, total = _cum_col_row(neg_a * _softplus(gac + dt_bias), neg_a * _softplus(gar + dt_bias))
    decay = _masked_exp(cum - cum_row, s <= r)
    kk = mm_nt(k, k)
    a_low = jnp.where(s < r, beta * kk * decay, 0.0)
    sol = hd_nn(inv_unit_lower(a_low), jnp.concatenate([v * beta, k * (beta * jnp.exp(cum))], axis=1))
    u_base, w_corr = sol[:, :HEAD], sol[:, HEAD:]
    qk = mm_nt(q, k) * decay
    u = u_base - mm_nn(w_corr, st)
    o = mm_nn(q * jnp.exp(cum), st) + mm_nn(qk, u)
    st_new = jnp.exp(total) * st + mm_tn(k * jnp.exp(total - cum), u)
    y = _rms(o, nw) * _silu(z)
    return (y,), st_new


def normmod_fn(rows, params):
    (x,), (nw, sc, sh) = rows, params
    return (_rms(x, nw) * (1.0 + sc) + sh,)


def ssmpost_fn(rows, params):
    (y, xs, z), (d_exp, nw) = rows, params
    y = (y + d_exp * xs) * _silu(z)
    gw = y.shape[1] // 2
    return (jnp.concatenate([_rms(y[:, :gw], nw[:, :gw]), _rms(y[:, gw:], nw[:, gw:])], axis=1),)


def merge_fn(rows, params):
    (yh, ys, yg, gl), (bm, wb) = rows, params
    d = wb.shape[2]
    gates = _sigmoid(gl + bm)
    out = 0.0
    for n, y in enumerate((yh, ys, yg)):
        out = out + gates[:, n * d:(n + 1) * d] * mm_nn(y, wb[n])
    return (out,)


def outproj_fn(rows, params):
    (m, x), (g1, w) = rows, params
    return (x + (1.0 + g1) * mm_nn(m, w),)


def resid_fn(rows, params):
    (x, o), (g2,) = rows, params
    return (x + (1.0 + g2) * o,)


def _params(sem):
    return pltpu.CompilerParams(dimension_semantics=sem, vmem_limit_bytes=VMEM_LIMIT)


def _whole(a):
    nd = a.ndim
    return pl.BlockSpec(a.shape, lambda *_: (0,) * nd)


def _pick(n, cands):
    for c in cands:
        if n % c == 0:
            return c
    return n


def matmul(name, a, b, mode, out_dtype):
    if mode == "nn":
        (m, k), n = a.shape, b.shape[1]
    elif mode == "nt":
        (m, k), n = a.shape, b.shape[0]
    else:
        (k, m), n = a.shape, b.shape[1]
    tm = _pick(m, (512, 256, 128))
    tn = _pick(n, (1280, 1024, 1408, 768, 512, 384, 256, 128))
    tk = _pick(k, (1024, 1280, 1408, 768, 512, 256, 128))
    if mode == "tn":
        tm = _pick(m, (1024, 768, 512, 256, 128))
        tk = _pick(k, (512, 256, 128))
    nk = k // tk
    ca, cb = {"nn": (1, 0), "nt": (1, 1), "tn": (0, 0)}[mode]

    def body(a_ref, b_ref, o_ref, acc_ref):
        kk = pl.program_id(2)

        @pl.when(kk == 0)
        def _():
            acc_ref[...] = jnp.zeros_like(acc_ref)

        acc_ref[...] += _dg(a_ref[...], b_ref[...], ca, cb)

        @pl.when(kk == nk - 1)
        def _():
            o_ref[...] = acc_ref[...].astype(o_ref.dtype)

    a_spec = (pl.BlockSpec((tk, tm), lambda i, j, q: (q, i)) if mode == "tn"
              else pl.BlockSpec((tm, tk), lambda i, j, q: (i, q)))
    b_spec = (pl.BlockSpec((tn, tk), lambda i, j, q: (j, q)) if mode == "nt"
              else pl.BlockSpec((tk, tn), lambda i, j, q: (q, j)))
    return pl.pallas_call(
        body, name=name, grid=(m // tm, n // tn, nk),
        in_specs=[a_spec, b_spec],
        out_specs=pl.BlockSpec((tm, tn), lambda i, j, q: (i, j)),
        out_shape=jax.ShapeDtypeStruct((m, n), out_dtype),
        scratch_shapes=[pltpu.VMEM((tm, tn), F32)],
        compiler_params=_params(("parallel", "parallel", "arbitrary")),
    )(a, b)


def bmatmul(name, a, b, mode, out_dtype, out_batched):
    ab, bb = a.ndim == 3, b.ndim == 3
    nb = a.shape[0] if ab else b.shape[0]
    a2, b2 = a.shape[-2:], b.shape[-2:]
    if mode == "nn":
        (m, k), n = a2, b2[1]
    elif mode == "nt":
        (m, k), n = a2, b2[0]
    else:
        (k, m), n = a2, b2[1]
    tm = _pick(m, (1024, 512, 256, 128) if mode == "tn" else (512, 256, 128))
    tn = _pick(n, (1024, 512, 256, 128))
    tk = _pick(k, (512, 256, 128) if mode == "tn" else (1024, 512, 256, 128))
    nk = k // tk
    ca, cb = {"nn": (1, 0), "nt": (1, 1), "tn": (0, 0)}[mode]
    ids = (lambda g: g) if out_batched else (lambda g: (g[2], g[0], g[1], g[3]))
    grid = (nb, m // tm, n // tn, nk) if out_batched else (m // tm, n // tn, nb, nk)

    def a_map(*g):
        bi, i, j, q = ids(g)
        idx = (q, i) if mode == "tn" else (i, q)
        return (bi,) + idx if ab else idx

    def b_map(*g):
        bi, i, j, q = ids(g)
        idx = (j, q) if mode == "nt" else (q, j)
        return (bi,) + idx if bb else idx

    def o_map(*g):
        bi, i, j, q = ids(g)
        return (bi, i, j) if out_batched else (i, j)

    def body(a_ref, b_ref, o_ref, acc_ref):
        bi, _, _, q = ids(tuple(pl.program_id(d) for d in range(4)))
        first = (q == 0) if out_batched else (q == 0) & (bi == 0)
        last = (q == nk - 1) if out_batched else (q == nk - 1) & (bi == nb - 1)

        @pl.when(first)
        def _():
            acc_ref[...] = jnp.zeros_like(acc_ref)

        acc_ref[...] += _dg(a_ref[...], b_ref[...], ca, cb)

        @pl.when(last)
        def _():
            o_ref[...] = acc_ref[...].astype(o_ref.dtype)

    a_blk = (tk, tm) if mode == "tn" else (tm, tk)
    b_blk = (tn, tk) if mode == "nt" else (tk, tn)
    return pl.pallas_call(
        body, name=name, grid=grid,
        in_specs=[pl.BlockSpec(((None,) if ab else ()) + a_blk, a_map), pl.BlockSpec(((None,) if bb else ()) + b_blk, b_map)],
        out_specs=pl.BlockSpec(((None,) if out_batched else ()) + (tm, tn), o_map),
        out_shape=jax.ShapeDtypeStruct(((nb,) if out_batched else ()) + (m, n), out_dtype),
        scratch_shapes=[pltpu.VMEM((tm, tn), F32)],
        compiler_params=_params(("parallel", "parallel", "arbitrary", "arbitrary")),
    )(a, b)


def colgather(name, src, idx, dst_w, out_dtype):
    nsrc, rows, w = src.shape
    nbs = -(-w // LANES)
    ne = idx.shape[0]
    nbd = idx.shape[1] // LANES
    tiles = [sorted(set((idx[e, t * LANES:(t + 1) * LANES][idx[e, t * LANES:(t + 1) * LANES] >= 0] // LANES).tolist()))
             for e in range(ne) for t in range(nbd)]
    nslot = max(1, max(len(t) for t in tiles))
    tbl = np.full((ne * nbd, nslot), -1, np.int32)
    for i, t in enumerate(tiles):
        tbl[i, :len(t)] = t
    exact3 = src.dtype == F32

    def body(tbl_ref, idx_ref, src_ref, o_ref, acc_ref):
        ti, si = pl.program_id(0), pl.program_id(1)

        @pl.when(si == 0)
        def _():
            acc_ref[...] = jnp.zeros_like(acc_ref)

        t = tbl_ref[ti * nslot + si]

        @pl.when(t >= 0)
        def _():
            onehot = ((_iota((LANES, LANES), 0) + t * LANES) == idx_ref[...]).astype(BF16)
            col = _iota((1, LANES), 1) + (t % nbs) * LANES
            xv = jnp.where(col < w, src_ref[...], jnp.zeros((), src_ref.dtype))
            d = lambda p: lax.dot_general(p, onehot, (((1,), (0,)), ((), ())), preferred_element_type=F32)
            if exact3:
                x1, x2, x3 = _split3(xv)
                acc_ref[...] += (d(x3) + d(x2)) + d(x1)
            else:
                acc_ref[...] += d(xv)

        @pl.when(si == nslot - 1)
        def _():
            o_ref[...] = acc_ref[...].astype(o_ref.dtype)

    def src_map(ti, si, tbl_ref):
        t = jnp.maximum(tbl_ref[ti * nslot + si], 0)
        return (t // nbs, 0, t % nbs)

    grid_spec = pltpu.PrefetchScalarGridSpec(
        num_scalar_prefetch=1, grid=(ne * nbd, nslot),
        in_specs=[pl.BlockSpec((None, 1, LANES), lambda ti, si, tbl_ref: (ti // nbd, 0, ti % nbd)),
                  pl.BlockSpec((None, rows, LANES), src_map)],
        out_specs=pl.BlockSpec((None, rows, LANES), lambda ti, si, tbl_ref: (ti // nbd, 0, ti % nbd)),
        scratch_shapes=[pltpu.VMEM((rows, LANES), F32)])
    return pl.pallas_call(
        body, name=name, grid_spec=grid_spec,
        out_shape=jax.ShapeDtypeStruct((ne, rows, dst_w), out_dtype),
        compiler_params=_params(("parallel", "arbitrary")),
    )(jnp.asarray(tbl.reshape(-1)), jnp.asarray(idx.reshape(ne, 1, nbd * LANES).astype(np.int32)), src)


def swiglu3_fwd(name, gu, tm):
    _, nb, s, w = gu.shape

    def body(x_ref, o_ref):
        o_ref[...] = (_silu(x_ref[0]) * x_ref[1]).astype(o_ref.dtype)

    return pl.pallas_call(
        body, name=name, grid=(nb, s // tm),
        in_specs=[pl.BlockSpec((2, None, tm, w), lambda b, i: (0, b, i, 0))],
        out_specs=pl.BlockSpec((None, tm, w), lambda b, i: (b, i, 0)),
        out_shape=jax.ShapeDtypeStruct((nb, s, w), BF16),
        compiler_params=_params(("parallel", "parallel")),
    )(gu)


def swiglu3_bwd(name, gu, dact, tm):
    _, nb, s, w = gu.shape

    def body(x_ref, g_ref, o_ref):
        _, vjp = jax.vjp(lambda a, b: _silu(a) * b, x_ref[0], x_ref[1])
        dg, du = vjp(g_ref[...].astype(F32))
        o_ref[0] = dg.astype(o_ref.dtype)
        o_ref[1] = du.astype(o_ref.dtype)

    return pl.pallas_call(
        body, name=name, grid=(nb, s // tm),
        in_specs=[pl.BlockSpec((2, None, tm, w), lambda b, i: (0, b, i, 0)),
                  pl.BlockSpec((None, tm, w), lambda b, i: (b, i, 0))],
        out_specs=pl.BlockSpec((2, None, tm, w), lambda b, i: (0, b, i, 0)),
        out_shape=jax.ShapeDtypeStruct(gu.shape, BF16),
        compiler_params=_params(("parallel", "parallel")),
    )(gu, dact)


def _row_specs(rows, tm):
    return [pl.BlockSpec((tm, w), lambda i, _c=c: (i, _c)) for (_, w, c) in rows]


def rowstage_fwd(name, fn, rows, params, outs, tm):
    s = rows[0][0].shape[0]
    nr, npar = len(rows), len(params)

    def body(*refs):
        r = [x[...].astype(F32) for x in refs[:nr]]
        p = [x[...].astype(F32) for x in refs[nr:nr + npar]]
        for ref, val in zip(refs[nr + npar:], fn(r, p)):
            ref[...] = val.astype(ref.dtype)

    res = pl.pallas_call(
        body, name=name, grid=(s // tm,),
        in_specs=_row_specs(rows, tm) + [_whole(p) for p in params],
        out_specs=[pl.BlockSpec((tm, w), lambda i: (i, 0)) for (w, _) in outs],
        out_shape=[jax.ShapeDtypeStruct((s, w), dt) for (w, dt) in outs],
        compiler_params=_params(("parallel",)),
    )(*[r[0] for r in rows], *params)
    return res


def rowstage_bwd(name, fn, rows, params, douts, drow_dtypes, tm, adds=None):
    s = rows[0][0].shape[0]
    nr, npar, no = len(rows), len(params), len(douts)
    adds = adds or {}
    add_idx = sorted(adds)
    na = len(add_idx)

    def body(*refs):
        r = [x[...].astype(F32) for x in refs[:nr]]
        p = [x[...].astype(F32) for x in refs[nr:nr + npar]]
        g = [x[...].astype(F32) for x in refs[nr + npar:nr + npar + no]]
        a_refs = refs[nr + npar + no:nr + npar + no + na]
        dr_refs = refs[nr + npar + no + na:nr + npar + no + na + nr]
        dp_refs = refs[nr + npar + no + na + nr:]
        _, vjp = jax.vjp(lambda r_, p_: tuple(fn(r_, p_)), r, p)
        dr, dp = vjp(tuple(g))
        for j, (ref, val) in enumerate(zip(dr_refs, dr)):
            if j in adds:
                val = val + a_refs[add_idx.index(j)][...].astype(F32)
            ref[...] = val.astype(ref.dtype)

        @pl.when(pl.program_id(0) == 0)
        def _():
            for ref in dp_refs:
                ref[...] = jnp.zeros_like(ref)

        for ref, val in zip(dp_refs, dp):
            ref[...] += val

    res = pl.pallas_call(
        body, name=name, grid=(s // tm,),
        in_specs=(_row_specs(rows, tm) + [_whole(p) for p in params]
                  + [pl.BlockSpec((tm, d.shape[1]), lambda i: (i, 0)) for d in douts]
                  + [pl.BlockSpec((tm, rows[j][1]), lambda i: (i, 0)) for j in add_idx]),
        out_specs=([pl.BlockSpec((tm, w), lambda i: (i, 0)) for (_, w, _) in rows] + [_whole(p) for p in params]),
        out_shape=([jax.ShapeDtypeStruct((s, w), dt) for (_, w, _), dt in zip(rows, drow_dtypes)]
                   + [jax.ShapeDtypeStruct(p.shape, F32) for p in params]),
        compiler_params=_params(("arbitrary",)),
    )(*[r[0] for r in rows], *params, *douts, *[adds[j] for j in add_idx])
    return res[:nr], res[nr:]


def _flip(index_map, nc):
    return lambda h, n: index_map(h, nc - 1 - n)


def _take(v, split, j):
    if split is None:
        return v
    if split[0] == "lane":
        return v[:, j * split[1]:(j + 1) * split[1]]
    if split[0] == "lead":
        return v[j * split[1]:(j + 1) * split[1]]
    return v[j]


def _where(split, j):
    if split[0] == "lane":
        return (slice(None), slice(j * split[1], (j + 1) * split[1]))
    if split[0] == "lead":
        return (slice(j * split[1], (j + 1) * split[1]),)
    return (j,)


def scan_fwd(name, chunk_fn, nblk, hb, nc, seqs, hparams, sparams, state_shape, outs):
    ns, nhp, nsp, no = len(seqs), len(hparams), len(sparams), len(outs)

    def body(*refs):
        seq_r, hp_r, sp_r = refs[:ns], refs[ns:ns + nhp], refs[ns + nhp:ns + nhp + nsp]
        out_r = refs[ns + nhp + nsp:ns + nhp + nsp + no]
        st_out, st_scr = refs[-2], refs[-1]

        @pl.when(pl.program_id(1) == 0)
        def _():
            st_scr[...] = jnp.zeros_like(st_scr)

        seq_v = [x[...].astype(F32) for x in seq_r]
        hp_v = [x[...] for x in hp_r]
        sp_v = [x[...] for x in sp_r]
        for j in range(hb):
            st = st_scr[j]
            st_out[j] = st
            o, st_new = chunk_fn([_take(v, s[3], j) for v, s in zip(seq_v, seqs)],
                                 [_take(v, s[3], j) for v, s in zip(hp_v, hparams)], sp_v, st)
            for ref, spec, val in zip(out_r, outs, o):
                ref[_where(spec[4], j)] = val.astype(ref.dtype)
            st_scr[j] = st_new

    nst = len(state_shape)
    res = pl.pallas_call(
        body, name=name, grid=(nblk, nc),
        in_specs=([pl.BlockSpec(bs, im) for (_, bs, im, _) in seqs]
                  + [pl.BlockSpec(bs, lambda h, n, _im=im: _im(h)) for (_, bs, im, _) in hparams]
                  + [_whole(p) for p in sparams]),
        out_specs=([pl.BlockSpec(bs, im) for (_, _, bs, im, _) in outs]
                   + [pl.BlockSpec((hb, None) + tuple(state_shape), lambda h, n: (h, n) + (0,) * nst)]),
        out_shape=([jax.ShapeDtypeStruct(fs, dt) for (fs, dt, _, _, _) in outs]
                   + [jax.ShapeDtypeStruct((nblk * hb, nc) + tuple(state_shape), F32)]),
        scratch_shapes=[pltpu.VMEM((hb,) + tuple(state_shape), F32)],
        compiler_params=_params(("parallel", "arbitrary")),
    )(*[x[0] for x in seqs], *[x[0] for x in hparams], *sparams)
    return res[:no], res[no]


def scan_bwd(name, chunk_fn, nblk, hb, nc, seqs, hparams, sparams, state_shape, states, douts, dseqs):
    ns, nhp, nsp, no = len(seqs), len(hparams), len(sparams), len(douts)
    nst = len(state_shape)

    def body(*refs):
        seq_r, hp_r, sp_r = refs[:ns], refs[ns:ns + nhp], refs[ns + nhp:ns + nhp + nsp]
        base = ns + nhp + nsp
        st_r = refs[base]
        do_r = refs[base + 1:base + 1 + no]
        base += 1 + no
        ds_r, dhp_r, dsp_r = refs[base:base + ns], refs[base + ns:base + ns + nhp], refs[base + ns + nhp:base + ns + nhp + nsp]
        dst_scr = refs[-1]
        h, n = pl.program_id(0), pl.program_id(1)

        @pl.when(n == 0)
        def _():
            dst_scr[...] = jnp.zeros_like(dst_scr)
            for ref in dhp_r:
                ref[...] = jnp.zeros_like(ref)

        @pl.when((n == 0) & (h == 0))
        def _():
            for ref in dsp_r:
                ref[...] = jnp.zeros_like(ref)

        seq_v = [x[...].astype(F32) for x in seq_r]
        hp_v = [x[...] for x in hp_r]
        sp_v = [x[...] for x in sp_r]
        do_v = [x[...].astype(F32) for x in do_r]
        shared = [None] * ns
        dsp_sum = [None] * nsp
        for j in range(hb):
            prim = ([_take(v, s[3], j) for v, s in zip(seq_v, seqs)],
                    [_take(v, s[3], j) for v, s in zip(hp_v, hparams)], sp_v, st_r[j])
            _, vjp = jax.vjp(lambda a, b, c, d: chunk_fn(a, b, c, d), *prim)
            cot = (tuple(_take(v, s[3], j) for v, s in zip(do_v, douts)), dst_scr[j])
            ds, dhp, dsp, dst = vjp(cot)
            for i, (ref, spec, val) in enumerate(zip(ds_r, dseqs, ds)):
                if spec[4] is None:
                    shared[i] = val if shared[i] is None else shared[i] + val
                else:
                    ref[_where(spec[4], j)] = val.astype(ref.dtype)
            for ref, spec, val in zip(dhp_r, hparams, dhp):
                ref[_where(spec[3], j)] += val
            dsp_sum = [val if acc is None else acc + val for acc, val in zip(dsp_sum, dsp)]
            dst_scr[j] = dst
        for ref, val in zip(ds_r, shared):
            if val is not None:
                ref[...] = val.astype(ref.dtype)
        for ref, val in zip(dsp_r, dsp_sum):
            ref[...] += val

    res = pl.pallas_call(
        body, name=name, grid=(nblk, nc),
        in_specs=([pl.BlockSpec(bs, _flip(im, nc)) for (_, bs, im, _) in seqs]
                  + [pl.BlockSpec(bs, lambda h, n, _im=im: _im(h)) for (_, bs, im, _) in hparams]
                  + [_whole(p) for p in sparams]
                  + [pl.BlockSpec((hb, None) + tuple(state_shape), lambda h, n: (h, nc - 1 - n) + (0,) * nst)]
                  + [pl.BlockSpec(bs, _flip(im, nc)) for (_, bs, im, _) in douts]),
        out_specs=([pl.BlockSpec(bs, _flip(im, nc)) for (_, _, bs, im, _) in dseqs]
                   + [pl.BlockSpec(bs, lambda h, n, _im=im: _im(h)) for (_, bs, im, _) in hparams]
                   + [_whole(p) for p in sparams]),
        out_shape=([jax.ShapeDtypeStruct(fs, dt) for (fs, dt, _, _, _) in dseqs]
                   + [jax.ShapeDtypeStruct(x[0].shape, F32) for x in hparams]
                   + [jax.ShapeDtypeStruct(p.shape, F32) for p in sparams]),
        scratch_shapes=[pltpu.VMEM((hb,) + tuple(state_shape), F32)],
        compiler_params=_params(("arbitrary", "arbitrary")),
    )(*[x[0] for x in seqs], *[x[0] for x in hparams], *sparams, states, *[x[0] for x in douts])
    return res[:ns], res[ns:ns + nhp], res[ns + nhp:]


def _shift_down(x, n, rows):
    if n == 0:
        return x
    return jnp.where(rows >= n, pltpu.roll(x, n, 0), 0.0)


def _shift_up(x, n, rows):
    if n == 0:
        return x
    s = x.shape[0]
    return jnp.where(rows < s - n, pltpu.roll(x, s - n, 0), 0.0)


def conv_fwd(name, x, col0, w, b):
    s, cw = x.shape[0], w.shape[1]

    def body(x_ref, w_ref, b_ref, o_ref):
        xv = x_ref[...]
        rows = _iota(xv.shape, 0)
        u = jnp.broadcast_to(b_ref[...], xv.shape)
        for j in range(CONV_K):
            u = u + w_ref[j:j + 1, :] * _shift_down(xv, CONV_K - 1 - j, rows)
        o_ref[...] = _silu(u)

    return pl.pallas_call(
        body, name=name, grid=(cw // LANES,),
        in_specs=[pl.BlockSpec((s, LANES), lambda j: (0, col0 + j)),
                  pl.BlockSpec((CONV_K, LANES), lambda j: (0, j)),
                  pl.BlockSpec((1, LANES), lambda j: (0, j))],
        out_specs=pl.BlockSpec((s, LANES), lambda j: (0, j)),
        out_shape=jax.ShapeDtypeStruct((s, cw), F32),
        compiler_params=_params(("parallel",)),
    )(x, w, b)


def conv_bwd(name, x, col0, w, b, dout):
    s, cw = x.shape[0], w.shape[1]

    def body(x_ref, w_ref, b_ref, g_ref, dx_ref, dw_ref, db_ref):
        xv = x_ref[...]
        rows = _iota(xv.shape, 0)
        sh = [_shift_down(xv, CONV_K - 1 - j, rows) for j in range(CONV_K)]
        u = jnp.broadcast_to(b_ref[...], xv.shape)
        for j in range(CONV_K):
            u = u + w_ref[j:j + 1, :] * sh[j]
        sg = _sigmoid(u)
        du = g_ref[...] * (sg * (1.0 + u * (1.0 - sg)))
        dx = jnp.zeros_like(xv)
        for j in range(CONV_K):
            dx = dx + w_ref[j:j + 1, :] * _shift_up(du, CONV_K - 1 - j, rows)
            dw_ref[j:j + 1, :] = jnp.sum(du * sh[j], axis=0, keepdims=True)
        dx_ref[...] = dx.astype(dx_ref.dtype)
        db_ref[...] = jnp.sum(du, axis=0, keepdims=True)

    return pl.pallas_call(
        body, name=name, grid=(cw // LANES,),
        in_specs=[pl.BlockSpec((s, LANES), lambda j: (0, col0 + j)),
                  pl.BlockSpec((CONV_K, LANES), lambda j: (0, j)),
                  pl.BlockSpec((1, LANES), lambda j: (0, j)),
                  pl.BlockSpec((s, LANES), lambda j: (0, j))],
        out_specs=[pl.BlockSpec((s, LANES), lambda j: (0, j)),
                   pl.BlockSpec((CONV_K, LANES), lambda j: (0, j)),
                   pl.BlockSpec((1, LANES), lambda j: (0, j))],
        out_shape=[jax.ShapeDtypeStruct((s, cw), BF16), jax.ShapeDtypeStruct((CONV_K, cw), F32),
                   jax.ShapeDtypeStruct((1, cw), F32)],
        compiler_params=_params(("parallel",)),
    )(x, w, b, dout)


def exchange(name, sends, broadcast):
    nop = len(sends)

    def body(*refs):
        send_refs, recv_refs = refs[:nop], refs[nop:2 * nop]
        send_sems, recv_sems, local_sems = refs[2 * nop:]
        x, y, c = lax.axis_index("x"), lax.axis_index("y"), lax.axis_index("c")
        me = 4 * x + 2 * y + c

        def src(i, peer):
            return send_refs[i] if broadcast else send_refs[i].at[peer]

        def remote(i, k, dev, peer, landing):
            return pltpu.make_async_remote_copy(
                src_ref=src(i, peer), dst_ref=recv_refs[i].at[landing], send_sem=send_sems.at[i * (N_DEV - 1) + k],
                recv_sem=recv_sems.at[i * (N_DEV - 1) + k], device_id=dev, device_id_type=pl.DeviceIdType.MESH)

        local = [pltpu.make_async_copy(src(i, me), recv_refs[i].at[me], local_sems.at[i]) for i in range(nop)]
        for cp in local:
            cp.start()
        peers = []
        for k in range(1, N_DEV):
            px = 1 - x if (k >> 2) & 1 else x
            py = 1 - y if (k >> 1) & 1 else y
            pc = 1 - c if k & 1 else c
            peers.append(((px, py, pc), 4 * px + 2 * py + pc))
        sent = []
        for k, (dev, peer) in enumerate(peers):
            for i in range(nop):
                cp = remote(i, k, dev, peer, me)
                cp.start()
                sent.append(cp)
        for k, (dev, peer) in enumerate(peers):
            for i in range(nop):
                remote(i, k, dev, peer, peer).wait_recv()
        for cp in sent:
            cp.wait_send()
        for cp in local:
            cp.wait()

    hbm = pl.BlockSpec(memory_space=pltpu.HBM)
    return pl.pallas_call(
        body, name=name,
        in_specs=[hbm] * nop, out_specs=[hbm] * nop,
        out_shape=[jax.ShapeDtypeStruct((N_DEV,) + tuple(t.shape if broadcast else t.shape[1:]), t.dtype) for t in sends],
        scratch_shapes=[pltpu.SemaphoreType.DMA((nop * (N_DEV - 1),)), pltpu.SemaphoreType.DMA((nop * (N_DEV - 1),)),
                        pltpu.SemaphoreType.DMA((nop,))],
        compiler_params=pltpu.CompilerParams(has_side_effects=True),
    )(*sends)


def adamw_sum(name, parts, w, m, v):
    rws, cols = w.shape
    nsum = parts.shape[0]
    tr = _pick(rws, (256, 128, 64, 32, 16, 8))
    c1 = 1.0 / (1.0 - ADAM_B1 ** ADAM_STEP)
    c2 = 1.0 / (1.0 - ADAM_B2 ** ADAM_STEP)

    def body(p_ref, w_ref, m_ref, v_ref, g_ref, d_ref, nm_ref, nv_ref):
        g = p_ref[0]
        for j in range(1, nsum):
            g = g + p_ref[j]
        nm = ADAM_B1 * m_ref[...] + (1.0 - ADAM_B1) * g
        nv = ADAM_B2 * v_ref[...] + (1.0 - ADAM_B2) * (g * g)
        g_ref[...] = g
        nm_ref[...] = nm
        nv_ref[...] = nv
        d_ref[...] = -ADAM_LR * ((nm * c1) / (jnp.sqrt(nv * c2) + ADAM_EPS) + ADAM_WD * w_ref[...])

    blk = pl.BlockSpec((tr, cols), lambda i: (i, 0))
    return pl.pallas_call(
        body, name=name, grid=(rws // tr,),
        in_specs=[pl.BlockSpec((nsum, tr, cols), lambda i: (0, i, 0)), blk, blk, blk],
        out_specs=[blk, blk, blk, blk],
        out_shape=[jax.ShapeDtypeStruct(w.shape, F32)] * 4,
        compiler_params=_params(("parallel",)),
    )(parts, w, m, v)


def ada_fwd(name, c_all, w, b):
    nl = w.shape[0]

    def body(c_ref, w_ref, b_ref, o_ref):
        ca = _silu(c_ref[...])
        for l in range(nl):
            o_ref[l] = mm_nn(ca, w_ref[l]) + b_ref[l]

    return pl.pallas_call(
        body, name=name,
        out_shape=jax.ShapeDtypeStruct((nl, c_all.shape[0], w.shape[2]), F32),
        compiler_params=pltpu.CompilerParams(vmem_limit_bytes=VMEM_LIMIT),
    )(c_all, w, b)


def ada_bwd(name, c_all, dmod):
    nl = dmod.shape[0]

    def body(c_ref, g_ref, o_ref):
        ca = _silu(c_ref[...])
        for l in range(nl):
            o_ref[l] = mm_tn(ca, g_ref[l])

    return pl.pallas_call(
        body, name=name,
        out_shape=jax.ShapeDtypeStruct((nl, c_all.shape[1], dmod.shape[2]), F32),
        compiler_params=pltpu.CompilerParams(vmem_limit_bytes=VMEM_LIMIT),
    )(c_all, dmod)


def lower_bounds_fn(rows, params):
    (lg,), _ = rows, params
    nl = lg.shape[0]
    mx = jnp.max(lg, axis=0, keepdims=True)
    e = jnp.exp(lg - mx)
    p = e / jnp.sum(e, axis=0, keepdims=True)
    layer = _iota((nl, 1), 0)
    acc = jnp.zeros_like(p)
    for j in range(1, nl):
        pj = jnp.sum(jnp.where(layer == j, p, 0.0), axis=0, keepdims=True)
        acc = acc + jnp.where(layer >= j, 1.0, 0.0) * pj
    return (acc,)


def loss_call(name, x, tgt, nw, tm):
    s, d = x.shape

    def body(x_ref, t_ref, w_ref, l_ref, dx_ref, dw_ref):
        def f(xv, wv):
            err = _rms(xv, wv) - t_ref[...]
            return jnp.sum(0.5 * jnp.mean(err * err, axis=-1, keepdims=True), axis=0, keepdims=True)

        val, vjp = jax.vjp(f, x_ref[...], w_ref[...])
        dx, dw = vjp(jnp.ones_like(val))

        @pl.when(pl.program_id(0) == 0)
        def _():
            l_ref[...] = jnp.zeros_like(l_ref)
            dw_ref[...] = jnp.zeros_like(dw_ref)

        l_ref[...] += jnp.broadcast_to(val, l_ref.shape)
        dw_ref[...] += dw
        dx_ref[...] = dx

    row = pl.BlockSpec((tm, d), lambda i: (i, 0))
    return pl.pallas_call(
        body, name=name, grid=(s // tm,),
        in_specs=[row, row, _whole(nw)],
        out_specs=[pl.BlockSpec((8, LANES), lambda i: (0, 0)), row, _whole(nw)],
        out_shape=[jax.ShapeDtypeStruct((8, LANES), F32), jax.ShapeDtypeStruct((s, d), F32),
                   jax.ShapeDtypeStruct(nw.shape, F32)],
        compiler_params=_params(("arbitrary",)),
    )(x, tgt, nw)


class Dims:
    def __init__(self, s, d, ffn):
        self.s, self.d, self.ffn = s, d, ffn
        self.mix = 3 * d // 4
        self.nh = self.mix // HEAD
        self.ssm_heads = self.mix // SSM_P
        self.pairs = self.mix // (2 * SSM_P)
        self.nc = s // CHUNK
        self.conv_ssm = self.mix + 4 * HEAD
        self.conv_w = self.conv_ssm + 3 * self.mix
        self.o_gates = 4 * self.mix
        self.o_sz = self.o_gates + 3 * d
        self.o_gz = self.o_sz + self.mix
        self.o_conv = self.o_gz + self.mix
        self.o_small = self.o_conv + self.conv_w
        used = self.o_small + LANES
        self.np = -(-used // 1280) * 1280
        self.tm = _pick(s, (256, 128, 64))
        mix, nh = self.mix, self.nh
        self.in_sizes = (mix, mix, mix, mix, mix, self.conv_ssm, self.ssm_heads, 3 * mix, mix, nh, nh, 3 * d)
        self.in_width = sum(self.in_sizes)


def w_in_tables(dm, nshard):
    off = np.cumsum((0,) + dm.in_sizes)
    hq, hf, hi, hg, sz, sxbc, sdt, gqkv, gz, gb, ga, gates = (np.arange(off[i], off[i + 1]) for i in range(12))
    hgrn = np.stack([t.reshape(dm.nh, HEAD) for t in (hq, hf, hi, hg)], axis=1).reshape(-1)
    perm = np.concatenate([hgrn, gates, sz, gz, sxbc, gqkv, sdt, gb, ga])
    perm = np.concatenate([perm, np.full(dm.np - perm.size, -1)])
    shard = dm.in_width // nshard
    wpad = -(-shard // LANES) * LANES
    fwd = np.where(perm >= 0, (perm // shard) * wpad + perm % shard, -1)[None]
    inv = np.zeros(dm.in_width, np.int64)
    inv[perm[perm >= 0]] = np.nonzero(perm >= 0)[0]
    bwd = np.full((nshard, wpad), -1)
    bwd[:, :shard] = inv.reshape(nshard, shard)
    return fwd.astype(np.int32), bwd.astype(np.int32)


def _small_views(dm, small):
    t = small.T
    col = lambda a: a[:, :, None]
    row = lambda a: a.reshape(a.shape[0], dm.nc, 1, CHUNK)
    a, b = dm.ssm_heads, dm.ssm_heads + dm.nh
    sdt, gb, ga = t[:a], t[a:b], t[b:b + dm.nh]
    return col(sdt), row(sdt), col(gb), col(ga), row(ga)


def _scan_specs(dm, proj, conv_out, views, lp):
    dt_col, dt_row, gb_col, ga_col, ga_row = views
    mixb, nh = dm.mix // LANES, dm.nh
    s, mix = dm.s, dm.mix
    lane = ("lane", LANES)
    hb = HEADS_PER_STEP
    hw = (CHUNK, hb * LANES)
    hgrn = dict(
        nblk=nh // hb, hb=hb, fn=hgrn_chunk,
        seqs=[(proj, (CHUNK, hb * 4 * HEAD), lambda h, n: (n, h), ("lane", 4 * HEAD))],
        hparams=[(lp["lb"], (1, hb * HEAD), lambda h: (0, h), lane)],
        sparams=[lp["hgrn_norm"]],
        dseqs=[((s, 4 * mix), BF16, (CHUNK, hb * 4 * HEAD), lambda h, n: (n, h), ("lane", 4 * HEAD))],
        io=(hw, lambda h, n: (n, h), lane))
    ppg = dm.pairs // 2
    gw = (CHUNK, ppg * LANES)
    pcol = ((2 * ppg, CHUNK, 1), lambda g, n: (g, n, 0), ("lead", 2))
    prow = ((2 * ppg, None, 1, CHUNK), lambda g, n: (g, n, 0, 0), ("lead", 2))
    ppar = ((2 * ppg, 1, 1), lambda g: (g, 0, 0), ("lead", 2))
    bc = lambda first: ((CHUNK, LANES), lambda g, n: (n, first + g), None)
    ssd = dict(
        nblk=2, hb=ppg, fn=ssd_chunk,
        seqs=[(conv_out, gw, lambda g, n: (n, g), lane), (conv_out,) + bc(mixb), (conv_out,) + bc(mixb + 2),
              (dt_col,) + pcol, (dt_row,) + prow],
        hparams=[(lp["ssm_dt_bias"],) + ppar, (lp["ssm_a_log"],) + ppar],
        sparams=[],
        dseqs=[((s, mix), F32, gw, lambda g, n: (n, g), lane), ((s, 2 * LANES), F32) + bc(0), ((s, 2 * LANES), F32) + bc(0),
               (dt_col.shape, F32) + pcol, (dt_row.shape, F32) + prow],
        io=(gw, lambda g, n: (n, g), lane))
    cq, cgz = dm.conv_ssm // LANES, dm.o_gz // LANES
    assert cq % hb == 0 and nh % hb == 0 and cgz % hb == 0
    hcol = ((hb, CHUNK, 1), lambda h, n: (h, n, 0), ("idx",))
    hrow = ((hb, None, 1, CHUNK), lambda h, n: (h, n, 0, 0), ("idx",))
    hpar = ((hb, 1, 1), lambda h: (h, 0, 0), ("idx",))
    at = lambda first: (hw, lambda h, n: (n, first // hb + h), lane)
    gdn = dict(
        nblk=nh // hb, hb=hb, fn=gdn_chunk,
        seqs=[(conv_out,) + at(cq), (conv_out,) + at(cq + nh), (conv_out,) + at(cq + 2 * nh), (proj,) + at(cgz),
              (gb_col,) + hcol, (ga_col,) + hcol, (ga_row,) + hrow],
        hparams=[(lp["gdn_dt_bias"],) + hpar, (lp["gdn_a_log"],) + hpar],
        sparams=[lp["gdn_norm"]],
        dseqs=[((s, mix), F32) + at(0), ((s, mix), F32) + at(0), ((s, mix), F32) + at(0), ((s, mix), BF16) + at(0),
               (gb_col.shape, F32) + hcol, (ga_col.shape, F32) + hcol, (ga_row.shape, F32) + hrow],
        io=(hw, lambda h, n: (n, h), lane))
    return hgrn, ssd, gdn


def _run_scan_fwd(dm, name, sp):
    out = ((dm.s, dm.mix), F32) + sp["io"]
    (y,), states = scan_fwd(name, sp["fn"], sp["nblk"], sp["hb"], dm.nc, sp["seqs"], sp["hparams"], sp["sparams"],
                            (HEAD, HEAD), [out])
    return y, states


def _run_scan_bwd(dm, name, sp, states, dy):
    return scan_bwd(name, sp["fn"], sp["nblk"], sp["hb"], dm.nc, sp["seqs"], sp["hparams"], sp["sparams"], (HEAD, HEAD),
                    states, [(dy,) + sp["io"]], sp["dseqs"])


def layer_fwd(dm, l, x, lp):
    tm, d, mix = dm.tm, dm.d, dm.mix
    tag = f"l{l}_"
    (h,) = rowstage_fwd(tag + "norm1", normmod_fn, [(x, d, 0)], [lp["norm_mix"], lp["sc1"], lp["sh1"]], [(d, BF16)], tm)
    proj = matmul(tag + "proj", h, lp["w_in"], "nn", F32)
    conv_out = conv_fwd(tag + "conv", proj, dm.o_conv // LANES, lp["conv_w"], lp["conv_b"])
    small = proj[:, dm.o_small:dm.o_small + LANES]
    views = _small_views(dm, small)
    hg, sd, gd = _scan_specs(dm, proj, conv_out, views, lp)
    yh, st_h = _run_scan_fwd(dm, tag + "hgrn", hg)
    y_ssd, st_s = _run_scan_fwd(dm, tag + "ssd", sd)
    yg, st_g = _run_scan_fwd(dm, tag + "gdn", gd)
    (ys,) = rowstage_fwd(tag + "ssmpost", ssmpost_fn,
                         [(y_ssd, mix, 0), (conv_out, mix, 0), (proj, mix, dm.o_sz // mix)],
                         [lp["ssm_d_exp"], lp["ssm_norm"]], [(mix, F32)], tm)
    (merged,) = rowstage_fwd(tag + "merge", merge_fn, [(yh, mix, 0), (ys, mix, 0), (yg, mix, 0), (proj, 3 * d, 1)],
                             [lp["b_merge"], lp["w_branch"]], [(d, BF16)], tm)
    (x1,) = rowstage_fwd(tag + "outproj", outproj_fn, [(merged, d, 0), (x, d, 0)], [lp["g1"], lp["w_out"]], [(d, F32)], tm)
    (h2,) = rowstage_fwd(tag + "norm2", normmod_fn, [(x1, d, 0)], [lp["norm_ffn"], lp["sc2"], lp["sh2"]], [(d, BF16)], tm)
    gu = bmatmul(tag + "ffn_in", h2, lp["w_ffn_in"], "nn", F32, True)
    gu = gu.reshape((2, gu.shape[0] // 2) + gu.shape[1:])
    act = swiglu3_fwd(tag + "swiglu", gu, tm)
    o2 = bmatmul(tag + "ffn_out", act, lp["w_ffn_out"], "nn", F32, False)
    (x2,) = rowstage_fwd(tag + "resid", resid_fn, [(x1, d, 0), (o2, d, 0)], [lp["g2"]], [(d, F32)], tm)
    saved = dict(x=x, h=h, proj=proj, conv_out=conv_out, views=views, yh=yh, y_ssd=y_ssd, yg=yg, ys=ys,
                 st_h=st_h, st_s=st_s, st_g=st_g, merged=merged, x1=x1, h2=h2, gu=gu, act=act, o2=o2)
    return x2, saved


def layer_bwd(dm, l, dx2, lp, sv):
    tm, d, mix, s = dm.tm, dm.d, dm.mix, dm.s
    tag = f"l{l}_b_"
    g = {}
    (dx1_a, do2), (g["g2"],) = rowstage_bwd(tag + "resid", resid_fn, [(sv["x1"], d, 0), (sv["o2"], d, 0)], [lp["g2"]],
                                            [dx2], [F32, BF16], tm)
    dact = bmatmul(tag + "ffn_out_dx", do2, lp["w_ffn_out"], "nt", BF16, True)
    g["w_ffn_out"] = bmatmul(tag + "ffn_out_dw", sv["act"], do2, "tn", F32, True)
    dgu = swiglu3_bwd(tag + "swiglu", sv["gu"], dact, tm)
    dgu = dgu.reshape((-1,) + dgu.shape[2:])
    dh2 = bmatmul(tag + "ffn_in_dx", dgu, lp["w_ffn_in"], "nt", BF16, False)
    g["w_ffn_in"] = bmatmul(tag + "ffn_in_dw", sv["h2"], dgu, "tn", F32, True)
    (dx1,), (g["norm_ffn"], g["sc2"], g["sh2"]) = rowstage_bwd(
        tag + "norm2", normmod_fn, [(sv["x1"], d, 0)], [lp["norm_ffn"], lp["sc2"], lp["sh2"]], [dh2], [F32], tm,
        adds={0: dx1_a})
    (dmerged, dx_a), (g["g1"], g["w_out"]) = rowstage_bwd(
        tag + "outproj", outproj_fn, [(sv["merged"], d, 0), (sv["x"], d, 0)], [lp["g1"], lp["w_out"]], [dx1],
        [BF16, F32], tm)
    proj, conv_out = sv["proj"], sv["conv_out"]
    (dyh, dys, dyg, dgates), (g["b_merge"], g["w_branch"]) = rowstage_bwd(
        tag + "merge", merge_fn, [(sv["yh"], mix, 0), (sv["ys"], mix, 0), (sv["yg"], mix, 0), (proj, 3 * d, 1)],
        [lp["b_merge"], lp["w_branch"]], [dmerged], [F32, F32, F32, BF16], tm)
    (dy_ssd, dxs_a, dsz), (g["ssm_d_exp"], g["ssm_norm"]) = rowstage_bwd(
        tag + "ssmpost", ssmpost_fn, [(sv["y_ssd"], mix, 0), (conv_out, mix, 0), (proj, mix, dm.o_sz // mix)],
        [lp["ssm_d_exp"], lp["ssm_norm"]], [dys], [F32, F32, BF16], tm)
    hg, sd, gd = _scan_specs(dm, proj, conv_out, sv["views"], lp)
    (dhgrn,), (g["lb"],), (g["hgrn_norm"],) = _run_scan_bwd(dm, tag + "hgrn", hg, sv["st_h"], dyh)
    (dxs_b, dbp, dcp, d_dt_col, d_dt_row), (g["ssm_dt_bias"], g["ssm_a_log"]), _ = _run_scan_bwd(
        dm, tag + "ssd", sd, sv["st_s"], dy_ssd)
    (dq, dk, dv, dgz, d_gb_col, d_ga_col, d_ga_row), (g["gdn_dt_bias"], g["gdn_a_log"]), (g["gdn_norm"],) = _run_scan_bwd(
        dm, tag + "gdn", gd, sv["st_g"], dyg)
    dconv = jnp.concatenate([dxs_a + dxs_b, dbp, dcp, dq, dk, dv], axis=1)
    dpc, g["conv_w"], g["conv_b"] = conv_bwd(tag + "conv", proj, dm.o_conv // LANES, lp["conv_w"], lp["conv_b"], dconv)
    unrow = lambda t: t.reshape(t.shape[0], s).T
    dsmall = jnp.concatenate([d_dt_col[:, :, 0].T + unrow(d_dt_row), d_gb_col[:, :, 0].T,
                              d_ga_col[:, :, 0].T + unrow(d_ga_row)], axis=1)
    pad = jnp.zeros((s, dm.np - dm.o_small - dsmall.shape[1]), BF16)
    dproj = jnp.concatenate([dhgrn, dgates, dsz, dgz, dpc, dsmall.astype(BF16), pad], axis=1)
    dh = matmul(tag + "proj_dx", dproj, lp["w_in"], "nt", BF16)
    g["w_in"] = matmul(tag + "proj_dw", sv["h"], dproj, "tn", F32)
    (dx,), (g["norm_mix"], g["sc1"], g["sh1"]) = rowstage_bwd(
        tag + "norm1", normmod_fn, [(sv["x"], d, 0)], [lp["norm_mix"], lp["sc1"], lp["sh1"]], [dh], [F32], tm,
        adds={0: dx_a})
    return dx, g


WEIGHTS = ("w_ada", "b_ada", "norm_mix", "norm_ffn", "w_in", "b_merge", "hgrn_lb_logits", "hgrn_norm", "ssm_conv_w",
           "ssm_conv_b", "ssm_dt_bias", "ssm_a_log", "ssm_d", "ssm_norm", "gdn_conv_w", "gdn_dt_bias", "gdn_a_log",
           "gdn_norm", "w_branch", "w_out", "w_ffn_in", "w_ffn_out", "norm_final")
GATHERED = ("w_in", "w_branch", "w_out", "w_ffn_in", "w_ffn_out")
PACKET = ("b_ada", "norm_mix", "norm_ffn", "b_merge", "hgrn_norm", "ssm_conv_b", "ssm_dt_bias", "ssm_a_log", "ssm_d",
          "ssm_norm", "gdn_dt_bias", "gdn_a_log", "gdn_norm", "norm_final")
MISC = ("hgrn_lb_logits", "ssm_conv_w", "gdn_conv_w")


def _pack(arrs, dtype, row_mult, lead=0):
    flat = jnp.concatenate([t.reshape(t.shape[:lead] + (-1,)).astype(dtype) for t in arrs], axis=lead)
    n = flat.shape[-1]
    unit = row_mult * LANES
    tot = -(-n // unit) * unit
    flat = jnp.pad(flat, [(0, 0)] * lead + [(0, tot - n)])
    return flat.reshape(flat.shape[:lead] + (tot // LANES, LANES))


def _unpack(packed, shapes, lead=0):
    flat = packed.reshape(packed.shape[:lead] + (-1,))
    out, off = [], 0
    for shp in shapes:
        n = int(np.prod(shp))
        out.append(flat[..., off:off + n].reshape(flat.shape[:lead] + tuple(shp)))
        off += n
    return out


def _shard2d(t):
    return t.reshape((-1, t.shape[-1]))


def weights_from_shards(dm, l, got, idx):
    w_in, wb, w_out, wf, wfo = got
    d, mix = dm.d, dm.mix
    return dict(
        w_in=colgather(f"l{l}_w_in", w_in, idx, dm.np, BF16)[0],
        w_branch=wb.reshape(N_DEV, 3, mix, d // N_DEV).transpose(1, 2, 0, 3).reshape(3, mix, d),
        w_out=w_out.reshape(d, d), w_ffn_in=wf, w_ffn_out=wfo.reshape(N_DEV // 2, -1, d))


def shards_of_grads(dm, l, g, idx):
    d, mix = dm.d, dm.mix
    return [colgather(f"l{l}_g_w_in", g["w_in"][None], idx, dm.in_width // N_DEV, F32),
            g["w_branch"].reshape(3, mix, N_DEV, d // N_DEV).transpose(2, 0, 1, 3).reshape(N_DEV, 3 * mix, d // N_DEV),
            g["w_out"].reshape(N_DEV, d // N_DEV, d), g["w_ffn_in"], g["w_ffn_out"].reshape(N_DEV, -1, d)]


def layer_params(dm, l, full, small, mod_l, lb_l):
    d, mix = dm.d, dm.mix
    row = lambda t: t.reshape(1, -1)
    head = lambda t: t.reshape(-1, 1, 1)
    sh1, sc1, g1, sh2, sc2, g2 = (row(mod_l[i * d:(i + 1) * d]) for i in range(6))
    conv_b = jnp.concatenate([small["ssm_conv_b"][l], jnp.zeros((3 * mix,), F32)])
    return dict(
        w_in=full["w_in"], w_branch=full["w_branch"], w_out=full["w_out"],
        w_ffn_in=full["w_ffn_in"], w_ffn_out=full["w_ffn_out"],
        norm_mix=row(small["norm_mix"][l]), norm_ffn=row(small["norm_ffn"][l]), b_merge=row(small["b_merge"][l]),
        hgrn_norm=row(small["hgrn_norm"][l]), lb=row(lb_l),
        conv_w=jnp.concatenate([small["ssm_conv_w"][l], small["gdn_conv_w"][l]], axis=1), conv_b=row(conv_b),
        ssm_dt_bias=head(small["ssm_dt_bias"][l]), ssm_a_log=head(small["ssm_a_log"][l]),
        ssm_d_exp=row(jnp.repeat(small["ssm_d"][l], SSM_P)), ssm_norm=row(small["ssm_norm"][l]),
        gdn_dt_bias=head(small["gdn_dt_bias"][l]), gdn_a_log=head(small["gdn_a_log"][l]), gdn_norm=row(small["gdn_norm"][l]),
        sh1=sh1, sc1=sc1, g1=g1, sh2=sh2, sc2=sc2, g2=g2)


def layer_grads(dm, g):
    cs = dm.conv_ssm
    out = dict(
        w_in=g["w_in"], w_branch=g["w_branch"], w_out=g["w_out"], w_ffn_in=g["w_ffn_in"],
        w_ffn_out=g["w_ffn_out"], norm_mix=g["norm_mix"][0], norm_ffn=g["norm_ffn"][0], b_merge=g["b_merge"][0],
        hgrn_norm=g["hgrn_norm"][0], ssm_conv_w=g["conv_w"][:, :cs], gdn_conv_w=g["conv_w"][:, cs:],
        ssm_conv_b=g["conv_b"][0, :cs], ssm_dt_bias=g["ssm_dt_bias"][:, 0, 0], ssm_a_log=g["ssm_a_log"][:, 0, 0],
        ssm_d=g["ssm_d_exp"].reshape(dm.ssm_heads, SSM_P).sum(axis=1), ssm_norm=g["ssm_norm"][0],
        gdn_dt_bias=g["gdn_dt_bias"][:, 0, 0], gdn_a_log=g["gdn_a_log"][:, 0, 0], gdn_norm=g["gdn_norm"][0])
    dmod = jnp.concatenate([g[k][0] for k in ("sh1", "sc1", "g1", "sh2", "sc2", "g2")])
    return out, dmod, g["lb"][0]


def local_step(dm, x, tgt, lps, norm_final):
    saved = []
    for l, lp in enumerate(lps):
        x, sv = layer_fwd(dm, l, x, lp)
        saved.append(sv)
    loss, dx, dnf = loss_call("loss", x, tgt, norm_final, dm.tm)
    grads = [None] * len(lps)
    for l in reversed(range(len(lps))):
        dx, grads[l] = layer_bwd(dm, l, dx, lps[l], saved[l])
    return loss, dx, dnf, grads


def kernel(x, c, w_ada, b_ada, norm_mix, norm_ffn, w_in, b_merge, hgrn_lb_logits, hgrn_norm, ssm_conv_w, ssm_conv_b, ssm_dt_bias, ssm_a_log, ssm_d, ssm_norm, gdn_conv_w, gdn_dt_bias, gdn_a_log, gdn_norm, w_branch, w_out, w_ffn_in, w_ffn_out, norm_final, loss_target, m_w_ada, m_b_ada, m_norm_mix, m_norm_ffn, m_w_in, m_b_merge, m_hgrn_lb_logits, m_hgrn_norm, m_ssm_conv_w, m_ssm_conv_b, m_ssm_dt_bias, m_ssm_a_log, m_ssm_d, m_ssm_norm, m_gdn_conv_w, m_gdn_dt_bias, m_gdn_a_log, m_gdn_norm, m_w_branch, m_w_out, m_w_ffn_in, m_w_ffn_out, m_norm_final, v_w_ada, v_b_ada, v_norm_mix, v_norm_ffn, v_w_in, v_b_merge, v_hgrn_lb_logits, v_hgrn_norm, v_ssm_conv_w, v_ssm_conv_b, v_ssm_dt_bias, v_ssm_a_log, v_ssm_d, v_ssm_norm, v_gdn_conv_w, v_gdn_dt_bias, v_gdn_a_log, v_gdn_norm, v_w_branch, v_w_out, v_w_ffn_in, v_w_ffn_out, v_norm_final):
    a = dict(locals())
    x, tgt = a["x"][0], a["loss_target"][0]
    s, d = x.shape
    nl = a["w_ada"].shape[0]
    dm = Dims(s, d, a["w_ffn_out"].shape[1] * N_DEV)
    me = 4 * lax.axis_index("x") + 2 * lax.axis_index("y") + lax.axis_index("c")

    first = [a["c"], a["ssm_conv_w"], a["gdn_conv_w"]]
    c_all, scw, gcw = _unpack(exchange("gather_c", [_pack(first, F32, 8)], True)[0], [t.shape for t in first], lead=1)
    small = dict(a, ssm_conv_w=scw.transpose(1, 2, 0, 3).reshape(scw.shape[1:3] + (-1,)),
                 gdn_conv_w=gcw.transpose(1, 2, 0, 3).reshape(gcw.shape[1:3] + (-1,)))
    c_pad = jnp.zeros((LANES, d), F32).at[:N_DEV].set(c_all.reshape(N_DEV, d))
    ncol = a["w_ada"].shape[2]
    b_mine = lax.dynamic_slice(a["b_ada"], (0, me * ncol), (nl, ncol))[:, None, :]
    mod_part = ada_fwd("ada_fwd", c_pad, a["w_ada"], b_mine)[:, :N_DEV, :]
    (mod,) = exchange("a2a_mod", [mod_part.transpose(1, 0, 2)], False)
    mod = mod.transpose(1, 0, 2).reshape(nl, N_DEV * ncol)
    (lb,) = rowstage_fwd("lower_bounds", lower_bounds_fn, [(a["hgrn_lb_logits"], dm.mix, 0)], [], [(dm.mix, F32)], nl)

    idx_fwd, idx_bwd = w_in_tables(dm, N_DEV)
    lps = []
    for l in range(nl):
        got = exchange(f"gather_w{l}", [_shard2d(a[n][l]).astype(BF16) for n in GATHERED], True)
        lps.append(layer_params(dm, l, weights_from_shards(dm, l, got, idx_fwd), small, mod[l], lb[l]))

    loss, dx, dnf, grads = local_step(dm, x, tgt, lps, a["norm_final"].reshape(1, d))

    per_layer = [layer_grads(dm, g) for g in grads]
    res = {}
    for l in range(nl):
        parts = exchange(f"scatter_g{l}", shards_of_grads(dm, l, per_layer[l][0], idx_bwd), False)
        for n, p in zip(GATHERED, parts):
            outs = adamw_sum(f"adamw_l{l}_{n}", p, *[_shard2d(a[q + n][l]) for q in ("", "m_", "v_")])
            for kind, o in zip(("grad", "delta", "new_m", "new_v"), outs):
                res.setdefault((kind, n), []).append(o.reshape(a[n].shape[1:]))
    for key in list(res):
        res[key] = jnp.stack(res[key])

    stackg = lambda n: jnp.stack([pl_[0][n] for pl_ in per_layer])
    dmod = jnp.stack([pl_[1] for pl_ in per_layer])
    dlb = jnp.stack([pl_[2] for pl_ in per_layer])
    pk_g = [dmod if n == "b_ada" else dnf if n == "norm_final" else stackg(n) for n in PACKET]
    extra = [dlb, stackg("ssm_conv_w"), stackg("gdn_conv_w"), loss[0, :1]]
    pk_shapes = [t.shape for t in pk_g + extra]
    zeros = [jnp.zeros(t.shape, F32) for t in extra]
    (parts,) = exchange("gather_small", [_pack(pk_g + extra, F32, 8)], True)
    outs = adamw_sum("adamw_small", parts, *[_pack([a[p + n] for n in PACKET] + zeros, F32, 8) for p in ("", "m_", "v_")])
    for kind, o in zip(("grad", "delta", "new_m", "new_v"), outs):
        un = _unpack(o, pk_shapes)
        for n, t in zip(PACKET, un):
            res[(kind, n)] = t.reshape(a[n].shape)
        if kind == "grad":
            dlb_sum, g_scw, g_gcw, loss_sum = un[len(PACKET):]

    (g_lb,), _ = rowstage_bwd("lower_bounds_b", lower_bounds_fn, [(a["hgrn_lb_logits"], dm.mix, 0)], [], [dlb_sum], [F32], nl)
    mine = lambda t, n: lax.dynamic_slice_in_dim(t, me * a[n].shape[-1], a[n].shape[-1], axis=t.ndim - 1)
    (dmod_cols,) = exchange("a2a_dmod", [dmod.reshape(nl, N_DEV, ncol).transpose(1, 0, 2)], False)
    dmod_pad = jnp.zeros((nl, LANES, ncol), F32).at[:, :N_DEV].set(dmod_cols.transpose(1, 0, 2))
    g_w_ada = ada_bwd("ada_bwd", c_pad, dmod_pad)
    outs = adamw_sum("adamw_w_ada", g_w_ada.reshape(1, nl * d, ncol), *[a[q + "w_ada"].reshape(nl * d, ncol) for q in ("", "m_", "v_")])
    for kind, o in zip(("grad", "delta", "new_m", "new_v"), outs):
        res[(kind, "w_ada")] = o.reshape(nl, d, ncol)
    g_misc = [g_lb, mine(g_scw, "ssm_conv_w"), mine(g_gcw, "gdn_conv_w")]
    outs = adamw_sum("adamw_misc", _pack(g_misc, F32, 8)[None], *[_pack([a[q + n] for n in MISC], F32, 8) for q in ("", "m_", "v_")])
    for kind, o in zip(("grad", "delta", "new_m", "new_v"), outs):
        for n, t in zip(MISC, _unpack(o, [a[n].shape for n in MISC])):
            res[(kind, n)] = t

    out = [loss_sum.reshape(()), dx[None]]
    for kind in ("grad", "delta", "new_m", "new_v"):
        out += [res[(kind, n)] for n in WEIGHTS]
    return tuple(out)
```

```python
import functools
import math

import numpy as np
import jax
import jax.numpy as jnp
from jax import lax
from jax.experimental import pallas as pl
from jax.experimental.pallas import tpu as pltpu

F32 = jnp.float32
BF16 = jnp.bfloat16

N_DEV = 8
CHUNK = 64
SUB = 16
HEADS_PER_STEP = 2
HEAD = 128
SSM_P = 64
CONV_K = 4
F_MIN = 1e-30
NORM_EPS = 1e-6
LANES = 128
VMEM_LIMIT = 56 * 1024 * 1024

ADAM_LR = 0.001
ADAM_B1 = 0.9
ADAM_B2 = 0.999
ADAM_EPS = 1e-08
ADAM_WD = 0.01
ADAM_STEP = 10


def _dg(a, b, ca, cb):
    return lax.dot_general(a.astype(BF16), b.astype(BF16), (((ca,), (cb,)), ((), ())),
                           preferred_element_type=F32)


def _split3(x):
    x1 = x.astype(BF16)
    r = x - x1.astype(F32)
    x2 = r.astype(BF16)
    x3 = (r - x2.astype(F32)).astype(BF16)
    return x1, x2, x3


def _hdg(a, b, ca, cb):
    a1, a2, _ = _split3(a)
    b1, b2, _ = _split3(b)
    dn = (((ca,), (cb,)), ((), ()))
    d = lambda p, q: lax.dot_general(p, q, dn, preferred_element_type=F32)
    return (d(a2, b1) + d(a1, b2)) + d(a1, b1)


def _dot_family(prim):
    @jax.custom_vjp
    def nn(a, b):
        return prim(a, b, 1, 0)

    @jax.custom_vjp
    def nt(a, b):
        return prim(a, b, 1, 1)

    @jax.custom_vjp
    def tn(a, b):
        return prim(a, b, 0, 0)

    nn.defvjp(lambda a, b: (nn(a, b), (a, b)), lambda r, g: (nt(g, r[1]), tn(r[0], g)))
    nt.defvjp(lambda a, b: (nt(a, b), (a, b)), lambda r, g: (nn(g, r[1]), tn(g, r[0])))
    tn.defvjp(lambda a, b: (tn(a, b), (a, b)), lambda r, g: (nt(r[1], g), nn(r[0], g)))
    return nn, nt, tn


mm_nn, mm_nt, mm_tn = _dot_family(_dg)
hd_nn, hd_nt, hd_tn = _dot_family(_hdg)


def _iota(shape, dim):
    return lax.broadcasted_iota(jnp.int32, shape, dim)


def _scan_rows(x, reverse):
    n = x.shape[0]
    rows = _iota(x.shape, 0)
    k = 1
    while k < n:
        if reverse:
            x = x + jnp.where(rows < n - k, pltpu.roll(x, n - k, 0), 0.0)
        else:
            x = x + jnp.where(rows >= k, pltpu.roll(x, k, 0), 0.0)
        k *= 2
    return x


@jax.custom_vjp
def cumsum_rows(x):
    return _scan_rows(x, False)


cumsum_rows.defvjp(lambda x: (_scan_rows(x, False), None), lambda _, g: (_scan_rows(g, True),))


def _sigmoid(x):
    return jax.nn.sigmoid(x)


def _silu(x):
    return x * jax.nn.sigmoid(x)


def _softplus(x):
    e = jnp.exp(-jnp.abs(x))
    small = e * (1.0 - e * (0.5 - e * (1.0 / 3.0)))
    return jnp.maximum(x, 0.0) + jnp.where(e < 1e-3, small, jnp.log(1.0 + e))


def _masked_exp(diff, mask):
    return jnp.where(mask, jnp.exp(jnp.where(mask, diff, 0.0)), 0.0)


def _rms(x, w):
    return x * lax.rsqrt(jnp.mean(x * x, axis=-1, keepdims=True) + NORM_EPS) * w


def _cum_col_row(lg_col, lg_row):
    c = lg_col.shape[0]
    r, s = _iota((c, c), 0), _iota((c, c), 1)
    cum_col = jnp.sum(jnp.where(s <= r, jnp.broadcast_to(lg_row, (c, c)), 0.0), axis=1, keepdims=True)
    cum_row = jnp.sum(jnp.where(r <= s, jnp.broadcast_to(lg_col, (c, c)), 0.0), axis=0, keepdims=True)
    total = jnp.sum(lg_col, axis=0, keepdims=True)
    return cum_col, cum_row, total


def hgrn_chunk(seq, hp, sp, st):
    (blk,), (lb,), (nw,) = seq, hp, sp
    c = blk.shape[0]
    q_raw, f_raw, v, g_raw = (blk[:, i * HEAD:(i + 1) * HEAD] for i in range(4))
    q = _silu(q_raw)
    f = lb + (1.0 - lb) * _sigmoid(f_raw)
    logf = jnp.log(jnp.maximum(f, F_MIN))
    k = (1.0 - lb) * _sigmoid(-f_raw)
    b = cumsum_rows(logf)
    o_inter = mm_nt(q * jnp.exp(b), st)
    nsub = c // SUB
    wide = (SUB, SUB, HEAD)
    er = _iota((SUB * SUB, SUB), 0)
    e_t = (er // SUB == _iota((SUB * SUB, SUB), 1)).astype(F32)
    pr = _iota((SUB * SUB, 1), 0)
    pmask = (pr % SUB) <= (pr // SUB)
    er64 = _iota((SUB * SUB, c), 0)
    ec64 = _iota((SUB * SUB, c), 1)
    rows_c = _iota((c, 1), 0)
    row = lambda a, i: jnp.sum(jnp.where(rows_c == i, a, 0.0), axis=0, keepdims=True)
    parts = []
    for i in range(nsub):
        sl = slice(SUB * i, SUB * (i + 1))
        qi, ki, bi = q[sl], k[sl], b[sl]
        qb = jnp.broadcast_to(qi[:, None, :], wide).reshape(SUB * SUB, HEAD)
        kb = jnp.broadcast_to(ki[None, :, :], wide).reshape(SUB * SUB, HEAD)
        bd = (bi[:, None, :] - bi[None, :, :]).reshape(SUB * SUB, HEAD)
        sc_col = jnp.sum(qb * kb * _masked_exp(bd, pmask), axis=1, keepdims=True)
        place = (ec64 == (er64 % SUB) + SUB * i).astype(F32)
        sc = mm_tn(e_t, sc_col * place)
        if i > 0:
            bref = row(b, SUB * i)
            qt = qi * jnp.exp(bi - bref)
            kt = k * _masked_exp(bref - b, rows_c < SUB * i)
            sc = sc + mm_nt(qt, kt)
        parts.append(mm_nn(sc, v))
    o = o_inter + jnp.concatenate(parts, axis=0)
    bend = row(b, c - 1)
    st_new = st * jnp.exp(bend) + mm_tn(v, k * jnp.exp(bend - b))
    y = _rms(o, nw) * _silu(g_raw)
    return (y,), st_new


def ssd_chunk(seq, hp, sp, st):
    xs, bm, cm, dtc, dtr = seq
    dt_bias, a_log = hp
    c = xs.shape[0]
    lane = _iota((1, 2 * SSM_P), 1)
    first = lane < SSM_P
    r, s = _iota((c, c), 0), _iota((c, c), 1)
    g = mm_nt(cm, bm)
    dts, cums, ends, segs = [], [], [], []
    for i in range(2):
        neg_a = -jnp.exp(a_log[i])
        dt_col = _softplus(dtc[i] + dt_bias[i])
        dt_row = _softplus(dtr[i] + dt_bias[i])
        cum_col, cum_row, total = _cum_col_row(neg_a * dt_col, neg_a * dt_row)
        dts.append(dt_col)
        cums.append(cum_col)
        ends.append(total)
        segs.append(_masked_exp(cum_col - cum_row, s <= r))
    dt_l = jnp.where(first, dts[0], dts[1])
    cum_l = jnp.where(first, cums[0], cums[1])
    end_l = jnp.where(first, ends[0], ends[1])
    xdt = xs * dt_l
    y_intra = (mm_nn(g * segs[0], jnp.where(first, xdt, 0.0))
               + mm_nn(g * segs[1], jnp.where(first, 0.0, xdt)))
    y_inter = mm_nn(cm, st) * jnp.exp(cum_l)
    st_new = st * jnp.exp(end_l) + mm_tn(bm, xdt * jnp.exp(end_l - cum_l))
    return (y_intra + y_inter,), st_new


def _neumann_inverse(a):
    n = a.shape[0]
    eye = (_iota((n, n), 0) == _iota((n, n), 1)).astype(F32)
    p = -a
    t = eye + p
    for _ in range(int(math.log2(n)) - 1):
        p = _hdg(p, p, 1, 0)
        t = t + _hdg(t, p, 1, 0)
    return t


@jax.custom_vjp
def inv_unit_lower(a):
    return _neumann_inverse(a)


def _inv_fwd(a):
    t = _neumann_inverse(a)
    return t, t


inv_unit_lower.defvjp(_inv_fwd, lambda t, g: (-hd_nt(hd_tn(t, g), t),))


def gdn_chunk(seq, hp, sp, st):
    q_raw, k_raw, v, z, gbc, gac, gar = seq
    dt_bias, a_log = hp
    (nw,) = sp
    c = v.shape[0]
    r, s = _iota((c, c), 0), _iota((c, c), 1)
    q = q_raw * lax.rsqrt(jnp.sum(q_raw * q_raw, axis=-1, keepdims=True) + NORM_EPS) * (HEAD ** -0.5)
    k = k_raw * lax.rsqrt(jnp.sum(k_raw * k_raw, axis=-1, keepdims=True) + NORM_EPS)
    beta = _sigmoid(gbc)
    neg_a = -jnp.exp(a_log)
    cum, cum_row, total = _cum_col_row(neg_a * _softplus(gac + dt_bias), neg_a * _softplus(gar + dt_bias))
    decay = _masked_exp(cum - cum_row, s <= r)
    kk = mm_nt(k, k)
    a_low = jnp.where(s < r, beta * kk * decay, 0.0)
    sol = hd_nn(inv_unit_lower(a_low), jnp.concatenate([v * beta, k * (beta * jnp.exp(cum))], axis=1))
    u_base, w_corr = sol[:, :HEAD], sol[:, HEAD:]
    qk = mm_nt(q, k) * decay
    u = u_base - mm_nn(w_corr, st)
    o = mm_nn(q * jnp.exp(cum), st) + mm_nn(qk, u)
    st_new = jnp.exp(total) * st + mm_tn(k * jnp.exp(total - cum), u)
    y = _rms(o, nw) * _silu(z)
    return (y,), st_new


def normmod_fn(rows, params):
    (x,), (nw, sc, sh) = rows, params
    return (_rms(x, nw) * (1.0 + sc) + sh,)


def ssmpost_fn(rows, params):
    (y, xs, z), (d_exp, nw) = rows, params
    y = (y + d_exp * xs) * _silu(z)
    gw = y.shape[1] // 2
    return (jnp.concatenate([_rms(y[:, :gw], nw[:, :gw]), _rms(y[:, gw:], nw[:, gw:])], axis=1),)


def merge_fn(rows, params):
    (yh, ys, yg, gl), (bm, wb) = rows, params
    d = wb.shape[2]
    gates = _sigmoid(gl + bm)
    out = 0.0
    for n, y in enumerate((yh, ys, yg)):
        out = out + gates[:, n * d:(n + 1) * d] * mm_nn(y, wb[n])
    return (out,)


def outproj_fn(rows, params):
    (m, x), (g1, w) = rows, params
    return (x + (1.0 + g1) * mm_nn(m, w),)


def resid_fn(rows, params):
    (x, o), (g2,) = rows, params
    return (x + (1.0 + g2) * o,)


def _params(sem, side_effects=False):
    return pltpu.CompilerParams(dimension_semantics=sem, vmem_limit_bytes=VMEM_LIMIT, has_side_effects=side_effects)


def _whole(a):
    nd = a.ndim
    return pl.BlockSpec(a.shape, lambda *_: (0,) * nd)


def _pick(n, cands):
    for c in cands:
        if n % c == 0:
            return c
    return n


def matmul(name, a, b, mode, out_dtype):
    if mode == "nn":
        (m, k), n = a.shape, b.shape[1]
    elif mode == "nt":
        (m, k), n = a.shape, b.shape[0]
    else:
        (k, m), n = a.shape, b.shape[1]
    tm = _pick(m, (512, 256, 128))
    tn = _pick(n, (1280, 1024, 1408, 768, 512, 384, 256, 128))
    tk = _pick(k, (1024, 1280, 1408, 768, 512, 256, 128))
    if mode == "tn":
        tm = _pick(m, (1024, 768, 512, 256, 128))
        tk = _pick(k, (512, 256, 128))
    nk = k // tk
    ca, cb = {"nn": (1, 0), "nt": (1, 1), "tn": (0, 0)}[mode]

    def body(a_ref, b_ref, o_ref, acc_ref):
        kk = pl.program_id(2)

        @pl.when(kk == 0)
        def _():
            acc_ref[...] = jnp.zeros_like(acc_ref)

        acc_ref[...] += _dg(a_ref[...], b_ref[...], ca, cb)

        @pl.when(kk == nk - 1)
        def _():
            o_ref[...] = acc_ref[...].astype(o_ref.dtype)

    a_spec = (pl.BlockSpec((tk, tm), lambda i, j, q: (q, i)) if mode == "tn"
              else pl.BlockSpec((tm, tk), lambda i, j, q: (i, q)))
    b_spec = (pl.BlockSpec((tn, tk), lambda i, j, q: (j, q)) if mode == "nt"
              else pl.BlockSpec((tk, tn), lambda i, j, q: (q, j)))
    return pl.pallas_call(
        body, name=name, grid=(m // tm, n // tn, nk),
        in_specs=[a_spec, b_spec],
        out_specs=pl.BlockSpec((tm, tn), lambda i, j, q: (i, j)),
        out_shape=jax.ShapeDtypeStruct((m, n), out_dtype),
        scratch_shapes=[pltpu.VMEM((tm, tn), F32)],
        compiler_params=_params(("parallel", "parallel", "arbitrary")),
    )(a, b)


def bmatmul(name, a, b, mode, out_dtype, out_batched):
    ab, bb = a.ndim == 3, b.ndim == 3
    nb = a.shape[0] if ab else b.shape[0]
    a2, b2 = a.shape[-2:], b.shape[-2:]
    if mode == "nn":
        (m, k), n = a2, b2[1]
    elif mode == "nt":
        (m, k), n = a2, b2[0]
    else:
        (k, m), n = a2, b2[1]
    tm = _pick(m, (1024, 512, 256, 128) if mode == "tn" else (512, 256, 128))
    tn = _pick(n, (1024, 512, 256, 128))
    tk = _pick(k, (512, 256, 128) if mode == "tn" else (1024, 512, 256, 128))
    nk = k // tk
    ca, cb = {"nn": (1, 0), "nt": (1, 1), "tn": (0, 0)}[mode]
    ids = (lambda g: g) if out_batched else (lambda g: (g[2], g[0], g[1], g[3]))
    grid = (nb, m // tm, n // tn, nk) if out_batched else (m // tm, n // tn, nb, nk)

    def a_map(*g):
        bi, i, j, q = ids(g)
        idx = (q, i) if mode == "tn" else (i, q)
        return (bi,) + idx if ab else idx

    def b_map(*g):
        bi, i, j, q = ids(g)
        idx = (j, q) if mode == "nt" else (q, j)
        return (bi,) + idx if bb else idx

    def o_map(*g):
        bi, i, j, q = ids(g)
        return (bi, i, j) if out_batched else (i, j)

    def body(a_ref, b_ref, o_ref, acc_ref):
        bi, _, _, q = ids(tuple(pl.program_id(d) for d in range(4)))
        first = (q == 0) if out_batched else (q == 0) & (bi == 0)
        last = (q == nk - 1) if out_batched else (q == nk - 1) & (bi == nb - 1)

        @pl.when(first)
        def _():
            acc_ref[...] = jnp.zeros_like(acc_ref)

        acc_ref[...] += _dg(a_ref[...], b_ref[...], ca, cb)

        @pl.when(last)
        def _():
            o_ref[...] = acc_ref[...].astype(o_ref.dtype)

    a_blk = (tk, tm) if mode == "tn" else (tm, tk)
    b_blk = (tn, tk) if mode == "nt" else (tk, tn)
    return pl.pallas_call(
        body, name=name, grid=grid,
        in_specs=[pl.BlockSpec(((None,) if ab else ()) + a_blk, a_map), pl.BlockSpec(((None,) if bb else ()) + b_blk, b_map)],
        out_specs=pl.BlockSpec(((None,) if out_batched else ()) + (tm, tn), o_map),
        out_shape=jax.ShapeDtypeStruct(((nb,) if out_batched else ()) + (m, n), out_dtype),
        scratch_shapes=[pltpu.VMEM((tm, tn), F32)],
        compiler_params=_params(("parallel", "parallel", "arbitrary", "arbitrary")),
    )(a, b)


def colgather(name, src, idx, dst_w, out_dtype):
    nsrc, rows, w = src.shape
    nbs = -(-w // LANES)
    ne = idx.shape[0]
    nbd = idx.shape[1] // LANES
    tiles = [sorted(set((idx[e, t * LANES:(t + 1) * LANES][idx[e, t * LANES:(t + 1) * LANES] >= 0] // LANES).tolist()))
             for e in range(ne) for t in range(nbd)]
    nslot = max(1, max(len(t) for t in tiles))
    tbl = np.full((ne * nbd, nslot), -1, np.int32)
    for i, t in enumerate(tiles):
        tbl[i, :len(t)] = t
    exact3 = src.dtype == F32

    def body(tbl_ref, idx_ref, src_ref, o_ref, acc_ref):
        ti, si = pl.program_id(0), pl.program_id(1)

        @pl.when(si == 0)
        def _():
            acc_ref[...] = jnp.zeros_like(acc_ref)

        t = tbl_ref[ti * nslot + si]

        @pl.when(t >= 0)
        def _():
            onehot = ((_iota((LANES, LANES), 0) + t * LANES) == idx_ref[...]).astype(BF16)
            col = _iota((1, LANES), 1) + (t % nbs) * LANES
            xv = jnp.where(col < w, src_ref[...], jnp.zeros((), src_ref.dtype))
            d = lambda p: lax.dot_general(p, onehot, (((1,), (0,)), ((), ())), preferred_element_type=F32)
            if exact3:
                x1, x2, x3 = _split3(xv)
                acc_ref[...] += (d(x3) + d(x2)) + d(x1)
            else:
                acc_ref[...] += d(xv)

        @pl.when(si == nslot - 1)
        def _():
            o_ref[...] = acc_ref[...].astype(o_ref.dtype)

    def src_map(ti, si, tbl_ref):
        t = jnp.maximum(tbl_ref[ti * nslot + si], 0)
        return (t // nbs, 0, t % nbs)

    grid_spec = pltpu.PrefetchScalarGridSpec(
        num_scalar_prefetch=1, grid=(ne * nbd, nslot),
        in_specs=[pl.BlockSpec((None, 1, LANES), lambda ti, si, tbl_ref: (ti // nbd, 0, ti % nbd)),
                  pl.BlockSpec((None, rows, LANES), src_map)],
        out_specs=pl.BlockSpec((None, rows, LANES), lambda ti, si, tbl_ref: (ti // nbd, 0, ti % nbd)),
        scratch_shapes=[pltpu.VMEM((rows, LANES), F32)])
    return pl.pallas_call(
        body, name=name, grid_spec=grid_spec,
        out_shape=jax.ShapeDtypeStruct((ne, rows, dst_w), out_dtype),
        compiler_params=_params(("parallel", "arbitrary")),
    )(jnp.asarray(tbl.reshape(-1)), jnp.asarray(idx.reshape(ne, 1, nbd * LANES).astype(np.int32)), src)


def swiglu3_fwd(name, gu, tm):
    _, nb, s, w = gu.shape

    def body(x_ref, o_ref):
        o_ref[...] = (_silu(x_ref[0]) * x_ref[1]).astype(o_ref.dtype)

    return pl.pallas_call(
        body, name=name, grid=(nb, s // tm),
        in_specs=[pl.BlockSpec((2, None, tm, w), lambda b, i: (0, b, i, 0))],
        out_specs=pl.BlockSpec((None, tm, w), lambda b, i: (b, i, 0)),
        out_shape=jax.ShapeDtypeStruct((nb, s, w), BF16),
        compiler_params=_params(("parallel", "parallel")),
    )(gu)


def swiglu3_bwd(name, gu, dact, tm):
    _, nb, s, w = gu.shape

    def body(x_ref, g_ref, o_ref):
        _, vjp = jax.vjp(lambda a, b: _silu(a) * b, x_ref[0], x_ref[1])
        dg, du = vjp(g_ref[...].astype(F32))
        o_ref[0] = dg.astype(o_ref.dtype)
        o_ref[1] = du.astype(o_ref.dtype)

    return pl.pallas_call(
        body, name=name, grid=(nb, s // tm),
        in_specs=[pl.BlockSpec((2, None, tm, w), lambda b, i: (0, b, i, 0)),
                  pl.BlockSpec((None, tm, w), lambda b, i: (b, i, 0))],
        out_specs=pl.BlockSpec((2, None, tm, w), lambda b, i: (0, b, i, 0)),
        out_shape=jax.ShapeDtypeStruct(gu.shape, BF16),
        compiler_params=_params(("parallel", "parallel")),
    )(gu, dact)


def _row_specs(rows, tm):
    return [pl.BlockSpec((tm, w), lambda i, _c=c: (i, _c)) for (_, w, c) in rows]


def rowstage_fwd(name, fn, rows, params, outs, tm):
    s = rows[0][0].shape[0]
    nr, npar = len(rows), len(params)

    def body(*refs):
        r = [x[...].astype(F32) for x in refs[:nr]]
        p = [x[...].astype(F32) for x in refs[nr:nr + npar]]
        for ref, val in zip(refs[nr + npar:], fn(r, p)):
            ref[...] = val.astype(ref.dtype)

    res = pl.pallas_call(
        body, name=name, grid=(s // tm,),
        in_specs=_row_specs(rows, tm) + [_whole(p) for p in params],
        out_specs=[pl.BlockSpec((tm, w), lambda i: (i, 0)) for (w, _) in outs],
        out_shape=[jax.ShapeDtypeStruct((s, w), dt) for (w, dt) in outs],
        compiler_params=_params(("parallel",)),
    )(*[r[0] for r in rows], *params)
    return res


def rowstage_bwd(name, fn, rows, params, douts, drow_dtypes, tm, adds=None):
    s = rows[0][0].shape[0]
    nr, npar, no = len(rows), len(params), len(douts)
    adds = adds or {}
    add_idx = sorted(adds)
    na = len(add_idx)

    def body(*refs):
        r = [x[...].astype(F32) for x in refs[:nr]]
        p = [x[...].astype(F32) for x in refs[nr:nr + npar]]
        g = [x[...].astype(F32) for x in refs[nr + npar:nr + npar + no]]
        a_refs = refs[nr + npar + no:nr + npar + no + na]
        dr_refs = refs[nr + npar + no + na:nr + npar + no + na + nr]
        dp_refs = refs[nr + npar + no + na + nr:]
        _, vjp = jax.vjp(lambda r_, p_: tuple(fn(r_, p_)), r, p)
        dr, dp = vjp(tuple(g))
        for j, (ref, val) in enumerate(zip(dr_refs, dr)):
            if j in adds:
                val = val + a_refs[add_idx.index(j)][...].astype(F32)
            ref[...] = val.astype(ref.dtype)

        @pl.when(pl.program_id(0) == 0)
        def _():
            for ref in dp_refs:
                ref[...] = jnp.zeros_like(ref)

        for ref, val in zip(dp_refs, dp):
            ref[...] += val

    res = pl.pallas_call(
        body, name=name, grid=(s // tm,),
        in_specs=(_row_specs(rows, tm) + [_whole(p) for p in params]
                  + [pl.BlockSpec((tm, d.shape[1]), lambda i: (i, 0)) for d in douts]
                  + [pl.BlockSpec((tm, rows[j][1]), lambda i: (i, 0)) for j in add_idx]),
        out_specs=([pl.BlockSpec((tm, w), lambda i: (i, 0)) for (_, w, _) in rows] + [_whole(p) for p in params]),
        out_shape=([jax.ShapeDtypeStruct((s, w), dt) for (_, w, _), dt in zip(rows, drow_dtypes)]
                   + [jax.ShapeDtypeStruct(p.shape, F32) for p in params]),
        compiler_params=_params(("arbitrary",)),
    )(*[r[0] for r in rows], *params, *douts, *[adds[j] for j in add_idx])
    return res[:nr], res[nr:]


def _flip(index_map, nc):
    return lambda h, n: index_map(h, nc - 1 - n)


def _with_side(core, n_in, n_out, side, grid):
    if side is None:
        return core, [], [], [], [], ()
    sends, broadcast = side
    k = len(sends)

    def body(*refs):
        ins, snd = refs[:n_in], refs[n_in:n_in + k]
        outs, rcv = refs[n_in + k:n_in + k + n_out], refs[n_in + k + n_out:n_in + 2 * k + n_out]
        scr = refs[n_in + 2 * k + n_out:]
        start, wait = _exchange_ops(snd, rcv, *scr[1:], broadcast)
        ids = [pl.program_id(d) for d in range(len(grid))]
        first = functools.reduce(lambda a, b: a & b, [i == 0 for i in ids])
        last = functools.reduce(lambda a, b: a & b, [i == g - 1 for i, g in zip(ids, grid)])
        pl.when(first)(start)
        core(*ins, *outs, scr[0])
        pl.when(last)(wait)

    return body, [HBM_SPEC] * k, [HBM_SPEC] * k, _exchange_out(sends, broadcast), _exchange_sems(k), tuple(sends)


def _take(v, split, j):
    if split is None:
        return v
    if split[0] == "lane":
        return v[:, j * split[1]:(j + 1) * split[1]]
    if split[0] == "lead":
        return v[j * split[1]:(j + 1) * split[1]]
    return v[j]


def _where(split, j):
    if split[0] == "lane":
        return (slice(None), slice(j * split[1], (j + 1) * split[1]))
    if split[0] == "lead":
        return (slice(j * split[1], (j + 1) * split[1]),)
    return (j,)


def scan_fwd(name, chunk_fn, nblk, hb, nc, seqs, hparams, sparams, state_shape, outs, side=None):
    ns, nhp, nsp, no = len(seqs), len(hparams), len(sparams), len(outs)

    def core(*refs):
        seq_r, hp_r, sp_r = refs[:ns], refs[ns:ns + nhp], refs[ns + nhp:ns + nhp + nsp]
        out_r = refs[ns + nhp + nsp:ns + nhp + nsp + no]
        st_out, st_scr = refs[-2], refs[-1]

        @pl.when(pl.program_id(1) == 0)
        def _():
            st_scr[...] = jnp.zeros_like(st_scr)

        seq_v = [x[...].astype(F32) for x in seq_r]
        hp_v = [x[...] for x in hp_r]
        sp_v = [x[...] for x in sp_r]
        for j in range(hb):
            st = st_scr[j]
            st_out[j] = st
            o, st_new = chunk_fn([_take(v, s[3], j) for v, s in zip(seq_v, seqs)],
                                 [_take(v, s[3], j) for v, s in zip(hp_v, hparams)], sp_v, st)
            for ref, spec, val in zip(out_r, outs, o):
                ref[_where(spec[4], j)] = val.astype(ref.dtype)
            st_scr[j] = st_new

    nst = len(state_shape)
    body, s_in, s_out, s_shape, s_scr, s_args = _with_side(core, ns + nhp + nsp, no + 1, side, (nblk, nc))
    res = pl.pallas_call(
        body, name=name, grid=(nblk, nc),
        in_specs=([pl.BlockSpec(bs, im) for (_, bs, im, _) in seqs]
                  + [pl.BlockSpec(bs, lambda h, n, _im=im: _im(h)) for (_, bs, im, _) in hparams]
                  + [_whole(p) for p in sparams] + s_in),
        out_specs=([pl.BlockSpec(bs, im) for (_, _, bs, im, _) in outs]
                   + [pl.BlockSpec((hb, None) + tuple(state_shape), lambda h, n: (h, n) + (0,) * nst)] + s_out),
        out_shape=([jax.ShapeDtypeStruct(fs, dt) for (fs, dt, _, _, _) in outs]
                   + [jax.ShapeDtypeStruct((nblk * hb, nc) + tuple(state_shape), F32)] + s_shape),
        scratch_shapes=[pltpu.VMEM((hb,) + tuple(state_shape), F32)] + s_scr,
        compiler_params=_params(("arbitrary", "arbitrary"), side is not None),
    )(*[x[0] for x in seqs], *[x[0] for x in hparams], *sparams, *s_args)
    return res[:no], res[no], res[no + 1:]


def scan_bwd(name, chunk_fn, nblk, hb, nc, seqs, hparams, sparams, state_shape, states, douts, dseqs, side=None):
    ns, nhp, nsp, no = len(seqs), len(hparams), len(sparams), len(douts)
    nst = len(state_shape)

    def core(*refs):
        seq_r, hp_r, sp_r = refs[:ns], refs[ns:ns + nhp], refs[ns + nhp:ns + nhp + nsp]
        base = ns + nhp + nsp
        st_r = refs[base]
        do_r = refs[base + 1:base + 1 + no]
        base += 1 + no
        ds_r, dhp_r, dsp_r = refs[base:base + ns], refs[base + ns:base + ns + nhp], refs[base + ns + nhp:base + ns + nhp + nsp]
        dst_scr = refs[-1]
        h, n = pl.program_id(0), pl.program_id(1)

        @pl.when(n == 0)
        def _():
            dst_scr[...] = jnp.zeros_like(dst_scr)
            for ref in dhp_r:
                ref[...] = jnp.zeros_like(ref)

        @pl.when((n == 0) & (h == 0))
        def _():
            for ref in dsp_r:
                ref[...] = jnp.zeros_like(ref)

        seq_v = [x[...].astype(F32) for x in seq_r]
        hp_v = [x[...] for x in hp_r]
        sp_v = [x[...] for x in sp_r]
        do_v = [x[...].astype(F32) for x in do_r]
        shared = [None] * ns
        dsp_sum = [None] * nsp
        for j in range(hb):
            prim = ([_take(v, s[3], j) for v, s in zip(seq_v, seqs)],
                    [_take(v, s[3], j) for v, s in zip(hp_v, hparams)], sp_v, st_r[j])
            _, vjp = jax.vjp(lambda a, b, c, d: chunk_fn(a, b, c, d), *prim)
            cot = (tuple(_take(v, s[3], j) for v, s in zip(do_v, douts)), dst_scr[j])
            ds, dhp, dsp, dst = vjp(cot)
            for i, (ref, spec, val) in enumerate(zip(ds_r, dseqs, ds)):
                if spec[4] is None:
                    shared[i] = val if shared[i] is None else shared[i] + val
                else:
                    ref[_where(spec[4], j)] = val.astype(ref.dtype)
            for ref, spec, val in zip(dhp_r, hparams, dhp):
                ref[_where(spec[3], j)] += val
            dsp_sum = [val if acc is None else acc + val for acc, val in zip(dsp_sum, dsp)]
            dst_scr[j] = dst
        for ref, val in zip(ds_r, shared):
            if val is not None:
                ref[...] = val.astype(ref.dtype)
        for ref, val in zip(dsp_r, dsp_sum):
            ref[...] += val

    n_in, n_out = ns + nhp + nsp + 1 + no, ns + nhp + nsp
    body, s_in, s_out, s_shape, s_scr, s_args = _with_side(core, n_in, n_out, side, (nblk, nc))
    res = pl.pallas_call(
        body, name=name, grid=(nblk, nc),
        in_specs=([pl.BlockSpec(bs, _flip(im, nc)) for (_, bs, im, _) in seqs]
                  + [pl.BlockSpec(bs, lambda h, n, _im=im: _im(h)) for (_, bs, im, _) in hparams]
                  + [_whole(p) for p in sparams]
                  + [pl.BlockSpec((hb, None) + tuple(state_shape), lambda h, n: (h, nc - 1 - n) + (0,) * nst)]
                  + [pl.BlockSpec(bs, _flip(im, nc)) for (_, bs, im, _) in douts] + s_in),
        out_specs=([pl.BlockSpec(bs, _flip(im, nc)) for (_, _, bs, im, _) in dseqs]
                   + [pl.BlockSpec(bs, lambda h, n, _im=im: _im(h)) for (_, bs, im, _) in hparams]
                   + [_whole(p) for p in sparams] + s_out),
        out_shape=([jax.ShapeDtypeStruct(fs, dt) for (fs, dt, _, _, _) in dseqs]
                   + [jax.ShapeDtypeStruct(x[0].shape, F32) for x in hparams]
                   + [jax.ShapeDtypeStruct(p.shape, F32) for p in sparams] + s_shape),
        scratch_shapes=[pltpu.VMEM((hb,) + tuple(state_shape), F32)] + s_scr,
        compiler_params=_params(("arbitrary", "arbitrary"), side is not None),
    )(*[x[0] for x in seqs], *[x[0] for x in hparams], *sparams, states, *[x[0] for x in douts], *s_args)
    return res[:ns], res[ns:ns + nhp], res[ns + nhp:n_out], res[n_out:]


def _shift_down(x, n, rows):
    if n == 0:
        return x
    return jnp.where(rows >= n, pltpu.roll(x, n, 0), 0.0)


def _shift_up(x, n, rows):
    if n == 0:
        return x
    s = x.shape[0]
    return jnp.where(rows < s - n, pltpu.roll(x, s - n, 0), 0.0)


def conv_fwd(name, x, col0, w, b):
    s, cw = x.shape[0], w.shape[1]

    def body(x_ref, w_ref, b_ref, o_ref):
        xv = x_ref[...]
        rows = _iota(xv.shape, 0)
        u = jnp.broadcast_to(b_ref[...], xv.shape)
        for j in range(CONV_K):
            u = u + w_ref[j:j + 1, :] * _shift_down(xv, CONV_K - 1 - j, rows)
        o_ref[...] = _silu(u)

    return pl.pallas_call(
        body, name=name, grid=(cw // LANES,),
        in_specs=[pl.BlockSpec((s, LANES), lambda j: (0, col0 + j)),
                  pl.BlockSpec((CONV_K, LANES), lambda j: (0, j)),
                  pl.BlockSpec((1, LANES), lambda j: (0, j))],
        out_specs=pl.BlockSpec((s, LANES), lambda j: (0, j)),
        out_shape=jax.ShapeDtypeStruct((s, cw), F32),
        compiler_params=_params(("parallel",)),
    )(x, w, b)


def conv_bwd(name, x, col0, w, b, dout):
    s, cw = x.shape[0], w.shape[1]

    def body(x_ref, w_ref, b_ref, g_ref, dx_ref, dw_ref, db_ref):
        xv = x_ref[...]
        rows = _iota(xv.shape, 0)
        sh = [_shift_down(xv, CONV_K - 1 - j, rows) for j in range(CONV_K)]
        u = jnp.broadcast_to(b_ref[...], xv.shape)
        for j in range(CONV_K):
            u = u + w_ref[j:j + 1, :] * sh[j]
        sg = _sigmoid(u)
        du = g_ref[...] * (sg * (1.0 + u * (1.0 - sg)))
        dx = jnp.zeros_like(xv)
        for j in range(CONV_K):
            dx = dx + w_ref[j:j + 1, :] * _shift_up(du, CONV_K - 1 - j, rows)
            dw_ref[j:j + 1, :] = jnp.sum(du * sh[j], axis=0, keepdims=True)
        dx_ref[...] = dx.astype(dx_ref.dtype)
        db_ref[...] = jnp.sum(du, axis=0, keepdims=True)

    return pl.pallas_call(
        body, name=name, grid=(cw // LANES,),
        in_specs=[pl.BlockSpec((s, LANES), lambda j: (0, col0 + j)),
                  pl.BlockSpec((CONV_K, LANES), lambda j: (0, j)),
                  pl.BlockSpec((1, LANES), lambda j: (0, j)),
                  pl.BlockSpec((s, LANES), lambda j: (0, j))],
        out_specs=[pl.BlockSpec((s, LANES), lambda j: (0, j)),
                   pl.BlockSpec((CONV_K, LANES), lambda j: (0, j)),
                   pl.BlockSpec((1, LANES), lambda j: (0, j))],
        out_shape=[jax.ShapeDtypeStruct((s, cw), BF16), jax.ShapeDtypeStruct((CONV_K, cw), F32),
                   jax.ShapeDtypeStruct((1, cw), F32)],
        compiler_params=_params(("parallel",)),
    )(x, w, b, dout)


def exchange(name, sends, broadcast):
    nop = len(sends)

    def body(*refs):
        start, wait = _exchange_ops(refs[:nop], refs[nop:2 * nop], *refs[2 * nop:], broadcast)
        start()
        wait()

    return pl.pallas_call(
        body, name=name,
        in_specs=[HBM_SPEC] * nop, out_specs=[HBM_SPEC] * nop,
        out_shape=_exchange_out(sends, broadcast), scratch_shapes=_exchange_sems(nop),
        compiler_params=pltpu.CompilerParams(has_side_effects=True),
    )(*sends)


HBM_SPEC = pl.BlockSpec(memory_space=pltpu.HBM)


def _exchange_out(sends, broadcast):
    return [jax.ShapeDtypeStruct((N_DEV,) + tuple(t.shape if broadcast else t.shape[1:]), t.dtype) for t in sends]


def _exchange_sems(nop):
    return [pltpu.SemaphoreType.DMA((nop * (N_DEV - 1),)), pltpu.SemaphoreType.DMA((nop * (N_DEV - 1),)),
            pltpu.SemaphoreType.DMA((nop,))]


def _exchange_ops(send_refs, recv_refs, send_sems, recv_sems, local_sems, broadcast):
    nop = len(send_refs)
    x, y, c = lax.axis_index("x"), lax.axis_index("y"), lax.axis_index("c")
    me = 4 * x + 2 * y + c
    peers = []
    for k in range(1, N_DEV):
        px = 1 - x if (k >> 2) & 1 else x
        py = 1 - y if (k >> 1) & 1 else y
        pc = 1 - c if k & 1 else c
        peers.append(((px, py, pc), 4 * px + 2 * py + pc))

    def src(i, peer):
        return send_refs[i] if broadcast else send_refs[i].at[peer]

    def remote(i, k, arrival):
        dev, peer = peers[k]
        return pltpu.make_async_remote_copy(
            src_ref=src(i, peer), dst_ref=recv_refs[i].at[peer if arrival else me],
            send_sem=send_sems.at[i * (N_DEV - 1) + k], recv_sem=recv_sems.at[i * (N_DEV - 1) + k],
            device_id=dev, device_id_type=pl.DeviceIdType.MESH)

    def local(i):
        return pltpu.make_async_copy(src(i, me), recv_refs[i].at[me], local_sems.at[i])

    def start():
        for i in range(nop):
            local(i).start()
        for k in range(N_DEV - 1):
            for i in range(nop):
                remote(i, k, False).start()

    def wait():
        for k in range(N_DEV - 1):
            for i in range(nop):
                remote(i, k, True).wait_recv()
        for k in range(N_DEV - 1):
            for i in range(nop):
                remote(i, k, False).wait_send()
        for i in range(nop):
            local(i).wait()

    return start, wait


def adamw_sum(name, parts, w, m, v):
    rws, cols = w.shape
    nsum = parts.shape[0]
    tr = _pick(rws, (256, 128, 64, 32, 16, 8))
    c1 = 1.0 / (1.0 - ADAM_B1 ** ADAM_STEP)
    c2 = 1.0 / (1.0 - ADAM_B2 ** ADAM_STEP)

    def body(p_ref, w_ref, m_ref, v_ref, g_ref, d_ref, nm_ref, nv_ref):
        g = p_ref[0]
        for j in range(1, nsum):
            g = g + p_ref[j]
        nm = ADAM_B1 * m_ref[...] + (1.0 - ADAM_B1) * g
        nv = ADAM_B2 * v_ref[...] + (1.0 - ADAM_B2) * (g * g)
        g_ref[...] = g
        nm_ref[...] = nm
        nv_ref[...] = nv
        d_ref[...] = -ADAM_LR * ((nm * c1) / (jnp.sqrt(nv * c2) + ADAM_EPS) + ADAM_WD * w_ref[...])

    blk = pl.BlockSpec((tr, cols), lambda i: (i, 0))
    return pl.pallas_call(
        body, name=name, grid=(rws // tr,),
        in_specs=[pl.BlockSpec((nsum, tr, cols), lambda i: (0, i, 0)), blk, blk, blk],
        out_specs=[blk, blk, blk, blk],
        out_shape=[jax.ShapeDtypeStruct(w.shape, F32)] * 4,
        compiler_params=_params(("parallel",)),
    )(parts, w, m, v)


def ada_fwd(name, c_all, w, b):
    nl = w.shape[0]

    def body(c_ref, w_ref, b_ref, o_ref):
        ca = _silu(c_ref[...])
        for l in range(nl):
            o_ref[l] = mm_nn(ca, w_ref[l]) + b_ref[l]

    return pl.pallas_call(
        body, name=name,
        out_shape=jax.ShapeDtypeStruct((nl, c_all.shape[0], w.shape[2]), F32),
        compiler_params=pltpu.CompilerParams(vmem_limit_bytes=VMEM_LIMIT),
    )(c_all, w, b)


def ada_bwd(name, c_all, dmod):
    nl = dmod.shape[0]

    def body(c_ref, g_ref, o_ref):
        ca = _silu(c_ref[...])
        for l in range(nl):
            o_ref[l] = mm_tn(ca, g_ref[l])

    return pl.pallas_call(
        body, name=name,
        out_shape=jax.ShapeDtypeStruct((nl, c_all.shape[1], dmod.shape[2]), F32),
        compiler_params=pltpu.CompilerParams(vmem_limit_bytes=VMEM_LIMIT),
    )(c_all, dmod)


def lower_bounds_fn(rows, params):
    (lg,), _ = rows, params
    nl = lg.shape[0]
    mx = jnp.max(lg, axis=0, keepdims=True)
    e = jnp.exp(lg - mx)
    p = e / jnp.sum(e, axis=0, keepdims=True)
    layer = _iota((nl, 1), 0)
    acc = jnp.zeros_like(p)
    for j in range(1, nl):
        pj = jnp.sum(jnp.where(layer == j, p, 0.0), axis=0, keepdims=True)
        acc = acc + jnp.where(layer >= j, 1.0, 0.0) * pj
    return (acc,)


def loss_call(name, x, tgt, nw, tm):
    s, d = x.shape

    def body(x_ref, t_ref, w_ref, l_ref, dx_ref, dw_ref):
        def f(xv, wv):
            err = _rms(xv, wv) - t_ref[...]
            return jnp.sum(0.5 * jnp.mean(err * err, axis=-1, keepdims=True), axis=0, keepdims=True)

        val, vjp = jax.vjp(f, x_ref[...], w_ref[...])
        dx, dw = vjp(jnp.ones_like(val))

        @pl.when(pl.program_id(0) == 0)
        def _():
            l_ref[...] = jnp.zeros_like(l_ref)
            dw_ref[...] = jnp.zeros_like(dw_ref)

        l_ref[...] += jnp.broadcast_to(val, l_ref.shape)
        dw_ref[...] += dw
        dx_ref[...] = dx

    row = pl.BlockSpec((tm, d), lambda i: (i, 0))
    return pl.pallas_call(
        body, name=name, grid=(s // tm,),
        in_specs=[row, row, _whole(nw)],
        out_specs=[pl.BlockSpec((8, LANES), lambda i: (0, 0)), row, _whole(nw)],
        out_shape=[jax.ShapeDtypeStruct((8, LANES), F32), jax.ShapeDtypeStruct((s, d), F32),
                   jax.ShapeDtypeStruct(nw.shape, F32)],
        compiler_params=_params(("arbitrary",)),
    )(x, tgt, nw)


class Dims:
    def __init__(self, s, d, ffn):
        self.s, self.d, self.ffn = s, d, ffn
        self.mix = 3 * d // 4
        self.nh = self.mix // HEAD
        self.ssm_heads = self.mix // SSM_P
        self.pairs = self.mix // (2 * SSM_P)
        self.nc = s // CHUNK
        self.conv_ssm = self.mix + 4 * HEAD
        self.conv_w = self.conv_ssm + 3 * self.mix
        self.o_gates = 4 * self.mix
        self.o_sz = self.o_gates + 3 * d
        self.o_gz = self.o_sz + self.mix
        self.o_conv = self.o_gz + self.mix
        self.o_small = self.o_conv + self.conv_w
        used = self.o_small + LANES
        self.np = -(-used // 1280) * 1280
        self.tm = _pick(s, (256, 128, 64))
        mix, nh = self.mix, self.nh
        self.in_sizes = (mix, mix, mix, mix, mix, self.conv_ssm, self.ssm_heads, 3 * mix, mix, nh, nh, 3 * d)
        self.in_width = sum(self.in_sizes)


def w_in_tables(dm, nshard):
    off = np.cumsum((0,) + dm.in_sizes)
    hq, hf, hi, hg, sz, sxbc, sdt, gqkv, gz, gb, ga, gates = (np.arange(off[i], off[i + 1]) for i in range(12))
    hgrn = np.stack([t.reshape(dm.nh, HEAD) for t in (hq, hf, hi, hg)], axis=1).reshape(-1)
    perm = np.concatenate([hgrn, gates, sz, gz, sxbc, gqkv, sdt, gb, ga])
    perm = np.concatenate([perm, np.full(dm.np - perm.size, -1)])
    shard = dm.in_width // nshard
    wpad = -(-shard // LANES) * LANES
    fwd = np.where(perm >= 0, (perm // shard) * wpad + perm % shard, -1)[None]
    inv = np.zeros(dm.in_width, np.int64)
    inv[perm[perm >= 0]] = np.nonzero(perm >= 0)[0]
    bwd = np.full((nshard, wpad), -1)
    bwd[:, :shard] = inv.reshape(nshard, shard)
    return fwd.astype(np.int32), bwd.astype(np.int32)


def _small_views(dm, small):
    t = small.T
    col = lambda a: a[:, :, None]
    row = lambda a: a.reshape(a.shape[0], dm.nc, 1, CHUNK)
    a, b = dm.ssm_heads, dm.ssm_heads + dm.nh
    sdt, gb, ga = t[:a], t[a:b], t[b:b + dm.nh]
    return col(sdt), row(sdt), col(gb), col(ga), row(ga)


def _scan_specs(dm, proj, conv_out, views, lp):
    dt_col, dt_row, gb_col, ga_col, ga_row = views
    mixb, nh = dm.mix // LANES, dm.nh
    s, mix = dm.s, dm.mix
    lane = ("lane", LANES)
    hb = HEADS_PER_STEP
    hw = (CHUNK, hb * LANES)
    hgrn = dict(
        nblk=nh // hb, hb=hb, fn=hgrn_chunk,
        seqs=[(proj, (CHUNK, hb * 4 * HEAD), lambda h, n: (n, h), ("lane", 4 * HEAD))],
        hparams=[(lp["lb"], (1, hb * HEAD), lambda h: (0, h), lane)],
        sparams=[lp["hgrn_norm"]],
        dseqs=[((s, 4 * mix), BF16, (CHUNK, hb * 4 * HEAD), lambda h, n: (n, h), ("lane", 4 * HEAD))],
        io=(hw, lambda h, n: (n, h), lane))
    ppg = dm.pairs // 2
    gw = (CHUNK, ppg * LANES)
    pcol = ((2 * ppg, CHUNK, 1), lambda g, n: (g, n, 0), ("lead", 2))
    prow = ((2 * ppg, None, 1, CHUNK), lambda g, n: (g, n, 0, 0), ("lead", 2))
    ppar = ((2 * ppg, 1, 1), lambda g: (g, 0, 0), ("lead", 2))
    bc = lambda first: ((CHUNK, LANES), lambda g, n: (n, first + g), None)
    ssd = dict(
        nblk=2, hb=ppg, fn=ssd_chunk,
        seqs=[(conv_out, gw, lambda g, n: (n, g), lane), (conv_out,) + bc(mixb), (conv_out,) + bc(mixb + 2),
              (dt_col,) + pcol, (dt_row,) + prow],
        hparams=[(lp["ssm_dt_bias"],) + ppar, (lp["ssm_a_log"],) + ppar],
        sparams=[],
        dseqs=[((s, mix), F32, gw, lambda g, n: (n, g), lane), ((s, 2 * LANES), F32) + bc(0), ((s, 2 * LANES), F32) + bc(0),
               (dt_col.shape, F32) + pcol, (dt_row.shape, F32) + prow],
        io=(gw, lambda g, n: (n, g), lane))
    cq, cgz = dm.conv_ssm // LANES, dm.o_gz // LANES
    assert cq % hb == 0 and nh % hb == 0 and cgz % hb == 0
    hcol = ((hb, CHUNK, 1), lambda h, n: (h, n, 0), ("idx",))
    hrow = ((hb, None, 1, CHUNK), lambda h, n: (h, n, 0, 0), ("idx",))
    hpar = ((hb, 1, 1), lambda h: (h, 0, 0), ("idx",))
    at = lambda first: (hw, lambda h, n: (n, first // hb + h), lane)
    gdn = dict(
        nblk=nh // hb, hb=hb, fn=gdn_chunk,
        seqs=[(conv_out,) + at(cq), (conv_out,) + at(cq + nh), (conv_out,) + at(cq + 2 * nh), (proj,) + at(cgz),
              (gb_col,) + hcol, (ga_col,) + hcol, (ga_row,) + hrow],
        hparams=[(lp["gdn_dt_bias"],) + hpar, (lp["gdn_a_log"],) + hpar],
        sparams=[lp["gdn_norm"]],
        dseqs=[((s, mix), F32) + at(0), ((s, mix), F32) + at(0), ((s, mix), F32) + at(0), ((s, mix), BF16) + at(0),
               (gb_col.shape, F32) + hcol, (ga_col.shape, F32) + hcol, (ga_row.shape, F32) + hrow],
        io=(hw, lambda h, n: (n, h), lane))
    return hgrn, ssd, gdn


def _run_scan_fwd(dm, name, sp, side=None):
    out = ((dm.s, dm.mix), F32) + sp["io"]
    (y,), states, arrived = scan_fwd(name, sp["fn"], sp["nblk"], sp["hb"], dm.nc, sp["seqs"], sp["hparams"],
                                     sp["sparams"], (HEAD, HEAD), [out], side)
    return y, states, arrived


def _run_scan_bwd(dm, name, sp, states, dy, side=None):
    return scan_bwd(name, sp["fn"], sp["nblk"], sp["hb"], dm.nc, sp["seqs"], sp["hparams"], sp["sparams"], (HEAD, HEAD),
                    states, [(dy,) + sp["io"]], sp["dseqs"], side)


def layer_fwd(dm, l, x, lp, side=None):
    tm, d, mix = dm.tm, dm.d, dm.mix
    tag = f"l{l}_"
    (h,) = rowstage_fwd(tag + "norm1", normmod_fn, [(x, d, 0)], [lp["norm_mix"], lp["sc1"], lp["sh1"]], [(d, BF16)], tm)
    proj = matmul(tag + "proj", h, lp["w_in"], "nn", F32)
    conv_out = conv_fwd(tag + "conv", proj, dm.o_conv // LANES, lp["conv_w"], lp["conv_b"])
    small = proj[:, dm.o_small:dm.o_small + LANES]
    views = _small_views(dm, small)
    hg, sd, gd = _scan_specs(dm, proj, conv_out, views, lp)
    yh, st_h, _ = _run_scan_fwd(dm, tag + "hgrn", hg)
    y_ssd, st_s, _ = _run_scan_fwd(dm, tag + "ssd", sd)
    yg, st_g, arrived = _run_scan_fwd(dm, tag + "gdn", gd, side)
    (ys,) = rowstage_fwd(tag + "ssmpost", ssmpost_fn,
                         [(y_ssd, mix, 0), (conv_out, mix, 0), (proj, mix, dm.o_sz // mix)],
                         [lp["ssm_d_exp"], lp["ssm_norm"]], [(mix, F32)], tm)
    (merged,) = rowstage_fwd(tag + "merge", merge_fn, [(yh, mix, 0), (ys, mix, 0), (yg, mix, 0), (proj, 3 * d, 1)],
                             [lp["b_merge"], lp["w_branch"]], [(d, BF16)], tm)
    (x1,) = rowstage_fwd(tag + "outproj", outproj_fn, [(merged, d, 0), (x, d, 0)], [lp["g1"], lp["w_out"]], [(d, F32)], tm)
    (h2,) = rowstage_fwd(tag + "norm2", normmod_fn, [(x1, d, 0)], [lp["norm_ffn"], lp["sc2"], lp["sh2"]], [(d, BF16)], tm)
    gu = bmatmul(tag + "ffn_in", h2, lp["w_ffn_in"], "nn", F32, True)
    gu = gu.reshape((2, gu.shape[0] // 2) + gu.shape[1:])
    act = swiglu3_fwd(tag + "swiglu", gu, tm)
    o2 = bmatmul(tag + "ffn_out", act, lp["w_ffn_out"], "nn", F32, False)
    (x2,) = rowstage_fwd(tag + "resid", resid_fn, [(x1, d, 0), (o2, d, 0)], [lp["g2"]], [(d, F32)], tm)
    saved = dict(x=x, h=h, proj=proj, conv_out=conv_out, views=views, yh=yh, y_ssd=y_ssd, yg=yg, ys=ys,
                 st_h=st_h, st_s=st_s, st_g=st_g, merged=merged, x1=x1, h2=h2, gu=gu, act=act, o2=o2)
    return x2, saved, arrived


def layer_bwd(dm, l, dx2, lp, sv, side=None):
    tm, d, mix, s = dm.tm, dm.d, dm.mix, dm.s
    tag = f"l{l}_b_"
    g = {}
    (dx1_a, do2), (g["g2"],) = rowstage_bwd(tag + "resid", resid_fn, [(sv["x1"], d, 0), (sv["o2"], d, 0)], [lp["g2"]],
                                            [dx2], [F32, BF16], tm)
    dact = bmatmul(tag + "ffn_out_dx", do2, lp["w_ffn_out"], "nt", BF16, True)
    g["w_ffn_out"] = bmatmul(tag + "ffn_out_dw", sv["act"], do2, "tn", F32, True)
    dgu = swiglu3_bwd(tag + "swiglu", sv["gu"], dact, tm)
    dgu = dgu.reshape((-1,) + dgu.shape[2:])
    dh2 = bmatmul(tag + "ffn_in_dx", dgu, lp["w_ffn_in"], "nt", BF16, False)
    g["w_ffn_in"] = bmatmul(tag + "ffn_in_dw", sv["h2"], dgu, "tn", F32, True)
    (dx1,), (g["norm_ffn"], g["sc2"], g["sh2"]) = rowstage_bwd(
        tag + "norm2", normmod_fn, [(sv["x1"], d, 0)], [lp["norm_ffn"], lp["sc2"], lp["sh2"]], [dh2], [F32], tm,
        adds={0: dx1_a})
    (dmerged, dx_a), (g["g1"], g["w_out"]) = rowstage_bwd(
        tag + "outproj", outproj_fn, [(sv["merged"], d, 0), (sv["x"], d, 0)], [lp["g1"], lp["w_out"]], [dx1],
        [BF16, F32], tm)
    proj, conv_out = sv["proj"], sv["conv_out"]
    (dyh, dys, dyg, dgates), (g["b_merge"], g["w_branch"]) = rowstage_bwd(
        tag + "merge", merge_fn, [(sv["yh"], mix, 0), (sv["ys"], mix, 0), (sv["yg"], mix, 0), (proj, 3 * d, 1)],
        [lp["b_merge"], lp["w_branch"]], [dmerged], [F32, F32, F32, BF16], tm)
    (dy_ssd, dxs_a, dsz), (g["ssm_d_exp"], g["ssm_norm"]) = rowstage_bwd(
        tag + "ssmpost", ssmpost_fn, [(sv["y_ssd"], mix, 0), (conv_out, mix, 0), (proj, mix, dm.o_sz // mix)],
        [lp["ssm_d_exp"], lp["ssm_norm"]], [dys], [F32, F32, BF16], tm)
    hg, sd, gd = _scan_specs(dm, proj, conv_out, sv["views"], lp)
    (dhgrn,), (g["lb"],), (g["hgrn_norm"],), _ = _run_scan_bwd(dm, tag + "hgrn", hg, sv["st_h"], dyh)
    (dxs_b, dbp, dcp, d_dt_col, d_dt_row), (g["ssm_dt_bias"], g["ssm_a_log"]), _, _ = _run_scan_bwd(
        dm, tag + "ssd", sd, sv["st_s"], dy_ssd)
    (dq, dk, dv, dgz, d_gb_col, d_ga_col, d_ga_row), (g["gdn_dt_bias"], g["gdn_a_log"]), (g["gdn_norm"],), arrived = _run_scan_bwd(
        dm, tag + "gdn", gd, sv["st_g"], dyg, side)
    dconv = jnp.concatenate([dxs_a + dxs_b, dbp, dcp, dq, dk, dv], axis=1)
    dpc, g["conv_w"], g["conv_b"] = conv_bwd(tag + "conv", proj, dm.o_conv // LANES, lp["conv_w"], lp["conv_b"], dconv)
    unrow = lambda t: t.reshape(t.shape[0], s).T
    dsmall = jnp.concatenate([d_dt_col[:, :, 0].T + unrow(d_dt_row), d_gb_col[:, :, 0].T,
                              d_ga_col[:, :, 0].T + unrow(d_ga_row)], axis=1)
    pad = jnp.zeros((s, dm.np - dm.o_small - dsmall.shape[1]), BF16)
    dproj = jnp.concatenate([dhgrn, dgates, dsz, dgz, dpc, dsmall.astype(BF16), pad], axis=1)
    dh = matmul(tag + "proj_dx", dproj, lp["w_in"], "nt", BF16)
    g["w_in"] = matmul(tag + "proj_dw", sv["h"], dproj, "tn", F32)
    (dx,), (g["norm_mix"], g["sc1"], g["sh1"]) = rowstage_bwd(
        tag + "norm1", normmod_fn, [(sv["x"], d, 0)], [lp["norm_mix"], lp["sc1"], lp["sh1"]], [dh], [F32], tm,
        adds={0: dx_a})
    return dx, g, arrived


WEIGHTS = ("w_ada", "b_ada", "norm_mix", "norm_ffn", "w_in", "b_merge", "hgrn_lb_logits", "hgrn_norm", "ssm_conv_w",
           "ssm_conv_b", "ssm_dt_bias", "ssm_a_log", "ssm_d", "ssm_norm", "gdn_conv_w", "gdn_dt_bias", "gdn_a_log",
           "gdn_norm", "w_branch", "w_out", "w_ffn_in", "w_ffn_out", "norm_final")
GATHERED = ("w_in", "w_branch", "w_out", "w_ffn_in", "w_ffn_out")
PACKET = ("b_ada", "norm_mix", "norm_ffn", "b_merge", "hgrn_norm", "ssm_conv_b", "ssm_dt_bias", "ssm_a_log", "ssm_d",
          "ssm_norm", "gdn_dt_bias", "gdn_a_log", "gdn_norm", "norm_final")
MISC = ("hgrn_lb_logits", "ssm_conv_w", "gdn_conv_w")


def _pack(arrs, dtype, row_mult, lead=0):
    flat = jnp.concatenate([t.reshape(t.shape[:lead] + (-1,)).astype(dtype) for t in arrs], axis=lead)
    n = flat.shape[-1]
    unit = row_mult * LANES
    tot = -(-n // unit) * unit
    flat = jnp.pad(flat, [(0, 0)] * lead + [(0, tot - n)])
    return flat.reshape(flat.shape[:lead] + (tot // LANES, LANES))


def _unpack(packed, shapes, lead=0):
    flat = packed.reshape(packed.shape[:lead] + (-1,))
    out, off = [], 0
    for shp in shapes:
        n = int(np.prod(shp))
        out.append(flat[..., off:off + n].reshape(flat.shape[:lead] + tuple(shp)))
        off += n
    return out


def _shard2d(t):
    return t.reshape((-1, t.shape[-1]))


def weights_from_shards(dm, l, got, idx):
    w_in, wb, w_out, wf, wfo = got
    d, mix = dm.d, dm.mix
    return dict(
        w_in=colgather(f"l{l}_w_in", w_in, idx, dm.np, BF16)[0],
        w_branch=wb.reshape(N_DEV, 3, mix, d // N_DEV).transpose(1, 2, 0, 3).reshape(3, mix, d),
        w_out=w_out.reshape(d, d), w_ffn_in=wf, w_ffn_out=wfo.reshape(N_DEV // 2, -1, d))


def shards_of_grads(dm, l, g, idx):
    d, mix = dm.d, dm.mix
    return [colgather(f"l{l}_g_w_in", g["w_in"][None], idx, dm.in_width // N_DEV, F32),
            g["w_branch"].reshape(3, mix, N_DEV, d // N_DEV).transpose(2, 0, 1, 3).reshape(N_DEV, 3 * mix, d // N_DEV),
            g["w_out"].reshape(N_DEV, d // N_DEV, d), g["w_ffn_in"], g["w_ffn_out"].reshape(N_DEV, -1, d)]


def layer_params(dm, l, full, small, mod_l, lb_l):
    d, mix = dm.d, dm.mix
    row = lambda t: t.reshape(1, -1)
    head = lambda t: t.reshape(-1, 1, 1)
    sh1, sc1, g1, sh2, sc2, g2 = (row(mod_l[i * d:(i + 1) * d]) for i in range(6))
    conv_b = jnp.concatenate([small["ssm_conv_b"][l], jnp.zeros((3 * mix,), F32)])
    return dict(
        w_in=full["w_in"], w_branch=full["w_branch"], w_out=full["w_out"],
        w_ffn_in=full["w_ffn_in"], w_ffn_out=full["w_ffn_out"],
        norm_mix=row(small["norm_mix"][l]), norm_ffn=row(small["norm_ffn"][l]), b_merge=row(small["b_merge"][l]),
        hgrn_norm=row(small["hgrn_norm"][l]), lb=row(lb_l),
        conv_w=jnp.concatenate([small["ssm_conv_w"][l], small["gdn_conv_w"][l]], axis=1), conv_b=row(conv_b),
        ssm_dt_bias=head(small["ssm_dt_bias"][l]), ssm_a_log=head(small["ssm_a_log"][l]),
        ssm_d_exp=row(jnp.repeat(small["ssm_d"][l], SSM_P)), ssm_norm=row(small["ssm_norm"][l]),
        gdn_dt_bias=head(small["gdn_dt_bias"][l]), gdn_a_log=head(small["gdn_a_log"][l]), gdn_norm=row(small["gdn_norm"][l]),
        sh1=sh1, sc1=sc1, g1=g1, sh2=sh2, sc2=sc2, g2=g2)


def layer_grads(dm, g):
    cs = dm.conv_ssm
    out = dict(
        w_in=g["w_in"], w_branch=g["w_branch"], w_out=g["w_out"], w_ffn_in=g["w_ffn_in"],
        w_ffn_out=g["w_ffn_out"], norm_mix=g["norm_mix"][0], norm_ffn=g["norm_ffn"][0], b_merge=g["b_merge"][0],
        hgrn_norm=g["hgrn_norm"][0], ssm_conv_w=g["conv_w"][:, :cs], gdn_conv_w=g["conv_w"][:, cs:],
        ssm_conv_b=g["conv_b"][0, :cs], ssm_dt_bias=g["ssm_dt_bias"][:, 0, 0], ssm_a_log=g["ssm_a_log"][:, 0, 0],
        ssm_d=g["ssm_d_exp"].reshape(dm.ssm_heads, SSM_P).sum(axis=1), ssm_norm=g["ssm_norm"][0],
        gdn_dt_bias=g["gdn_dt_bias"][:, 0, 0], gdn_a_log=g["gdn_a_log"][:, 0, 0], gdn_norm=g["gdn_norm"][0])
    dmod = jnp.concatenate([g[k][0] for k in ("sh1", "sc1", "g1", "sh2", "sc2", "g2")])
    return out, dmod, g["lb"][0]


def local_step(dm, nl, x, tgt, norm_final, params_of, gather_of=None, scatter_of=None):
    arrived = exchange("gather_w0", gather_of(0), True) if gather_of else None
    lps, saved = [], []
    for l in range(nl):
        lps.append(params_of(l, arrived))
        side = (gather_of(l + 1), True) if gather_of and l + 1 < nl else None
        x, sv, arrived = layer_fwd(dm, l, x, lps[l], side)
        saved.append(sv)
    loss, dx, dnf = loss_call("loss", x, tgt, norm_final, dm.tm)
    grads, parts, side = [None] * nl, [None] * nl, None
    for l in reversed(range(nl)):
        dx, grads[l], got = layer_bwd(dm, l, dx, lps[l], saved[l], side)
        if side is not None:
            parts[l + 1] = got
        side = (scatter_of(l, grads[l]), False) if scatter_of else None
    if side is not None:
        parts[0] = exchange("scatter_g0", side[0], False)
    return loss, dx, dnf, grads, parts


def kernel(x, c, w_ada, b_ada, norm_mix, norm_ffn, w_in, b_merge, hgrn_lb_logits, hgrn_norm, ssm_conv_w, ssm_conv_b, ssm_dt_bias, ssm_a_log, ssm_d, ssm_norm, gdn_conv_w, gdn_dt_bias, gdn_a_log, gdn_norm, w_branch, w_out, w_ffn_in, w_ffn_out, norm_final, loss_target, m_w_ada, m_b_ada, m_norm_mix, m_norm_ffn, m_w_in, m_b_merge, m_hgrn_lb_logits, m_hgrn_norm, m_ssm_conv_w, m_ssm_conv_b, m_ssm_dt_bias, m_ssm_a_log, m_ssm_d, m_ssm_norm, m_gdn_conv_w, m_gdn_dt_bias, m_gdn_a_log, m_gdn_norm, m_w_branch, m_w_out, m_w_ffn_in, m_w_ffn_out, m_norm_final, v_w_ada, v_b_ada, v_norm_mix, v_norm_ffn, v_w_in, v_b_merge, v_hgrn_lb_logits, v_hgrn_norm, v_ssm_conv_w, v_ssm_conv_b, v_ssm_dt_bias, v_ssm_a_log, v_ssm_d, v_ssm_norm, v_gdn_conv_w, v_gdn_dt_bias, v_gdn_a_log, v_gdn_norm, v_w_branch, v_w_out, v_w_ffn_in, v_w_ffn_out, v_norm_final):
    a = dict(locals())
    x, tgt = a["x"][0], a["loss_target"][0]
    s, d = x.shape
    nl = a["w_ada"].shape[0]
    dm = Dims(s, d, a["w_ffn_out"].shape[1] * N_DEV)
    me = 4 * lax.axis_index("x") + 2 * lax.axis_index("y") + lax.axis_index("c")

    first = [a["c"], a["ssm_conv_w"], a["gdn_conv_w"]]
    c_all, scw, gcw = _unpack(exchange("gather_c", [_pack(first, F32, 8)], True)[0], [t.shape for t in first], lead=1)
    small = dict(a, ssm_conv_w=scw.transpose(1, 2, 0, 3).reshape(scw.shape[1:3] + (-1,)),
                 gdn_conv_w=gcw.transpose(1, 2, 0, 3).reshape(gcw.shape[1:3] + (-1,)))
    c_pad = jnp.zeros((LANES, d), F32).at[:N_DEV].set(c_all.reshape(N_DEV, d))
    ncol = a["w_ada"].shape[2]
    b_mine = lax.dynamic_slice(a["b_ada"], (0, me * ncol), (nl, ncol))[:, None, :]
    mod_part = ada_fwd("ada_fwd", c_pad, a["w_ada"], b_mine)[:, :N_DEV, :]
    (mod,) = exchange("a2a_mod", [mod_part.transpose(1, 0, 2)], False)
    mod = mod.transpose(1, 0, 2).reshape(nl, N_DEV * ncol)
    (lb,) = rowstage_fwd("lower_bounds", lower_bounds_fn, [(a["hgrn_lb_logits"], dm.mix, 0)], [], [(dm.mix, F32)], nl)

    idx_fwd, idx_bwd = w_in_tables(dm, N_DEV)
    loss, dx, dnf, grads, parts = local_step(
        dm, nl, x, tgt, a["norm_final"].reshape(1, d),
        params_of=lambda l, got: layer_params(dm, l, weights_from_shards(dm, l, got, idx_fwd), small, mod[l], lb[l]),
        gather_of=lambda l: [_shard2d(a[n][l]).astype(BF16) for n in GATHERED],
        scatter_of=lambda l, g: shards_of_grads(dm, l, g, idx_bwd))

    per_layer = [layer_grads(dm, g) for g in grads]
    res = {}
    for l in range(nl):
        for n, p in zip(GATHERED, parts[l]):
            outs = adamw_sum(f"adamw_l{l}_{n}", p, *[_shard2d(a[q + n][l]) for q in ("", "m_", "v_")])
            for kind, o in zip(("grad", "delta", "new_m", "new_v"), outs):
                res.setdefault((kind, n), []).append(o.reshape(a[n].shape[1:]))
    for key in list(res):
        res[key] = jnp.stack(res[key])

    stackg = lambda n: jnp.stack([pl_[0][n] for pl_ in per_layer])
    dmod = jnp.stack([pl_[1] for pl_ in per_layer])
    dlb = jnp.stack([pl_[2] for pl_ in per_layer])
    pk_g = [dmod if n == "b_ada" else dnf if n == "norm_final" else stackg(n) for n in PACKET]
    extra = [dlb, stackg("ssm_conv_w"), stackg("gdn_conv_w"), loss[0, :1]]
    pk_shapes = [t.shape for t in pk_g + extra]
    zeros = [jnp.zeros(t.shape, F32) for t in extra]
    (parts,) = exchange("gather_small", [_pack(pk_g + extra, F32, 8)], True)
    outs = adamw_sum("adamw_small", parts, *[_pack([a[p + n] for n in PACKET] + zeros, F32, 8) for p in ("", "m_", "v_")])
    for kind, o in zip(("grad", "delta", "new_m", "new_v"), outs):
        un = _unpack(o, pk_shapes)
        for n, t in zip(PACKET, un):
            res[(kind, n)] = t.reshape(a[n].shape)
        if kind == "grad":
            dlb_sum, g_scw, g_gcw, loss_sum = un[len(PACKET):]

    (g_lb,), _ = rowstage_bwd("lower_bounds_b", lower_bounds_fn, [(a["hgrn_lb_logits"], dm.mix, 0)], [], [dlb_sum], [F32], nl)
    mine = lambda t, n: lax.dynamic_slice_in_dim(t, me * a[n].shape[-1], a[n].shape[-1], axis=t.ndim - 1)
    (dmod_cols,) = exchange("a2a_dmod", [dmod.reshape(nl, N_DEV, ncol).transpose(1, 0, 2)], False)
    dmod_pad = jnp.zeros((nl, LANES, ncol), F32).at[:, :N_DEV].set(dmod_cols.transpose(1, 0, 2))
    g_w_ada = ada_bwd("ada_bwd", c_pad, dmod_pad)
    outs = adamw_sum("adamw_w_ada", g_w_ada.reshape(1, nl * d, ncol), *[a[q + "w_ada"].reshape(nl * d, ncol) for q in ("", "m_", "v_")])
    for kind, o in zip(("grad", "delta", "new_m", "new_v"), outs):
        res[(kind, "w_ada")] = o.reshape(nl, d, ncol)
    g_misc = [g_lb, mine(g_scw, "ssm_conv_w"), mine(g_gcw, "gdn_conv_w")]
    outs = adamw_sum("adamw_misc", _pack(g_misc, F32, 8)[None], *[_pack([a[q + n] for n in MISC], F32, 8) for q in ("", "m_", "v_")])
    for kind, o in zip(("grad", "delta", "new_m", "new_v"), outs):
        for n, t in zip(MISC, _unpack(o, [a[n].shape for n in MISC])):
            res[(kind, n)] = t

    out = [loss_sum.reshape(()), dx[None]]
    for kind in ("grad", "delta", "new_m", "new_v"):
        out += [res[(kind, n)] for n in WEIGHTS]
    return tuple(out)
```

```python
import functools
import math

import numpy as np
import jax
import jax.numpy as jnp
from jax import lax
from jax.experimental import pallas as pl
from jax.experimental.pallas import tpu as pltpu

F32 = jnp.float32
BF16 = jnp.bfloat16

N_DEV = 8
CHUNK = 64
SUB = 16
HGRN_HEADS_PER_STEP = 2
GDN_HEADS_PER_STEP = 6
HEAD = 128
SSM_P = 64
CONV_K = 4
F_MIN = 1e-30
NORM_EPS = 1e-6
LANES = 128
VMEM_LIMIT = 56 * 1024 * 1024

ADAM_LR = 0.001
ADAM_B1 = 0.9
ADAM_B2 = 0.999
ADAM_EPS = 1e-08
ADAM_WD = 0.01
ADAM_STEP = 10


def _dg(a, b, ca, cb):
    return lax.dot_general(a.astype(BF16), b.astype(BF16), (((ca,), (cb,)), ((), ())),
                           preferred_element_type=F32)


def _split3(x):
    x1 = x.astype(BF16)
    r = x - x1.astype(F32)
    x2 = r.astype(BF16)
    x3 = (r - x2.astype(F32)).astype(BF16)
    return x1, x2, x3


def _hdg(a, b, ca, cb):
    a1, a2, _ = _split3(a)
    b1, b2, _ = _split3(b)
    dn = (((ca,), (cb,)), ((), ()))
    d = lambda p, q: lax.dot_general(p, q, dn, preferred_element_type=F32)
    return (d(a2, b1) + d(a1, b2)) + d(a1, b1)


def _dot_family(prim):
    @jax.custom_vjp
    def nn(a, b):
        return prim(a, b, 1, 0)

    @jax.custom_vjp
    def nt(a, b):
        return prim(a, b, 1, 1)

    @jax.custom_vjp
    def tn(a, b):
        return prim(a, b, 0, 0)

    nn.defvjp(lambda a, b: (nn(a, b), (a, b)), lambda r, g: (nt(g, r[1]), tn(r[0], g)))
    nt.defvjp(lambda a, b: (nt(a, b), (a, b)), lambda r, g: (nn(g, r[1]), tn(g, r[0])))
    tn.defvjp(lambda a, b: (tn(a, b), (a, b)), lambda r, g: (nt(r[1], g), nn(r[0], g)))
    return nn, nt, tn


mm_nn, mm_nt, mm_tn = _dot_family(_dg)
hd_nn, hd_nt, hd_tn = _dot_family(_hdg)


def _iota(shape, dim):
    return lax.broadcasted_iota(jnp.int32, shape, dim)


def _scan_rows(x, reverse):
    n = x.shape[0]
    rows = _iota(x.shape, 0)
    k = 1
    while k < n:
        if reverse:
            x = x + jnp.where(rows < n - k, pltpu.roll(x, n - k, 0), 0.0)
        else:
            x = x + jnp.where(rows >= k, pltpu.roll(x, k, 0), 0.0)
        k *= 2
    return x


@jax.custom_vjp
def cumsum_rows(x):
    return _scan_rows(x, False)


cumsum_rows.defvjp(lambda x: (_scan_rows(x, False), None), lambda _, g: (_scan_rows(g, True),))


def _sigmoid(x):
    return jax.nn.sigmoid(x)


def _silu(x):
    return x * jax.nn.sigmoid(x)


def _softplus(x):
    e = jnp.exp(-jnp.abs(x))
    small = e * (1.0 - e * (0.5 - e * (1.0 / 3.0)))
    return jnp.maximum(x, 0.0) + jnp.where(e < 1e-3, small, jnp.log(1.0 + e))


def _masked_exp(diff, mask):
    return jnp.where(mask, jnp.exp(jnp.where(mask, diff, 0.0)), 0.0)


def _rms(x, w):
    return x * lax.rsqrt(jnp.mean(x * x, axis=-1, keepdims=True) + NORM_EPS) * w


def _cum_col_row(lg_col, lg_row):
    c = lg_col.shape[0]
    r, s = _iota((c, c), 0), _iota((c, c), 1)
    cum_col = jnp.sum(jnp.where(s <= r, jnp.broadcast_to(lg_row, (c, c)), 0.0), axis=1, keepdims=True)
    cum_row = jnp.sum(jnp.where(r <= s, jnp.broadcast_to(lg_col, (c, c)), 0.0), axis=0, keepdims=True)
    total = jnp.sum(lg_col, axis=0, keepdims=True)
    return cum_col, cum_row, total


def hgrn_chunk(seq, hp, sp, st):
    (blk,), (lb,), (nw,) = seq, hp, sp
    c = blk.shape[0]
    q_raw, f_raw, v, g_raw = (blk[:, i * HEAD:(i + 1) * HEAD] for i in range(4))
    q = _silu(q_raw)
    f = lb + (1.0 - lb) * _sigmoid(f_raw)
    logf = jnp.log(jnp.maximum(f, F_MIN))
    k = (1.0 - lb) * _sigmoid(-f_raw)
    b = cumsum_rows(logf)
    o_inter = mm_nt(q * jnp.exp(b), st)
    nsub = c // SUB
    wide = (SUB, SUB, HEAD)
    er = _iota((SUB * SUB, SUB), 0)
    e_t = (er // SUB == _iota((SUB * SUB, SUB), 1)).astype(F32)
    pr = _iota((SUB * SUB, 1), 0)
    pmask = (pr % SUB) <= (pr // SUB)
    er64 = _iota((SUB * SUB, c), 0)
    ec64 = _iota((SUB * SUB, c), 1)
    rows_c = _iota((c, 1), 0)
    row = lambda a, i: jnp.sum(jnp.where(rows_c == i, a, 0.0), axis=0, keepdims=True)
    parts = []
    for i in range(nsub):
        sl = slice(SUB * i, SUB * (i + 1))
        qi, ki, bi = q[sl], k[sl], b[sl]
        qb = jnp.broadcast_to(qi[:, None, :], wide).reshape(SUB * SUB, HEAD)
        kb = jnp.broadcast_to(ki[None, :, :], wide).reshape(SUB * SUB, HEAD)
        bd = (bi[:, None, :] - bi[None, :, :]).reshape(SUB * SUB, HEAD)
        sc_col = jnp.sum(qb * kb * _masked_exp(bd, pmask), axis=1, keepdims=True)
        place = (ec64 == (er64 % SUB) + SUB * i).astype(F32)
        sc = mm_tn(e_t, sc_col * place)
        if i > 0:
            bref = row(b, SUB * i)
            qt = qi * jnp.exp(bi - bref)
            kt = k * _masked_exp(bref - b, rows_c < SUB * i)
            sc = sc + mm_nt(qt, kt)
        parts.append(mm_nn(sc, v))
    o = o_inter + jnp.concatenate(parts, axis=0)
    bend = row(b, c - 1)
    st_new = st * jnp.exp(bend) + mm_tn(v, k * jnp.exp(bend - b))
    y = _rms(o, nw) * _silu(g_raw)
    return (y,), st_new


def ssd_chunk(seq, hp, sp, st):
    xs, bm, cm, dtc, dtr = seq
    dt_bias, a_log = hp
    c = xs.shape[0]
    lane = _iota((1, 2 * SSM_P), 1)
    first = lane < SSM_P
    r, s = _iota((c, c), 0), _iota((c, c), 1)
    g = mm_nt(cm, bm)
    dts, cums, ends, segs = [], [], [], []
    for i in range(2):
        neg_a = -jnp.exp(a_log[i])
        dt_col = _softplus(dtc[i] + dt_bias[i])
        dt_row = _softplus(dtr[i] + dt_bias[i])
        cum_col, cum_row, total = _cum_col_row(neg_a * dt_col, neg_a * dt_row)
        dts.append(dt_col)
        cums.append(cum_col)
        ends.append(total)
        segs.append(_masked_exp(cum_col - cum_row, s <= r))
    dt_l = jnp.where(first, dts[0], dts[1])
    cum_l = jnp.where(first, cums[0], cums[1])
    end_l = jnp.where(first, ends[0], ends[1])
    xdt = xs * dt_l
    y_intra = (mm_nn(g * segs[0], jnp.where(first, xdt, 0.0))
               + mm_nn(g * segs[1], jnp.where(first, 0.0, xdt)))
    y_inter = mm_nn(cm, st) * jnp.exp(cum_l)
    st_new = st * jnp.exp(end_l) + mm_tn(bm, xdt * jnp.exp(end_l - cum_l))
    return (y_intra + y_inter,), st_new


def _neumann_inverse(a):
    n = a.shape[0]
    eye = (_iota((n, n), 0) == _iota((n, n), 1)).astype(F32)
    p = -a
    t = eye + p
    for _ in range(int(math.log2(n)) - 1):
        p = _hdg(p, p, 1, 0)
        t = t + _hdg(t, p, 1, 0)
    return t


@jax.custom_vjp
def inv_unit_lower(a):
    return _neumann_inverse(a)


def _inv_fwd(a):
    t = _neumann_inverse(a)
    return t, t


inv_unit_lower.defvjp(_inv_fwd, lambda t, g: (-hd_nt(hd_tn(t, g), t),))


def gdn_chunk(seq, hp, sp, st):
    q_raw, k_raw, v, z, gbc, gac, gar = seq
    dt_bias, a_log = hp
    (nw,) = sp
    c = v.shape[0]
    r, s = _iota((c, c), 0), _iota((c, c), 1)
    q = q_raw * lax.rsqrt(jnp.sum(q_raw * q_raw, axis=-1, keepdims=True) + NORM_EPS) * (HEAD ** -0.5)
    k = k_raw * lax.rsqrt(jnp.sum(k_raw * k_raw, axis=-1, keepdims=True) + NORM_EPS)
    beta = _sigmoid(gbc)
    neg_a = -jnp.exp(a_log)
    cum, cum_row, total = _cum_col_row(neg_a * _softplus(gac + dt_bias), neg_a * _softplus(gar + dt_bias))
    decay = _masked_exp(cum - cum_row, s <= r)
    kk = mm_nt(k, k)
    a_low = jnp.where(s < r, beta * kk * decay, 0.0)
    sol = hd_nn(inv_unit_lower(a_low), jnp.concatenate([v * beta, k * (beta * jnp.exp(cum))], axis=1))
    u_base, w_corr = sol[:, :HEAD], sol[:, HEAD:]
    qk = mm_nt(q, k) * decay
    u = u_base - mm_nn(w_corr, st)
    o = mm_nn(q * jnp.exp(cum), st) + mm_nn(qk, u)
    st_new = jnp.exp(total) * st + mm_tn(k * jnp.exp(total - cum), u)
    y = _rms(o, nw) * _silu(z)
    return (y,), st_new


def normmod_fn(rows, params):
    (x,), (nw, sc, sh) = rows, params
    return (_rms(x, nw) * (1.0 + sc) + sh,)


def ssmpost_fn(rows, params):
    (y, xs, z), (d_exp, nw) = rows, params
    y = (y + d_exp * xs) * _silu(z)
    gw = y.shape[1] // 2
    return (jnp.concatenate([_rms(y[:, :gw], nw[:, :gw]), _rms(y[:, gw:], nw[:, gw:])], axis=1),)


def merge_fn(rows, params):
    (yh, ys, yg, gl), (bm, wb) = rows, params
    d = wb.shape[2]
    gates = _sigmoid(gl + bm)
    out = 0.0
    for n, y in enumerate((yh, ys, yg)):
        out = out + gates[:, n * d:(n + 1) * d] * mm_nn(y, wb[n])
    return (out,)


def outproj_fn(rows, params):
    (m, x), (g1, w) = rows, params
    return (x + (1.0 + g1) * mm_nn(m, w),)


def resid_fn(rows, params):
    (x, o), (g2,) = rows, params
    return (x + (1.0 + g2) * o,)


def _params(sem, side_effects=False):
    return pltpu.CompilerParams(dimension_semantics=sem, vmem_limit_bytes=VMEM_LIMIT, has_side_effects=side_effects)


def _whole(a):
    nd = a.ndim
    return pl.BlockSpec(a.shape, lambda *_: (0,) * nd)


def _pick(n, cands):
    for c in cands:
        if n % c == 0:
            return c
    return n


def matmul(name, a, b, mode, out_dtype):
    if mode == "nn":
        (m, k), n = a.shape, b.shape[1]
    elif mode == "nt":
        (m, k), n = a.shape, b.shape[0]
    else:
        (k, m), n = a.shape, b.shape[1]
    tm = _pick(m, (512, 256, 128))
    tn = _pick(n, (1280, 1024, 1408, 768, 512, 384, 256, 128))
    tk = _pick(k, (1024, 1280, 1408, 768, 512, 256, 128))
    if mode == "tn":
        tm = _pick(m, (1024, 768, 512, 256, 128))
        tk = _pick(k, (512, 256, 128))
    nk = k // tk
    ca, cb = {"nn": (1, 0), "nt": (1, 1), "tn": (0, 0)}[mode]

    def body(a_ref, b_ref, o_ref, acc_ref):
        kk = pl.program_id(2)

        @pl.when(kk == 0)
        def _():
            acc_ref[...] = jnp.zeros_like(acc_ref)

        acc_ref[...] += _dg(a_ref[...], b_ref[...], ca, cb)

        @pl.when(kk == nk - 1)
        def _():
            o_ref[...] = acc_ref[...].astype(o_ref.dtype)

    a_spec = (pl.BlockSpec((tk, tm), lambda i, j, q: (q, i)) if mode == "tn"
              else pl.BlockSpec((tm, tk), lambda i, j, q: (i, q)))
    b_spec = (pl.BlockSpec((tn, tk), lambda i, j, q: (j, q)) if mode == "nt"
              else pl.BlockSpec((tk, tn), lambda i, j, q: (q, j)))
    return pl.pallas_call(
        body, name=name, grid=(m // tm, n // tn, nk),
        in_specs=[a_spec, b_spec],
        out_specs=pl.BlockSpec((tm, tn), lambda i, j, q: (i, j)),
        out_shape=jax.ShapeDtypeStruct((m, n), out_dtype),
        scratch_shapes=[pltpu.VMEM((tm, tn), F32)],
        compiler_params=_params(("parallel", "parallel", "arbitrary")),
    )(a, b)


def bmatmul(name, a, b, mode, out_dtype, out_batched):
    ab, bb = a.ndim == 3, b.ndim == 3
    nb = a.shape[0] if ab else b.shape[0]
    a2, b2 = a.shape[-2:], b.shape[-2:]
    if mode == "nn":
        (m, k), n = a2, b2[1]
    elif mode == "nt":
        (m, k), n = a2, b2[0]
    else:
        (k, m), n = a2, b2[1]
    tm = _pick(m, (1024, 512, 256, 128) if mode == "tn" else (512, 256, 128))
    tn = _pick(n, (1024, 512, 256, 128))
    tk = _pick(k, (512, 256, 128) if mode == "tn" else (1024, 512, 256, 128))
    nk = k // tk
    ca, cb = {"nn": (1, 0), "nt": (1, 1), "tn": (0, 0)}[mode]
    ids = (lambda g: g) if out_batched else (lambda g: (g[2], g[0], g[1], g[3]))
    grid = (nb, m // tm, n // tn, nk) if out_batched else (m // tm, n // tn, nb, nk)

    def a_map(*g):
        bi, i, j, q = ids(g)
        idx = (q, i) if mode == "tn" else (i, q)
        return (bi,) + idx if ab else idx

    def b_map(*g):
        bi, i, j, q = ids(g)
        idx = (j, q) if mode == "nt" else (q, j)
        return (bi,) + idx if bb else idx

    def o_map(*g):
        bi, i, j, q = ids(g)
        return (bi, i, j) if out_batched else (i, j)

    def body(a_ref, b_ref, o_ref, acc_ref):
        bi, _, _, q = ids(tuple(pl.program_id(d) for d in range(4)))
        first = (q == 0) if out_batched else (q == 0) & (bi == 0)
        last = (q == nk - 1) if out_batched else (q == nk - 1) & (bi == nb - 1)

        @pl.when(first)
        def _():
            acc_ref[...] = jnp.zeros_like(acc_ref)

        acc_ref[...] += _dg(a_ref[...], b_ref[...], ca, cb)

        @pl.when(last)
        def _():
            o_ref[...] = acc_ref[...].astype(o_ref.dtype)

    a_blk = (tk, tm) if mode == "tn" else (tm, tk)
    b_blk = (tn, tk) if mode == "nt" else (tk, tn)
    return pl.pallas_call(
        body, name=name, grid=grid,
        in_specs=[pl.BlockSpec(((None,) if ab else ()) + a_blk, a_map), pl.BlockSpec(((None,) if bb else ()) + b_blk, b_map)],
        out_specs=pl.BlockSpec(((None,) if out_batched else ()) + (tm, tn), o_map),
        out_shape=jax.ShapeDtypeStruct(((nb,) if out_batched else ()) + (m, n), out_dtype),
        scratch_shapes=[pltpu.VMEM((tm, tn), F32)],
        compiler_params=_params(("parallel", "parallel", "arbitrary", "arbitrary")),
    )(a, b)


def colgather(name, src, idx, dst_w, out_dtype):
    nsrc, rows, w = src.shape
    nbs = -(-w // LANES)
    ne = idx.shape[0]
    nbd = idx.shape[1] // LANES
    tiles = [sorted(set((idx[e, t * LANES:(t + 1) * LANES][idx[e, t * LANES:(t + 1) * LANES] >= 0] // LANES).tolist()))
             for e in range(ne) for t in range(nbd)]
    nslot = max(1, max(len(t) for t in tiles))
    tbl = np.full((ne * nbd, nslot), -1, np.int32)
    for i, t in enumerate(tiles):
        tbl[i, :len(t)] = t
    exact3 = src.dtype == F32

    def body(tbl_ref, idx_ref, src_ref, o_ref, acc_ref):
        ti, si = pl.program_id(0), pl.program_id(1)

        @pl.when(si == 0)
        def _():
            acc_ref[...] = jnp.zeros_like(acc_ref)

        t = tbl_ref[ti * nslot + si]

        @pl.when(t >= 0)
        def _():
            onehot = ((_iota((LANES, LANES), 0) + t * LANES) == idx_ref[...]).astype(BF16)
            col = _iota((1, LANES), 1) + (t % nbs) * LANES
            xv = jnp.where(col < w, src_ref[...], jnp.zeros((), src_ref.dtype))
            d = lambda p: lax.dot_general(p, onehot, (((1,), (0,)), ((), ())), preferred_element_type=F32)
            if exact3:
                x1, x2, x3 = _split3(xv)
                acc_ref[...] += (d(x3) + d(x2)) + d(x1)
            else:
                acc_ref[...] += d(xv)

        @pl.when(si == nslot - 1)
        def _():
            o_ref[...] = acc_ref[...].astype(o_ref.dtype)

    def src_map(ti, si, tbl_ref):
        t = jnp.maximum(tbl_ref[ti * nslot + si], 0)
        return (t // nbs, 0, t % nbs)

    grid_spec = pltpu.PrefetchScalarGridSpec(
        num_scalar_prefetch=1, grid=(ne * nbd, nslot),
        in_specs=[pl.BlockSpec((None, 1, LANES), lambda ti, si, tbl_ref: (ti // nbd, 0, ti % nbd)),
                  pl.BlockSpec((None, rows, LANES), src_map)],
        out_specs=pl.BlockSpec((None, rows, LANES), lambda ti, si, tbl_ref: (ti // nbd, 0, ti % nbd)),
        scratch_shapes=[pltpu.VMEM((rows, LANES), F32)])
    return pl.pallas_call(
        body, name=name, grid_spec=grid_spec,
        out_shape=jax.ShapeDtypeStruct((ne, rows, dst_w), out_dtype),
        compiler_params=_params(("parallel", "arbitrary")),
    )(jnp.asarray(tbl.reshape(-1)), jnp.asarray(idx.reshape(ne, 1, nbd * LANES).astype(np.int32)), src)


def swiglu3_fwd(name, gu, tm):
    _, nb, s, w = gu.shape

    def body(x_ref, o_ref):
        o_ref[...] = (_silu(x_ref[0]) * x_ref[1]).astype(o_ref.dtype)

    return pl.pallas_call(
        body, name=name, grid=(nb, s // tm),
        in_specs=[pl.BlockSpec((2, None, tm, w), lambda b, i: (0, b, i, 0))],
        out_specs=pl.BlockSpec((None, tm, w), lambda b, i: (b, i, 0)),
        out_shape=jax.ShapeDtypeStruct((nb, s, w), BF16),
        compiler_params=_params(("parallel", "parallel")),
    )(gu)


def swiglu3_bwd(name, gu, dact, tm):
    _, nb, s, w = gu.shape

    def body(x_ref, g_ref, o_ref):
        _, vjp = jax.vjp(lambda a, b: _silu(a) * b, x_ref[0], x_ref[1])
        dg, du = vjp(g_ref[...].astype(F32))
        o_ref[0] = dg.astype(o_ref.dtype)
        o_ref[1] = du.astype(o_ref.dtype)

    return pl.pallas_call(
        body, name=name, grid=(nb, s // tm),
        in_specs=[pl.BlockSpec((2, None, tm, w), lambda b, i: (0, b, i, 0)),
                  pl.BlockSpec((None, tm, w), lambda b, i: (b, i, 0))],
        out_specs=pl.BlockSpec((2, None, tm, w), lambda b, i: (0, b, i, 0)),
        out_shape=jax.ShapeDtypeStruct(gu.shape, BF16),
        compiler_params=_params(("parallel", "parallel")),
    )(gu, dact)


def _row_specs(rows, tm):
    return [pl.BlockSpec((tm, w), lambda i, _c=c: (i, _c)) for (_, w, c) in rows]


def rowstage_fwd(name, fn, rows, params, outs, tm):
    s = rows[0][0].shape[0]
    nr, npar = len(rows), len(params)

    def body(*refs):
        r = [x[...].astype(F32) for x in refs[:nr]]
        p = [x[...].astype(F32) for x in refs[nr:nr + npar]]
        for ref, val in zip(refs[nr + npar:], fn(r, p)):
            ref[...] = val.astype(ref.dtype)

    res = pl.pallas_call(
        body, name=name, grid=(s // tm,),
        in_specs=_row_specs(rows, tm) + [_whole(p) for p in params],
        out_specs=[pl.BlockSpec((tm, w), lambda i: (i, 0)) for (w, _) in outs],
        out_shape=[jax.ShapeDtypeStruct((s, w), dt) for (w, dt) in outs],
        compiler_params=_params(("parallel",)),
    )(*[r[0] for r in rows], *params)
    return res


def rowstage_bwd(name, fn, rows, params, douts, drow_dtypes, tm, adds=None):
    s = rows[0][0].shape[0]
    nr, npar, no = len(rows), len(params), len(douts)
    adds = adds or {}
    add_idx = sorted(adds)
    na = len(add_idx)

    def body(*refs):
        r = [x[...].astype(F32) for x in refs[:nr]]
        p = [x[...].astype(F32) for x in refs[nr:nr + npar]]
        g = [x[...].astype(F32) for x in refs[nr + npar:nr + npar + no]]
        a_refs = refs[nr + npar + no:nr + npar + no + na]
        dr_refs = refs[nr + npar + no + na:nr + npar + no + na + nr]
        dp_refs = refs[nr + npar + no + na + nr:]
        _, vjp = jax.vjp(lambda r_, p_: tuple(fn(r_, p_)), r, p)
        dr, dp = vjp(tuple(g))
        for j, (ref, val) in enumerate(zip(dr_refs, dr)):
            if j in adds:
                val = val + a_refs[add_idx.index(j)][...].astype(F32)
            ref[...] = val.astype(ref.dtype)

        @pl.when(pl.program_id(0) == 0)
        def _():
            for ref in dp_refs:
                ref[...] = jnp.zeros_like(ref)

        for ref, val in zip(dp_refs, dp):
            ref[...] += val

    res = pl.pallas_call(
        body, name=name, grid=(s // tm,),
        in_specs=(_row_specs(rows, tm) + [_whole(p) for p in params]
                  + [pl.BlockSpec((tm, d.shape[1]), lambda i: (i, 0)) for d in douts]
                  + [pl.BlockSpec((tm, rows[j][1]), lambda i: (i, 0)) for j in add_idx]),
        out_specs=([pl.BlockSpec((tm, w), lambda i: (i, 0)) for (_, w, _) in rows] + [_whole(p) for p in params]),
        out_shape=([jax.ShapeDtypeStruct((s, w), dt) for (_, w, _), dt in zip(rows, drow_dtypes)]
                   + [jax.ShapeDtypeStruct(p.shape, F32) for p in params]),
        compiler_params=_params(("arbitrary",)),
    )(*[r[0] for r in rows], *params, *douts, *[adds[j] for j in add_idx])
    return res[:nr], res[nr:]


def _flip(index_map, nc):
    return lambda h, n: index_map(h, nc - 1 - n)


def _with_side(core, n_in, n_out, side, grid):
    if side is None:
        return core, [], [], [], [], ()
    sends, broadcast = side
    k = len(sends)

    def body(*refs):
        ins, snd = refs[:n_in], refs[n_in:n_in + k]
        outs, rcv = refs[n_in + k:n_in + k + n_out], refs[n_in + k + n_out:n_in + 2 * k + n_out]
        scr = refs[n_in + 2 * k + n_out:]
        start, wait = _exchange_ops(snd, rcv, *scr[1:], broadcast)
        ids = [pl.program_id(d) for d in range(len(grid))]
        first = functools.reduce(lambda a, b: a & b, [i == 0 for i in ids])
        last = functools.reduce(lambda a, b: a & b, [i == g - 1 for i, g in zip(ids, grid)])
        pl.when(first)(start)
        core(*ins, *outs, scr[0])
        pl.when(last)(wait)

    return body, [HBM_SPEC] * k, [HBM_SPEC] * k, _exchange_out(sends, broadcast), _exchange_sems(k), tuple(sends)


def _take(v, split, j):
    if split is None:
        return v
    if split[0] == "lane":
        return v[:, j * split[1]:(j + 1) * split[1]]
    if split[0] == "lead":
        return v[j * split[1]:(j + 1) * split[1]]
    return v[j]


def _heads(vals, specs, hb):
    return [v if s[-1] is None else jnp.stack([_take(v, s[-1], j) for j in range(hb)]) for v, s in zip(vals, specs)]


def _over_heads(chunk_fn, hb, seqs, hparams, batched):
    seq_ax = [None if s[3] is None else 0 for s in seqs]
    hp_ax = [None if s[3] is None else 0 for s in hparams]
    if batched:
        return jax.vmap(chunk_fn, in_axes=(seq_ax, hp_ax, None, 0))

    def looped(seq, hp, sp, st):
        pick = lambda vals, axes, j: [v if a is None else v[j] for v, a in zip(vals, axes)]
        res = [chunk_fn(pick(seq, seq_ax, j), pick(hp, hp_ax, j), sp, st[j]) for j in range(hb)]
        pile = lambda parts: jnp.concatenate([p[None] for p in parts], axis=0)
        return tuple(pile(o) for o in zip(*[r[0] for r in res])), pile([r[1] for r in res])

    return looped


def _where(split, j):
    if split[0] == "lane":
        return (slice(None), slice(j * split[1], (j + 1) * split[1]))
    if split[0] == "lead":
        return (slice(j * split[1], (j + 1) * split[1]),)
    return (j,)


def scan_fwd(name, chunk_fn, nblk, hb, nc, seqs, hparams, sparams, state_shape, outs, batched, side=None):
    ns, nhp, nsp, no = len(seqs), len(hparams), len(sparams), len(outs)

    def core(*refs):
        seq_r, hp_r, sp_r = refs[:ns], refs[ns:ns + nhp], refs[ns + nhp:ns + nhp + nsp]
        out_r = refs[ns + nhp + nsp:ns + nhp + nsp + no]
        st_out, st_scr = refs[-2], refs[-1]

        @pl.when(pl.program_id(1) == 0)
        def _():
            st_scr[...] = jnp.zeros_like(st_scr)

        seq_v = [x[...].astype(F32) for x in seq_r]
        hp_v = [x[...] for x in hp_r]
        sp_v = [x[...] for x in sp_r]
        st = st_scr[...]
        st_out[...] = st
        heads = _over_heads(chunk_fn, hb, seqs, hparams, batched)
        o, st_new = heads(_heads(seq_v, seqs, hb), _heads(hp_v, hparams, hb), sp_v, st)
        for ref, spec, val in zip(out_r, outs, o):
            for j in range(hb):
                ref[_where(spec[4], j)] = val[j].astype(ref.dtype)
        st_scr[...] = st_new

    nst = len(state_shape)
    body, s_in, s_out, s_shape, s_scr, s_args = _with_side(core, ns + nhp + nsp, no + 1, side, (nblk, nc))
    res = pl.pallas_call(
        body, name=name, grid=(nblk, nc),
        in_specs=([pl.BlockSpec(bs, im) for (_, bs, im, _) in seqs]
                  + [pl.BlockSpec(bs, lambda h, n, _im=im: _im(h)) for (_, bs, im, _) in hparams]
                  + [_whole(p) for p in sparams] + s_in),
        out_specs=([pl.BlockSpec(bs, im) for (_, _, bs, im, _) in outs]
                   + [pl.BlockSpec((hb, None) + tuple(state_shape), lambda h, n: (h, n) + (0,) * nst)] + s_out),
        out_shape=([jax.ShapeDtypeStruct(fs, dt) for (fs, dt, _, _, _) in outs]
                   + [jax.ShapeDtypeStruct((nblk * hb, nc) + tuple(state_shape), F32)] + s_shape),
        scratch_shapes=[pltpu.VMEM((hb,) + tuple(state_shape), F32)] + s_scr,
        compiler_params=_params(("arbitrary", "arbitrary"), side is not None),
    )(*[x[0] for x in seqs], *[x[0] for x in hparams], *sparams, *s_args)
    return res[:no], res[no], res[no + 1:]


def scan_bwd(name, chunk_fn, nblk, hb, nc, seqs, hparams, sparams, state_shape, states, douts, dseqs, batched, side=None):
    ns, nhp, nsp, no = len(seqs), len(hparams), len(sparams), len(douts)
    nst = len(state_shape)

    def core(*refs):
        seq_r, hp_r, sp_r = refs[:ns], refs[ns:ns + nhp], refs[ns + nhp:ns + nhp + nsp]
        base = ns + nhp + nsp
        st_r = refs[base]
        do_r = refs[base + 1:base + 1 + no]
        base += 1 + no
        ds_r, dhp_r, dsp_r = refs[base:base + ns], refs[base + ns:base + ns + nhp], refs[base + ns + nhp:base + ns + nhp + nsp]
        dst_scr = refs[-1]
        h, n = pl.program_id(0), pl.program_id(1)

        @pl.when(n == 0)
        def _():
            dst_scr[...] = jnp.zeros_like(dst_scr)
            for ref in dhp_r:
                ref[...] = jnp.zeros_like(ref)

        @pl.when((n == 0) & (h == 0))
        def _():
            for ref in dsp_r:
                ref[...] = jnp.zeros_like(ref)

        seq_v = [x[...].astype(F32) for x in seq_r]
        hp_v = [x[...] for x in hp_r]
        sp_v = [x[...] for x in sp_r]
        do_v = [x[...].astype(F32) for x in do_r]
        prim = (_heads(seq_v, seqs, hb), _heads(hp_v, hparams, hb), sp_v, st_r[...])
        _, vjp = jax.vjp(_over_heads(chunk_fn, hb, seqs, hparams, batched), *prim)
        ds, dhp, dsp, dst = vjp((tuple(_heads(do_v, douts, hb)), dst_scr[...]))
        for ref, spec, val in zip(ds_r, dseqs, ds):
            if spec[4] is None:
                ref[...] = val.astype(ref.dtype)
            else:
                for j in range(hb):
                    ref[_where(spec[4], j)] = val[j].astype(ref.dtype)
        for ref, spec, val in zip(dhp_r, hparams, dhp):
            for j in range(hb):
                ref[_where(spec[3], j)] += val[j]
        for ref, val in zip(dsp_r, dsp):
            ref[...] += val
        dst_scr[...] = dst

    n_in, n_out = ns + nhp + nsp + 1 + no, ns + nhp + nsp
    body, s_in, s_out, s_shape, s_scr, s_args = _with_side(core, n_in, n_out, side, (nblk, nc))
    res = pl.pallas_call(
        body, name=name, grid=(nblk, nc),
        in_specs=([pl.BlockSpec(bs, _flip(im, nc)) for (_, bs, im, _) in seqs]
                  + [pl.BlockSpec(bs, lambda h, n, _im=im: _im(h)) for (_, bs, im, _) in hparams]
                  + [_whole(p) for p in sparams]
                  + [pl.BlockSpec((hb, None) + tuple(state_shape), lambda h, n: (h, nc - 1 - n) + (0,) * nst)]
                  + [pl.BlockSpec(bs, _flip(im, nc)) for (_, bs, im, _) in douts] + s_in),
        out_specs=([pl.BlockSpec(bs, _flip(im, nc)) for (_, _, bs, im, _) in dseqs]
                   + [pl.BlockSpec(bs, lambda h, n, _im=im: _im(h)) for (_, bs, im, _) in hparams]
                   + [_whole(p) for p in sparams] + s_out),
        out_shape=([jax.ShapeDtypeStruct(fs, dt) for (fs, dt, _, _, _) in dseqs]
                   + [jax.ShapeDtypeStruct(x[0].shape, F32) for x in hparams]
                   + [jax.ShapeDtypeStruct(p.shape, F32) for p in sparams] + s_shape),
        scratch_shapes=[pltpu.VMEM((hb,) + tuple(state_shape), F32)] + s_scr,
        compiler_params=_params(("arbitrary", "arbitrary"), side is not None),
    )(*[x[0] for x in seqs], *[x[0] for x in hparams], *sparams, states, *[x[0] for x in douts], *s_args)
    return res[:ns], res[ns:ns + nhp], res[ns + nhp:n_out], res[n_out:]


def _shift_down(x, n, rows):
    if n == 0:
        return x
    return jnp.where(rows >= n, pltpu.roll(x, n, 0), 0.0)


def _shift_up(x, n, rows):
    if n == 0:
        return x
    s = x.shape[0]
    return jnp.where(rows < s - n, pltpu.roll(x, s - n, 0), 0.0)


def conv_fwd(name, x, col0, w, b):
    s, cw = x.shape[0], w.shape[1]

    def body(x_ref, w_ref, b_ref, o_ref):
        xv = x_ref[...]
        rows = _iota(xv.shape, 0)
        u = jnp.broadcast_to(b_ref[...], xv.shape)
        for j in range(CONV_K):
            u = u + w_ref[j:j + 1, :] * _shift_down(xv, CONV_K - 1 - j, rows)
        o_ref[...] = _silu(u)

    return pl.pallas_call(
        body, name=name, grid=(cw // LANES,),
        in_specs=[pl.BlockSpec((s, LANES), lambda j: (0, col0 + j)),
                  pl.BlockSpec((CONV_K, LANES), lambda j: (0, j)),
                  pl.BlockSpec((1, LANES), lambda j: (0, j))],
        out_specs=pl.BlockSpec((s, LANES), lambda j: (0, j)),
        out_shape=jax.ShapeDtypeStruct((s, cw), F32),
        compiler_params=_params(("parallel",)),
    )(x, w, b)


def conv_bwd(name, x, col0, w, b, dout):
    s, cw = x.shape[0], w.shape[1]

    def body(x_ref, w_ref, b_ref, g_ref, dx_ref, dw_ref, db_ref):
        xv = x_ref[...]
        rows = _iota(xv.shape, 0)
        sh = [_shift_down(xv, CONV_K - 1 - j, rows) for j in range(CONV_K)]
        u = jnp.broadcast_to(b_ref[...], xv.shape)
        for j in range(CONV_K):
            u = u + w_ref[j:j + 1, :] * sh[j]
        sg = _sigmoid(u)
        du = g_ref[...] * (sg * (1.0 + u * (1.0 - sg)))
        dx = jnp.zeros_like(xv)
        for j in range(CONV_K):
            dx = dx + w_ref[j:j + 1, :] * _shift_up(du, CONV_K - 1 - j, rows)
            dw_ref[j:j + 1, :] = jnp.sum(du * sh[j], axis=0, keepdims=True)
        dx_ref[...] = dx.astype(dx_ref.dtype)
        db_ref[...] = jnp.sum(du, axis=0, keepdims=True)

    return pl.pallas_call(
        body, name=name, grid=(cw // LANES,),
        in_specs=[pl.BlockSpec((s, LANES), lambda j: (0, col0 + j)),
                  pl.BlockSpec((CONV_K, LANES), lambda j: (0, j)),
                  pl.BlockSpec((1, LANES), lambda j: (0, j)),
                  pl.BlockSpec((s, LANES), lambda j: (0, j))],
        out_specs=[pl.BlockSpec((s, LANES), lambda j: (0, j)),
                   pl.BlockSpec((CONV_K, LANES), lambda j: (0, j)),
                   pl.BlockSpec((1, LANES), lambda j: (0, j))],
        out_shape=[jax.ShapeDtypeStruct((s, cw), BF16), jax.ShapeDtypeStruct((CONV_K, cw), F32),
                   jax.ShapeDtypeStruct((1, cw), F32)],
        compiler_params=_params(("parallel",)),
    )(x, w, b, dout)


def exchange(name, sends, broadcast):
    nop = len(sends)

    def body(*refs):
        start, wait = _exchange_ops(refs[:nop], refs[nop:2 * nop], *refs[2 * nop:], broadcast)
        start()
        wait()

    return pl.pallas_call(
        body, name=name,
        in_specs=[HBM_SPEC] * nop, out_specs=[HBM_SPEC] * nop,
        out_shape=_exchange_out(sends, broadcast), scratch_shapes=_exchange_sems(nop),
        compiler_params=pltpu.CompilerParams(has_side_effects=True),
    )(*sends)


HBM_SPEC = pl.BlockSpec(memory_space=pltpu.HBM)


def _exchange_out(sends, broadcast):
    return [jax.ShapeDtypeStruct((N_DEV,) + tuple(t.shape if broadcast else t.shape[1:]), t.dtype) for t in sends]


def _exchange_sems(nop):
    return [pltpu.SemaphoreType.DMA((nop * (N_DEV - 1),)), pltpu.SemaphoreType.DMA((nop * (N_DEV - 1),)),
            pltpu.SemaphoreType.DMA((nop,))]


def _exchange_ops(send_refs, recv_refs, send_sems, recv_sems, local_sems, broadcast):
    nop = len(send_refs)
    x, y, c = lax.axis_index("x"), lax.axis_index("y"), lax.axis_index("c")
    me = 4 * x + 2 * y + c
    peers = []
    for k in range(1, N_DEV):
        px = 1 - x if (k >> 2) & 1 else x
        py = 1 - y if (k >> 1) & 1 else y
        pc = 1 - c if k & 1 else c
        peers.append(((px, py, pc), 4 * px + 2 * py + pc))

    def src(i, peer):
        return send_refs[i] if broadcast else send_refs[i].at[peer]

    def remote(i, k, arrival):
        dev, peer = peers[k]
        return pltpu.make_async_remote_copy(
            src_ref=src(i, peer), dst_ref=recv_refs[i].at[peer if arrival else me],
            send_sem=send_sems.at[i * (N_DEV - 1) + k], recv_sem=recv_sems.at[i * (N_DEV - 1) + k],
            device_id=dev, device_id_type=pl.DeviceIdType.MESH)

    def local(i):
        return pltpu.make_async_copy(src(i, me), recv_refs[i].at[me], local_sems.at[i])

    def start():
        for i in range(nop):
            local(i).start()
        for k in range(N_DEV - 1):
            for i in range(nop):
                remote(i, k, False).start()

    def wait():
        for k in range(N_DEV - 1):
            for i in range(nop):
                remote(i, k, True).wait_recv()
        for k in range(N_DEV - 1):
            for i in range(nop):
                remote(i, k, False).wait_send()
        for i in range(nop):
            local(i).wait()

    return start, wait


def adamw_sum(name, parts, w, m, v):
    rws, cols = w.shape
    nsum = parts.shape[0]
    tr = _pick(rws, (256, 128, 64, 32, 16, 8))
    c1 = 1.0 / (1.0 - ADAM_B1 ** ADAM_STEP)
    c2 = 1.0 / (1.0 - ADAM_B2 ** ADAM_STEP)

    def body(p_ref, w_ref, m_ref, v_ref, g_ref, d_ref, nm_ref, nv_ref):
        g = p_ref[0]
        for j in range(1, nsum):
            g = g + p_ref[j]
        nm = ADAM_B1 * m_ref[...] + (1.0 - ADAM_B1) * g
        nv = ADAM_B2 * v_ref[...] + (1.0 - ADAM_B2) * (g * g)
        g_ref[...] = g
        nm_ref[...] = nm
        nv_ref[...] = nv
        d_ref[...] = -ADAM_LR * ((nm * c1) / (jnp.sqrt(nv * c2) + ADAM_EPS) + ADAM_WD * w_ref[...])

    blk = pl.BlockSpec((tr, cols), lambda i: (i, 0))
    return pl.pallas_call(
        body, name=name, grid=(rws // tr,),
        in_specs=[pl.BlockSpec((nsum, tr, cols), lambda i: (0, i, 0)), blk, blk, blk],
        out_specs=[blk, blk, blk, blk],
        out_shape=[jax.ShapeDtypeStruct(w.shape, F32)] * 4,
        compiler_params=_params(("parallel",)),
    )(parts, w, m, v)


def ada_fwd(name, c_all, w, b):
    nl = w.shape[0]

    def body(c_ref, w_ref, b_ref, o_ref):
        ca = _silu(c_ref[...])
        for l in range(nl):
            o_ref[l] = mm_nn(ca, w_ref[l]) + b_ref[l]

    return pl.pallas_call(
        body, name=name,
        out_shape=jax.ShapeDtypeStruct((nl, c_all.shape[0], w.shape[2]), F32),
        compiler_params=pltpu.CompilerParams(vmem_limit_bytes=VMEM_LIMIT),
    )(c_all, w, b)


def ada_bwd(name, c_all, dmod):
    nl = dmod.shape[0]

    def body(c_ref, g_ref, o_ref):
        ca = _silu(c_ref[...])
        for l in range(nl):
            o_ref[l] = mm_tn(ca, g_ref[l])

    return pl.pallas_call(
        body, name=name,
        out_shape=jax.ShapeDtypeStruct((nl, c_all.shape[1], dmod.shape[2]), F32),
        compiler_params=pltpu.CompilerParams(vmem_limit_bytes=VMEM_LIMIT),
    )(c_all, dmod)


def lower_bounds_fn(rows, params):
    (lg,), _ = rows, params
    nl = lg.shape[0]
    mx = jnp.max(lg, axis=0, keepdims=True)
    e = jnp.exp(lg - mx)
    p = e / jnp.sum(e, axis=0, keepdims=True)
    layer = _iota((nl, 1), 0)
    acc = jnp.zeros_like(p)
    for j in range(1, nl):
        pj = jnp.sum(jnp.where(layer == j, p, 0.0), axis=0, keepdims=True)
        acc = acc + jnp.where(layer >= j, 1.0, 0.0) * pj
    return (acc,)


def loss_call(name, x, tgt, nw, tm):
    s, d = x.shape

    def body(x_ref, t_ref, w_ref, l_ref, dx_ref, dw_ref):
        def f(xv, wv):
            err = _rms(xv, wv) - t_ref[...]
            return jnp.sum(0.5 * jnp.mean(err * err, axis=-1, keepdims=True), axis=0, keepdims=True)

        val, vjp = jax.vjp(f, x_ref[...], w_ref[...])
        dx, dw = vjp(jnp.ones_like(val))

        @pl.when(pl.program_id(0) == 0)
        def _():
            l_ref[...] = jnp.zeros_like(l_ref)
            dw_ref[...] = jnp.zeros_like(dw_ref)

        l_ref[...] += jnp.broadcast_to(val, l_ref.shape)
        dw_ref[...] += dw
        dx_ref[...] = dx

    row = pl.BlockSpec((tm, d), lambda i: (i, 0))
    return pl.pallas_call(
        body, name=name, grid=(s // tm,),
        in_specs=[row, row, _whole(nw)],
        out_specs=[pl.BlockSpec((8, LANES), lambda i: (0, 0)), row, _whole(nw)],
        out_shape=[jax.ShapeDtypeStruct((8, LANES), F32), jax.ShapeDtypeStruct((s, d), F32),
                   jax.ShapeDtypeStruct(nw.shape, F32)],
        compiler_params=_params(("arbitrary",)),
    )(x, tgt, nw)


class Dims:
    def __init__(self, s, d, ffn):
        self.s, self.d, self.ffn = s, d, ffn
        self.mix = 3 * d // 4
        self.nh = self.mix // HEAD
        self.ssm_heads = self.mix // SSM_P
        self.pairs = self.mix // (2 * SSM_P)
        self.nc = s // CHUNK
        self.conv_ssm = self.mix + 4 * HEAD
        self.conv_w = self.conv_ssm + 3 * self.mix
        self.o_gates = 4 * self.mix
        self.o_sz = self.o_gates + 3 * d
        self.o_gz = self.o_sz + self.mix
        self.o_conv = self.o_gz + self.mix
        self.o_small = self.o_conv + self.conv_w
        used = self.o_small + LANES
        self.np = -(-used // 1280) * 1280
        self.tm = _pick(s, (256, 128, 64))
        mix, nh = self.mix, self.nh
        self.in_sizes = (mix, mix, mix, mix, mix, self.conv_ssm, self.ssm_heads, 3 * mix, mix, nh, nh, 3 * d)
        self.in_width = sum(self.in_sizes)


def w_in_tables(dm, nshard):
    off = np.cumsum((0,) + dm.in_sizes)
    hq, hf, hi, hg, sz, sxbc, sdt, gqkv, gz, gb, ga, gates = (np.arange(off[i], off[i + 1]) for i in range(12))
    hgrn = np.stack([t.reshape(dm.nh, HEAD) for t in (hq, hf, hi, hg)], axis=1).reshape(-1)
    perm = np.concatenate([hgrn, gates, sz, gz, gqkv, sxbc, sdt, gb, ga])
    perm = np.concatenate([perm, np.full(dm.np - perm.size, -1)])
    shard = dm.in_width // nshard
    wpad = -(-shard // LANES) * LANES
    fwd = np.where(perm >= 0, (perm // shard) * wpad + perm % shard, -1)[None]
    inv = np.zeros(dm.in_width, np.int64)
    inv[perm[perm >= 0]] = np.nonzero(perm >= 0)[0]
    bwd = np.full((nshard, wpad), -1)
    bwd[:, :shard] = inv.reshape(nshard, shard)
    return fwd.astype(np.int32), bwd.astype(np.int32)


def _small_views(dm, small):
    t = small.T
    col = lambda a: a[:, :, None]
    row = lambda a: a.reshape(a.shape[0], dm.nc, 1, CHUNK)
    a, b = dm.ssm_heads, dm.ssm_heads + dm.nh
    sdt, gb, ga = t[:a], t[a:b], t[b:b + dm.nh]
    return col(sdt), row(sdt), col(gb), col(ga), row(ga)


def _scan_specs(dm, proj, conv_out, views, lp):
    dt_col, dt_row, gb_col, ga_col, ga_row = views
    mixb, nh = dm.mix // LANES, dm.nh
    s, mix = dm.s, dm.mix
    lane = ("lane", LANES)
    hb = HGRN_HEADS_PER_STEP
    hw = (CHUNK, hb * LANES)
    hgrn = dict(
        nblk=nh // hb, hb=hb, fn=hgrn_chunk, batched=False,
        seqs=[(proj, (CHUNK, hb * 4 * HEAD), lambda h, n: (n, h), ("lane", 4 * HEAD))],
        hparams=[(lp["lb"], (1, hb * HEAD), lambda h: (0, h), lane)],
        sparams=[lp["hgrn_norm"]],
        dseqs=[((s, 4 * mix), BF16, (CHUNK, hb * 4 * HEAD), lambda h, n: (n, h), ("lane", 4 * HEAD))],
        io=(hw, lambda h, n: (n, h), lane))
    ppg = dm.pairs // 2
    qb = 3 * mixb
    gw = (CHUNK, ppg * LANES)
    pcol = ((2 * ppg, CHUNK, 1), lambda g, n: (g, n, 0), ("lead", 2))
    prow = ((2 * ppg, None, 1, CHUNK), lambda g, n: (g, n, 0, 0), ("lead", 2))
    ppar = ((2 * ppg, 1, 1), lambda g: (g, 0, 0), ("lead", 2))
    bc = lambda first: ((CHUNK, LANES), lambda g, n: (n, first + g), None)
    ssd = dict(
        nblk=2, hb=ppg, fn=ssd_chunk, batched=False,
        seqs=[(conv_out, gw, lambda g, n: (n, qb // ppg + g), lane), (conv_out,) + bc(qb + mixb), (conv_out,) + bc(qb + mixb + 2),
              (dt_col,) + pcol, (dt_row,) + prow],
        hparams=[(lp["ssm_dt_bias"],) + ppar, (lp["ssm_a_log"],) + ppar],
        sparams=[],
        dseqs=[((s, mix), F32, gw, lambda g, n: (n, g), lane), ((s, 2 * LANES), F32) + bc(0), ((s, 2 * LANES), F32) + bc(0),
               (dt_col.shape, F32) + pcol, (dt_row.shape, F32) + prow],
        io=(gw, lambda g, n: (n, g), lane))
    hb = GDN_HEADS_PER_STEP
    hw = (CHUNK, hb * LANES)
    cq, cgz = 0, dm.o_gz // LANES
    assert nh % hb == 0 and cgz % hb == 0 and qb % ppg == 0
    hcol = ((hb, CHUNK, 1), lambda h, n: (h, n, 0), ("idx",))
    hrow = ((hb, None, 1, CHUNK), lambda h, n: (h, n, 0, 0), ("idx",))
    hpar = ((hb, 1, 1), lambda h: (h, 0, 0), ("idx",))
    at = lambda first: (hw, lambda h, n: (n, first // hb + h), lane)
    gdn = dict(
        nblk=nh // hb, hb=hb, fn=gdn_chunk, batched=True,
        seqs=[(conv_out,) + at(cq), (conv_out,) + at(cq + nh), (conv_out,) + at(cq + 2 * nh), (proj,) + at(cgz),
              (gb_col,) + hcol, (ga_col,) + hcol, (ga_row,) + hrow],
        hparams=[(lp["gdn_dt_bias"],) + hpar, (lp["gdn_a_log"],) + hpar],
        sparams=[lp["gdn_norm"]],
        dseqs=[((s, mix), F32) + at(0), ((s, mix), F32) + at(0), ((s, mix), F32) + at(0), ((s, mix), BF16) + at(0),
               (gb_col.shape, F32) + hcol, (ga_col.shape, F32) + hcol, (ga_row.shape, F32) + hrow],
        io=(hw, lambda h, n: (n, h), lane))
    return hgrn, ssd, gdn


def _run_scan_fwd(dm, name, sp, side=None):
    out = ((dm.s, dm.mix), F32) + sp["io"]
    (y,), states, arrived = scan_fwd(name, sp["fn"], sp["nblk"], sp["hb"], dm.nc, sp["seqs"], sp["hparams"],
                                     sp["sparams"], (HEAD, HEAD), [out], sp["batched"], side)
    return y, states, arrived


def _run_scan_bwd(dm, name, sp, states, dy, side=None):
    return scan_bwd(name, sp["fn"], sp["nblk"], sp["hb"], dm.nc, sp["seqs"], sp["hparams"], sp["sparams"], (HEAD, HEAD),
                    states, [(dy,) + sp["io"]], sp["dseqs"], sp["batched"], side)


def _share_out(side):
    if side is None:
        return None, None, None
    s, broadcast = side
    return ([s[0]], broadcast), ([s[1], s[2], s[4]], broadcast), ([s[3]], broadcast)


def _collect(got_h, got_s, got_g):
    if not got_h:
        return None
    return [got_h[0], got_s[0], got_s[1], got_g[0], got_s[2]]


def layer_fwd(dm, l, x, lp, side=None):
    tm, d, mix = dm.tm, dm.d, dm.mix
    tag = f"l{l}_"
    (h,) = rowstage_fwd(tag + "norm1", normmod_fn, [(x, d, 0)], [lp["norm_mix"], lp["sc1"], lp["sh1"]], [(d, BF16)], tm)
    proj = matmul(tag + "proj", h, lp["w_in"], "nn", F32)
    conv_out = conv_fwd(tag + "conv", proj, dm.o_conv // LANES, lp["conv_w"], lp["conv_b"])
    small = proj[:, dm.o_small:dm.o_small + LANES]
    views = _small_views(dm, small)
    hg, sd, gd = _scan_specs(dm, proj, conv_out, views, lp)
    side_h, side_s, side_g = _share_out(side)
    yh, st_h, got_h = _run_scan_fwd(dm, tag + "hgrn", hg, side_h)
    y_ssd, st_s, got_s = _run_scan_fwd(dm, tag + "ssd", sd, side_s)
    yg, st_g, got_g = _run_scan_fwd(dm, tag + "gdn", gd, side_g)
    arrived = _collect(got_h, got_s, got_g)
    (ys,) = rowstage_fwd(tag + "ssmpost", ssmpost_fn,
                         [(y_ssd, mix, 0), (conv_out, mix, 3), (proj, mix, dm.o_sz // mix)],
                         [lp["ssm_d_exp"], lp["ssm_norm"]], [(mix, F32)], tm)
    (merged,) = rowstage_fwd(tag + "merge", merge_fn, [(yh, mix, 0), (ys, mix, 0), (yg, mix, 0), (proj, 3 * d, 1)],
                             [lp["b_merge"], lp["w_branch"]], [(d, BF16)], tm)
    (x1,) = rowstage_fwd(tag + "outproj", outproj_fn, [(merged, d, 0), (x, d, 0)], [lp["g1"], lp["w_out"]], [(d, F32)], tm)
    (h2,) = rowstage_fwd(tag + "norm2", normmod_fn, [(x1, d, 0)], [lp["norm_ffn"], lp["sc2"], lp["sh2"]], [(d, BF16)], tm)
    gu = bmatmul(tag + "ffn_in", h2, lp["w_ffn_in"], "nn", F32, True)
    gu = gu.reshape((2, gu.shape[0] // 2) + gu.shape[1:])
    act = swiglu3_fwd(tag + "swiglu", gu, tm)
    o2 = bmatmul(tag + "ffn_out", act, lp["w_ffn_out"], "nn", F32, False)
    (x2,) = rowstage_fwd(tag + "resid", resid_fn, [(x1, d, 0), (o2, d, 0)], [lp["g2"]], [(d, F32)], tm)
    saved = dict(x=x, h=h, proj=proj, conv_out=conv_out, views=views, yh=yh, y_ssd=y_ssd, yg=yg, ys=ys,
                 st_h=st_h, st_s=st_s, st_g=st_g, merged=merged, x1=x1, h2=h2, gu=gu, act=act, o2=o2)
    return x2, saved, arrived


def layer_bwd(dm, l, dx2, lp, sv, side=None):
    tm, d, mix, s = dm.tm, dm.d, dm.mix, dm.s
    tag = f"l{l}_b_"
    g = {}
    (dx1_a, do2), (g["g2"],) = rowstage_bwd(tag + "resid", resid_fn, [(sv["x1"], d, 0), (sv["o2"], d, 0)], [lp["g2"]],
                                            [dx2], [F32, BF16], tm)
    dact = bmatmul(tag + "ffn_out_dx", do2, lp["w_ffn_out"], "nt", BF16, True)
    g["w_ffn_out"] = bmatmul(tag + "ffn_out_dw", sv["act"], do2, "tn", F32, True)
    dgu = swiglu3_bwd(tag + "swiglu", sv["gu"], dact, tm)
    dgu = dgu.reshape((-1,) + dgu.shape[2:])
    dh2 = bmatmul(tag + "ffn_in_dx", dgu, lp["w_ffn_in"], "nt", BF16, False)
    g["w_ffn_in"] = bmatmul(tag + "ffn_in_dw", sv["h2"], dgu, "tn", F32, True)
    (dx1,), (g["norm_ffn"], g["sc2"], g["sh2"]) = rowstage_bwd(
        tag + "norm2", normmod_fn, [(sv["x1"], d, 0)], [lp["norm_ffn"], lp["sc2"], lp["sh2"]], [dh2], [F32], tm,
        adds={0: dx1_a})
    (dmerged, dx_a), (g["g1"], g["w_out"]) = rowstage_bwd(
        tag + "outproj", outproj_fn, [(sv["merged"], d, 0), (sv["x"], d, 0)], [lp["g1"], lp["w_out"]], [dx1],
        [BF16, F32], tm)
    proj, conv_out = sv["proj"], sv["conv_out"]
    (dyh, dys, dyg, dgates), (g["b_merge"], g["w_branch"]) = rowstage_bwd(
        tag + "merge", merge_fn, [(sv["yh"], mix, 0), (sv["ys"], mix, 0), (sv["yg"], mix, 0), (proj, 3 * d, 1)],
        [lp["b_merge"], lp["w_branch"]], [dmerged], [F32, F32, F32, BF16], tm)
    (dy_ssd, dxs_a, dsz), (g["ssm_d_exp"], g["ssm_norm"]) = rowstage_bwd(
        tag + "ssmpost", ssmpost_fn, [(sv["y_ssd"], mix, 0), (conv_out, mix, 3), (proj, mix, dm.o_sz // mix)],
        [lp["ssm_d_exp"], lp["ssm_norm"]], [dys], [F32, F32, BF16], tm)
    hg, sd, gd = _scan_specs(dm, proj, conv_out, sv["views"], lp)
    side_h, side_s, side_g = _share_out(side)
    (dhgrn,), (g["lb"],), (g["hgrn_norm"],), got_h = _run_scan_bwd(dm, tag + "hgrn", hg, sv["st_h"], dyh, side_h)
    (dxs_b, dbp, dcp, d_dt_col, d_dt_row), (g["ssm_dt_bias"], g["ssm_a_log"]), _, got_s = _run_scan_bwd(
        dm, tag + "ssd", sd, sv["st_s"], dy_ssd, side_s)
    (dq, dk, dv, dgz, d_gb_col, d_ga_col, d_ga_row), (g["gdn_dt_bias"], g["gdn_a_log"]), (g["gdn_norm"],), got_g = _run_scan_bwd(
        dm, tag + "gdn", gd, sv["st_g"], dyg, side_g)
    arrived = _collect(got_h, got_s, got_g)
    dconv = jnp.concatenate([dq, dk, dv, dxs_a + dxs_b, dbp, dcp], axis=1)
    dpc, g["conv_w"], g["conv_b"] = conv_bwd(tag + "conv", proj, dm.o_conv // LANES, lp["conv_w"], lp["conv_b"], dconv)
    unrow = lambda t: t.reshape(t.shape[0], s).T
    dsmall = jnp.concatenate([d_dt_col[:, :, 0].T + unrow(d_dt_row), d_gb_col[:, :, 0].T,
                              d_ga_col[:, :, 0].T + unrow(d_ga_row)], axis=1)
    pad = jnp.zeros((s, dm.np - dm.o_small - dsmall.shape[1]), BF16)
    dproj = jnp.concatenate([dhgrn, dgates, dsz, dgz, dpc, dsmall.astype(BF16), pad], axis=1)
    dh = matmul(tag + "proj_dx", dproj, lp["w_in"], "nt", BF16)
    g["w_in"] = matmul(tag + "proj_dw", sv["h"], dproj, "tn", F32)
    (dx,), (g["norm_mix"], g["sc1"], g["sh1"]) = rowstage_bwd(
        tag + "norm1", normmod_fn, [(sv["x"], d, 0)], [lp["norm_mix"], lp["sc1"], lp["sh1"]], [dh], [F32], tm,
        adds={0: dx_a})
    return dx, g, arrived


WEIGHTS = ("w_ada", "b_ada", "norm_mix", "norm_ffn", "w_in", "b_merge", "hgrn_lb_logits", "hgrn_norm", "ssm_conv_w",
           "ssm_conv_b", "ssm_dt_bias", "ssm_a_log", "ssm_d", "ssm_norm", "gdn_conv_w", "gdn_dt_bias", "gdn_a_log",
           "gdn_norm", "w_branch", "w_out", "w_ffn_in", "w_ffn_out", "norm_final")
GATHERED = ("w_in", "w_branch", "w_out", "w_ffn_in", "w_ffn_out")
PACKET = ("b_ada", "norm_mix", "norm_ffn", "b_merge", "hgrn_norm", "ssm_conv_b", "ssm_dt_bias", "ssm_a_log", "ssm_d",
          "ssm_norm", "gdn_dt_bias", "gdn_a_log", "gdn_norm", "norm_final")
MISC = ("hgrn_lb_logits", "ssm_conv_w", "gdn_conv_w")


def _pack(arrs, dtype, row_mult, lead=0):
    flat = jnp.concatenate([t.reshape(t.shape[:lead] + (-1,)).astype(dtype) for t in arrs], axis=lead)
    n = flat.shape[-1]
    unit = row_mult * LANES
    tot = -(-n // unit) * unit
    flat = jnp.pad(flat, [(0, 0)] * lead + [(0, tot - n)])
    return flat.reshape(flat.shape[:lead] + (tot // LANES, LANES))


def _unpack(packed, shapes, lead=0):
    flat = packed.reshape(packed.shape[:lead] + (-1,))
    out, off = [], 0
    for shp in shapes:
        n = int(np.prod(shp))
        out.append(flat[..., off:off + n].reshape(flat.shape[:lead] + tuple(shp)))
        off += n
    return out


def _shard2d(t):
    return t.reshape((-1, t.shape[-1]))


def weights_from_shards(dm, l, got, idx):
    w_in, wb, w_out, wf, wfo = got
    d, mix = dm.d, dm.mix
    return dict(
        w_in=colgather(f"l{l}_w_in", w_in, idx, dm.np, BF16)[0],
        w_branch=wb.reshape(N_DEV, 3, mix, d // N_DEV).transpose(1, 2, 0, 3).reshape(3, mix, d),
        w_out=w_out.reshape(d, d), w_ffn_in=wf, w_ffn_out=wfo.reshape(N_DEV // 2, -1, d))


def shards_of_grads(dm, l, g, idx):
    d, mix = dm.d, dm.mix
    return [colgather(f"l{l}_g_w_in", g["w_in"][None], idx, dm.in_width // N_DEV, F32),
            g["w_branch"].reshape(3, mix, N_DEV, d // N_DEV).transpose(2, 0, 1, 3).reshape(N_DEV, 3 * mix, d // N_DEV),
            g["w_out"].reshape(N_DEV, d // N_DEV, d), g["w_ffn_in"], g["w_ffn_out"].reshape(N_DEV, -1, d)]


def layer_params(dm, l, full, small, mod_l, lb_l):
    d, mix = dm.d, dm.mix
    row = lambda t: t.reshape(1, -1)
    head = lambda t: t.reshape(-1, 1, 1)
    sh1, sc1, g1, sh2, sc2, g2 = (row(mod_l[i * d:(i + 1) * d]) for i in range(6))
    conv_b = jnp.concatenate([jnp.zeros((3 * mix,), F32), small["ssm_conv_b"][l]])
    return dict(
        w_in=full["w_in"], w_branch=full["w_branch"], w_out=full["w_out"],
        w_ffn_in=full["w_ffn_in"], w_ffn_out=full["w_ffn_out"],
        norm_mix=row(small["norm_mix"][l]), norm_ffn=row(small["norm_ffn"][l]), b_merge=row(small["b_merge"][l]),
        hgrn_norm=row(small["hgrn_norm"][l]), lb=row(lb_l),
        conv_w=jnp.concatenate([small["gdn_conv_w"][l], small["ssm_conv_w"][l]], axis=1), conv_b=row(conv_b),
        ssm_dt_bias=head(small["ssm_dt_bias"][l]), ssm_a_log=head(small["ssm_a_log"][l]),
        ssm_d_exp=row(jnp.repeat(small["ssm_d"][l], SSM_P)), ssm_norm=row(small["ssm_norm"][l]),
        gdn_dt_bias=head(small["gdn_dt_bias"][l]), gdn_a_log=head(small["gdn_a_log"][l]), gdn_norm=row(small["gdn_norm"][l]),
        sh1=sh1, sc1=sc1, g1=g1, sh2=sh2, sc2=sc2, g2=g2)


def layer_grads(dm, g):
    cs = 3 * dm.mix
    out = dict(
        w_in=g["w_in"], w_branch=g["w_branch"], w_out=g["w_out"], w_ffn_in=g["w_ffn_in"],
        w_ffn_out=g["w_ffn_out"], norm_mix=g["norm_mix"][0], norm_ffn=g["norm_ffn"][0], b_merge=g["b_merge"][0],
        hgrn_norm=g["hgrn_norm"][0], ssm_conv_w=g["conv_w"][:, cs:], gdn_conv_w=g["conv_w"][:, :cs],
        ssm_conv_b=g["conv_b"][0, cs:], ssm_dt_bias=g["ssm_dt_bias"][:, 0, 0], ssm_a_log=g["ssm_a_log"][:, 0, 0],
        ssm_d=g["ssm_d_exp"].reshape(dm.ssm_heads, SSM_P).sum(axis=1), ssm_norm=g["ssm_norm"][0],
        gdn_dt_bias=g["gdn_dt_bias"][:, 0, 0], gdn_a_log=g["gdn_a_log"][:, 0, 0], gdn_norm=g["gdn_norm"][0])
    dmod = jnp.concatenate([g[k][0] for k in ("sh1", "sc1", "g1", "sh2", "sc2", "g2")])
    return out, dmod, g["lb"][0]


def local_step(dm, nl, x, tgt, norm_final, params_of, gather_of=None, scatter_of=None):
    arrived = exchange("gather_w0", gather_of(0), True) if gather_of else None
    lps, saved = [], []
    for l in range(nl):
        lps.append(params_of(l, arrived))
        side = (gather_of(l + 1), True) if gather_of and l + 1 < nl else None
        x, sv, arrived = layer_fwd(dm, l, x, lps[l], side)
        saved.append(sv)
    loss, dx, dnf = loss_call("loss", x, tgt, norm_final, dm.tm)
    grads, parts, side = [None] * nl, [None] * nl, None
    for l in reversed(range(nl)):
        dx, grads[l], got = layer_bwd(dm, l, dx, lps[l], saved[l], side)
        if side is not None:
            parts[l + 1] = got
        side = (scatter_of(l, grads[l]), False) if scatter_of else None
    if side is not None:
        parts[0] = exchange("scatter_g0", side[0], False)
    return loss, dx, dnf, grads, parts


def kernel(x, c, w_ada, b_ada, norm_mix, norm_ffn, w_in, b_merge, hgrn_lb_logits, hgrn_norm, ssm_conv_w, ssm_conv_b, ssm_dt_bias, ssm_a_log, ssm_d, ssm_norm, gdn_conv_w, gdn_dt_bias, gdn_a_log, gdn_norm, w_branch, w_out, w_ffn_in, w_ffn_out, norm_final, loss_target, m_w_ada, m_b_ada, m_norm_mix, m_norm_ffn, m_w_in, m_b_merge, m_hgrn_lb_logits, m_hgrn_norm, m_ssm_conv_w, m_ssm_conv_b, m_ssm_dt_bias, m_ssm_a_log, m_ssm_d, m_ssm_norm, m_gdn_conv_w, m_gdn_dt_bias, m_gdn_a_log, m_gdn_norm, m_w_branch, m_w_out, m_w_ffn_in, m_w_ffn_out, m_norm_final, v_w_ada, v_b_ada, v_norm_mix, v_norm_ffn, v_w_in, v_b_merge, v_hgrn_lb_logits, v_hgrn_norm, v_ssm_conv_w, v_ssm_conv_b, v_ssm_dt_bias, v_ssm_a_log, v_ssm_d, v_ssm_norm, v_gdn_conv_w, v_gdn_dt_bias, v_gdn_a_log, v_gdn_norm, v_w_branch, v_w_out, v_w_ffn_in, v_w_ffn_out, v_norm_final):
    a = dict(locals())
    x, tgt = a["x"][0], a["loss_target"][0]
    s, d = x.shape
    nl = a["w_ada"].shape[0]
    dm = Dims(s, d, a["w_ffn_out"].shape[1] * N_DEV)
    me = 4 * lax.axis_index("x") + 2 * lax.axis_index("y") + lax.axis_index("c")

    first = [a["c"], a["ssm_conv_w"], a["gdn_conv_w"]]
    c_all, scw, gcw = _unpack(exchange("gather_c", [_pack(first, F32, 8)], True)[0], [t.shape for t in first], lead=1)
    small = dict(a, ssm_conv_w=scw.transpose(1, 2, 0, 3).reshape(scw.shape[1:3] + (-1,)),
                 gdn_conv_w=gcw.transpose(1, 2, 0, 3).reshape(gcw.shape[1:3] + (-1,)))
    c_pad = jnp.zeros((LANES, d), F32).at[:N_DEV].set(c_all.reshape(N_DEV, d))
    ncol = a["w_ada"].shape[2]
    b_mine = lax.dynamic_slice(a["b_ada"], (0, me * ncol), (nl, ncol))[:, None, :]
    mod_part = ada_fwd("ada_fwd", c_pad, a["w_ada"], b_mine)[:, :N_DEV, :]
    (mod,) = exchange("a2a_mod", [mod_part.transpose(1, 0, 2)], False)
    mod = mod.transpose(1, 0, 2).reshape(nl, N_DEV * ncol)
    (lb,) = rowstage_fwd("lower_bounds", lower_bounds_fn, [(a["hgrn_lb_logits"], dm.mix, 0)], [], [(dm.mix, F32)], nl)

    idx_fwd, idx_bwd = w_in_tables(dm, N_DEV)
    loss, dx, dnf, grads, parts = local_step(
        dm, nl, x, tgt, a["norm_final"].reshape(1, d),
        params_of=lambda l, got: layer_params(dm, l, weights_from_shards(dm, l, got, idx_fwd), small, mod[l], lb[l]),
        gather_of=lambda l: [_shard2d(a[n][l]).astype(BF16) for n in GATHERED],
        scatter_of=lambda l, g: shards_of_grads(dm, l, g, idx_bwd))

    per_layer = [layer_grads(dm, g) for g in grads]
    res = {}
    for l in range(nl):
        for n, p in zip(GATHERED, parts[l]):
            outs = adamw_sum(f"adamw_l{l}_{n}", p, *[_shard2d(a[q + n][l]) for q in ("", "m_", "v_")])
            for kind, o in zip(("grad", "delta", "new_m", "new_v"), outs):
                res.setdefault((kind, n), []).append(o.reshape(a[n].shape[1:]))
    for key in list(res):
        res[key] = jnp.stack(res[key])

    stackg = lambda n: jnp.stack([pl_[0][n] for pl_ in per_layer])
    dmod = jnp.stack([pl_[1] for pl_ in per_layer])
    dlb = jnp.stack([pl_[2] for pl_ in per_layer])
    pk_g = [dmod if n == "b_ada" else dnf if n == "norm_final" else stackg(n) for n in PACKET]
    extra = [dlb, stackg("ssm_conv_w"), stackg("gdn_conv_w"), loss[0, :1]]
    pk_shapes = [t.shape for t in pk_g + extra]
    zeros = [jnp.zeros(t.shape, F32) for t in extra]
    (parts,) = exchange("gather_small", [_pack(pk_g + extra, F32, 8)], True)
    outs = adamw_sum("adamw_small", parts, *[_pack([a[p + n] for n in PACKET] + zeros, F32, 8) for p in ("", "m_", "v_")])
    for kind, o in zip(("grad", "delta", "new_m", "new_v"), outs):
        un = _unpack(o, pk_shapes)
        for n, t in zip(PACKET, un):
            res[(kind, n)] = t.reshape(a[n].shape)
        if kind == "grad":
            dlb_sum, g_scw, g_gcw, loss_sum = un[len(PACKET):]

    (g_lb,), _ = rowstage_bwd("lower_bounds_b", lower_bounds_fn, [(a["hgrn_lb_logits"], dm.mix, 0)], [], [dlb_sum], [F32], nl)
    mine = lambda t, n: lax.dynamic_slice_in_dim(t, me * a[n].shape[-1], a[n].shape[-1], axis=t.ndim - 1)
    (dmod_cols,) = exchange("a2a_dmod", [dmod.reshape(nl, N_DEV, ncol).transpose(1, 0, 2)], False)
    dmod_pad = jnp.zeros((nl, LANES, ncol), F32).at[:, :N_DEV].set(dmod_cols.transpose(1, 0, 2))
    g_w_ada = ada_bwd("ada_bwd", c_pad, dmod_pad)
    outs = adamw_sum("adamw_w_ada", g_w_ada.reshape(1, nl * d, ncol), *[a[q + "w_ada"].reshape(nl * d, ncol) for q in ("", "m_", "v_")])
    for kind, o in zip(("grad", "delta", "new_m", "new_v"), outs):
        res[(kind, "w_ada")] = o.reshape(nl, d, ncol)
    g_misc = [g_lb, mine(g_scw, "ssm_conv_w"), mine(g_gcw, "gdn_conv_w")]
    outs = adamw_sum("adamw_misc", _pack(g_misc, F32, 8)[None], *[_pack([a[q + n] for n in MISC], F32, 8) for q in ("", "m_", "v_")])
    for kind, o in zip(("grad", "delta", "new_m", "new_v"), outs):
        for n, t in zip(MISC, _unpack(o, [a[n].shape for n in MISC])):
            res[(kind, n)] = t

    out = [loss_sum.reshape(()), dx[None]]
    for kind in ("grad", "delta", "new_m", "new_v"):
        out += [res[(kind, n)] for n in WEIGHTS]
    return tuple(out)
```

```python
import functools
import math

import numpy as np
import jax
import jax.numpy as jnp
from jax import lax
from jax.experimental import pallas as pl
from jax.experimental.pallas import tpu as pltpu

F32 = jnp.float32
BF16 = jnp.bfloat16

N_DEV = 8
CHUNK = 64
SUB = 32
HGRN_HEADS_PER_STEP = 2
GDN_HEADS_PER_STEP = 6
HEAD = 128
SSM_P = 64
CONV_K = 4
F_MIN = 1e-30
NORM_EPS = 1e-6
LANES = 128
GATHER_TILE = 256
VMEM_LIMIT = 56 * 1024 * 1024

ADAM_LR = 0.001
ADAM_B1 = 0.9
ADAM_B2 = 0.999
ADAM_EPS = 1e-08
ADAM_WD = 0.01
ADAM_STEP = 10


def _dg(a, b, ca, cb):
    return lax.dot_general(a.astype(BF16), b.astype(BF16), (((ca,), (cb,)), ((), ())),
                           preferred_element_type=F32)


def _split3(x):
    x1 = x.astype(BF16)
    r = x - x1.astype(F32)
    x2 = r.astype(BF16)
    x3 = (r - x2.astype(F32)).astype(BF16)
    return x1, x2, x3


def _hdg(a, b, ca, cb):
    a1, a2, _ = _split3(a)
    b1, b2, _ = _split3(b)
    dn = (((ca,), (cb,)), ((), ()))
    d = lambda p, q: lax.dot_general(p, q, dn, preferred_element_type=F32)
    return (d(a2, b1) + d(a1, b2)) + d(a1, b1)


def _dot_family(prim):
    @jax.custom_vjp
    def nn(a, b):
        return prim(a, b, 1, 0)

    @jax.custom_vjp
    def nt(a, b):
        return prim(a, b, 1, 1)

    @jax.custom_vjp
    def tn(a, b):
        return prim(a, b, 0, 0)

    nn.defvjp(lambda a, b: (nn(a, b), (a, b)), lambda r, g: (nt(g, r[1]), tn(r[0], g)))
    nt.defvjp(lambda a, b: (nt(a, b), (a, b)), lambda r, g: (nn(g, r[1]), tn(g, r[0])))
    tn.defvjp(lambda a, b: (tn(a, b), (a, b)), lambda r, g: (nt(r[1], g), nn(r[0], g)))
    return nn, nt, tn


mm_nn, mm_nt, mm_tn = _dot_family(_dg)
hd_nn, hd_nt, hd_tn = _dot_family(_hdg)


def _iota(shape, dim):
    return lax.broadcasted_iota(jnp.int32, shape, dim)


def _scan_rows(x, reverse):
    n = x.shape[0]
    rows = _iota(x.shape, 0)
    k = 1
    while k < n:
        if reverse:
            x = x + jnp.where(rows < n - k, pltpu.roll(x, n - k, 0), 0.0)
        else:
            x = x + jnp.where(rows >= k, pltpu.roll(x, k, 0), 0.0)
        k *= 2
    return x


@jax.custom_vjp
def cumsum_rows(x):
    return _scan_rows(x, False)


cumsum_rows.defvjp(lambda x: (_scan_rows(x, False), None), lambda _, g: (_scan_rows(g, True),))


def _sigmoid(x):
    return jax.nn.sigmoid(x)


def _silu(x):
    return x * jax.nn.sigmoid(x)


def _softplus(x):
    e = jnp.exp(-jnp.abs(x))
    small = e * (1.0 - e * (0.5 - e * (1.0 / 3.0)))
    return jnp.maximum(x, 0.0) + jnp.where(e < 1e-3, small, jnp.log(1.0 + e))


def _masked_exp(diff, mask):
    return jnp.where(mask, jnp.exp(jnp.where(mask, diff, 0.0)), 0.0)


def _rms(x, w):
    return x * lax.rsqrt(jnp.mean(x * x, axis=-1, keepdims=True) + NORM_EPS) * w


def _cum_col_row(lg_col, lg_row):
    c = lg_col.shape[0]
    r, s = _iota((c, c), 0), _iota((c, c), 1)
    cum_col = jnp.sum(jnp.where(s <= r, jnp.broadcast_to(lg_row, (c, c)), 0.0), axis=1, keepdims=True)
    cum_row = jnp.sum(jnp.where(r <= s, jnp.broadcast_to(lg_col, (c, c)), 0.0), axis=0, keepdims=True)
    total = jnp.sum(lg_col, axis=0, keepdims=True)
    return cum_col, cum_row, total


def hgrn_chunk(seq, hp, sp, st):
    (blk,), (lb,), (nw,) = seq, hp, sp
    c = blk.shape[0]
    q_raw, f_raw, v, g_raw = (blk[:, i * HEAD:(i + 1) * HEAD] for i in range(4))
    q = _silu(q_raw)
    f = lb + (1.0 - lb) * _sigmoid(f_raw)
    logf = jnp.log(jnp.maximum(f, F_MIN))
    k = (1.0 - lb) * _sigmoid(-f_raw)
    b = cumsum_rows(logf)
    o_inter = mm_nt(q * jnp.exp(b), st)
    nsub = c // SUB
    wide = (SUB, SUB, HEAD)
    er = _iota((SUB * SUB, SUB), 0)
    e_t = (er // SUB == _iota((SUB * SUB, SUB), 1)).astype(F32)
    pr = _iota((SUB * SUB, 1), 0)
    pmask = (pr % SUB) <= (pr // SUB)
    er64 = _iota((SUB * SUB, c), 0)
    ec64 = _iota((SUB * SUB, c), 1)
    rows_c = _iota((c, 1), 0)
    row = lambda a, i: jnp.sum(jnp.where(rows_c == i, a, 0.0), axis=0, keepdims=True)
    parts = []
    for i in range(nsub):
        sl = slice(SUB * i, SUB * (i + 1))
        qi, ki, bi = q[sl], k[sl], b[sl]
        qb = jnp.broadcast_to(qi[:, None, :], wide).reshape(SUB * SUB, HEAD)
        kb = jnp.broadcast_to(ki[None, :, :], wide).reshape(SUB * SUB, HEAD)
        bd = (bi[:, None, :] - bi[None, :, :]).reshape(SUB * SUB, HEAD)
        sc_col = jnp.sum(qb * kb * _masked_exp(bd, pmask), axis=1, keepdims=True)
        place = (ec64 == (er64 % SUB) + SUB * i).astype(F32)
        sc = mm_tn(e_t, sc_col * place)
        if i > 0:
            bref = row(b, SUB * i)
            qt = qi * jnp.exp(bi - bref)
            kt = k * _masked_exp(bref - b, rows_c < SUB * i)
            sc = sc + mm_nt(qt, kt)
        parts.append(mm_nn(sc, v))
    o = o_inter + jnp.concatenate(parts, axis=0)
    bend = row(b, c - 1)
    st_new = st * jnp.exp(bend) + mm_tn(v, k * jnp.exp(bend - b))
    y = _rms(o, nw) * _silu(g_raw)
    return (y,), st_new


def ssd_chunk(seq, hp, sp, st):
    xs, bm, cm, dtc, dtr = seq
    dt_bias, a_log = hp
    c = xs.shape[0]
    lane = _iota((1, 2 * SSM_P), 1)
    first = lane < SSM_P
    r, s = _iota((c, c), 0), _iota((c, c), 1)
    g = mm_nt(cm, bm)
    dts, cums, ends, segs = [], [], [], []
    for i in range(2):
        neg_a = -jnp.exp(a_log[i])
        dt_col = _softplus(dtc[i] + dt_bias[i])
        dt_row = _softplus(dtr[i] + dt_bias[i])
        cum_col, cum_row, total = _cum_col_row(neg_a * dt_col, neg_a * dt_row)
        dts.append(dt_col)
        cums.append(cum_col)
        ends.append(total)
        segs.append(_masked_exp(cum_col - cum_row, s <= r))
    dt_l = jnp.where(first, dts[0], dts[1])
    cum_l = jnp.where(first, cums[0], cums[1])
    end_l = jnp.where(first, ends[0], ends[1])
    xdt = xs * dt_l
    y_intra = (mm_nn(g * segs[0], jnp.where(first, xdt, 0.0))
               + mm_nn(g * segs[1], jnp.where(first, 0.0, xdt)))
    y_inter = mm_nn(cm, st) * jnp.exp(cum_l)
    st_new = st * jnp.exp(end_l) + mm_tn(bm, xdt * jnp.exp(end_l - cum_l))
    return (y_intra + y_inter,), st_new


def _neumann_inverse(a):
    n = a.shape[0]
    eye = (_iota((n, n), 0) == _iota((n, n), 1)).astype(F32)
    p = -a
    t = eye + p
    for _ in range(int(math.log2(n)) - 1):
        p = _hdg(p, p, 1, 0)
        t = t + _hdg(t, p, 1, 0)
    return t


@jax.custom_vjp
def inv_unit_lower(a):
    return _neumann_inverse(a)


def _inv_fwd(a):
    t = _neumann_inverse(a)
    return t, t


inv_unit_lower.defvjp(_inv_fwd, lambda t, g: (-hd_nt(hd_tn(t, g), t),))


def gdn_chunk(seq, hp, sp, st):
    q_raw, k_raw, v, z, gbc, gac, gar = seq
    dt_bias, a_log = hp
    (nw,) = sp
    c = v.shape[0]
    r, s = _iota((c, c), 0), _iota((c, c), 1)
    q = q_raw * lax.rsqrt(jnp.sum(q_raw * q_raw, axis=-1, keepdims=True) + NORM_EPS) * (HEAD ** -0.5)
    k = k_raw * lax.rsqrt(jnp.sum(k_raw * k_raw, axis=-1, keepdims=True) + NORM_EPS)
    beta = _sigmoid(gbc)
    neg_a = -jnp.exp(a_log)
    cum, cum_row, total = _cum_col_row(neg_a * _softplus(gac + dt_bias), neg_a * _softplus(gar + dt_bias))
    decay = _masked_exp(cum - cum_row, s <= r)
    kk = mm_nt(k, k)
    a_low = jnp.where(s < r, beta * kk * decay, 0.0)
    sol = hd_nn(inv_unit_lower(a_low), jnp.concatenate([v * beta, k * (beta * jnp.exp(cum))], axis=1))
    u_base, w_corr = sol[:, :HEAD], sol[:, HEAD:]
    qk = mm_nt(q, k) * decay
    u = u_base - mm_nn(w_corr, st)
    o = mm_nn(q * jnp.exp(cum), st) + mm_nn(qk, u)
    st_new = jnp.exp(total) * st + mm_tn(k * jnp.exp(total - cum), u)
    y = _rms(o, nw) * _silu(z)
    return (y,), st_new


def normmod_fn(rows, params):
    (x,), (nw, sc, sh) = rows, params
    return (_rms(x, nw) * (1.0 + sc) + sh,)


def ssmpost_fn(rows, params):
    (y, xs, z), (d_exp, nw) = rows, params
    y = (y + d_exp * xs) * _silu(z)
    gw = y.shape[1] // 2
    return (jnp.concatenate([_rms(y[:, :gw], nw[:, :gw]), _rms(y[:, gw:], nw[:, gw:])], axis=1),)


def merge_fn(rows, params):
    (yh, ys, yg, gl), (bm, wb) = rows, params
    d = wb.shape[2]
    gates = _sigmoid(gl + bm)
    out = 0.0
    for n, y in enumerate((yh, ys, yg)):
        out = out + gates[:, n * d:(n + 1) * d] * mm_nn(y, wb[n])
    return (out,)


def outproj_fn(rows, params):
    (m, x), (g1, w) = rows, params
    return (x + (1.0 + g1) * mm_nn(m, w),)


def resid_fn(rows, params):
    (x, o), (g2,) = rows, params
    return (x + (1.0 + g2) * o,)


def _params(sem, side_effects=False):
    return pltpu.CompilerParams(dimension_semantics=sem, vmem_limit_bytes=VMEM_LIMIT, has_side_effects=side_effects)


def _whole(a):
    nd = a.ndim
    return pl.BlockSpec(a.shape, lambda *_: (0,) * nd)


def _pick(n, cands):
    for c in cands:
        if n % c == 0:
            return c
    return n


def matmul(name, a, b, mode, out_dtype):
    if mode == "nn":
        (m, k), n = a.shape, b.shape[1]
    elif mode == "nt":
        (m, k), n = a.shape, b.shape[0]
    else:
        (k, m), n = a.shape, b.shape[1]
    tm = _pick(m, (512, 256, 128))
    tn = _pick(n, (1280, 1024, 1408, 768, 512, 384, 256, 128))
    tk = _pick(k, (1024, 1280, 1408, 768, 512, 256, 128))
    if mode == "tn":
        tm = _pick(m, (1024, 768, 512, 256, 128))
        tk = _pick(k, (512, 256, 128))
    nk = k // tk
    ca, cb = {"nn": (1, 0), "nt": (1, 1), "tn": (0, 0)}[mode]

    def body(a_ref, b_ref, o_ref, acc_ref):
        kk = pl.program_id(2)

        @pl.when(kk == 0)
        def _():
            acc_ref[...] = jnp.zeros_like(acc_ref)

        acc_ref[...] += _dg(a_ref[...], b_ref[...], ca, cb)

        @pl.when(kk == nk - 1)
        def _():
            o_ref[...] = acc_ref[...].astype(o_ref.dtype)

    a_spec = (pl.BlockSpec((tk, tm), lambda i, j, q: (q, i)) if mode == "tn"
              else pl.BlockSpec((tm, tk), lambda i, j, q: (i, q)))
    b_spec = (pl.BlockSpec((tn, tk), lambda i, j, q: (j, q)) if mode == "nt"
              else pl.BlockSpec((tk, tn), lambda i, j, q: (q, j)))
    return pl.pallas_call(
        body, name=name, grid=(m // tm, n // tn, nk),
        in_specs=[a_spec, b_spec],
        out_specs=pl.BlockSpec((tm, tn), lambda i, j, q: (i, j)),
        out_shape=jax.ShapeDtypeStruct((m, n), out_dtype),
        scratch_shapes=[pltpu.VMEM((tm, tn), F32)],
        compiler_params=_params(("parallel", "parallel", "arbitrary")),
    )(a, b)


def bmatmul(name, a, b, mode, out_dtype, out_batched):
    ab, bb = a.ndim == 3, b.ndim == 3
    nb = a.shape[0] if ab else b.shape[0]
    a2, b2 = a.shape[-2:], b.shape[-2:]
    if mode == "nn":
        (m, k), n = a2, b2[1]
    elif mode == "nt":
        (m, k), n = a2, b2[0]
    else:
        (k, m), n = a2, b2[1]
    tm = _pick(m, (1024, 512, 256, 128) if mode == "tn" else (512, 256, 128))
    tn = _pick(n, (1024, 512, 256, 128))
    tk = _pick(k, (512, 256, 128) if mode == "tn" else (1024, 512, 256, 128))
    nk = k // tk
    ca, cb = {"nn": (1, 0), "nt": (1, 1), "tn": (0, 0)}[mode]
    ids = (lambda g: g) if out_batched else (lambda g: (g[2], g[0], g[1], g[3]))
    grid = (nb, m // tm, n // tn, nk) if out_batched else (m // tm, n // tn, nb, nk)

    def a_map(*g):
        bi, i, j, q = ids(g)
        idx = (q, i) if mode == "tn" else (i, q)
        return (bi,) + idx if ab else idx

    def b_map(*g):
        bi, i, j, q = ids(g)
        idx = (j, q) if mode == "nt" else (q, j)
        return (bi,) + idx if bb else idx

    def o_map(*g):
        bi, i, j, q = ids(g)
        return (bi, i, j) if out_batched else (i, j)

    def body(a_ref, b_ref, o_ref, acc_ref):
        bi, _, _, q = ids(tuple(pl.program_id(d) for d in range(4)))
        first = (q == 0) if out_batched else (q == 0) & (bi == 0)
        last = (q == nk - 1) if out_batched else (q == nk - 1) & (bi == nb - 1)

        @pl.when(first)
        def _():
            acc_ref[...] = jnp.zeros_like(acc_ref)

        acc_ref[...] += _dg(a_ref[...], b_ref[...], ca, cb)

        @pl.when(last)
        def _():
            o_ref[...] = acc_ref[...].astype(o_ref.dtype)

    a_blk = (tk, tm) if mode == "tn" else (tm, tk)
    b_blk = (tn, tk) if mode == "nt" else (tk, tn)
    return pl.pallas_call(
        body, name=name, grid=grid,
        in_specs=[pl.BlockSpec(((None,) if ab else ()) + a_blk, a_map), pl.BlockSpec(((None,) if bb else ()) + b_blk, b_map)],
        out_specs=pl.BlockSpec(((None,) if out_batched else ()) + (tm, tn), o_map),
        out_shape=jax.ShapeDtypeStruct(((nb,) if out_batched else ()) + (m, n), out_dtype),
        scratch_shapes=[pltpu.VMEM((tm, tn), F32)],
        compiler_params=_params(("parallel", "parallel", "arbitrary", "arbitrary")),
    )(a, b)


def colgather(name, src, idx, dst_w, out_dtype):
    nsrc, rows, w = src.shape
    tw = GATHER_TILE
    nbs = -(-w // tw)
    ne = idx.shape[0]
    nbd = idx.shape[1] // tw
    tiles = [sorted(set((idx[e, t * tw:(t + 1) * tw][idx[e, t * tw:(t + 1) * tw] >= 0] // tw).tolist()))
             for e in range(ne) for t in range(nbd)]
    nslot = max(1, max(len(t) for t in tiles))
    tbl = np.full((ne * nbd, nslot), -1, np.int32)
    for i, t in enumerate(tiles):
        tbl[i, :len(t)] = t
    exact3 = src.dtype == F32

    def body(tbl_ref, idx_ref, src_ref, o_ref, acc_ref):
        ti, si = pl.program_id(0), pl.program_id(1)

        @pl.when(si == 0)
        def _():
            acc_ref[...] = jnp.zeros_like(acc_ref)

        t = tbl_ref[ti * nslot + si]

        @pl.when(t >= 0)
        def _():
            onehot = ((_iota((tw, tw), 0) + t * tw) == idx_ref[...]).astype(BF16)
            col = _iota((1, tw), 1) + (t % nbs) * tw
            xv = jnp.where(col < w, src_ref[...], jnp.zeros((), src_ref.dtype))
            d = lambda p: lax.dot_general(p, onehot, (((1,), (0,)), ((), ())), preferred_element_type=F32)
            if exact3:
                x1, x2, x3 = _split3(xv)
                acc_ref[...] += (d(x3) + d(x2)) + d(x1)
            else:
                acc_ref[...] += d(xv)

        @pl.when(si == nslot - 1)
        def _():
            o_ref[...] = acc_ref[...].astype(o_ref.dtype)

    def src_map(ti, si, tbl_ref):
        t = jnp.maximum(tbl_ref[ti * nslot + si], 0)
        return (t // nbs, 0, t % nbs)

    grid_spec = pltpu.PrefetchScalarGridSpec(
        num_scalar_prefetch=1, grid=(ne * nbd, nslot),
        in_specs=[pl.BlockSpec((None, 1, tw), lambda ti, si, tbl_ref: (ti // nbd, 0, ti % nbd)),
                  pl.BlockSpec((None, rows, tw), src_map)],
        out_specs=pl.BlockSpec((None, rows, tw), lambda ti, si, tbl_ref: (ti // nbd, 0, ti % nbd)),
        scratch_shapes=[pltpu.VMEM((rows, tw), F32)])
    return pl.pallas_call(
        body, name=name, grid_spec=grid_spec,
        out_shape=jax.ShapeDtypeStruct((ne, rows, dst_w), out_dtype),
        compiler_params=_params(("parallel", "arbitrary")),
    )(jnp.asarray(tbl.reshape(-1)), jnp.asarray(idx.reshape(ne, 1, nbd * tw).astype(np.int32)), src)


def swiglu3_fwd(name, gu, tm):
    _, nb, s, w = gu.shape

    def body(x_ref, o_ref):
        o_ref[...] = (_silu(x_ref[0]) * x_ref[1]).astype(o_ref.dtype)

    return pl.pallas_call(
        body, name=name, grid=(nb, s // tm),
        in_specs=[pl.BlockSpec((2, None, tm, w), lambda b, i: (0, b, i, 0))],
        out_specs=pl.BlockSpec((None, tm, w), lambda b, i: (b, i, 0)),
        out_shape=jax.ShapeDtypeStruct((nb, s, w), BF16),
        compiler_params=_params(("parallel", "parallel")),
    )(gu)


def swiglu3_bwd(name, gu, dact, tm):
    _, nb, s, w = gu.shape

    def body(x_ref, g_ref, o_ref):
        _, vjp = jax.vjp(lambda a, b: _silu(a) * b, x_ref[0], x_ref[1])
        dg, du = vjp(g_ref[...].astype(F32))
        o_ref[0] = dg.astype(o_ref.dtype)
        o_ref[1] = du.astype(o_ref.dtype)

    return pl.pallas_call(
        body, name=name, grid=(nb, s // tm),
        in_specs=[pl.BlockSpec((2, None, tm, w), lambda b, i: (0, b, i, 0)),
                  pl.BlockSpec((None, tm, w), lambda b, i: (b, i, 0))],
        out_specs=pl.BlockSpec((2, None, tm, w), lambda b, i: (0, b, i, 0)),
        out_shape=jax.ShapeDtypeStruct(gu.shape, BF16),
        compiler_params=_params(("parallel", "parallel")),
    )(gu, dact)


def _row_specs(rows, tm):
    return [pl.BlockSpec((tm, w), lambda i, _c=c: (i, _c)) for (_, w, c) in rows]


def rowstage_fwd(name, fn, rows, params, outs, tm):
    s = rows[0][0].shape[0]
    nr, npar = len(rows), len(params)

    def body(*refs):
        r = [x[...].astype(F32) for x in refs[:nr]]
        p = [x[...].astype(F32) for x in refs[nr:nr + npar]]
        for ref, val in zip(refs[nr + npar:], fn(r, p)):
            ref[...] = val.astype(ref.dtype)

    res = pl.pallas_call(
        body, name=name, grid=(s // tm,),
        in_specs=_row_specs(rows, tm) + [_whole(p) for p in params],
        out_specs=[pl.BlockSpec((tm, w), lambda i: (i, 0)) for (w, _) in outs],
        out_shape=[jax.ShapeDtypeStruct((s, w), dt) for (w, dt) in outs],
        compiler_params=_params(("parallel",)),
    )(*[r[0] for r in rows], *params)
    return res


def rowstage_bwd(name, fn, rows, params, douts, drow_dtypes, tm, adds=None):
    s = rows[0][0].shape[0]
    nr, npar, no = len(rows), len(params), len(douts)
    adds = adds or {}
    add_idx = sorted(adds)
    na = len(add_idx)

    def body(*refs):
        r = [x[...].astype(F32) for x in refs[:nr]]
        p = [x[...].astype(F32) for x in refs[nr:nr + npar]]
        g = [x[...].astype(F32) for x in refs[nr + npar:nr + npar + no]]
        a_refs = refs[nr + npar + no:nr + npar + no + na]
        dr_refs = refs[nr + npar + no + na:nr + npar + no + na + nr]
        dp_refs = refs[nr + npar + no + na + nr:]
        _, vjp = jax.vjp(lambda r_, p_: tuple(fn(r_, p_)), r, p)
        dr, dp = vjp(tuple(g))
        for j, (ref, val) in enumerate(zip(dr_refs, dr)):
            if j in adds:
                val = val + a_refs[add_idx.index(j)][...].astype(F32)
            ref[...] = val.astype(ref.dtype)

        @pl.when(pl.program_id(0) == 0)
        def _():
            for ref in dp_refs:
                ref[...] = jnp.zeros_like(ref)

        for ref, val in zip(dp_refs, dp):
            ref[...] += val

    res = pl.pallas_call(
        body, name=name, grid=(s // tm,),
        in_specs=(_row_specs(rows, tm) + [_whole(p) for p in params]
                  + [pl.BlockSpec((tm, d.shape[1]), lambda i: (i, 0)) for d in douts]
                  + [pl.BlockSpec((tm, rows[j][1]), lambda i: (i, 0)) for j in add_idx]),
        out_specs=([pl.BlockSpec((tm, w), lambda i: (i, 0)) for (_, w, _) in rows] + [_whole(p) for p in params]),
        out_shape=([jax.ShapeDtypeStruct((s, w), dt) for (_, w, _), dt in zip(rows, drow_dtypes)]
                   + [jax.ShapeDtypeStruct(p.shape, F32) for p in params]),
        compiler_params=_params(("arbitrary",)),
    )(*[r[0] for r in rows], *params, *douts, *[adds[j] for j in add_idx])
    return res[:nr], res[nr:]


def _flip(index_map, nc):
    return lambda h, n: index_map(h, nc - 1 - n)


def _with_side(core, n_in, n_out, side, grid):
    if side is None:
        return core, [], [], [], [], ()
    sends, broadcast = side
    k = len(sends)

    def body(*refs):
        ins, snd = refs[:n_in], refs[n_in:n_in + k]
        outs, rcv = refs[n_in + k:n_in + k + n_out], refs[n_in + k + n_out:n_in + 2 * k + n_out]
        scr = refs[n_in + 2 * k + n_out:]
        start, wait = _exchange_ops(snd, rcv, *scr[1:], broadcast)
        ids = [pl.program_id(d) for d in range(len(grid))]
        first = functools.reduce(lambda a, b: a & b, [i == 0 for i in ids])
        last = functools.reduce(lambda a, b: a & b, [i == g - 1 for i, g in zip(ids, grid)])
        pl.when(first)(start)
        core(*ins, *outs, scr[0])
        pl.when(last)(wait)

    return body, [HBM_SPEC] * k, [HBM_SPEC] * k, _exchange_out(sends, broadcast), _exchange_sems(k), tuple(sends)


def _take(v, split, j):
    if split is None:
        return v
    if split[0] == "lane":
        return v[:, j * split[1]:(j + 1) * split[1]]
    if split[0] == "lead":
        return v[j * split[1]:(j + 1) * split[1]]
    return v[j]


def _heads(vals, specs, hb):
    return [v if s[-1] is None else jnp.stack([_take(v, s[-1], j) for j in range(hb)]) for v, s in zip(vals, specs)]


def _over_heads(chunk_fn, hb, seqs, hparams, batched):
    seq_ax = [None if s[3] is None else 0 for s in seqs]
    hp_ax = [None if s[3] is None else 0 for s in hparams]
    if batched:
        return jax.vmap(chunk_fn, in_axes=(seq_ax, hp_ax, None, 0))

    def looped(seq, hp, sp, st):
        pick = lambda vals, axes, j: [v if a is None else v[j] for v, a in zip(vals, axes)]
        res = [chunk_fn(pick(seq, seq_ax, j), pick(hp, hp_ax, j), sp, st[j]) for j in range(hb)]
        pile = lambda parts: jnp.concatenate([p[None] for p in parts], axis=0)
        return tuple(pile(o) for o in zip(*[r[0] for r in res])), pile([r[1] for r in res])

    return looped


def _where(split, j):
    if split[0] == "lane":
        return (slice(None), slice(j * split[1], (j + 1) * split[1]))
    if split[0] == "lead":
        return (slice(j * split[1], (j + 1) * split[1]),)
    return (j,)


def scan_fwd(name, chunk_fn, nblk, hb, nc, seqs, hparams, sparams, state_shape, outs, batched, side=None):
    ns, nhp, nsp, no = len(seqs), len(hparams), len(sparams), len(outs)

    def core(*refs):
        seq_r, hp_r, sp_r = refs[:ns], refs[ns:ns + nhp], refs[ns + nhp:ns + nhp + nsp]
        out_r = refs[ns + nhp + nsp:ns + nhp + nsp + no]
        st_out, st_scr = refs[-2], refs[-1]

        @pl.when(pl.program_id(1) == 0)
        def _():
            st_scr[...] = jnp.zeros_like(st_scr)

        seq_v = [x[...].astype(F32) for x in seq_r]
        hp_v = [x[...] for x in hp_r]
        sp_v = [x[...] for x in sp_r]
        st = st_scr[...]
        st_out[...] = st
        heads = _over_heads(chunk_fn, hb, seqs, hparams, batched)
        o, st_new = heads(_heads(seq_v, seqs, hb), _heads(hp_v, hparams, hb), sp_v, st)
        for ref, spec, val in zip(out_r, outs, o):
            for j in range(hb):
                ref[_where(spec[4], j)] = val[j].astype(ref.dtype)
        st_scr[...] = st_new

    nst = len(state_shape)
    body, s_in, s_out, s_shape, s_scr, s_args = _with_side(core, ns + nhp + nsp, no + 1, side, (nblk, nc))
    res = pl.pallas_call(
        body, name=name, grid=(nblk, nc),
        in_specs=([pl.BlockSpec(bs, im) for (_, bs, im, _) in seqs]
                  + [pl.BlockSpec(bs, lambda h, n, _im=im: _im(h)) for (_, bs, im, _) in hparams]
                  + [_whole(p) for p in sparams] + s_in),
        out_specs=([pl.BlockSpec(bs, im) for (_, _, bs, im, _) in outs]
                   + [pl.BlockSpec((hb, None) + tuple(state_shape), lambda h, n: (h, n) + (0,) * nst)] + s_out),
        out_shape=([jax.ShapeDtypeStruct(fs, dt) for (fs, dt, _, _, _) in outs]
                   + [jax.ShapeDtypeStruct((nblk * hb, nc) + tuple(state_shape), F32)] + s_shape),
        scratch_shapes=[pltpu.VMEM((hb,) + tuple(state_shape), F32)] + s_scr,
        compiler_params=_params(("arbitrary", "arbitrary"), side is not None),
    )(*[x[0] for x in seqs], *[x[0] for x in hparams], *sparams, *s_args)
    return res[:no], res[no], res[no + 1:]


def scan_bwd(name, chunk_fn, nblk, hb, nc, seqs, hparams, sparams, state_shape, states, douts, dseqs, batched, side=None):
    ns, nhp, nsp, no = len(seqs), len(hparams), len(sparams), len(douts)
    nst = len(state_shape)

    def core(*refs):
        seq_r, hp_r, sp_r = refs[:ns], refs[ns:ns + nhp], refs[ns + nhp:ns + nhp + nsp]
        base = ns + nhp + nsp
        st_r = refs[base]
        do_r = refs[base + 1:base + 1 + no]
        base += 1 + no
        ds_r, dhp_r, dsp_r = refs[base:base + ns], refs[base + ns:base + ns + nhp], refs[base + ns + nhp:base + ns + nhp + nsp]
        dst_scr = refs[-1]
        h, n = pl.program_id(0), pl.program_id(1)

        @pl.when(n == 0)
        def _():
            dst_scr[...] = jnp.zeros_like(dst_scr)
            for ref in dhp_r:
                ref[...] = jnp.zeros_like(ref)

        @pl.when((n == 0) & (h == 0))
        def _():
            for ref in dsp_r:
                ref[...] = jnp.zeros_like(ref)

        seq_v = [x[...].astype(F32) for x in seq_r]
        hp_v = [x[...] for x in hp_r]
        sp_v = [x[...] for x in sp_r]
        do_v = [x[...].astype(F32) for x in do_r]
        prim = (_heads(seq_v, seqs, hb), _heads(hp_v, hparams, hb), sp_v, st_r[...])
        _, vjp = jax.vjp(_over_heads(chunk_fn, hb, seqs, hparams, batched), *prim)
        ds, dhp, dsp, dst = vjp((tuple(_heads(do_v, douts, hb)), dst_scr[...]))
        for ref, spec, val in zip(ds_r, dseqs, ds):
            if spec[4] is None:
                ref[...] = val.astype(ref.dtype)
            else:
                for j in range(hb):
                    ref[_where(spec[4], j)] = val[j].astype(ref.dtype)
        for ref, spec, val in zip(dhp_r, hparams, dhp):
            for j in range(hb):
                ref[_where(spec[3], j)] += val[j]
        for ref, val in zip(dsp_r, dsp):
            ref[...] += val
        dst_scr[...] = dst

    n_in, n_out = ns + nhp + nsp + 1 + no, ns + nhp + nsp
    body, s_in, s_out, s_shape, s_scr, s_args = _with_side(core, n_in, n_out, side, (nblk, nc))
    res = pl.pallas_call(
        body, name=name, grid=(nblk, nc),
        in_specs=([pl.BlockSpec(bs, _flip(im, nc)) for (_, bs, im, _) in seqs]
                  + [pl.BlockSpec(bs, lambda h, n, _im=im: _im(h)) for (_, bs, im, _) in hparams]
                  + [_whole(p) for p in sparams]
                  + [pl.BlockSpec((hb, None) + tuple(state_shape), lambda h, n: (h, nc - 1 - n) + (0,) * nst)]
                  + [pl.BlockSpec(bs, _flip(im, nc)) for (_, bs, im, _) in douts] + s_in),
        out_specs=([pl.BlockSpec(bs, _flip(im, nc)) for (_, _, bs, im, _) in dseqs]
                   + [pl.BlockSpec(bs, lambda h, n, _im=im: _im(h)) for (_, bs, im, _) in hparams]
                   + [_whole(p) for p in sparams] + s_out),
        out_shape=([jax.ShapeDtypeStruct(fs, dt) for (fs, dt, _, _, _) in dseqs]
                   + [jax.ShapeDtypeStruct(x[0].shape, F32) for x in hparams]
                   + [jax.ShapeDtypeStruct(p.shape, F32) for p in sparams] + s_shape),
        scratch_shapes=[pltpu.VMEM((hb,) + tuple(state_shape), F32)] + s_scr,
        compiler_params=_params(("arbitrary", "arbitrary"), side is not None),
    )(*[x[0] for x in seqs], *[x[0] for x in hparams], *sparams, states, *[x[0] for x in douts], *s_args)
    return res[:ns], res[ns:ns + nhp], res[ns + nhp:n_out], res[n_out:]


def _shift_down(x, n, rows):
    if n == 0:
        return x
    return jnp.where(rows >= n, pltpu.roll(x, n, 0), 0.0)


def _shift_up(x, n, rows):
    if n == 0:
        return x
    s = x.shape[0]
    return jnp.where(rows < s - n, pltpu.roll(x, s - n, 0), 0.0)


def conv_fwd(name, x, col0, w, b):
    s, cw = x.shape[0], w.shape[1]

    def body(x_ref, w_ref, b_ref, o_ref):
        xv = x_ref[...]
        rows = _iota(xv.shape, 0)
        u = jnp.broadcast_to(b_ref[...], xv.shape)
        for j in range(CONV_K):
            u = u + w_ref[j:j + 1, :] * _shift_down(xv, CONV_K - 1 - j, rows)
        o_ref[...] = _silu(u)

    return pl.pallas_call(
        body, name=name, grid=(cw // LANES,),
        in_specs=[pl.BlockSpec((s, LANES), lambda j: (0, col0 + j)),
                  pl.BlockSpec((CONV_K, LANES), lambda j: (0, j)),
                  pl.BlockSpec((1, LANES), lambda j: (0, j))],
        out_specs=pl.BlockSpec((s, LANES), lambda j: (0, j)),
        out_shape=jax.ShapeDtypeStruct((s, cw), F32),
        compiler_params=_params(("parallel",)),
    )(x, w, b)


def conv_bwd(name, x, col0, w, b, dout):
    s, cw = x.shape[0], w.shape[1]

    def body(x_ref, w_ref, b_ref, g_ref, dx_ref, dw_ref, db_ref):
        xv = x_ref[...]
        rows = _iota(xv.shape, 0)
        sh = [_shift_down(xv, CONV_K - 1 - j, rows) for j in range(CONV_K)]
        u = jnp.broadcast_to(b_ref[...], xv.shape)
        for j in range(CONV_K):
            u = u + w_ref[j:j + 1, :] * sh[j]
        sg = _sigmoid(u)
        du = g_ref[...] * (sg * (1.0 + u * (1.0 - sg)))
        dx = jnp.zeros_like(xv)
        for j in range(CONV_K):
            dx = dx + w_ref[j:j + 1, :] * _shift_up(du, CONV_K - 1 - j, rows)
            dw_ref[j:j + 1, :] = jnp.sum(du * sh[j], axis=0, keepdims=True)
        dx_ref[...] = dx.astype(dx_ref.dtype)
        db_ref[...] = jnp.sum(du, axis=0, keepdims=True)

    return pl.pallas_call(
        body, name=name, grid=(cw // LANES,),
        in_specs=[pl.BlockSpec((s, LANES), lambda j: (0, col0 + j)),
                  pl.BlockSpec((CONV_K, LANES), lambda j: (0, j)),
                  pl.BlockSpec((1, LANES), lambda j: (0, j)),
                  pl.BlockSpec((s, LANES), lambda j: (0, j))],
        out_specs=[pl.BlockSpec((s, LANES), lambda j: (0, j)),
                   pl.BlockSpec((CONV_K, LANES), lambda j: (0, j)),
                   pl.BlockSpec((1, LANES), lambda j: (0, j))],
        out_shape=[jax.ShapeDtypeStruct((s, cw), BF16), jax.ShapeDtypeStruct((CONV_K, cw), F32),
                   jax.ShapeDtypeStruct((1, cw), F32)],
        compiler_params=_params(("parallel",)),
    )(x, w, b, dout)


def exchange(name, sends, broadcast):
    nop = len(sends)

    def body(*refs):
        start, wait = _exchange_ops(refs[:nop], refs[nop:2 * nop], *refs[2 * nop:], broadcast)
        start()
        wait()

    return pl.pallas_call(
        body, name=name,
        in_specs=[HBM_SPEC] * nop, out_specs=[HBM_SPEC] * nop,
        out_shape=_exchange_out(sends, broadcast), scratch_shapes=_exchange_sems(nop),
        compiler_params=pltpu.CompilerParams(has_side_effects=True),
    )(*sends)


HBM_SPEC = pl.BlockSpec(memory_space=pltpu.HBM)


def _exchange_out(sends, broadcast):
    return [jax.ShapeDtypeStruct((N_DEV,) + tuple(t.shape if broadcast else t.shape[1:]), t.dtype) for t in sends]


def _exchange_sems(nop):
    return [pltpu.SemaphoreType.DMA((nop * (N_DEV - 1),)), pltpu.SemaphoreType.DMA((nop * (N_DEV - 1),)),
            pltpu.SemaphoreType.DMA((nop,))]


def _exchange_ops(send_refs, recv_refs, send_sems, recv_sems, local_sems, broadcast):
    nop = len(send_refs)
    x, y, c = lax.axis_index("x"), lax.axis_index("y"), lax.axis_index("c")
    me = 4 * x + 2 * y + c
    peers = []
    for k in range(1, N_DEV):
        px = 1 - x if (k >> 2) & 1 else x
        py = 1 - y if (k >> 1) & 1 else y
        pc = 1 - c if k & 1 else c
        peers.append(((px, py, pc), 4 * px + 2 * py + pc))

    def src(i, peer):
        return send_refs[i] if broadcast else send_refs[i].at[peer]

    def remote(i, k, arrival):
        dev, peer = peers[k]
        return pltpu.make_async_remote_copy(
            src_ref=src(i, peer), dst_ref=recv_refs[i].at[peer if arrival else me],
            send_sem=send_sems.at[i * (N_DEV - 1) + k], recv_sem=recv_sems.at[i * (N_DEV - 1) + k],
            device_id=dev, device_id_type=pl.DeviceIdType.MESH)

    def local(i):
        return pltpu.make_async_copy(src(i, me), recv_refs[i].at[me], local_sems.at[i])

    def start():
        for i in range(nop):
            local(i).start()
        for k in range(N_DEV - 1):
            for i in range(nop):
                remote(i, k, False).start()

    def wait():
        for k in range(N_DEV - 1):
            for i in range(nop):
                remote(i, k, True).wait_recv()
        for k in range(N_DEV - 1):
            for i in range(nop):
                remote(i, k, False).wait_send()
        for i in range(nop):
            local(i).wait()

    return start, wait


def adamw_sum(name, parts, w, m, v, layer=None, into=None):
    rws, cols = w.shape[-2:]
    nsum = parts.shape[0]
    tr = _pick(rws, (256, 128, 64, 32, 16, 8))
    c1 = 1.0 / (1.0 - ADAM_B1 ** ADAM_STEP)
    c2 = 1.0 / (1.0 - ADAM_B2 ** ADAM_STEP)

    def body(p_ref, w_ref, m_ref, v_ref, *rest):
        g_ref, d_ref, nm_ref, nv_ref = rest[-4:]
        g = p_ref[0]
        for j in range(1, nsum):
            g = g + p_ref[j]
        nm = ADAM_B1 * m_ref[...] + (1.0 - ADAM_B1) * g
        nv = ADAM_B2 * v_ref[...] + (1.0 - ADAM_B2) * (g * g)
        g_ref[...] = g
        nm_ref[...] = nm
        nv_ref[...] = nv
        d_ref[...] = -ADAM_LR * ((nm * c1) / (jnp.sqrt(nv * c2) + ADAM_EPS) + ADAM_WD * w_ref[...])

    if layer is None:
        blk = pl.BlockSpec((tr, cols), lambda i: (i, 0))
    else:
        blk = pl.BlockSpec((None, tr, cols), lambda i: (layer, i, 0))
    if into is None and layer is not None:
        into = [lax.empty(w.shape, F32) for _ in range(4)]
    extra = list(into) if into else []
    return pl.pallas_call(
        body, name=name, grid=(rws // tr,),
        in_specs=([pl.BlockSpec((nsum, tr, cols), lambda i: (0, i, 0)), blk, blk, blk]
                  + [pl.BlockSpec(memory_space=pl.ANY)] * len(extra)),
        out_specs=[blk, blk, blk, blk],
        out_shape=[jax.ShapeDtypeStruct(w.shape, F32)] * 4,
        input_output_aliases={4 + k: k for k in range(len(extra))},
        compiler_params=_params(("parallel",)),
    )(parts, w, m, v, *extra)


def ada_fwd(name, c_all, w, b):
    nl = w.shape[0]

    def body(c_ref, w_ref, b_ref, o_ref):
        ca = _silu(c_ref[...])
        for l in range(nl):
            o_ref[l] = mm_nn(ca, w_ref[l]) + b_ref[l]

    return pl.pallas_call(
        body, name=name,
        out_shape=jax.ShapeDtypeStruct((nl, c_all.shape[0], w.shape[2]), F32),
        compiler_params=pltpu.CompilerParams(vmem_limit_bytes=VMEM_LIMIT),
    )(c_all, w, b)


def ada_bwd(name, c_all, dmod):
    nl = dmod.shape[0]

    def body(c_ref, g_ref, o_ref):
        ca = _silu(c_ref[...])
        for l in range(nl):
            o_ref[l] = mm_tn(ca, g_ref[l])

    return pl.pallas_call(
        body, name=name,
        out_shape=jax.ShapeDtypeStruct((nl, c_all.shape[1], dmod.shape[2]), F32),
        compiler_params=pltpu.CompilerParams(vmem_limit_bytes=VMEM_LIMIT),
    )(c_all, dmod)


def lower_bounds_fn(rows, params):
    (lg,), _ = rows, params
    nl = lg.shape[0]
    mx = jnp.max(lg, axis=0, keepdims=True)
    e = jnp.exp(lg - mx)
    p = e / jnp.sum(e, axis=0, keepdims=True)
    layer = _iota((nl, 1), 0)
    acc = jnp.zeros_like(p)
    for j in range(1, nl):
        pj = jnp.sum(jnp.where(layer == j, p, 0.0), axis=0, keepdims=True)
        acc = acc + jnp.where(layer >= j, 1.0, 0.0) * pj
    return (acc,)


def loss_call(name, x, tgt, nw, tm):
    s, d = x.shape

    def body(x_ref, t_ref, w_ref, l_ref, dx_ref, dw_ref):
        def f(xv, wv):
            err = _rms(xv, wv) - t_ref[...]
            return jnp.sum(0.5 * jnp.mean(err * err, axis=-1, keepdims=True), axis=0, keepdims=True)

        val, vjp = jax.vjp(f, x_ref[...], w_ref[...])
        dx, dw = vjp(jnp.ones_like(val))

        @pl.when(pl.program_id(0) == 0)
        def _():
            l_ref[...] = jnp.zeros_like(l_ref)
            dw_ref[...] = jnp.zeros_like(dw_ref)

        l_ref[...] += jnp.broadcast_to(val, l_ref.shape)
        dw_ref[...] += dw
        dx_ref[...] = dx

    row = pl.BlockSpec((tm, d), lambda i: (i, 0))
    return pl.pallas_call(
        body, name=name, grid=(s // tm,),
        in_specs=[row, row, _whole(nw)],
        out_specs=[pl.BlockSpec((8, LANES), lambda i: (0, 0)), row, _whole(nw)],
        out_shape=[jax.ShapeDtypeStruct((8, LANES), F32), jax.ShapeDtypeStruct((s, d), F32),
                   jax.ShapeDtypeStruct(nw.shape, F32)],
        compiler_params=_params(("arbitrary",)),
    )(x, tgt, nw)


class Dims:
    def __init__(self, s, d, ffn):
        self.s, self.d, self.ffn = s, d, ffn
        self.mix = 3 * d // 4
        self.nh = self.mix // HEAD
        self.ssm_heads = self.mix // SSM_P
        self.pairs = self.mix // (2 * SSM_P)
        self.nc = s // CHUNK
        self.conv_ssm = self.mix + 4 * HEAD
        self.conv_w = self.conv_ssm + 3 * self.mix
        self.o_gates = 4 * self.mix
        self.o_sz = self.o_gates + 3 * d
        self.o_gz = self.o_sz + self.mix
        self.o_conv = self.o_gz + self.mix
        self.o_small = self.o_conv + self.conv_w
        used = self.o_small + LANES
        self.np = -(-used // 1280) * 1280
        self.tm = _pick(s, (256, 128, 64))
        mix, nh = self.mix, self.nh
        self.in_sizes = (mix, mix, mix, mix, mix, self.conv_ssm, self.ssm_heads, 3 * mix, mix, nh, nh, 3 * d)
        self.in_width = sum(self.in_sizes)


def w_in_tables(dm, nshard):
    off = np.cumsum((0,) + dm.in_sizes)
    hq, hf, hi, hg, sz, sxbc, sdt, gqkv, gz, gb, ga, gates = (np.arange(off[i], off[i + 1]) for i in range(12))
    hgrn = np.stack([t.reshape(dm.nh, HEAD) for t in (hq, hf, hi, hg)], axis=1).reshape(-1)
    perm = np.concatenate([hgrn, gates, sz, gz, gqkv, sxbc, sdt, gb, ga])
    perm = np.concatenate([perm, np.full(dm.np - perm.size, -1)])
    shard = dm.in_width // nshard
    wpad = -(-shard // GATHER_TILE) * GATHER_TILE
    fwd = np.where(perm >= 0, (perm // shard) * wpad + perm % shard, -1)[None]
    inv = np.zeros(dm.in_width, np.int64)
    inv[perm[perm >= 0]] = np.nonzero(perm >= 0)[0]
    bwd = np.full((nshard, wpad), -1)
    bwd[:, :shard] = inv.reshape(nshard, shard)
    return fwd.astype(np.int32), bwd.astype(np.int32)


def _small_views(dm, small):
    t = small.T
    col = lambda a: a[:, :, None]
    row = lambda a: a.reshape(a.shape[0], dm.nc, 1, CHUNK)
    a, b = dm.ssm_heads, dm.ssm_heads + dm.nh
    sdt, gb, ga = t[:a], t[a:b], t[b:b + dm.nh]
    return col(sdt), row(sdt), col(gb), col(ga), row(ga)


def _scan_specs(dm, proj, conv_out, views, lp):
    dt_col, dt_row, gb_col, ga_col, ga_row = views
    mixb, nh = dm.mix // LANES, dm.nh
    s, mix = dm.s, dm.mix
    lane = ("lane", LANES)
    hb = HGRN_HEADS_PER_STEP
    hw = (CHUNK, hb * LANES)
    hgrn = dict(
        nblk=nh // hb, hb=hb, fn=hgrn_chunk, batched=False,
        seqs=[(proj, (CHUNK, hb * 4 * HEAD), lambda h, n: (n, h), ("lane", 4 * HEAD))],
        hparams=[(lp["lb"], (1, hb * HEAD), lambda h: (0, h), lane)],
        sparams=[lp["hgrn_norm"]],
        dseqs=[((s, 4 * mix), BF16, (CHUNK, hb * 4 * HEAD), lambda h, n: (n, h), ("lane", 4 * HEAD))],
        io=(hw, lambda h, n: (n, h), lane))
    ppg = dm.pairs // 2
    qb = 3 * mixb
    gw = (CHUNK, ppg * LANES)
    pcol = ((2 * ppg, CHUNK, 1), lambda g, n: (g, n, 0), ("lead", 2))
    prow = ((2 * ppg, None, 1, CHUNK), lambda g, n: (g, n, 0, 0), ("lead", 2))
    ppar = ((2 * ppg, 1, 1), lambda g: (g, 0, 0), ("lead", 2))
    bc = lambda first: ((CHUNK, LANES), lambda g, n: (n, first + g), None)
    ssd = dict(
        nblk=2, hb=ppg, fn=ssd_chunk, batched=False,
        seqs=[(conv_out, gw, lambda g, n: (n, qb // ppg + g), lane), (conv_out,) + bc(qb + mixb), (conv_out,) + bc(qb + mixb + 2),
              (dt_col,) + pcol, (dt_row,) + prow],
        hparams=[(lp["ssm_dt_bias"],) + ppar, (lp["ssm_a_log"],) + ppar],
        sparams=[],
        dseqs=[((s, mix), F32, gw, lambda g, n: (n, g), lane), ((s, 2 * LANES), F32) + bc(0), ((s, 2 * LANES), F32) + bc(0),
               (dt_col.shape, F32) + pcol, (dt_row.shape, F32) + prow],
        io=(gw, lambda g, n: (n, g), lane))
    hb = GDN_HEADS_PER_STEP
    hw = (CHUNK, hb * LANES)
    cq, cgz = 0, dm.o_gz // LANES
    assert nh % hb == 0 and cgz % hb == 0 and qb % ppg == 0
    hcol = ((hb, CHUNK, 1), lambda h, n: (h, n, 0), ("idx",))
    hrow = ((hb, None, 1, CHUNK), lambda h, n: (h, n, 0, 0), ("idx",))
    hpar = ((hb, 1, 1), lambda h: (h, 0, 0), ("idx",))
    at = lambda first: (hw, lambda h, n: (n, first // hb + h), lane)
    gdn = dict(
        nblk=nh // hb, hb=hb, fn=gdn_chunk, batched=True,
        seqs=[(conv_out,) + at(cq), (conv_out,) + at(cq + nh), (conv_out,) + at(cq + 2 * nh), (proj,) + at(cgz),
              (gb_col,) + hcol, (ga_col,) + hcol, (ga_row,) + hrow],
        hparams=[(lp["gdn_dt_bias"],) + hpar, (lp["gdn_a_log"],) + hpar],
        sparams=[lp["gdn_norm"]],
        dseqs=[((s, mix), F32) + at(0), ((s, mix), F32) + at(0), ((s, mix), F32) + at(0), ((s, mix), BF16) + at(0),
               (gb_col.shape, F32) + hcol, (ga_col.shape, F32) + hcol, (ga_row.shape, F32) + hrow],
        io=(hw, lambda h, n: (n, h), lane))
    return hgrn, ssd, gdn


def _run_scan_fwd(dm, name, sp, side=None):
    out = ((dm.s, dm.mix), F32) + sp["io"]
    (y,), states, arrived = scan_fwd(name, sp["fn"], sp["nblk"], sp["hb"], dm.nc, sp["seqs"], sp["hparams"],
                                     sp["sparams"], (HEAD, HEAD), [out], sp["batched"], side)
    return y, states, arrived


def _run_scan_bwd(dm, name, sp, states, dy, side=None):
    return scan_bwd(name, sp["fn"], sp["nblk"], sp["hb"], dm.nc, sp["seqs"], sp["hparams"], sp["sparams"], (HEAD, HEAD),
                    states, [(dy,) + sp["io"]], sp["dseqs"], sp["batched"], side)


def _share_out(side):
    if side is None:
        return None, None, None
    s, broadcast = side
    return ([s[0]], broadcast), ([s[1], s[2], s[4]], broadcast), ([s[3]], broadcast)


def _collect(got_h, got_s, got_g):
    if not got_h:
        return None
    return [got_h[0], got_s[0], got_s[1], got_g[0], got_s[2]]


def layer_fwd(dm, l, x, lp, side=None):
    tm, d, mix = dm.tm, dm.d, dm.mix
    tag = f"l{l}_"
    (h,) = rowstage_fwd(tag + "norm1", normmod_fn, [(x, d, 0)], [lp["norm_mix"], lp["sc1"], lp["sh1"]], [(d, BF16)], tm)
    proj = matmul(tag + "proj", h, lp["w_in"], "nn", F32)
    conv_out = conv_fwd(tag + "conv", proj, dm.o_conv // LANES, lp["conv_w"], lp["conv_b"])
    small = proj[:, dm.o_small:dm.o_small + LANES]
    views = _small_views(dm, small)
    hg, sd, gd = _scan_specs(dm, proj, conv_out, views, lp)
    side_h, side_s, side_g = _share_out(side)
    yh, st_h, got_h = _run_scan_fwd(dm, tag + "hgrn", hg, side_h)
    y_ssd, st_s, got_s = _run_scan_fwd(dm, tag + "ssd", sd, side_s)
    yg, st_g, got_g = _run_scan_fwd(dm, tag + "gdn", gd, side_g)
    arrived = _collect(got_h, got_s, got_g)
    (ys,) = rowstage_fwd(tag + "ssmpost", ssmpost_fn,
                         [(y_ssd, mix, 0), (conv_out, mix, 3), (proj, mix, dm.o_sz // mix)],
                         [lp["ssm_d_exp"], lp["ssm_norm"]], [(mix, F32)], tm)
    (merged,) = rowstage_fwd(tag + "merge", merge_fn, [(yh, mix, 0), (ys, mix, 0), (yg, mix, 0), (proj, 3 * d, 1)],
                             [lp["b_merge"], lp["w_branch"]], [(d, BF16)], tm)
    (x1,) = rowstage_fwd(tag + "outproj", outproj_fn, [(merged, d, 0), (x, d, 0)], [lp["g1"], lp["w_out"]], [(d, F32)], tm)
    (h2,) = rowstage_fwd(tag + "norm2", normmod_fn, [(x1, d, 0)], [lp["norm_ffn"], lp["sc2"], lp["sh2"]], [(d, BF16)], tm)
    gu = bmatmul(tag + "ffn_in", h2, lp["w_ffn_in"], "nn", F32, True)
    gu = gu.reshape((2, gu.shape[0] // 2) + gu.shape[1:])
    act = swiglu3_fwd(tag + "swiglu", gu, tm)
    o2 = bmatmul(tag + "ffn_out", act, lp["w_ffn_out"], "nn", F32, False)
    (x2,) = rowstage_fwd(tag + "resid", resid_fn, [(x1, d, 0), (o2, d, 0)], [lp["g2"]], [(d, F32)], tm)
    saved = dict(x=x, h=h, proj=proj, conv_out=conv_out, views=views, yh=yh, y_ssd=y_ssd, yg=yg, ys=ys,
                 st_h=st_h, st_s=st_s, st_g=st_g, merged=merged, x1=x1, h2=h2, gu=gu, act=act, o2=o2)
    return x2, saved, arrived


def layer_bwd(dm, l, dx2, lp, sv, side=None):
    tm, d, mix, s = dm.tm, dm.d, dm.mix, dm.s
    tag = f"l{l}_b_"
    g = {}
    (dx1_a, do2), (g["g2"],) = rowstage_bwd(tag + "resid", resid_fn, [(sv["x1"], d, 0), (sv["o2"], d, 0)], [lp["g2"]],
                                            [dx2], [F32, BF16], tm)
    dact = bmatmul(tag + "ffn_out_dx", do2, lp["w_ffn_out"], "nt", BF16, True)
    g["w_ffn_out"] = bmatmul(tag + "ffn_out_dw", sv["act"], do2, "tn", F32, True)
    dgu = swiglu3_bwd(tag + "swiglu", sv["gu"], dact, tm)
    dgu = dgu.reshape((-1,) + dgu.shape[2:])
    dh2 = bmatmul(tag + "ffn_in_dx", dgu, lp["w_ffn_in"], "nt", BF16, False)
    g["w_ffn_in"] = bmatmul(tag + "ffn_in_dw", sv["h2"], dgu, "tn", F32, True)
    (dx1,), (g["norm_ffn"], g["sc2"], g["sh2"]) = rowstage_bwd(
        tag + "norm2", normmod_fn, [(sv["x1"], d, 0)], [lp["norm_ffn"], lp["sc2"], lp["sh2"]], [dh2], [F32], tm,
        adds={0: dx1_a})
    (dmerged, dx_a), (g["g1"], g["w_out"]) = rowstage_bwd(
        tag + "outproj", outproj_fn, [(sv["merged"], d, 0), (sv["x"], d, 0)], [lp["g1"], lp["w_out"]], [dx1],
        [BF16, F32], tm)
    proj, conv_out = sv["proj"], sv["conv_out"]
    (dyh, dys, dyg, dgates), (g["b_merge"], g["w_branch"]) = rowstage_bwd(
        tag + "merge", merge_fn, [(sv["yh"], mix, 0), (sv["ys"], mix, 0), (sv["yg"], mix, 0), (proj, 3 * d, 1)],
        [lp["b_merge"], lp["w_branch"]], [dmerged], [F32, F32, F32, BF16], tm)
    (dy_ssd, dxs_a, dsz), (g["ssm_d_exp"], g["ssm_norm"]) = rowstage_bwd(
        tag + "ssmpost", ssmpost_fn, [(sv["y_ssd"], mix, 0), (conv_out, mix, 3), (proj, mix, dm.o_sz // mix)],
        [lp["ssm_d_exp"], lp["ssm_norm"]], [dys], [F32, F32, BF16], tm)
    hg, sd, gd = _scan_specs(dm, proj, conv_out, sv["views"], lp)
    side_h, side_s, side_g = _share_out(side)
    (dhgrn,), (g["lb"],), (g["hgrn_norm"],), got_h = _run_scan_bwd(dm, tag + "hgrn", hg, sv["st_h"], dyh, side_h)
    (dxs_b, dbp, dcp, d_dt_col, d_dt_row), (g["ssm_dt_bias"], g["ssm_a_log"]), _, got_s = _run_scan_bwd(
        dm, tag + "ssd", sd, sv["st_s"], dy_ssd, side_s)
    (dq, dk, dv, dgz, d_gb_col, d_ga_col, d_ga_row), (g["gdn_dt_bias"], g["gdn_a_log"]), (g["gdn_norm"],), got_g = _run_scan_bwd(
        dm, tag + "gdn", gd, sv["st_g"], dyg, side_g)
    arrived = _collect(got_h, got_s, got_g)
    dconv = jnp.concatenate([dq, dk, dv, dxs_a + dxs_b, dbp, dcp], axis=1)
    dpc, g["conv_w"], g["conv_b"] = conv_bwd(tag + "conv", proj, dm.o_conv // LANES, lp["conv_w"], lp["conv_b"], dconv)
    unrow = lambda t: t.reshape(t.shape[0], s).T
    dsmall = jnp.concatenate([d_dt_col[:, :, 0].T + unrow(d_dt_row), d_gb_col[:, :, 0].T,
                              d_ga_col[:, :, 0].T + unrow(d_ga_row)], axis=1)
    pad = jnp.zeros((s, dm.np - dm.o_small - dsmall.shape[1]), BF16)
    dproj = jnp.concatenate([dhgrn, dgates, dsz, dgz, dpc, dsmall.astype(BF16), pad], axis=1)
    dh = matmul(tag + "proj_dx", dproj, lp["w_in"], "nt", BF16)
    g["w_in"] = matmul(tag + "proj_dw", sv["h"], dproj, "tn", F32)
    (dx,), (g["norm_mix"], g["sc1"], g["sh1"]) = rowstage_bwd(
        tag + "norm1", normmod_fn, [(sv["x"], d, 0)], [lp["norm_mix"], lp["sc1"], lp["sh1"]], [dh], [F32], tm,
        adds={0: dx_a})
    return dx, g, arrived


WEIGHTS = ("w_ada", "b_ada", "norm_mix", "norm_ffn", "w_in", "b_merge", "hgrn_lb_logits", "hgrn_norm", "ssm_conv_w",
           "ssm_conv_b", "ssm_dt_bias", "ssm_a_log", "ssm_d", "ssm_norm", "gdn_conv_w", "gdn_dt_bias", "gdn_a_log",
           "gdn_norm", "w_branch", "w_out", "w_ffn_in", "w_ffn_out", "norm_final")
GATHERED = ("w_in", "w_branch", "w_out", "w_ffn_in", "w_ffn_out")
PACKET = ("b_ada", "norm_mix", "norm_ffn", "b_merge", "hgrn_norm", "ssm_conv_b", "ssm_dt_bias", "ssm_a_log", "ssm_d",
          "ssm_norm", "gdn_dt_bias", "gdn_a_log", "gdn_norm", "norm_final")
MISC = ("hgrn_lb_logits", "ssm_conv_w", "gdn_conv_w")


def _pack(arrs, dtype, row_mult, lead=0):
    flat = jnp.concatenate([t.reshape(t.shape[:lead] + (-1,)).astype(dtype) for t in arrs], axis=lead)
    n = flat.shape[-1]
    unit = row_mult * LANES
    tot = -(-n // unit) * unit
    flat = jnp.pad(flat, [(0, 0)] * lead + [(0, tot - n)])
    return flat.reshape(flat.shape[:lead] + (tot // LANES, LANES))


def _unpack(packed, shapes, lead=0):
    flat = packed.reshape(packed.shape[:lead] + (-1,))
    out, off = [], 0
    for shp in shapes:
        n = int(np.prod(shp))
        out.append(flat[..., off:off + n].reshape(flat.shape[:lead] + tuple(shp)))
        off += n
    return out


def _shard2d(t):
    return t.reshape((-1, t.shape[-1]))


def weights_from_shards(dm, l, got, idx):
    w_in, wb, w_out, wf, wfo = got
    d, mix = dm.d, dm.mix
    return dict(
        w_in=colgather(f"l{l}_w_in", w_in, idx, dm.np, BF16)[0],
        w_branch=wb.reshape(N_DEV, 3, mix, d // N_DEV).transpose(1, 2, 0, 3).reshape(3, mix, d),
        w_out=w_out.reshape(d, d), w_ffn_in=wf, w_ffn_out=wfo.reshape(N_DEV // 2, -1, d))


def shards_of_grads(dm, l, g, idx):
    d, mix = dm.d, dm.mix
    return [colgather(f"l{l}_g_w_in", g["w_in"][None], idx, dm.in_width // N_DEV, F32),
            g["w_branch"].reshape(3, mix, N_DEV, d // N_DEV).transpose(2, 0, 1, 3).reshape(N_DEV, 3 * mix, d // N_DEV),
            g["w_out"].reshape(N_DEV, d // N_DEV, d), g["w_ffn_in"], g["w_ffn_out"].reshape(N_DEV, -1, d)]


def layer_params(dm, l, full, small, mod_l, lb_l):
    d, mix = dm.d, dm.mix
    row = lambda t: t.reshape(1, -1)
    head = lambda t: t.reshape(-1, 1, 1)
    sh1, sc1, g1, sh2, sc2, g2 = (row(mod_l[i * d:(i + 1) * d]) for i in range(6))
    conv_b = jnp.concatenate([jnp.zeros((3 * mix,), F32), small["ssm_conv_b"][l]])
    return dict(
        w_in=full["w_in"], w_branch=full["w_branch"], w_out=full["w_out"],
        w_ffn_in=full["w_ffn_in"], w_ffn_out=full["w_ffn_out"],
        norm_mix=row(small["norm_mix"][l]), norm_ffn=row(small["norm_ffn"][l]), b_merge=row(small["b_merge"][l]),
        hgrn_norm=row(small["hgrn_norm"][l]), lb=row(lb_l),
        conv_w=jnp.concatenate([small["gdn_conv_w"][l], small["ssm_conv_w"][l]], axis=1), conv_b=row(conv_b),
        ssm_dt_bias=head(small["ssm_dt_bias"][l]), ssm_a_log=head(small["ssm_a_log"][l]),
        ssm_d_exp=row(jnp.repeat(small["ssm_d"][l], SSM_P)), ssm_norm=row(small["ssm_norm"][l]),
        gdn_dt_bias=head(small["gdn_dt_bias"][l]), gdn_a_log=head(small["gdn_a_log"][l]), gdn_norm=row(small["gdn_norm"][l]),
        sh1=sh1, sc1=sc1, g1=g1, sh2=sh2, sc2=sc2, g2=g2)


def layer_grads(dm, g):
    cs = 3 * dm.mix
    out = dict(
        w_in=g["w_in"], w_branch=g["w_branch"], w_out=g["w_out"], w_ffn_in=g["w_ffn_in"],
        w_ffn_out=g["w_ffn_out"], norm_mix=g["norm_mix"][0], norm_ffn=g["norm_ffn"][0], b_merge=g["b_merge"][0],
        hgrn_norm=g["hgrn_norm"][0], ssm_conv_w=g["conv_w"][:, cs:], gdn_conv_w=g["conv_w"][:, :cs],
        ssm_conv_b=g["conv_b"][0, cs:], ssm_dt_bias=g["ssm_dt_bias"][:, 0, 0], ssm_a_log=g["ssm_a_log"][:, 0, 0],
        ssm_d=g["ssm_d_exp"].reshape(dm.ssm_heads, SSM_P).sum(axis=1), ssm_norm=g["ssm_norm"][0],
        gdn_dt_bias=g["gdn_dt_bias"][:, 0, 0], gdn_a_log=g["gdn_a_log"][:, 0, 0], gdn_norm=g["gdn_norm"][0])
    dmod = jnp.concatenate([g[k][0] for k in ("sh1", "sc1", "g1", "sh2", "sc2", "g2")])
    return out, dmod, g["lb"][0]


def local_step(dm, nl, x, tgt, norm_final, params_of, gather_of=None, scatter_of=None):
    arrived = exchange("gather_w0", gather_of(0), True) if gather_of else None
    lps, saved = [], []
    for l in range(nl):
        lps.append(params_of(l, arrived))
        side = (gather_of(l + 1), True) if gather_of and l + 1 < nl else None
        x, sv, arrived = layer_fwd(dm, l, x, lps[l], side)
        saved.append(sv)
    loss, dx, dnf = loss_call("loss", x, tgt, norm_final, dm.tm)
    grads, parts, side = [None] * nl, [None] * nl, None
    for l in reversed(range(nl)):
        dx, grads[l], got = layer_bwd(dm, l, dx, lps[l], saved[l], side)
        if side is not None:
            parts[l + 1] = got
        side = (scatter_of(l, grads[l]), False) if scatter_of else None
    if side is not None:
        parts[0] = exchange("scatter_g0", side[0], False)
    return loss, dx, dnf, grads, parts


def kernel(x, c, w_ada, b_ada, norm_mix, norm_ffn, w_in, b_merge, hgrn_lb_logits, hgrn_norm, ssm_conv_w, ssm_conv_b, ssm_dt_bias, ssm_a_log, ssm_d, ssm_norm, gdn_conv_w, gdn_dt_bias, gdn_a_log, gdn_norm, w_branch, w_out, w_ffn_in, w_ffn_out, norm_final, loss_target, m_w_ada, m_b_ada, m_norm_mix, m_norm_ffn, m_w_in, m_b_merge, m_hgrn_lb_logits, m_hgrn_norm, m_ssm_conv_w, m_ssm_conv_b, m_ssm_dt_bias, m_ssm_a_log, m_ssm_d, m_ssm_norm, m_gdn_conv_w, m_gdn_dt_bias, m_gdn_a_log, m_gdn_norm, m_w_branch, m_w_out, m_w_ffn_in, m_w_ffn_out, m_norm_final, v_w_ada, v_b_ada, v_norm_mix, v_norm_ffn, v_w_in, v_b_merge, v_hgrn_lb_logits, v_hgrn_norm, v_ssm_conv_w, v_ssm_conv_b, v_ssm_dt_bias, v_ssm_a_log, v_ssm_d, v_ssm_norm, v_gdn_conv_w, v_gdn_dt_bias, v_gdn_a_log, v_gdn_norm, v_w_branch, v_w_out, v_w_ffn_in, v_w_ffn_out, v_norm_final):
    a = dict(locals())
    x, tgt = a["x"][0], a["loss_target"][0]
    s, d = x.shape
    nl = a["w_ada"].shape[0]
    dm = Dims(s, d, a["w_ffn_out"].shape[1] * N_DEV)
    me = 4 * lax.axis_index("x") + 2 * lax.axis_index("y") + lax.axis_index("c")

    first = [a["c"], a["ssm_conv_w"], a["gdn_conv_w"]]
    c_all, scw, gcw = _unpack(exchange("gather_c", [_pack(first, F32, 8)], True)[0], [t.shape for t in first], lead=1)
    small = dict(a, ssm_conv_w=scw.transpose(1, 2, 0, 3).reshape(scw.shape[1:3] + (-1,)),
                 gdn_conv_w=gcw.transpose(1, 2, 0, 3).reshape(gcw.shape[1:3] + (-1,)))
    c_pad = jnp.zeros((LANES, d), F32).at[:N_DEV].set(c_all.reshape(N_DEV, d))
    ncol = a["w_ada"].shape[2]
    b_mine = lax.dynamic_slice(a["b_ada"], (0, me * ncol), (nl, ncol))[:, None, :]
    mod_part = ada_fwd("ada_fwd", c_pad, a["w_ada"], b_mine)[:, :N_DEV, :]
    (mod,) = exchange("a2a_mod", [mod_part.transpose(1, 0, 2)], False)
    mod = mod.transpose(1, 0, 2).reshape(nl, N_DEV * ncol)
    (lb,) = rowstage_fwd("lower_bounds", lower_bounds_fn, [(a["hgrn_lb_logits"], dm.mix, 0)], [], [(dm.mix, F32)], nl)

    idx_fwd, idx_bwd = w_in_tables(dm, N_DEV)
    loss, dx, dnf, grads, parts = local_step(
        dm, nl, x, tgt, a["norm_final"].reshape(1, d),
        params_of=lambda l, got: layer_params(dm, l, weights_from_shards(dm, l, got, idx_fwd), small, mod[l], lb[l]),
        gather_of=lambda l: [_shard2d(a[n][l]).astype(BF16) for n in GATHERED],
        scatter_of=lambda l, g: shards_of_grads(dm, l, g, idx_bwd))

    per_layer = [layer_grads(dm, g) for g in grads]
    res = {}
    for i, n in enumerate(GATHERED):
        wmv = [a[q + n].reshape((nl, -1, a[n].shape[-1])) for q in ("", "m_", "v_")]
        outs = None
        for l in range(nl):
            outs = adamw_sum(f"adamw_l{l}_{n}", parts[l][i], *wmv, layer=l, into=outs)
        for kind, o in zip(("grad", "delta", "new_m", "new_v"), outs):
            res[(kind, n)] = o.reshape(a[n].shape)

    stackg = lambda n: jnp.stack([pl_[0][n] for pl_ in per_layer])
    dmod = jnp.stack([pl_[1] for pl_ in per_layer])
    dlb = jnp.stack([pl_[2] for pl_ in per_layer])
    pk_g = [dmod if n == "b_ada" else dnf if n == "norm_final" else stackg(n) for n in PACKET]
    extra = [dlb, stackg("ssm_conv_w"), stackg("gdn_conv_w"), loss[0, :1]]
    pk_shapes = [t.shape for t in pk_g + extra]
    zeros = [jnp.zeros(t.shape, F32) for t in extra]
    (parts,) = exchange("gather_small", [_pack(pk_g + extra, F32, 8)], True)
    outs = adamw_sum("adamw_small", parts, *[_pack([a[p + n] for n in PACKET] + zeros, F32, 8) for p in ("", "m_", "v_")])
    for kind, o in zip(("grad", "delta", "new_m", "new_v"), outs):
        un = _unpack(o, pk_shapes)
        for n, t in zip(PACKET, un):
            res[(kind, n)] = t.reshape(a[n].shape)
        if kind == "grad":
            dlb_sum, g_scw, g_gcw, loss_sum = un[len(PACKET):]

    (g_lb,), _ = rowstage_bwd("lower_bounds_b", lower_bounds_fn, [(a["hgrn_lb_logits"], dm.mix, 0)], [], [dlb_sum], [F32], nl)
    mine = lambda t, n: lax.dynamic_slice_in_dim(t, me * a[n].shape[-1], a[n].shape[-1], axis=t.ndim - 1)
    (dmod_cols,) = exchange("a2a_dmod", [dmod.reshape(nl, N_DEV, ncol).transpose(1, 0, 2)], False)
    dmod_pad = jnp.zeros((nl, LANES, ncol), F32).at[:, :N_DEV].set(dmod_cols.transpose(1, 0, 2))
    g_w_ada = ada_bwd("ada_bwd", c_pad, dmod_pad)
    outs = adamw_sum("adamw_w_ada", g_w_ada.reshape(1, nl * d, ncol), *[a[q + "w_ada"].reshape(nl * d, ncol) for q in ("", "m_", "v_")])
    for kind, o in zip(("grad", "delta", "new_m", "new_v"), outs):
        res[(kind, "w_ada")] = o.reshape(nl, d, ncol)
    g_misc = [g_lb, mine(g_scw, "ssm_conv_w"), mine(g_gcw, "gdn_conv_w")]
    outs = adamw_sum("adamw_misc", _pack(g_misc, F32, 8)[None], *[_pack([a[q + n] for n in MISC], F32, 8) for q in ("", "m_", "v_")])
    for kind, o in zip(("grad", "delta", "new_m", "new_v"), outs):
        for n, t in zip(MISC, _unpack(o, [a[n].shape for n in MISC])):
            res[(kind, n)] = t

    out = [loss_sum.reshape(()), dx[None]]
    for kind in ("grad", "delta", "new_m", "new_v"):
        out += [res[(kind, n)] for n in WEIGHTS]
    return tuple(out)
```

```python
import functools
import math

import numpy as np
import jax
import jax.numpy as jnp
from jax import lax
from jax.experimental import pallas as pl
from jax.experimental.pallas import tpu as pltpu

F32 = jnp.float32
BF16 = jnp.bfloat16

N_DEV = 8
CHUNK = 64
SUB = 32
HGRN_HEADS_PER_STEP = 2
GDN_HEADS_PER_STEP = 6
HEAD = 128
SSM_P = 64
CONV_K = 4
F_MIN = 1e-30
NORM_EPS = 1e-6
LANES = 128
GATHER_TILE = 256
VMEM_LIMIT = 56 * 1024 * 1024

ADAM_LR = 0.001
ADAM_B1 = 0.9
ADAM_B2 = 0.999
ADAM_EPS = 1e-08
ADAM_WD = 0.01
ADAM_STEP = 10


def _dg(a, b, ca, cb):
    return lax.dot_general(a.astype(BF16), b.astype(BF16), (((ca,), (cb,)), ((), ())),
                           preferred_element_type=F32)


def _split3(x):
    x1 = x.astype(BF16)
    r = x - x1.astype(F32)
    x2 = r.astype(BF16)
    x3 = (r - x2.astype(F32)).astype(BF16)
    return x1, x2, x3


def _hdg(a, b, ca, cb):
    a1, a2, _ = _split3(a)
    b1, b2, _ = _split3(b)
    dn = (((ca,), (cb,)), ((), ()))
    d = lambda p, q: lax.dot_general(p, q, dn, preferred_element_type=F32)
    return (d(a2, b1) + d(a1, b2)) + d(a1, b1)


def _dot_family(prim):
    @jax.custom_vjp
    def nn(a, b):
        return prim(a, b, 1, 0)

    @jax.custom_vjp
    def nt(a, b):
        return prim(a, b, 1, 1)

    @jax.custom_vjp
    def tn(a, b):
        return prim(a, b, 0, 0)

    nn.defvjp(lambda a, b: (nn(a, b), (a, b)), lambda r, g: (nt(g, r[1]), tn(r[0], g)))
    nt.defvjp(lambda a, b: (nt(a, b), (a, b)), lambda r, g: (nn(g, r[1]), tn(g, r[0])))
    tn.defvjp(lambda a, b: (tn(a, b), (a, b)), lambda r, g: (nt(r[1], g), nn(r[0], g)))
    return nn, nt, tn


mm_nn, mm_nt, mm_tn = _dot_family(_dg)
hd_nn, hd_nt, hd_tn = _dot_family(_hdg)


def _iota(shape, dim):
    return lax.broadcasted_iota(jnp.int32, shape, dim)


def _scan_rows(x, reverse):
    n = x.shape[0]
    rows = _iota(x.shape, 0)
    k = 1
    while k < n:
        if reverse:
            x = x + jnp.where(rows < n - k, pltpu.roll(x, n - k, 0), 0.0)
        else:
            x = x + jnp.where(rows >= k, pltpu.roll(x, k, 0), 0.0)
        k *= 2
    return x


@jax.custom_vjp
def cumsum_rows(x):
    return _scan_rows(x, False)


cumsum_rows.defvjp(lambda x: (_scan_rows(x, False), None), lambda _, g: (_scan_rows(g, True),))


def _sigmoid(x):
    return jax.nn.sigmoid(x)


def _silu(x):
    return x * jax.nn.sigmoid(x)


def _softplus(x):
    e = jnp.exp(-jnp.abs(x))
    small = e * (1.0 - e * (0.5 - e * (1.0 / 3.0)))
    return jnp.maximum(x, 0.0) + jnp.where(e < 1e-3, small, jnp.log(1.0 + e))


def _masked_exp(diff, mask):
    return jnp.where(mask, jnp.exp(jnp.where(mask, diff, 0.0)), 0.0)


def _rms(x, w):
    return x * lax.rsqrt(jnp.mean(x * x, axis=-1, keepdims=True) + NORM_EPS) * w


def _cum_col_row(lg_col, lg_row):
    c = lg_col.shape[0]
    r, s = _iota((c, c), 0), _iota((c, c), 1)
    cum_col = jnp.sum(jnp.where(s <= r, jnp.broadcast_to(lg_row, (c, c)), 0.0), axis=1, keepdims=True)
    cum_row = jnp.sum(jnp.where(r <= s, jnp.broadcast_to(lg_col, (c, c)), 0.0), axis=0, keepdims=True)
    total = jnp.sum(lg_col, axis=0, keepdims=True)
    return cum_col, cum_row, total


def hgrn_chunk(seq, hp, sp, st):
    (blk,), (lb,), (nw,) = seq, hp, sp
    c = blk.shape[0]
    q_raw, f_raw, v, g_raw = (blk[:, i * HEAD:(i + 1) * HEAD] for i in range(4))
    q = _silu(q_raw)
    f = lb + (1.0 - lb) * _sigmoid(f_raw)
    logf = jnp.log(jnp.maximum(f, F_MIN))
    k = (1.0 - lb) * _sigmoid(-f_raw)
    b = cumsum_rows(logf)
    o_inter = mm_nt(q * jnp.exp(b), st)
    nsub = c // SUB
    wide = (SUB, SUB, HEAD)
    er = _iota((SUB * SUB, SUB), 0)
    e_t = (er // SUB == _iota((SUB * SUB, SUB), 1)).astype(F32)
    pr = _iota((SUB * SUB, 1), 0)
    pmask = (pr % SUB) <= (pr // SUB)
    er64 = _iota((SUB * SUB, c), 0)
    ec64 = _iota((SUB * SUB, c), 1)
    rows_c = _iota((c, 1), 0)
    row = lambda a, i: jnp.sum(jnp.where(rows_c == i, a, 0.0), axis=0, keepdims=True)
    parts = []
    for i in range(nsub):
        sl = slice(SUB * i, SUB * (i + 1))
        qi, ki, bi = q[sl], k[sl], b[sl]
        qb = jnp.broadcast_to(qi[:, None, :], wide).reshape(SUB * SUB, HEAD)
        kb = jnp.broadcast_to(ki[None, :, :], wide).reshape(SUB * SUB, HEAD)
        bd = (bi[:, None, :] - bi[None, :, :]).reshape(SUB * SUB, HEAD)
        sc_col = jnp.sum(qb * kb * _masked_exp(bd, pmask), axis=1, keepdims=True)
        place = (ec64 == (er64 % SUB) + SUB * i).astype(F32)
        sc = mm_tn(e_t, sc_col * place)
        if i > 0:
            bref = row(b, SUB * i)
            qt = qi * jnp.exp(bi - bref)
            kt = k * _masked_exp(bref - b, rows_c < SUB * i)
            sc = sc + mm_nt(qt, kt)
        parts.append(mm_nn(sc, v))
    o = o_inter + jnp.concatenate(parts, axis=0)
    bend = row(b, c - 1)
    st_new = st * jnp.exp(bend) + mm_tn(v, k * jnp.exp(bend - b))
    y = _rms(o, nw) * _silu(g_raw)
    return (y,), st_new


def ssd_chunk(seq, hp, sp, st):
    xs, bm, cm, dtc, dtr = seq
    dt_bias, a_log = hp
    c = xs.shape[0]
    lane = _iota((1, 2 * SSM_P), 1)
    first = lane < SSM_P
    r, s = _iota((c, c), 0), _iota((c, c), 1)
    g = mm_nt(cm, bm)
    dts, cums, ends, segs = [], [], [], []
    for i in range(2):
        neg_a = -jnp.exp(a_log[i])
        dt_col = _softplus(dtc[i] + dt_bias[i])
        dt_row = _softplus(dtr[i] + dt_bias[i])
        cum_col, cum_row, total = _cum_col_row(neg_a * dt_col, neg_a * dt_row)
        dts.append(dt_col)
        cums.append(cum_col)
        ends.append(total)
        segs.append(_masked_exp(cum_col - cum_row, s <= r))
    dt_l = jnp.where(first, dts[0], dts[1])
    cum_l = jnp.where(first, cums[0], cums[1])
    end_l = jnp.where(first, ends[0], ends[1])
    xdt = xs * dt_l
    y_intra = (mm_nn(g * segs[0], jnp.where(first, xdt, 0.0))
               + mm_nn(g * segs[1], jnp.where(first, 0.0, xdt)))
    y_inter = mm_nn(cm, st) * jnp.exp(cum_l)
    st_new = st * jnp.exp(end_l) + mm_tn(bm, xdt * jnp.exp(end_l - cum_l))
    return (y_intra + y_inter,), st_new


def _neumann_inverse(a):
    n = a.shape[0]
    eye = (_iota((n, n), 0) == _iota((n, n), 1)).astype(F32)
    p = -a
    t = eye + p
    for _ in range(int(math.log2(n)) - 1):
        p = _hdg(p, p, 1, 0)
        t = t + _hdg(t, p, 1, 0)
    return t


@jax.custom_vjp
def inv_unit_lower(a):
    return _neumann_inverse(a)


def _inv_fwd(a):
    t = _neumann_inverse(a)
    return t, t


inv_unit_lower.defvjp(_inv_fwd, lambda t, g: (-hd_nt(hd_tn(t, g), t),))


def gdn_chunk(seq, hp, sp, st):
    q_raw, k_raw, v, z, gbc, gac, gar = seq
    dt_bias, a_log = hp
    (nw,) = sp
    c = v.shape[0]
    r, s = _iota((c, c), 0), _iota((c, c), 1)
    q = q_raw * lax.rsqrt(jnp.sum(q_raw * q_raw, axis=-1, keepdims=True) + NORM_EPS) * (HEAD ** -0.5)
    k = k_raw * lax.rsqrt(jnp.sum(k_raw * k_raw, axis=-1, keepdims=True) + NORM_EPS)
    beta = _sigmoid(gbc)
    neg_a = -jnp.exp(a_log)
    cum, cum_row, total = _cum_col_row(neg_a * _softplus(gac + dt_bias), neg_a * _softplus(gar + dt_bias))
    decay = _masked_exp(cum - cum_row, s <= r)
    kk = mm_nt(k, k)
    a_low = jnp.where(s < r, beta * kk * decay, 0.0)
    sol = hd_nn(inv_unit_lower(a_low), jnp.concatenate([v * beta, k * (beta * jnp.exp(cum))], axis=1))
    u_base, w_corr = sol[:, :HEAD], sol[:, HEAD:]
    qk = mm_nt(q, k) * decay
    u = u_base - mm_nn(w_corr, st)
    o = mm_nn(q * jnp.exp(cum), st) + mm_nn(qk, u)
    st_new = jnp.exp(total) * st + mm_tn(k * jnp.exp(total - cum), u)
    y = _rms(o, nw) * _silu(z)
    return (y,), st_new


def normmod_fn(rows, params):
    (x,), (nw, sc, sh) = rows, params
    return (_rms(x, nw) * (1.0 + sc) + sh,)


def ssmpost_fn(rows, params):
    (y, xs, z), (d_exp, nw) = rows, params
    y = (y + d_exp * xs) * _silu(z)
    gw = y.shape[1] // 2
    return (jnp.concatenate([_rms(y[:, :gw], nw[:, :gw]), _rms(y[:, gw:], nw[:, gw:])], axis=1),)


def merge_fn(rows, params):
    (yh, ys, yg, gl), (bm, wb) = rows, params
    d = wb.shape[2]
    gates = _sigmoid(gl + bm)
    out = 0.0
    for n, y in enumerate((yh, ys, yg)):
        out = out + gates[:, n * d:(n + 1) * d] * mm_nn(y, wb[n])
    return (out,)


def outproj_fn(rows, params):
    (m, x), (g1, w) = rows, params
    return (x + (1.0 + g1) * mm_nn(m, w),)


def resid_fn(rows, params):
    (x, o), (g2,) = rows, params
    return (x + (1.0 + g2) * o,)


def _params(sem, side_effects=False):
    return pltpu.CompilerParams(dimension_semantics=sem, vmem_limit_bytes=VMEM_LIMIT, has_side_effects=side_effects)


def _whole(a):
    nd = a.ndim
    return pl.BlockSpec(a.shape, lambda *_: (0,) * nd)


def _pick(n, cands):
    for c in cands:
        if n % c == 0:
            return c
    return n


def matmul(name, a, b, mode, out_dtype, side=None):
    if mode == "nn":
        (m, k), n = a.shape, b.shape[1]
    elif mode == "nt":
        (m, k), n = a.shape, b.shape[0]
    else:
        (k, m), n = a.shape, b.shape[1]
    tm = _pick(m, (512, 256, 128))
    tn = _pick(n, (1280, 1024, 1408, 768, 512, 384, 256, 128))
    tk = _pick(k, (1024, 1280, 1408, 768, 512, 256, 128))
    if mode == "tn":
        tm = _pick(m, (1024, 768, 512, 256, 128))
        tk = _pick(k, (512, 256, 128))
    nk = k // tk
    ca, cb = {"nn": (1, 0), "nt": (1, 1), "tn": (0, 0)}[mode]

    def core(a_ref, b_ref, o_ref, acc_ref):
        kk = pl.program_id(2)

        @pl.when(kk == 0)
        def _():
            acc_ref[...] = jnp.zeros_like(acc_ref)

        acc_ref[...] += _dg(a_ref[...], b_ref[...], ca, cb)

        @pl.when(kk == nk - 1)
        def _():
            o_ref[...] = acc_ref[...].astype(o_ref.dtype)

    a_spec = (pl.BlockSpec((tk, tm), lambda i, j, q: (q, i)) if mode == "tn"
              else pl.BlockSpec((tm, tk), lambda i, j, q: (i, q)))
    b_spec = (pl.BlockSpec((tn, tk), lambda i, j, q: (j, q)) if mode == "nt"
              else pl.BlockSpec((tk, tn), lambda i, j, q: (q, j)))
    grid = (m // tm, n // tn, nk)
    body, s_in, s_out, s_shape, s_scr, s_args = _with_side(core, 2, 1, side, grid)
    sem = ("arbitrary",) * 3 if side else ("parallel", "parallel", "arbitrary")
    res = pl.pallas_call(
        body, name=name, grid=grid,
        in_specs=[a_spec, b_spec] + s_in,
        out_specs=[pl.BlockSpec((tm, tn), lambda i, j, q: (i, j))] + s_out,
        out_shape=[jax.ShapeDtypeStruct((m, n), out_dtype)] + s_shape,
        scratch_shapes=[pltpu.VMEM((tm, tn), F32)] + s_scr,
        compiler_params=_params(sem, side is not None),
    )(a, b, *s_args)
    return (res[0], res[1:]) if side else res[0]


def bmatmul(name, a, b, mode, out_dtype, out_batched):
    ab, bb = a.ndim == 3, b.ndim == 3
    nb = a.shape[0] if ab else b.shape[0]
    a2, b2 = a.shape[-2:], b.shape[-2:]
    if mode == "nn":
        (m, k), n = a2, b2[1]
    elif mode == "nt":
        (m, k), n = a2, b2[0]
    else:
        (k, m), n = a2, b2[1]
    tm = _pick(m, (1024, 512, 256, 128) if mode == "tn" else (512, 256, 128))
    tn = _pick(n, (1024, 512, 256, 128))
    tk = _pick(k, (512, 256, 128) if mode == "tn" else (1024, 512, 256, 128))
    nk = k // tk
    ca, cb = {"nn": (1, 0), "nt": (1, 1), "tn": (0, 0)}[mode]
    ids = (lambda g: g) if out_batched else (lambda g: (g[2], g[0], g[1], g[3]))
    grid = (nb, m // tm, n // tn, nk) if out_batched else (m // tm, n // tn, nb, nk)

    def a_map(*g):
        bi, i, j, q = ids(g)
        idx = (q, i) if mode == "tn" else (i, q)
        return (bi,) + idx if ab else idx

    def b_map(*g):
        bi, i, j, q = ids(g)
        idx = (j, q) if mode == "nt" else (q, j)
        return (bi,) + idx if bb else idx

    def o_map(*g):
        bi, i, j, q = ids(g)
        return (bi, i, j) if out_batched else (i, j)

    def body(a_ref, b_ref, o_ref, acc_ref):
        bi, _, _, q = ids(tuple(pl.program_id(d) for d in range(4)))
        first = (q == 0) if out_batched else (q == 0) & (bi == 0)
        last = (q == nk - 1) if out_batched else (q == nk - 1) & (bi == nb - 1)

        @pl.when(first)
        def _():
            acc_ref[...] = jnp.zeros_like(acc_ref)

        acc_ref[...] += _dg(a_ref[...], b_ref[...], ca, cb)

        @pl.when(last)
        def _():
            o_ref[...] = acc_ref[...].astype(o_ref.dtype)

    a_blk = (tk, tm) if mode == "tn" else (tm, tk)
    b_blk = (tn, tk) if mode == "nt" else (tk, tn)
    return pl.pallas_call(
        body, name=name, grid=grid,
        in_specs=[pl.BlockSpec(((None,) if ab else ()) + a_blk, a_map), pl.BlockSpec(((None,) if bb else ()) + b_blk, b_map)],
        out_specs=pl.BlockSpec(((None,) if out_batched else ()) + (tm, tn), o_map),
        out_shape=jax.ShapeDtypeStruct(((nb,) if out_batched else ()) + (m, n), out_dtype),
        scratch_shapes=[pltpu.VMEM((tm, tn), F32)],
        compiler_params=_params(("parallel", "parallel", "arbitrary", "arbitrary")),
    )(a, b)


def colgather(name, src, idx, dst_w, out_dtype):
    nsrc, rows, w = src.shape
    tw = GATHER_TILE
    nbs = -(-w // tw)
    ne = idx.shape[0]
    nbd = idx.shape[1] // tw
    tiles = [sorted(set((idx[e, t * tw:(t + 1) * tw][idx[e, t * tw:(t + 1) * tw] >= 0] // tw).tolist()))
             for e in range(ne) for t in range(nbd)]
    nslot = max(1, max(len(t) for t in tiles))
    tbl = np.full((ne * nbd, nslot), -1, np.int32)
    for i, t in enumerate(tiles):
        tbl[i, :len(t)] = t
    exact3 = src.dtype == F32

    def body(tbl_ref, idx_ref, src_ref, o_ref, acc_ref):
        ti, si = pl.program_id(0), pl.program_id(1)

        @pl.when(si == 0)
        def _():
            acc_ref[...] = jnp.zeros_like(acc_ref)

        t = tbl_ref[ti * nslot + si]

        @pl.when(t >= 0)
        def _():
            onehot = ((_iota((tw, tw), 0) + t * tw) == idx_ref[...]).astype(BF16)
            col = _iota((1, tw), 1) + (t % nbs) * tw
            xv = jnp.where(col < w, src_ref[...], jnp.zeros((), src_ref.dtype))
            d = lambda p: lax.dot_general(p, onehot, (((1,), (0,)), ((), ())), preferred_element_type=F32)
            if exact3:
                x1, x2, x3 = _split3(xv)
                acc_ref[...] += (d(x3) + d(x2)) + d(x1)
            else:
                acc_ref[...] += d(xv)

        @pl.when(si == nslot - 1)
        def _():
            o_ref[...] = acc_ref[...].astype(o_ref.dtype)

    def src_map(ti, si, tbl_ref):
        t = jnp.maximum(tbl_ref[ti * nslot + si], 0)
        return (t // nbs, 0, t % nbs)

    grid_spec = pltpu.PrefetchScalarGridSpec(
        num_scalar_prefetch=1, grid=(ne * nbd, nslot),
        in_specs=[pl.BlockSpec((None, 1, tw), lambda ti, si, tbl_ref: (ti // nbd, 0, ti % nbd)),
                  pl.BlockSpec((None, rows, tw), src_map)],
        out_specs=pl.BlockSpec((None, rows, tw), lambda ti, si, tbl_ref: (ti // nbd, 0, ti % nbd)),
        scratch_shapes=[pltpu.VMEM((rows, tw), F32)])
    return pl.pallas_call(
        body, name=name, grid_spec=grid_spec,
        out_shape=jax.ShapeDtypeStruct((ne, rows, dst_w), out_dtype),
        compiler_params=_params(("parallel", "arbitrary")),
    )(jnp.asarray(tbl.reshape(-1)), jnp.asarray(idx.reshape(ne, 1, nbd * tw).astype(np.int32)), src)


def swiglu3_fwd(name, gu, tm):
    _, nb, s, w = gu.shape

    def body(x_ref, o_ref):
        o_ref[...] = (_silu(x_ref[0]) * x_ref[1]).astype(o_ref.dtype)

    return pl.pallas_call(
        body, name=name, grid=(nb, s // tm),
        in_specs=[pl.BlockSpec((2, None, tm, w), lambda b, i: (0, b, i, 0))],
        out_specs=pl.BlockSpec((None, tm, w), lambda b, i: (b, i, 0)),
        out_shape=jax.ShapeDtypeStruct((nb, s, w), BF16),
        compiler_params=_params(("parallel", "parallel")),
    )(gu)


def swiglu3_bwd(name, gu, dact, tm):
    _, nb, s, w = gu.shape

    def body(x_ref, g_ref, o_ref):
        _, vjp = jax.vjp(lambda a, b: _silu(a) * b, x_ref[0], x_ref[1])
        dg, du = vjp(g_ref[...].astype(F32))
        o_ref[0] = dg.astype(o_ref.dtype)
        o_ref[1] = du.astype(o_ref.dtype)

    return pl.pallas_call(
        body, name=name, grid=(nb, s // tm),
        in_specs=[pl.BlockSpec((2, None, tm, w), lambda b, i: (0, b, i, 0)),
                  pl.BlockSpec((None, tm, w), lambda b, i: (b, i, 0))],
        out_specs=pl.BlockSpec((2, None, tm, w), lambda b, i: (0, b, i, 0)),
        out_shape=jax.ShapeDtypeStruct(gu.shape, BF16),
        compiler_params=_params(("parallel", "parallel")),
    )(gu, dact)


def _row_specs(rows, tm):
    return [pl.BlockSpec((tm, w), lambda i, _c=c: (i, _c)) for (_, w, c) in rows]


def rowstage_fwd(name, fn, rows, params, outs, tm):
    s = rows[0][0].shape[0]
    nr, npar = len(rows), len(params)

    def body(*refs):
        r = [x[...].astype(F32) for x in refs[:nr]]
        p = [x[...].astype(F32) for x in refs[nr:nr + npar]]
        for ref, val in zip(refs[nr + npar:], fn(r, p)):
            ref[...] = val.astype(ref.dtype)

    res = pl.pallas_call(
        body, name=name, grid=(s // tm,),
        in_specs=_row_specs(rows, tm) + [_whole(p) for p in params],
        out_specs=[pl.BlockSpec((tm, w), lambda i: (i, 0)) for (w, _) in outs],
        out_shape=[jax.ShapeDtypeStruct((s, w), dt) for (w, dt) in outs],
        compiler_params=_params(("parallel",)),
    )(*[r[0] for r in rows], *params)
    return res


def rowstage_bwd(name, fn, rows, params, douts, drow_dtypes, tm, adds=None, into=None):
    s = rows[0][0].shape[0]
    nr, npar, no = len(rows), len(params), len(douts)
    adds = adds or {}
    add_idx = sorted(adds)
    na = len(add_idx)
    into = into or {}
    into_idx = sorted(into)
    nb = len(into_idx)

    def body(*refs):
        r = [x[...].astype(F32) for x in refs[:nr]]
        p = [x[...].astype(F32) for x in refs[nr:nr + npar]]
        g = [x[...].astype(F32) for x in refs[nr + npar:nr + npar + no]]
        a_refs = refs[nr + npar + no:nr + npar + no + na]
        dr_refs = refs[nr + npar + no + na + nb:nr + npar + no + na + nb + nr]
        dp_refs = refs[nr + npar + no + na + nb + nr:]
        _, vjp = jax.vjp(lambda r_, p_: tuple(fn(r_, p_)), r, p)
        dr, dp = vjp(tuple(g))
        for j, (ref, val) in enumerate(zip(dr_refs, dr)):
            if j in adds:
                val = val + a_refs[add_idx.index(j)][...].astype(F32)
            ref[...] = val.astype(ref.dtype)

        @pl.when(pl.program_id(0) == 0)
        def _():
            for ref in dp_refs:
                ref[...] = jnp.zeros_like(ref)

        for ref, val in zip(dp_refs, dp):
            ref[...] += val

    res = pl.pallas_call(
        body, name=name, grid=(s // tm,),
        in_specs=(_row_specs(rows, tm) + [_whole(p) for p in params]
                  + [pl.BlockSpec((tm, d.shape[1]), lambda i: (i, 0)) for d in douts]
                  + [pl.BlockSpec((tm, rows[j][1]), lambda i: (i, 0)) for j in add_idx]
                  + [pl.BlockSpec(memory_space=pl.ANY)] * nb),
        out_specs=([pl.BlockSpec((tm, w), lambda i, _c=(into[j][1] if j in into else 0): (i, _c))
                    for j, (_, w, _) in enumerate(rows)] + [_whole(p) for p in params]),
        out_shape=([jax.ShapeDtypeStruct(into[j][0].shape if j in into else (s, w), dt)
                    for j, ((_, w, _), dt) in enumerate(zip(rows, drow_dtypes))]
                   + [jax.ShapeDtypeStruct(p.shape, F32) for p in params]),
        input_output_aliases={nr + npar + no + na + k: j for k, j in enumerate(into_idx)},
        compiler_params=_params(("arbitrary",)),
    )(*[r[0] for r in rows], *params, *douts, *[adds[j] for j in add_idx], *[into[j][0] for j in into_idx])
    return res[:nr], res[nr:]


def _flip(index_map, nc):
    return lambda h, n: index_map(h, nc - 1 - n)


def _with_side(core, n_in, n_out, side, grid):
    if side is None:
        return core, [], [], [], [], ()
    sends, broadcast = side
    k = len(sends)

    def body(*refs):
        ins, snd = refs[:n_in], refs[n_in:n_in + k]
        outs, rcv = refs[n_in + k:n_in + k + n_out], refs[n_in + k + n_out:n_in + 2 * k + n_out]
        scr = refs[n_in + 2 * k + n_out:]
        start, wait = _exchange_ops(snd, rcv, *scr[1:], broadcast)
        ids = [pl.program_id(d) for d in range(len(grid))]
        first = functools.reduce(lambda a, b: a & b, [i == 0 for i in ids])
        last = functools.reduce(lambda a, b: a & b, [i == g - 1 for i, g in zip(ids, grid)])
        pl.when(first)(start)
        core(*ins, *outs, scr[0])
        pl.when(last)(wait)

    return body, [HBM_SPEC] * k, [HBM_SPEC] * k, _exchange_out(sends, broadcast), _exchange_sems(k), tuple(sends)


def _take(v, split, j):
    if split is None:
        return v
    if split[0] == "lane":
        return v[:, j * split[1]:(j + 1) * split[1]]
    if split[0] == "lead":
        return v[j * split[1]:(j + 1) * split[1]]
    return v[j]


def _heads(vals, specs, hb):
    return [v if s[-1] is None else jnp.stack([_take(v, s[-1], j) for j in range(hb)]) for v, s in zip(vals, specs)]


def _over_heads(chunk_fn, hb, seqs, hparams, batched):
    seq_ax = [None if s[3] is None else 0 for s in seqs]
    hp_ax = [None if s[3] is None else 0 for s in hparams]
    if batched:
        return jax.vmap(chunk_fn, in_axes=(seq_ax, hp_ax, None, 0))

    def looped(seq, hp, sp, st):
        pick = lambda vals, axes, j: [v if a is None else v[j] for v, a in zip(vals, axes)]
        res = [chunk_fn(pick(seq, seq_ax, j), pick(hp, hp_ax, j), sp, st[j]) for j in range(hb)]
        pile = lambda parts: jnp.concatenate([p[None] for p in parts], axis=0)
        return tuple(pile(o) for o in zip(*[r[0] for r in res])), pile([r[1] for r in res])

    return looped


def _where(split, j):
    if split[0] == "lane":
        return (slice(None), slice(j * split[1], (j + 1) * split[1]))
    if split[0] == "lead":
        return (slice(j * split[1], (j + 1) * split[1]),)
    return (j,)


def scan_fwd(name, chunk_fn, nblk, hb, nc, seqs, hparams, sparams, state_shape, outs, batched, side=None):
    ns, nhp, nsp, no = len(seqs), len(hparams), len(sparams), len(outs)

    def core(*refs):
        seq_r, hp_r, sp_r = refs[:ns], refs[ns:ns + nhp], refs[ns + nhp:ns + nhp + nsp]
        out_r = refs[ns + nhp + nsp:ns + nhp + nsp + no]
        st_out, st_scr = refs[-2], refs[-1]

        @pl.when(pl.program_id(1) == 0)
        def _():
            st_scr[...] = jnp.zeros_like(st_scr)

        seq_v = [x[...].astype(F32) for x in seq_r]
        hp_v = [x[...] for x in hp_r]
        sp_v = [x[...] for x in sp_r]
        st = st_scr[...]
        st_out[...] = st
        heads = _over_heads(chunk_fn, hb, seqs, hparams, batched)
        o, st_new = heads(_heads(seq_v, seqs, hb), _heads(hp_v, hparams, hb), sp_v, st)
        for ref, spec, val in zip(out_r, outs, o):
            for j in range(hb):
                ref[_where(spec[4], j)] = val[j].astype(ref.dtype)
        st_scr[...] = st_new

    nst = len(state_shape)
    body, s_in, s_out, s_shape, s_scr, s_args = _with_side(core, ns + nhp + nsp, no + 1, side, (nblk, nc))
    res = pl.pallas_call(
        body, name=name, grid=(nblk, nc),
        in_specs=([pl.BlockSpec(bs, im) for (_, bs, im, _) in seqs]
                  + [pl.BlockSpec(bs, lambda h, n, _im=im: _im(h)) for (_, bs, im, _) in hparams]
                  + [_whole(p) for p in sparams] + s_in),
        out_specs=([pl.BlockSpec(bs, im) for (_, _, bs, im, _) in outs]
                   + [pl.BlockSpec((hb, None) + tuple(state_shape), lambda h, n: (h, n) + (0,) * nst)] + s_out),
        out_shape=([jax.ShapeDtypeStruct(fs, dt) for (fs, dt, _, _, _) in outs]
                   + [jax.ShapeDtypeStruct((nblk * hb, nc) + tuple(state_shape), F32)] + s_shape),
        scratch_shapes=[pltpu.VMEM((hb,) + tuple(state_shape), F32)] + s_scr,
        compiler_params=_params(("arbitrary", "arbitrary"), side is not None),
    )(*[x[0] for x in seqs], *[x[0] for x in hparams], *sparams, *s_args)
    return res[:no], res[no], res[no + 1:]


def scan_bwd(name, chunk_fn, nblk, hb, nc, seqs, hparams, sparams, state_shape, states, douts, dseqs, batched, side=None):
    ns, nhp, nsp, no = len(seqs), len(hparams), len(sparams), len(douts)
    nst = len(state_shape)
    buf_of = [i for i, sp in enumerate(dseqs) if len(sp) > 5 and sp[5] is not None]
    bufs = [dseqs[i][5] for i in buf_of]

    def core(*refs):
        seq_r, hp_r, sp_r = refs[:ns], refs[ns:ns + nhp], refs[ns + nhp:ns + nhp + nsp]
        base = ns + nhp + nsp
        st_r = refs[base]
        do_r = refs[base + 1:base + 1 + no]
        base += 1 + no + len(bufs)
        ds_r, dhp_r, dsp_r = refs[base:base + ns], refs[base + ns:base + ns + nhp], refs[base + ns + nhp:base + ns + nhp + nsp]
        dst_scr = refs[-1]
        h, n = pl.program_id(0), pl.program_id(1)

        @pl.when(n == 0)
        def _():
            dst_scr[...] = jnp.zeros_like(dst_scr)
            for ref in dhp_r:
                ref[...] = jnp.zeros_like(ref)

        @pl.when((n == 0) & (h == 0))
        def _():
            for ref in dsp_r:
                ref[...] = jnp.zeros_like(ref)

        seq_v = [x[...].astype(F32) for x in seq_r]
        hp_v = [x[...] for x in hp_r]
        sp_v = [x[...] for x in sp_r]
        do_v = [x[...].astype(F32) for x in do_r]
        prim = (_heads(seq_v, seqs, hb), _heads(hp_v, hparams, hb), sp_v, st_r[...])
        _, vjp = jax.vjp(_over_heads(chunk_fn, hb, seqs, hparams, batched), *prim)
        ds, dhp, dsp, dst = vjp((tuple(_heads(do_v, douts, hb)), dst_scr[...]))
        for ref, spec, val in zip(ds_r, dseqs, ds):
            if spec[4] is None:
                ref[...] = val.astype(ref.dtype)
            else:
                for j in range(hb):
                    ref[_where(spec[4], j)] = val[j].astype(ref.dtype)
        for ref, spec, val in zip(dhp_r, hparams, dhp):
            for j in range(hb):
                ref[_where(spec[3], j)] += val[j]
        for ref, val in zip(dsp_r, dsp):
            ref[...] += val
        dst_scr[...] = dst

    n_in, n_out = ns + nhp + nsp + 1 + no + len(bufs), ns + nhp + nsp
    body, s_in, s_out, s_shape, s_scr, s_args = _with_side(core, n_in, n_out, side, (nblk, nc))
    res = pl.pallas_call(
        body, name=name, grid=(nblk, nc),
        in_specs=([pl.BlockSpec(bs, _flip(im, nc)) for (_, bs, im, _) in seqs]
                  + [pl.BlockSpec(bs, lambda h, n, _im=im: _im(h)) for (_, bs, im, _) in hparams]
                  + [_whole(p) for p in sparams]
                  + [pl.BlockSpec((hb, None) + tuple(state_shape), lambda h, n: (h, nc - 1 - n) + (0,) * nst)]
                  + [pl.BlockSpec(bs, _flip(im, nc)) for (_, bs, im, _) in douts]
                  + [pl.BlockSpec(memory_space=pl.ANY)] * len(bufs) + s_in),
        out_specs=([pl.BlockSpec(sp[2], _flip(sp[3], nc)) for sp in dseqs]
                   + [pl.BlockSpec(bs, lambda h, n, _im=im: _im(h)) for (_, bs, im, _) in hparams]
                   + [_whole(p) for p in sparams] + s_out),
        out_shape=([jax.ShapeDtypeStruct(sp[0], sp[1]) for sp in dseqs]
                   + [jax.ShapeDtypeStruct(x[0].shape, F32) for x in hparams]
                   + [jax.ShapeDtypeStruct(p.shape, F32) for p in sparams] + s_shape),
        scratch_shapes=[pltpu.VMEM((hb,) + tuple(state_shape), F32)] + s_scr,
        input_output_aliases={n_in - len(bufs) + k: i for k, i in enumerate(buf_of)},
        compiler_params=_params(("arbitrary", "arbitrary"), side is not None),
    )(*[x[0] for x in seqs], *[x[0] for x in hparams], *sparams, states, *[x[0] for x in douts], *bufs, *s_args)
    return res[:ns], res[ns:ns + nhp], res[ns + nhp:n_out], res[n_out:]


def _shift_down(x, n, rows):
    if n == 0:
        return x
    return jnp.where(rows >= n, pltpu.roll(x, n, 0), 0.0)


def _shift_up(x, n, rows):
    if n == 0:
        return x
    s = x.shape[0]
    return jnp.where(rows < s - n, pltpu.roll(x, s - n, 0), 0.0)


def conv_fwd(name, x, col0, w, b):
    s, cw = x.shape[0], w.shape[1]

    def body(x_ref, w_ref, b_ref, o_ref):
        xv = x_ref[...]
        rows = _iota(xv.shape, 0)
        u = jnp.broadcast_to(b_ref[...], xv.shape)
        for j in range(CONV_K):
            u = u + w_ref[j:j + 1, :] * _shift_down(xv, CONV_K - 1 - j, rows)
        o_ref[...] = _silu(u)

    return pl.pallas_call(
        body, name=name, grid=(cw // LANES,),
        in_specs=[pl.BlockSpec((s, LANES), lambda j: (0, col0 + j)),
                  pl.BlockSpec((CONV_K, LANES), lambda j: (0, j)),
                  pl.BlockSpec((1, LANES), lambda j: (0, j))],
        out_specs=pl.BlockSpec((s, LANES), lambda j: (0, j)),
        out_shape=jax.ShapeDtypeStruct((s, cw), F32),
        compiler_params=_params(("parallel",)),
    )(x, w, b)


def conv_bwd(name, x, col0, w, b, dout, into):
    s, cw = x.shape[0], w.shape[1]

    def body(x_ref, w_ref, b_ref, g_ref, into_ref, dx_ref, dw_ref, db_ref):
        xv = x_ref[...]
        rows = _iota(xv.shape, 0)
        sh = [_shift_down(xv, CONV_K - 1 - j, rows) for j in range(CONV_K)]
        u = jnp.broadcast_to(b_ref[...], xv.shape)
        for j in range(CONV_K):
            u = u + w_ref[j:j + 1, :] * sh[j]
        sg = _sigmoid(u)
        du = g_ref[...] * (sg * (1.0 + u * (1.0 - sg)))
        dx = jnp.zeros_like(xv)
        for j in range(CONV_K):
            dx = dx + w_ref[j:j + 1, :] * _shift_up(du, CONV_K - 1 - j, rows)
            dw_ref[j:j + 1, :] = jnp.sum(du * sh[j], axis=0, keepdims=True)
        dx_ref[...] = dx.astype(dx_ref.dtype)
        db_ref[...] = jnp.sum(du, axis=0, keepdims=True)

    return pl.pallas_call(
        body, name=name, grid=(cw // LANES,),
        in_specs=[pl.BlockSpec((s, LANES), lambda j: (0, col0 + j)),
                  pl.BlockSpec((CONV_K, LANES), lambda j: (0, j)),
                  pl.BlockSpec((1, LANES), lambda j: (0, j)),
                  pl.BlockSpec((s, LANES), lambda j: (0, j)),
                  pl.BlockSpec(memory_space=pl.ANY)],
        out_specs=[pl.BlockSpec((s, LANES), lambda j: (0, col0 + j)),
                   pl.BlockSpec((CONV_K, LANES), lambda j: (0, j)),
                   pl.BlockSpec((1, LANES), lambda j: (0, j))],
        out_shape=[jax.ShapeDtypeStruct(into.shape, into.dtype), jax.ShapeDtypeStruct((CONV_K, cw), F32),
                   jax.ShapeDtypeStruct((1, cw), F32)],
        input_output_aliases={4: 0},
        compiler_params=_params(("parallel",)),
    )(x, w, b, dout, into)


def exchange(name, sends, broadcast):
    nop = len(sends)

    def body(*refs):
        start, wait = _exchange_ops(refs[:nop], refs[nop:2 * nop], *refs[2 * nop:], broadcast)
        start()
        wait()

    return pl.pallas_call(
        body, name=name,
        in_specs=[HBM_SPEC] * nop, out_specs=[HBM_SPEC] * nop,
        out_shape=_exchange_out(sends, broadcast), scratch_shapes=_exchange_sems(nop),
        compiler_params=pltpu.CompilerParams(has_side_effects=True),
    )(*sends)


HBM_SPEC = pl.BlockSpec(memory_space=pltpu.HBM)


def _exchange_out(sends, broadcast):
    return [jax.ShapeDtypeStruct((N_DEV,) + tuple(t.shape if broadcast else t.shape[1:]), t.dtype) for t in sends]


def _exchange_sems(nop):
    return [pltpu.SemaphoreType.DMA((nop * (N_DEV - 1),)), pltpu.SemaphoreType.DMA((nop * (N_DEV - 1),)),
            pltpu.SemaphoreType.DMA((nop,))]


def _exchange_ops(send_refs, recv_refs, send_sems, recv_sems, local_sems, broadcast):
    nop = len(send_refs)
    x, y, c = lax.axis_index("x"), lax.axis_index("y"), lax.axis_index("c")
    me = 4 * x + 2 * y + c
    peers = []
    for k in range(1, N_DEV):
        px = 1 - x if (k >> 2) & 1 else x
        py = 1 - y if (k >> 1) & 1 else y
        pc = 1 - c if k & 1 else c
        peers.append(((px, py, pc), 4 * px + 2 * py + pc))

    def src(i, peer):
        return send_refs[i] if broadcast else send_refs[i].at[peer]

    def remote(i, k, arrival):
        dev, peer = peers[k]
        return pltpu.make_async_remote_copy(
            src_ref=src(i, peer), dst_ref=recv_refs[i].at[peer if arrival else me],
            send_sem=send_sems.at[i * (N_DEV - 1) + k], recv_sem=recv_sems.at[i * (N_DEV - 1) + k],
            device_id=dev, device_id_type=pl.DeviceIdType.MESH)

    def local(i):
        return pltpu.make_async_copy(src(i, me), recv_refs[i].at[me], local_sems.at[i])

    def start():
        for i in range(nop):
            local(i).start()
        for k in range(N_DEV - 1):
            for i in range(nop):
                remote(i, k, False).start()

    def wait():
        for k in range(N_DEV - 1):
            for i in range(nop):
                remote(i, k, True).wait_recv()
        for k in range(N_DEV - 1):
            for i in range(nop):
                remote(i, k, False).wait_send()
        for i in range(nop):
            local(i).wait()

    return start, wait


def adamw_sum(name, parts, w, m, v, layer=None, into=None):
    rws, cols = w.shape[-2:]
    nsum = parts.shape[0]
    tr = _pick(rws, (256, 128, 64, 32, 16, 8))
    c1 = 1.0 / (1.0 - ADAM_B1 ** ADAM_STEP)
    c2 = 1.0 / (1.0 - ADAM_B2 ** ADAM_STEP)

    def body(p_ref, w_ref, m_ref, v_ref, *rest):
        g_ref, d_ref, nm_ref, nv_ref = rest[-4:]
        g = p_ref[0]
        for j in range(1, nsum):
            g = g + p_ref[j]
        nm = ADAM_B1 * m_ref[...] + (1.0 - ADAM_B1) * g
        nv = ADAM_B2 * v_ref[...] + (1.0 - ADAM_B2) * (g * g)
        g_ref[...] = g
        nm_ref[...] = nm
        nv_ref[...] = nv
        d_ref[...] = -ADAM_LR * ((nm * c1) / (jnp.sqrt(nv * c2) + ADAM_EPS) + ADAM_WD * w_ref[...])

    if layer is None:
        blk = pl.BlockSpec((tr, cols), lambda i: (i, 0))
    else:
        blk = pl.BlockSpec((None, tr, cols), lambda i: (layer, i, 0))
    if into is None and layer is not None:
        into = [lax.empty(w.shape, F32) for _ in range(4)]
    extra = list(into) if into else []
    return pl.pallas_call(
        body, name=name, grid=(rws // tr,),
        in_specs=([pl.BlockSpec((nsum, tr, cols), lambda i: (0, i, 0)), blk, blk, blk]
                  + [pl.BlockSpec(memory_space=pl.ANY)] * len(extra)),
        out_specs=[blk, blk, blk, blk],
        out_shape=[jax.ShapeDtypeStruct(w.shape, F32)] * 4,
        input_output_aliases={4 + k: k for k in range(len(extra))},
        compiler_params=_params(("parallel",)),
    )(parts, w, m, v, *extra)


def ada_fwd(name, c_all, w, b):
    nl = w.shape[0]

    def body(c_ref, w_ref, b_ref, o_ref):
        ca = _silu(c_ref[...])
        for l in range(nl):
            o_ref[l] = mm_nn(ca, w_ref[l]) + b_ref[l]

    return pl.pallas_call(
        body, name=name,
        out_shape=jax.ShapeDtypeStruct((nl, c_all.shape[0], w.shape[2]), F32),
        compiler_params=pltpu.CompilerParams(vmem_limit_bytes=VMEM_LIMIT),
    )(c_all, w, b)


def ada_bwd(name, c_all, dmod):
    nl = dmod.shape[0]

    def body(c_ref, g_ref, o_ref):
        ca = _silu(c_ref[...])
        for l in range(nl):
            o_ref[l] = mm_tn(ca, g_ref[l])

    return pl.pallas_call(
        body, name=name,
        out_shape=jax.ShapeDtypeStruct((nl, c_all.shape[1], dmod.shape[2]), F32),
        compiler_params=pltpu.CompilerParams(vmem_limit_bytes=VMEM_LIMIT),
    )(c_all, dmod)


def lower_bounds_fn(rows, params):
    (lg,), _ = rows, params
    nl = lg.shape[0]
    mx = jnp.max(lg, axis=0, keepdims=True)
    e = jnp.exp(lg - mx)
    p = e / jnp.sum(e, axis=0, keepdims=True)
    layer = _iota((nl, 1), 0)
    acc = jnp.zeros_like(p)
    for j in range(1, nl):
        pj = jnp.sum(jnp.where(layer == j, p, 0.0), axis=0, keepdims=True)
        acc = acc + jnp.where(layer >= j, 1.0, 0.0) * pj
    return (acc,)


def loss_call(name, x, tgt, nw, tm):
    s, d = x.shape

    def body(x_ref, t_ref, w_ref, l_ref, dx_ref, dw_ref):
        def f(xv, wv):
            err = _rms(xv, wv) - t_ref[...]
            return jnp.sum(0.5 * jnp.mean(err * err, axis=-1, keepdims=True), axis=0, keepdims=True)

        val, vjp = jax.vjp(f, x_ref[...], w_ref[...])
        dx, dw = vjp(jnp.ones_like(val))

        @pl.when(pl.program_id(0) == 0)
        def _():
            l_ref[...] = jnp.zeros_like(l_ref)
            dw_ref[...] = jnp.zeros_like(dw_ref)

        l_ref[...] += jnp.broadcast_to(val, l_ref.shape)
        dw_ref[...] += dw
        dx_ref[...] = dx

    row = pl.BlockSpec((tm, d), lambda i: (i, 0))
    return pl.pallas_call(
        body, name=name, grid=(s // tm,),
        in_specs=[row, row, _whole(nw)],
        out_specs=[pl.BlockSpec((8, LANES), lambda i: (0, 0)), row, _whole(nw)],
        out_shape=[jax.ShapeDtypeStruct((8, LANES), F32), jax.ShapeDtypeStruct((s, d), F32),
                   jax.ShapeDtypeStruct(nw.shape, F32)],
        compiler_params=_params(("arbitrary",)),
    )(x, tgt, nw)


class Dims:
    def __init__(self, s, d, ffn):
        self.s, self.d, self.ffn = s, d, ffn
        self.mix = 3 * d // 4
        self.nh = self.mix // HEAD
        self.ssm_heads = self.mix // SSM_P
        self.pairs = self.mix // (2 * SSM_P)
        self.nc = s // CHUNK
        self.conv_ssm = self.mix + 4 * HEAD
        self.conv_w = self.conv_ssm + 3 * self.mix
        self.o_gates = 4 * self.mix
        self.o_sz = self.o_gates + 3 * d
        self.o_gz = self.o_sz + self.mix
        self.o_conv = self.o_gz + self.mix
        self.o_small = self.o_conv + self.conv_w
        used = self.o_small + LANES
        self.np = -(-used // 1280) * 1280
        self.tm = _pick(s, (256, 128, 64))
        mix, nh = self.mix, self.nh
        self.in_sizes = (mix, mix, mix, mix, mix, self.conv_ssm, self.ssm_heads, 3 * mix, mix, nh, nh, 3 * d)
        self.in_width = sum(self.in_sizes)


def w_in_tables(dm, nshard):
    off = np.cumsum((0,) + dm.in_sizes)
    hq, hf, hi, hg, sz, sxbc, sdt, gqkv, gz, gb, ga, gates = (np.arange(off[i], off[i + 1]) for i in range(12))
    hgrn = np.stack([t.reshape(dm.nh, HEAD) for t in (hq, hf, hi, hg)], axis=1).reshape(-1)
    perm = np.concatenate([hgrn, gates, sz, gz, gqkv, sxbc, sdt, gb, ga])
    perm = np.concatenate([perm, np.full(dm.np - perm.size, -1)])
    shard = dm.in_width // nshard
    wpad = -(-shard // GATHER_TILE) * GATHER_TILE
    fwd = np.where(perm >= 0, (perm // shard) * wpad + perm % shard, -1)[None]
    inv = np.zeros(dm.in_width, np.int64)
    inv[perm[perm >= 0]] = np.nonzero(perm >= 0)[0]
    bwd = np.full((nshard, wpad), -1)
    bwd[:, :shard] = inv.reshape(nshard, shard)
    return fwd.astype(np.int32), bwd.astype(np.int32)


def _small_views(dm, small):
    t = small.T
    col = lambda a: a[:, :, None]
    row = lambda a: a.reshape(a.shape[0], dm.nc, 1, CHUNK)
    a, b = dm.ssm_heads, dm.ssm_heads + dm.nh
    sdt, gb, ga = t[:a], t[a:b], t[b:b + dm.nh]
    return col(sdt), row(sdt), col(gb), col(ga), row(ga)


def _scan_specs(dm, proj, conv_out, views, lp, dproj=None):
    dt_col, dt_row, gb_col, ga_col, ga_row = views
    mixb, nh = dm.mix // LANES, dm.nh
    s, mix = dm.s, dm.mix
    lane = ("lane", LANES)
    hb = HGRN_HEADS_PER_STEP
    hw = (CHUNK, hb * LANES)
    hgrn = dict(
        nblk=nh // hb, hb=hb, fn=hgrn_chunk, batched=False,
        seqs=[(proj, (CHUNK, hb * 4 * HEAD), lambda h, n: (n, h), ("lane", 4 * HEAD))],
        hparams=[(lp["lb"], (1, hb * HEAD), lambda h: (0, h), lane)],
        sparams=[lp["hgrn_norm"]],
        dseqs=[((s, dm.np), BF16, (CHUNK, hb * 4 * HEAD), lambda h, n: (n, h), ("lane", 4 * HEAD), dproj)],
        io=(hw, lambda h, n: (n, h), lane))
    ppg = dm.pairs // 2
    qb = 3 * mixb
    gw = (CHUNK, ppg * LANES)
    pcol = ((2 * ppg, CHUNK, 1), lambda g, n: (g, n, 0), ("lead", 2))
    prow = ((2 * ppg, None, 1, CHUNK), lambda g, n: (g, n, 0, 0), ("lead", 2))
    ppar = ((2 * ppg, 1, 1), lambda g: (g, 0, 0), ("lead", 2))
    bc = lambda first: ((CHUNK, LANES), lambda g, n: (n, first + g), None)
    ssd = dict(
        nblk=2, hb=ppg, fn=ssd_chunk, batched=False,
        seqs=[(conv_out, gw, lambda g, n: (n, qb // ppg + g), lane), (conv_out,) + bc(qb + mixb), (conv_out,) + bc(qb + mixb + 2),
              (dt_col,) + pcol, (dt_row,) + prow],
        hparams=[(lp["ssm_dt_bias"],) + ppar, (lp["ssm_a_log"],) + ppar],
        sparams=[],
        dseqs=[((s, mix), F32, gw, lambda g, n: (n, g), lane), ((s, 2 * LANES), F32) + bc(0), ((s, 2 * LANES), F32) + bc(0),
               (dt_col.shape, F32) + pcol, (dt_row.shape, F32) + prow],
        io=(gw, lambda g, n: (n, g), lane))
    hb = GDN_HEADS_PER_STEP
    hw = (CHUNK, hb * LANES)
    cq, cgz = 0, dm.o_gz // LANES
    assert nh % hb == 0 and cgz % hb == 0 and qb % ppg == 0
    hcol = ((hb, CHUNK, 1), lambda h, n: (h, n, 0), ("idx",))
    hrow = ((hb, None, 1, CHUNK), lambda h, n: (h, n, 0, 0), ("idx",))
    hpar = ((hb, 1, 1), lambda h: (h, 0, 0), ("idx",))
    at = lambda first: (hw, lambda h, n: (n, first // hb + h), lane)
    gdn = dict(
        nblk=nh // hb, hb=hb, fn=gdn_chunk, batched=True,
        seqs=[(conv_out,) + at(cq), (conv_out,) + at(cq + nh), (conv_out,) + at(cq + 2 * nh), (proj,) + at(cgz),
              (gb_col,) + hcol, (ga_col,) + hcol, (ga_row,) + hrow],
        hparams=[(lp["gdn_dt_bias"],) + hpar, (lp["gdn_a_log"],) + hpar],
        sparams=[lp["gdn_norm"]],
        dseqs=[((s, mix), F32) + at(0), ((s, mix), F32) + at(0), ((s, mix), F32) + at(0), ((s, dm.np), BF16) + at(cgz) + (dproj,),
               (gb_col.shape, F32) + hcol, (ga_col.shape, F32) + hcol, (ga_row.shape, F32) + hrow],
        io=(hw, lambda h, n: (n, h), lane))
    return hgrn, ssd, gdn


def _run_scan_fwd(dm, name, sp, side=None):
    out = ((dm.s, dm.mix), F32) + sp["io"]
    (y,), states, arrived = scan_fwd(name, sp["fn"], sp["nblk"], sp["hb"], dm.nc, sp["seqs"], sp["hparams"],
                                     sp["sparams"], (HEAD, HEAD), [out], sp["batched"], side)
    return y, states, arrived


def _run_scan_bwd(dm, name, sp, states, dy, side=None):
    return scan_bwd(name, sp["fn"], sp["nblk"], sp["hb"], dm.nc, sp["seqs"], sp["hparams"], sp["sparams"], (HEAD, HEAD),
                    states, [(dy,) + sp["io"]], sp["dseqs"], sp["batched"], side)


def _share_out(side):
    if side is None:
        return None, None, None
    s, broadcast = side
    return ([s[0]], broadcast), ([s[1], s[2], s[4]], broadcast), ([s[3]], broadcast)


def _collect(got_h, got_s, got_g):
    if not got_h:
        return None
    return [got_h[0], got_s[0], got_s[1], got_g[0], got_s[2]]


def layer_fwd(dm, l, x, lp, side=None):
    tm, d, mix = dm.tm, dm.d, dm.mix
    tag = f"l{l}_"
    (h,) = rowstage_fwd(tag + "norm1", normmod_fn, [(x, d, 0)], [lp["norm_mix"], lp["sc1"], lp["sh1"]], [(d, BF16)], tm)
    proj = matmul(tag + "proj", h, lp["w_in"], "nn", F32)
    conv_out = conv_fwd(tag + "conv", proj, dm.o_conv // LANES, lp["conv_w"], lp["conv_b"])
    small = proj[:, dm.o_small:dm.o_small + LANES]
    views = _small_views(dm, small)
    hg, sd, gd = _scan_specs(dm, proj, conv_out, views, lp)
    side_h, side_s, side_g = _share_out(side)
    yh, st_h, got_h = _run_scan_fwd(dm, tag + "hgrn", hg, side_h)
    y_ssd, st_s, got_s = _run_scan_fwd(dm, tag + "ssd", sd, side_s)
    yg, st_g, got_g = _run_scan_fwd(dm, tag + "gdn", gd, side_g)
    arrived = _collect(got_h, got_s, got_g)
    (ys,) = rowstage_fwd(tag + "ssmpost", ssmpost_fn,
                         [(y_ssd, mix, 0), (conv_out, mix, 3), (proj, mix, dm.o_sz // mix)],
                         [lp["ssm_d_exp"], lp["ssm_norm"]], [(mix, F32)], tm)
    (merged,) = rowstage_fwd(tag + "merge", merge_fn, [(yh, mix, 0), (ys, mix, 0), (yg, mix, 0), (proj, 3 * d, 1)],
                             [lp["b_merge"], lp["w_branch"]], [(d, BF16)], tm)
    (x1,) = rowstage_fwd(tag + "outproj", outproj_fn, [(merged, d, 0), (x, d, 0)], [lp["g1"], lp["w_out"]], [(d, F32)], tm)
    (h2,) = rowstage_fwd(tag + "norm2", normmod_fn, [(x1, d, 0)], [lp["norm_ffn"], lp["sc2"], lp["sh2"]], [(d, BF16)], tm)
    gu = bmatmul(tag + "ffn_in", h2, lp["w_ffn_in"], "nn", F32, True)
    gu = gu.reshape((2, gu.shape[0] // 2) + gu.shape[1:])
    act = swiglu3_fwd(tag + "swiglu", gu, tm)
    o2 = bmatmul(tag + "ffn_out", act, lp["w_ffn_out"], "nn", F32, False)
    (x2,) = rowstage_fwd(tag + "resid", resid_fn, [(x1, d, 0), (o2, d, 0)], [lp["g2"]], [(d, F32)], tm)
    saved = dict(x=x, h=h, proj=proj, conv_out=conv_out, views=views, yh=yh, y_ssd=y_ssd, yg=yg, ys=ys,
                 st_h=st_h, st_s=st_s, st_g=st_g, merged=merged, x1=x1, h2=h2, gu=gu, act=act, o2=o2)
    return x2, saved, arrived


def layer_bwd(dm, l, dx2, lp, sv, side=None, own=False):
    tm, d, mix, s = dm.tm, dm.d, dm.mix, dm.s
    tag = f"l{l}_b_"
    g = {}
    (dx1_a, do2), (g["g2"],) = rowstage_bwd(tag + "resid", resid_fn, [(sv["x1"], d, 0), (sv["o2"], d, 0)], [lp["g2"]],
                                            [dx2], [F32, BF16], tm)
    dact = bmatmul(tag + "ffn_out_dx", do2, lp["w_ffn_out"], "nt", BF16, True)
    g["w_ffn_out"] = bmatmul(tag + "ffn_out_dw", sv["act"], do2, "tn", F32, True)
    dgu = swiglu3_bwd(tag + "swiglu", sv["gu"], dact, tm)
    dgu = dgu.reshape((-1,) + dgu.shape[2:])
    dh2 = bmatmul(tag + "ffn_in_dx", dgu, lp["w_ffn_in"], "nt", BF16, False)
    g["w_ffn_in"] = bmatmul(tag + "ffn_in_dw", sv["h2"], dgu, "tn", F32, True)
    (dx1,), (g["norm_ffn"], g["sc2"], g["sh2"]) = rowstage_bwd(
        tag + "norm2", normmod_fn, [(sv["x1"], d, 0)], [lp["norm_ffn"], lp["sc2"], lp["sh2"]], [dh2], [F32], tm,
        adds={0: dx1_a})
    (dmerged, dx_a), (g["g1"], g["w_out"]) = rowstage_bwd(
        tag + "outproj", outproj_fn, [(sv["merged"], d, 0), (sv["x"], d, 0)], [lp["g1"], lp["w_out"]], [dx1],
        [BF16, F32], tm)
    proj, conv_out = sv["proj"], sv["conv_out"]
    dproj = lax.empty((s, dm.np), BF16)
    (dyh, dys, dyg, dproj), (g["b_merge"], g["w_branch"]) = rowstage_bwd(
        tag + "merge", merge_fn, [(sv["yh"], mix, 0), (sv["ys"], mix, 0), (sv["yg"], mix, 0), (proj, 3 * d, 1)],
        [lp["b_merge"], lp["w_branch"]], [dmerged], [F32, F32, F32, BF16], tm, into={3: (dproj, 1)})
    (dy_ssd, dxs_a, dproj), (g["ssm_d_exp"], g["ssm_norm"]) = rowstage_bwd(
        tag + "ssmpost", ssmpost_fn, [(sv["y_ssd"], mix, 0), (conv_out, mix, 3), (proj, mix, dm.o_sz // mix)],
        [lp["ssm_d_exp"], lp["ssm_norm"]], [dys], [F32, F32, BF16], tm, into={2: (dproj, dm.o_sz // mix)})
    side_h, side_s, side_g = _share_out(side)
    hg, sd, _ = _scan_specs(dm, proj, conv_out, sv["views"], lp, dproj)
    (dproj,), (g["lb"],), (g["hgrn_norm"],), got_h = _run_scan_bwd(dm, tag + "hgrn", hg, sv["st_h"], dyh, side_h)
    gd = _scan_specs(dm, proj, conv_out, sv["views"], lp, dproj)[2]
    (dxs_b, dbp, dcp, d_dt_col, d_dt_row), (g["ssm_dt_bias"], g["ssm_a_log"]), _, got_s = _run_scan_bwd(
        dm, tag + "ssd", sd, sv["st_s"], dy_ssd, side_s)
    (dq, dk, dv, dproj, d_gb_col, d_ga_col, d_ga_row), (g["gdn_dt_bias"], g["gdn_a_log"]), (g["gdn_norm"],), got_g = _run_scan_bwd(
        dm, tag + "gdn", gd, sv["st_g"], dyg, side_g)
    arrived = _collect(got_h, got_s, got_g)
    dconv = jnp.concatenate([dq, dk, dv, dxs_a + dxs_b, dbp, dcp], axis=1)
    dproj, g["conv_w"], g["conv_b"] = conv_bwd(tag + "conv", proj, dm.o_conv // LANES, lp["conv_w"], lp["conv_b"], dconv,
                                               dproj)
    unrow = lambda t: t.reshape(t.shape[0], s).T
    dsmall = jnp.concatenate([d_dt_col[:, :, 0].T + unrow(d_dt_row), d_gb_col[:, :, 0].T,
                              d_ga_col[:, :, 0].T + unrow(d_ga_row)], axis=1)
    tail = jnp.pad(dsmall.astype(BF16), ((0, 0), (0, dm.np - dm.o_small - dsmall.shape[1])))
    dproj = lax.dynamic_update_slice(dproj, tail, (0, dm.o_small))
    if own:
        s_wb, s_wout, s_wf, s_wfo = small_shards(dm, g)
        dh, (got_wf,) = matmul(tag + "proj_dx", dproj, lp["w_in"], "nt", BF16, ([s_wf], False))
        g["w_in"], (got_wb, got_wout, got_wfo) = matmul(tag + "proj_dw", sv["h"], dproj, "tn", F32,
                                                        ([s_wb, s_wout, s_wfo], False))
    else:
        dh = matmul(tag + "proj_dx", dproj, lp["w_in"], "nt", BF16)
        g["w_in"] = matmul(tag + "proj_dw", sv["h"], dproj, "tn", F32)
    (dx,), (g["norm_mix"], g["sc1"], g["sh1"]) = rowstage_bwd(
        tag + "norm1", normmod_fn, [(sv["x"], d, 0)], [lp["norm_mix"], lp["sc1"], lp["sh1"]], [dh], [F32], tm,
        adds={0: dx_a})
    if own:
        return dx, g, arrived, [got_wb, got_wout, got_wf, got_wfo]
    return dx, g, arrived


WEIGHTS = ("w_ada", "b_ada", "norm_mix", "norm_ffn", "w_in", "b_merge", "hgrn_lb_logits", "hgrn_norm", "ssm_conv_w",
           "ssm_conv_b", "ssm_dt_bias", "ssm_a_log", "ssm_d", "ssm_norm", "gdn_conv_w", "gdn_dt_bias", "gdn_a_log",
           "gdn_norm", "w_branch", "w_out", "w_ffn_in", "w_ffn_out", "norm_final")
GATHERED = ("w_in", "w_branch", "w_out", "w_ffn_in", "w_ffn_out")
PACKET = ("b_ada", "norm_mix", "norm_ffn", "b_merge", "hgrn_norm", "ssm_conv_b", "ssm_dt_bias", "ssm_a_log", "ssm_d",
          "ssm_norm", "gdn_dt_bias", "gdn_a_log", "gdn_norm", "norm_final")
MISC = ("hgrn_lb_logits", "ssm_conv_w", "gdn_conv_w")


def _pack(arrs, dtype, row_mult, lead=0):
    flat = jnp.concatenate([t.reshape(t.shape[:lead] + (-1,)).astype(dtype) for t in arrs], axis=lead)
    n = flat.shape[-1]
    unit = row_mult * LANES
    tot = -(-n // unit) * unit
    flat = jnp.pad(flat, [(0, 0)] * lead + [(0, tot - n)])
    return flat.reshape(flat.shape[:lead] + (tot // LANES, LANES))


def _unpack(packed, shapes, lead=0):
    flat = packed.reshape(packed.shape[:lead] + (-1,))
    out, off = [], 0
    for shp in shapes:
        n = int(np.prod(shp))
        out.append(flat[..., off:off + n].reshape(flat.shape[:lead] + tuple(shp)))
        off += n
    return out


def _shard2d(t):
    return t.reshape((-1, t.shape[-1]))


def weights_from_shards(dm, l, got, idx):
    w_in, wb, w_out, wf, wfo = got
    d, mix = dm.d, dm.mix
    return dict(
        w_in=colgather(f"l{l}_w_in", w_in, idx, dm.np, BF16)[0],
        w_branch=wb.reshape(N_DEV, 3, mix, d // N_DEV).transpose(1, 2, 0, 3).reshape(3, mix, d),
        w_out=w_out.reshape(d, d), w_ffn_in=wf, w_ffn_out=wfo.reshape(N_DEV // 2, -1, d))


def small_shards(dm, g):
    d, mix = dm.d, dm.mix
    return [g["w_branch"].reshape(3, mix, N_DEV, d // N_DEV).transpose(2, 0, 1, 3).reshape(N_DEV, 3 * mix, d // N_DEV),
            g["w_out"].reshape(N_DEV, d // N_DEV, d), g["w_ffn_in"], g["w_ffn_out"].reshape(N_DEV, -1, d)]


def w_in_shards(dm, l, g, idx):
    return colgather(f"l{l}_g_w_in", g["w_in"][None], idx, dm.in_width // N_DEV, F32)


def layer_params(dm, l, full, small, mod_l, lb_l):
    d, mix = dm.d, dm.mix
    row = lambda t: t.reshape(1, -1)
    head = lambda t: t.reshape(-1, 1, 1)
    sh1, sc1, g1, sh2, sc2, g2 = (row(mod_l[i * d:(i + 1) * d]) for i in range(6))
    conv_b = jnp.concatenate([jnp.zeros((3 * mix,), F32), small["ssm_conv_b"][l]])
    return dict(
        w_in=full["w_in"], w_branch=full["w_branch"], w_out=full["w_out"],
        w_ffn_in=full["w_ffn_in"], w_ffn_out=full["w_ffn_out"],
        norm_mix=row(small["norm_mix"][l]), norm_ffn=row(small["norm_ffn"][l]), b_merge=row(small["b_merge"][l]),
        hgrn_norm=row(small["hgrn_norm"][l]), lb=row(lb_l),
        conv_w=jnp.concatenate([small["gdn_conv_w"][l], small["ssm_conv_w"][l]], axis=1), conv_b=row(conv_b),
        ssm_dt_bias=head(small["ssm_dt_bias"][l]), ssm_a_log=head(small["ssm_a_log"][l]),
        ssm_d_exp=row(jnp.repeat(small["ssm_d"][l], SSM_P)), ssm_norm=row(small["ssm_norm"][l]),
        gdn_dt_bias=head(small["gdn_dt_bias"][l]), gdn_a_log=head(small["gdn_a_log"][l]), gdn_norm=row(small["gdn_norm"][l]),
        sh1=sh1, sc1=sc1, g1=g1, sh2=sh2, sc2=sc2, g2=g2)


def layer_grads(dm, g):
    cs = 3 * dm.mix
    out = dict(
        w_in=g["w_in"], w_branch=g["w_branch"], w_out=g["w_out"], w_ffn_in=g["w_ffn_in"],
        w_ffn_out=g["w_ffn_out"], norm_mix=g["norm_mix"][0], norm_ffn=g["norm_ffn"][0], b_merge=g["b_merge"][0],
        hgrn_norm=g["hgrn_norm"][0], ssm_conv_w=g["conv_w"][:, cs:], gdn_conv_w=g["conv_w"][:, :cs],
        ssm_conv_b=g["conv_b"][0, cs:], ssm_dt_bias=g["ssm_dt_bias"][:, 0, 0], ssm_a_log=g["ssm_a_log"][:, 0, 0],
        ssm_d=g["ssm_d_exp"].reshape(dm.ssm_heads, SSM_P).sum(axis=1), ssm_norm=g["ssm_norm"][0],
        gdn_dt_bias=g["gdn_dt_bias"][:, 0, 0], gdn_a_log=g["gdn_a_log"][:, 0, 0], gdn_norm=g["gdn_norm"][0])
    dmod = jnp.concatenate([g[k][0] for k in ("sh1", "sc1", "g1", "sh2", "sc2", "g2")])
    return out, dmod, g["lb"][0]


def local_step(dm, nl, x, tgt, norm_final, params_of, gather_of=None, scatter_of=None):
    arrived = exchange("gather_w0", gather_of(0), True) if gather_of else None
    lps, saved = [], []
    for l in range(nl):
        lps.append(params_of(l, arrived))
        side = (gather_of(l + 1), True) if gather_of and l + 1 < nl else None
        x, sv, arrived = layer_fwd(dm, l, x, lps[l], side)
        saved.append(sv)
    loss, dx, dnf = loss_call("loss", x, tgt, norm_final, dm.tm)
    grads, parts, side = [None] * nl, [None] * nl, None
    for l in reversed(range(nl)):
        if scatter_of and l == 0:
            dx, grads[l], got, own = layer_bwd(dm, l, dx, lps[l], saved[l], side, own=True)
            parts[0] = list(exchange("scatter_g0", [scatter_of(0, grads[0])], False)) + own
        else:
            dx, grads[l], got = layer_bwd(dm, l, dx, lps[l], saved[l], side)
        if side is not None:
            parts[l + 1] = got
        side = ([scatter_of(l, grads[l])] + small_shards(dm, grads[l]), False) if scatter_of and l > 0 else None
    return loss, dx, dnf, grads, parts


def kernel(x, c, w_ada, b_ada, norm_mix, norm_ffn, w_in, b_merge, hgrn_lb_logits, hgrn_norm, ssm_conv_w, ssm_conv_b, ssm_dt_bias, ssm_a_log, ssm_d, ssm_norm, gdn_conv_w, gdn_dt_bias, gdn_a_log, gdn_norm, w_branch, w_out, w_ffn_in, w_ffn_out, norm_final, loss_target, m_w_ada, m_b_ada, m_norm_mix, m_norm_ffn, m_w_in, m_b_merge, m_hgrn_lb_logits, m_hgrn_norm, m_ssm_conv_w, m_ssm_conv_b, m_ssm_dt_bias, m_ssm_a_log, m_ssm_d, m_ssm_norm, m_gdn_conv_w, m_gdn_dt_bias, m_gdn_a_log, m_gdn_norm, m_w_branch, m_w_out, m_w_ffn_in, m_w_ffn_out, m_norm_final, v_w_ada, v_b_ada, v_norm_mix, v_norm_ffn, v_w_in, v_b_merge, v_hgrn_lb_logits, v_hgrn_norm, v_ssm_conv_w, v_ssm_conv_b, v_ssm_dt_bias, v_ssm_a_log, v_ssm_d, v_ssm_norm, v_gdn_conv_w, v_gdn_dt_bias, v_gdn_a_log, v_gdn_norm, v_w_branch, v_w_out, v_w_ffn_in, v_w_ffn_out, v_norm_final):
    a = dict(locals())
    x, tgt = a["x"][0], a["loss_target"][0]
    s, d = x.shape
    nl = a["w_ada"].shape[0]
    dm = Dims(s, d, a["w_ffn_out"].shape[1] * N_DEV)
    me = 4 * lax.axis_index("x") + 2 * lax.axis_index("y") + lax.axis_index("c")

    first = [a["c"], a["ssm_conv_w"], a["gdn_conv_w"]]
    c_all, scw, gcw = _unpack(exchange("gather_c", [_pack(first, F32, 8)], True)[0], [t.shape for t in first], lead=1)
    small = dict(a, ssm_conv_w=scw.transpose(1, 2, 0, 3).reshape(scw.shape[1:3] + (-1,)),
                 gdn_conv_w=gcw.transpose(1, 2, 0, 3).reshape(gcw.shape[1:3] + (-1,)))
    c_pad = jnp.zeros((LANES, d), F32).at[:N_DEV].set(c_all.reshape(N_DEV, d))
    ncol = a["w_ada"].shape[2]
    b_mine = lax.dynamic_slice(a["b_ada"], (0, me * ncol), (nl, ncol))[:, None, :]
    mod_part = ada_fwd("ada_fwd", c_pad, a["w_ada"], b_mine)[:, :N_DEV, :]
    (mod,) = exchange("a2a_mod", [mod_part.transpose(1, 0, 2)], False)
    mod = mod.transpose(1, 0, 2).reshape(nl, N_DEV * ncol)
    (lb,) = rowstage_fwd("lower_bounds", lower_bounds_fn, [(a["hgrn_lb_logits"], dm.mix, 0)], [], [(dm.mix, F32)], nl)

    idx_fwd, idx_bwd = w_in_tables(dm, N_DEV)
    loss, dx, dnf, grads, parts = local_step(
        dm, nl, x, tgt, a["norm_final"].reshape(1, d),
        params_of=lambda l, got: layer_params(dm, l, weights_from_shards(dm, l, got, idx_fwd), small, mod[l], lb[l]),
        gather_of=lambda l: [_shard2d(a[n][l]).astype(BF16) for n in GATHERED],
        scatter_of=lambda l, g: w_in_shards(dm, l, g, idx_bwd))

    per_layer = [layer_grads(dm, g) for g in grads]
    res = {}
    for i, n in enumerate(GATHERED):
        wmv = [a[q + n].reshape((nl, -1, a[n].shape[-1])) for q in ("", "m_", "v_")]
        outs = None
        for l in range(nl):
            outs = adamw_sum(f"adamw_l{l}_{n}", parts[l][i], *wmv, layer=l, into=outs)
        for kind, o in zip(("grad", "delta", "new_m", "new_v"), outs):
            res[(kind, n)] = o.reshape(a[n].shape)

    stackg = lambda n: jnp.stack([pl_[0][n] for pl_ in per_layer])
    dmod = jnp.stack([pl_[1] for pl_ in per_layer])
    dlb = jnp.stack([pl_[2] for pl_ in per_layer])
    pk_g = [dmod if n == "b_ada" else dnf if n == "norm_final" else stackg(n) for n in PACKET]
    extra = [dlb, stackg("ssm_conv_w"), stackg("gdn_conv_w"), loss[0, :1]]
    pk_shapes = [t.shape for t in pk_g + extra]
    zeros = [jnp.zeros(t.shape, F32) for t in extra]
    (parts,) = exchange("gather_small", [_pack(pk_g + extra, F32, 8)], True)
    outs = adamw_sum("adamw_small", parts, *[_pack([a[p + n] for n in PACKET] + zeros, F32, 8) for p in ("", "m_", "v_")])
    for kind, o in zip(("grad", "delta", "new_m", "new_v"), outs):
        un = _unpack(o, pk_shapes)
        for n, t in zip(PACKET, un):
            res[(kind, n)] = t.reshape(a[n].shape)
        if kind == "grad":
            dlb_sum, g_scw, g_gcw, loss_sum = un[len(PACKET):]

    (g_lb,), _ = rowstage_bwd("lower_bounds_b", lower_bounds_fn, [(a["hgrn_lb_logits"], dm.mix, 0)], [], [dlb_sum], [F32], nl)
    mine = lambda t, n: lax.dynamic_slice_in_dim(t, me * a[n].shape[-1], a[n].shape[-1], axis=t.ndim - 1)
    (dmod_cols,) = exchange("a2a_dmod", [dmod.reshape(nl, N_DEV, ncol).transpose(1, 0, 2)], False)
    dmod_pad = jnp.zeros((nl, LANES, ncol), F32).at[:, :N_DEV].set(dmod_cols.transpose(1, 0, 2))
    g_w_ada = ada_bwd("ada_bwd", c_pad, dmod_pad)
    outs = adamw_sum("adamw_w_ada", g_w_ada.reshape(1, nl * d, ncol), *[a[q + "w_ada"].reshape(nl * d, ncol) for q in ("", "m_", "v_")])
    for kind, o in zip(("grad", "delta", "new_m", "new_v"), outs):
        res[(kind, "w_ada")] = o.reshape(nl, d, ncol)
    g_misc = [g_lb, mine(g_scw, "ssm_conv_w"), mine(g_gcw, "gdn_conv_w")]
    outs = adamw_sum("adamw_misc", _pack(g_misc, F32, 8)[None], *[_pack([a[q + n] for n in MISC], F32, 8) for q in ("", "m_", "v_")])
    for kind, o in zip(("grad", "delta", "new_m", "new_v"), outs):
        for n, t in zip(MISC, _unpack(o, [a[n].shape for n in MISC])):
            res[(kind, n)] = t

    out = [loss_sum.reshape(()), dx[None]]
    for kind in ("grad", "delta", "new_m", "new_v"):
        out += [res[(kind, n)] for n in WEIGHTS]
    return tuple(out)
```

```python
import functools
import math

import numpy as np
import jax
import jax.numpy as jnp
from jax import lax
from jax.experimental import pallas as pl
from jax.experimental.pallas import tpu as pltpu

F32 = jnp.float32
BF16 = jnp.bfloat16

N_DEV = 8
CHUNK = 64
SUB = 8
HGRN_HEADS_PER_STEP = 2
GDN_HEADS_PER_STEP = 6
HEAD = 128
SSM_P = 64
CONV_K = 4
F_MIN = 1e-30
NORM_EPS = 1e-6
LANES = 128
GATHER_TILE = 256
VMEM_LIMIT = 56 * 1024 * 1024

ADAM_LR = 0.001
ADAM_B1 = 0.9
ADAM_B2 = 0.999
ADAM_EPS = 1e-08
ADAM_WD = 0.01
ADAM_STEP = 10


def _dg(a, b, ca, cb):
    return lax.dot_general(a.astype(BF16), b.astype(BF16), (((ca,), (cb,)), ((), ())),
                           preferred_element_type=F32)


def _split3(x):
    x1 = x.astype(BF16)
    r = x - x1.astype(F32)
    x2 = r.astype(BF16)
    x3 = (r - x2.astype(F32)).astype(BF16)
    return x1, x2, x3


def _hdg(a, b, ca, cb):
    a1, a2, _ = _split3(a)
    b1, b2, _ = _split3(b)
    dn = (((ca,), (cb,)), ((), ()))
    d = lambda p, q: lax.dot_general(p, q, dn, preferred_element_type=F32)
    return (d(a2, b1) + d(a1, b2)) + d(a1, b1)


def _dot_family(prim):
    @jax.custom_vjp
    def nn(a, b):
        return prim(a, b, 1, 0)

    @jax.custom_vjp
    def nt(a, b):
        return prim(a, b, 1, 1)

    @jax.custom_vjp
    def tn(a, b):
        return prim(a, b, 0, 0)

    nn.defvjp(lambda a, b: (nn(a, b), (a, b)), lambda r, g: (nt(g, r[1]), tn(r[0], g)))
    nt.defvjp(lambda a, b: (nt(a, b), (a, b)), lambda r, g: (nn(g, r[1]), tn(g, r[0])))
    tn.defvjp(lambda a, b: (tn(a, b), (a, b)), lambda r, g: (nt(r[1], g), nn(r[0], g)))
    return nn, nt, tn


mm_nn, mm_nt, mm_tn = _dot_family(_dg)
hd_nn, hd_nt, hd_tn = _dot_family(_hdg)


def _iota(shape, dim):
    return lax.broadcasted_iota(jnp.int32, shape, dim)


def _scan_rows(x, reverse):
    n = x.shape[0]
    rows = _iota(x.shape, 0)
    k = 1
    while k < n:
        if reverse:
            x = x + jnp.where(rows < n - k, pltpu.roll(x, n - k, 0), 0.0)
        else:
            x = x + jnp.where(rows >= k, pltpu.roll(x, k, 0), 0.0)
        k *= 2
    return x


@jax.custom_vjp
def cumsum_rows(x):
    return _scan_rows(x, False)


cumsum_rows.defvjp(lambda x: (_scan_rows(x, False), None), lambda _, g: (_scan_rows(g, True),))


def _sigmoid(x):
    return jax.nn.sigmoid(x)


def _silu(x):
    return x * jax.nn.sigmoid(x)


def _softplus(x):
    e = jnp.exp(-jnp.abs(x))
    small = e * (1.0 - e * (0.5 - e * (1.0 / 3.0)))
    return jnp.maximum(x, 0.0) + jnp.where(e < 1e-3, small, jnp.log(1.0 + e))


def _masked_exp(diff, mask):
    return jnp.where(mask, jnp.exp(jnp.where(mask, diff, 0.0)), 0.0)


def _rms(x, w):
    return x * lax.rsqrt(jnp.mean(x * x, axis=-1, keepdims=True) + NORM_EPS) * w


def _cum_col_row(lg_col, lg_row):
    c = lg_col.shape[0]
    r, s = _iota((c, c), 0), _iota((c, c), 1)
    cum_col = jnp.sum(jnp.where(s <= r, jnp.broadcast_to(lg_row, (c, c)), 0.0), axis=1, keepdims=True)
    cum_row = jnp.sum(jnp.where(r <= s, jnp.broadcast_to(lg_col, (c, c)), 0.0), axis=0, keepdims=True)
    total = jnp.sum(lg_col, axis=0, keepdims=True)
    return cum_col, cum_row, total


def hgrn_chunk(seq, hp, sp, st):
    (blk,), (lb,), (nw,) = seq, hp, sp
    c = blk.shape[0]
    q_raw, f_raw, v, g_raw = (blk[:, i * HEAD:(i + 1) * HEAD] for i in range(4))
    q = _silu(q_raw)
    f = lb + (1.0 - lb) * _sigmoid(f_raw)
    logf = jnp.log(jnp.maximum(f, F_MIN))
    k = (1.0 - lb) * _sigmoid(-f_raw)
    b = cumsum_rows(logf)
    o_inter = mm_nt(q * jnp.exp(b), st)
    nsub = c // SUB
    wide = (SUB, SUB, HEAD)
    er = _iota((SUB * SUB, SUB), 0)
    e_t = (er // SUB == _iota((SUB * SUB, SUB), 1)).astype(F32)
    pr = _iota((SUB * SUB, 1), 0)
    pmask = (pr % SUB) <= (pr // SUB)
    er64 = _iota((SUB * SUB, c), 0)
    ec64 = _iota((SUB * SUB, c), 1)
    rows_c = _iota((c, 1), 0)
    row = lambda a, i: jnp.sum(jnp.where(rows_c == i, a, 0.0), axis=0, keepdims=True)
    def sub_chunk(qi, ki, bi, bref, first, place):
        qb = jnp.broadcast_to(qi[:, None, :], wide).reshape(SUB * SUB, HEAD)
        kb = jnp.broadcast_to(ki[None, :, :], wide).reshape(SUB * SUB, HEAD)
        bd = (bi[:, None, :] - bi[None, :, :]).reshape(SUB * SUB, HEAD)
        sc_col = jnp.sum(qb * kb * _masked_exp(bd, pmask), axis=1, keepdims=True)
        sc = mm_tn(e_t, sc_col * place)
        sc = sc + mm_nt(qi * jnp.exp(bi - bref), k * _masked_exp(bref - b, rows_c < first))
        return mm_nn(sc, v)

    firsts = [SUB * i for i in range(nsub)]
    pile = lambda parts: jnp.concatenate([p[None] for p in parts], axis=0)
    cut = lambda a: a.reshape(nsub, SUB, HEAD)
    brefs = pile([row(b, f) for f in firsts])
    starts = pile([jnp.full((1, 1), f, jnp.int32) for f in firsts])
    places = pile([(ec64 == (er64 % SUB) + f).astype(F32) for f in firsts])
    o_intra = jax.vmap(sub_chunk)(cut(q), cut(k), cut(b), brefs, starts, places)
    o = o_inter + o_intra.reshape(c, HEAD)
    bend = row(b, c - 1)
    st_new = st * jnp.exp(bend) + mm_tn(v, k * jnp.exp(bend - b))
    y = _rms(o, nw) * _silu(g_raw)
    return (y,), st_new


def ssd_chunk(seq, hp, sp, st):
    xs, bm, cm, dtc, dtr = seq
    dt_bias, a_log = hp
    c = xs.shape[0]
    lane = _iota((1, 2 * SSM_P), 1)
    first = lane < SSM_P
    r, s = _iota((c, c), 0), _iota((c, c), 1)
    g = mm_nt(cm, bm)
    dts, cums, ends, segs = [], [], [], []
    for i in range(2):
        neg_a = -jnp.exp(a_log[i])
        dt_col = _softplus(dtc[i] + dt_bias[i])
        dt_row = _softplus(dtr[i] + dt_bias[i])
        cum_col, cum_row, total = _cum_col_row(neg_a * dt_col, neg_a * dt_row)
        dts.append(dt_col)
        cums.append(cum_col)
        ends.append(total)
        segs.append(_masked_exp(cum_col - cum_row, s <= r))
    dt_l = jnp.where(first, dts[0], dts[1])
    cum_l = jnp.where(first, cums[0], cums[1])
    end_l = jnp.where(first, ends[0], ends[1])
    xdt = xs * dt_l
    y_intra = (mm_nn(g * segs[0], jnp.where(first, xdt, 0.0))
               + mm_nn(g * segs[1], jnp.where(first, 0.0, xdt)))
    y_inter = mm_nn(cm, st) * jnp.exp(cum_l)
    st_new = st * jnp.exp(end_l) + mm_tn(bm, xdt * jnp.exp(end_l - cum_l))
    return (y_intra + y_inter,), st_new


def _neumann_inverse(a):
    n = a.shape[0]
    eye = (_iota((n, n), 0) == _iota((n, n), 1)).astype(F32)
    p = -a
    t = eye + p
    for _ in range(int(math.log2(n)) - 1):
        p = _hdg(p, p, 1, 0)
        t = t + _hdg(t, p, 1, 0)
    return t


@jax.custom_vjp
def inv_unit_lower(a):
    return _neumann_inverse(a)


def _inv_fwd(a):
    t = _neumann_inverse(a)
    return t, t


inv_unit_lower.defvjp(_inv_fwd, lambda t, g: (-hd_nt(hd_tn(t, g), t),))


def gdn_chunk(seq, hp, sp, st):
    q_raw, k_raw, v, z, gbc, gac, gar = seq
    dt_bias, a_log = hp
    (nw,) = sp
    c = v.shape[0]
    r, s = _iota((c, c), 0), _iota((c, c), 1)
    q = q_raw * lax.rsqrt(jnp.sum(q_raw * q_raw, axis=-1, keepdims=True) + NORM_EPS) * (HEAD ** -0.5)
    k = k_raw * lax.rsqrt(jnp.sum(k_raw * k_raw, axis=-1, keepdims=True) + NORM_EPS)
    beta = _sigmoid(gbc)
    neg_a = -jnp.exp(a_log)
    cum, cum_row, total = _cum_col_row(neg_a * _softplus(gac + dt_bias), neg_a * _softplus(gar + dt_bias))
    decay = _masked_exp(cum - cum_row, s <= r)
    kk = mm_nt(k, k)
    a_low = jnp.where(s < r, beta * kk * decay, 0.0)
    sol = hd_nn(inv_unit_lower(a_low), jnp.concatenate([v * beta, k * (beta * jnp.exp(cum))], axis=1))
    u_base, w_corr = sol[:, :HEAD], sol[:, HEAD:]
    qk = mm_nt(q, k) * decay
    u = u_base - mm_nn(w_corr, st)
    o = mm_nn(q * jnp.exp(cum), st) + mm_nn(qk, u)
    st_new = jnp.exp(total) * st + mm_tn(k * jnp.exp(total - cum), u)
    y = _rms(o, nw) * _silu(z)
    return (y,), st_new


def normmod_fn(rows, params):
    (x,), (nw, sc, sh) = rows, params
    return (_rms(x, nw) * (1.0 + sc) + sh,)


def ssmpost_fn(rows, params):
    (y, xs, z), (d_exp, nw) = rows, params
    y = (y + d_exp * xs) * _silu(z)
    gw = y.shape[1] // 2
    return (jnp.concatenate([_rms(y[:, :gw], nw[:, :gw]), _rms(y[:, gw:], nw[:, gw:])], axis=1),)


def merge_fn(rows, params):
    (yh, ys, yg, gl), (bm, wb) = rows, params
    d = wb.shape[2]
    gates = _sigmoid(gl + bm)
    out = 0.0
    for n, y in enumerate((yh, ys, yg)):
        out = out + gates[:, n * d:(n + 1) * d] * mm_nn(y, wb[n])
    return (out,)


def outproj_fn(rows, params):
    (m, x), (g1, w) = rows, params
    return (x + (1.0 + g1) * mm_nn(m, w),)


def resid_fn(rows, params):
    (x, o), (g2,) = rows, params
    return (x + (1.0 + g2) * o,)


def _params(sem, side_effects=False):
    return pltpu.CompilerParams(dimension_semantics=sem, vmem_limit_bytes=VMEM_LIMIT, has_side_effects=side_effects)


def _whole(a):
    nd = a.ndim
    return pl.BlockSpec(a.shape, lambda *_: (0,) * nd)


def _pick(n, cands):
    for c in cands:
        if n % c == 0:
            return c
    return n


def matmul(name, a, b, mode, out_dtype, side=None):
    if mode == "nn":
        (m, k), n = a.shape, b.shape[1]
    elif mode == "nt":
        (m, k), n = a.shape, b.shape[0]
    else:
        (k, m), n = a.shape, b.shape[1]
    tm = _pick(m, (512, 256, 128))
    tn = _pick(n, (1280, 1024, 1408, 768, 512, 384, 256, 128))
    tk = _pick(k, (1024, 1280, 1408, 768, 512, 256, 128))
    if mode == "tn":
        tm = _pick(m, (1024, 768, 512, 256, 128))
        tk = _pick(k, (512, 256, 128))
    nk = k // tk
    ca, cb = {"nn": (1, 0), "nt": (1, 1), "tn": (0, 0)}[mode]

    def core(a_ref, b_ref, o_ref, acc_ref):
        kk = pl.program_id(2)

        @pl.when(kk == 0)
        def _():
            acc_ref[...] = jnp.zeros_like(acc_ref)

        acc_ref[...] += _dg(a_ref[...], b_ref[...], ca, cb)

        @pl.when(kk == nk - 1)
        def _():
            o_ref[...] = acc_ref[...].astype(o_ref.dtype)

    a_spec = (pl.BlockSpec((tk, tm), lambda i, j, q: (q, i)) if mode == "tn"
              else pl.BlockSpec((tm, tk), lambda i, j, q: (i, q)))
    b_spec = (pl.BlockSpec((tn, tk), lambda i, j, q: (j, q)) if mode == "nt"
              else pl.BlockSpec((tk, tn), lambda i, j, q: (q, j)))
    grid = (m // tm, n // tn, nk)
    body, s_in, s_out, s_shape, s_scr, s_args = _with_side(core, 2, 1, side, grid)
    sem = ("arbitrary",) * 3 if side else ("parallel", "parallel", "arbitrary")
    res = pl.pallas_call(
        body, name=name, grid=grid,
        in_specs=[a_spec, b_spec] + s_in,
        out_specs=[pl.BlockSpec((tm, tn), lambda i, j, q: (i, j))] + s_out,
        out_shape=[jax.ShapeDtypeStruct((m, n), out_dtype)] + s_shape,
        scratch_shapes=[pltpu.VMEM((tm, tn), F32)] + s_scr,
        compiler_params=_params(sem, side is not None),
    )(a, b, *s_args)
    return (res[0], res[1:]) if side else res[0]


def bmatmul(name, a, b, mode, out_dtype, out_batched):
    ab, bb = a.ndim == 3, b.ndim == 3
    nb = a.shape[0] if ab else b.shape[0]
    a2, b2 = a.shape[-2:], b.shape[-2:]
    if mode == "nn":
        (m, k), n = a2, b2[1]
    elif mode == "nt":
        (m, k), n = a2, b2[0]
    else:
        (k, m), n = a2, b2[1]
    tm = _pick(m, (1024, 512, 256, 128) if mode == "tn" else (512, 256, 128))
    tn = _pick(n, (1024, 512, 256, 128))
    tk = _pick(k, (512, 256, 128) if mode == "tn" else (1024, 512, 256, 128))
    nk = k // tk
    ca, cb = {"nn": (1, 0), "nt": (1, 1), "tn": (0, 0)}[mode]
    ids = (lambda g: g) if out_batched else (lambda g: (g[2], g[0], g[1], g[3]))
    grid = (nb, m // tm, n // tn, nk) if out_batched else (m // tm, n // tn, nb, nk)

    def a_map(*g):
        bi, i, j, q = ids(g)
        idx = (q, i) if mode == "tn" else (i, q)
        return (bi,) + idx if ab else idx

    def b_map(*g):
        bi, i, j, q = ids(g)
        idx = (j, q) if mode == "nt" else (q, j)
        return (bi,) + idx if bb else idx

    def o_map(*g):
        bi, i, j, q = ids(g)
        return (bi, i, j) if out_batched else (i, j)

    def body(a_ref, b_ref, o_ref, acc_ref):
        bi, _, _, q = ids(tuple(pl.program_id(d) for d in range(4)))
        first = (q == 0) if out_batched else (q == 0) & (bi == 0)
        last = (q == nk - 1) if out_batched else (q == nk - 1) & (bi == nb - 1)

        @pl.when(first)
        def _():
            acc_ref[...] = jnp.zeros_like(acc_ref)

        acc_ref[...] += _dg(a_ref[...], b_ref[...], ca, cb)

        @pl.when(last)
        def _():
            o_ref[...] = acc_ref[...].astype(o_ref.dtype)

    a_blk = (tk, tm) if mode == "tn" else (tm, tk)
    b_blk = (tn, tk) if mode == "nt" else (tk, tn)
    return pl.pallas_call(
        body, name=name, grid=grid,
        in_specs=[pl.BlockSpec(((None,) if ab else ()) + a_blk, a_map), pl.BlockSpec(((None,) if bb else ()) + b_blk, b_map)],
        out_specs=pl.BlockSpec(((None,) if out_batched else ()) + (tm, tn), o_map),
        out_shape=jax.ShapeDtypeStruct(((nb,) if out_batched else ()) + (m, n), out_dtype),
        scratch_shapes=[pltpu.VMEM((tm, tn), F32)],
        compiler_params=_params(("parallel", "parallel", "arbitrary", "arbitrary")),
    )(a, b)


def colgather(name, src, idx, dst_w, out_dtype):
    nsrc, rows, w = src.shape
    tw = GATHER_TILE
    nbs = -(-w // tw)
    ne = idx.shape[0]
    nbd = idx.shape[1] // tw
    tiles = [sorted(set((idx[e, t * tw:(t + 1) * tw][idx[e, t * tw:(t + 1) * tw] >= 0] // tw).tolist()))
             for e in range(ne) for t in range(nbd)]
    nslot = max(1, max(len(t) for t in tiles))
    tbl = np.full((ne * nbd, nslot), -1, np.int32)
    for i, t in enumerate(tiles):
        tbl[i, :len(t)] = t
    exact3 = src.dtype == F32

    def body(tbl_ref, idx_ref, src_ref, o_ref, acc_ref):
        ti, si = pl.program_id(0), pl.program_id(1)

        @pl.when(si == 0)
        def _():
            acc_ref[...] = jnp.zeros_like(acc_ref)

        t = tbl_ref[ti * nslot + si]

        @pl.when(t >= 0)
        def _():
            onehot = ((_iota((tw, tw), 0) + t * tw) == idx_ref[...]).astype(BF16)
            col = _iota((1, tw), 1) + (t % nbs) * tw
            xv = jnp.where(col < w, src_ref[...], jnp.zeros((), src_ref.dtype))
            d = lambda p: lax.dot_general(p, onehot, (((1,), (0,)), ((), ())), preferred_element_type=F32)
            if exact3:
                x1, x2, x3 = _split3(xv)
                acc_ref[...] += (d(x3) + d(x2)) + d(x1)
            else:
                acc_ref[...] += d(xv)

        @pl.when(si == nslot - 1)
        def _():
            o_ref[...] = acc_ref[...].astype(o_ref.dtype)

    def src_map(ti, si, tbl_ref):
        t = jnp.maximum(tbl_ref[ti * nslot + si], 0)
        return (t // nbs, 0, t % nbs)

    grid_spec = pltpu.PrefetchScalarGridSpec(
        num_scalar_prefetch=1, grid=(ne * nbd, nslot),
        in_specs=[pl.BlockSpec((None, 1, tw), lambda ti, si, tbl_ref: (ti // nbd, 0, ti % nbd)),
                  pl.BlockSpec((None, rows, tw), src_map)],
        out_specs=pl.BlockSpec((None, rows, tw), lambda ti, si, tbl_ref: (ti // nbd, 0, ti % nbd)),
        scratch_shapes=[pltpu.VMEM((rows, tw), F32)])
    return pl.pallas_call(
        body, name=name, grid_spec=grid_spec,
        out_shape=jax.ShapeDtypeStruct((ne, rows, dst_w), out_dtype),
        compiler_params=_params(("parallel", "arbitrary")),
    )(jnp.asarray(tbl.reshape(-1)), jnp.asarray(idx.reshape(ne, 1, nbd * tw).astype(np.int32)), src)


def swiglu3_fwd(name, gu, tm):
    _, nb, s, w = gu.shape

    def body(x_ref, o_ref):
        o_ref[...] = (_silu(x_ref[0].astype(F32)) * x_ref[1].astype(F32)).astype(o_ref.dtype)

    return pl.pallas_call(
        body, name=name, grid=(nb, s // tm),
        in_specs=[pl.BlockSpec((2, None, tm, w), lambda b, i: (0, b, i, 0))],
        out_specs=pl.BlockSpec((None, tm, w), lambda b, i: (b, i, 0)),
        out_shape=jax.ShapeDtypeStruct((nb, s, w), BF16),
        compiler_params=_params(("parallel", "parallel")),
    )(gu)


def swiglu3_bwd(name, gu, dact, tm):
    _, nb, s, w = gu.shape

    def body(x_ref, g_ref, o_ref):
        _, vjp = jax.vjp(lambda a, b: _silu(a) * b, x_ref[0].astype(F32), x_ref[1].astype(F32))
        dg, du = vjp(g_ref[...].astype(F32))
        o_ref[0] = dg.astype(o_ref.dtype)
        o_ref[1] = du.astype(o_ref.dtype)

    return pl.pallas_call(
        body, name=name, grid=(nb, s // tm),
        in_specs=[pl.BlockSpec((2, None, tm, w), lambda b, i: (0, b, i, 0)),
                  pl.BlockSpec((None, tm, w), lambda b, i: (b, i, 0))],
        out_specs=pl.BlockSpec((2, None, tm, w), lambda b, i: (0, b, i, 0)),
        out_shape=jax.ShapeDtypeStruct(gu.shape, BF16),
        compiler_params=_params(("parallel", "parallel")),
    )(gu, dact)


def _row_specs(rows, tm):
    return [pl.BlockSpec((tm, w), lambda i, _c=c: (i, _c)) for (_, w, c) in rows]


def rowstage_fwd(name, fn, rows, params, outs, tm):
    s = rows[0][0].shape[0]
    nr, npar = len(rows), len(params)

    def body(*refs):
        r = [x[...].astype(F32) for x in refs[:nr]]
        p = [x[...].astype(F32) for x in refs[nr:nr + npar]]
        for ref, val in zip(refs[nr + npar:], fn(r, p)):
            ref[...] = val.astype(ref.dtype)

    res = pl.pallas_call(
        body, name=name, grid=(s // tm,),
        in_specs=_row_specs(rows, tm) + [_whole(p) for p in params],
        out_specs=[pl.BlockSpec((tm, w), lambda i: (i, 0)) for (w, _) in outs],
        out_shape=[jax.ShapeDtypeStruct((s, w), dt) for (w, dt) in outs],
        compiler_params=_params(("parallel",)),
    )(*[r[0] for r in rows], *params)
    return res


def rowstage_bwd(name, fn, rows, params, douts, drow_dtypes, tm, adds=None, into=None):
    s = rows[0][0].shape[0]
    nr, npar, no = len(rows), len(params), len(douts)
    adds = adds or {}
    add_idx = sorted(adds)
    na = len(add_idx)
    into = into or {}
    into_idx = sorted(into)
    nb = len(into_idx)

    def body(*refs):
        r = [x[...].astype(F32) for x in refs[:nr]]
        p = [x[...].astype(F32) for x in refs[nr:nr + npar]]
        g = [x[...].astype(F32) for x in refs[nr + npar:nr + npar + no]]
        a_refs = refs[nr + npar + no:nr + npar + no + na]
        dr_refs = refs[nr + npar + no + na + nb:nr + npar + no + na + nb + nr]
        dp_refs = refs[nr + npar + no + na + nb + nr:]
        _, vjp = jax.vjp(lambda r_, p_: tuple(fn(r_, p_)), r, p)
        dr, dp = vjp(tuple(g))
        for j, (ref, val) in enumerate(zip(dr_refs, dr)):
            if j in adds:
                val = val + a_refs[add_idx.index(j)][...].astype(F32)
            ref[...] = val.astype(ref.dtype)

        @pl.when(pl.program_id(0) == 0)
        def _():
            for ref in dp_refs:
                ref[...] = jnp.zeros_like(ref)

        for ref, val in zip(dp_refs, dp):
            ref[...] += val

    res = pl.pallas_call(
        body, name=name, grid=(s // tm,),
        in_specs=(_row_specs(rows, tm) + [_whole(p) for p in params]
                  + [pl.BlockSpec((tm, d.shape[1]), lambda i: (i, 0)) for d in douts]
                  + [pl.BlockSpec((tm, rows[j][1]), lambda i: (i, 0)) for j in add_idx]
                  + [pl.BlockSpec(memory_space=pl.ANY)] * nb),
        out_specs=([pl.BlockSpec((tm, w), lambda i, _c=(into[j][1] if j in into else 0): (i, _c))
                    for j, (_, w, _) in enumerate(rows)] + [_whole(p) for p in params]),
        out_shape=([jax.ShapeDtypeStruct(into[j][0].shape if j in into else (s, w), dt)
                    for j, ((_, w, _), dt) in enumerate(zip(rows, drow_dtypes))]
                   + [jax.ShapeDtypeStruct(p.shape, F32) for p in params]),
        input_output_aliases={nr + npar + no + na + k: j for k, j in enumerate(into_idx)},
        compiler_params=_params(("arbitrary",)),
    )(*[r[0] for r in rows], *params, *douts, *[adds[j] for j in add_idx], *[into[j][0] for j in into_idx])
    return res[:nr], res[nr:]


def _flip(index_map, nc):
    return lambda h, n: index_map(h, nc - 1 - n)


def _with_side(core, n_in, n_out, side, grid):
    if side is None:
        return core, [], [], [], [], ()
    sends, broadcast = side
    k = len(sends)

    def body(*refs):
        ins, snd = refs[:n_in], refs[n_in:n_in + k]
        outs, rcv = refs[n_in + k:n_in + k + n_out], refs[n_in + k + n_out:n_in + 2 * k + n_out]
        scr = refs[n_in + 2 * k + n_out:]
        start, wait = _exchange_ops(snd, rcv, *scr[1:], broadcast)
        ids = [pl.program_id(d) for d in range(len(grid))]
        first = functools.reduce(lambda a, b: a & b, [i == 0 for i in ids])
        last = functools.reduce(lambda a, b: a & b, [i == g - 1 for i, g in zip(ids, grid)])
        pl.when(first)(start)
        core(*ins, *outs, scr[0])
        pl.when(last)(wait)

    return body, [HBM_SPEC] * k, [HBM_SPEC] * k, _exchange_out(sends, broadcast), _exchange_sems(k), tuple(sends)


def _take(v, split, j):
    if split is None:
        return v
    if split[0] == "lane":
        return v[:, j * split[1]:(j + 1) * split[1]]
    if split[0] == "lead":
        return v[j * split[1]:(j + 1) * split[1]]
    return v[j]


def _heads(vals, specs, hb):
    return [v if s[-1] is None else jnp.stack([_take(v, s[-1], j) for j in range(hb)]) for v, s in zip(vals, specs)]


def _over_heads(chunk_fn, hb, seqs, hparams, batched):
    seq_ax = [None if s[3] is None else 0 for s in seqs]
    hp_ax = [None if s[3] is None else 0 for s in hparams]
    if batched:
        return jax.vmap(chunk_fn, in_axes=(seq_ax, hp_ax, None, 0))

    def looped(seq, hp, sp, st):
        pick = lambda vals, axes, j: [v if a is None else v[j] for v, a in zip(vals, axes)]
        res = [chunk_fn(pick(seq, seq_ax, j), pick(hp, hp_ax, j), sp, st[j]) for j in range(hb)]
        pile = lambda parts: jnp.concatenate([p[None] for p in parts], axis=0)
        return tuple(pile(o) for o in zip(*[r[0] for r in res])), pile([r[1] for r in res])

    return looped


def _where(split, j):
    if split[0] == "lane":
        return (slice(None), slice(j * split[1], (j + 1) * split[1]))
    if split[0] == "lead":
        return (slice(j * split[1], (j + 1) * split[1]),)
    return (j,)


def scan_fwd(name, chunk_fn, nblk, hb, nc, seqs, hparams, sparams, state_shape, outs, batched, side=None):
    ns, nhp, nsp, no = len(seqs), len(hparams), len(sparams), len(outs)

    def core(*refs):
        seq_r, hp_r, sp_r = refs[:ns], refs[ns:ns + nhp], refs[ns + nhp:ns + nhp + nsp]
        out_r = refs[ns + nhp + nsp:ns + nhp + nsp + no]
        st_out, st_scr = refs[-2], refs[-1]

        @pl.when(pl.program_id(1) == 0)
        def _():
            st_scr[...] = jnp.zeros_like(st_scr)

        seq_v = [x[...].astype(F32) for x in seq_r]
        hp_v = [x[...] for x in hp_r]
        sp_v = [x[...] for x in sp_r]
        st = st_scr[...]
        st_out[...] = st
        heads = _over_heads(chunk_fn, hb, seqs, hparams, batched)
        o, st_new = heads(_heads(seq_v, seqs, hb), _heads(hp_v, hparams, hb), sp_v, st)
        for ref, spec, val in zip(out_r, outs, o):
            for j in range(hb):
                ref[_where(spec[4], j)] = val[j].astype(ref.dtype)
        st_scr[...] = st_new

    nst = len(state_shape)
    body, s_in, s_out, s_shape, s_scr, s_args = _with_side(core, ns + nhp + nsp, no + 1, side, (nblk, nc))
    res = pl.pallas_call(
        body, name=name, grid=(nblk, nc),
        in_specs=([pl.BlockSpec(bs, im) for (_, bs, im, _) in seqs]
                  + [pl.BlockSpec(bs, lambda h, n, _im=im: _im(h)) for (_, bs, im, _) in hparams]
                  + [_whole(p) for p in sparams] + s_in),
        out_specs=([pl.BlockSpec(bs, im) for (_, _, bs, im, _) in outs]
                   + [pl.BlockSpec((hb, None) + tuple(state_shape), lambda h, n: (h, n) + (0,) * nst)] + s_out),
        out_shape=([jax.ShapeDtypeStruct(fs, dt) for (fs, dt, _, _, _) in outs]
                   + [jax.ShapeDtypeStruct((nblk * hb, nc) + tuple(state_shape), F32)] + s_shape),
        scratch_shapes=[pltpu.VMEM((hb,) + tuple(state_shape), F32)] + s_scr,
        compiler_params=_params(("arbitrary", "arbitrary"), side is not None),
    )(*[x[0] for x in seqs], *[x[0] for x in hparams], *sparams, *s_args)
    return res[:no], res[no], res[no + 1:]


def scan_bwd(name, chunk_fn, nblk, hb, nc, seqs, hparams, sparams, state_shape, states, douts, dseqs, batched, side=None):
    ns, nhp, nsp, no = len(seqs), len(hparams), len(sparams), len(douts)
    nst = len(state_shape)
    buf_of = [i for i, sp in enumerate(dseqs) if len(sp) > 5 and sp[5] is not None]
    bufs = [dseqs[i][5] for i in buf_of]

    def core(*refs):
        seq_r, hp_r, sp_r = refs[:ns], refs[ns:ns + nhp], refs[ns + nhp:ns + nhp + nsp]
        base = ns + nhp + nsp
        st_r = refs[base]
        do_r = refs[base + 1:base + 1 + no]
        base += 1 + no + len(bufs)
        ds_r, dhp_r, dsp_r = refs[base:base + ns], refs[base + ns:base + ns + nhp], refs[base + ns + nhp:base + ns + nhp + nsp]
        dst_scr = refs[-1]
        h, n = pl.program_id(0), pl.program_id(1)

        @pl.when(n == 0)
        def _():
            dst_scr[...] = jnp.zeros_like(dst_scr)
            for ref in dhp_r:
                ref[...] = jnp.zeros_like(ref)

        @pl.when((n == 0) & (h == 0))
        def _():
            for ref in dsp_r:
                ref[...] = jnp.zeros_like(ref)

        seq_v = [x[...].astype(F32) for x in seq_r]
        hp_v = [x[...] for x in hp_r]
        sp_v = [x[...] for x in sp_r]
        do_v = [x[...].astype(F32) for x in do_r]
        prim = (_heads(seq_v, seqs, hb), _heads(hp_v, hparams, hb), sp_v, st_r[...])
        _, vjp = jax.vjp(_over_heads(chunk_fn, hb, seqs, hparams, batched), *prim)
        ds, dhp, dsp, dst = vjp((tuple(_heads(do_v, douts, hb)), dst_scr[...]))
        for ref, spec, val in zip(ds_r, dseqs, ds):
            if spec[4] is None:
                ref[...] = val.astype(ref.dtype)
            else:
                for j in range(hb):
                    ref[_where(spec[4], j)] = val[j].astype(ref.dtype)
        for ref, spec, val in zip(dhp_r, hparams, dhp):
            for j in range(hb):
                ref[_where(spec[3], j)] += val[j]
        for ref, val in zip(dsp_r, dsp):
            ref[...] += val
        dst_scr[...] = dst

    n_in, n_out = ns + nhp + nsp + 1 + no + len(bufs), ns + nhp + nsp
    body, s_in, s_out, s_shape, s_scr, s_args = _with_side(core, n_in, n_out, side, (nblk, nc))
    res = pl.pallas_call(
        body, name=name, grid=(nblk, nc),
        in_specs=([pl.BlockSpec(bs, _flip(im, nc)) for (_, bs, im, _) in seqs]
                  + [pl.BlockSpec(bs, lambda h, n, _im=im: _im(h)) for (_, bs, im, _) in hparams]
                  + [_whole(p) for p in sparams]
                  + [pl.BlockSpec((hb, None) + tuple(state_shape), lambda h, n: (h, nc - 1 - n) + (0,) * nst)]
                  + [pl.BlockSpec(bs, _flip(im, nc)) for (_, bs, im, _) in douts]
                  + [pl.BlockSpec(memory_space=pl.ANY)] * len(bufs) + s_in),
        out_specs=([pl.BlockSpec(sp[2], _flip(sp[3], nc)) for sp in dseqs]
                   + [pl.BlockSpec(bs, lambda h, n, _im=im: _im(h)) for (_, bs, im, _) in hparams]
                   + [_whole(p) for p in sparams] + s_out),
        out_shape=([jax.ShapeDtypeStruct(sp[0], sp[1]) for sp in dseqs]
                   + [jax.ShapeDtypeStruct(x[0].shape, F32) for x in hparams]
                   + [jax.ShapeDtypeStruct(p.shape, F32) for p in sparams] + s_shape),
        scratch_shapes=[pltpu.VMEM((hb,) + tuple(state_shape), F32)] + s_scr,
        input_output_aliases={n_in - len(bufs) + k: i for k, i in enumerate(buf_of)},
        compiler_params=_params(("arbitrary", "arbitrary"), side is not None),
    )(*[x[0] for x in seqs], *[x[0] for x in hparams], *sparams, states, *[x[0] for x in douts], *bufs, *s_args)
    return res[:ns], res[ns:ns + nhp], res[ns + nhp:n_out], res[n_out:]


def _shift_down(x, n, rows):
    if n == 0:
        return x
    return jnp.where(rows >= n, pltpu.roll(x, n, 0), 0.0)


def _shift_up(x, n, rows):
    if n == 0:
        return x
    s = x.shape[0]
    return jnp.where(rows < s - n, pltpu.roll(x, s - n, 0), 0.0)


def conv_fwd(name, x, col0, w, b):
    s, cw = x.shape[0], w.shape[1]

    def body(x_ref, w_ref, b_ref, o_ref):
        xv = x_ref[...]
        rows = _iota(xv.shape, 0)
        u = jnp.broadcast_to(b_ref[...], xv.shape)
        for j in range(CONV_K):
            u = u + w_ref[j:j + 1, :] * _shift_down(xv, CONV_K - 1 - j, rows)
        o_ref[...] = _silu(u)

    return pl.pallas_call(
        body, name=name, grid=(cw // LANES,),
        in_specs=[pl.BlockSpec((s, LANES), lambda j: (0, col0 + j)),
                  pl.BlockSpec((CONV_K, LANES), lambda j: (0, j)),
                  pl.BlockSpec((1, LANES), lambda j: (0, j))],
        out_specs=pl.BlockSpec((s, LANES), lambda j: (0, j)),
        out_shape=jax.ShapeDtypeStruct((s, cw), F32),
        compiler_params=_params(("parallel",)),
    )(x, w, b)


def conv_bwd(name, x, col0, w, b, dout, into):
    s, cw = x.shape[0], w.shape[1]

    def body(x_ref, w_ref, b_ref, g_ref, into_ref, dx_ref, dw_ref, db_ref):
        xv = x_ref[...]
        rows = _iota(xv.shape, 0)
        sh = [_shift_down(xv, CONV_K - 1 - j, rows) for j in range(CONV_K)]
        u = jnp.broadcast_to(b_ref[...], xv.shape)
        for j in range(CONV_K):
            u = u + w_ref[j:j + 1, :] * sh[j]
        sg = _sigmoid(u)
        du = g_ref[...] * (sg * (1.0 + u * (1.0 - sg)))
        dx = jnp.zeros_like(xv)
        for j in range(CONV_K):
            dx = dx + w_ref[j:j + 1, :] * _shift_up(du, CONV_K - 1 - j, rows)
            dw_ref[j:j + 1, :] = jnp.sum(du * sh[j], axis=0, keepdims=True)
        dx_ref[...] = dx.astype(dx_ref.dtype)
        db_ref[...] = jnp.sum(du, axis=0, keepdims=True)

    return pl.pallas_call(
        body, name=name, grid=(cw // LANES,),
        in_specs=[pl.BlockSpec((s, LANES), lambda j: (0, col0 + j)),
                  pl.BlockSpec((CONV_K, LANES), lambda j: (0, j)),
                  pl.BlockSpec((1, LANES), lambda j: (0, j)),
                  pl.BlockSpec((s, LANES), lambda j: (0, j)),
                  pl.BlockSpec(memory_space=pl.ANY)],
        out_specs=[pl.BlockSpec((s, LANES), lambda j: (0, col0 + j)),
                   pl.BlockSpec((CONV_K, LANES), lambda j: (0, j)),
                   pl.BlockSpec((1, LANES), lambda j: (0, j))],
        out_shape=[jax.ShapeDtypeStruct(into.shape, into.dtype), jax.ShapeDtypeStruct((CONV_K, cw), F32),
                   jax.ShapeDtypeStruct((1, cw), F32)],
        input_output_aliases={4: 0},
        compiler_params=_params(("parallel",)),
    )(x, w, b, dout, into)


def exchange(name, sends, broadcast):
    nop = len(sends)

    def body(*refs):
        start, wait = _exchange_ops(refs[:nop], refs[nop:2 * nop], *refs[2 * nop:], broadcast)
        start()
        wait()

    return pl.pallas_call(
        body, name=name,
        in_specs=[HBM_SPEC] * nop, out_specs=[HBM_SPEC] * nop,
        out_shape=_exchange_out(sends, broadcast), scratch_shapes=_exchange_sems(nop),
        compiler_params=pltpu.CompilerParams(has_side_effects=True),
    )(*sends)


HBM_SPEC = pl.BlockSpec(memory_space=pltpu.HBM)


def _exchange_out(sends, broadcast):
    return [jax.ShapeDtypeStruct((N_DEV,) + tuple(t.shape if broadcast else t.shape[1:]), t.dtype) for t in sends]


def _exchange_sems(nop):
    return [pltpu.SemaphoreType.DMA((nop * (N_DEV - 1),)), pltpu.SemaphoreType.DMA((nop * (N_DEV - 1),)),
            pltpu.SemaphoreType.DMA((nop,))]


def _exchange_ops(send_refs, recv_refs, send_sems, recv_sems, local_sems, broadcast):
    nop = len(send_refs)
    x, y, c = lax.axis_index("x"), lax.axis_index("y"), lax.axis_index("c")
    me = 4 * x + 2 * y + c
    peers = []
    for k in range(1, N_DEV):
        px = 1 - x if (k >> 2) & 1 else x
        py = 1 - y if (k >> 1) & 1 else y
        pc = 1 - c if k & 1 else c
        peers.append(((px, py, pc), 4 * px + 2 * py + pc))

    def src(i, peer):
        return send_refs[i] if broadcast else send_refs[i].at[peer]

    def remote(i, k, arrival):
        dev, peer = peers[k]
        return pltpu.make_async_remote_copy(
            src_ref=src(i, peer), dst_ref=recv_refs[i].at[peer if arrival else me],
            send_sem=send_sems.at[i * (N_DEV - 1) + k], recv_sem=recv_sems.at[i * (N_DEV - 1) + k],
            device_id=dev, device_id_type=pl.DeviceIdType.MESH)

    def local(i):
        return pltpu.make_async_copy(src(i, me), recv_refs[i].at[me], local_sems.at[i])

    def start():
        for i in range(nop):
            local(i).start()
        for k in range(N_DEV - 1):
            for i in range(nop):
                remote(i, k, False).start()

    def wait():
        for k in range(N_DEV - 1):
            for i in range(nop):
                remote(i, k, True).wait_recv()
        for k in range(N_DEV - 1):
            for i in range(nop):
                remote(i, k, False).wait_send()
        for i in range(nop):
            local(i).wait()

    return start, wait


def adamw_sum(name, parts, w, m, v, layer=None, into=None):
    rws, cols = w.shape[-2:]
    nsum = parts.shape[0]
    tr = _pick(rws, (256, 128, 64, 32, 16, 8))
    c1 = 1.0 / (1.0 - ADAM_B1 ** ADAM_STEP)
    c2 = 1.0 / (1.0 - ADAM_B2 ** ADAM_STEP)

    def body(p_ref, w_ref, m_ref, v_ref, *rest):
        g_ref, d_ref, nm_ref, nv_ref = rest[-4:]
        g = p_ref[0]
        for j in range(1, nsum):
            g = g + p_ref[j]
        nm = ADAM_B1 * m_ref[...] + (1.0 - ADAM_B1) * g
        nv = ADAM_B2 * v_ref[...] + (1.0 - ADAM_B2) * (g * g)
        g_ref[...] = g
        nm_ref[...] = nm
        nv_ref[...] = nv
        d_ref[...] = -ADAM_LR * ((nm * c1) / (jnp.sqrt(nv * c2) + ADAM_EPS) + ADAM_WD * w_ref[...])

    if layer is None:
        blk = pl.BlockSpec((tr, cols), lambda i: (i, 0))
    else:
        blk = pl.BlockSpec((None, tr, cols), lambda i: (layer, i, 0))
    if into is None and layer is not None:
        into = [lax.empty(w.shape, F32) for _ in range(4)]
    extra = list(into) if into else []
    return pl.pallas_call(
        body, name=name, grid=(rws // tr,),
        in_specs=([pl.BlockSpec((nsum, tr, cols), lambda i: (0, i, 0)), blk, blk, blk]
                  + [pl.BlockSpec(memory_space=pl.ANY)] * len(extra)),
        out_specs=[blk, blk, blk, blk],
        out_shape=[jax.ShapeDtypeStruct(w.shape, F32)] * 4,
        input_output_aliases={4 + k: k for k in range(len(extra))},
        compiler_params=_params(("parallel",)),
    )(parts, w, m, v, *extra)


def ada_fwd(name, c_all, w, b):
    nl = w.shape[0]

    def body(c_ref, w_ref, b_ref, o_ref):
        ca = _silu(c_ref[...])
        for l in range(nl):
            o_ref[l] = mm_nn(ca, w_ref[l]) + b_ref[l]

    return pl.pallas_call(
        body, name=name,
        out_shape=jax.ShapeDtypeStruct((nl, c_all.shape[0], w.shape[2]), F32),
        compiler_params=pltpu.CompilerParams(vmem_limit_bytes=VMEM_LIMIT),
    )(c_all, w, b)


def ada_bwd(name, c_all, dmod):
    nl = dmod.shape[0]

    def body(c_ref, g_ref, o_ref):
        ca = _silu(c_ref[...])
        for l in range(nl):
            o_ref[l] = mm_tn(ca, g_ref[l])

    return pl.pallas_call(
        body, name=name,
        out_shape=jax.ShapeDtypeStruct((nl, c_all.shape[1], dmod.shape[2]), F32),
        compiler_params=pltpu.CompilerParams(vmem_limit_bytes=VMEM_LIMIT),
    )(c_all, dmod)


def lower_bounds_fn(rows, params):
    (lg,), _ = rows, params
    nl = lg.shape[0]
    mx = jnp.max(lg, axis=0, keepdims=True)
    e = jnp.exp(lg - mx)
    p = e / jnp.sum(e, axis=0, keepdims=True)
    layer = _iota((nl, 1), 0)
    acc = jnp.zeros_like(p)
    for j in range(1, nl):
        pj = jnp.sum(jnp.where(layer == j, p, 0.0), axis=0, keepdims=True)
        acc = acc + jnp.where(layer >= j, 1.0, 0.0) * pj
    return (acc,)


def loss_call(name, x, tgt, nw, tm):
    s, d = x.shape

    def body(x_ref, t_ref, w_ref, l_ref, dx_ref, dw_ref):
        def f(xv, wv):
            err = _rms(xv, wv) - t_ref[...]
            return jnp.sum(0.5 * jnp.mean(err * err, axis=-1, keepdims=True), axis=0, keepdims=True)

        val, vjp = jax.vjp(f, x_ref[...], w_ref[...])
        dx, dw = vjp(jnp.ones_like(val))

        @pl.when(pl.program_id(0) == 0)
        def _():
            l_ref[...] = jnp.zeros_like(l_ref)
            dw_ref[...] = jnp.zeros_like(dw_ref)

        l_ref[...] += jnp.broadcast_to(val, l_ref.shape)
        dw_ref[...] += dw
        dx_ref[...] = dx

    row = pl.BlockSpec((tm, d), lambda i: (i, 0))
    return pl.pallas_call(
        body, name=name, grid=(s // tm,),
        in_specs=[row, row, _whole(nw)],
        out_specs=[pl.BlockSpec((8, LANES), lambda i: (0, 0)), row, _whole(nw)],
        out_shape=[jax.ShapeDtypeStruct((8, LANES), F32), jax.ShapeDtypeStruct((s, d), F32),
                   jax.ShapeDtypeStruct(nw.shape, F32)],
        compiler_params=_params(("arbitrary",)),
    )(x, tgt, nw)


class Dims:
    def __init__(self, s, d, ffn):
        self.s, self.d, self.ffn = s, d, ffn
        self.mix = 3 * d // 4
        self.nh = self.mix // HEAD
        self.ssm_heads = self.mix // SSM_P
        self.pairs = self.mix // (2 * SSM_P)
        self.nc = s // CHUNK
        self.conv_ssm = self.mix + 4 * HEAD
        self.conv_w = self.conv_ssm + 3 * self.mix
        self.o_gates = 4 * self.mix
        self.o_sz = self.o_gates + 3 * d
        self.o_gz = self.o_sz + self.mix
        self.o_conv = self.o_gz + self.mix
        self.o_small = self.o_conv + self.conv_w
        used = self.o_small + LANES
        self.np = -(-used // 1280) * 1280
        self.tm = _pick(s, (256, 128, 64))
        mix, nh = self.mix, self.nh
        self.in_sizes = (mix, mix, mix, mix, mix, self.conv_ssm, self.ssm_heads, 3 * mix, mix, nh, nh, 3 * d)
        self.in_width = sum(self.in_sizes)


def w_in_tables(dm, nshard):
    off = np.cumsum((0,) + dm.in_sizes)
    hq, hf, hi, hg, sz, sxbc, sdt, gqkv, gz, gb, ga, gates = (np.arange(off[i], off[i + 1]) for i in range(12))
    hgrn = np.stack([t.reshape(dm.nh, HEAD) for t in (hq, hf, hi, hg)], axis=1).reshape(-1)
    perm = np.concatenate([hgrn, gates, sz, gz, gqkv, sxbc, sdt, gb, ga])
    perm = np.concatenate([perm, np.full(dm.np - perm.size, -1)])
    shard = dm.in_width // nshard
    wpad = -(-shard // GATHER_TILE) * GATHER_TILE
    fwd = np.where(perm >= 0, (perm // shard) * wpad + perm % shard, -1)[None]
    inv = np.zeros(dm.in_width, np.int64)
    inv[perm[perm >= 0]] = np.nonzero(perm >= 0)[0]
    bwd = np.full((nshard, wpad), -1)
    bwd[:, :shard] = inv.reshape(nshard, shard)
    return fwd.astype(np.int32), bwd.astype(np.int32)


def _small_views(dm, small):
    t = small.T
    col = lambda a: a[:, :, None]
    row = lambda a: a.reshape(a.shape[0], dm.nc, 1, CHUNK)
    a, b = dm.ssm_heads, dm.ssm_heads + dm.nh
    sdt, gb, ga = t[:a], t[a:b], t[b:b + dm.nh]
    return col(sdt), row(sdt), col(gb), col(ga), row(ga)


def _scan_specs(dm, proj, conv_out, views, lp, dproj=None):
    dt_col, dt_row, gb_col, ga_col, ga_row = views
    mixb, nh = dm.mix // LANES, dm.nh
    s, mix = dm.s, dm.mix
    lane = ("lane", LANES)
    hb = HGRN_HEADS_PER_STEP
    hw = (CHUNK, hb * LANES)
    hgrn = dict(
        nblk=nh // hb, hb=hb, fn=hgrn_chunk, batched=False,
        seqs=[(proj, (CHUNK, hb * 4 * HEAD), lambda h, n: (n, h), ("lane", 4 * HEAD))],
        hparams=[(lp["lb"], (1, hb * HEAD), lambda h: (0, h), lane)],
        sparams=[lp["hgrn_norm"]],
        dseqs=[((s, dm.np), BF16, (CHUNK, hb * 4 * HEAD), lambda h, n: (n, h), ("lane", 4 * HEAD), dproj)],
        io=(hw, lambda h, n: (n, h), lane))
    ppg = dm.pairs // 2
    qb = 3 * mixb
    gw = (CHUNK, ppg * LANES)
    pcol = ((2 * ppg, CHUNK, 1), lambda g, n: (g, n, 0), ("lead", 2))
    prow = ((2 * ppg, None, 1, CHUNK), lambda g, n: (g, n, 0, 0), ("lead", 2))
    ppar = ((2 * ppg, 1, 1), lambda g: (g, 0, 0), ("lead", 2))
    bc = lambda first: ((CHUNK, LANES), lambda g, n: (n, first + g), None)
    ssd = dict(
        nblk=2, hb=ppg, fn=ssd_chunk, batched=False,
        seqs=[(conv_out, gw, lambda g, n: (n, qb // ppg + g), lane), (conv_out,) + bc(qb + mixb), (conv_out,) + bc(qb + mixb + 2),
              (dt_col,) + pcol, (dt_row,) + prow],
        hparams=[(lp["ssm_dt_bias"],) + ppar, (lp["ssm_a_log"],) + ppar],
        sparams=[],
        dseqs=[((s, mix), F32, gw, lambda g, n: (n, g), lane), ((s, 2 * LANES), F32) + bc(0), ((s, 2 * LANES), F32) + bc(0),
               (dt_col.shape, F32) + pcol, (dt_row.shape, F32) + prow],
        io=(gw, lambda g, n: (n, g), lane))
    hb = GDN_HEADS_PER_STEP
    hw = (CHUNK, hb * LANES)
    cq, cgz = 0, dm.o_gz // LANES
    assert nh % hb == 0 and cgz % hb == 0 and qb % ppg == 0
    hcol = ((hb, CHUNK, 1), lambda h, n: (h, n, 0), ("idx",))
    hrow = ((hb, None, 1, CHUNK), lambda h, n: (h, n, 0, 0), ("idx",))
    hpar = ((hb, 1, 1), lambda h: (h, 0, 0), ("idx",))
    at = lambda first: (hw, lambda h, n: (n, first // hb + h), lane)
    gdn = dict(
        nblk=nh // hb, hb=hb, fn=gdn_chunk, batched=True,
        seqs=[(conv_out,) + at(cq), (conv_out,) + at(cq + nh), (conv_out,) + at(cq + 2 * nh), (proj,) + at(cgz),
              (gb_col,) + hcol, (ga_col,) + hcol, (ga_row,) + hrow],
        hparams=[(lp["gdn_dt_bias"],) + hpar, (lp["gdn_a_log"],) + hpar],
        sparams=[lp["gdn_norm"]],
        dseqs=[((s, mix), F32) + at(0), ((s, mix), F32) + at(0), ((s, mix), F32) + at(0), ((s, dm.np), BF16) + at(cgz) + (dproj,),
               (gb_col.shape, F32) + hcol, (ga_col.shape, F32) + hcol, (ga_row.shape, F32) + hrow],
        io=(hw, lambda h, n: (n, h), lane))
    return hgrn, ssd, gdn


def _run_scan_fwd(dm, name, sp, side=None):
    out = ((dm.s, dm.mix), F32) + sp["io"]
    (y,), states, arrived = scan_fwd(name, sp["fn"], sp["nblk"], sp["hb"], dm.nc, sp["seqs"], sp["hparams"],
                                     sp["sparams"], (HEAD, HEAD), [out], sp["batched"], side)
    return y, states, arrived


def _run_scan_bwd(dm, name, sp, states, dy, side=None):
    return scan_bwd(name, sp["fn"], sp["nblk"], sp["hb"], dm.nc, sp["seqs"], sp["hparams"], sp["sparams"], (HEAD, HEAD),
                    states, [(dy,) + sp["io"]], sp["dseqs"], sp["batched"], side)


def _share_out(side):
    if side is None:
        return None, None, None
    s, broadcast = side
    return ([s[0]], broadcast), ([s[1], s[2], s[4]], broadcast), ([s[3]], broadcast)


def _collect(got_h, got_s, got_g):
    if not got_h:
        return None
    return [got_h[0], got_s[0], got_s[1], got_g[0], got_s[2]]


def layer_fwd(dm, l, x, lp, side=None):
    tm, d, mix = dm.tm, dm.d, dm.mix
    tag = f"l{l}_"
    (h,) = rowstage_fwd(tag + "norm1", normmod_fn, [(x, d, 0)], [lp["norm_mix"], lp["sc1"], lp["sh1"]], [(d, BF16)], tm)
    proj = matmul(tag + "proj", h, lp["w_in"], "nn", F32)
    conv_out = conv_fwd(tag + "conv", proj, dm.o_conv // LANES, lp["conv_w"], lp["conv_b"])
    small = proj[:, dm.o_small:dm.o_small + LANES]
    views = _small_views(dm, small)
    hg, sd, gd = _scan_specs(dm, proj, conv_out, views, lp)
    side_h, side_s, side_g = _share_out(side)
    yh, st_h, got_h = _run_scan_fwd(dm, tag + "hgrn", hg, side_h)
    y_ssd, st_s, got_s = _run_scan_fwd(dm, tag + "ssd", sd, side_s)
    yg, st_g, got_g = _run_scan_fwd(dm, tag + "gdn", gd, side_g)
    arrived = _collect(got_h, got_s, got_g)
    (ys,) = rowstage_fwd(tag + "ssmpost", ssmpost_fn,
                         [(y_ssd, mix, 0), (conv_out, mix, 3), (proj, mix, dm.o_sz // mix)],
                         [lp["ssm_d_exp"], lp["ssm_norm"]], [(mix, F32)], tm)
    (merged,) = rowstage_fwd(tag + "merge", merge_fn, [(yh, mix, 0), (ys, mix, 0), (yg, mix, 0), (proj, 3 * d, 1)],
                             [lp["b_merge"], lp["w_branch"]], [(d, BF16)], tm)
    (x1,) = rowstage_fwd(tag + "outproj", outproj_fn, [(merged, d, 0), (x, d, 0)], [lp["g1"], lp["w_out"]], [(d, F32)], tm)
    (h2,) = rowstage_fwd(tag + "norm2", normmod_fn, [(x1, d, 0)], [lp["norm_ffn"], lp["sc2"], lp["sh2"]], [(d, BF16)], tm)
    gu = bmatmul(tag + "ffn_in", h2, lp["w_ffn_in"], "nn", BF16, True)
    gu = gu.reshape((2, gu.shape[0] // 2) + gu.shape[1:])
    act = swiglu3_fwd(tag + "swiglu", gu, tm)
    o2 = bmatmul(tag + "ffn_out", act, lp["w_ffn_out"], "nn", F32, False)
    (x2,) = rowstage_fwd(tag + "resid", resid_fn, [(x1, d, 0), (o2, d, 0)], [lp["g2"]], [(d, F32)], tm)
    saved = dict(x=x, h=h, proj=proj, conv_out=conv_out, views=views, yh=yh, y_ssd=y_ssd, yg=yg, ys=ys,
                 st_h=st_h, st_s=st_s, st_g=st_g, merged=merged, x1=x1, h2=h2, gu=gu, act=act, o2=o2)
    return x2, saved, arrived


def layer_bwd(dm, l, dx2, lp, sv, side=None, own=False):
    tm, d, mix, s = dm.tm, dm.d, dm.mix, dm.s
    tag = f"l{l}_b_"
    g = {}
    (dx1_a, do2), (g["g2"],) = rowstage_bwd(tag + "resid", resid_fn, [(sv["x1"], d, 0), (sv["o2"], d, 0)], [lp["g2"]],
                                            [dx2], [F32, BF16], tm)
    dact = bmatmul(tag + "ffn_out_dx", do2, lp["w_ffn_out"], "nt", BF16, True)
    g["w_ffn_out"] = bmatmul(tag + "ffn_out_dw", sv["act"], do2, "tn", F32, True)
    dgu = swiglu3_bwd(tag + "swiglu", sv["gu"], dact, tm)
    dgu = dgu.reshape((-1,) + dgu.shape[2:])
    dh2 = bmatmul(tag + "ffn_in_dx", dgu, lp["w_ffn_in"], "nt", BF16, False)
    g["w_ffn_in"] = bmatmul(tag + "ffn_in_dw", sv["h2"], dgu, "tn", F32, True)
    (dx1,), (g["norm_ffn"], g["sc2"], g["sh2"]) = rowstage_bwd(
        tag + "norm2", normmod_fn, [(sv["x1"], d, 0)], [lp["norm_ffn"], lp["sc2"], lp["sh2"]], [dh2], [F32], tm,
        adds={0: dx1_a})
    (dmerged, dx_a), (g["g1"], g["w_out"]) = rowstage_bwd(
        tag + "outproj", outproj_fn, [(sv["merged"], d, 0), (sv["x"], d, 0)], [lp["g1"], lp["w_out"]], [dx1],
        [BF16, F32], tm)
    proj, conv_out = sv["proj"], sv["conv_out"]
    dproj = lax.empty((s, dm.np), BF16)
    (dyh, dys, dyg, dproj), (g["b_merge"], g["w_branch"]) = rowstage_bwd(
        tag + "merge", merge_fn, [(sv["yh"], mix, 0), (sv["ys"], mix, 0), (sv["yg"], mix, 0), (proj, 3 * d, 1)],
        [lp["b_merge"], lp["w_branch"]], [dmerged], [F32, F32, F32, BF16], tm, into={3: (dproj, 1)})
    (dy_ssd, dxs_a, dproj), (g["ssm_d_exp"], g["ssm_norm"]) = rowstage_bwd(
        tag + "ssmpost", ssmpost_fn, [(sv["y_ssd"], mix, 0), (conv_out, mix, 3), (proj, mix, dm.o_sz // mix)],
        [lp["ssm_d_exp"], lp["ssm_norm"]], [dys], [F32, F32, BF16], tm, into={2: (dproj, dm.o_sz // mix)})
    side_h, side_s, side_g = _share_out(side)
    hg, sd, _ = _scan_specs(dm, proj, conv_out, sv["views"], lp, dproj)
    (dproj,), (g["lb"],), (g["hgrn_norm"],), got_h = _run_scan_bwd(dm, tag + "hgrn", hg, sv["st_h"], dyh, side_h)
    gd = _scan_specs(dm, proj, conv_out, sv["views"], lp, dproj)[2]
    (dxs_b, dbp, dcp, d_dt_col, d_dt_row), (g["ssm_dt_bias"], g["ssm_a_log"]), _, got_s = _run_scan_bwd(
        dm, tag + "ssd", sd, sv["st_s"], dy_ssd, side_s)
    (dq, dk, dv, dproj, d_gb_col, d_ga_col, d_ga_row), (g["gdn_dt_bias"], g["gdn_a_log"]), (g["gdn_norm"],), got_g = _run_scan_bwd(
        dm, tag + "gdn", gd, sv["st_g"], dyg, side_g)
    arrived = _collect(got_h, got_s, got_g)
    dconv = jnp.concatenate([dq, dk, dv, dxs_a + dxs_b, dbp, dcp], axis=1)
    dproj, g["conv_w"], g["conv_b"] = conv_bwd(tag + "conv", proj, dm.o_conv // LANES, lp["conv_w"], lp["conv_b"], dconv,
                                               dproj)
    unrow = lambda t: t.reshape(t.shape[0], s).T
    dsmall = jnp.concatenate([d_dt_col[:, :, 0].T + unrow(d_dt_row), d_gb_col[:, :, 0].T,
                              d_ga_col[:, :, 0].T + unrow(d_ga_row)], axis=1)
    tail = jnp.pad(dsmall.astype(BF16), ((0, 0), (0, dm.np - dm.o_small - dsmall.shape[1])))
    dproj = lax.dynamic_update_slice(dproj, tail, (0, dm.o_small))
    if own:
        s_wb, s_wout, s_wf, s_wfo = small_shards(dm, g)
        dh, (got_wf,) = matmul(tag + "proj_dx", dproj, lp["w_in"], "nt", BF16, ([s_wf], False))
        g["w_in"], (got_wb, got_wout, got_wfo) = matmul(tag + "proj_dw", sv["h"], dproj, "tn", F32,
                                                        ([s_wb, s_wout, s_wfo], False))
    else:
        dh = matmul(tag + "proj_dx", dproj, lp["w_in"], "nt", BF16)
        g["w_in"] = matmul(tag + "proj_dw", sv["h"], dproj, "tn", F32)
    (dx,), (g["norm_mix"], g["sc1"], g["sh1"]) = rowstage_bwd(
        tag + "norm1", normmod_fn, [(sv["x"], d, 0)], [lp["norm_mix"], lp["sc1"], lp["sh1"]], [dh], [F32], tm,
        adds={0: dx_a})
    if own:
        return dx, g, arrived, [got_wb, got_wout, got_wf, got_wfo]
    return dx, g, arrived


WEIGHTS = ("w_ada", "b_ada", "norm_mix", "norm_ffn", "w_in", "b_merge", "hgrn_lb_logits", "hgrn_norm", "ssm_conv_w",
           "ssm_conv_b", "ssm_dt_bias", "ssm_a_log", "ssm_d", "ssm_norm", "gdn_conv_w", "gdn_dt_bias", "gdn_a_log",
           "gdn_norm", "w_branch", "w_out", "w_ffn_in", "w_ffn_out", "norm_final")
GATHERED = ("w_in", "w_branch", "w_out", "w_ffn_in", "w_ffn_out")
PACKET = ("b_ada", "norm_mix", "norm_ffn", "b_merge", "hgrn_norm", "ssm_conv_b", "ssm_dt_bias", "ssm_a_log", "ssm_d",
          "ssm_norm", "gdn_dt_bias", "gdn_a_log", "gdn_norm", "norm_final")
MISC = ("hgrn_lb_logits", "ssm_conv_w", "gdn_conv_w")


def _pack(arrs, dtype, row_mult, lead=0):
    flat = jnp.concatenate([t.reshape(t.shape[:lead] + (-1,)).astype(dtype) for t in arrs], axis=lead)
    n = flat.shape[-1]
    unit = row_mult * LANES
    tot = -(-n // unit) * unit
    flat = jnp.pad(flat, [(0, 0)] * lead + [(0, tot - n)])
    return flat.reshape(flat.shape[:lead] + (tot // LANES, LANES))


def _unpack(packed, shapes, lead=0):
    flat = packed.reshape(packed.shape[:lead] + (-1,))
    out, off = [], 0
    for shp in shapes:
        n = int(np.prod(shp))
        out.append(flat[..., off:off + n].reshape(flat.shape[:lead] + tuple(shp)))
        off += n
    return out


def _shard2d(t):
    return t.reshape((-1, t.shape[-1]))


def weights_from_shards(dm, l, got, idx):
    w_in, wb, w_out, wf, wfo = got
    d, mix = dm.d, dm.mix
    return dict(
        w_in=colgather(f"l{l}_w_in", w_in, idx, dm.np, BF16)[0],
        w_branch=wb.reshape(N_DEV, 3, mix, d // N_DEV).transpose(1, 2, 0, 3).reshape(3, mix, d),
        w_out=w_out.reshape(d, d), w_ffn_in=wf, w_ffn_out=wfo.reshape(N_DEV // 2, -1, d))


def small_shards(dm, g):
    d, mix = dm.d, dm.mix
    return [g["w_branch"].reshape(3, mix, N_DEV, d // N_DEV).transpose(2, 0, 1, 3).reshape(N_DEV, 3 * mix, d // N_DEV),
            g["w_out"].reshape(N_DEV, d // N_DEV, d), g["w_ffn_in"], g["w_ffn_out"].reshape(N_DEV, -1, d)]


def w_in_shards(dm, l, g, idx):
    return colgather(f"l{l}_g_w_in", g["w_in"][None], idx, dm.in_width // N_DEV, F32)


def layer_params(dm, l, full, small, mod_l, lb_l):
    d, mix = dm.d, dm.mix
    row = lambda t: t.reshape(1, -1)
    head = lambda t: t.reshape(-1, 1, 1)
    sh1, sc1, g1, sh2, sc2, g2 = (row(mod_l[i * d:(i + 1) * d]) for i in range(6))
    conv_b = jnp.concatenate([jnp.zeros((3 * mix,), F32), small["ssm_conv_b"][l]])
    return dict(
        w_in=full["w_in"], w_branch=full["w_branch"], w_out=full["w_out"],
        w_ffn_in=full["w_ffn_in"], w_ffn_out=full["w_ffn_out"],
        norm_mix=row(small["norm_mix"][l]), norm_ffn=row(small["norm_ffn"][l]), b_merge=row(small["b_merge"][l]),
        hgrn_norm=row(small["hgrn_norm"][l]), lb=row(lb_l),
        conv_w=jnp.concatenate([small["gdn_conv_w"][l], small["ssm_conv_w"][l]], axis=1), conv_b=row(conv_b),
        ssm_dt_bias=head(small["ssm_dt_bias"][l]), ssm_a_log=head(small["ssm_a_log"][l]),
        ssm_d_exp=row(jnp.repeat(small["ssm_d"][l], SSM_P)), ssm_norm=row(small["ssm_norm"][l]),
        gdn_dt_bias=head(small["gdn_dt_bias"][l]), gdn_a_log=head(small["gdn_a_log"][l]), gdn_norm=row(small["gdn_norm"][l]),
        sh1=sh1, sc1=sc1, g1=g1, sh2=sh2, sc2=sc2, g2=g2)


def layer_grads(dm, g):
    cs = 3 * dm.mix
    out = dict(
        w_in=g["w_in"], w_branch=g["w_branch"], w_out=g["w_out"], w_ffn_in=g["w_ffn_in"],
        w_ffn_out=g["w_ffn_out"], norm_mix=g["norm_mix"][0], norm_ffn=g["norm_ffn"][0], b_merge=g["b_merge"][0],
        hgrn_norm=g["hgrn_norm"][0], ssm_conv_w=g["conv_w"][:, cs:], gdn_conv_w=g["conv_w"][:, :cs],
        ssm_conv_b=g["conv_b"][0, cs:], ssm_dt_bias=g["ssm_dt_bias"][:, 0, 0], ssm_a_log=g["ssm_a_log"][:, 0, 0],
        ssm_d=g["ssm_d_exp"].reshape(dm.ssm_heads, SSM_P).sum(axis=1), ssm_norm=g["ssm_norm"][0],
        gdn_dt_bias=g["gdn_dt_bias"][:, 0, 0], gdn_a_log=g["gdn_a_log"][:, 0, 0], gdn_norm=g["gdn_norm"][0])
    dmod = jnp.concatenate([g[k][0] for k in ("sh1", "sc1", "g1", "sh2", "sc2", "g2")])
    return out, dmod, g["lb"][0]


def local_step(dm, nl, x, tgt, norm_final, params_of, gather_of=None, scatter_of=None):
    arrived = exchange("gather_w0", gather_of(0), True) if gather_of else None
    lps, saved = [], []
    for l in range(nl):
        lps.append(params_of(l, arrived))
        side = (gather_of(l + 1), True) if gather_of and l + 1 < nl else None
        x, sv, arrived = layer_fwd(dm, l, x, lps[l], side)
        saved.append(sv)
    loss, dx, dnf = loss_call("loss", x, tgt, norm_final, dm.tm)
    grads, parts, side = [None] * nl, [None] * nl, None
    for l in reversed(range(nl)):
        if scatter_of and l == 0:
            dx, grads[l], got, own = layer_bwd(dm, l, dx, lps[l], saved[l], side, own=True)
            parts[0] = list(exchange("scatter_g0", [scatter_of(0, grads[0])], False)) + own
        else:
            dx, grads[l], got = layer_bwd(dm, l, dx, lps[l], saved[l], side)
        if side is not None:
            parts[l + 1] = got
        side = ([scatter_of(l, grads[l])] + small_shards(dm, grads[l]), False) if scatter_of and l > 0 else None
    return loss, dx, dnf, grads, parts


def kernel(x, c, w_ada, b_ada, norm_mix, norm_ffn, w_in, b_merge, hgrn_lb_logits, hgrn_norm, ssm_conv_w, ssm_conv_b, ssm_dt_bias, ssm_a_log, ssm_d, ssm_norm, gdn_conv_w, gdn_dt_bias, gdn_a_log, gdn_norm, w_branch, w_out, w_ffn_in, w_ffn_out, norm_final, loss_target, m_w_ada, m_b_ada, m_norm_mix, m_norm_ffn, m_w_in, m_b_merge, m_hgrn_lb_logits, m_hgrn_norm, m_ssm_conv_w, m_ssm_conv_b, m_ssm_dt_bias, m_ssm_a_log, m_ssm_d, m_ssm_norm, m_gdn_conv_w, m_gdn_dt_bias, m_gdn_a_log, m_gdn_norm, m_w_branch, m_w_out, m_w_ffn_in, m_w_ffn_out, m_norm_final, v_w_ada, v_b_ada, v_norm_mix, v_norm_ffn, v_w_in, v_b_merge, v_hgrn_lb_logits, v_hgrn_norm, v_ssm_conv_w, v_ssm_conv_b, v_ssm_dt_bias, v_ssm_a_log, v_ssm_d, v_ssm_norm, v_gdn_conv_w, v_gdn_dt_bias, v_gdn_a_log, v_gdn_norm, v_w_branch, v_w_out, v_w_ffn_in, v_w_ffn_out, v_norm_final):
    a = dict(locals())
    x, tgt = a["x"][0], a["loss_target"][0]
    s, d = x.shape
    nl = a["w_ada"].shape[0]
    dm = Dims(s, d, a["w_ffn_out"].shape[1] * N_DEV)
    me = 4 * lax.axis_index("x") + 2 * lax.axis_index("y") + lax.axis_index("c")

    first = [a["c"], a["ssm_conv_w"], a["gdn_conv_w"]]
    c_all, scw, gcw = _unpack(exchange("gather_c", [_pack(first, F32, 8)], True)[0], [t.shape for t in first], lead=1)
    small = dict(a, ssm_conv_w=scw.transpose(1, 2, 0, 3).reshape(scw.shape[1:3] + (-1,)),
                 gdn_conv_w=gcw.transpose(1, 2, 0, 3).reshape(gcw.shape[1:3] + (-1,)))
    c_pad = jnp.zeros((LANES, d), F32).at[:N_DEV].set(c_all.reshape(N_DEV, d))
    ncol = a["w_ada"].shape[2]
    b_mine = lax.dynamic_slice(a["b_ada"], (0, me * ncol), (nl, ncol))[:, None, :]
    mod_part = ada_fwd("ada_fwd", c_pad, a["w_ada"], b_mine)[:, :N_DEV, :]
    (mod,) = exchange("a2a_mod", [mod_part.transpose(1, 0, 2)], False)
    mod = mod.transpose(1, 0, 2).reshape(nl, N_DEV * ncol)
    (lb,) = rowstage_fwd("lower_bounds", lower_bounds_fn, [(a["hgrn_lb_logits"], dm.mix, 0)], [], [(dm.mix, F32)], nl)

    idx_fwd, idx_bwd = w_in_tables(dm, N_DEV)
    loss, dx, dnf, grads, parts = local_step(
        dm, nl, x, tgt, a["norm_final"].reshape(1, d),
        params_of=lambda l, got: layer_params(dm, l, weights_from_shards(dm, l, got, idx_fwd), small, mod[l], lb[l]),
        gather_of=lambda l: [_shard2d(a[n][l]).astype(BF16) for n in GATHERED],
        scatter_of=lambda l, g: w_in_shards(dm, l, g, idx_bwd))

    per_layer = [layer_grads(dm, g) for g in grads]
    res = {}
    for i, n in enumerate(GATHERED):
        wmv = [a[q + n].reshape((nl, -1, a[n].shape[-1])) for q in ("", "m_", "v_")]
        outs = None
        for l in range(nl):
            outs = adamw_sum(f"adamw_l{l}_{n}", parts[l][i], *wmv, layer=l, into=outs)
        for kind, o in zip(("grad", "delta", "new_m", "new_v"), outs):
            res[(kind, n)] = o.reshape(a[n].shape)

    stackg = lambda n: jnp.stack([pl_[0][n] for pl_ in per_layer])
    dmod = jnp.stack([pl_[1] for pl_ in per_layer])
    dlb = jnp.stack([pl_[2] for pl_ in per_layer])
    pk_g = [dmod if n == "b_ada" else dnf if n == "norm_final" else stackg(n) for n in PACKET]
    extra = [dlb, stackg("ssm_conv_w"), stackg("gdn_conv_w"), loss[0, :1]]
    pk_shapes = [t.shape for t in pk_g + extra]
    zeros = [jnp.zeros(t.shape, F32) for t in extra]
    (parts,) = exchange("gather_small", [_pack(pk_g + extra, F32, 8)], True)
    outs = adamw_sum("adamw_small", parts, *[_pack([a[p + n] for n in PACKET] + zeros, F32, 8) for p in ("", "m_", "v_")])
    for kind, o in zip(("grad", "delta", "new_m", "new_v"), outs):
        un = _unpack(o, pk_shapes)
        for n, t in zip(PACKET, un):
            res[(kind, n)] = t.reshape(a[n].shape)
        if kind == "grad":
            dlb_sum, g_scw, g_gcw, loss_sum = un[len(PACKET):]

    (g_lb,), _ = rowstage_bwd("lower_bounds_b", lower_bounds_fn, [(a["hgrn_lb_logits"], dm.mix, 0)], [], [dlb_sum], [F32], nl)
    mine = lambda t, n: lax.dynamic_slice_in_dim(t, me * a[n].shape[-1], a[n].shape[-1], axis=t.ndim - 1)
    (dmod_cols,) = exchange("a2a_dmod", [dmod.reshape(nl, N_DEV, ncol).transpose(1, 0, 2)], False)
    dmod_pad = jnp.zeros((nl, LANES, ncol), F32).at[:, :N_DEV].set(dmod_cols.transpose(1, 0, 2))
    g_w_ada = ada_bwd("ada_bwd", c_pad, dmod_pad)
    outs = adamw_sum("adamw_w_ada", g_w_ada.reshape(1, nl * d, ncol), *[a[q + "w_ada"].reshape(nl * d, ncol) for q in ("", "m_", "v_")])
    for kind, o in zip(("grad", "delta", "new_m", "new_v"), outs):
        res[(kind, "w_ada")] = o.reshape(nl, d, ncol)
    g_misc = [g_lb, mine(g_scw, "ssm_conv_w"), mine(g_gcw, "gdn_conv_w")]
    outs = adamw_sum("adamw_misc", _pack(g_misc, F32, 8)[None], *[_pack([a[q + n] for n in MISC], F32, 8) for q in ("", "m_", "v_")])
    for kind, o in zip(("grad", "delta", "new_m", "new_v"), outs):
        for n, t in zip(MISC, _unpack(o, [a[n].shape for n in MISC])):
            res[(kind, n)] = t

    out = [loss_sum.reshape(()), dx[None]]
    for kind in ("grad", "delta", "new_m", "new_v"):
        out += [res[(kind, n)] for n in WEIGHTS]
    return tuple(out)
```

```python
import functools
import math

import numpy as np
import jax
import jax.numpy as jnp
from jax import lax
from jax.experimental import pallas as pl
from jax.experimental.pallas import tpu as pltpu

F32 = jnp.float32
BF16 = jnp.bfloat16

N_DEV = 8
CHUNK = 64
SUB = 8
HGRN_HEADS_PER_STEP = 6
GDN_HEADS_PER_STEP = 6
HEAD = 128
SSM_P = 64
CONV_K = 4
F_MIN = 1e-30
NORM_EPS = 1e-6
LANES = 128
GATHER_TILE = 256
VMEM_LIMIT = 56 * 1024 * 1024

ADAM_LR = 0.001
ADAM_B1 = 0.9
ADAM_B2 = 0.999
ADAM_EPS = 1e-08
ADAM_WD = 0.01
ADAM_STEP = 10


def _dg(a, b, ca, cb):
    return lax.dot_general(a.astype(BF16), b.astype(BF16), (((ca,), (cb,)), ((), ())),
                           preferred_element_type=F32)


def _split3(x):
    x1 = x.astype(BF16)
    r = x - x1.astype(F32)
    x2 = r.astype(BF16)
    x3 = (r - x2.astype(F32)).astype(BF16)
    return x1, x2, x3


def _hdg(a, b, ca, cb):
    a1, a2, _ = _split3(a)
    b1, b2, _ = _split3(b)
    dn = (((ca,), (cb,)), ((), ()))
    d = lambda p, q: lax.dot_general(p, q, dn, preferred_element_type=F32)
    return (d(a2, b1) + d(a1, b2)) + d(a1, b1)


def _dot_family(prim):
    @jax.custom_vjp
    def nn(a, b):
        return prim(a, b, 1, 0)

    @jax.custom_vjp
    def nt(a, b):
        return prim(a, b, 1, 1)

    @jax.custom_vjp
    def tn(a, b):
        return prim(a, b, 0, 0)

    nn.defvjp(lambda a, b: (nn(a, b), (a, b)), lambda r, g: (nt(g, r[1]), tn(r[0], g)))
    nt.defvjp(lambda a, b: (nt(a, b), (a, b)), lambda r, g: (nn(g, r[1]), tn(g, r[0])))
    tn.defvjp(lambda a, b: (tn(a, b), (a, b)), lambda r, g: (nt(r[1], g), nn(r[0], g)))
    return nn, nt, tn


mm_nn, mm_nt, mm_tn = _dot_family(_dg)
hd_nn, hd_nt, hd_tn = _dot_family(_hdg)


def _iota(shape, dim):
    return lax.broadcasted_iota(jnp.int32, shape, dim)


def _scan_rows(x, reverse):
    n = x.shape[0]
    rows = _iota(x.shape, 0)
    k = 1
    while k < n:
        if reverse:
            x = x + jnp.where(rows < n - k, pltpu.roll(x, n - k, 0), 0.0)
        else:
            x = x + jnp.where(rows >= k, pltpu.roll(x, k, 0), 0.0)
        k *= 2
    return x


@jax.custom_vjp
def cumsum_rows(x):
    return _scan_rows(x, False)


cumsum_rows.defvjp(lambda x: (_scan_rows(x, False), None), lambda _, g: (_scan_rows(g, True),))


def _sigmoid(x):
    return jax.nn.sigmoid(x)


def _silu(x):
    return x * jax.nn.sigmoid(x)


def _softplus(x):
    e = jnp.exp(-jnp.abs(x))
    small = e * (1.0 - e * (0.5 - e * (1.0 / 3.0)))
    return jnp.maximum(x, 0.0) + jnp.where(e < 1e-3, small, jnp.log(1.0 + e))


def _masked_exp(diff, mask):
    return jnp.where(mask, jnp.exp(jnp.where(mask, diff, 0.0)), 0.0)


def _rms(x, w):
    return x * lax.rsqrt(jnp.mean(x * x, axis=-1, keepdims=True) + NORM_EPS) * w


def _cum_col_row(lg_col, lg_row):
    c = lg_col.shape[0]
    r, s = _iota((c, c), 0), _iota((c, c), 1)
    cum_col = jnp.sum(jnp.where(s <= r, jnp.broadcast_to(lg_row, (c, c)), 0.0), axis=1, keepdims=True)
    cum_row = jnp.sum(jnp.where(r <= s, jnp.broadcast_to(lg_col, (c, c)), 0.0), axis=0, keepdims=True)
    total = jnp.sum(lg_col, axis=0, keepdims=True)
    return cum_col, cum_row, total


def hgrn_chunk(seq, hp, sp, st):
    (blk,), (lb,), (nw,) = seq, hp, sp
    c = blk.shape[0]
    q_raw, f_raw, v, g_raw = (blk[:, i * HEAD:(i + 1) * HEAD] for i in range(4))
    q = _silu(q_raw)
    f = lb + (1.0 - lb) * _sigmoid(f_raw)
    logf = jnp.log(jnp.maximum(f, F_MIN))
    k = (1.0 - lb) * _sigmoid(-f_raw)
    b = cumsum_rows(logf)
    o_inter = mm_nt(q * jnp.exp(b), st)
    nsub = c // SUB
    wide = (SUB, SUB, HEAD)
    er = _iota((SUB * SUB, SUB), 0)
    e_t = (er // SUB == _iota((SUB * SUB, SUB), 1)).astype(F32)
    pr = _iota((SUB * SUB, 1), 0)
    pmask = (pr % SUB) <= (pr // SUB)
    er64 = _iota((SUB * SUB, c), 0)
    ec64 = _iota((SUB * SUB, c), 1)
    rows_c = _iota((c, 1), 0)
    row = lambda a, i: jnp.sum(jnp.where(rows_c == i, a, 0.0), axis=0, keepdims=True)
    def sub_chunk(qi, ki, bi, bref, first, place):
        qb = jnp.broadcast_to(qi[:, None, :], wide).reshape(SUB * SUB, HEAD)
        kb = jnp.broadcast_to(ki[None, :, :], wide).reshape(SUB * SUB, HEAD)
        bd = (bi[:, None, :] - bi[None, :, :]).reshape(SUB * SUB, HEAD)
        sc_col = jnp.sum(qb * kb * _masked_exp(bd, pmask), axis=1, keepdims=True)
        sc = mm_tn(e_t, sc_col * place)
        sc = sc + mm_nt(qi * jnp.exp(bi - bref), k * _masked_exp(bref - b, rows_c < first))
        return mm_nn(sc, v)

    firsts = [SUB * i for i in range(nsub)]
    pile = lambda parts: jnp.concatenate([p[None] for p in parts], axis=0)
    cut = lambda a: a.reshape(nsub, SUB, HEAD)
    brefs = pile([row(b, f) for f in firsts])
    starts = pile([jnp.full((1, 1), f, jnp.int32) for f in firsts])
    places = pile([(ec64 == (er64 % SUB) + f).astype(F32) for f in firsts])
    o_intra = jax.vmap(sub_chunk)(cut(q), cut(k), cut(b), brefs, starts, places)
    o = o_inter + o_intra.reshape(c, HEAD)
    bend = row(b, c - 1)
    st_new = st * jnp.exp(bend) + mm_tn(v, k * jnp.exp(bend - b))
    y = _rms(o, nw) * _silu(g_raw)
    return (y,), st_new


def ssd_chunk(seq, hp, sp, st):
    xs, bm, cm, dtc, dtr = seq
    dt_bias, a_log = hp
    c, width = xs.shape
    nheads = width // SSM_P
    head_of = _iota((1, width), 1) // SSM_P
    r, s = _iota((c, c), 0), _iota((c, c), 1)
    g = mm_nt(cm, bm)
    dt_l, cum_l, end_l, scores = 0.0, 0.0, 0.0, []
    for i in range(nheads):
        neg_a = -jnp.exp(a_log[i])
        dt_col = _softplus(dtc[i] + dt_bias[i])
        dt_row = _softplus(dtr[i] + dt_bias[i])
        cum_col, cum_row, total = _cum_col_row(neg_a * dt_col, neg_a * dt_row)
        mine = head_of == i
        dt_l = dt_l + jnp.where(mine, dt_col, 0.0)
        cum_l = cum_l + jnp.where(mine, cum_col, 0.0)
        end_l = end_l + jnp.where(mine, total, 0.0)
        scores.append(g * _masked_exp(cum_col - cum_row, s <= r))
    xdt = xs * dt_l
    stacked = mm_nn(jnp.concatenate(scores, axis=0), xdt)
    y_intra = 0.0
    for i in range(nheads):
        y_intra = y_intra + jnp.where(head_of == i, stacked[i * c:(i + 1) * c], 0.0)
    y_inter = mm_nn(cm, st) * jnp.exp(cum_l)
    st_new = st * jnp.exp(end_l) + mm_tn(bm, xdt * jnp.exp(end_l - cum_l))
    return (y_intra + y_inter,), st_new


def _neumann_inverse(a):
    n = a.shape[0]
    eye = (_iota((n, n), 0) == _iota((n, n), 1)).astype(F32)
    p = -a
    t = eye + p
    for _ in range(int(math.log2(n)) - 1):
        p = _hdg(p, p, 1, 0)
        t = t + _hdg(t, p, 1, 0)
    return t


@jax.custom_vjp
def inv_unit_lower(a):
    return _neumann_inverse(a)


def _inv_fwd(a):
    t = _neumann_inverse(a)
    return t, t


inv_unit_lower.defvjp(_inv_fwd, lambda t, g: (-hd_nt(hd_tn(t, g), t),))


def gdn_chunk(seq, hp, sp, st):
    q_raw, k_raw, v, z, gbc, gac, gar = seq
    dt_bias, a_log = hp
    (nw,) = sp
    c = v.shape[0]
    r, s = _iota((c, c), 0), _iota((c, c), 1)
    q = q_raw * lax.rsqrt(jnp.sum(q_raw * q_raw, axis=-1, keepdims=True) + NORM_EPS) * (HEAD ** -0.5)
    k = k_raw * lax.rsqrt(jnp.sum(k_raw * k_raw, axis=-1, keepdims=True) + NORM_EPS)
    beta = _sigmoid(gbc)
    neg_a = -jnp.exp(a_log)
    cum, cum_row, total = _cum_col_row(neg_a * _softplus(gac + dt_bias), neg_a * _softplus(gar + dt_bias))
    decay = _masked_exp(cum - cum_row, s <= r)
    kk = mm_nt(k, k)
    a_low = jnp.where(s < r, beta * kk * decay, 0.0)
    sol = hd_nn(inv_unit_lower(a_low), jnp.concatenate([v * beta, k * (beta * jnp.exp(cum))], axis=1))
    u_base, w_corr = sol[:, :HEAD], sol[:, HEAD:]
    qk = mm_nt(q, k) * decay
    u = u_base - mm_nn(w_corr, st)
    o = mm_nn(q * jnp.exp(cum), st) + mm_nn(qk, u)
    st_new = jnp.exp(total) * st + mm_tn(k * jnp.exp(total - cum), u)
    y = _rms(o, nw) * _silu(z)
    return (y,), st_new


def normmod_fn(rows, params):
    (x,), (nw, sc, sh) = rows, params
    return (_rms(x, nw) * (1.0 + sc) + sh,)


def ssmpost_fn(rows, params):
    (y, xs, z), (d_exp, nw) = rows, params
    y = (y + d_exp * xs) * _silu(z)
    gw = y.shape[1] // 2
    return (jnp.concatenate([_rms(y[:, :gw], nw[:, :gw]), _rms(y[:, gw:], nw[:, gw:])], axis=1),)


def merge_fn(rows, params):
    (yh, ys, yg, gl), (bm, wb) = rows, params
    d = wb.shape[2]
    gates = _sigmoid(gl + bm)
    out = 0.0
    for n, y in enumerate((yh, ys, yg)):
        out = out + gates[:, n * d:(n + 1) * d] * mm_nn(y, wb[n])
    return (out,)


def outproj_fn(rows, params):
    (m, x), (g1, w) = rows, params
    return (x + (1.0 + g1) * mm_nn(m, w),)


def resid_fn(rows, params):
    (x, o), (g2,) = rows, params
    return (x + (1.0 + g2) * o,)


def _params(sem, side_effects=False):
    return pltpu.CompilerParams(dimension_semantics=sem, vmem_limit_bytes=VMEM_LIMIT, has_side_effects=side_effects)


def _whole(a):
    nd = a.ndim
    return pl.BlockSpec(a.shape, lambda *_: (0,) * nd)


def _pick(n, cands):
    for c in cands:
        if n % c == 0:
            return c
    return n


def matmul(name, a, b, mode, out_dtype, side=None):
    if mode == "nn":
        (m, k), n = a.shape, b.shape[1]
    elif mode == "nt":
        (m, k), n = a.shape, b.shape[0]
    else:
        (k, m), n = a.shape, b.shape[1]
    tm = _pick(m, (512, 256, 128))
    tn = _pick(n, (1280, 1024, 1408, 768, 512, 384, 256, 128))
    tk = _pick(k, (1024, 1280, 1408, 768, 512, 256, 128))
    if mode == "tn":
        tm = _pick(m, (1024, 768, 512, 256, 128))
        tk = _pick(k, (512, 256, 128))
    nk = k // tk
    ca, cb = {"nn": (1, 0), "nt": (1, 1), "tn": (0, 0)}[mode]

    def core(a_ref, b_ref, o_ref, acc_ref):
        kk = pl.program_id(2)

        @pl.when(kk == 0)
        def _():
            acc_ref[...] = jnp.zeros_like(acc_ref)

        acc_ref[...] += _dg(a_ref[...], b_ref[...], ca, cb)

        @pl.when(kk == nk - 1)
        def _():
            o_ref[...] = acc_ref[...].astype(o_ref.dtype)

    a_spec = (pl.BlockSpec((tk, tm), lambda i, j, q: (q, i)) if mode == "tn"
              else pl.BlockSpec((tm, tk), lambda i, j, q: (i, q)))
    b_spec = (pl.BlockSpec((tn, tk), lambda i, j, q: (j, q)) if mode == "nt"
              else pl.BlockSpec((tk, tn), lambda i, j, q: (q, j)))
    grid = (m // tm, n // tn, nk)
    body, s_in, s_out, s_shape, s_scr, s_args = _with_side(core, 2, 1, side, grid)
    sem = ("arbitrary",) * 3 if side else ("parallel", "parallel", "arbitrary")
    res = pl.pallas_call(
        body, name=name, grid=grid,
        in_specs=[a_spec, b_spec] + s_in,
        out_specs=[pl.BlockSpec((tm, tn), lambda i, j, q: (i, j))] + s_out,
        out_shape=[jax.ShapeDtypeStruct((m, n), out_dtype)] + s_shape,
        scratch_shapes=[pltpu.VMEM((tm, tn), F32)] + s_scr,
        compiler_params=_params(sem, side is not None),
    )(a, b, *s_args)
    return (res[0], res[1:]) if side else res[0]


def bmatmul(name, a, b, mode, out_dtype, out_batched):
    ab, bb = a.ndim == 3, b.ndim == 3
    nb = a.shape[0] if ab else b.shape[0]
    a2, b2 = a.shape[-2:], b.shape[-2:]
    if mode == "nn":
        (m, k), n = a2, b2[1]
    elif mode == "nt":
        (m, k), n = a2, b2[0]
    else:
        (k, m), n = a2, b2[1]
    tm = _pick(m, (1024, 512, 256, 128) if mode == "tn" else (512, 256, 128))
    tn = _pick(n, (1024, 512, 256, 128))
    tk = _pick(k, (512, 256, 128) if mode == "tn" else (1024, 512, 256, 128))
    nk = k // tk
    ca, cb = {"nn": (1, 0), "nt": (1, 1), "tn": (0, 0)}[mode]
    ids = (lambda g: g) if out_batched else (lambda g: (g[2], g[0], g[1], g[3]))
    grid = (nb, m // tm, n // tn, nk) if out_batched else (m // tm, n // tn, nb, nk)

    def a_map(*g):
        bi, i, j, q = ids(g)
        idx = (q, i) if mode == "tn" else (i, q)
        return (bi,) + idx if ab else idx

    def b_map(*g):
        bi, i, j, q = ids(g)
        idx = (j, q) if mode == "nt" else (q, j)
        return (bi,) + idx if bb else idx

    def o_map(*g):
        bi, i, j, q = ids(g)
        return (bi, i, j) if out_batched else (i, j)

    def body(a_ref, b_ref, o_ref, acc_ref):
        bi, _, _, q = ids(tuple(pl.program_id(d) for d in range(4)))
        first = (q == 0) if out_batched else (q == 0) & (bi == 0)
        last = (q == nk - 1) if out_batched else (q == nk - 1) & (bi == nb - 1)

        @pl.when(first)
        def _():
            acc_ref[...] = jnp.zeros_like(acc_ref)

        acc_ref[...] += _dg(a_ref[...], b_ref[...], ca, cb)

        @pl.when(last)
        def _():
            o_ref[...] = acc_ref[...].astype(o_ref.dtype)

    a_blk = (tk, tm) if mode == "tn" else (tm, tk)
    b_blk = (tn, tk) if mode == "nt" else (tk, tn)
    return pl.pallas_call(
        body, name=name, grid=grid,
        in_specs=[pl.BlockSpec(((None,) if ab else ()) + a_blk, a_map), pl.BlockSpec(((None,) if bb else ()) + b_blk, b_map)],
        out_specs=pl.BlockSpec(((None,) if out_batched else ()) + (tm, tn), o_map),
        out_shape=jax.ShapeDtypeStruct(((nb,) if out_batched else ()) + (m, n), out_dtype),
        scratch_shapes=[pltpu.VMEM((tm, tn), F32)],
        compiler_params=_params(("parallel", "parallel", "arbitrary", "arbitrary")),
    )(a, b)


def colgather(name, src, idx, dst_w, out_dtype):
    nsrc, rows, w = src.shape
    tw = GATHER_TILE
    nbs = -(-w // tw)
    ne = idx.shape[0]
    nbd = idx.shape[1] // tw
    tiles = [sorted(set((idx[e, t * tw:(t + 1) * tw][idx[e, t * tw:(t + 1) * tw] >= 0] // tw).tolist()))
             for e in range(ne) for t in range(nbd)]
    nslot = max(1, max(len(t) for t in tiles))
    tbl = np.full((ne * nbd, nslot), -1, np.int32)
    for i, t in enumerate(tiles):
        tbl[i, :len(t)] = t
    exact3 = src.dtype == F32

    def body(tbl_ref, idx_ref, src_ref, o_ref, acc_ref):
        ti, si = pl.program_id(0), pl.program_id(1)

        @pl.when(si == 0)
        def _():
            acc_ref[...] = jnp.zeros_like(acc_ref)

        t = tbl_ref[ti * nslot + si]

        @pl.when(t >= 0)
        def _():
            onehot = ((_iota((tw, tw), 0) + t * tw) == idx_ref[...]).astype(BF16)
            col = _iota((1, tw), 1) + (t % nbs) * tw
            xv = jnp.where(col < w, src_ref[...], jnp.zeros((), src_ref.dtype))
            d = lambda p: lax.dot_general(p, onehot, (((1,), (0,)), ((), ())), preferred_element_type=F32)
            if exact3:
                x1, x2, x3 = _split3(xv)
                acc_ref[...] += (d(x3) + d(x2)) + d(x1)
            else:
                acc_ref[...] += d(xv)

        @pl.when(si == nslot - 1)
        def _():
            o_ref[...] = acc_ref[...].astype(o_ref.dtype)

    def src_map(ti, si, tbl_ref):
        t = jnp.maximum(tbl_ref[ti * nslot + si], 0)
        return (t // nbs, 0, t % nbs)

    grid_spec = pltpu.PrefetchScalarGridSpec(
        num_scalar_prefetch=1, grid=(ne * nbd, nslot),
        in_specs=[pl.BlockSpec((None, 1, tw), lambda ti, si, tbl_ref: (ti // nbd, 0, ti % nbd)),
                  pl.BlockSpec((None, rows, tw), src_map)],
        out_specs=pl.BlockSpec((None, rows, tw), lambda ti, si, tbl_ref: (ti // nbd, 0, ti % nbd)),
        scratch_shapes=[pltpu.VMEM((rows, tw), F32)])
    return pl.pallas_call(
        body, name=name, grid_spec=grid_spec,
        out_shape=jax.ShapeDtypeStruct((ne, rows, dst_w), out_dtype),
        compiler_params=_params(("parallel", "arbitrary")),
    )(jnp.asarray(tbl.reshape(-1)), jnp.asarray(idx.reshape(ne, 1, nbd * tw).astype(np.int32)), src)


def swiglu3_fwd(name, gu, tm):
    _, nb, s, w = gu.shape

    def body(x_ref, o_ref):
        o_ref[...] = (_silu(x_ref[0].astype(F32)) * x_ref[1].astype(F32)).astype(o_ref.dtype)

    return pl.pallas_call(
        body, name=name, grid=(nb, s // tm),
        in_specs=[pl.BlockSpec((2, None, tm, w), lambda b, i: (0, b, i, 0))],
        out_specs=pl.BlockSpec((None, tm, w), lambda b, i: (b, i, 0)),
        out_shape=jax.ShapeDtypeStruct((nb, s, w), BF16),
        compiler_params=_params(("parallel", "parallel")),
    )(gu)


def swiglu3_bwd(name, gu, dact, tm):
    _, nb, s, w = gu.shape

    def body(x_ref, g_ref, o_ref):
        _, vjp = jax.vjp(lambda a, b: _silu(a) * b, x_ref[0].astype(F32), x_ref[1].astype(F32))
        dg, du = vjp(g_ref[...].astype(F32))
        o_ref[0] = dg.astype(o_ref.dtype)
        o_ref[1] = du.astype(o_ref.dtype)

    return pl.pallas_call(
        body, name=name, grid=(nb, s // tm),
        in_specs=[pl.BlockSpec((2, None, tm, w), lambda b, i: (0, b, i, 0)),
                  pl.BlockSpec((None, tm, w), lambda b, i: (b, i, 0))],
        out_specs=pl.BlockSpec((2, None, tm, w), lambda b, i: (0, b, i, 0)),
        out_shape=jax.ShapeDtypeStruct(gu.shape, BF16),
        compiler_params=_params(("parallel", "parallel")),
    )(gu, dact)


def _row_specs(rows, tm):
    return [pl.BlockSpec((tm, w), lambda i, _c=c: (i, _c)) for (_, w, c) in rows]


def rowstage_fwd(name, fn, rows, params, outs, tm):
    s = rows[0][0].shape[0]
    nr, npar = len(rows), len(params)

    def body(*refs):
        r = [x[...].astype(F32) for x in refs[:nr]]
        p = [x[...].astype(F32) for x in refs[nr:nr + npar]]
        for ref, val in zip(refs[nr + npar:], fn(r, p)):
            ref[...] = val.astype(ref.dtype)

    res = pl.pallas_call(
        body, name=name, grid=(s // tm,),
        in_specs=_row_specs(rows, tm) + [_whole(p) for p in params],
        out_specs=[pl.BlockSpec((tm, w), lambda i: (i, 0)) for (w, _) in outs],
        out_shape=[jax.ShapeDtypeStruct((s, w), dt) for (w, dt) in outs],
        compiler_params=_params(("parallel",)),
    )(*[r[0] for r in rows], *params)
    return res


def rowstage_bwd(name, fn, rows, params, douts, drow_dtypes, tm, adds=None, into=None):
    s = rows[0][0].shape[0]
    nr, npar, no = len(rows), len(params), len(douts)
    adds = adds or {}
    add_idx = sorted(adds)
    na = len(add_idx)
    into = into or {}
    into_idx = sorted(into)
    nb = len(into_idx)

    def body(*refs):
        r = [x[...].astype(F32) for x in refs[:nr]]
        p = [x[...].astype(F32) for x in refs[nr:nr + npar]]
        g = [x[...].astype(F32) for x in refs[nr + npar:nr + npar + no]]
        a_refs = refs[nr + npar + no:nr + npar + no + na]
        dr_refs = refs[nr + npar + no + na + nb:nr + npar + no + na + nb + nr]
        dp_refs = refs[nr + npar + no + na + nb + nr:]
        _, vjp = jax.vjp(lambda r_, p_: tuple(fn(r_, p_)), r, p)
        dr, dp = vjp(tuple(g))
        for j, (ref, val) in enumerate(zip(dr_refs, dr)):
            if j in adds:
                val = val + a_refs[add_idx.index(j)][...].astype(F32)
            ref[...] = val.astype(ref.dtype)

        @pl.when(pl.program_id(0) == 0)
        def _():
            for ref in dp_refs:
                ref[...] = jnp.zeros_like(ref)

        for ref, val in zip(dp_refs, dp):
            ref[...] += val

    res = pl.pallas_call(
        body, name=name, grid=(s // tm,),
        in_specs=(_row_specs(rows, tm) + [_whole(p) for p in params]
                  + [pl.BlockSpec((tm, d.shape[1]), lambda i: (i, 0)) for d in douts]
                  + [pl.BlockSpec((tm, rows[j][1]), lambda i: (i, 0)) for j in add_idx]
                  + [pl.BlockSpec(memory_space=pl.ANY)] * nb),
        out_specs=([pl.BlockSpec((tm, w), lambda i, _c=(into[j][1] if j in into else 0): (i, _c))
                    for j, (_, w, _) in enumerate(rows)] + [_whole(p) for p in params]),
        out_shape=([jax.ShapeDtypeStruct(into[j][0].shape if j in into else (s, w), dt)
                    for j, ((_, w, _), dt) in enumerate(zip(rows, drow_dtypes))]
                   + [jax.ShapeDtypeStruct(p.shape, F32) for p in params]),
        input_output_aliases={nr + npar + no + na + k: j for k, j in enumerate(into_idx)},
        compiler_params=_params(("arbitrary",)),
    )(*[r[0] for r in rows], *params, *douts, *[adds[j] for j in add_idx], *[into[j][0] for j in into_idx])
    return res[:nr], res[nr:]


def _flip(index_map, nc):
    return lambda h, n: index_map(h, nc - 1 - n)


def _with_side(core, n_in, n_out, side, grid):
    if side is None:
        return core, [], [], [], [], ()
    sends, broadcast = side
    k = len(sends)

    def body(*refs):
        ins, snd = refs[:n_in], refs[n_in:n_in + k]
        outs, rcv = refs[n_in + k:n_in + k + n_out], refs[n_in + k + n_out:n_in + 2 * k + n_out]
        scr = refs[n_in + 2 * k + n_out:]
        start, wait = _exchange_ops(snd, rcv, *scr[1:], broadcast)
        ids = [pl.program_id(d) for d in range(len(grid))]
        first = functools.reduce(lambda a, b: a & b, [i == 0 for i in ids])
        last = functools.reduce(lambda a, b: a & b, [i == g - 1 for i, g in zip(ids, grid)])
        pl.when(first)(start)
        core(*ins, *outs, scr[0])
        pl.when(last)(wait)

    return body, [HBM_SPEC] * k, [HBM_SPEC] * k, _exchange_out(sends, broadcast), _exchange_sems(k), tuple(sends)


def _take(v, split, j):
    if split is None:
        return v
    if split[0] == "lane":
        return v[:, j * split[1]:(j + 1) * split[1]]
    if split[0] == "lead":
        return v[j * split[1]:(j + 1) * split[1]]
    return v[j]


def _heads(vals, specs, hb):
    return [v if s[-1] is None else jnp.stack([_take(v, s[-1], j) for j in range(hb)]) for v, s in zip(vals, specs)]


def _over_heads(chunk_fn, hb, seqs, hparams, batched):
    seq_ax = [None if s[3] is None else 0 for s in seqs]
    hp_ax = [None if s[3] is None else 0 for s in hparams]
    if batched:
        return jax.vmap(chunk_fn, in_axes=(seq_ax, hp_ax, None, 0))

    def looped(seq, hp, sp, st):
        pick = lambda vals, axes, j: [v if a is None else v[j] for v, a in zip(vals, axes)]
        res = [chunk_fn(pick(seq, seq_ax, j), pick(hp, hp_ax, j), sp, st[j]) for j in range(hb)]
        pile = lambda parts: jnp.concatenate([p[None] for p in parts], axis=0)
        return tuple(pile(o) for o in zip(*[r[0] for r in res])), pile([r[1] for r in res])

    return looped


def _where(split, j):
    if split[0] == "lane":
        return (slice(None), slice(j * split[1], (j + 1) * split[1]))
    if split[0] == "lead":
        return (slice(j * split[1], (j + 1) * split[1]),)
    return (j,)


def scan_fwd(name, chunk_fn, nblk, hb, nc, seqs, hparams, sparams, state_shape, outs, batched, side=None):
    ns, nhp, nsp, no = len(seqs), len(hparams), len(sparams), len(outs)

    def core(*refs):
        seq_r, hp_r, sp_r = refs[:ns], refs[ns:ns + nhp], refs[ns + nhp:ns + nhp + nsp]
        out_r = refs[ns + nhp + nsp:ns + nhp + nsp + no]
        st_out, st_scr = refs[-2], refs[-1]

        @pl.when(pl.program_id(1) == 0)
        def _():
            st_scr[...] = jnp.zeros_like(st_scr)

        seq_v = [x[...].astype(F32) for x in seq_r]
        hp_v = [x[...] for x in hp_r]
        sp_v = [x[...] for x in sp_r]
        st = st_scr[...]
        st_out[...] = st
        heads = _over_heads(chunk_fn, hb, seqs, hparams, batched)
        o, st_new = heads(_heads(seq_v, seqs, hb), _heads(hp_v, hparams, hb), sp_v, st)
        for ref, spec, val in zip(out_r, outs, o):
            for j in range(hb):
                ref[_where(spec[4], j)] = val[j].astype(ref.dtype)
        st_scr[...] = st_new

    nst = len(state_shape)
    body, s_in, s_out, s_shape, s_scr, s_args = _with_side(core, ns + nhp + nsp, no + 1, side, (nblk, nc))
    res = pl.pallas_call(
        body, name=name, grid=(nblk, nc),
        in_specs=([pl.BlockSpec(bs, im) for (_, bs, im, _) in seqs]
                  + [pl.BlockSpec(bs, lambda h, n, _im=im: _im(h)) for (_, bs, im, _) in hparams]
                  + [_whole(p) for p in sparams] + s_in),
        out_specs=([pl.BlockSpec(bs, im) for (_, _, bs, im, _) in outs]
                   + [pl.BlockSpec((hb, None) + tuple(state_shape), lambda h, n: (h, n) + (0,) * nst)] + s_out),
        out_shape=([jax.ShapeDtypeStruct(fs, dt) for (fs, dt, _, _, _) in outs]
                   + [jax.ShapeDtypeStruct((nblk * hb, nc) + tuple(state_shape), F32)] + s_shape),
        scratch_shapes=[pltpu.VMEM((hb,) + tuple(state_shape), F32)] + s_scr,
        compiler_params=_params(("arbitrary", "arbitrary"), side is not None),
    )(*[x[0] for x in seqs], *[x[0] for x in hparams], *sparams, *s_args)
    return res[:no], res[no], res[no + 1:]


def scan_bwd(name, chunk_fn, nblk, hb, nc, seqs, hparams, sparams, state_shape, states, douts, dseqs, batched, side=None):
    ns, nhp, nsp, no = len(seqs), len(hparams), len(sparams), len(douts)
    nst = len(state_shape)
    buf_of = [i for i, sp in enumerate(dseqs) if len(sp) > 5 and sp[5] is not None]
    bufs = [dseqs[i][5] for i in buf_of]

    def core(*refs):
        seq_r, hp_r, sp_r = refs[:ns], refs[ns:ns + nhp], refs[ns + nhp:ns + nhp + nsp]
        base = ns + nhp + nsp
        st_r = refs[base]
        do_r = refs[base + 1:base + 1 + no]
        base += 1 + no + len(bufs)
        ds_r, dhp_r, dsp_r = refs[base:base + ns], refs[base + ns:base + ns + nhp], refs[base + ns + nhp:base + ns + nhp + nsp]
        dst_scr = refs[-1]
        h, n = pl.program_id(0), pl.program_id(1)

        @pl.when(n == 0)
        def _():
            dst_scr[...] = jnp.zeros_like(dst_scr)
            for ref in dhp_r:
                ref[...] = jnp.zeros_like(ref)

        @pl.when((n == 0) & (h == 0))
        def _():
            for ref in dsp_r:
                ref[...] = jnp.zeros_like(ref)

        seq_v = [x[...].astype(F32) for x in seq_r]
        hp_v = [x[...] for x in hp_r]
        sp_v = [x[...] for x in sp_r]
        do_v = [x[...].astype(F32) for x in do_r]
        prim = (_heads(seq_v, seqs, hb), _heads(hp_v, hparams, hb), sp_v, st_r[...])
        _, vjp = jax.vjp(_over_heads(chunk_fn, hb, seqs, hparams, batched), *prim)
        ds, dhp, dsp, dst = vjp((tuple(_heads(do_v, douts, hb)), dst_scr[...]))
        for ref, spec, val in zip(ds_r, dseqs, ds):
            if spec[4] is None:
                ref[...] = val.astype(ref.dtype)
            else:
                for j in range(hb):
                    ref[_where(spec[4], j)] = val[j].astype(ref.dtype)
        for ref, spec, val in zip(dhp_r, hparams, dhp):
            for j in range(hb):
                ref[_where(spec[3], j)] += val[j]
        for ref, val in zip(dsp_r, dsp):
            ref[...] += val
        dst_scr[...] = dst

    n_in, n_out = ns + nhp + nsp + 1 + no + len(bufs), ns + nhp + nsp
    body, s_in, s_out, s_shape, s_scr, s_args = _with_side(core, n_in, n_out, side, (nblk, nc))
    res = pl.pallas_call(
        body, name=name, grid=(nblk, nc),
        in_specs=([pl.BlockSpec(bs, _flip(im, nc)) for (_, bs, im, _) in seqs]
                  + [pl.BlockSpec(bs, lambda h, n, _im=im: _im(h)) for (_, bs, im, _) in hparams]
                  + [_whole(p) for p in sparams]
                  + [pl.BlockSpec((hb, None) + tuple(state_shape), lambda h, n: (h, nc - 1 - n) + (0,) * nst)]
                  + [pl.BlockSpec(bs, _flip(im, nc)) for (_, bs, im, _) in douts]
                  + [pl.BlockSpec(memory_space=pl.ANY)] * len(bufs) + s_in),
        out_specs=([pl.BlockSpec(sp[2], _flip(sp[3], nc)) for sp in dseqs]
                   + [pl.BlockSpec(bs, lambda h, n, _im=im: _im(h)) for (_, bs, im, _) in hparams]
                   + [_whole(p) for p in sparams] + s_out),
        out_shape=([jax.ShapeDtypeStruct(sp[0], sp[1]) for sp in dseqs]
                   + [jax.ShapeDtypeStruct(x[0].shape, F32) for x in hparams]
                   + [jax.ShapeDtypeStruct(p.shape, F32) for p in sparams] + s_shape),
        scratch_shapes=[pltpu.VMEM((hb,) + tuple(state_shape), F32)] + s_scr,
        input_output_aliases={n_in - len(bufs) + k: i for k, i in enumerate(buf_of)},
        compiler_params=_params(("arbitrary", "arbitrary"), side is not None),
    )(*[x[0] for x in seqs], *[x[0] for x in hparams], *sparams, states, *[x[0] for x in douts], *bufs, *s_args)
    return res[:ns], res[ns:ns + nhp], res[ns + nhp:n_out], res[n_out:]


def _shift_down(x, n, rows):
    if n == 0:
        return x
    return jnp.where(rows >= n, pltpu.roll(x, n, 0), 0.0)


def _shift_up(x, n, rows):
    if n == 0:
        return x
    s = x.shape[0]
    return jnp.where(rows < s - n, pltpu.roll(x, s - n, 0), 0.0)


def conv_fwd(name, x, col0, w, b):
    s, cw = x.shape[0], w.shape[1]

    def body(x_ref, w_ref, b_ref, o_ref):
        xv = x_ref[...]
        rows = _iota(xv.shape, 0)
        u = jnp.broadcast_to(b_ref[...], xv.shape)
        for j in range(CONV_K):
            u = u + w_ref[j:j + 1, :] * _shift_down(xv, CONV_K - 1 - j, rows)
        o_ref[...] = _silu(u)

    return pl.pallas_call(
        body, name=name, grid=(cw // LANES,),
        in_specs=[pl.BlockSpec((s, LANES), lambda j: (0, col0 + j)),
                  pl.BlockSpec((CONV_K, LANES), lambda j: (0, j)),
                  pl.BlockSpec((1, LANES), lambda j: (0, j))],
        out_specs=pl.BlockSpec((s, LANES), lambda j: (0, j)),
        out_shape=jax.ShapeDtypeStruct((s, cw), F32),
        compiler_params=_params(("parallel",)),
    )(x, w, b)


def conv_bwd(name, x, col0, w, b, dout, into):
    s, cw = x.shape[0], w.shape[1]

    def body(x_ref, w_ref, b_ref, g_ref, into_ref, dx_ref, dw_ref, db_ref):
        xv = x_ref[...]
        rows = _iota(xv.shape, 0)
        sh = [_shift_down(xv, CONV_K - 1 - j, rows) for j in range(CONV_K)]
        u = jnp.broadcast_to(b_ref[...], xv.shape)
        for j in range(CONV_K):
            u = u + w_ref[j:j + 1, :] * sh[j]
        sg = _sigmoid(u)
        du = g_ref[...] * (sg * (1.0 + u * (1.0 - sg)))
        dx = jnp.zeros_like(xv)
        for j in range(CONV_K):
            dx = dx + w_ref[j:j + 1, :] * _shift_up(du, CONV_K - 1 - j, rows)
            dw_ref[j:j + 1, :] = jnp.sum(du * sh[j], axis=0, keepdims=True)
        dx_ref[...] = dx.astype(dx_ref.dtype)
        db_ref[...] = jnp.sum(du, axis=0, keepdims=True)

    return pl.pallas_call(
        body, name=name, grid=(cw // LANES,),
        in_specs=[pl.BlockSpec((s, LANES), lambda j: (0, col0 + j)),
                  pl.BlockSpec((CONV_K, LANES), lambda j: (0, j)),
                  pl.BlockSpec((1, LANES), lambda j: (0, j)),
                  pl.BlockSpec((s, LANES), lambda j: (0, j)),
                  pl.BlockSpec(memory_space=pl.ANY)],
        out_specs=[pl.BlockSpec((s, LANES), lambda j: (0, col0 + j)),
                   pl.BlockSpec((CONV_K, LANES), lambda j: (0, j)),
                   pl.BlockSpec((1, LANES), lambda j: (0, j))],
        out_shape=[jax.ShapeDtypeStruct(into.shape, into.dtype), jax.ShapeDtypeStruct((CONV_K, cw), F32),
                   jax.ShapeDtypeStruct((1, cw), F32)],
        input_output_aliases={4: 0},
        compiler_params=_params(("parallel",)),
    )(x, w, b, dout, into)


def exchange(name, sends, broadcast):
    nop = len(sends)

    def body(*refs):
        start, wait = _exchange_ops(refs[:nop], refs[nop:2 * nop], *refs[2 * nop:], broadcast)
        start()
        wait()

    return pl.pallas_call(
        body, name=name,
        in_specs=[HBM_SPEC] * nop, out_specs=[HBM_SPEC] * nop,
        out_shape=_exchange_out(sends, broadcast), scratch_shapes=_exchange_sems(nop),
        compiler_params=pltpu.CompilerParams(has_side_effects=True),
    )(*sends)


HBM_SPEC = pl.BlockSpec(memory_space=pltpu.HBM)


def _exchange_out(sends, broadcast):
    return [jax.ShapeDtypeStruct((N_DEV,) + tuple(t.shape if broadcast else t.shape[1:]), t.dtype) for t in sends]


def _exchange_sems(nop):
    return [pltpu.SemaphoreType.DMA((nop * (N_DEV - 1),)), pltpu.SemaphoreType.DMA((nop * (N_DEV - 1),)),
            pltpu.SemaphoreType.DMA((nop,))]


def _exchange_ops(send_refs, recv_refs, send_sems, recv_sems, local_sems, broadcast):
    nop = len(send_refs)
    x, y, c = lax.axis_index("x"), lax.axis_index("y"), lax.axis_index("c")
    me = 4 * x + 2 * y + c
    peers = []
    for k in range(1, N_DEV):
        px = 1 - x if (k >> 2) & 1 else x
        py = 1 - y if (k >> 1) & 1 else y
        pc = 1 - c if k & 1 else c
        peers.append(((px, py, pc), 4 * px + 2 * py + pc))

    def src(i, peer):
        return send_refs[i] if broadcast else send_refs[i].at[peer]

    def remote(i, k, arrival):
        dev, peer = peers[k]
        return pltpu.make_async_remote_copy(
            src_ref=src(i, peer), dst_ref=recv_refs[i].at[peer if arrival else me],
            send_sem=send_sems.at[i * (N_DEV - 1) + k], recv_sem=recv_sems.at[i * (N_DEV - 1) + k],
            device_id=dev, device_id_type=pl.DeviceIdType.MESH)

    def local(i):
        return pltpu.make_async_copy(src(i, me), recv_refs[i].at[me], local_sems.at[i])

    def start():
        for i in range(nop):
            local(i).start()
        for k in range(N_DEV - 1):
            for i in range(nop):
                remote(i, k, False).start()

    def wait():
        for k in range(N_DEV - 1):
            for i in range(nop):
                remote(i, k, True).wait_recv()
        for k in range(N_DEV - 1):
            for i in range(nop):
                remote(i, k, False).wait_send()
        for i in range(nop):
            local(i).wait()

    return start, wait


def adamw_sum(name, parts, w, m, v, layer=None, into=None):
    rws, cols = w.shape[-2:]
    nsum = parts.shape[0]
    tr = _pick(rws, (256, 128, 64, 32, 16, 8))
    c1 = 1.0 / (1.0 - ADAM_B1 ** ADAM_STEP)
    c2 = 1.0 / (1.0 - ADAM_B2 ** ADAM_STEP)

    def body(p_ref, w_ref, m_ref, v_ref, *rest):
        g_ref, d_ref, nm_ref, nv_ref = rest[-4:]
        g = p_ref[0]
        for j in range(1, nsum):
            g = g + p_ref[j]
        nm = ADAM_B1 * m_ref[...] + (1.0 - ADAM_B1) * g
        nv = ADAM_B2 * v_ref[...] + (1.0 - ADAM_B2) * (g * g)
        g_ref[...] = g
        nm_ref[...] = nm
        nv_ref[...] = nv
        d_ref[...] = -ADAM_LR * ((nm * c1) / (jnp.sqrt(nv * c2) + ADAM_EPS) + ADAM_WD * w_ref[...])

    if layer is None:
        blk = pl.BlockSpec((tr, cols), lambda i: (i, 0))
    else:
        blk = pl.BlockSpec((None, tr, cols), lambda i: (layer, i, 0))
    if into is None and layer is not None:
        into = [lax.empty(w.shape, F32) for _ in range(4)]
    extra = list(into) if into else []
    return pl.pallas_call(
        body, name=name, grid=(rws // tr,),
        in_specs=([pl.BlockSpec((nsum, tr, cols), lambda i: (0, i, 0)), blk, blk, blk]
                  + [pl.BlockSpec(memory_space=pl.ANY)] * len(extra)),
        out_specs=[blk, blk, blk, blk],
        out_shape=[jax.ShapeDtypeStruct(w.shape, F32)] * 4,
        input_output_aliases={4 + k: k for k in range(len(extra))},
        compiler_params=_params(("parallel",)),
    )(parts, w, m, v, *extra)


def ada_fwd(name, c_all, w, b):
    nl = w.shape[0]

    def body(c_ref, w_ref, b_ref, o_ref):
        ca = _silu(c_ref[...])
        for l in range(nl):
            o_ref[l] = mm_nn(ca, w_ref[l]) + b_ref[l]

    return pl.pallas_call(
        body, name=name,
        out_shape=jax.ShapeDtypeStruct((nl, c_all.shape[0], w.shape[2]), F32),
        compiler_params=pltpu.CompilerParams(vmem_limit_bytes=VMEM_LIMIT),
    )(c_all, w, b)


def ada_bwd(name, c_all, dmod):
    nl = dmod.shape[0]

    def body(c_ref, g_ref, o_ref):
        ca = _silu(c_ref[...])
        for l in range(nl):
            o_ref[l] = mm_tn(ca, g_ref[l])

    return pl.pallas_call(
        body, name=name,
        out_shape=jax.ShapeDtypeStruct((nl, c_all.shape[1], dmod.shape[2]), F32),
        compiler_params=pltpu.CompilerParams(vmem_limit_bytes=VMEM_LIMIT),
    )(c_all, dmod)


def lower_bounds_fn(rows, params):
    (lg,), _ = rows, params
    nl = lg.shape[0]
    mx = jnp.max(lg, axis=0, keepdims=True)
    e = jnp.exp(lg - mx)
    p = e / jnp.sum(e, axis=0, keepdims=True)
    layer = _iota((nl, 1), 0)
    acc = jnp.zeros_like(p)
    for j in range(1, nl):
        pj = jnp.sum(jnp.where(layer == j, p, 0.0), axis=0, keepdims=True)
        acc = acc + jnp.where(layer >= j, 1.0, 0.0) * pj
    return (acc,)


def loss_call(name, x, tgt, nw, tm):
    s, d = x.shape

    def body(x_ref, t_ref, w_ref, l_ref, dx_ref, dw_ref):
        def f(xv, wv):
            err = _rms(xv, wv) - t_ref[...]
            return jnp.sum(0.5 * jnp.mean(err * err, axis=-1, keepdims=True), axis=0, keepdims=True)

        val, vjp = jax.vjp(f, x_ref[...], w_ref[...])
        dx, dw = vjp(jnp.ones_like(val))

        @pl.when(pl.program_id(0) == 0)
        def _():
            l_ref[...] = jnp.zeros_like(l_ref)
            dw_ref[...] = jnp.zeros_like(dw_ref)

        l_ref[...] += jnp.broadcast_to(val, l_ref.shape)
        dw_ref[...] += dw
        dx_ref[...] = dx

    row = pl.BlockSpec((tm, d), lambda i: (i, 0))
    return pl.pallas_call(
        body, name=name, grid=(s // tm,),
        in_specs=[row, row, _whole(nw)],
        out_specs=[pl.BlockSpec((8, LANES), lambda i: (0, 0)), row, _whole(nw)],
        out_shape=[jax.ShapeDtypeStruct((8, LANES), F32), jax.ShapeDtypeStruct((s, d), F32),
                   jax.ShapeDtypeStruct(nw.shape, F32)],
        compiler_params=_params(("arbitrary",)),
    )(x, tgt, nw)


class Dims:
    def __init__(self, s, d, ffn):
        self.s, self.d, self.ffn = s, d, ffn
        self.mix = 3 * d // 4
        self.nh = self.mix // HEAD
        self.ssm_heads = self.mix // SSM_P
        self.pairs = self.mix // (2 * SSM_P)
        self.nc = s // CHUNK
        self.conv_ssm = self.mix + 4 * HEAD
        self.conv_w = self.conv_ssm + 3 * self.mix
        self.o_gates = 4 * self.mix
        self.o_sz = self.o_gates + 3 * d
        self.o_gz = self.o_sz + self.mix
        self.o_conv = self.o_gz + self.mix
        self.o_small = self.o_conv + self.conv_w
        used = self.o_small + LANES
        self.np = -(-used // 1280) * 1280
        self.tm = _pick(s, (256, 128, 64))
        mix, nh = self.mix, self.nh
        self.in_sizes = (mix, mix, mix, mix, mix, self.conv_ssm, self.ssm_heads, 3 * mix, mix, nh, nh, 3 * d)
        self.in_width = sum(self.in_sizes)


def w_in_tables(dm, nshard):
    off = np.cumsum((0,) + dm.in_sizes)
    hq, hf, hi, hg, sz, sxbc, sdt, gqkv, gz, gb, ga, gates = (np.arange(off[i], off[i + 1]) for i in range(12))
    hgrn = np.stack([t.reshape(dm.nh, HEAD) for t in (hq, hf, hi, hg)], axis=1).reshape(-1)
    perm = np.concatenate([hgrn, gates, sz, gz, gqkv, sxbc, sdt, gb, ga])
    perm = np.concatenate([perm, np.full(dm.np - perm.size, -1)])
    shard = dm.in_width // nshard
    wpad = -(-shard // GATHER_TILE) * GATHER_TILE
    fwd = np.where(perm >= 0, (perm // shard) * wpad + perm % shard, -1)[None]
    inv = np.zeros(dm.in_width, np.int64)
    inv[perm[perm >= 0]] = np.nonzero(perm >= 0)[0]
    bwd = np.full((nshard, wpad), -1)
    bwd[:, :shard] = inv.reshape(nshard, shard)
    return fwd.astype(np.int32), bwd.astype(np.int32)


def _small_views(dm, small):
    t = small.T
    col = lambda a: a[:, :, None]
    row = lambda a: a.reshape(a.shape[0], dm.nc, 1, CHUNK)
    a, b = dm.ssm_heads, dm.ssm_heads + dm.nh
    sdt, gb, ga = t[:a], t[a:b], t[b:b + dm.nh]
    return col(sdt), row(sdt), col(gb), col(ga), row(ga)


def _scan_specs(dm, proj, conv_out, views, lp, dproj=None):
    dt_col, dt_row, gb_col, ga_col, ga_row = views
    mixb, nh = dm.mix // LANES, dm.nh
    s, mix = dm.s, dm.mix
    lane = ("lane", LANES)
    hb = HGRN_HEADS_PER_STEP
    hw = (CHUNK, hb * LANES)
    hgrn = dict(
        nblk=nh // hb, hb=hb, fn=hgrn_chunk, batched=False, state=(HEAD, HEAD),
        seqs=[(proj, (CHUNK, hb * 4 * HEAD), lambda h, n: (n, h), ("lane", 4 * HEAD))],
        hparams=[(lp["lb"], (1, hb * HEAD), lambda h: (0, h), lane)],
        sparams=[lp["hgrn_norm"]],
        dseqs=[((s, dm.np), BF16, (CHUNK, hb * 4 * HEAD), lambda h, n: (n, h), ("lane", 4 * HEAD), dproj)],
        io=(hw, lambda h, n: (n, h), lane))
    ppg = dm.pairs // 2
    qb = 3 * mixb
    gw = (CHUNK, ppg * LANES)
    group = ("lane", ppg * LANES)
    pcol = ((2 * ppg, CHUNK, 1), lambda g, n: (g, n, 0), ("lead", 2 * ppg))
    prow = ((2 * ppg, None, 1, CHUNK), lambda g, n: (g, n, 0, 0), ("lead", 2 * ppg))
    ppar = ((2 * ppg, 1, 1), lambda g: (g, 0, 0), ("lead", 2 * ppg))
    bc = lambda first: ((CHUNK, LANES), lambda g, n: (n, first + g), None)
    ssd = dict(
        nblk=2, hb=1, fn=ssd_chunk, batched=False, state=(HEAD, ppg * LANES),
        seqs=[(conv_out, gw, lambda g, n: (n, qb // ppg + g), group), (conv_out,) + bc(qb + mixb), (conv_out,) + bc(qb + mixb + 2),
              (dt_col,) + pcol, (dt_row,) + prow],
        hparams=[(lp["ssm_dt_bias"],) + ppar, (lp["ssm_a_log"],) + ppar],
        sparams=[],
        dseqs=[((s, mix), F32, gw, lambda g, n: (n, g), group), ((s, 2 * LANES), F32) + bc(0), ((s, 2 * LANES), F32) + bc(0),
               (dt_col.shape, F32) + pcol, (dt_row.shape, F32) + prow],
        io=(gw, lambda g, n: (n, g), group))
    hb = GDN_HEADS_PER_STEP
    hw = (CHUNK, hb * LANES)
    cq, cgz = 0, dm.o_gz // LANES
    assert nh % hb == 0 and cgz % hb == 0 and qb % ppg == 0
    hcol = ((hb, CHUNK, 1), lambda h, n: (h, n, 0), ("idx",))
    hrow = ((hb, None, 1, CHUNK), lambda h, n: (h, n, 0, 0), ("idx",))
    hpar = ((hb, 1, 1), lambda h: (h, 0, 0), ("idx",))
    at = lambda first: (hw, lambda h, n: (n, first // hb + h), lane)
    gdn = dict(
        nblk=nh // hb, hb=hb, fn=gdn_chunk, batched=True, state=(HEAD, HEAD),
        seqs=[(conv_out,) + at(cq), (conv_out,) + at(cq + nh), (conv_out,) + at(cq + 2 * nh), (proj,) + at(cgz),
              (gb_col,) + hcol, (ga_col,) + hcol, (ga_row,) + hrow],
        hparams=[(lp["gdn_dt_bias"],) + hpar, (lp["gdn_a_log"],) + hpar],
        sparams=[lp["gdn_norm"]],
        dseqs=[((s, mix), F32) + at(0), ((s, mix), F32) + at(0), ((s, mix), F32) + at(0), ((s, dm.np), BF16) + at(cgz) + (dproj,),
               (gb_col.shape, F32) + hcol, (ga_col.shape, F32) + hcol, (ga_row.shape, F32) + hrow],
        io=(hw, lambda h, n: (n, h), lane))
    return hgrn, ssd, gdn


def _run_scan_fwd(dm, name, sp, side=None):
    out = ((dm.s, dm.mix), F32) + sp["io"]
    (y,), states, arrived = scan_fwd(name, sp["fn"], sp["nblk"], sp["hb"], dm.nc, sp["seqs"], sp["hparams"],
                                     sp["sparams"], sp["state"], [out], sp["batched"], side)
    return y, states, arrived


def _run_scan_bwd(dm, name, sp, states, dy, side=None):
    return scan_bwd(name, sp["fn"], sp["nblk"], sp["hb"], dm.nc, sp["seqs"], sp["hparams"], sp["sparams"], sp["state"],
                    states, [(dy,) + sp["io"]], sp["dseqs"], sp["batched"], side)


def _share_out(side):
    if side is None:
        return None, None, None
    s, broadcast = side
    return ([s[0]], broadcast), ([s[1], s[2], s[4]], broadcast), ([s[3]], broadcast)


def _collect(got_h, got_s, got_g):
    if not got_h:
        return None
    return [got_h[0], got_s[0], got_s[1], got_g[0], got_s[2]]


def layer_fwd(dm, l, x, lp, side=None):
    tm, d, mix = dm.tm, dm.d, dm.mix
    tag = f"l{l}_"
    (h,) = rowstage_fwd(tag + "norm1", normmod_fn, [(x, d, 0)], [lp["norm_mix"], lp["sc1"], lp["sh1"]], [(d, BF16)], tm)
    proj = matmul(tag + "proj", h, lp["w_in"], "nn", F32)
    conv_out = conv_fwd(tag + "conv", proj, dm.o_conv // LANES, lp["conv_w"], lp["conv_b"])
    small = proj[:, dm.o_small:dm.o_small + LANES]
    views = _small_views(dm, small)
    hg, sd, gd = _scan_specs(dm, proj, conv_out, views, lp)
    side_h, side_s, side_g = _share_out(side)
    yh, st_h, got_h = _run_scan_fwd(dm, tag + "hgrn", hg, side_h)
    y_ssd, st_s, got_s = _run_scan_fwd(dm, tag + "ssd", sd, side_s)
    yg, st_g, got_g = _run_scan_fwd(dm, tag + "gdn", gd, side_g)
    arrived = _collect(got_h, got_s, got_g)
    (ys,) = rowstage_fwd(tag + "ssmpost", ssmpost_fn,
                         [(y_ssd, mix, 0), (conv_out, mix, 3), (proj, mix, dm.o_sz // mix)],
                         [lp["ssm_d_exp"], lp["ssm_norm"]], [(mix, F32)], tm)
    (merged,) = rowstage_fwd(tag + "merge", merge_fn, [(yh, mix, 0), (ys, mix, 0), (yg, mix, 0), (proj, 3 * d, 1)],
                             [lp["b_merge"], lp["w_branch"]], [(d, BF16)], tm)
    (x1,) = rowstage_fwd(tag + "outproj", outproj_fn, [(merged, d, 0), (x, d, 0)], [lp["g1"], lp["w_out"]], [(d, F32)], tm)
    (h2,) = rowstage_fwd(tag + "norm2", normmod_fn, [(x1, d, 0)], [lp["norm_ffn"], lp["sc2"], lp["sh2"]], [(d, BF16)], tm)
    gu = bmatmul(tag + "ffn_in", h2, lp["w_ffn_in"], "nn", BF16, True)
    gu = gu.reshape((2, gu.shape[0] // 2) + gu.shape[1:])
    act = swiglu3_fwd(tag + "swiglu", gu, tm)
    o2 = bmatmul(tag + "ffn_out", act, lp["w_ffn_out"], "nn", F32, False)
    (x2,) = rowstage_fwd(tag + "resid", resid_fn, [(x1, d, 0), (o2, d, 0)], [lp["g2"]], [(d, F32)], tm)
    saved = dict(x=x, h=h, proj=proj, conv_out=conv_out, views=views, yh=yh, y_ssd=y_ssd, yg=yg, ys=ys,
                 st_h=st_h, st_s=st_s, st_g=st_g, merged=merged, x1=x1, h2=h2, gu=gu, act=act, o2=o2)
    return x2, saved, arrived


def layer_bwd(dm, l, dx2, lp, sv, side=None, own=False):
    tm, d, mix, s = dm.tm, dm.d, dm.mix, dm.s
    tag = f"l{l}_b_"
    g = {}
    (dx1_a, do2), (g["g2"],) = rowstage_bwd(tag + "resid", resid_fn, [(sv["x1"], d, 0), (sv["o2"], d, 0)], [lp["g2"]],
                                            [dx2], [F32, BF16], tm)
    dact = bmatmul(tag + "ffn_out_dx", do2, lp["w_ffn_out"], "nt", BF16, True)
    g["w_ffn_out"] = bmatmul(tag + "ffn_out_dw", sv["act"], do2, "tn", F32, True)
    dgu = swiglu3_bwd(tag + "swiglu", sv["gu"], dact, tm)
    dgu = dgu.reshape((-1,) + dgu.shape[2:])
    dh2 = bmatmul(tag + "ffn_in_dx", dgu, lp["w_ffn_in"], "nt", BF16, False)
    g["w_ffn_in"] = bmatmul(tag + "ffn_in_dw", sv["h2"], dgu, "tn", F32, True)
    (dx1,), (g["norm_ffn"], g["sc2"], g["sh2"]) = rowstage_bwd(
        tag + "norm2", normmod_fn, [(sv["x1"], d, 0)], [lp["norm_ffn"], lp["sc2"], lp["sh2"]], [dh2], [F32], tm,
        adds={0: dx1_a})
    (dmerged, dx_a), (g["g1"], g["w_out"]) = rowstage_bwd(
        tag + "outproj", outproj_fn, [(sv["merged"], d, 0), (sv["x"], d, 0)], [lp["g1"], lp["w_out"]], [dx1],
        [BF16, F32], tm)
    proj, conv_out = sv["proj"], sv["conv_out"]
    dproj = lax.empty((s, dm.np), BF16)
    (dyh, dys, dyg, dproj), (g["b_merge"], g["w_branch"]) = rowstage_bwd(
        tag + "merge", merge_fn, [(sv["yh"], mix, 0), (sv["ys"], mix, 0), (sv["yg"], mix, 0), (proj, 3 * d, 1)],
        [lp["b_merge"], lp["w_branch"]], [dmerged], [F32, F32, F32, BF16], tm, into={3: (dproj, 1)})
    (dy_ssd, dxs_a, dproj), (g["ssm_d_exp"], g["ssm_norm"]) = rowstage_bwd(
        tag + "ssmpost", ssmpost_fn, [(sv["y_ssd"], mix, 0), (conv_out, mix, 3), (proj, mix, dm.o_sz // mix)],
        [lp["ssm_d_exp"], lp["ssm_norm"]], [dys], [F32, F32, BF16], tm, into={2: (dproj, dm.o_sz // mix)})
    side_h, side_s, side_g = _share_out(side)
    hg, sd, _ = _scan_specs(dm, proj, conv_out, sv["views"], lp, dproj)
    (dproj,), (g["lb"],), (g["hgrn_norm"],), got_h = _run_scan_bwd(dm, tag + "hgrn", hg, sv["st_h"], dyh, side_h)
    gd = _scan_specs(dm, proj, conv_out, sv["views"], lp, dproj)[2]
    (dxs_b, dbp, dcp, d_dt_col, d_dt_row), (g["ssm_dt_bias"], g["ssm_a_log"]), _, got_s = _run_scan_bwd(
        dm, tag + "ssd", sd, sv["st_s"], dy_ssd, side_s)
    (dq, dk, dv, dproj, d_gb_col, d_ga_col, d_ga_row), (g["gdn_dt_bias"], g["gdn_a_log"]), (g["gdn_norm"],), got_g = _run_scan_bwd(
        dm, tag + "gdn", gd, sv["st_g"], dyg, side_g)
    arrived = _collect(got_h, got_s, got_g)
    dconv = jnp.concatenate([dq, dk, dv, dxs_a + dxs_b, dbp, dcp], axis=1)
    dproj, g["conv_w"], g["conv_b"] = conv_bwd(tag + "conv", proj, dm.o_conv // LANES, lp["conv_w"], lp["conv_b"], dconv,
                                               dproj)
    unrow = lambda t: t.reshape(t.shape[0], s).T
    dsmall = jnp.concatenate([d_dt_col[:, :, 0].T + unrow(d_dt_row), d_gb_col[:, :, 0].T,
                              d_ga_col[:, :, 0].T + unrow(d_ga_row)], axis=1)
    tail = jnp.pad(dsmall.astype(BF16), ((0, 0), (0, dm.np - dm.o_small - dsmall.shape[1])))
    dproj = lax.dynamic_update_slice(dproj, tail, (0, dm.o_small))
    if own:
        s_wb, s_wout, s_wf, s_wfo = small_shards(dm, g)
        dh, (got_wf,) = matmul(tag + "proj_dx", dproj, lp["w_in"], "nt", BF16, ([s_wf], False))
        g["w_in"], (got_wb, got_wout, got_wfo) = matmul(tag + "proj_dw", sv["h"], dproj, "tn", F32,
                                                        ([s_wb, s_wout, s_wfo], False))
    else:
        dh = matmul(tag + "proj_dx", dproj, lp["w_in"], "nt", BF16)
        g["w_in"] = matmul(tag + "proj_dw", sv["h"], dproj, "tn", F32)
    (dx,), (g["norm_mix"], g["sc1"], g["sh1"]) = rowstage_bwd(
        tag + "norm1", normmod_fn, [(sv["x"], d, 0)], [lp["norm_mix"], lp["sc1"], lp["sh1"]], [dh], [F32], tm,
        adds={0: dx_a})
    if own:
        return dx, g, arrived, [got_wb, got_wout, got_wf, got_wfo]
    return dx, g, arrived


WEIGHTS = ("w_ada", "b_ada", "norm_mix", "norm_ffn", "w_in", "b_merge", "hgrn_lb_logits", "hgrn_norm", "ssm_conv_w",
           "ssm_conv_b", "ssm_dt_bias", "ssm_a_log", "ssm_d", "ssm_norm", "gdn_conv_w", "gdn_dt_bias", "gdn_a_log",
           "gdn_norm", "w_branch", "w_out", "w_ffn_in", "w_ffn_out", "norm_final")
GATHERED = ("w_in", "w_branch", "w_out", "w_ffn_in", "w_ffn_out")
PACKET = ("b_ada", "norm_mix", "norm_ffn", "b_merge", "hgrn_norm", "ssm_conv_b", "ssm_dt_bias", "ssm_a_log", "ssm_d",
          "ssm_norm", "gdn_dt_bias", "gdn_a_log", "gdn_norm", "norm_final")
MISC = ("hgrn_lb_logits", "ssm_conv_w", "gdn_conv_w")


def _pack(arrs, dtype, row_mult, lead=0):
    flat = jnp.concatenate([t.reshape(t.shape[:lead] + (-1,)).astype(dtype) for t in arrs], axis=lead)
    n = flat.shape[-1]
    unit = row_mult * LANES
    tot = -(-n // unit) * unit
    flat = jnp.pad(flat, [(0, 0)] * lead + [(0, tot - n)])
    return flat.reshape(flat.shape[:lead] + (tot // LANES, LANES))


def _unpack(packed, shapes, lead=0):
    flat = packed.reshape(packed.shape[:lead] + (-1,))
    out, off = [], 0
    for shp in shapes:
        n = int(np.prod(shp))
        out.append(flat[..., off:off + n].reshape(flat.shape[:lead] + tuple(shp)))
        off += n
    return out


def _shard2d(t):
    return t.reshape((-1, t.shape[-1]))


def weights_from_shards(dm, l, got, idx):
    w_in, wb, w_out, wf, wfo = got
    d, mix = dm.d, dm.mix
    return dict(
        w_in=colgather(f"l{l}_w_in", w_in, idx, dm.np, BF16)[0],
        w_branch=wb.reshape(N_DEV, 3, mix, d // N_DEV).transpose(1, 2, 0, 3).reshape(3, mix, d),
        w_out=w_out.reshape(d, d), w_ffn_in=wf, w_ffn_out=wfo.reshape(N_DEV // 2, -1, d))


def small_shards(dm, g):
    d, mix = dm.d, dm.mix
    return [g["w_branch"].reshape(3, mix, N_DEV, d // N_DEV).transpose(2, 0, 1, 3).reshape(N_DEV, 3 * mix, d // N_DEV),
            g["w_out"].reshape(N_DEV, d // N_DEV, d), g["w_ffn_in"], g["w_ffn_out"].reshape(N_DEV, -1, d)]


def w_in_shards(dm, l, g, idx):
    return colgather(f"l{l}_g_w_in", g["w_in"][None], idx, dm.in_width // N_DEV, F32)


def layer_params(dm, l, full, small, mod_l, lb_l):
    d, mix = dm.d, dm.mix
    row = lambda t: t.reshape(1, -1)
    head = lambda t: t.reshape(-1, 1, 1)
    sh1, sc1, g1, sh2, sc2, g2 = (row(mod_l[i * d:(i + 1) * d]) for i in range(6))
    conv_b = jnp.concatenate([jnp.zeros((3 * mix,), F32), small["ssm_conv_b"][l]])
    return dict(
        w_in=full["w_in"], w_branch=full["w_branch"], w_out=full["w_out"],
        w_ffn_in=full["w_ffn_in"], w_ffn_out=full["w_ffn_out"],
        norm_mix=row(small["norm_mix"][l]), norm_ffn=row(small["norm_ffn"][l]), b_merge=row(small["b_merge"][l]),
        hgrn_norm=row(small["hgrn_norm"][l]), lb=row(lb_l),
        conv_w=jnp.concatenate([small["gdn_conv_w"][l], small["ssm_conv_w"][l]], axis=1), conv_b=row(conv_b),
        ssm_dt_bias=head(small["ssm_dt_bias"][l]), ssm_a_log=head(small["ssm_a_log"][l]),
        ssm_d_exp=row(jnp.repeat(small["ssm_d"][l], SSM_P)), ssm_norm=row(small["ssm_norm"][l]),
        gdn_dt_bias=head(small["gdn_dt_bias"][l]), gdn_a_log=head(small["gdn_a_log"][l]), gdn_norm=row(small["gdn_norm"][l]),
        sh1=sh1, sc1=sc1, g1=g1, sh2=sh2, sc2=sc2, g2=g2)


def layer_grads(dm, g):
    cs = 3 * dm.mix
    out = dict(
        w_in=g["w_in"], w_branch=g["w_branch"], w_out=g["w_out"], w_ffn_in=g["w_ffn_in"],
        w_ffn_out=g["w_ffn_out"], norm_mix=g["norm_mix"][0], norm_ffn=g["norm_ffn"][0], b_merge=g["b_merge"][0],
        hgrn_norm=g["hgrn_norm"][0], ssm_conv_w=g["conv_w"][:, cs:], gdn_conv_w=g["conv_w"][:, :cs],
        ssm_conv_b=g["conv_b"][0, cs:], ssm_dt_bias=g["ssm_dt_bias"][:, 0, 0], ssm_a_log=g["ssm_a_log"][:, 0, 0],
        ssm_d=g["ssm_d_exp"].reshape(dm.ssm_heads, SSM_P).sum(axis=1), ssm_norm=g["ssm_norm"][0],
        gdn_dt_bias=g["gdn_dt_bias"][:, 0, 0], gdn_a_log=g["gdn_a_log"][:, 0, 0], gdn_norm=g["gdn_norm"][0])
    dmod = jnp.concatenate([g[k][0] for k in ("sh1", "sc1", "g1", "sh2", "sc2", "g2")])
    return out, dmod, g["lb"][0]


def local_step(dm, nl, x, tgt, norm_final, params_of, gather_of=None, scatter_of=None):
    arrived = exchange("gather_w0", gather_of(0), True) if gather_of else None
    lps, saved = [], []
    for l in range(nl):
        lps.append(params_of(l, arrived))
        side = (gather_of(l + 1), True) if gather_of and l + 1 < nl else None
        x, sv, arrived = layer_fwd(dm, l, x, lps[l], side)
        saved.append(sv)
    loss, dx, dnf = loss_call("loss", x, tgt, norm_final, dm.tm)
    grads, parts, side = [None] * nl, [None] * nl, None
    for l in reversed(range(nl)):
        if scatter_of and l == 0:
            dx, grads[l], got, own = layer_bwd(dm, l, dx, lps[l], saved[l], side, own=True)
            parts[0] = list(exchange("scatter_g0", [scatter_of(0, grads[0])], False)) + own
        else:
            dx, grads[l], got = layer_bwd(dm, l, dx, lps[l], saved[l], side)
        if side is not None:
            parts[l + 1] = got
        side = ([scatter_of(l, grads[l])] + small_shards(dm, grads[l]), False) if scatter_of and l > 0 else None
    return loss, dx, dnf, grads, parts


def kernel(x, c, w_ada, b_ada, norm_mix, norm_ffn, w_in, b_merge, hgrn_lb_logits, hgrn_norm, ssm_conv_w, ssm_conv_b, ssm_dt_bias, ssm_a_log, ssm_d, ssm_norm, gdn_conv_w, gdn_dt_bias, gdn_a_log, gdn_norm, w_branch, w_out, w_ffn_in, w_ffn_out, norm_final, loss_target, m_w_ada, m_b_ada, m_norm_mix, m_norm_ffn, m_w_in, m_b_merge, m_hgrn_lb_logits, m_hgrn_norm, m_ssm_conv_w, m_ssm_conv_b, m_ssm_dt_bias, m_ssm_a_log, m_ssm_d, m_ssm_norm, m_gdn_conv_w, m_gdn_dt_bias, m_gdn_a_log, m_gdn_norm, m_w_branch, m_w_out, m_w_ffn_in, m_w_ffn_out, m_norm_final, v_w_ada, v_b_ada, v_norm_mix, v_norm_ffn, v_w_in, v_b_merge, v_hgrn_lb_logits, v_hgrn_norm, v_ssm_conv_w, v_ssm_conv_b, v_ssm_dt_bias, v_ssm_a_log, v_ssm_d, v_ssm_norm, v_gdn_conv_w, v_gdn_dt_bias, v_gdn_a_log, v_gdn_norm, v_w_branch, v_w_out, v_w_ffn_in, v_w_ffn_out, v_norm_final):
    a = dict(locals())
    x, tgt = a["x"][0], a["loss_target"][0]
    s, d = x.shape
    nl = a["w_ada"].shape[0]
    dm = Dims(s, d, a["w_ffn_out"].shape[1] * N_DEV)
    me = 4 * lax.axis_index("x") + 2 * lax.axis_index("y") + lax.axis_index("c")

    first = [a["c"], a["ssm_conv_w"], a["gdn_conv_w"]]
    c_all, scw, gcw = _unpack(exchange("gather_c", [_pack(first, F32, 8)], True)[0], [t.shape for t in first], lead=1)
    small = dict(a, ssm_conv_w=scw.transpose(1, 2, 0, 3).reshape(scw.shape[1:3] + (-1,)),
                 gdn_conv_w=gcw.transpose(1, 2, 0, 3).reshape(gcw.shape[1:3] + (-1,)))
    c_pad = jnp.zeros((LANES, d), F32).at[:N_DEV].set(c_all.reshape(N_DEV, d))
    ncol = a["w_ada"].shape[2]
    b_mine = lax.dynamic_slice(a["b_ada"], (0, me * ncol), (nl, ncol))[:, None, :]
    mod_part = ada_fwd("ada_fwd", c_pad, a["w_ada"], b_mine)[:, :N_DEV, :]
    (mod,) = exchange("a2a_mod", [mod_part.transpose(1, 0, 2)], False)
    mod = mod.transpose(1, 0, 2).reshape(nl, N_DEV * ncol)
    (lb,) = rowstage_fwd("lower_bounds", lower_bounds_fn, [(a["hgrn_lb_logits"], dm.mix, 0)], [], [(dm.mix, F32)], nl)

    idx_fwd, idx_bwd = w_in_tables(dm, N_DEV)
    loss, dx, dnf, grads, parts = local_step(
        dm, nl, x, tgt, a["norm_final"].reshape(1, d),
        params_of=lambda l, got: layer_params(dm, l, weights_from_shards(dm, l, got, idx_fwd), small, mod[l], lb[l]),
        gather_of=lambda l: [_shard2d(a[n][l]).astype(BF16) for n in GATHERED],
        scatter_of=lambda l, g: w_in_shards(dm, l, g, idx_bwd))

    per_layer = [layer_grads(dm, g) for g in grads]
    res = {}
    for i, n in enumerate(GATHERED):
        wmv = [a[q + n].reshape((nl, -1, a[n].shape[-1])) for q in ("", "m_", "v_")]
        outs = None
        for l in range(nl):
            outs = adamw_sum(f"adamw_l{l}_{n}", parts[l][i], *wmv, layer=l, into=outs)
        for kind, o in zip(("grad", "delta", "new_m", "new_v"), outs):
            res[(kind, n)] = o.reshape(a[n].shape)

    stackg = lambda n: jnp.stack([pl_[0][n] for pl_ in per_layer])
    dmod = jnp.stack([pl_[1] for pl_ in per_layer])
    dlb = jnp.stack([pl_[2] for pl_ in per_layer])
    pk_g = [dmod if n == "b_ada" else dnf if n == "norm_final" else stackg(n) for n in PACKET]
    extra = [dlb, stackg("ssm_conv_w"), stackg("gdn_conv_w"), loss[0, :1]]
    pk_shapes = [t.shape for t in pk_g + extra]
    zeros = [jnp.zeros(t.shape, F32) for t in extra]
    (parts,) = exchange("gather_small", [_pack(pk_g + extra, F32, 8)], True)
    outs = adamw_sum("adamw_small", parts, *[_pack([a[p + n] for n in PACKET] + zeros, F32, 8) for p in ("", "m_", "v_")])
    for kind, o in zip(("grad", "delta", "new_m", "new_v"), outs):
        un = _unpack(o, pk_shapes)
        for n, t in zip(PACKET, un):
            res[(kind, n)] = t.reshape(a[n].shape)
        if kind == "grad":
            dlb_sum, g_scw, g_gcw, loss_sum = un[len(PACKET):]

    (g_lb,), _ = rowstage_bwd("lower_bounds_b", lower_bounds_fn, [(a["hgrn_lb_logits"], dm.mix, 0)], [], [dlb_sum], [F32], nl)
    mine = lambda t, n: lax.dynamic_slice_in_dim(t, me * a[n].shape[-1], a[n].shape[-1], axis=t.ndim - 1)
    (dmod_cols,) = exchange("a2a_dmod", [dmod.reshape(nl, N_DEV, ncol).transpose(1, 0, 2)], False)
    dmod_pad = jnp.zeros((nl, LANES, ncol), F32).at[:, :N_DEV].set(dmod_cols.transpose(1, 0, 2))
    g_w_ada = ada_bwd("ada_bwd", c_pad, dmod_pad)
    outs = adamw_sum("adamw_w_ada", g_w_ada.reshape(1, nl * d, ncol), *[a[q + "w_ada"].reshape(nl * d, ncol) for q in ("", "m_", "v_")])
    for kind, o in zip(("grad", "delta", "new_m", "new_v"), outs):
        res[(kind, "w_ada")] = o.reshape(nl, d, ncol)
    g_misc = [g_lb, mine(g_scw, "ssm_conv_w"), mine(g_gcw, "gdn_conv_w")]
    outs = adamw_sum("adamw_misc", _pack(g_misc, F32, 8)[None], *[_pack([a[q + n] for n in MISC], F32, 8) for q in ("", "m_", "v_")])
    for kind, o in zip(("grad", "delta", "new_m", "new_v"), outs):
        for n, t in zip(MISC, _unpack(o, [a[n].shape for n in MISC])):
            res[(kind, n)] = t

    out = [loss_sum.reshape(()), dx[None]]
    for kind in ("grad", "delta", "new_m", "new_v"):
        out += [res[(kind, n)] for n in WEIGHTS]
    return tuple(out)
```

```python
import functools
import math

import numpy as np
import jax
import jax.numpy as jnp
from jax import lax
from jax.experimental import pallas as pl
from jax.experimental.pallas import tpu as pltpu

F32 = jnp.float32
BF16 = jnp.bfloat16

N_DEV = 8
CHUNK = 64
SUB = 8
HGRN_HEADS_PER_STEP = 6
GDN_HEADS_PER_STEP = 6
HEAD = 128
SSM_P = 64
CONV_K = 4
F_MIN = 1e-30
NORM_EPS = 1e-6
LANES = 128
GATHER_TILE = 256
VMEM_LIMIT = 56 * 1024 * 1024

ADAM_LR = 0.001
ADAM_B1 = 0.9
ADAM_B2 = 0.999
ADAM_EPS = 1e-08
ADAM_WD = 0.01
ADAM_STEP = 10


def _dg(a, b, ca, cb):
    return lax.dot_general(a.astype(BF16), b.astype(BF16), (((ca,), (cb,)), ((), ())),
                           preferred_element_type=F32)


def _split3(x):
    x1 = x.astype(BF16)
    r = x - x1.astype(F32)
    x2 = r.astype(BF16)
    x3 = (r - x2.astype(F32)).astype(BF16)
    return x1, x2, x3


def _hdg(a, b, ca, cb):
    a1, a2, _ = _split3(a)
    b1, b2, _ = _split3(b)
    dn = (((ca,), (cb,)), ((), ()))
    d = lambda p, q: lax.dot_general(p, q, dn, preferred_element_type=F32)
    return (d(a2, b1) + d(a1, b2)) + d(a1, b1)


def _dot_family(prim):
    @jax.custom_vjp
    def nn(a, b):
        return prim(a, b, 1, 0)

    @jax.custom_vjp
    def nt(a, b):
        return prim(a, b, 1, 1)

    @jax.custom_vjp
    def tn(a, b):
        return prim(a, b, 0, 0)

    nn.defvjp(lambda a, b: (nn(a, b), (a, b)), lambda r, g: (nt(g, r[1]), tn(r[0], g)))
    nt.defvjp(lambda a, b: (nt(a, b), (a, b)), lambda r, g: (nn(g, r[1]), tn(g, r[0])))
    tn.defvjp(lambda a, b: (tn(a, b), (a, b)), lambda r, g: (nt(r[1], g), nn(r[0], g)))
    return nn, nt, tn


mm_nn, mm_nt, mm_tn = _dot_family(_dg)
hd_nn, hd_nt, hd_tn = _dot_family(_hdg)


def _iota(shape, dim):
    return lax.broadcasted_iota(jnp.int32, shape, dim)


def _scan_rows(x, reverse):
    n = x.shape[0]
    rows = _iota(x.shape, 0)
    k = 1
    while k < n:
        if reverse:
            x = x + jnp.where(rows < n - k, pltpu.roll(x, n - k, 0), 0.0)
        else:
            x = x + jnp.where(rows >= k, pltpu.roll(x, k, 0), 0.0)
        k *= 2
    return x


@jax.custom_vjp
def cumsum_rows(x):
    return _scan_rows(x, False)


cumsum_rows.defvjp(lambda x: (_scan_rows(x, False), None), lambda _, g: (_scan_rows(g, True),))


def _sigmoid(x):
    return jax.nn.sigmoid(x)


def _silu(x):
    return x * jax.nn.sigmoid(x)


def _softplus(x):
    e = jnp.exp(-jnp.abs(x))
    small = e * (1.0 - e * (0.5 - e * (1.0 / 3.0)))
    return jnp.maximum(x, 0.0) + jnp.where(e < 1e-3, small, jnp.log(1.0 + e))


def _masked_exp(diff, mask):
    return jnp.where(mask, jnp.exp(jnp.where(mask, diff, 0.0)), 0.0)


def _rms(x, w):
    return x * lax.rsqrt(jnp.mean(x * x, axis=-1, keepdims=True) + NORM_EPS) * w


def _cum_col_row(lg_col, lg_row):
    c = lg_col.shape[0]
    r, s = _iota((c, c), 0), _iota((c, c), 1)
    cum_col = jnp.sum(jnp.where(s <= r, jnp.broadcast_to(lg_row, (c, c)), 0.0), axis=1, keepdims=True)
    cum_row = jnp.sum(jnp.where(r <= s, jnp.broadcast_to(lg_col, (c, c)), 0.0), axis=0, keepdims=True)
    total = jnp.sum(lg_col, axis=0, keepdims=True)
    return cum_col, cum_row, total


def hgrn_chunk(seq, hp, sp, st):
    (blk,), (lb,), (nw,) = seq, hp, sp
    c = blk.shape[0]
    q_raw, f_raw, v, g_raw = (blk[:, i * HEAD:(i + 1) * HEAD] for i in range(4))
    q = _silu(q_raw)
    f = lb + (1.0 - lb) * _sigmoid(f_raw)
    logf = jnp.log(jnp.maximum(f, F_MIN))
    k = (1.0 - lb) * _sigmoid(-f_raw)
    b = cumsum_rows(logf)
    o_inter = mm_nt(q * jnp.exp(b), st)
    nsub = c // SUB
    wide = (SUB, SUB, HEAD)
    er = _iota((SUB * SUB, SUB), 0)
    e_t = (er // SUB == _iota((SUB * SUB, SUB), 1)).astype(F32)
    pr = _iota((SUB * SUB, 1), 0)
    pmask = (pr % SUB) <= (pr // SUB)
    er64 = _iota((SUB * SUB, c), 0)
    ec64 = _iota((SUB * SUB, c), 1)
    rows_c = _iota((c, 1), 0)
    row = lambda a, i: jnp.sum(jnp.where(rows_c == i, a, 0.0), axis=0, keepdims=True)
    def sub_chunk(qi, ki, bi, bref, first, place):
        qb = jnp.broadcast_to(qi[:, None, :], wide).reshape(SUB * SUB, HEAD)
        kb = jnp.broadcast_to(ki[None, :, :], wide).reshape(SUB * SUB, HEAD)
        bd = (bi[:, None, :] - bi[None, :, :]).reshape(SUB * SUB, HEAD)
        sc_col = jnp.sum(qb * kb * _masked_exp(bd, pmask), axis=1, keepdims=True)
        sc = mm_tn(e_t, sc_col * place)
        sc = sc + mm_nt(qi * jnp.exp(bi - bref), k * _masked_exp(bref - b, rows_c < first))
        return mm_nn(sc, v)

    firsts = [SUB * i for i in range(nsub)]
    pile = lambda parts: jnp.concatenate([p[None] for p in parts], axis=0)
    cut = lambda a: a.reshape(nsub, SUB, HEAD)
    brefs = pile([row(b, f) for f in firsts])
    starts = pile([jnp.full((1, 1), f, jnp.int32) for f in firsts])
    places = pile([(ec64 == (er64 % SUB) + f).astype(F32) for f in firsts])
    o_intra = jax.vmap(sub_chunk)(cut(q), cut(k), cut(b), brefs, starts, places)
    o = o_inter + o_intra.reshape(c, HEAD)
    bend = row(b, c - 1)
    st_new = st * jnp.exp(bend) + mm_tn(v, k * jnp.exp(bend - b))
    y = _rms(o, nw) * _silu(g_raw)
    return (y,), st_new


def ssd_chunk(seq, hp, sp, st):
    xs, bm, cm, dtc, dtr = seq
    dt_bias, a_log = hp
    c, width = xs.shape
    nheads = width // SSM_P
    head_of = _iota((1, width), 1) // SSM_P
    r, s = _iota((c, c), 0), _iota((c, c), 1)
    g = mm_nt(cm, bm)
    dt_l, cum_l, end_l, scores = 0.0, 0.0, 0.0, []
    for i in range(nheads):
        neg_a = -jnp.exp(a_log[i])
        dt_col = _softplus(dtc[i] + dt_bias[i])
        dt_row = _softplus(dtr[i] + dt_bias[i])
        cum_col, cum_row, total = _cum_col_row(neg_a * dt_col, neg_a * dt_row)
        mine = head_of == i
        dt_l = dt_l + jnp.where(mine, dt_col, 0.0)
        cum_l = cum_l + jnp.where(mine, cum_col, 0.0)
        end_l = end_l + jnp.where(mine, total, 0.0)
        scores.append(g * _masked_exp(cum_col - cum_row, s <= r))
    xdt = xs * dt_l
    stacked = mm_nn(jnp.concatenate(scores, axis=0), xdt)
    y_intra = 0.0
    for i in range(nheads):
        y_intra = y_intra + jnp.where(head_of == i, stacked[i * c:(i + 1) * c], 0.0)
    y_inter = mm_nn(cm, st) * jnp.exp(cum_l)
    st_new = st * jnp.exp(end_l) + mm_tn(bm, xdt * jnp.exp(end_l - cum_l))
    return (y_intra + y_inter,), st_new


def _neumann_inverse(a):
    n = a.shape[0]
    eye = (_iota((n, n), 0) == _iota((n, n), 1)).astype(F32)
    p = -a
    t = eye + p
    for _ in range(int(math.log2(n)) - 1):
        p = _hdg(p, p, 1, 0)
        t = t + _hdg(t, p, 1, 0)
    return t


@jax.custom_vjp
def inv_unit_lower(a):
    return _neumann_inverse(a)


def _inv_fwd(a):
    t = _neumann_inverse(a)
    return t, t


inv_unit_lower.defvjp(_inv_fwd, lambda t, g: (-hd_nt(hd_tn(t, g), t),))


def gdn_chunk(seq, hp, sp, st):
    q_raw, k_raw, v, z, gbc, gac, gar = seq
    dt_bias, a_log = hp
    (nw,) = sp
    c = v.shape[0]
    r, s = _iota((c, c), 0), _iota((c, c), 1)
    q = q_raw * lax.rsqrt(jnp.sum(q_raw * q_raw, axis=-1, keepdims=True) + NORM_EPS) * (HEAD ** -0.5)
    k = k_raw * lax.rsqrt(jnp.sum(k_raw * k_raw, axis=-1, keepdims=True) + NORM_EPS)
    beta = _sigmoid(gbc)
    neg_a = -jnp.exp(a_log)
    cum, cum_row, total = _cum_col_row(neg_a * _softplus(gac + dt_bias), neg_a * _softplus(gar + dt_bias))
    decay = _masked_exp(cum - cum_row, s <= r)
    kk = mm_nt(k, k)
    a_low = jnp.where(s < r, beta * kk * decay, 0.0)
    sol = hd_nn(inv_unit_lower(a_low), jnp.concatenate([v * beta, k * (beta * jnp.exp(cum))], axis=1))
    u_base, w_corr = sol[:, :HEAD], sol[:, HEAD:]
    qk = mm_nt(q, k) * decay
    u = u_base - mm_nn(w_corr, st)
    o = mm_nn(q * jnp.exp(cum), st) + mm_nn(qk, u)
    st_new = jnp.exp(total) * st + mm_tn(k * jnp.exp(total - cum), u)
    y = _rms(o, nw) * _silu(z)
    return (y,), st_new


def normmod_fn(rows, params):
    (x,), (nw, sc, sh) = rows, params
    return (_rms(x, nw) * (1.0 + sc) + sh,)


def ssmpost_fn(rows, params):
    (y, xs, z), (d_exp, nw) = rows, params
    y = (y + d_exp * xs) * _silu(z)
    gw = y.shape[1] // 2
    return (jnp.concatenate([_rms(y[:, :gw], nw[:, :gw]), _rms(y[:, gw:], nw[:, gw:])], axis=1),)


def merge_fn(rows, params):
    (yh, ys, yg, gl), (bm, wb) = rows, params
    d = wb.shape[2]
    gates = _sigmoid(gl + bm)
    out = 0.0
    for n, y in enumerate((yh, ys, yg)):
        out = out + gates[:, n * d:(n + 1) * d] * mm_nn(y, wb[n])
    return (out,)


def outproj_fn(rows, params):
    (m, x), (g1, w) = rows, params
    return (x + (1.0 + g1) * mm_nn(m, w),)


def resid_fn(rows, params):
    (x, o), (g2,) = rows, params
    return (x + (1.0 + g2) * o,)


def _params(sem, side_effects=False):
    return pltpu.CompilerParams(dimension_semantics=sem, vmem_limit_bytes=VMEM_LIMIT, has_side_effects=side_effects)


def _whole(a):
    nd = a.ndim
    return pl.BlockSpec(a.shape, lambda *_: (0,) * nd)


def _pick(n, cands):
    for c in cands:
        if n % c == 0:
            return c
    return n


def matmul(name, a, b, mode, out_dtype, side=None):
    if mode == "nn":
        (m, k), n = a.shape, b.shape[1]
    elif mode == "nt":
        (m, k), n = a.shape, b.shape[0]
    else:
        (k, m), n = a.shape, b.shape[1]
    tm = _pick(m, (512, 256, 128))
    tn = _pick(n, (1280, 1024, 1408, 768, 512, 384, 256, 128))
    tk = _pick(k, (1024, 1280, 1408, 768, 512, 256, 128))
    if mode == "tn":
        tm = _pick(m, (1024, 768, 512, 256, 128))
        tk = _pick(k, (512, 256, 128))
    nk = k // tk
    ca, cb = {"nn": (1, 0), "nt": (1, 1), "tn": (0, 0)}[mode]

    def core(a_ref, b_ref, o_ref, acc_ref):
        kk = pl.program_id(2)

        @pl.when(kk == 0)
        def _():
            acc_ref[...] = jnp.zeros_like(acc_ref)

        acc_ref[...] += _dg(a_ref[...], b_ref[...], ca, cb)

        @pl.when(kk == nk - 1)
        def _():
            o_ref[...] = acc_ref[...].astype(o_ref.dtype)

    a_spec = (pl.BlockSpec((tk, tm), lambda i, j, q: (q, i)) if mode == "tn"
              else pl.BlockSpec((tm, tk), lambda i, j, q: (i, q)))
    b_spec = (pl.BlockSpec((tn, tk), lambda i, j, q: (j, q)) if mode == "nt"
              else pl.BlockSpec((tk, tn), lambda i, j, q: (q, j)))
    grid = (m // tm, n // tn, nk)
    body, s_in, s_out, s_shape, s_scr, s_args = _with_side(core, 2, 1, side, grid)
    sem = ("arbitrary",) * 3 if side else ("parallel", "parallel", "arbitrary")
    res = pl.pallas_call(
        body, name=name, grid=grid,
        in_specs=[a_spec, b_spec] + s_in,
        out_specs=[pl.BlockSpec((tm, tn), lambda i, j, q: (i, j))] + s_out,
        out_shape=[jax.ShapeDtypeStruct((m, n), out_dtype)] + s_shape,
        scratch_shapes=[pltpu.VMEM((tm, tn), F32)] + s_scr,
        compiler_params=_params(sem, side is not None),
    )(a, b, *s_args)
    return (res[0], res[1:]) if side else res[0]


def bmatmul(name, a, b, mode, out_dtype, out_batched):
    ab, bb = a.ndim == 3, b.ndim == 3
    nb = a.shape[0] if ab else b.shape[0]
    a2, b2 = a.shape[-2:], b.shape[-2:]
    if mode == "nn":
        (m, k), n = a2, b2[1]
    elif mode == "nt":
        (m, k), n = a2, b2[0]
    else:
        (k, m), n = a2, b2[1]
    tm = _pick(m, (1024, 512, 256, 128) if mode == "tn" else (512, 256, 128))
    tn = _pick(n, (1024, 512, 256, 128))
    tk = _pick(k, (512, 256, 128) if mode == "tn" else (1024, 512, 256, 128))
    nk = k // tk
    ca, cb = {"nn": (1, 0), "nt": (1, 1), "tn": (0, 0)}[mode]
    ids = (lambda g: g) if out_batched else (lambda g: (g[2], g[0], g[1], g[3]))
    grid = (nb, m // tm, n // tn, nk) if out_batched else (m // tm, n // tn, nb, nk)

    def a_map(*g):
        bi, i, j, q = ids(g)
        idx = (q, i) if mode == "tn" else (i, q)
        return (bi,) + idx if ab else idx

    def b_map(*g):
        bi, i, j, q = ids(g)
        idx = (j, q) if mode == "nt" else (q, j)
        return (bi,) + idx if bb else idx

    def o_map(*g):
        bi, i, j, q = ids(g)
        return (bi, i, j) if out_batched else (i, j)

    def body(a_ref, b_ref, o_ref, acc_ref):
        bi, _, _, q = ids(tuple(pl.program_id(d) for d in range(4)))
        first = (q == 0) if out_batched else (q == 0) & (bi == 0)
        last = (q == nk - 1) if out_batched else (q == nk - 1) & (bi == nb - 1)

        @pl.when(first)
        def _():
            acc_ref[...] = jnp.zeros_like(acc_ref)

        acc_ref[...] += _dg(a_ref[...], b_ref[...], ca, cb)

        @pl.when(last)
        def _():
            o_ref[...] = acc_ref[...].astype(o_ref.dtype)

    a_blk = (tk, tm) if mode == "tn" else (tm, tk)
    b_blk = (tn, tk) if mode == "nt" else (tk, tn)
    return pl.pallas_call(
        body, name=name, grid=grid,
        in_specs=[pl.BlockSpec(((None,) if ab else ()) + a_blk, a_map), pl.BlockSpec(((None,) if bb else ()) + b_blk, b_map)],
        out_specs=pl.BlockSpec(((None,) if out_batched else ()) + (tm, tn), o_map),
        out_shape=jax.ShapeDtypeStruct(((nb,) if out_batched else ()) + (m, n), out_dtype),
        scratch_shapes=[pltpu.VMEM((tm, tn), F32)],
        compiler_params=_params(("parallel", "parallel", "arbitrary", "arbitrary")),
    )(a, b)


def colgather(name, src, idx, dst_w, out_dtype):
    nsrc, rows, w = src.shape
    tw = GATHER_TILE
    nbs = -(-w // tw)
    ne = idx.shape[0]
    nbd = idx.shape[1] // tw
    tiles = [sorted(set((idx[e, t * tw:(t + 1) * tw][idx[e, t * tw:(t + 1) * tw] >= 0] // tw).tolist()))
             for e in range(ne) for t in range(nbd)]
    nslot = max(1, max(len(t) for t in tiles))
    tbl = np.full((ne * nbd, nslot), -1, np.int32)
    for i, t in enumerate(tiles):
        tbl[i, :len(t)] = t
    exact3 = src.dtype == F32

    def body(tbl_ref, idx_ref, src_ref, o_ref, acc_ref):
        ti, si = pl.program_id(0), pl.program_id(1)

        @pl.when(si == 0)
        def _():
            acc_ref[...] = jnp.zeros_like(acc_ref)

        t = tbl_ref[ti * nslot + si]

        @pl.when(t >= 0)
        def _():
            onehot = ((_iota((tw, tw), 0) + t * tw) == idx_ref[...]).astype(BF16)
            col = _iota((1, tw), 1) + (t % nbs) * tw
            xv = jnp.where(col < w, src_ref[...], jnp.zeros((), src_ref.dtype))
            d = lambda p: lax.dot_general(p, onehot, (((1,), (0,)), ((), ())), preferred_element_type=F32)
            if exact3:
                x1, x2, x3 = _split3(xv)
                acc_ref[...] += (d(x3) + d(x2)) + d(x1)
            else:
                acc_ref[...] += d(xv)

        @pl.when(si == nslot - 1)
        def _():
            o_ref[...] = acc_ref[...].astype(o_ref.dtype)

    def src_map(ti, si, tbl_ref):
        t = jnp.maximum(tbl_ref[ti * nslot + si], 0)
        return (t // nbs, 0, t % nbs)

    grid_spec = pltpu.PrefetchScalarGridSpec(
        num_scalar_prefetch=1, grid=(ne * nbd, nslot),
        in_specs=[pl.BlockSpec((None, 1, tw), lambda ti, si, tbl_ref: (ti // nbd, 0, ti % nbd)),
                  pl.BlockSpec((None, rows, tw), src_map)],
        out_specs=pl.BlockSpec((None, rows, tw), lambda ti, si, tbl_ref: (ti // nbd, 0, ti % nbd)),
        scratch_shapes=[pltpu.VMEM((rows, tw), F32)])
    return pl.pallas_call(
        body, name=name, grid_spec=grid_spec,
        out_shape=jax.ShapeDtypeStruct((ne, rows, dst_w), out_dtype),
        compiler_params=_params(("parallel", "arbitrary")),
    )(jnp.asarray(tbl.reshape(-1)), jnp.asarray(idx.reshape(ne, 1, nbd * tw).astype(np.int32)), src)


def swiglu3_fwd(name, gu, tm):
    _, nb, s, w = gu.shape

    def body(x_ref, o_ref):
        o_ref[...] = (_silu(x_ref[0].astype(F32)) * x_ref[1].astype(F32)).astype(o_ref.dtype)

    return pl.pallas_call(
        body, name=name, grid=(nb, s // tm),
        in_specs=[pl.BlockSpec((2, None, tm, w), lambda b, i: (0, b, i, 0))],
        out_specs=pl.BlockSpec((None, tm, w), lambda b, i: (b, i, 0)),
        out_shape=jax.ShapeDtypeStruct((nb, s, w), BF16),
        compiler_params=_params(("parallel", "parallel")),
    )(gu)


def swiglu3_bwd(name, gu, dact, tm):
    _, nb, s, w = gu.shape

    def body(x_ref, g_ref, o_ref):
        _, vjp = jax.vjp(lambda a, b: _silu(a) * b, x_ref[0].astype(F32), x_ref[1].astype(F32))
        dg, du = vjp(g_ref[...].astype(F32))
        o_ref[0] = dg.astype(o_ref.dtype)
        o_ref[1] = du.astype(o_ref.dtype)

    return pl.pallas_call(
        body, name=name, grid=(nb, s // tm),
        in_specs=[pl.BlockSpec((2, None, tm, w), lambda b, i: (0, b, i, 0)),
                  pl.BlockSpec((None, tm, w), lambda b, i: (b, i, 0))],
        out_specs=pl.BlockSpec((2, None, tm, w), lambda b, i: (0, b, i, 0)),
        out_shape=jax.ShapeDtypeStruct(gu.shape, BF16),
        compiler_params=_params(("parallel", "parallel")),
    )(gu, dact)


def _row_specs(rows, tm):
    return [pl.BlockSpec((tm, w), lambda i, _c=c: (i, _c)) for (_, w, c) in rows]


def rowstage_fwd(name, fn, rows, params, outs, tm):
    s = rows[0][0].shape[0]
    nr, npar = len(rows), len(params)

    def body(*refs):
        r = [x[...].astype(F32) for x in refs[:nr]]
        p = [x[...].astype(F32) for x in refs[nr:nr + npar]]
        for ref, val in zip(refs[nr + npar:], fn(r, p)):
            ref[...] = val.astype(ref.dtype)

    res = pl.pallas_call(
        body, name=name, grid=(s // tm,),
        in_specs=_row_specs(rows, tm) + [_whole(p) for p in params],
        out_specs=[pl.BlockSpec((tm, w), lambda i: (i, 0)) for (w, _) in outs],
        out_shape=[jax.ShapeDtypeStruct((s, w), dt) for (w, dt) in outs],
        compiler_params=_params(("parallel",)),
    )(*[r[0] for r in rows], *params)
    return res


def rowstage_bwd(name, fn, rows, params, douts, drow_dtypes, tm, adds=None, into=None):
    s = rows[0][0].shape[0]
    nr, npar, no = len(rows), len(params), len(douts)
    adds = adds or {}
    add_idx = sorted(adds)
    na = len(add_idx)
    into = into or {}
    into_idx = sorted(into)
    nb = len(into_idx)

    def body(*refs):
        r = [x[...].astype(F32) for x in refs[:nr]]
        p = [x[...].astype(F32) for x in refs[nr:nr + npar]]
        g = [x[...].astype(F32) for x in refs[nr + npar:nr + npar + no]]
        a_refs = refs[nr + npar + no:nr + npar + no + na]
        dr_refs = refs[nr + npar + no + na + nb:nr + npar + no + na + nb + nr]
        dp_refs = refs[nr + npar + no + na + nb + nr:]
        _, vjp = jax.vjp(lambda r_, p_: tuple(fn(r_, p_)), r, p)
        dr, dp = vjp(tuple(g))
        for j, (ref, val) in enumerate(zip(dr_refs, dr)):
            if j in adds:
                val = val + a_refs[add_idx.index(j)][...].astype(F32)
            ref[...] = val.astype(ref.dtype)

        @pl.when(pl.program_id(0) == 0)
        def _():
            for ref in dp_refs:
                ref[...] = jnp.zeros_like(ref)

        for ref, val in zip(dp_refs, dp):
            ref[...] += val

    res = pl.pallas_call(
        body, name=name, grid=(s // tm,),
        in_specs=(_row_specs(rows, tm) + [_whole(p) for p in params]
                  + [pl.BlockSpec((tm, d.shape[1]), lambda i: (i, 0)) for d in douts]
                  + [pl.BlockSpec((tm, rows[j][1]), lambda i: (i, 0)) for j in add_idx]
                  + [pl.BlockSpec(memory_space=pl.ANY)] * nb),
        out_specs=([pl.BlockSpec((tm, w), lambda i, _c=(into[j][1] if j in into else 0): (i, _c))
                    for j, (_, w, _) in enumerate(rows)] + [_whole(p) for p in params]),
        out_shape=([jax.ShapeDtypeStruct(into[j][0].shape if j in into else (s, w), dt)
                    for j, ((_, w, _), dt) in enumerate(zip(rows, drow_dtypes))]
                   + [jax.ShapeDtypeStruct(p.shape, F32) for p in params]),
        input_output_aliases={nr + npar + no + na + k: j for k, j in enumerate(into_idx)},
        compiler_params=_params(("arbitrary",)),
    )(*[r[0] for r in rows], *params, *douts, *[adds[j] for j in add_idx], *[into[j][0] for j in into_idx])
    return res[:nr], res[nr:]


def _flip(index_map, nc):
    return lambda h, n: index_map(h, nc - 1 - n)


def _with_side(core, n_in, n_out, side, grid):
    if side is None:
        return core, [], [], [], [], ()
    sends, broadcast = side
    k = len(sends)

    def body(*refs):
        ins, snd = refs[:n_in], refs[n_in:n_in + k]
        outs, rcv = refs[n_in + k:n_in + k + n_out], refs[n_in + k + n_out:n_in + 2 * k + n_out]
        scr = refs[n_in + 2 * k + n_out:]
        start, wait = _exchange_ops(snd, rcv, *scr[1:], broadcast)
        ids = [pl.program_id(d) for d in range(len(grid))]
        first = functools.reduce(lambda a, b: a & b, [i == 0 for i in ids])
        last = functools.reduce(lambda a, b: a & b, [i == g - 1 for i, g in zip(ids, grid)])
        pl.when(first)(start)
        core(*ins, *outs, scr[0])
        pl.when(last)(wait)

    return body, [HBM_SPEC] * k, [HBM_SPEC] * k, _exchange_out(sends, broadcast), _exchange_sems(k), tuple(sends)


def _take(v, split, j):
    if split is None:
        return v
    if split[0] == "lane":
        return v[:, j * split[1]:(j + 1) * split[1]]
    if split[0] == "lead":
        return v[j * split[1]:(j + 1) * split[1]]
    return v[j]


def _heads(vals, specs, hb):
    return [v if s[-1] is None else jnp.stack([_take(v, s[-1], j) for j in range(hb)]) for v, s in zip(vals, specs)]


def _over_heads(chunk_fn, hb, seqs, hparams, batched):
    seq_ax = [None if s[3] is None else 0 for s in seqs]
    hp_ax = [None if s[3] is None else 0 for s in hparams]
    if batched:
        return jax.vmap(chunk_fn, in_axes=(seq_ax, hp_ax, None, 0))

    def looped(seq, hp, sp, st):
        pick = lambda vals, axes, j: [v if a is None else v[j] for v, a in zip(vals, axes)]
        res = [chunk_fn(pick(seq, seq_ax, j), pick(hp, hp_ax, j), sp, st[j]) for j in range(hb)]
        pile = lambda parts: jnp.concatenate([p[None] for p in parts], axis=0)
        return tuple(pile(o) for o in zip(*[r[0] for r in res])), pile([r[1] for r in res])

    return looped


def _where(split, j):
    if split[0] == "lane":
        return (slice(None), slice(j * split[1], (j + 1) * split[1]))
    if split[0] == "lead":
        return (slice(j * split[1], (j + 1) * split[1]),)
    return (j,)


def scan_fwd(name, chunk_fn, nblk, hb, nc, seqs, hparams, sparams, state_shape, outs, batched, side=None):
    ns, nhp, nsp, no = len(seqs), len(hparams), len(sparams), len(outs)

    def core(*refs):
        seq_r, hp_r, sp_r = refs[:ns], refs[ns:ns + nhp], refs[ns + nhp:ns + nhp + nsp]
        out_r = refs[ns + nhp + nsp:ns + nhp + nsp + no]
        st_out, st_scr = refs[-2], refs[-1]

        @pl.when(pl.program_id(1) == 0)
        def _():
            st_scr[...] = jnp.zeros_like(st_scr)

        seq_v = [x[...].astype(F32) for x in seq_r]
        hp_v = [x[...] for x in hp_r]
        sp_v = [x[...] for x in sp_r]
        st = st_scr[...]
        st_out[...] = st
        heads = _over_heads(chunk_fn, hb, seqs, hparams, batched)
        o, st_new = heads(_heads(seq_v, seqs, hb), _heads(hp_v, hparams, hb), sp_v, st)
        for ref, spec, val in zip(out_r, outs, o):
            for j in range(hb):
                ref[_where(spec[4], j)] = val[j].astype(ref.dtype)
        st_scr[...] = st_new

    nst = len(state_shape)
    body, s_in, s_out, s_shape, s_scr, s_args = _with_side(core, ns + nhp + nsp, no + 1, side, (nblk, nc))
    res = pl.pallas_call(
        body, name=name, grid=(nblk, nc),
        in_specs=([pl.BlockSpec(bs, im) for (_, bs, im, _) in seqs]
                  + [pl.BlockSpec(bs, lambda h, n, _im=im: _im(h)) for (_, bs, im, _) in hparams]
                  + [_whole(p) for p in sparams] + s_in),
        out_specs=([pl.BlockSpec(bs, im) for (_, _, bs, im, _) in outs]
                   + [pl.BlockSpec((hb, None) + tuple(state_shape), lambda h, n: (h, n) + (0,) * nst)] + s_out),
        out_shape=([jax.ShapeDtypeStruct(fs, dt) for (fs, dt, _, _, _) in outs]
                   + [jax.ShapeDtypeStruct((nblk * hb, nc) + tuple(state_shape), F32)] + s_shape),
        scratch_shapes=[pltpu.VMEM((hb,) + tuple(state_shape), F32)] + s_scr,
        compiler_params=_params(("arbitrary", "arbitrary"), side is not None),
    )(*[x[0] for x in seqs], *[x[0] for x in hparams], *sparams, *s_args)
    return res[:no], res[no], res[no + 1:]


def scan_bwd(name, chunk_fn, nblk, hb, nc, seqs, hparams, sparams, state_shape, states, douts, dseqs, batched, side=None):
    ns, nhp, nsp, no = len(seqs), len(hparams), len(sparams), len(douts)
    nst = len(state_shape)
    buf_of = [i for i, sp in enumerate(dseqs) if len(sp) > 5 and sp[5] is not None]
    bufs = [dseqs[i][5] for i in buf_of]

    def core(*refs):
        seq_r, hp_r, sp_r = refs[:ns], refs[ns:ns + nhp], refs[ns + nhp:ns + nhp + nsp]
        base = ns + nhp + nsp
        st_r = refs[base]
        do_r = refs[base + 1:base + 1 + no]
        base += 1 + no + len(bufs)
        ds_r, dhp_r, dsp_r = refs[base:base + ns], refs[base + ns:base + ns + nhp], refs[base + ns + nhp:base + ns + nhp + nsp]
        dst_scr = refs[-1]
        h, n = pl.program_id(0), pl.program_id(1)

        @pl.when(n == 0)
        def _():
            dst_scr[...] = jnp.zeros_like(dst_scr)
            for ref in dhp_r:
                ref[...] = jnp.zeros_like(ref)

        @pl.when((n == 0) & (h == 0))
        def _():
            for ref in dsp_r:
                ref[...] = jnp.zeros_like(ref)

        seq_v = [x[...].astype(F32) for x in seq_r]
        hp_v = [x[...] for x in hp_r]
        sp_v = [x[...] for x in sp_r]
        do_v = [x[...].astype(F32) for x in do_r]
        prim = (_heads(seq_v, seqs, hb), _heads(hp_v, hparams, hb), sp_v, st_r[...])
        _, vjp = jax.vjp(_over_heads(chunk_fn, hb, seqs, hparams, batched), *prim)
        ds, dhp, dsp, dst = vjp((tuple(_heads(do_v, douts, hb)), dst_scr[...]))
        for ref, spec, val in zip(ds_r, dseqs, ds):
            if spec[4] is None:
                ref[...] = val.astype(ref.dtype)
            else:
                for j in range(hb):
                    ref[_where(spec[4], j)] = val[j].astype(ref.dtype)
        for ref, spec, val in zip(dhp_r, hparams, dhp):
            for j in range(hb):
                ref[_where(spec[3], j)] += val[j]
        for ref, val in zip(dsp_r, dsp):
            ref[...] += val
        dst_scr[...] = dst

    n_in, n_out = ns + nhp + nsp + 1 + no + len(bufs), ns + nhp + nsp
    body, s_in, s_out, s_shape, s_scr, s_args = _with_side(core, n_in, n_out, side, (nblk, nc))
    res = pl.pallas_call(
        body, name=name, grid=(nblk, nc),
        in_specs=([pl.BlockSpec(bs, _flip(im, nc)) for (_, bs, im, _) in seqs]
                  + [pl.BlockSpec(bs, lambda h, n, _im=im: _im(h)) for (_, bs, im, _) in hparams]
                  + [_whole(p) for p in sparams]
                  + [pl.BlockSpec((hb, None) + tuple(state_shape), lambda h, n: (h, nc - 1 - n) + (0,) * nst)]
                  + [pl.BlockSpec(bs, _flip(im, nc)) for (_, bs, im, _) in douts]
                  + [pl.BlockSpec(memory_space=pl.ANY)] * len(bufs) + s_in),
        out_specs=([pl.BlockSpec(sp[2], _flip(sp[3], nc)) for sp in dseqs]
                   + [pl.BlockSpec(bs, lambda h, n, _im=im: _im(h)) for (_, bs, im, _) in hparams]
                   + [_whole(p) for p in sparams] + s_out),
        out_shape=([jax.ShapeDtypeStruct(sp[0], sp[1]) for sp in dseqs]
                   + [jax.ShapeDtypeStruct(x[0].shape, F32) for x in hparams]
                   + [jax.ShapeDtypeStruct(p.shape, F32) for p in sparams] + s_shape),
        scratch_shapes=[pltpu.VMEM((hb,) + tuple(state_shape), F32)] + s_scr,
        input_output_aliases={n_in - len(bufs) + k: i for k, i in enumerate(buf_of)},
        compiler_params=_params(("arbitrary", "arbitrary"), side is not None),
    )(*[x[0] for x in seqs], *[x[0] for x in hparams], *sparams, states, *[x[0] for x in douts], *bufs, *s_args)
    return res[:ns], res[ns:ns + nhp], res[ns + nhp:n_out], res[n_out:]


def _shift_down(x, n, rows):
    if n == 0:
        return x
    return jnp.where(rows >= n, pltpu.roll(x, n, 0), 0.0)


def _shift_up(x, n, rows):
    if n == 0:
        return x
    s = x.shape[0]
    return jnp.where(rows < s - n, pltpu.roll(x, s - n, 0), 0.0)


def conv_fwd(name, x, col0, w, b):
    s, cw = x.shape[0], w.shape[1]

    def body(x_ref, w_ref, b_ref, o_ref):
        xv = x_ref[...]
        rows = _iota(xv.shape, 0)
        u = jnp.broadcast_to(b_ref[...], xv.shape)
        for j in range(CONV_K):
            u = u + w_ref[j:j + 1, :] * _shift_down(xv, CONV_K - 1 - j, rows)
        o_ref[...] = _silu(u)

    return pl.pallas_call(
        body, name=name, grid=(cw // LANES,),
        in_specs=[pl.BlockSpec((s, LANES), lambda j: (0, col0 + j)),
                  pl.BlockSpec((CONV_K, LANES), lambda j: (0, j)),
                  pl.BlockSpec((1, LANES), lambda j: (0, j))],
        out_specs=pl.BlockSpec((s, LANES), lambda j: (0, j)),
        out_shape=jax.ShapeDtypeStruct((s, cw), F32),
        compiler_params=_params(("parallel",)),
    )(x, w, b)


def conv_bwd(name, x, col0, w, b, dout, into):
    s, cw = x.shape[0], w.shape[1]

    def body(x_ref, w_ref, b_ref, g_ref, into_ref, dx_ref, dw_ref, db_ref):
        xv = x_ref[...]
        rows = _iota(xv.shape, 0)
        sh = [_shift_down(xv, CONV_K - 1 - j, rows) for j in range(CONV_K)]
        u = jnp.broadcast_to(b_ref[...], xv.shape)
        for j in range(CONV_K):
            u = u + w_ref[j:j + 1, :] * sh[j]
        sg = _sigmoid(u)
        du = g_ref[...] * (sg * (1.0 + u * (1.0 - sg)))
        dx = jnp.zeros_like(xv)
        for j in range(CONV_K):
            dx = dx + w_ref[j:j + 1, :] * _shift_up(du, CONV_K - 1 - j, rows)
            dw_ref[j:j + 1, :] = jnp.sum(du * sh[j], axis=0, keepdims=True)
        dx_ref[...] = dx.astype(dx_ref.dtype)
        db_ref[...] = jnp.sum(du, axis=0, keepdims=True)

    return pl.pallas_call(
        body, name=name, grid=(cw // LANES,),
        in_specs=[pl.BlockSpec((s, LANES), lambda j: (0, col0 + j)),
                  pl.BlockSpec((CONV_K, LANES), lambda j: (0, j)),
                  pl.BlockSpec((1, LANES), lambda j: (0, j)),
                  pl.BlockSpec((s, LANES), lambda j: (0, j)),
                  pl.BlockSpec(memory_space=pl.ANY)],
        out_specs=[pl.BlockSpec((s, LANES), lambda j: (0, col0 + j)),
                   pl.BlockSpec((CONV_K, LANES), lambda j: (0, j)),
                   pl.BlockSpec((1, LANES), lambda j: (0, j))],
        out_shape=[jax.ShapeDtypeStruct(into.shape, into.dtype), jax.ShapeDtypeStruct((CONV_K, cw), F32),
                   jax.ShapeDtypeStruct((1, cw), F32)],
        input_output_aliases={4: 0},
        compiler_params=_params(("parallel",)),
    )(x, w, b, dout, into)


def exchange(name, sends, broadcast):
    nop = len(sends)

    def body(*refs):
        start, wait = _exchange_ops(refs[:nop], refs[nop:2 * nop], *refs[2 * nop:], broadcast)
        start()
        wait()

    return pl.pallas_call(
        body, name=name,
        in_specs=[HBM_SPEC] * nop, out_specs=[HBM_SPEC] * nop,
        out_shape=_exchange_out(sends, broadcast), scratch_shapes=_exchange_sems(nop),
        compiler_params=pltpu.CompilerParams(has_side_effects=True),
    )(*sends)


HBM_SPEC = pl.BlockSpec(memory_space=pltpu.HBM)


def _exchange_out(sends, broadcast):
    return [jax.ShapeDtypeStruct((N_DEV,) + tuple(t.shape if broadcast else t.shape[1:]), t.dtype) for t in sends]


def _exchange_sems(nop):
    return [pltpu.SemaphoreType.DMA((nop * (N_DEV - 1),)), pltpu.SemaphoreType.DMA((nop * (N_DEV - 1),)),
            pltpu.SemaphoreType.DMA((nop,))]


def _exchange_ops(send_refs, recv_refs, send_sems, recv_sems, local_sems, broadcast):
    nop = len(send_refs)
    x, y, c = lax.axis_index("x"), lax.axis_index("y"), lax.axis_index("c")
    me = 4 * x + 2 * y + c
    peers = []
    for k in range(1, N_DEV):
        px = 1 - x if (k >> 2) & 1 else x
        py = 1 - y if (k >> 1) & 1 else y
        pc = 1 - c if k & 1 else c
        peers.append(((px, py, pc), 4 * px + 2 * py + pc))

    def src(i, peer):
        return send_refs[i] if broadcast else send_refs[i].at[peer]

    def remote(i, k, arrival):
        dev, peer = peers[k]
        return pltpu.make_async_remote_copy(
            src_ref=src(i, peer), dst_ref=recv_refs[i].at[peer if arrival else me],
            send_sem=send_sems.at[i * (N_DEV - 1) + k], recv_sem=recv_sems.at[i * (N_DEV - 1) + k],
            device_id=dev, device_id_type=pl.DeviceIdType.MESH)

    def local(i):
        return pltpu.make_async_copy(src(i, me), recv_refs[i].at[me], local_sems.at[i])

    def start():
        for i in range(nop):
            local(i).start()
        for k in range(N_DEV - 1):
            for i in range(nop):
                remote(i, k, False).start()

    def wait():
        for k in range(N_DEV - 1):
            for i in range(nop):
                remote(i, k, True).wait_recv()
        for k in range(N_DEV - 1):
            for i in range(nop):
                remote(i, k, False).wait_send()
        for i in range(nop):
            local(i).wait()

    return start, wait


def adamw_sum(name, parts, w, m, v, layer=None, into=None):
    rws, cols = w.shape[-2:]
    nsum = parts.shape[0]
    tr = _pick(rws, (256, 128, 64, 32, 16, 8))
    c1 = 1.0 / (1.0 - ADAM_B1 ** ADAM_STEP)
    c2 = 1.0 / (1.0 - ADAM_B2 ** ADAM_STEP)

    def body(p_ref, w_ref, m_ref, v_ref, *rest):
        g_ref, d_ref, nm_ref, nv_ref = rest[-4:]
        g = p_ref[0]
        for j in range(1, nsum):
            g = g + p_ref[j]
        nm = ADAM_B1 * m_ref[...] + (1.0 - ADAM_B1) * g
        nv = ADAM_B2 * v_ref[...] + (1.0 - ADAM_B2) * (g * g)
        g_ref[...] = g
        nm_ref[...] = nm
        nv_ref[...] = nv
        d_ref[...] = -ADAM_LR * ((nm * c1) / (jnp.sqrt(nv * c2) + ADAM_EPS) + ADAM_WD * w_ref[...])

    if layer is None:
        blk = pl.BlockSpec((tr, cols), lambda i: (i, 0))
    else:
        blk = pl.BlockSpec((None, tr, cols), lambda i: (layer, i, 0))
    if into is None and layer is not None:
        into = [lax.empty(w.shape, F32) for _ in range(4)]
    extra = list(into) if into else []
    return pl.pallas_call(
        body, name=name, grid=(rws // tr,),
        in_specs=([pl.BlockSpec((nsum, tr, cols), lambda i: (0, i, 0)), blk, blk, blk]
                  + [pl.BlockSpec(memory_space=pl.ANY)] * len(extra)),
        out_specs=[blk, blk, blk, blk],
        out_shape=[jax.ShapeDtypeStruct(w.shape, F32)] * 4,
        input_output_aliases={4 + k: k for k in range(len(extra))},
        compiler_params=_params(("parallel",)),
    )(parts, w, m, v, *extra)


def ada_fwd(name, c_all, w, b):
    nl = w.shape[0]

    def body(c_ref, w_ref, b_ref, o_ref):
        ca = _silu(c_ref[...])
        for l in range(nl):
            o_ref[l] = mm_nn(ca, w_ref[l]) + b_ref[l]

    return pl.pallas_call(
        body, name=name,
        out_shape=jax.ShapeDtypeStruct((nl, c_all.shape[0], w.shape[2]), F32),
        compiler_params=pltpu.CompilerParams(vmem_limit_bytes=VMEM_LIMIT),
    )(c_all, w, b)


def ada_bwd(name, c_all, dmod):
    nl = dmod.shape[0]

    def body(c_ref, g_ref, o_ref):
        ca = _silu(c_ref[...])
        for l in range(nl):
            o_ref[l] = mm_tn(ca, g_ref[l])

    return pl.pallas_call(
        body, name=name,
        out_shape=jax.ShapeDtypeStruct((nl, c_all.shape[1], dmod.shape[2]), F32),
        compiler_params=pltpu.CompilerParams(vmem_limit_bytes=VMEM_LIMIT),
    )(c_all, dmod)


def lower_bounds_fn(rows, params):
    (lg,), _ = rows, params
    nl = lg.shape[0]
    mx = jnp.max(lg, axis=0, keepdims=True)
    e = jnp.exp(lg - mx)
    p = e / jnp.sum(e, axis=0, keepdims=True)
    layer = _iota((nl, 1), 0)
    acc = jnp.zeros_like(p)
    for j in range(1, nl):
        pj = jnp.sum(jnp.where(layer == j, p, 0.0), axis=0, keepdims=True)
        acc = acc + jnp.where(layer >= j, 1.0, 0.0) * pj
    return (acc,)


def loss_call(name, x, tgt, nw, tm):
    s, d = x.shape

    def body(x_ref, t_ref, w_ref, l_ref, dx_ref, dw_ref):
        def f(xv, wv):
            err = _rms(xv, wv) - t_ref[...]
            return jnp.sum(0.5 * jnp.mean(err * err, axis=-1, keepdims=True), axis=0, keepdims=True)

        val, vjp = jax.vjp(f, x_ref[...], w_ref[...])
        dx, dw = vjp(jnp.ones_like(val))

        @pl.when(pl.program_id(0) == 0)
        def _():
            l_ref[...] = jnp.zeros_like(l_ref)
            dw_ref[...] = jnp.zeros_like(dw_ref)

        l_ref[...] += jnp.broadcast_to(val, l_ref.shape)
        dw_ref[...] += dw
        dx_ref[...] = dx

    row = pl.BlockSpec((tm, d), lambda i: (i, 0))
    return pl.pallas_call(
        body, name=name, grid=(s // tm,),
        in_specs=[row, row, _whole(nw)],
        out_specs=[pl.BlockSpec((8, LANES), lambda i: (0, 0)), row, _whole(nw)],
        out_shape=[jax.ShapeDtypeStruct((8, LANES), F32), jax.ShapeDtypeStruct((s, d), F32),
                   jax.ShapeDtypeStruct(nw.shape, F32)],
        compiler_params=_params(("arbitrary",)),
    )(x, tgt, nw)


class Dims:
    def __init__(self, s, d, ffn):
        self.s, self.d, self.ffn = s, d, ffn
        self.mix = 3 * d // 4
        self.nh = self.mix // HEAD
        self.ssm_heads = self.mix // SSM_P
        self.pairs = self.mix // (2 * SSM_P)
        self.nc = s // CHUNK
        self.conv_ssm = self.mix + 4 * HEAD
        self.conv_w = self.conv_ssm + 3 * self.mix
        self.o_gates = 4 * self.mix
        self.o_sz = self.o_gates + 3 * d
        self.o_gz = self.o_sz + self.mix
        self.o_conv = self.o_gz + self.mix
        self.o_small = self.o_conv + self.conv_w
        used = self.o_small + LANES
        self.np = -(-used // 1280) * 1280
        self.tm = _pick(s, (256, 128, 64))
        mix, nh = self.mix, self.nh
        self.in_sizes = (mix, mix, mix, mix, mix, self.conv_ssm, self.ssm_heads, 3 * mix, mix, nh, nh, 3 * d)
        self.in_width = sum(self.in_sizes)


def w_in_tables(dm, nshard):
    off = np.cumsum((0,) + dm.in_sizes)
    hq, hf, hi, hg, sz, sxbc, sdt, gqkv, gz, gb, ga, gates = (np.arange(off[i], off[i + 1]) for i in range(12))
    hgrn = np.stack([t.reshape(dm.nh, HEAD) for t in (hq, hf, hi, hg)], axis=1).reshape(-1)
    perm = np.concatenate([hgrn, gates, sz, gz, gqkv, sxbc, sdt, gb, ga])
    perm = np.concatenate([perm, np.full(dm.np - perm.size, -1)])
    shard = dm.in_width // nshard
    wpad = -(-shard // GATHER_TILE) * GATHER_TILE
    fwd = np.where(perm >= 0, (perm // shard) * wpad + perm % shard, -1)[None]
    inv = np.zeros(dm.in_width, np.int64)
    inv[perm[perm >= 0]] = np.nonzero(perm >= 0)[0]
    bwd = np.full((nshard, wpad), -1)
    bwd[:, :shard] = inv.reshape(nshard, shard)
    return fwd.astype(np.int32), bwd.astype(np.int32)


def _small_views(dm, small):
    t = small.T
    col = lambda a: a[:, :, None]
    row = lambda a: a.reshape(a.shape[0], dm.nc, 1, CHUNK)
    a, b = dm.ssm_heads, dm.ssm_heads + dm.nh
    sdt, gb, ga = t[:a], t[a:b], t[b:b + dm.nh]
    return col(sdt), row(sdt), col(gb), col(ga), row(ga)


def _scan_specs(dm, proj, conv_out, views, lp, dproj=None):
    dt_col, dt_row, gb_col, ga_col, ga_row = views
    mixb, nh = dm.mix // LANES, dm.nh
    s, mix = dm.s, dm.mix
    lane = ("lane", LANES)
    hb = HGRN_HEADS_PER_STEP
    hw = (CHUNK, hb * LANES)
    hgrn = dict(
        nblk=nh // hb, hb=hb, fn=hgrn_chunk, batched=False, state=(HEAD, HEAD),
        seqs=[(proj, (CHUNK, hb * 4 * HEAD), lambda h, n: (n, h), ("lane", 4 * HEAD))],
        hparams=[(lp["lb"], (1, hb * HEAD), lambda h: (0, h), lane)],
        sparams=[lp["hgrn_norm"]],
        dseqs=[((s, dm.np), BF16, (CHUNK, hb * 4 * HEAD), lambda h, n: (n, h), ("lane", 4 * HEAD), dproj)],
        io=(hw, lambda h, n: (n, h), lane))
    ppg = dm.pairs // 2
    qb = 3 * mixb
    gw = (CHUNK, ppg * LANES)
    group = ("lane", ppg * LANES)
    pcol = ((2 * ppg, CHUNK, 1), lambda g, n: (g, n, 0), ("lead", 2 * ppg))
    prow = ((2 * ppg, None, 1, CHUNK), lambda g, n: (g, n, 0, 0), ("lead", 2 * ppg))
    ppar = ((2 * ppg, 1, 1), lambda g: (g, 0, 0), ("lead", 2 * ppg))
    bc = lambda first: ((CHUNK, LANES), lambda g, n: (n, first + g), None)
    ssd = dict(
        nblk=2, hb=1, fn=ssd_chunk, batched=False, state=(HEAD, ppg * LANES),
        seqs=[(conv_out, gw, lambda g, n: (n, qb // ppg + g), group), (conv_out,) + bc(qb + mixb), (conv_out,) + bc(qb + mixb + 2),
              (dt_col,) + pcol, (dt_row,) + prow],
        hparams=[(lp["ssm_dt_bias"],) + ppar, (lp["ssm_a_log"],) + ppar],
        sparams=[],
        dseqs=[((s, mix), F32, gw, lambda g, n: (n, g), group), ((s, 2 * LANES), F32) + bc(0), ((s, 2 * LANES), F32) + bc(0),
               (dt_col.shape, F32) + pcol, (dt_row.shape, F32) + prow],
        io=(gw, lambda g, n: (n, g), group))
    hb = GDN_HEADS_PER_STEP
    hw = (CHUNK, hb * LANES)
    cq, cgz = 0, dm.o_gz // LANES
    assert nh % hb == 0 and cgz % hb == 0 and qb % ppg == 0
    hcol = ((hb, CHUNK, 1), lambda h, n: (h, n, 0), ("idx",))
    hrow = ((hb, None, 1, CHUNK), lambda h, n: (h, n, 0, 0), ("idx",))
    hpar = ((hb, 1, 1), lambda h: (h, 0, 0), ("idx",))
    at = lambda first: (hw, lambda h, n: (n, first // hb + h), lane)
    gdn = dict(
        nblk=nh // hb, hb=hb, fn=gdn_chunk, batched=True, state=(HEAD, HEAD),
        seqs=[(conv_out,) + at(cq), (conv_out,) + at(cq + nh), (conv_out,) + at(cq + 2 * nh), (proj,) + at(cgz),
              (gb_col,) + hcol, (ga_col,) + hcol, (ga_row,) + hrow],
        hparams=[(lp["gdn_dt_bias"],) + hpar, (lp["gdn_a_log"],) + hpar],
        sparams=[lp["gdn_norm"]],
        dseqs=[((s, mix), F32) + at(0), ((s, mix), F32) + at(0), ((s, mix), F32) + at(0), ((s, dm.np), BF16) + at(cgz) + (dproj,),
               (gb_col.shape, F32) + hcol, (ga_col.shape, F32) + hcol, (ga_row.shape, F32) + hrow],
        io=(hw, lambda h, n: (n, h), lane))
    return hgrn, ssd, gdn


def _run_scan_fwd(dm, name, sp, side=None):
    out = ((dm.s, dm.mix), F32) + sp["io"]
    (y,), states, arrived = scan_fwd(name, sp["fn"], sp["nblk"], sp["hb"], dm.nc, sp["seqs"], sp["hparams"],
                                     sp["sparams"], sp["state"], [out], sp["batched"], side)
    return y, states, arrived


def _run_scan_bwd(dm, name, sp, states, dy, side=None):
    return scan_bwd(name, sp["fn"], sp["nblk"], sp["hb"], dm.nc, sp["seqs"], sp["hparams"], sp["sparams"], sp["state"],
                    states, [(dy,) + sp["io"]], sp["dseqs"], sp["batched"], side)


SHARE_FWD = ((4, 1), (2,), (0,), (3,))
SHARE_BWD = ((0,), (4, 2), (3,), (1,))


def _share_out(side, share):
    if side is None:
        return None, None, None, None
    s, broadcast = side
    return tuple(([s[i] for i in idx], broadcast) for idx in share)


def _collect(share, *got):
    if not got[0]:
        return None
    out = [None] * len(GATHERED)
    for idx, arrived in zip(share, got):
        for i, t in zip(idx, arrived):
            out[i] = t
    return out


def layer_fwd(dm, l, x, lp, side=None):
    tm, d, mix = dm.tm, dm.d, dm.mix
    tag = f"l{l}_"
    (h,) = rowstage_fwd(tag + "norm1", normmod_fn, [(x, d, 0)], [lp["norm_mix"], lp["sc1"], lp["sh1"]], [(d, BF16)], tm)
    side_h, side_s, side_g, side_m = _share_out(side, SHARE_FWD)
    if side:
        proj, got_m = matmul(tag + "proj", h, lp["w_in"], "nn", F32, side_m)
    else:
        proj, got_m = matmul(tag + "proj", h, lp["w_in"], "nn", F32), None
    conv_out = conv_fwd(tag + "conv", proj, dm.o_conv // LANES, lp["conv_w"], lp["conv_b"])
    small = proj[:, dm.o_small:dm.o_small + LANES]
    views = _small_views(dm, small)
    hg, sd, gd = _scan_specs(dm, proj, conv_out, views, lp)
    yh, st_h, got_h = _run_scan_fwd(dm, tag + "hgrn", hg, side_h)
    y_ssd, st_s, got_s = _run_scan_fwd(dm, tag + "ssd", sd, side_s)
    yg, st_g, got_g = _run_scan_fwd(dm, tag + "gdn", gd, side_g)
    arrived = _collect(SHARE_FWD, got_h, got_s, got_g, got_m)
    (ys,) = rowstage_fwd(tag + "ssmpost", ssmpost_fn,
                         [(y_ssd, mix, 0), (conv_out, mix, 3), (proj, mix, dm.o_sz // mix)],
                         [lp["ssm_d_exp"], lp["ssm_norm"]], [(mix, F32)], tm)
    (merged,) = rowstage_fwd(tag + "merge", merge_fn, [(yh, mix, 0), (ys, mix, 0), (yg, mix, 0), (proj, 3 * d, 1)],
                             [lp["b_merge"], lp["w_branch"]], [(d, BF16)], tm)
    (x1,) = rowstage_fwd(tag + "outproj", outproj_fn, [(merged, d, 0), (x, d, 0)], [lp["g1"], lp["w_out"]], [(d, F32)], tm)
    (h2,) = rowstage_fwd(tag + "norm2", normmod_fn, [(x1, d, 0)], [lp["norm_ffn"], lp["sc2"], lp["sh2"]], [(d, BF16)], tm)
    gu = bmatmul(tag + "ffn_in", h2, lp["w_ffn_in"], "nn", BF16, True)
    gu = gu.reshape((2, gu.shape[0] // 2) + gu.shape[1:])
    act = swiglu3_fwd(tag + "swiglu", gu, tm)
    o2 = bmatmul(tag + "ffn_out", act, lp["w_ffn_out"], "nn", F32, False)
    (x2,) = rowstage_fwd(tag + "resid", resid_fn, [(x1, d, 0), (o2, d, 0)], [lp["g2"]], [(d, F32)], tm)
    saved = dict(x=x, h=h, proj=proj, conv_out=conv_out, views=views, yh=yh, y_ssd=y_ssd, yg=yg, ys=ys,
                 st_h=st_h, st_s=st_s, st_g=st_g, merged=merged, x1=x1, h2=h2, gu=gu, act=act, o2=o2)
    return x2, saved, arrived


def layer_bwd(dm, l, dx2, lp, sv, side=None, own=False):
    tm, d, mix, s = dm.tm, dm.d, dm.mix, dm.s
    tag = f"l{l}_b_"
    g = {}
    (dx1_a, do2), (g["g2"],) = rowstage_bwd(tag + "resid", resid_fn, [(sv["x1"], d, 0), (sv["o2"], d, 0)], [lp["g2"]],
                                            [dx2], [F32, BF16], tm)
    dact = bmatmul(tag + "ffn_out_dx", do2, lp["w_ffn_out"], "nt", BF16, True)
    g["w_ffn_out"] = bmatmul(tag + "ffn_out_dw", sv["act"], do2, "tn", F32, True)
    dgu = swiglu3_bwd(tag + "swiglu", sv["gu"], dact, tm)
    dgu = dgu.reshape((-1,) + dgu.shape[2:])
    dh2 = bmatmul(tag + "ffn_in_dx", dgu, lp["w_ffn_in"], "nt", BF16, False)
    g["w_ffn_in"] = bmatmul(tag + "ffn_in_dw", sv["h2"], dgu, "tn", F32, True)
    (dx1,), (g["norm_ffn"], g["sc2"], g["sh2"]) = rowstage_bwd(
        tag + "norm2", normmod_fn, [(sv["x1"], d, 0)], [lp["norm_ffn"], lp["sc2"], lp["sh2"]], [dh2], [F32], tm,
        adds={0: dx1_a})
    (dmerged, dx_a), (g["g1"], g["w_out"]) = rowstage_bwd(
        tag + "outproj", outproj_fn, [(sv["merged"], d, 0), (sv["x"], d, 0)], [lp["g1"], lp["w_out"]], [dx1],
        [BF16, F32], tm)
    proj, conv_out = sv["proj"], sv["conv_out"]
    dproj = lax.empty((s, dm.np), BF16)
    (dyh, dys, dyg, dproj), (g["b_merge"], g["w_branch"]) = rowstage_bwd(
        tag + "merge", merge_fn, [(sv["yh"], mix, 0), (sv["ys"], mix, 0), (sv["yg"], mix, 0), (proj, 3 * d, 1)],
        [lp["b_merge"], lp["w_branch"]], [dmerged], [F32, F32, F32, BF16], tm, into={3: (dproj, 1)})
    (dy_ssd, dxs_a, dproj), (g["ssm_d_exp"], g["ssm_norm"]) = rowstage_bwd(
        tag + "ssmpost", ssmpost_fn, [(sv["y_ssd"], mix, 0), (conv_out, mix, 3), (proj, mix, dm.o_sz // mix)],
        [lp["ssm_d_exp"], lp["ssm_norm"]], [dys], [F32, F32, BF16], tm, into={2: (dproj, dm.o_sz // mix)})
    side_h, side_s, side_g, side_m = _share_out(side, SHARE_BWD)
    hg, sd, _ = _scan_specs(dm, proj, conv_out, sv["views"], lp, dproj)
    (dproj,), (g["lb"],), (g["hgrn_norm"],), got_h = _run_scan_bwd(dm, tag + "hgrn", hg, sv["st_h"], dyh, side_h)
    gd = _scan_specs(dm, proj, conv_out, sv["views"], lp, dproj)[2]
    (dxs_b, dbp, dcp, d_dt_col, d_dt_row), (g["ssm_dt_bias"], g["ssm_a_log"]), _, got_s = _run_scan_bwd(
        dm, tag + "ssd", sd, sv["st_s"], dy_ssd, side_s)
    (dq, dk, dv, dproj, d_gb_col, d_ga_col, d_ga_row), (g["gdn_dt_bias"], g["gdn_a_log"]), (g["gdn_norm"],), got_g = _run_scan_bwd(
        dm, tag + "gdn", gd, sv["st_g"], dyg, side_g)
    dconv =jnp.concatenate([dq, dk, dv, dxs_a + dxs_b, dbp, dcp], axis=1)
    dproj, g["conv_w"], g["conv_b"] = conv_bwd(tag + "conv", proj, dm.o_conv // LANES, lp["conv_w"], lp["conv_b"], dconv,
                                               dproj)
    unrow = lambda t: t.reshape(t.shape[0], s).T
    dsmall = jnp.concatenate([d_dt_col[:, :, 0].T + unrow(d_dt_row), d_gb_col[:, :, 0].T,
                              d_ga_col[:, :, 0].T + unrow(d_ga_row)], axis=1)
    tail = jnp.pad(dsmall.astype(BF16), ((0, 0), (0, dm.np - dm.o_small - dsmall.shape[1])))
    dproj = lax.dynamic_update_slice(dproj, tail, (0, dm.o_small))
    beside_dx = list(side_m[0]) if side else []
    if own:
        s_wb, s_wout, s_wf, s_wfo = small_shards(dm, g)
        beside_dx = beside_dx + [s_wf]
        g["w_in"], (got_wb, got_wout, got_wfo) = matmul(tag + "proj_dw", sv["h"], dproj, "tn", F32,
                                                        ([s_wb, s_wout, s_wfo], False))
    else:
        g["w_in"] = matmul(tag + "proj_dw", sv["h"], dproj, "tn", F32)
    if beside_dx:
        dh, got_dx = matmul(tag + "proj_dx", dproj, lp["w_in"], "nt", BF16, (beside_dx, False))
    else:
        dh, got_dx = matmul(tag + "proj_dx", dproj, lp["w_in"], "nt", BF16), []
    arrived = _collect(SHARE_BWD, got_h, got_s, got_g, got_dx[:1]) if side else None
    (dx,), (g["norm_mix"], g["sc1"], g["sh1"]) = rowstage_bwd(
        tag + "norm1", normmod_fn, [(sv["x"], d, 0)], [lp["norm_mix"], lp["sc1"], lp["sh1"]], [dh], [F32], tm,
        adds={0: dx_a})
    if own:
        return dx, g, arrived, [got_wb, got_wout, got_dx[-1], got_wfo]
    return dx, g, arrived


WEIGHTS = ("w_ada", "b_ada", "norm_mix", "norm_ffn", "w_in", "b_merge", "hgrn_lb_logits", "hgrn_norm", "ssm_conv_w",
           "ssm_conv_b", "ssm_dt_bias", "ssm_a_log", "ssm_d", "ssm_norm", "gdn_conv_w", "gdn_dt_bias", "gdn_a_log",
           "gdn_norm", "w_branch", "w_out", "w_ffn_in", "w_ffn_out", "norm_final")
GATHERED = ("w_in", "w_branch", "w_out", "w_ffn_in", "w_ffn_out")
PACKET = ("b_ada", "norm_mix", "norm_ffn", "b_merge", "hgrn_norm", "ssm_conv_b", "ssm_dt_bias", "ssm_a_log", "ssm_d",
          "ssm_norm", "gdn_dt_bias", "gdn_a_log", "gdn_norm", "norm_final")
MISC = ("hgrn_lb_logits", "ssm_conv_w", "gdn_conv_w")


def _pack(arrs, dtype, row_mult, lead=0):
    flat = jnp.concatenate([t.reshape(t.shape[:lead] + (-1,)).astype(dtype) for t in arrs], axis=lead)
    n = flat.shape[-1]
    unit = row_mult * LANES
    tot = -(-n // unit) * unit
    flat = jnp.pad(flat, [(0, 0)] * lead + [(0, tot - n)])
    return flat.reshape(flat.shape[:lead] + (tot // LANES, LANES))


def _unpack(packed, shapes, lead=0):
    flat = packed.reshape(packed.shape[:lead] + (-1,))
    out, off = [], 0
    for shp in shapes:
        n = int(np.prod(shp))
        out.append(flat[..., off:off + n].reshape(flat.shape[:lead] + tuple(shp)))
        off += n
    return out


def _shard2d(t):
    return t.reshape((-1, t.shape[-1]))


def weights_from_shards(dm, l, got, idx):
    w_in, wb, w_out, wf, wfo = got
    d, mix = dm.d, dm.mix
    return dict(
        w_in=colgather(f"l{l}_w_in", w_in, idx, dm.np, BF16)[0],
        w_branch=wb.reshape(N_DEV, 3, mix, d // N_DEV).transpose(1, 2, 0, 3).reshape(3, mix, d),
        w_out=w_out.reshape(d, d), w_ffn_in=wf, w_ffn_out=wfo.reshape(N_DEV // 2, -1, d))


def small_shards(dm, g):
    d, mix = dm.d, dm.mix
    return [g["w_branch"].reshape(3, mix, N_DEV, d // N_DEV).transpose(2, 0, 1, 3).reshape(N_DEV, 3 * mix, d // N_DEV),
            g["w_out"].reshape(N_DEV, d // N_DEV, d), g["w_ffn_in"], g["w_ffn_out"].reshape(N_DEV, -1, d)]


def w_in_shards(dm, l, g, idx):
    return colgather(f"l{l}_g_w_in", g["w_in"][None], idx, dm.in_width // N_DEV, F32)


def layer_params(dm, l, full, small, mod_l, lb_l):
    d, mix = dm.d, dm.mix
    row = lambda t: t.reshape(1, -1)
    head = lambda t: t.reshape(-1, 1, 1)
    sh1, sc1, g1, sh2, sc2, g2 = (row(mod_l[i * d:(i + 1) * d]) for i in range(6))
    conv_b = jnp.concatenate([jnp.zeros((3 * mix,), F32), small["ssm_conv_b"][l]])
    return dict(
        w_in=full["w_in"], w_branch=full["w_branch"], w_out=full["w_out"],
        w_ffn_in=full["w_ffn_in"], w_ffn_out=full["w_ffn_out"],
        norm_mix=row(small["norm_mix"][l]), norm_ffn=row(small["norm_ffn"][l]), b_merge=row(small["b_merge"][l]),
        hgrn_norm=row(small["hgrn_norm"][l]), lb=row(lb_l),
        conv_w=jnp.concatenate([small["gdn_conv_w"][l], small["ssm_conv_w"][l]], axis=1), conv_b=row(conv_b),
        ssm_dt_bias=head(small["ssm_dt_bias"][l]), ssm_a_log=head(small["ssm_a_log"][l]),
        ssm_d_exp=row(jnp.repeat(small["ssm_d"][l], SSM_P)), ssm_norm=row(small["ssm_norm"][l]),
        gdn_dt_bias=head(small["gdn_dt_bias"][l]), gdn_a_log=head(small["gdn_a_log"][l]), gdn_norm=row(small["gdn_norm"][l]),
        sh1=sh1, sc1=sc1, g1=g1, sh2=sh2, sc2=sc2, g2=g2)


def layer_grads(dm, g):
    cs = 3 * dm.mix
    out = dict(
        w_in=g["w_in"], w_branch=g["w_branch"], w_out=g["w_out"], w_ffn_in=g["w_ffn_in"],
        w_ffn_out=g["w_ffn_out"], norm_mix=g["norm_mix"][0], norm_ffn=g["norm_ffn"][0], b_merge=g["b_merge"][0],
        hgrn_norm=g["hgrn_norm"][0], ssm_conv_w=g["conv_w"][:, cs:], gdn_conv_w=g["conv_w"][:, :cs],
        ssm_conv_b=g["conv_b"][0, cs:], ssm_dt_bias=g["ssm_dt_bias"][:, 0, 0], ssm_a_log=g["ssm_a_log"][:, 0, 0],
        ssm_d=g["ssm_d_exp"].reshape(dm.ssm_heads, SSM_P).sum(axis=1), ssm_norm=g["ssm_norm"][0],
        gdn_dt_bias=g["gdn_dt_bias"][:, 0, 0], gdn_a_log=g["gdn_a_log"][:, 0, 0], gdn_norm=g["gdn_norm"][0])
    dmod = jnp.concatenate([g[k][0] for k in ("sh1", "sc1", "g1", "sh2", "sc2", "g2")])
    return out, dmod, g["lb"][0]


def local_step(dm, nl, x, tgt, norm_final, params_of, gather_of=None, scatter_of=None):
    arrived = exchange("gather_w0", gather_of(0), True) if gather_of else None
    lps, saved = [], []
    for l in range(nl):
        lps.append(params_of(l, arrived))
        side = (gather_of(l + 1), True) if gather_of and l + 1 < nl else None
        x, sv, arrived = layer_fwd(dm, l, x, lps[l], side)
        saved.append(sv)
    loss, dx, dnf = loss_call("loss", x, tgt, norm_final, dm.tm)
    grads, parts, side = [None] * nl, [None] * nl, None
    for l in reversed(range(nl)):
        if scatter_of and l == 0:
            dx, grads[l], got, own = layer_bwd(dm, l, dx, lps[l], saved[l], side, own=True)
            parts[0] = list(exchange("scatter_g0", [scatter_of(0, grads[0])], False)) + own
        else:
            dx, grads[l], got = layer_bwd(dm, l, dx, lps[l], saved[l], side)
        if side is not None:
            parts[l + 1] = got
        side = ([scatter_of(l, grads[l])] + small_shards(dm, grads[l]), False) if scatter_of and l > 0 else None
    return loss, dx, dnf, grads, parts


def kernel(x, c, w_ada, b_ada, norm_mix, norm_ffn, w_in, b_merge, hgrn_lb_logits, hgrn_norm, ssm_conv_w, ssm_conv_b, ssm_dt_bias, ssm_a_log, ssm_d, ssm_norm, gdn_conv_w, gdn_dt_bias, gdn_a_log, gdn_norm, w_branch, w_out, w_ffn_in, w_ffn_out, norm_final, loss_target, m_w_ada, m_b_ada, m_norm_mix, m_norm_ffn, m_w_in, m_b_merge, m_hgrn_lb_logits, m_hgrn_norm, m_ssm_conv_w, m_ssm_conv_b, m_ssm_dt_bias, m_ssm_a_log, m_ssm_d, m_ssm_norm, m_gdn_conv_w, m_gdn_dt_bias, m_gdn_a_log, m_gdn_norm, m_w_branch, m_w_out, m_w_ffn_in, m_w_ffn_out, m_norm_final, v_w_ada, v_b_ada, v_norm_mix, v_norm_ffn, v_w_in, v_b_merge, v_hgrn_lb_logits, v_hgrn_norm, v_ssm_conv_w, v_ssm_conv_b, v_ssm_dt_bias, v_ssm_a_log, v_ssm_d, v_ssm_norm, v_gdn_conv_w, v_gdn_dt_bias, v_gdn_a_log, v_gdn_norm, v_w_branch, v_w_out, v_w_ffn_in, v_w_ffn_out, v_norm_final):
    a = dict(locals())
    x, tgt = a["x"][0], a["loss_target"][0]
    s, d = x.shape
    nl = a["w_ada"].shape[0]
    dm = Dims(s, d, a["w_ffn_out"].shape[1] * N_DEV)
    me = 4 * lax.axis_index("x") + 2 * lax.axis_index("y") + lax.axis_index("c")

    first = [a["c"], a["ssm_conv_w"], a["gdn_conv_w"]]
    c_all, scw, gcw = _unpack(exchange("gather_c", [_pack(first, F32, 8)], True)[0], [t.shape for t in first], lead=1)
    small = dict(a, ssm_conv_w=scw.transpose(1, 2, 0, 3).reshape(scw.shape[1:3] + (-1,)),
                 gdn_conv_w=gcw.transpose(1, 2, 0, 3).reshape(gcw.shape[1:3] + (-1,)))
    c_pad = jnp.zeros((LANES, d), F32).at[:N_DEV].set(c_all.reshape(N_DEV, d))
    ncol = a["w_ada"].shape[2]
    b_mine = lax.dynamic_slice(a["b_ada"], (0, me * ncol), (nl, ncol))[:, None, :]
    mod_part = ada_fwd("ada_fwd", c_pad, a["w_ada"], b_mine)[:, :N_DEV, :]
    (mod,) = exchange("a2a_mod", [mod_part.transpose(1, 0, 2)], False)
    mod = mod.transpose(1, 0, 2).reshape(nl, N_DEV * ncol)
    (lb,) = rowstage_fwd("lower_bounds", lower_bounds_fn, [(a["hgrn_lb_logits"], dm.mix, 0)], [], [(dm.mix, F32)], nl)

    idx_fwd, idx_bwd = w_in_tables(dm, N_DEV)
    loss, dx, dnf, grads, parts = local_step(
        dm, nl, x, tgt, a["norm_final"].reshape(1, d),
        params_of=lambda l, got: layer_params(dm, l, weights_from_shards(dm, l, got, idx_fwd), small, mod[l], lb[l]),
        gather_of=lambda l: [_shard2d(a[n][l]).astype(BF16) for n in GATHERED],
        scatter_of=lambda l, g: w_in_shards(dm, l, g, idx_bwd))

    per_layer = [layer_grads(dm, g) for g in grads]
    res = {}
    for i, n in enumerate(GATHERED):
        wmv = [a[q + n].reshape((nl, -1, a[n].shape[-1])) for q in ("", "m_", "v_")]
        outs = None
        for l in range(nl):
            outs = adamw_sum(f"adamw_l{l}_{n}", parts[l][i], *wmv, layer=l, into=outs)
        for kind, o in zip(("grad", "delta", "new_m", "new_v"), outs):
            res[(kind, n)] = o.reshape(a[n].shape)

    stackg = lambda n: jnp.stack([pl_[0][n] for pl_ in per_layer])
    dmod = jnp.stack([pl_[1] for pl_ in per_layer])
    dlb = jnp.stack([pl_[2] for pl_ in per_layer])
    pk_g = [dmod if n == "b_ada" else dnf if n == "norm_final" else stackg(n) for n in PACKET]
    extra = [dlb, stackg("ssm_conv_w"), stackg("gdn_conv_w"), loss[0, :1]]
    pk_shapes = [t.shape for t in pk_g + extra]
    zeros = [jnp.zeros(t.shape, F32) for t in extra]
    (parts,) = exchange("gather_small", [_pack(pk_g + extra, F32, 8)], True)
    outs = adamw_sum("adamw_small", parts, *[_pack([a[p + n] for n in PACKET] + zeros, F32, 8) for p in ("", "m_", "v_")])
    for kind, o in zip(("grad", "delta", "new_m", "new_v"), outs):
        un = _unpack(o, pk_shapes)
        for n, t in zip(PACKET, un):
            res[(kind, n)] = t.reshape(a[n].shape)
        if kind == "grad":
            dlb_sum, g_scw, g_gcw, loss_sum = un[len(PACKET):]

    (g_lb,), _ = rowstage_bwd("lower_bounds_b", lower_bounds_fn, [(a["hgrn_lb_logits"], dm.mix, 0)], [], [dlb_sum], [F32], nl)
    mine = lambda t, n: lax.dynamic_slice_in_dim(t, me * a[n].shape[-1], a[n].shape[-1], axis=t.ndim - 1)
    (dmod_cols,) = exchange("a2a_dmod", [dmod.reshape(nl, N_DEV, ncol).transpose(1, 0, 2)], False)
    dmod_pad = jnp.zeros((nl, LANES, ncol), F32).at[:, :N_DEV].set(dmod_cols.transpose(1, 0, 2))
    g_w_ada = ada_bwd("ada_bwd", c_pad, dmod_pad)
    outs = adamw_sum("adamw_w_ada", g_w_ada.reshape(1, nl * d, ncol), *[a[q + "w_ada"].reshape(nl * d, ncol) for q in ("", "m_", "v_")])
    for kind, o in zip(("grad", "delta", "new_m", "new_v"), outs):
        res[(kind, "w_ada")] = o.reshape(nl, d, ncol)
    g_misc = [g_lb, mine(g_scw, "ssm_conv_w"), mine(g_gcw, "gdn_conv_w")]
    outs = adamw_sum("adamw_misc", _pack(g_misc, F32, 8)[None], *[_pack([a[q + n] for n in MISC], F32, 8) for q in ("", "m_", "v_")])
    for kind, o in zip(("grad", "delta", "new_m", "new_v"), outs):
        for n, t in zip(MISC, _unpack(o, [a[n].shape for n in MISC])):
            res[(kind, n)] = t

    out = [loss_sum.reshape(()), dx[None]]
    for kind in ("grad", "delta", "new_m", "new_v"):
        out += [res[(kind, n)] for n in WEIGHTS]
    return tuple(out)
```

```python
import functools
import math

import numpy as np
import jax
import jax.numpy as jnp
from jax import lax
from jax.experimental import pallas as pl
from jax.experimental.pallas import tpu as pltpu

F32 = jnp.float32
BF16 = jnp.bfloat16

N_DEV = 8
CHUNK = 64
SUB = 8
HGRN_HEADS_PER_STEP = 6
GDN_HEADS_PER_STEP = 6
HEAD = 128
SSM_P = 64
CONV_K = 4
F_MIN = 1e-30
NORM_EPS = 1e-6
LANES = 128
GATHER_TILE = 256
VMEM_LIMIT = 56 * 1024 * 1024

ADAM_LR = 0.001
ADAM_B1 = 0.9
ADAM_B2 = 0.999
ADAM_EPS = 1e-08
ADAM_WD = 0.01
ADAM_STEP = 10


def _dg(a, b, ca, cb):
    return lax.dot_general(a.astype(BF16), b.astype(BF16), (((ca,), (cb,)), ((), ())),
                           preferred_element_type=F32)


def _split3(x):
    x1 = x.astype(BF16)
    r = x - x1.astype(F32)
    x2 = r.astype(BF16)
    x3 = (r - x2.astype(F32)).astype(BF16)
    return x1, x2, x3


def _hdg(a, b, ca, cb):
    a1, a2, _ = _split3(a)
    b1, b2, _ = _split3(b)
    dn = (((ca,), (cb,)), ((), ()))
    d = lambda p, q: lax.dot_general(p, q, dn, preferred_element_type=F32)
    return (d(a2, b1) + d(a1, b2)) + d(a1, b1)


def _dot_family(prim):
    @jax.custom_vjp
    def nn(a, b):
        return prim(a, b, 1, 0)

    @jax.custom_vjp
    def nt(a, b):
        return prim(a, b, 1, 1)

    @jax.custom_vjp
    def tn(a, b):
        return prim(a, b, 0, 0)

    nn.defvjp(lambda a, b: (nn(a, b), (a, b)), lambda r, g: (nt(g, r[1]), tn(r[0], g)))
    nt.defvjp(lambda a, b: (nt(a, b), (a, b)), lambda r, g: (nn(g, r[1]), tn(g, r[0])))
    tn.defvjp(lambda a, b: (tn(a, b), (a, b)), lambda r, g: (nt(r[1], g), nn(r[0], g)))
    return nn, nt, tn


mm_nn, mm_nt, mm_tn = _dot_family(_dg)
hd_nn, hd_nt, hd_tn = _dot_family(_hdg)


def _iota(shape, dim):
    return lax.broadcasted_iota(jnp.int32, shape, dim)


def _scan_rows(x, reverse):
    n = x.shape[0]
    rows = _iota(x.shape, 0)
    k = 1
    while k < n:
        if reverse:
            x = x + jnp.where(rows < n - k, pltpu.roll(x, n - k, 0), 0.0)
        else:
            x = x + jnp.where(rows >= k, pltpu.roll(x, k, 0), 0.0)
        k *= 2
    return x


@jax.custom_vjp
def cumsum_rows(x):
    return _scan_rows(x, False)


cumsum_rows.defvjp(lambda x: (_scan_rows(x, False), None), lambda _, g: (_scan_rows(g, True),))


def _sigmoid(x):
    return jax.nn.sigmoid(x)


def _silu(x):
    return x * jax.nn.sigmoid(x)


def _softplus(x):
    e = jnp.exp(-jnp.abs(x))
    small = e * (1.0 - e * (0.5 - e * (1.0 / 3.0)))
    return jnp.maximum(x, 0.0) + jnp.where(e < 1e-3, small, jnp.log(1.0 + e))


def _masked_exp(diff, mask):
    return jnp.where(mask, jnp.exp(jnp.where(mask, diff, 0.0)), 0.0)


def _rms(x, w):
    return x * lax.rsqrt(jnp.mean(x * x, axis=-1, keepdims=True) + NORM_EPS) * w


def _cum_col_row(lg_col, lg_row):
    c = lg_col.shape[0]
    r, s = _iota((c, c), 0), _iota((c, c), 1)
    cum_col = jnp.sum(jnp.where(s <= r, jnp.broadcast_to(lg_row, (c, c)), 0.0), axis=1, keepdims=True)
    cum_row = jnp.sum(jnp.where(r <= s, jnp.broadcast_to(lg_col, (c, c)), 0.0), axis=0, keepdims=True)
    total = jnp.sum(lg_col, axis=0, keepdims=True)
    return cum_col, cum_row, total


def hgrn_chunk(seq, hp, sp, st):
    (blk,), (lb,), (nw,) = seq, hp, sp
    c = blk.shape[0]
    q_raw, f_raw, v, g_raw = (blk[:, i * HEAD:(i + 1) * HEAD] for i in range(4))
    q = _silu(q_raw)
    f = lb + (1.0 - lb) * _sigmoid(f_raw)
    logf = jnp.log(jnp.maximum(f, F_MIN))
    k = (1.0 - lb) * _sigmoid(-f_raw)
    b = cumsum_rows(logf)
    o_inter = mm_nt(q * jnp.exp(b), st)
    nsub = c // SUB
    wide = (SUB, SUB, HEAD)
    er = _iota((SUB * SUB, SUB), 0)
    e_t = (er // SUB == _iota((SUB * SUB, SUB), 1)).astype(F32)
    pr = _iota((SUB * SUB, 1), 0)
    pmask = (pr % SUB) <= (pr // SUB)
    er64 = _iota((SUB * SUB, c), 0)
    ec64 = _iota((SUB * SUB, c), 1)
    rows_c = _iota((c, 1), 0)
    row = lambda a, i: jnp.sum(jnp.where(rows_c == i, a, 0.0), axis=0, keepdims=True)
    def sub_chunk(qi, ki, bi, bref, first, place):
        qb = jnp.broadcast_to(qi[:, None, :], wide).reshape(SUB * SUB, HEAD)
        kb = jnp.broadcast_to(ki[None, :, :], wide).reshape(SUB * SUB, HEAD)
        bd = (bi[:, None, :] - bi[None, :, :]).reshape(SUB * SUB, HEAD)
        sc_col = jnp.sum(qb * kb * _masked_exp(bd, pmask), axis=1, keepdims=True)
        sc = mm_tn(e_t, sc_col * place)
        sc = sc + mm_nt(qi * jnp.exp(bi - bref), k * _masked_exp(bref - b, rows_c < first))
        return mm_nn(sc, v)

    firsts = [SUB * i for i in range(nsub)]
    pile = lambda parts: jnp.concatenate([p[None] for p in parts], axis=0)
    cut = lambda a: a.reshape(nsub, SUB, HEAD)
    brefs = pile([row(b, f) for f in firsts])
    starts = pile([jnp.full((1, 1), f, jnp.int32) for f in firsts])
    places = pile([(ec64 == (er64 % SUB) + f).astype(F32) for f in firsts])
    o_intra = jax.vmap(sub_chunk)(cut(q), cut(k), cut(b), brefs, starts, places)
    o = o_inter + o_intra.reshape(c, HEAD)
    bend = row(b, c - 1)
    st_new = st * jnp.exp(bend) + mm_tn(v, k * jnp.exp(bend - b))
    y = _rms(o, nw) * _silu(g_raw)
    return (y,), st_new


def ssd_chunk(seq, hp, sp, st):
    xs, bm, cm, dtc, dtr = seq
    dt_bias, a_log = hp
    c, width = xs.shape
    nheads = width // SSM_P
    head_of = _iota((1, width), 1) // SSM_P
    r, s = _iota((c, c), 0), _iota((c, c), 1)
    g = mm_nt(cm, bm)
    dt_l, cum_l, end_l, scores = 0.0, 0.0, 0.0, []
    for i in range(nheads):
        neg_a = -jnp.exp(a_log[i])
        dt_col = _softplus(dtc[i] + dt_bias[i])
        dt_row = _softplus(dtr[i] + dt_bias[i])
        cum_col, cum_row, total = _cum_col_row(neg_a * dt_col, neg_a * dt_row)
        mine = head_of == i
        dt_l = dt_l + jnp.where(mine, dt_col, 0.0)
        cum_l = cum_l + jnp.where(mine, cum_col, 0.0)
        end_l = end_l + jnp.where(mine, total, 0.0)
        scores.append(g * _masked_exp(cum_col - cum_row, s <= r))
    xdt = xs * dt_l
    stacked = mm_nn(jnp.concatenate(scores, axis=0), xdt)
    y_intra = 0.0
    for i in range(nheads):
        y_intra = y_intra + jnp.where(head_of == i, stacked[i * c:(i + 1) * c], 0.0)
    y_inter = mm_nn(cm, st) * jnp.exp(cum_l)
    st_new = st * jnp.exp(end_l) + mm_tn(bm, xdt * jnp.exp(end_l - cum_l))
    return (y_intra + y_inter,), st_new


def _neumann_inverse(a):
    n = a.shape[0]
    eye = (_iota((n, n), 0) == _iota((n, n), 1)).astype(F32)
    p = -a
    t = eye + p
    for _ in range(int(math.log2(n)) - 1):
        p = _hdg(p, p, 1, 0)
        t = t + _hdg(t, p, 1, 0)
    return t


@jax.custom_vjp
def inv_unit_lower(a):
    return _neumann_inverse(a)


def _inv_fwd(a):
    t = _neumann_inverse(a)
    return t, t


inv_unit_lower.defvjp(_inv_fwd, lambda t, g: (-hd_nt(hd_tn(t, g), t),))


def gdn_chunk(seq, hp, sp, st):
    q_raw, k_raw, v, z, gbc, gac, gar = seq
    dt_bias, a_log = hp
    (nw,) = sp
    c = v.shape[0]
    r, s = _iota((c, c), 0), _iota((c, c), 1)
    q = q_raw * lax.rsqrt(jnp.sum(q_raw * q_raw, axis=-1, keepdims=True) + NORM_EPS) * (HEAD ** -0.5)
    k = k_raw * lax.rsqrt(jnp.sum(k_raw * k_raw, axis=-1, keepdims=True) + NORM_EPS)
    beta = _sigmoid(gbc)
    neg_a = -jnp.exp(a_log)
    cum, cum_row, total = _cum_col_row(neg_a * _softplus(gac + dt_bias), neg_a * _softplus(gar + dt_bias))
    decay = _masked_exp(cum - cum_row, s <= r)
    kk = mm_nt(k, k)
    a_low = jnp.where(s < r, beta * kk * decay, 0.0)
    sol = hd_nn(inv_unit_lower(a_low), jnp.concatenate([v * beta, k * (beta * jnp.exp(cum))], axis=1))
    u_base, w_corr = sol[:, :HEAD], sol[:, HEAD:]
    qk = mm_nt(q, k) * decay
    u = u_base - mm_nn(w_corr, st)
    o = mm_nn(q * jnp.exp(cum), st) + mm_nn(qk, u)
    st_new = jnp.exp(total) * st + mm_tn(k * jnp.exp(total - cum), u)
    y = _rms(o, nw) * _silu(z)
    return (y,), st_new


def normmod_fn(rows, params):
    (x,), (nw, sc, sh) = rows, params
    return (_rms(x, nw) * (1.0 + sc) + sh,)


def ssmpost_fn(rows, params):
    (y, xs, z), (d_exp, nw) = rows, params
    y = (y + d_exp * xs) * _silu(z)
    gw = y.shape[1] // 2
    return (jnp.concatenate([_rms(y[:, :gw], nw[:, :gw]), _rms(y[:, gw:], nw[:, gw:])], axis=1),)


def merge_fn(rows, params):
    (yh, ys, yg, gl), (bm, wb) = rows, params
    d = wb.shape[2]
    gates = _sigmoid(gl + bm)
    out = 0.0
    for n, y in enumerate((yh, ys, yg)):
        out = out + gates[:, n * d:(n + 1) * d] * mm_nn(y, wb[n])
    return (out,)


def outproj_fn(rows, params):
    (m, x), (g1, w) = rows, params
    return (x + (1.0 + g1) * mm_nn(m, w),)


def resid_fn(rows, params):
    (x, o), (g2,) = rows, params
    return (x + (1.0 + g2) * o,)


def _params(sem, side_effects=False):
    return pltpu.CompilerParams(dimension_semantics=sem, vmem_limit_bytes=VMEM_LIMIT, has_side_effects=side_effects)


def _whole(a):
    nd = a.ndim
    return pl.BlockSpec(a.shape, lambda *_: (0,) * nd)


def _pick(n, cands):
    for c in cands:
        if n % c == 0:
            return c
    return n


def matmul(name, a, b, mode, out_dtype, side=None):
    if mode == "nn":
        (m, k), n = a.shape, b.shape[1]
    elif mode == "nt":
        (m, k), n = a.shape, b.shape[0]
    else:
        (k, m), n = a.shape, b.shape[1]
    tm = _pick(m, (512, 256, 128))
    tn = _pick(n, (1280, 1024, 1408, 768, 512, 384, 256, 128))
    tk = _pick(k, (1024, 1280, 1408, 768, 512, 256, 128))
    if mode == "tn":
        tm = _pick(m, (1024, 768, 512, 256, 128))
        tk = _pick(k, (512, 256, 128))
    nk = k // tk
    ca, cb = {"nn": (1, 0), "nt": (1, 1), "tn": (0, 0)}[mode]

    def core(a_ref, b_ref, o_ref, acc_ref):
        kk = pl.program_id(2)

        @pl.when(kk == 0)
        def _():
            acc_ref[...] = jnp.zeros_like(acc_ref)

        acc_ref[...] += _dg(a_ref[...], b_ref[...], ca, cb)

        @pl.when(kk == nk - 1)
        def _():
            o_ref[...] = acc_ref[...].astype(o_ref.dtype)

    a_spec = (pl.BlockSpec((tk, tm), lambda i, j, q: (q, i)) if mode == "tn"
              else pl.BlockSpec((tm, tk), lambda i, j, q: (i, q)))
    b_spec = (pl.BlockSpec((tn, tk), lambda i, j, q: (j, q)) if mode == "nt"
              else pl.BlockSpec((tk, tn), lambda i, j, q: (q, j)))
    grid = (m // tm, n // tn, nk)
    body, s_in, s_out, s_shape, s_scr, s_args = _with_side(core, 2, 1, side, grid)
    sem = ("arbitrary",) * 3 if side else ("parallel", "parallel", "arbitrary")
    res = pl.pallas_call(
        body, name=name, grid=grid,
        in_specs=[a_spec, b_spec] + s_in,
        out_specs=[pl.BlockSpec((tm, tn), lambda i, j, q: (i, j))] + s_out,
        out_shape=[jax.ShapeDtypeStruct((m, n), out_dtype)] + s_shape,
        scratch_shapes=[pltpu.VMEM((tm, tn), F32)] + s_scr,
        compiler_params=_params(sem, side is not None),
    )(a, b, *s_args)
    return (res[0], res[1:]) if side else res[0]


def bmatmul(name, a, b, mode, out_dtype, out_batched):
    ab, bb = a.ndim == 3, b.ndim == 3
    nb = a.shape[0] if ab else b.shape[0]
    a2, b2 = a.shape[-2:], b.shape[-2:]
    if mode == "nn":
        (m, k), n = a2, b2[1]
    elif mode == "nt":
        (m, k), n = a2, b2[0]
    else:
        (k, m), n = a2, b2[1]
    tm = _pick(m, (1024, 512, 256, 128) if mode == "tn" else (512, 256, 128))
    tn = _pick(n, (1024, 512, 256, 128))
    tk = _pick(k, (512, 256, 128) if mode == "tn" else (1024, 512, 256, 128))
    nk = k // tk
    ca, cb = {"nn": (1, 0), "nt": (1, 1), "tn": (0, 0)}[mode]
    ids = (lambda g: g) if out_batched else (lambda g: (g[2], g[0], g[1], g[3]))
    grid = (nb, m // tm, n // tn, nk) if out_batched else (m // tm, n // tn, nb, nk)

    def a_map(*g):
        bi, i, j, q = ids(g)
        idx = (q, i) if mode == "tn" else (i, q)
        return (bi,) + idx if ab else idx

    def b_map(*g):
        bi, i, j, q = ids(g)
        idx = (j, q) if mode == "nt" else (q, j)
        return (bi,) + idx if bb else idx

    def o_map(*g):
        bi, i, j, q = ids(g)
        return (bi, i, j) if out_batched else (i, j)

    def body(a_ref, b_ref, o_ref, acc_ref):
        bi, _, _, q = ids(tuple(pl.program_id(d) for d in range(4)))
        first = (q == 0) if out_batched else (q == 0) & (bi == 0)
        last = (q == nk - 1) if out_batched else (q == nk - 1) & (bi == nb - 1)

        @pl.when(first)
        def _():
            acc_ref[...] = jnp.zeros_like(acc_ref)

        acc_ref[...] += _dg(a_ref[...], b_ref[...], ca, cb)

        @pl.when(last)
        def _():
            o_ref[...] = acc_ref[...].astype(o_ref.dtype)

    a_blk = (tk, tm) if mode == "tn" else (tm, tk)
    b_blk = (tn, tk) if mode == "nt" else (tk, tn)
    return pl.pallas_call(
        body, name=name, grid=grid,
        in_specs=[pl.BlockSpec(((None,) if ab else ()) + a_blk, a_map), pl.BlockSpec(((None,) if bb else ()) + b_blk, b_map)],
        out_specs=pl.BlockSpec(((None,) if out_batched else ()) + (tm, tn), o_map),
        out_shape=jax.ShapeDtypeStruct(((nb,) if out_batched else ()) + (m, n), out_dtype),
        scratch_shapes=[pltpu.VMEM((tm, tn), F32)],
        compiler_params=_params(("parallel", "parallel", "arbitrary", "arbitrary")),
    )(a, b)


def colgather(name, src, idx, dst_w, out_dtype):
    nsrc, rows, w = src.shape
    tw = GATHER_TILE
    nbs = -(-w // tw)
    ne = idx.shape[0]
    nbd = idx.shape[1] // tw
    tiles = [sorted(set((idx[e, t * tw:(t + 1) * tw][idx[e, t * tw:(t + 1) * tw] >= 0] // tw).tolist()))
             for e in range(ne) for t in range(nbd)]
    nslot = max(1, max(len(t) for t in tiles))
    tbl = np.full((ne * nbd, nslot), -1, np.int32)
    for i, t in enumerate(tiles):
        tbl[i, :len(t)] = t
    exact3 = src.dtype == F32

    def body(tbl_ref, idx_ref, src_ref, o_ref, acc_ref):
        ti, si = pl.program_id(0), pl.program_id(1)

        @pl.when(si == 0)
        def _():
            acc_ref[...] = jnp.zeros_like(acc_ref)

        t = tbl_ref[ti * nslot + si]

        @pl.when(t >= 0)
        def _():
            onehot = ((_iota((tw, tw), 0) + t * tw) == idx_ref[...]).astype(BF16)
            col = _iota((1, tw), 1) + (t % nbs) * tw
            xv = jnp.where(col < w, src_ref[...], jnp.zeros((), src_ref.dtype))
            d = lambda p: lax.dot_general(p, onehot, (((1,), (0,)), ((), ())), preferred_element_type=F32)
            if exact3:
                x1, x2, x3 = _split3(xv)
                acc_ref[...] += (d(x3) + d(x2)) + d(x1)
            else:
                acc_ref[...] += d(xv)

        @pl.when(si == nslot - 1)
        def _():
            o_ref[...] = acc_ref[...].astype(o_ref.dtype)

    def src_map(ti, si, tbl_ref):
        t = jnp.maximum(tbl_ref[ti * nslot + si], 0)
        return (t // nbs, 0, t % nbs)

    grid_spec = pltpu.PrefetchScalarGridSpec(
        num_scalar_prefetch=1, grid=(ne * nbd, nslot),
        in_specs=[pl.BlockSpec((None, 1, tw), lambda ti, si, tbl_ref: (ti // nbd, 0, ti % nbd)),
                  pl.BlockSpec((None, rows, tw), src_map)],
        out_specs=pl.BlockSpec((None, rows, tw), lambda ti, si, tbl_ref: (ti // nbd, 0, ti % nbd)),
        scratch_shapes=[pltpu.VMEM((rows, tw), F32)])
    return pl.pallas_call(
        body, name=name, grid_spec=grid_spec,
        out_shape=jax.ShapeDtypeStruct((ne, rows, dst_w), out_dtype),
        compiler_params=_params(("parallel", "arbitrary")),
    )(jnp.asarray(tbl.reshape(-1)), jnp.asarray(idx.reshape(ne, 1, nbd * tw).astype(np.int32)), src)


def swiglu3_fwd(name, gu, tm):
    _, nb, s, w = gu.shape

    def body(x_ref, o_ref):
        o_ref[...] = (_silu(x_ref[0].astype(F32)) * x_ref[1].astype(F32)).astype(o_ref.dtype)

    return pl.pallas_call(
        body, name=name, grid=(nb, s // tm),
        in_specs=[pl.BlockSpec((2, None, tm, w), lambda b, i: (0, b, i, 0))],
        out_specs=pl.BlockSpec((None, tm, w), lambda b, i: (b, i, 0)),
        out_shape=jax.ShapeDtypeStruct((nb, s, w), BF16),
        compiler_params=_params(("parallel", "parallel")),
    )(gu)


def swiglu3_bwd(name, gu, dact, tm):
    _, nb, s, w = gu.shape

    def body(x_ref, g_ref, o_ref):
        _, vjp = jax.vjp(lambda a, b: _silu(a) * b, x_ref[0].astype(F32), x_ref[1].astype(F32))
        dg, du = vjp(g_ref[...].astype(F32))
        o_ref[0] = dg.astype(o_ref.dtype)
        o_ref[1] = du.astype(o_ref.dtype)

    return pl.pallas_call(
        body, name=name, grid=(nb, s // tm),
        in_specs=[pl.BlockSpec((2, None, tm, w), lambda b, i: (0, b, i, 0)),
                  pl.BlockSpec((None, tm, w), lambda b, i: (b, i, 0))],
        out_specs=pl.BlockSpec((2, None, tm, w), lambda b, i: (0, b, i, 0)),
        out_shape=jax.ShapeDtypeStruct(gu.shape, BF16),
        compiler_params=_params(("parallel", "parallel")),
    )(gu, dact)


def _row_specs(rows, tm):
    return [pl.BlockSpec((tm, w), lambda i, _c=c: (i, _c)) for (_, w, c) in rows]


def rowstage_fwd(name, fn, rows, params, outs, tm):
    s = rows[0][0].shape[0]
    nr, npar = len(rows), len(params)

    def body(*refs):
        r = [x[...].astype(F32) for x in refs[:nr]]
        p = [x[...].astype(F32) for x in refs[nr:nr + npar]]
        for ref, val in zip(refs[nr + npar:], fn(r, p)):
            ref[...] = val.astype(ref.dtype)

    res = pl.pallas_call(
        body, name=name, grid=(s // tm,),
        in_specs=_row_specs(rows, tm) + [_whole(p) for p in params],
        out_specs=[pl.BlockSpec((tm, w), lambda i: (i, 0)) for (w, _) in outs],
        out_shape=[jax.ShapeDtypeStruct((s, w), dt) for (w, dt) in outs],
        compiler_params=_params(("parallel",)),
    )(*[r[0] for r in rows], *params)
    return res


def rowstage_bwd(name, fn, rows, params, douts, drow_dtypes, tm, adds=None, into=None):
    s = rows[0][0].shape[0]
    nr, npar, no = len(rows), len(params), len(douts)
    adds = adds or {}
    add_idx = sorted(adds)
    na = len(add_idx)
    into = into or {}
    into_idx = sorted(into)
    nb = len(into_idx)

    def body(*refs):
        r = [x[...].astype(F32) for x in refs[:nr]]
        p = [x[...].astype(F32) for x in refs[nr:nr + npar]]
        g = [x[...].astype(F32) for x in refs[nr + npar:nr + npar + no]]
        a_refs = refs[nr + npar + no:nr + npar + no + na]
        dr_refs = refs[nr + npar + no + na + nb:nr + npar + no + na + nb + nr]
        dp_refs = refs[nr + npar + no + na + nb + nr:]
        _, vjp = jax.vjp(lambda r_, p_: tuple(fn(r_, p_)), r, p)
        dr, dp = vjp(tuple(g))
        for j, (ref, val) in enumerate(zip(dr_refs, dr)):
            if j in adds:
                val = val + a_refs[add_idx.index(j)][...].astype(F32)
            ref[...] = val.astype(ref.dtype)

        @pl.when(pl.program_id(0) == 0)
        def _():
            for ref in dp_refs:
                ref[...] = jnp.zeros_like(ref)

        for ref, val in zip(dp_refs, dp):
            ref[...] += val

    res = pl.pallas_call(
        body, name=name, grid=(s // tm,),
        in_specs=(_row_specs(rows, tm) + [_whole(p) for p in params]
                  + [pl.BlockSpec((tm, d.shape[1]), lambda i: (i, 0)) for d in douts]
                  + [pl.BlockSpec((tm, rows[j][1]), lambda i: (i, 0)) for j in add_idx]
                  + [pl.BlockSpec(memory_space=pl.ANY)] * nb),
        out_specs=([pl.BlockSpec((tm, w), lambda i, _c=(into[j][1] if j in into else 0): (i, _c))
                    for j, (_, w, _) in enumerate(rows)] + [_whole(p) for p in params]),
        out_shape=([jax.ShapeDtypeStruct(into[j][0].shape if j in into else (s, w), dt)
                    for j, ((_, w, _), dt) in enumerate(zip(rows, drow_dtypes))]
                   + [jax.ShapeDtypeStruct(p.shape, F32) for p in params]),
        input_output_aliases={nr + npar + no + na + k: j for k, j in enumerate(into_idx)},
        compiler_params=_params(("arbitrary",)),
    )(*[r[0] for r in rows], *params, *douts, *[adds[j] for j in add_idx], *[into[j][0] for j in into_idx])
    return res[:nr], res[nr:]


def _flip(index_map, nc):
    return lambda h, n: index_map(h, nc - 1 - n)


def _with_side(core, n_in, n_out, side, grid):
    if side is None:
        return core, [], [], [], [], ()
    sends, broadcast = side
    k = len(sends)

    def body(*refs):
        ins, snd = refs[:n_in], refs[n_in:n_in + k]
        outs, rcv = refs[n_in + k:n_in + k + n_out], refs[n_in + k + n_out:n_in + 2 * k + n_out]
        scr = refs[n_in + 2 * k + n_out:]
        start, wait = _exchange_ops(snd, rcv, *scr[1:], broadcast)
        ids = [pl.program_id(d) for d in range(len(grid))]
        first = functools.reduce(lambda a, b: a & b, [i == 0 for i in ids])
        last = functools.reduce(lambda a, b: a & b, [i == g - 1 for i, g in zip(ids, grid)])
        pl.when(first)(start)
        core(*ins, *outs, scr[0])
        pl.when(last)(wait)

    return body, [HBM_SPEC] * k, [HBM_SPEC] * k, _exchange_out(sends, broadcast), _exchange_sems(k), tuple(sends)


def _take(v, split, j):
    if split is None:
        return v
    if split[0] == "lane":
        return v[:, j * split[1]:(j + 1) * split[1]]
    if split[0] == "lead":
        return v[j * split[1]:(j + 1) * split[1]]
    return v[j]


def _heads(vals, specs, hb):
    return [v if s[-1] is None else jnp.stack([_take(v, s[-1], j) for j in range(hb)]) for v, s in zip(vals, specs)]


def _over_heads(chunk_fn, hb, seqs, hparams, batched):
    seq_ax = [None if s[3] is None else 0 for s in seqs]
    hp_ax = [None if s[3] is None else 0 for s in hparams]
    if batched:
        return jax.vmap(chunk_fn, in_axes=(seq_ax, hp_ax, None, 0))

    def looped(seq, hp, sp, st):
        pick = lambda vals, axes, j: [v if a is None else v[j] for v, a in zip(vals, axes)]
        res = [chunk_fn(pick(seq, seq_ax, j), pick(hp, hp_ax, j), sp, st[j]) for j in range(hb)]
        pile = lambda parts: jnp.concatenate([p[None] for p in parts], axis=0)
        return tuple(pile(o) for o in zip(*[r[0] for r in res])), pile([r[1] for r in res])

    return looped


def _where(split, j):
    if split[0] == "lane":
        return (slice(None), slice(j * split[1], (j + 1) * split[1]))
    if split[0] == "lead":
        return (slice(j * split[1], (j + 1) * split[1]),)
    return (j,)


def scan_fwd(name, chunk_fn, nblk, hb, nc, seqs, hparams, sparams, state_shape, outs, batched, side=None):
    ns, nhp, nsp, no = len(seqs), len(hparams), len(sparams), len(outs)

    def core(*refs):
        seq_r, hp_r, sp_r = refs[:ns], refs[ns:ns + nhp], refs[ns + nhp:ns + nhp + nsp]
        out_r = refs[ns + nhp + nsp:ns + nhp + nsp + no]
        st_out, st_scr = refs[-2], refs[-1]

        @pl.when(pl.program_id(1) == 0)
        def _():
            st_scr[...] = jnp.zeros_like(st_scr)

        seq_v = [x[...].astype(F32) for x in seq_r]
        hp_v = [x[...] for x in hp_r]
        sp_v = [x[...] for x in sp_r]
        st = st_scr[...]
        st_out[...] = st
        heads = _over_heads(chunk_fn, hb, seqs, hparams, batched)
        o, st_new = heads(_heads(seq_v, seqs, hb), _heads(hp_v, hparams, hb), sp_v, st)
        for ref, spec, val in zip(out_r, outs, o):
            for j in range(hb):
                ref[_where(spec[4], j)] = val[j].astype(ref.dtype)
        st_scr[...] = st_new

    nst = len(state_shape)
    body, s_in, s_out, s_shape, s_scr, s_args = _with_side(core, ns + nhp + nsp, no + 1, side, (nblk, nc))
    res = pl.pallas_call(
        body, name=name, grid=(nblk, nc),
        in_specs=([pl.BlockSpec(bs, im) for (_, bs, im, _) in seqs]
                  + [pl.BlockSpec(bs, lambda h, n, _im=im: _im(h)) for (_, bs, im, _) in hparams]
                  + [_whole(p) for p in sparams] + s_in),
        out_specs=([pl.BlockSpec(bs, im) for (_, _, bs, im, _) in outs]
                   + [pl.BlockSpec((hb, None) + tuple(state_shape), lambda h, n: (h, n) + (0,) * nst)] + s_out),
        out_shape=([jax.ShapeDtypeStruct(fs, dt) for (fs, dt, _, _, _) in outs]
                   + [jax.ShapeDtypeStruct((nblk * hb, nc) + tuple(state_shape), F32)] + s_shape),
        scratch_shapes=[pltpu.VMEM((hb,) + tuple(state_shape), F32)] + s_scr,
        compiler_params=_params(("arbitrary", "arbitrary"), side is not None),
    )(*[x[0] for x in seqs], *[x[0] for x in hparams], *sparams, *s_args)
    return res[:no], res[no], res[no + 1:]


def scan_bwd(name, chunk_fn, nblk, hb, nc, seqs, hparams, sparams, state_shape, states, douts, dseqs, batched, side=None):
    ns, nhp, nsp, no = len(seqs), len(hparams), len(sparams), len(douts)
    nst = len(state_shape)
    buf_of = [i for i, sp in enumerate(dseqs) if len(sp) > 5 and sp[5] is not None]
    bufs = [dseqs[i][5] for i in buf_of]

    def core(*refs):
        seq_r, hp_r, sp_r = refs[:ns], refs[ns:ns + nhp], refs[ns + nhp:ns + nhp + nsp]
        base = ns + nhp + nsp
        st_r = refs[base]
        do_r = refs[base + 1:base + 1 + no]
        base += 1 + no + len(bufs)
        ds_r, dhp_r, dsp_r = refs[base:base + ns], refs[base + ns:base + ns + nhp], refs[base + ns + nhp:base + ns + nhp + nsp]
        dst_scr = refs[-1]
        h, n = pl.program_id(0), pl.program_id(1)

        @pl.when(n == 0)
        def _():
            dst_scr[...] = jnp.zeros_like(dst_scr)
            for ref in dhp_r:
                ref[...] = jnp.zeros_like(ref)

        @pl.when((n == 0) & (h == 0))
        def _():
            for ref in dsp_r:
                ref[...] = jnp.zeros_like(ref)

        seq_v = [x[...].astype(F32) for x in seq_r]
        hp_v = [x[...] for x in hp_r]
        sp_v = [x[...] for x in sp_r]
        do_v = [x[...].astype(F32) for x in do_r]
        prim = (_heads(seq_v, seqs, hb), _heads(hp_v, hparams, hb), sp_v, st_r[...])
        _, vjp = jax.vjp(_over_heads(chunk_fn, hb, seqs, hparams, batched), *prim)
        ds, dhp, dsp, dst = vjp((tuple(_heads(do_v, douts, hb)), dst_scr[...]))
        for ref, spec, val in zip(ds_r, dseqs, ds):
            if spec[4] is None:
                ref[...] = val.astype(ref.dtype)
            else:
                for j in range(hb):
                    ref[_where(spec[4], j)] = val[j].astype(ref.dtype)
        for ref, spec, val in zip(dhp_r, hparams, dhp):
            for j in range(hb):
                ref[_where(spec[3], j)] += val[j]
        for ref, val in zip(dsp_r, dsp):
            ref[...] += val
        dst_scr[...] = dst

    n_in, n_out = ns + nhp + nsp + 1 + no + len(bufs), ns + nhp + nsp
    body, s_in, s_out, s_shape, s_scr, s_args = _with_side(core, n_in, n_out, side, (nblk, nc))
    res = pl.pallas_call(
        body, name=name, grid=(nblk, nc),
        in_specs=([pl.BlockSpec(bs, _flip(im, nc)) for (_, bs, im, _) in seqs]
                  + [pl.BlockSpec(bs, lambda h, n, _im=im: _im(h)) for (_, bs, im, _) in hparams]
                  + [_whole(p) for p in sparams]
                  + [pl.BlockSpec((hb, None) + tuple(state_shape), lambda h, n: (h, nc - 1 - n) + (0,) * nst)]
                  + [pl.BlockSpec(bs, _flip(im, nc)) for (_, bs, im, _) in douts]
                  + [pl.BlockSpec(memory_space=pl.ANY)] * len(bufs) + s_in),
        out_specs=([pl.BlockSpec(sp[2], _flip(sp[3], nc)) for sp in dseqs]
                   + [pl.BlockSpec(bs, lambda h, n, _im=im: _im(h)) for (_, bs, im, _) in hparams]
                   + [_whole(p) for p in sparams] + s_out),
        out_shape=([jax.ShapeDtypeStruct(sp[0], sp[1]) for sp in dseqs]
                   + [jax.ShapeDtypeStruct(x[0].shape, F32) for x in hparams]
                   + [jax.ShapeDtypeStruct(p.shape, F32) for p in sparams] + s_shape),
        scratch_shapes=[pltpu.VMEM((hb,) + tuple(state_shape), F32)] + s_scr,
        input_output_aliases={n_in - len(bufs) + k: i for k, i in enumerate(buf_of)},
        compiler_params=_params(("arbitrary", "arbitrary"), side is not None),
    )(*[x[0] for x in seqs], *[x[0] for x in hparams], *sparams, states, *[x[0] for x in douts], *bufs, *s_args)
    return res[:ns], res[ns:ns + nhp], res[ns + nhp:n_out], res[n_out:]


def _shift_down(x, n, rows):
    if n == 0:
        return x
    return jnp.where(rows >= n, pltpu.roll(x, n, 0), 0.0)


def _shift_up(x, n, rows):
    if n == 0:
        return x
    s = x.shape[0]
    return jnp.where(rows < s - n, pltpu.roll(x, s - n, 0), 0.0)


def conv_fwd(name, x, col0, w, b):
    s, cw = x.shape[0], w.shape[1]

    def body(x_ref, w_ref, b_ref, o_ref):
        xv = x_ref[...]
        rows = _iota(xv.shape, 0)
        u = jnp.broadcast_to(b_ref[...], xv.shape)
        for j in range(CONV_K):
            u = u + w_ref[j:j + 1, :] * _shift_down(xv, CONV_K - 1 - j, rows)
        o_ref[...] = _silu(u)

    return pl.pallas_call(
        body, name=name, grid=(cw // LANES,),
        in_specs=[pl.BlockSpec((s, LANES), lambda j: (0, col0 + j)),
                  pl.BlockSpec((CONV_K, LANES), lambda j: (0, j)),
                  pl.BlockSpec((1, LANES), lambda j: (0, j))],
        out_specs=pl.BlockSpec((s, LANES), lambda j: (0, j)),
        out_shape=jax.ShapeDtypeStruct((s, cw), F32),
        compiler_params=_params(("parallel",)),
    )(x, w, b)


def conv_bwd(name, x, col0, w, b, dout, into):
    s, cw = x.shape[0], w.shape[1]

    def body(x_ref, w_ref, b_ref, g_ref, into_ref, dx_ref, dw_ref, db_ref):
        xv = x_ref[...]
        rows = _iota(xv.shape, 0)
        sh = [_shift_down(xv, CONV_K - 1 - j, rows) for j in range(CONV_K)]
        u = jnp.broadcast_to(b_ref[...], xv.shape)
        for j in range(CONV_K):
            u = u + w_ref[j:j + 1, :] * sh[j]
        sg = _sigmoid(u)
        du = g_ref[...] * (sg * (1.0 + u * (1.0 - sg)))
        dx = jnp.zeros_like(xv)
        for j in range(CONV_K):
            dx = dx + w_ref[j:j + 1, :] * _shift_up(du, CONV_K - 1 - j, rows)
            dw_ref[j:j + 1, :] = jnp.sum(du * sh[j], axis=0, keepdims=True)
        dx_ref[...] = dx.astype(dx_ref.dtype)
        db_ref[...] = jnp.sum(du, axis=0, keepdims=True)

    return pl.pallas_call(
        body, name=name, grid=(cw // LANES,),
        in_specs=[pl.BlockSpec((s, LANES), lambda j: (0, col0 + j)),
                  pl.BlockSpec((CONV_K, LANES), lambda j: (0, j)),
                  pl.BlockSpec((1, LANES), lambda j: (0, j)),
                  pl.BlockSpec((s, LANES), lambda j: (0, j)),
                  pl.BlockSpec(memory_space=pl.ANY)],
        out_specs=[pl.BlockSpec((s, LANES), lambda j: (0, col0 + j)),
                   pl.BlockSpec((CONV_K, LANES), lambda j: (0, j)),
                   pl.BlockSpec((1, LANES), lambda j: (0, j))],
        out_shape=[jax.ShapeDtypeStruct(into.shape, into.dtype), jax.ShapeDtypeStruct((CONV_K, cw), F32),
                   jax.ShapeDtypeStruct((1, cw), F32)],
        input_output_aliases={4: 0},
        compiler_params=_params(("parallel",)),
    )(x, w, b, dout, into)


def exchange(name, sends, broadcast):
    nop = len(sends)

    def body(*refs):
        start, wait = _exchange_ops(refs[:nop], refs[nop:2 * nop], *refs[2 * nop:], broadcast)
        start()
        wait()

    return pl.pallas_call(
        body, name=name,
        in_specs=[HBM_SPEC] * nop, out_specs=[HBM_SPEC] * nop,
        out_shape=_exchange_out(sends, broadcast), scratch_shapes=_exchange_sems(nop),
        compiler_params=pltpu.CompilerParams(has_side_effects=True),
    )(*sends)


HBM_SPEC = pl.BlockSpec(memory_space=pltpu.HBM)


def _exchange_out(sends, broadcast):
    return [jax.ShapeDtypeStruct((N_DEV,) + tuple(t.shape if broadcast else t.shape[1:]), t.dtype) for t in sends]


def _exchange_sems(nop):
    return [pltpu.SemaphoreType.DMA((nop * (N_DEV - 1),)), pltpu.SemaphoreType.DMA((nop * (N_DEV - 1),)),
            pltpu.SemaphoreType.DMA((nop,))]


def _gather_ops(send_refs, recv_refs, send_sems, recv_sems, local_sems):
    nop = len(send_refs)
    x, y, c = lax.axis_index("x"), lax.axis_index("y"), lax.axis_index("c")
    me, other = (x, y, c), (x, y, 1 - c)
    chips = [(1 - x, y), (x, 1 - y), (1 - x, 1 - y)]
    slab = lambda dev: 4 * dev[0] + 2 * dev[1] + dev[2]

    def copy(i, k, block, to, passed_on=False):
        return pltpu.make_async_remote_copy(
            src_ref=recv_refs[i].at[slab(block)] if passed_on else send_refs[i], dst_ref=recv_refs[i].at[slab(block)],
            send_sem=send_sems.at[i * (N_DEV - 1) + k], recv_sem=recv_sems.at[i * (N_DEV - 1) + k],
            device_id=to, device_id_type=pl.DeviceIdType.MESH)

    def local(i):
        return pltpu.make_async_copy(send_refs[i], recv_refs[i].at[slab(me)], local_sems.at[i])

    def start():
        for i in range(nop):
            local(i).start()
            copy(i, 0, me, other).start()
        for j, chip in enumerate(chips):
            for i in range(nop):
                copy(i, 1 + j, me, chip + (c,)).start()

    def wait():
        for j, chip in enumerate(chips):
            for i in range(nop):
                copy(i, 1 + j, chip + (c,), me).wait_recv()
                copy(i, 4 + j, chip + (c,), other, passed_on=True).start()
        for i in range(nop):
            copy(i, 0, other, me).wait_recv()
        for j, chip in enumerate(chips):
            for i in range(nop):
                copy(i, 4 + j, chip + (1 - c,), me, passed_on=True).wait_recv()
        for i in range(nop):
            copy(i, 0, me, other).wait_send()
            for j, chip in enumerate(chips):
                copy(i, 1 + j, me, chip + (c,)).wait_send()
                copy(i, 4 + j, chip + (c,), other, passed_on=True).wait_send()
            local(i).wait()

    return start, wait


def _exchange_ops(send_refs, recv_refs, send_sems, recv_sems, local_sems, broadcast):
    if broadcast:
        return _gather_ops(send_refs, recv_refs, send_sems, recv_sems, local_sems)
    nop = len(send_refs)
    x, y, c = lax.axis_index("x"), lax.axis_index("y"), lax.axis_index("c")
    me = 4 * x + 2 * y + c
    peers = []
    for k in range(1, N_DEV):
        px = 1 - x if (k >> 2) & 1 else x
        py = 1 - y if (k >> 1) & 1 else y
        pc = 1 - c if k & 1 else c
        peers.append(((px, py, pc), 4 * px + 2 * py + pc))

    def src(i, peer):
        return send_refs[i].at[peer]

    def remote(i, k, arrival):
        dev, peer = peers[k]
        return pltpu.make_async_remote_copy(
            src_ref=src(i, peer), dst_ref=recv_refs[i].at[peer if arrival else me],
            send_sem=send_sems.at[i * (N_DEV - 1) + k], recv_sem=recv_sems.at[i * (N_DEV - 1) + k],
            device_id=dev, device_id_type=pl.DeviceIdType.MESH)

    def local(i):
        return pltpu.make_async_copy(src(i, me), recv_refs[i].at[me], local_sems.at[i])

    def start():
        for i in range(nop):
            local(i).start()
        for k in range(N_DEV - 1):
            for i in range(nop):
                remote(i, k, False).start()

    def wait():
        for k in range(N_DEV - 1):
            for i in range(nop):
                remote(i, k, True).wait_recv()
        for k in range(N_DEV - 1):
            for i in range(nop):
                remote(i, k, False).wait_send()
        for i in range(nop):
            local(i).wait()

    return start, wait


def adamw_sum(name, parts, w, m, v, layer=None, into=None):
    rws, cols = w.shape[-2:]
    nsum = parts.shape[0]
    tr = _pick(rws, (256, 128, 64, 32, 16, 8))
    c1 = 1.0 / (1.0 - ADAM_B1 ** ADAM_STEP)
    c2 = 1.0 / (1.0 - ADAM_B2 ** ADAM_STEP)

    def body(p_ref, w_ref, m_ref, v_ref, *rest):
        g_ref, d_ref, nm_ref, nv_ref = rest[-4:]
        g = p_ref[0]
        for j in range(1, nsum):
            g = g + p_ref[j]
        nm = ADAM_B1 * m_ref[...] + (1.0 - ADAM_B1) * g
        nv = ADAM_B2 * v_ref[...] + (1.0 - ADAM_B2) * (g * g)
        g_ref[...] = g
        nm_ref[...] = nm
        nv_ref[...] = nv
        d_ref[...] = -ADAM_LR * ((nm * c1) / (jnp.sqrt(nv * c2) + ADAM_EPS) + ADAM_WD * w_ref[...])

    if layer is None:
        blk = pl.BlockSpec((tr, cols), lambda i: (i, 0))
    else:
        blk = pl.BlockSpec((None, tr, cols), lambda i: (layer, i, 0))
    if into is None and layer is not None:
        into = [lax.empty(w.shape, F32) for _ in range(4)]
    extra = list(into) if into else []
    return pl.pallas_call(
        body, name=name, grid=(rws // tr,),
        in_specs=([pl.BlockSpec((nsum, tr, cols), lambda i: (0, i, 0)), blk, blk, blk]
                  + [pl.BlockSpec(memory_space=pl.ANY)] * len(extra)),
        out_specs=[blk, blk, blk, blk],
        out_shape=[jax.ShapeDtypeStruct(w.shape, F32)] * 4,
        input_output_aliases={4 + k: k for k in range(len(extra))},
        compiler_params=_params(("parallel",)),
    )(parts, w, m, v, *extra)


def ada_fwd(name, c_all, w, b):
    nl = w.shape[0]

    def body(c_ref, w_ref, b_ref, o_ref):
        ca = _silu(c_ref[...])
        for l in range(nl):
            o_ref[l] = mm_nn(ca, w_ref[l]) + b_ref[l]

    return pl.pallas_call(
        body, name=name,
        out_shape=jax.ShapeDtypeStruct((nl, c_all.shape[0], w.shape[2]), F32),
        compiler_params=pltpu.CompilerParams(vmem_limit_bytes=VMEM_LIMIT),
    )(c_all, w, b)


def ada_bwd(name, c_all, dmod):
    nl = dmod.shape[0]

    def body(c_ref, g_ref, o_ref):
        ca = _silu(c_ref[...])
        for l in range(nl):
            o_ref[l] = mm_tn(ca, g_ref[l])

    return pl.pallas_call(
        body, name=name,
        out_shape=jax.ShapeDtypeStruct((nl, c_all.shape[1], dmod.shape[2]), F32),
        compiler_params=pltpu.CompilerParams(vmem_limit_bytes=VMEM_LIMIT),
    )(c_all, dmod)


def lower_bounds_fn(rows, params):
    (lg,), _ = rows, params
    nl = lg.shape[0]
    mx = jnp.max(lg, axis=0, keepdims=True)
    e = jnp.exp(lg - mx)
    p = e / jnp.sum(e, axis=0, keepdims=True)
    layer = _iota((nl, 1), 0)
    acc = jnp.zeros_like(p)
    for j in range(1, nl):
        pj = jnp.sum(jnp.where(layer == j, p, 0.0), axis=0, keepdims=True)
        acc = acc + jnp.where(layer >= j, 1.0, 0.0) * pj
    return (acc,)


def loss_call(name, x, tgt, nw, tm):
    s, d = x.shape

    def body(x_ref, t_ref, w_ref, l_ref, dx_ref, dw_ref):
        def f(xv, wv):
            err = _rms(xv, wv) - t_ref[...]
            return jnp.sum(0.5 * jnp.mean(err * err, axis=-1, keepdims=True), axis=0, keepdims=True)

        val, vjp = jax.vjp(f, x_ref[...], w_ref[...])
        dx, dw = vjp(jnp.ones_like(val))

        @pl.when(pl.program_id(0) == 0)
        def _():
            l_ref[...] = jnp.zeros_like(l_ref)
            dw_ref[...] = jnp.zeros_like(dw_ref)

        l_ref[...] += jnp.broadcast_to(val, l_ref.shape)
        dw_ref[...] += dw
        dx_ref[...] = dx

    row = pl.BlockSpec((tm, d), lambda i: (i, 0))
    return pl.pallas_call(
        body, name=name, grid=(s // tm,),
        in_specs=[row, row, _whole(nw)],
        out_specs=[pl.BlockSpec((8, LANES), lambda i: (0, 0)), row, _whole(nw)],
        out_shape=[jax.ShapeDtypeStruct((8, LANES), F32), jax.ShapeDtypeStruct((s, d), F32),
                   jax.ShapeDtypeStruct(nw.shape, F32)],
        compiler_params=_params(("arbitrary",)),
    )(x, tgt, nw)


class Dims:
    def __init__(self, s, d, ffn):
        self.s, self.d, self.ffn = s, d, ffn
        self.mix = 3 * d // 4
        self.nh = self.mix // HEAD
        self.ssm_heads = self.mix // SSM_P
        self.pairs = self.mix // (2 * SSM_P)
        self.nc = s // CHUNK
        self.conv_ssm = self.mix + 4 * HEAD
        self.conv_w = self.conv_ssm + 3 * self.mix
        self.o_gates = 4 * self.mix
        self.o_sz = self.o_gates + 3 * d
        self.o_gz = self.o_sz + self.mix
        self.o_conv = self.o_gz + self.mix
        self.o_small = self.o_conv + self.conv_w
        used = self.o_small + LANES
        self.np = -(-used // 1280) * 1280
        self.tm = _pick(s, (256, 128, 64))
        mix, nh = self.mix, self.nh
        self.in_sizes = (mix, mix, mix, mix, mix, self.conv_ssm, self.ssm_heads, 3 * mix, mix, nh, nh, 3 * d)
        self.in_width = sum(self.in_sizes)


def w_in_tables(dm, nshard):
    off = np.cumsum((0,) + dm.in_sizes)
    hq, hf, hi, hg, sz, sxbc, sdt, gqkv, gz, gb, ga, gates = (np.arange(off[i], off[i + 1]) for i in range(12))
    hgrn = np.stack([t.reshape(dm.nh, HEAD) for t in (hq, hf, hi, hg)], axis=1).reshape(-1)
    perm = np.concatenate([hgrn, gates, sz, gz, gqkv, sxbc, sdt, gb, ga])
    perm = np.concatenate([perm, np.full(dm.np - perm.size, -1)])
    shard = dm.in_width // nshard
    wpad = -(-shard // GATHER_TILE) * GATHER_TILE
    fwd = np.where(perm >= 0, (perm // shard) * wpad + perm % shard, -1)[None]
    inv = np.zeros(dm.in_width, np.int64)
    inv[perm[perm >= 0]] = np.nonzero(perm >= 0)[0]
    bwd = np.full((nshard, wpad), -1)
    bwd[:, :shard] = inv.reshape(nshard, shard)
    return fwd.astype(np.int32), bwd.astype(np.int32)


def _small_views(dm, small):
    t = small.T
    col = lambda a: a[:, :, None]
    row = lambda a: a.reshape(a.shape[0], dm.nc, 1, CHUNK)
    a, b = dm.ssm_heads, dm.ssm_heads + dm.nh
    sdt, gb, ga = t[:a], t[a:b], t[b:b + dm.nh]
    return col(sdt), row(sdt), col(gb), col(ga), row(ga)


def _scan_specs(dm, proj, conv_out, views, lp, dproj=None):
    dt_col, dt_row, gb_col, ga_col, ga_row = views
    mixb, nh = dm.mix // LANES, dm.nh
    s, mix = dm.s, dm.mix
    lane = ("lane", LANES)
    hb = HGRN_HEADS_PER_STEP
    hw = (CHUNK, hb * LANES)
    hgrn = dict(
        nblk=nh // hb, hb=hb, fn=hgrn_chunk, batched=False, state=(HEAD, HEAD),
        seqs=[(proj, (CHUNK, hb * 4 * HEAD), lambda h, n: (n, h), ("lane", 4 * HEAD))],
        hparams=[(lp["lb"], (1, hb * HEAD), lambda h: (0, h), lane)],
        sparams=[lp["hgrn_norm"]],
        dseqs=[((s, dm.np), BF16, (CHUNK, hb * 4 * HEAD), lambda h, n: (n, h), ("lane", 4 * HEAD), dproj)],
        io=(hw, lambda h, n: (n, h), lane))
    ppg = dm.pairs // 2
    qb = 3 * mixb
    gw = (CHUNK, ppg * LANES)
    group = ("lane", ppg * LANES)
    pcol = ((2 * ppg, CHUNK, 1), lambda g, n: (g, n, 0), ("lead", 2 * ppg))
    prow = ((2 * ppg, None, 1, CHUNK), lambda g, n: (g, n, 0, 0), ("lead", 2 * ppg))
    ppar = ((2 * ppg, 1, 1), lambda g: (g, 0, 0), ("lead", 2 * ppg))
    bc = lambda first: ((CHUNK, LANES), lambda g, n: (n, first + g), None)
    ssd = dict(
        nblk=2, hb=1, fn=ssd_chunk, batched=False, state=(HEAD, ppg * LANES),
        seqs=[(conv_out, gw, lambda g, n: (n, qb // ppg + g), group), (conv_out,) + bc(qb + mixb), (conv_out,) + bc(qb + mixb + 2),
              (dt_col,) + pcol, (dt_row,) + prow],
        hparams=[(lp["ssm_dt_bias"],) + ppar, (lp["ssm_a_log"],) + ppar],
        sparams=[],
        dseqs=[((s, mix), F32, gw, lambda g, n: (n, g), group), ((s, 2 * LANES), F32) + bc(0), ((s, 2 * LANES), F32) + bc(0),
               (dt_col.shape, F32) + pcol, (dt_row.shape, F32) + prow],
        io=(gw, lambda g, n: (n, g), group))
    hb = GDN_HEADS_PER_STEP
    hw = (CHUNK, hb * LANES)
    cq, cgz = 0, dm.o_gz // LANES
    assert nh % hb == 0 and cgz % hb == 0 and qb % ppg == 0
    hcol = ((hb, CHUNK, 1), lambda h, n: (h, n, 0), ("idx",))
    hrow = ((hb, None, 1, CHUNK), lambda h, n: (h, n, 0, 0), ("idx",))
    hpar = ((hb, 1, 1), lambda h: (h, 0, 0), ("idx",))
    at = lambda first: (hw, lambda h, n: (n, first // hb + h), lane)
    gdn = dict(
        nblk=nh // hb, hb=hb, fn=gdn_chunk, batched=True, state=(HEAD, HEAD),
        seqs=[(conv_out,) + at(cq), (conv_out,) + at(cq + nh), (conv_out,) + at(cq + 2 * nh), (proj,) + at(cgz),
              (gb_col,) + hcol, (ga_col,) + hcol, (ga_row,) + hrow],
        hparams=[(lp["gdn_dt_bias"],) + hpar, (lp["gdn_a_log"],) + hpar],
        sparams=[lp["gdn_norm"]],
        dseqs=[((s, mix), F32) + at(0), ((s, mix), F32) + at(0), ((s, mix), F32) + at(0), ((s, dm.np), BF16) + at(cgz) + (dproj,),
               (gb_col.shape, F32) + hcol, (ga_col.shape, F32) + hcol, (ga_row.shape, F32) + hrow],
        io=(hw, lambda h, n: (n, h), lane))
    return hgrn, ssd, gdn


def _run_scan_fwd(dm, name, sp, side=None):
    out = ((dm.s, dm.mix), F32) + sp["io"]
    (y,), states, arrived = scan_fwd(name, sp["fn"], sp["nblk"], sp["hb"], dm.nc, sp["seqs"], sp["hparams"],
                                     sp["sparams"], sp["state"], [out], sp["batched"], side)
    return y, states, arrived


def _run_scan_bwd(dm, name, sp, states, dy, side=None):
    return scan_bwd(name, sp["fn"], sp["nblk"], sp["hb"], dm.nc, sp["seqs"], sp["hparams"], sp["sparams"], sp["state"],
                    states, [(dy,) + sp["io"]], sp["dseqs"], sp["batched"], side)


SHARE_FWD = ((4, 1), (2,), (0,), (3,))
SHARE_BWD = ((0,), (4, 2), (3,), (1,))


def _share_out(side, share):
    if side is None:
        return None, None, None, None
    s, broadcast = side
    return tuple(([s[i] for i in idx], broadcast) for idx in share)


def _collect(share, *got):
    if not got[0]:
        return None
    out = [None] * len(GATHERED)
    for idx, arrived in zip(share, got):
        for i, t in zip(idx, arrived):
            out[i] = t
    return out


def layer_fwd(dm, l, x, lp, side=None):
    tm, d, mix = dm.tm, dm.d, dm.mix
    tag = f"l{l}_"
    (h,) = rowstage_fwd(tag + "norm1", normmod_fn, [(x, d, 0)], [lp["norm_mix"], lp["sc1"], lp["sh1"]], [(d, BF16)], tm)
    side_h, side_s, side_g, side_m = _share_out(side, SHARE_FWD)
    if side:
        proj, got_m = matmul(tag + "proj", h, lp["w_in"], "nn", F32, side_m)
    else:
        proj, got_m = matmul(tag + "proj", h, lp["w_in"], "nn", F32), None
    conv_out = conv_fwd(tag + "conv", proj, dm.o_conv // LANES, lp["conv_w"], lp["conv_b"])
    small = proj[:, dm.o_small:dm.o_small + LANES]
    views = _small_views(dm, small)
    hg, sd, gd = _scan_specs(dm, proj, conv_out, views, lp)
    yh, st_h, got_h = _run_scan_fwd(dm, tag + "hgrn", hg, side_h)
    y_ssd, st_s, got_s = _run_scan_fwd(dm, tag + "ssd", sd, side_s)
    yg, st_g, got_g = _run_scan_fwd(dm, tag + "gdn", gd, side_g)
    arrived = _collect(SHARE_FWD, got_h, got_s, got_g, got_m)
    (ys,) = rowstage_fwd(tag + "ssmpost", ssmpost_fn,
                         [(y_ssd, mix, 0), (conv_out, mix, 3), (proj, mix, dm.o_sz // mix)],
                         [lp["ssm_d_exp"], lp["ssm_norm"]], [(mix, F32)], tm)
    (merged,) = rowstage_fwd(tag + "merge", merge_fn, [(yh, mix, 0), (ys, mix, 0), (yg, mix, 0), (proj, 3 * d, 1)],
                             [lp["b_merge"], lp["w_branch"]], [(d, BF16)], tm)
    (x1,) = rowstage_fwd(tag + "outproj", outproj_fn, [(merged, d, 0), (x, d, 0)], [lp["g1"], lp["w_out"]], [(d, F32)], tm)
    (h2,) = rowstage_fwd(tag + "norm2", normmod_fn, [(x1, d, 0)], [lp["norm_ffn"], lp["sc2"], lp["sh2"]], [(d, BF16)], tm)
    gu = bmatmul(tag + "ffn_in", h2, lp["w_ffn_in"], "nn", BF16, True)
    gu = gu.reshape((2, gu.shape[0] // 2) + gu.shape[1:])
    act = swiglu3_fwd(tag + "swiglu", gu, tm)
    o2 = bmatmul(tag + "ffn_out", act, lp["w_ffn_out"], "nn", F32, False)
    (x2,) = rowstage_fwd(tag + "resid", resid_fn, [(x1, d, 0), (o2, d, 0)], [lp["g2"]], [(d, F32)], tm)
    saved = dict(x=x, h=h, proj=proj, conv_out=conv_out, views=views, yh=yh, y_ssd=y_ssd, yg=yg, ys=ys,
                 st_h=st_h, st_s=st_s, st_g=st_g, merged=merged, x1=x1, h2=h2, gu=gu, act=act, o2=o2)
    return x2, saved, arrived


def layer_bwd(dm, l, dx2, lp, sv, side=None, own=False):
    tm, d, mix, s = dm.tm, dm.d, dm.mix, dm.s
    tag = f"l{l}_b_"
    g = {}
    (dx1_a, do2), (g["g2"],) = rowstage_bwd(tag + "resid", resid_fn, [(sv["x1"], d, 0), (sv["o2"], d, 0)], [lp["g2"]],
                                            [dx2], [F32, BF16], tm)
    dact = bmatmul(tag + "ffn_out_dx", do2, lp["w_ffn_out"], "nt", BF16, True)
    g["w_ffn_out"] = bmatmul(tag + "ffn_out_dw", sv["act"], do2, "tn", F32, True)
    dgu = swiglu3_bwd(tag + "swiglu", sv["gu"], dact, tm)
    dgu = dgu.reshape((-1,) + dgu.shape[2:])
    dh2 = bmatmul(tag + "ffn_in_dx", dgu, lp["w_ffn_in"], "nt", BF16, False)
    g["w_ffn_in"] = bmatmul(tag + "ffn_in_dw", sv["h2"], dgu, "tn", F32, True)
    (dx1,), (g["norm_ffn"], g["sc2"], g["sh2"]) = rowstage_bwd(
        tag + "norm2", normmod_fn, [(sv["x1"], d, 0)], [lp["norm_ffn"], lp["sc2"], lp["sh2"]], [dh2], [F32], tm,
        adds={0: dx1_a})
    (dmerged, dx_a), (g["g1"], g["w_out"]) = rowstage_bwd(
        tag + "outproj", outproj_fn, [(sv["merged"], d, 0), (sv["x"], d, 0)], [lp["g1"], lp["w_out"]], [dx1],
        [BF16, F32], tm)
    proj, conv_out = sv["proj"], sv["conv_out"]
    dproj = lax.empty((s, dm.np), BF16)
    (dyh, dys, dyg, dproj), (g["b_merge"], g["w_branch"]) = rowstage_bwd(
        tag + "merge", merge_fn, [(sv["yh"], mix, 0), (sv["ys"], mix, 0), (sv["yg"], mix, 0), (proj, 3 * d, 1)],
        [lp["b_merge"], lp["w_branch"]], [dmerged], [F32, F32, F32, BF16], tm, into={3: (dproj, 1)})
    (dy_ssd, dxs_a, dproj), (g["ssm_d_exp"], g["ssm_norm"]) = rowstage_bwd(
        tag + "ssmpost", ssmpost_fn, [(sv["y_ssd"], mix, 0), (conv_out, mix, 3), (proj, mix, dm.o_sz // mix)],
        [lp["ssm_d_exp"], lp["ssm_norm"]], [dys], [F32, F32, BF16], tm, into={2: (dproj, dm.o_sz // mix)})
    side_h, side_s, side_g, side_m = _share_out(side, SHARE_BWD)
    hg, sd, _ = _scan_specs(dm, proj, conv_out, sv["views"], lp, dproj)
    (dproj,), (g["lb"],), (g["hgrn_norm"],), got_h = _run_scan_bwd(dm, tag + "hgrn", hg, sv["st_h"], dyh, side_h)
    gd = _scan_specs(dm, proj, conv_out, sv["views"], lp, dproj)[2]
    (dxs_b, dbp, dcp, d_dt_col, d_dt_row), (g["ssm_dt_bias"], g["ssm_a_log"]), _, got_s = _run_scan_bwd(
        dm, tag + "ssd", sd, sv["st_s"], dy_ssd, side_s)
    (dq, dk, dv, dproj, d_gb_col, d_ga_col, d_ga_row), (g["gdn_dt_bias"], g["gdn_a_log"]), (g["gdn_norm"],), got_g = _run_scan_bwd(
        dm, tag + "gdn", gd, sv["st_g"], dyg, side_g)
    dconv =jnp.concatenate([dq, dk, dv, dxs_a + dxs_b, dbp, dcp], axis=1)
    dproj, g["conv_w"], g["conv_b"] = conv_bwd(tag + "conv", proj, dm.o_conv // LANES, lp["conv_w"], lp["conv_b"], dconv,
                                               dproj)
    unrow = lambda t: t.reshape(t.shape[0], s).T
    dsmall = jnp.concatenate([d_dt_col[:, :, 0].T + unrow(d_dt_row), d_gb_col[:, :, 0].T,
                              d_ga_col[:, :, 0].T + unrow(d_ga_row)], axis=1)
    tail = jnp.pad(dsmall.astype(BF16), ((0, 0), (0, dm.np - dm.o_small - dsmall.shape[1])))
    dproj = lax.dynamic_update_slice(dproj, tail, (0, dm.o_small))
    beside_dx = list(side_m[0]) if side else []
    if own:
        s_wb, s_wout, s_wf, s_wfo = small_shards(dm, g)
        beside_dx = beside_dx + [s_wf]
        g["w_in"], (got_wb, got_wout, got_wfo) = matmul(tag + "proj_dw", sv["h"], dproj, "tn", F32,
                                                        ([s_wb, s_wout, s_wfo], False))
    else:
        g["w_in"] = matmul(tag + "proj_dw", sv["h"], dproj, "tn", F32)
    if beside_dx:
        dh, got_dx = matmul(tag + "proj_dx", dproj, lp["w_in"], "nt", BF16, (beside_dx, False))
    else:
        dh, got_dx = matmul(tag + "proj_dx", dproj, lp["w_in"], "nt", BF16), []
    arrived = _collect(SHARE_BWD, got_h, got_s, got_g, got_dx[:1]) if side else None
    (dx,), (g["norm_mix"], g["sc1"], g["sh1"]) = rowstage_bwd(
        tag + "norm1", normmod_fn, [(sv["x"], d, 0)], [lp["norm_mix"], lp["sc1"], lp["sh1"]], [dh], [F32], tm,
        adds={0: dx_a})
    if own:
        return dx, g, arrived, [got_wb, got_wout, got_dx[-1], got_wfo]
    return dx, g, arrived


WEIGHTS = ("w_ada", "b_ada", "norm_mix", "norm_ffn", "w_in", "b_merge", "hgrn_lb_logits", "hgrn_norm", "ssm_conv_w",
           "ssm_conv_b", "ssm_dt_bias", "ssm_a_log", "ssm_d", "ssm_norm", "gdn_conv_w", "gdn_dt_bias", "gdn_a_log",
           "gdn_norm", "w_branch", "w_out", "w_ffn_in", "w_ffn_out", "norm_final")
GATHERED = ("w_in", "w_branch", "w_out", "w_ffn_in", "w_ffn_out")
PACKET = ("b_ada", "norm_mix", "norm_ffn", "b_merge", "hgrn_norm", "ssm_conv_b", "ssm_dt_bias", "ssm_a_log", "ssm_d",
          "ssm_norm", "gdn_dt_bias", "gdn_a_log", "gdn_norm", "norm_final")
MISC = ("hgrn_lb_logits", "ssm_conv_w", "gdn_conv_w")


def _pack(arrs, dtype, row_mult, lead=0):
    flat = jnp.concatenate([t.reshape(t.shape[:lead] + (-1,)).astype(dtype) for t in arrs], axis=lead)
    n = flat.shape[-1]
    unit = row_mult * LANES
    tot = -(-n // unit) * unit
    flat = jnp.pad(flat, [(0, 0)] * lead + [(0, tot - n)])
    return flat.reshape(flat.shape[:lead] + (tot // LANES, LANES))


def _unpack(packed, shapes, lead=0):
    flat = packed.reshape(packed.shape[:lead] + (-1,))
    out, off = [], 0
    for shp in shapes:
        n = int(np.prod(shp))
        out.append(flat[..., off:off + n].reshape(flat.shape[:lead] + tuple(shp)))
        off += n
    return out


def _shard2d(t):
    return t.reshape((-1, t.shape[-1]))


def weights_from_shards(dm, l, got, idx):
    w_in, wb, w_out, wf, wfo = got
    d, mix = dm.d, dm.mix
    return dict(
        w_in=colgather(f"l{l}_w_in", w_in, idx, dm.np, BF16)[0],
        w_branch=wb.reshape(N_DEV, 3, mix, d // N_DEV).transpose(1, 2, 0, 3).reshape(3, mix, d),
        w_out=w_out.reshape(d, d), w_ffn_in=wf, w_ffn_out=wfo.reshape(N_DEV // 2, -1, d))


def small_shards(dm, g):
    d, mix = dm.d, dm.mix
    return [g["w_branch"].reshape(3, mix, N_DEV, d // N_DEV).transpose(2, 0, 1, 3).reshape(N_DEV, 3 * mix, d // N_DEV),
            g["w_out"].reshape(N_DEV, d // N_DEV, d), g["w_ffn_in"], g["w_ffn_out"].reshape(N_DEV, -1, d)]


def w_in_shards(dm, l, g, idx):
    return colgather(f"l{l}_g_w_in", g["w_in"][None], idx, dm.in_width // N_DEV, F32)


def layer_params(dm, l, full, small, mod_l, lb_l):
    d, mix = dm.d, dm.mix
    row = lambda t: t.reshape(1, -1)
    head = lambda t: t.reshape(-1, 1, 1)
    sh1, sc1, g1, sh2, sc2, g2 = (row(mod_l[i * d:(i + 1) * d]) for i in range(6))
    conv_b = jnp.concatenate([jnp.zeros((3 * mix,), F32), small["ssm_conv_b"][l]])
    return dict(
        w_in=full["w_in"], w_branch=full["w_branch"], w_out=full["w_out"],
        w_ffn_in=full["w_ffn_in"], w_ffn_out=full["w_ffn_out"],
        norm_mix=row(small["norm_mix"][l]), norm_ffn=row(small["norm_ffn"][l]), b_merge=row(small["b_merge"][l]),
        hgrn_norm=row(small["hgrn_norm"][l]), lb=row(lb_l),
        conv_w=jnp.concatenate([small["gdn_conv_w"][l], small["ssm_conv_w"][l]], axis=1), conv_b=row(conv_b),
        ssm_dt_bias=head(small["ssm_dt_bias"][l]), ssm_a_log=head(small["ssm_a_log"][l]),
        ssm_d_exp=row(jnp.repeat(small["ssm_d"][l], SSM_P)), ssm_norm=row(small["ssm_norm"][l]),
        gdn_dt_bias=head(small["gdn_dt_bias"][l]), gdn_a_log=head(small["gdn_a_log"][l]), gdn_norm=row(small["gdn_norm"][l]),
        sh1=sh1, sc1=sc1, g1=g1, sh2=sh2, sc2=sc2, g2=g2)


def layer_grads(dm, g):
    cs = 3 * dm.mix
    out = dict(
        w_in=g["w_in"], w_branch=g["w_branch"], w_out=g["w_out"], w_ffn_in=g["w_ffn_in"],
        w_ffn_out=g["w_ffn_out"], norm_mix=g["norm_mix"][0], norm_ffn=g["norm_ffn"][0], b_merge=g["b_merge"][0],
        hgrn_norm=g["hgrn_norm"][0], ssm_conv_w=g["conv_w"][:, cs:], gdn_conv_w=g["conv_w"][:, :cs],
        ssm_conv_b=g["conv_b"][0, cs:], ssm_dt_bias=g["ssm_dt_bias"][:, 0, 0], ssm_a_log=g["ssm_a_log"][:, 0, 0],
        ssm_d=g["ssm_d_exp"].reshape(dm.ssm_heads, SSM_P).sum(axis=1), ssm_norm=g["ssm_norm"][0],
        gdn_dt_bias=g["gdn_dt_bias"][:, 0, 0], gdn_a_log=g["gdn_a_log"][:, 0, 0], gdn_norm=g["gdn_norm"][0])
    dmod = jnp.concatenate([g[k][0] for k in ("sh1", "sc1", "g1", "sh2", "sc2", "g2")])
    return out, dmod, g["lb"][0]


def local_step(dm, nl, x, tgt, norm_final, params_of, gather_of=None, scatter_of=None):
    arrived = exchange("gather_w0", gather_of(0), True) if gather_of else None
    lps, saved = [], []
    for l in range(nl):
        lps.append(params_of(l, arrived))
        side = (gather_of(l + 1), True) if gather_of and l + 1 < nl else None
        x, sv, arrived = layer_fwd(dm, l, x, lps[l], side)
        saved.append(sv)
    loss, dx, dnf = loss_call("loss", x, tgt, norm_final, dm.tm)
    grads, parts, side = [None] * nl, [None] * nl, None
    for l in reversed(range(nl)):
        if scatter_of and l == 0:
            dx, grads[l], got, own = layer_bwd(dm, l, dx, lps[l], saved[l], side, own=True)
            parts[0] = list(exchange("scatter_g0", [scatter_of(0, grads[0])], False)) + own
        else:
            dx, grads[l], got = layer_bwd(dm, l, dx, lps[l], saved[l], side)
        if side is not None:
            parts[l + 1] = got
        side = ([scatter_of(l, grads[l])] + small_shards(dm, grads[l]), False) if scatter_of and l > 0 else None
    return loss, dx, dnf, grads, parts


def kernel(x, c, w_ada, b_ada, norm_mix, norm_ffn, w_in, b_merge, hgrn_lb_logits, hgrn_norm, ssm_conv_w, ssm_conv_b, ssm_dt_bias, ssm_a_log, ssm_d, ssm_norm, gdn_conv_w, gdn_dt_bias, gdn_a_log, gdn_norm, w_branch, w_out, w_ffn_in, w_ffn_out, norm_final, loss_target, m_w_ada, m_b_ada, m_norm_mix, m_norm_ffn, m_w_in, m_b_merge, m_hgrn_lb_logits, m_hgrn_norm, m_ssm_conv_w, m_ssm_conv_b, m_ssm_dt_bias, m_ssm_a_log, m_ssm_d, m_ssm_norm, m_gdn_conv_w, m_gdn_dt_bias, m_gdn_a_log, m_gdn_norm, m_w_branch, m_w_out, m_w_ffn_in, m_w_ffn_out, m_norm_final, v_w_ada, v_b_ada, v_norm_mix, v_norm_ffn, v_w_in, v_b_merge, v_hgrn_lb_logits, v_hgrn_norm, v_ssm_conv_w, v_ssm_conv_b, v_ssm_dt_bias, v_ssm_a_log, v_ssm_d, v_ssm_norm, v_gdn_conv_w, v_gdn_dt_bias, v_gdn_a_log, v_gdn_norm, v_w_branch, v_w_out, v_w_ffn_in, v_w_ffn_out, v_norm_final):
    a = dict(locals())
    x, tgt = a["x"][0], a["loss_target"][0]
    s, d = x.shape
    nl = a["w_ada"].shape[0]
    dm = Dims(s, d, a["w_ffn_out"].shape[1] * N_DEV)
    me = 4 * lax.axis_index("x") + 2 * lax.axis_index("y") + lax.axis_index("c")

    first = [a["c"], a["ssm_conv_w"], a["gdn_conv_w"]]
    c_all, scw, gcw = _unpack(exchange("gather_c", [_pack(first, F32, 8)], True)[0], [t.shape for t in first], lead=1)
    small = dict(a, ssm_conv_w=scw.transpose(1, 2, 0, 3).reshape(scw.shape[1:3] + (-1,)),
                 gdn_conv_w=gcw.transpose(1, 2, 0, 3).reshape(gcw.shape[1:3] + (-1,)))
    c_pad = jnp.zeros((LANES, d), F32).at[:N_DEV].set(c_all.reshape(N_DEV, d))
    ncol = a["w_ada"].shape[2]
    b_mine = lax.dynamic_slice(a["b_ada"], (0, me * ncol), (nl, ncol))[:, None, :]
    mod_part = ada_fwd("ada_fwd", c_pad, a["w_ada"], b_mine)[:, :N_DEV, :]
    (mod,) = exchange("a2a_mod", [mod_part.transpose(1, 0, 2)], False)
    mod = mod.transpose(1, 0, 2).reshape(nl, N_DEV * ncol)
    (lb,) = rowstage_fwd("lower_bounds", lower_bounds_fn, [(a["hgrn_lb_logits"], dm.mix, 0)], [], [(dm.mix, F32)], nl)

    idx_fwd, idx_bwd = w_in_tables(dm, N_DEV)
    loss, dx, dnf, grads, parts = local_step(
        dm, nl, x, tgt, a["norm_final"].reshape(1, d),
        params_of=lambda l, got: layer_params(dm, l, weights_from_shards(dm, l, got, idx_fwd), small, mod[l], lb[l]),
        gather_of=lambda l: [_shard2d(a[n][l]).astype(BF16) for n in GATHERED],
        scatter_of=lambda l, g: w_in_shards(dm, l, g, idx_bwd))

    per_layer = [layer_grads(dm, g) for g in grads]
    res = {}
    for i, n in enumerate(GATHERED):
        wmv = [a[q + n].reshape((nl, -1, a[n].shape[-1])) for q in ("", "m_", "v_")]
        outs = None
        for l in range(nl):
            outs = adamw_sum(f"adamw_l{l}_{n}", parts[l][i], *wmv, layer=l, into=outs)
        for kind, o in zip(("grad", "delta", "new_m", "new_v"), outs):
            res[(kind, n)] = o.reshape(a[n].shape)

    stackg = lambda n: jnp.stack([pl_[0][n] for pl_ in per_layer])
    dmod = jnp.stack([pl_[1] for pl_ in per_layer])
    dlb = jnp.stack([pl_[2] for pl_ in per_layer])
    pk_g = [dmod if n == "b_ada" else dnf if n == "norm_final" else stackg(n) for n in PACKET]
    extra = [dlb, stackg("ssm_conv_w"), stackg("gdn_conv_w"), loss[0, :1]]
    pk_shapes = [t.shape for t in pk_g + extra]
    zeros = [jnp.zeros(t.shape, F32) for t in extra]
    (parts,) = exchange("gather_small", [_pack(pk_g + extra, F32, 8)], True)
    outs = adamw_sum("adamw_small", parts, *[_pack([a[p + n] for n in PACKET] + zeros, F32, 8) for p in ("", "m_", "v_")])
    for kind, o in zip(("grad", "delta", "new_m", "new_v"), outs):
        un = _unpack(o, pk_shapes)
        for n, t in zip(PACKET, un):
            res[(kind, n)] = t.reshape(a[n].shape)
        if kind == "grad":
            dlb_sum, g_scw, g_gcw, loss_sum = un[len(PACKET):]

    (g_lb,), _ = rowstage_bwd("lower_bounds_b", lower_bounds_fn, [(a["hgrn_lb_logits"], dm.mix, 0)], [], [dlb_sum], [F32], nl)
    mine = lambda t, n: lax.dynamic_slice_in_dim(t, me * a[n].shape[-1], a[n].shape[-1], axis=t.ndim - 1)
    (dmod_cols,) = exchange("a2a_dmod", [dmod.reshape(nl, N_DEV, ncol).transpose(1, 0, 2)], False)
    dmod_pad = jnp.zeros((nl, LANES, ncol), F32).at[:, :N_DEV].set(dmod_cols.transpose(1, 0, 2))
    g_w_ada = ada_bwd("ada_bwd", c_pad, dmod_pad)
    outs = adamw_sum("adamw_w_ada", g_w_ada.reshape(1, nl * d, ncol), *[a[q + "w_ada"].reshape(nl * d, ncol) for q in ("", "m_", "v_")])
    for kind, o in zip(("grad", "delta", "new_m", "new_v"), outs):
        res[(kind, "w_ada")] = o.reshape(nl, d, ncol)
    g_misc = [g_lb, mine(g_scw, "ssm_conv_w"), mine(g_gcw, "gdn_conv_w")]
    outs = adamw_sum("adamw_misc", _pack(g_misc, F32, 8)[None], *[_pack([a[q + n] for n in MISC], F32, 8) for q in ("", "m_", "v_")])
    for kind, o in zip(("grad", "delta", "new_m", "new_v"), outs):
        for n, t in zip(MISC, _unpack(o, [a[n].shape for n in MISC])):
            res[(kind, n)] = t

    out = [loss_sum.reshape(()), dx[None]]
    for kind in ("grad", "delta", "new_m", "new_v"):
        out += [res[(kind, n)] for n in WEIGHTS]
    return tuple(out)
```

```python
import functools
import math

import numpy as np
import jax
import jax.numpy as jnp
from jax import lax
from jax.experimental import pallas as pl
from jax.experimental.pallas import tpu as pltpu

F32 = jnp.float32
BF16 = jnp.bfloat16

N_DEV = 8
CHUNK = 64
SUB = 8
HGRN_HEADS_PER_STEP = 6
GDN_HEADS_PER_STEP = 6
HEAD = 128
SSM_P = 64
CONV_K = 4
F_MIN = 1e-30
NORM_EPS = 1e-6
LANES = 128
GATHER_TILE = 256
VMEM_LIMIT = 56 * 1024 * 1024

ADAM_LR = 0.001
ADAM_B1 = 0.9
ADAM_B2 = 0.999
ADAM_EPS = 1e-08
ADAM_WD = 0.01
ADAM_STEP = 10


def _dg(a, b, ca, cb):
    return lax.dot_general(a.astype(BF16), b.astype(BF16), (((ca,), (cb,)), ((), ())),
                           preferred_element_type=F32)


def _split3(x):
    x1 = x.astype(BF16)
    r = x - x1.astype(F32)
    x2 = r.astype(BF16)
    x3 = (r - x2.astype(F32)).astype(BF16)
    return x1, x2, x3


def _hdg(a, b, ca, cb):
    a1, a2, _ = _split3(a)
    b1, b2, _ = _split3(b)
    dn = (((ca,), (cb,)), ((), ()))
    d = lambda p, q: lax.dot_general(p, q, dn, preferred_element_type=F32)
    return (d(a2, b1) + d(a1, b2)) + d(a1, b1)


def _dot_family(prim):
    @jax.custom_vjp
    def nn(a, b):
        return prim(a, b, 1, 0)

    @jax.custom_vjp
    def nt(a, b):
        return prim(a, b, 1, 1)

    @jax.custom_vjp
    def tn(a, b):
        return prim(a, b, 0, 0)

    nn.defvjp(lambda a, b: (nn(a, b), (a, b)), lambda r, g: (nt(g, r[1]), tn(r[0], g)))
    nt.defvjp(lambda a, b: (nt(a, b), (a, b)), lambda r, g: (nn(g, r[1]), tn(g, r[0])))
    tn.defvjp(lambda a, b: (tn(a, b), (a, b)), lambda r, g: (nt(r[1], g), nn(r[0], g)))
    return nn, nt, tn


mm_nn, mm_nt, mm_tn = _dot_family(_dg)
hd_nn, hd_nt, hd_tn = _dot_family(_hdg)


def _iota(shape, dim):
    return lax.broadcasted_iota(jnp.int32, shape, dim)


def _scan_rows(x, reverse):
    n = x.shape[0]
    rows = _iota(x.shape, 0)
    k = 1
    while k < n:
        if reverse:
            x = x + jnp.where(rows < n - k, pltpu.roll(x, n - k, 0), 0.0)
        else:
            x = x + jnp.where(rows >= k, pltpu.roll(x, k, 0), 0.0)
        k *= 2
    return x


@jax.custom_vjp
def cumsum_rows(x):
    return _scan_rows(x, False)


cumsum_rows.defvjp(lambda x: (_scan_rows(x, False), None), lambda _, g: (_scan_rows(g, True),))


def _sigmoid(x):
    return jax.nn.sigmoid(x)


def _silu(x):
    return x * jax.nn.sigmoid(x)


def _softplus(x):
    e = jnp.exp(-jnp.abs(x))
    small = e * (1.0 - e * (0.5 - e * (1.0 / 3.0)))
    return jnp.maximum(x, 0.0) + jnp.where(e < 1e-3, small, jnp.log(1.0 + e))


def _masked_exp(diff, mask):
    return jnp.where(mask, jnp.exp(jnp.where(mask, diff, 0.0)), 0.0)


def _rms(x, w):
    return x * lax.rsqrt(jnp.mean(x * x, axis=-1, keepdims=True) + NORM_EPS) * w


def _cum_col_row(lg_col, lg_row):
    c = lg_col.shape[0]
    r, s = _iota((c, c), 0), _iota((c, c), 1)
    cum_col = jnp.sum(jnp.where(s <= r, jnp.broadcast_to(lg_row, (c, c)), 0.0), axis=1, keepdims=True)
    cum_row = jnp.sum(jnp.where(r <= s, jnp.broadcast_to(lg_col, (c, c)), 0.0), axis=0, keepdims=True)
    total = jnp.sum(lg_col, axis=0, keepdims=True)
    return cum_col, cum_row, total


def hgrn_chunk(seq, hp, sp, st):
    (blk,), (lb,), (nw,) = seq, hp, sp
    c = blk.shape[0]
    q_raw, f_raw, v, g_raw = (blk[:, i * HEAD:(i + 1) * HEAD] for i in range(4))
    q = _silu(q_raw)
    f = lb + (1.0 - lb) * _sigmoid(f_raw)
    logf = jnp.log(jnp.maximum(f, F_MIN))
    k = (1.0 - lb) * _sigmoid(-f_raw)
    b = cumsum_rows(logf)
    o_inter = mm_nt(q * jnp.exp(b), st)
    nsub = c // SUB
    wide = (SUB, SUB, HEAD)
    er = _iota((SUB * SUB, SUB), 0)
    e_t = (er // SUB == _iota((SUB * SUB, SUB), 1)).astype(F32)
    pr = _iota((SUB * SUB, 1), 0)
    pmask = (pr % SUB) <= (pr // SUB)
    er64 = _iota((SUB * SUB, c), 0)
    ec64 = _iota((SUB * SUB, c), 1)
    rows_c = _iota((c, 1), 0)
    row = lambda a, i: jnp.sum(jnp.where(rows_c == i, a, 0.0), axis=0, keepdims=True)
    def sub_chunk(qi, ki, bi, bref, first, place):
        qb = jnp.broadcast_to(qi[:, None, :], wide).reshape(SUB * SUB, HEAD)
        kb = jnp.broadcast_to(ki[None, :, :], wide).reshape(SUB * SUB, HEAD)
        bd = (bi[:, None, :] - bi[None, :, :]).reshape(SUB * SUB, HEAD)
        sc_col = jnp.sum(qb * kb * _masked_exp(bd, pmask), axis=1, keepdims=True)
        sc = mm_tn(e_t, sc_col * place)
        sc = sc + mm_nt(qi * jnp.exp(bi - bref), k * _masked_exp(bref - b, rows_c < first))
        return mm_nn(sc, v)

    firsts = [SUB * i for i in range(nsub)]
    pile = lambda parts: jnp.concatenate([p[None] for p in parts], axis=0)
    cut = lambda a: a.reshape(nsub, SUB, HEAD)
    brefs = pile([row(b, f) for f in firsts])
    starts = pile([jnp.full((1, 1), f, jnp.int32) for f in firsts])
    places = pile([(ec64 == (er64 % SUB) + f).astype(F32) for f in firsts])
    o_intra = jax.vmap(sub_chunk)(cut(q), cut(k), cut(b), brefs, starts, places)
    o = o_inter + o_intra.reshape(c, HEAD)
    bend = row(b, c - 1)
    st_new = st * jnp.exp(bend) + mm_tn(v, k * jnp.exp(bend - b))
    y = _rms(o, nw) * _silu(g_raw)
    return (y,), st_new


def ssd_chunk(seq, hp, sp, st):
    xs, bm, cm, dtc, dtr = seq
    dt_bias, a_log = hp
    c, width = xs.shape
    nheads = width // SSM_P
    head_of = _iota((1, width), 1) // SSM_P
    r, s = _iota((c, c), 0), _iota((c, c), 1)
    g = mm_nt(cm, bm)
    dt_l, cum_l, end_l, scores = 0.0, 0.0, 0.0, []
    for i in range(nheads):
        neg_a = -jnp.exp(a_log[i])
        dt_col = _softplus(dtc[i] + dt_bias[i])
        dt_row = _softplus(dtr[i] + dt_bias[i])
        cum_col, cum_row, total = _cum_col_row(neg_a * dt_col, neg_a * dt_row)
        mine = head_of == i
        dt_l = dt_l + jnp.where(mine, dt_col, 0.0)
        cum_l = cum_l + jnp.where(mine, cum_col, 0.0)
        end_l = end_l + jnp.where(mine, total, 0.0)
        scores.append(g * _masked_exp(cum_col - cum_row, s <= r))
    xdt = xs * dt_l
    stacked = mm_nn(jnp.concatenate(scores, axis=0), xdt)
    y_intra = 0.0
    for i in range(nheads):
        y_intra = y_intra + jnp.where(head_of == i, stacked[i * c:(i + 1) * c], 0.0)
    y_inter = mm_nn(cm, st) * jnp.exp(cum_l)
    st_new = st * jnp.exp(end_l) + mm_tn(bm, xdt * jnp.exp(end_l - cum_l))
    return (y_intra + y_inter,), st_new


def _neumann_inverse(a):
    n = a.shape[0]
    eye = (_iota((n, n), 0) == _iota((n, n), 1)).astype(F32)
    p = -a
    t = eye + p
    for _ in range(int(math.log2(n)) - 1):
        p = _hdg(p, p, 1, 0)
        t = t + _hdg(t, p, 1, 0)
    return t


@jax.custom_vjp
def inv_unit_lower(a):
    return _neumann_inverse(a)


def _inv_fwd(a):
    t = _neumann_inverse(a)
    return t, t


inv_unit_lower.defvjp(_inv_fwd, lambda t, g: (-hd_nt(hd_tn(t, g), t),))


def gdn_chunk(seq, hp, sp, st):
    q_raw, k_raw, v, z, gbc, gac, gar = seq
    dt_bias, a_log = hp
    (nw,) = sp
    c = v.shape[0]
    r, s = _iota((c, c), 0), _iota((c, c), 1)
    q = q_raw * lax.rsqrt(jnp.sum(q_raw * q_raw, axis=-1, keepdims=True) + NORM_EPS) * (HEAD ** -0.5)
    k = k_raw * lax.rsqrt(jnp.sum(k_raw * k_raw, axis=-1, keepdims=True) + NORM_EPS)
    beta = _sigmoid(gbc)
    neg_a = -jnp.exp(a_log)
    cum, cum_row, total = _cum_col_row(neg_a * _softplus(gac + dt_bias), neg_a * _softplus(gar + dt_bias))
    decay = _masked_exp(cum - cum_row, s <= r)
    kk = mm_nt(k, k)
    a_low = jnp.where(s < r, beta * kk * decay, 0.0)
    sol = hd_nn(inv_unit_lower(a_low), jnp.concatenate([v * beta, k * (beta * jnp.exp(cum))], axis=1))
    u_base, w_corr = sol[:, :HEAD], sol[:, HEAD:]
    qk = mm_nt(q, k) * decay
    u = u_base - mm_nn(w_corr, st)
    o = mm_nn(q * jnp.exp(cum), st) + mm_nn(qk, u)
    st_new = jnp.exp(total) * st + mm_tn(k * jnp.exp(total - cum), u)
    y = _rms(o, nw) * _silu(z)
    return (y,), st_new


def normmod_fn(rows, params):
    (x,), (nw, sc, sh) = rows, params
    return (_rms(x, nw) * (1.0 + sc) + sh,)


def ssmpost_fn(rows, params):
    (y, xs, z), (d_exp, nw) = rows, params
    y = (y + d_exp * xs) * _silu(z)
    gw = y.shape[1] // 2
    return (jnp.concatenate([_rms(y[:, :gw], nw[:, :gw]), _rms(y[:, gw:], nw[:, gw:])], axis=1),)


def merge_fn(rows, params):
    (yh, ys, yg, gl), (bm, wb) = rows, params
    d = wb.shape[2]
    gates = _sigmoid(gl + bm)
    out = 0.0
    for n, y in enumerate((yh, ys, yg)):
        out = out + gates[:, n * d:(n + 1) * d] * mm_nn(y, wb[n])
    return (out,)


def outproj_fn(rows, params):
    (m, x), (g1, w) = rows, params
    return (x + (1.0 + g1) * mm_nn(m, w),)


def resid_fn(rows, params):
    (x, o), (g2,) = rows, params
    return (x + (1.0 + g2) * o,)


def _params(sem, side_effects=False):
    return pltpu.CompilerParams(dimension_semantics=sem, vmem_limit_bytes=VMEM_LIMIT, has_side_effects=side_effects)


def _whole(a):
    nd = a.ndim
    return pl.BlockSpec(a.shape, lambda *_: (0,) * nd)


def _pick(n, cands):
    for c in cands:
        if n % c == 0:
            return c
    return n


def matmul(name, a, b, mode, out_dtype, side=None):
    if mode == "nn":
        (m, k), n = a.shape, b.shape[1]
    elif mode == "nt":
        (m, k), n = a.shape, b.shape[0]
    else:
        (k, m), n = a.shape, b.shape[1]
    tm = _pick(m, (1024, 512, 256, 128))
    tn = _pick(n, (1280, 1024, 1408, 768, 512, 384, 256, 128))
    tk = _pick(k, (1024, 1280, 1408, 768, 512, 256, 128))
    if mode == "tn":
        tm = _pick(m, (1024, 768, 512, 256, 128))
        tk = _pick(k, (1024, 512, 256, 128))
    nk = k // tk
    ca, cb = {"nn": (1, 0), "nt": (1, 1), "tn": (0, 0)}[mode]

    def core(a_ref, b_ref, o_ref, acc_ref):
        kk = pl.program_id(2)

        @pl.when(kk == 0)
        def _():
            acc_ref[...] = jnp.zeros_like(acc_ref)

        acc_ref[...] += _dg(a_ref[...], b_ref[...], ca, cb)

        @pl.when(kk == nk - 1)
        def _():
            o_ref[...] = acc_ref[...].astype(o_ref.dtype)

    a_spec = (pl.BlockSpec((tk, tm), lambda i, j, q: (q, i)) if mode == "tn"
              else pl.BlockSpec((tm, tk), lambda i, j, q: (i, q)))
    b_spec = (pl.BlockSpec((tn, tk), lambda i, j, q: (j, q)) if mode == "nt"
              else pl.BlockSpec((tk, tn), lambda i, j, q: (q, j)))
    grid = (m // tm, n // tn, nk)
    body, s_in, s_out, s_shape, s_scr, s_args = _with_side(core, 2, 1, side, grid)
    sem = ("arbitrary",) * 3 if side else ("parallel", "parallel", "arbitrary")
    res = pl.pallas_call(
        body, name=name, grid=grid,
        in_specs=[a_spec, b_spec] + s_in,
        out_specs=[pl.BlockSpec((tm, tn), lambda i, j, q: (i, j))] + s_out,
        out_shape=[jax.ShapeDtypeStruct((m, n), out_dtype)] + s_shape,
        scratch_shapes=[pltpu.VMEM((tm, tn), F32)] + s_scr,
        compiler_params=_params(sem, side is not None),
    )(a, b, *s_args)
    return (res[0], res[1:]) if side else res[0]


def bmatmul(name, a, b, mode, out_dtype, out_batched):
    ab, bb = a.ndim == 3, b.ndim == 3
    nb = a.shape[0] if ab else b.shape[0]
    a2, b2 = a.shape[-2:], b.shape[-2:]
    if mode == "nn":
        (m, k), n = a2, b2[1]
    elif mode == "nt":
        (m, k), n = a2, b2[0]
    else:
        (k, m), n = a2, b2[1]
    tm = _pick(m, (1024, 512, 256, 128) if mode == "tn" else (512, 256, 128))
    tn = _pick(n, (1024, 512, 256, 128))
    tk = _pick(k, (512, 256, 128) if mode == "tn" else (1024, 512, 256, 128))
    nk = k // tk
    ca, cb = {"nn": (1, 0), "nt": (1, 1), "tn": (0, 0)}[mode]
    ids = (lambda g: g) if out_batched else (lambda g: (g[2], g[0], g[1], g[3]))
    grid = (nb, m // tm, n // tn, nk) if out_batched else (m // tm, n // tn, nb, nk)

    def a_map(*g):
        bi, i, j, q = ids(g)
        idx = (q, i) if mode == "tn" else (i, q)
        return (bi,) + idx if ab else idx

    def b_map(*g):
        bi, i, j, q = ids(g)
        idx = (j, q) if mode == "nt" else (q, j)
        return (bi,) + idx if bb else idx

    def o_map(*g):
        bi, i, j, q = ids(g)
        return (bi, i, j) if out_batched else (i, j)

    def body(a_ref, b_ref, o_ref, acc_ref):
        bi, _, _, q = ids(tuple(pl.program_id(d) for d in range(4)))
        first = (q == 0) if out_batched else (q == 0) & (bi == 0)
        last = (q == nk - 1) if out_batched else (q == nk - 1) & (bi == nb - 1)

        @pl.when(first)
        def _():
            acc_ref[...] = jnp.zeros_like(acc_ref)

        acc_ref[...] += _dg(a_ref[...], b_ref[...], ca, cb)

        @pl.when(last)
        def _():
            o_ref[...] = acc_ref[...].astype(o_ref.dtype)

    a_blk = (tk, tm) if mode == "tn" else (tm, tk)
    b_blk = (tn, tk) if mode == "nt" else (tk, tn)
    return pl.pallas_call(
        body, name=name, grid=grid,
        in_specs=[pl.BlockSpec(((None,) if ab else ()) + a_blk, a_map), pl.BlockSpec(((None,) if bb else ()) + b_blk, b_map)],
        out_specs=pl.BlockSpec(((None,) if out_batched else ()) + (tm, tn), o_map),
        out_shape=jax.ShapeDtypeStruct(((nb,) if out_batched else ()) + (m, n), out_dtype),
        scratch_shapes=[pltpu.VMEM((tm, tn), F32)],
        compiler_params=_params(("parallel", "parallel", "arbitrary", "arbitrary")),
    )(a, b)


def colgather(name, src, idx, dst_w, out_dtype):
    nsrc, rows, w = src.shape
    tw = GATHER_TILE
    nbs = -(-w // tw)
    ne = idx.shape[0]
    nbd = idx.shape[1] // tw
    tiles = [sorted(set((idx[e, t * tw:(t + 1) * tw][idx[e, t * tw:(t + 1) * tw] >= 0] // tw).tolist()))
             for e in range(ne) for t in range(nbd)]
    nslot = max(1, max(len(t) for t in tiles))
    tbl = np.full((ne * nbd, nslot), -1, np.int32)
    for i, t in enumerate(tiles):
        tbl[i, :len(t)] = t
    exact3 = src.dtype == F32

    def body(tbl_ref, idx_ref, src_ref, o_ref, acc_ref):
        ti, si = pl.program_id(0), pl.program_id(1)

        @pl.when(si == 0)
        def _():
            acc_ref[...] = jnp.zeros_like(acc_ref)

        t = tbl_ref[ti * nslot + si]

        @pl.when(t >= 0)
        def _():
            onehot = ((_iota((tw, tw), 0) + t * tw) == idx_ref[...]).astype(BF16)
            col = _iota((1, tw), 1) + (t % nbs) * tw
            xv = jnp.where(col < w, src_ref[...], jnp.zeros((), src_ref.dtype))
            d = lambda p: lax.dot_general(p, onehot, (((1,), (0,)), ((), ())), preferred_element_type=F32)
            if exact3:
                x1, x2, x3 = _split3(xv)
                acc_ref[...] += (d(x3) + d(x2)) + d(x1)
            else:
                acc_ref[...] += d(xv)

        @pl.when(si == nslot - 1)
        def _():
            o_ref[...] = acc_ref[...].astype(o_ref.dtype)

    def src_map(ti, si, tbl_ref):
        t = jnp.maximum(tbl_ref[ti * nslot + si], 0)
        return (t // nbs, 0, t % nbs)

    grid_spec = pltpu.PrefetchScalarGridSpec(
        num_scalar_prefetch=1, grid=(ne * nbd, nslot),
        in_specs=[pl.BlockSpec((None, 1, tw), lambda ti, si, tbl_ref: (ti // nbd, 0, ti % nbd)),
                  pl.BlockSpec((None, rows, tw), src_map)],
        out_specs=pl.BlockSpec((None, rows, tw), lambda ti, si, tbl_ref: (ti // nbd, 0, ti % nbd)),
        scratch_shapes=[pltpu.VMEM((rows, tw), F32)])
    return pl.pallas_call(
        body, name=name, grid_spec=grid_spec,
        out_shape=jax.ShapeDtypeStruct((ne, rows, dst_w), out_dtype),
        compiler_params=_params(("parallel", "arbitrary")),
    )(jnp.asarray(tbl.reshape(-1)), jnp.asarray(idx.reshape(ne, 1, nbd * tw).astype(np.int32)), src)


def swiglu3_fwd(name, gu, tm):
    _, nb, s, w = gu.shape

    def body(x_ref, o_ref):
        o_ref[...] = (_silu(x_ref[0].astype(F32)) * x_ref[1].astype(F32)).astype(o_ref.dtype)

    return pl.pallas_call(
        body, name=name, grid=(nb, s // tm),
        in_specs=[pl.BlockSpec((2, None, tm, w), lambda b, i: (0, b, i, 0))],
        out_specs=pl.BlockSpec((None, tm, w), lambda b, i: (b, i, 0)),
        out_shape=jax.ShapeDtypeStruct((nb, s, w), BF16),
        compiler_params=_params(("parallel", "parallel")),
    )(gu)


def swiglu3_bwd(name, gu, dact, tm):
    _, nb, s, w = gu.shape

    def body(x_ref, g_ref, o_ref):
        _, vjp = jax.vjp(lambda a, b: _silu(a) * b, x_ref[0].astype(F32), x_ref[1].astype(F32))
        dg, du = vjp(g_ref[...].astype(F32))
        o_ref[0] = dg.astype(o_ref.dtype)
        o_ref[1] = du.astype(o_ref.dtype)

    return pl.pallas_call(
        body, name=name, grid=(nb, s // tm),
        in_specs=[pl.BlockSpec((2, None, tm, w), lambda b, i: (0, b, i, 0)),
                  pl.BlockSpec((None, tm, w), lambda b, i: (b, i, 0))],
        out_specs=pl.BlockSpec((2, None, tm, w), lambda b, i: (0, b, i, 0)),
        out_shape=jax.ShapeDtypeStruct(gu.shape, BF16),
        compiler_params=_params(("parallel", "parallel")),
    )(gu, dact)


def _row_specs(rows, tm):
    return [pl.BlockSpec((tm, w), lambda i, _c=c: (i, _c)) for (_, w, c) in rows]


def rowstage_fwd(name, fn, rows, params, outs, tm):
    s = rows[0][0].shape[0]
    nr, npar = len(rows), len(params)

    def body(*refs):
        r = [x[...].astype(F32) for x in refs[:nr]]
        p = [x[...].astype(F32) for x in refs[nr:nr + npar]]
        for ref, val in zip(refs[nr + npar:], fn(r, p)):
            ref[...] = val.astype(ref.dtype)

    res = pl.pallas_call(
        body, name=name, grid=(s // tm,),
        in_specs=_row_specs(rows, tm) + [_whole(p) for p in params],
        out_specs=[pl.BlockSpec((tm, w), lambda i: (i, 0)) for (w, _) in outs],
        out_shape=[jax.ShapeDtypeStruct((s, w), dt) for (w, dt) in outs],
        compiler_params=_params(("parallel",)),
    )(*[r[0] for r in rows], *params)
    return res


def rowstage_bwd(name, fn, rows, params, douts, drow_dtypes, tm, adds=None, into=None):
    s = rows[0][0].shape[0]
    nr, npar, no = len(rows), len(params), len(douts)
    adds = adds or {}
    add_idx = sorted(adds)
    na = len(add_idx)
    into = into or {}
    into_idx = sorted(into)
    nb = len(into_idx)

    def body(*refs):
        r = [x[...].astype(F32) for x in refs[:nr]]
        p = [x[...].astype(F32) for x in refs[nr:nr + npar]]
        g = [x[...].astype(F32) for x in refs[nr + npar:nr + npar + no]]
        a_refs = refs[nr + npar + no:nr + npar + no + na]
        dr_refs = refs[nr + npar + no + na + nb:nr + npar + no + na + nb + nr]
        dp_refs = refs[nr + npar + no + na + nb + nr:]
        _, vjp = jax.vjp(lambda r_, p_: tuple(fn(r_, p_)), r, p)
        dr, dp = vjp(tuple(g))
        for j, (ref, val) in enumerate(zip(dr_refs, dr)):
            if j in adds:
                val = val + a_refs[add_idx.index(j)][...].astype(F32)
            ref[...] = val.astype(ref.dtype)

        @pl.when(pl.program_id(0) == 0)
        def _():
            for ref in dp_refs:
                ref[...] = jnp.zeros_like(ref)

        for ref, val in zip(dp_refs, dp):
            ref[...] += val

    res = pl.pallas_call(
        body, name=name, grid=(s // tm,),
        in_specs=(_row_specs(rows, tm) + [_whole(p) for p in params]
                  + [pl.BlockSpec((tm, d.shape[1]), lambda i: (i, 0)) for d in douts]
                  + [pl.BlockSpec((tm, rows[j][1]), lambda i: (i, 0)) for j in add_idx]
                  + [pl.BlockSpec(memory_space=pl.ANY)] * nb),
        out_specs=([pl.BlockSpec((tm, w), lambda i, _c=(into[j][1] if j in into else 0): (i, _c))
                    for j, (_, w, _) in enumerate(rows)] + [_whole(p) for p in params]),
        out_shape=([jax.ShapeDtypeStruct(into[j][0].shape if j in into else (s, w), dt)
                    for j, ((_, w, _), dt) in enumerate(zip(rows, drow_dtypes))]
                   + [jax.ShapeDtypeStruct(p.shape, F32) for p in params]),
        input_output_aliases={nr + npar + no + na + k: j for k, j in enumerate(into_idx)},
        compiler_params=_params(("arbitrary",)),
    )(*[r[0] for r in rows], *params, *douts, *[adds[j] for j in add_idx], *[into[j][0] for j in into_idx])
    return res[:nr], res[nr:]


def _flip(index_map, nc):
    return lambda h, n: index_map(h, nc - 1 - n)


def _with_side(core, n_in, n_out, side, grid):
    if side is None:
        return core, [], [], [], [], ()
    sends, broadcast = side
    k = len(sends)

    def body(*refs):
        ins, snd = refs[:n_in], refs[n_in:n_in + k]
        outs, rcv = refs[n_in + k:n_in + k + n_out], refs[n_in + k + n_out:n_in + 2 * k + n_out]
        scr = refs[n_in + 2 * k + n_out:]
        start, wait = _exchange_ops(snd, rcv, *scr[1:], broadcast)
        ids = [pl.program_id(d) for d in range(len(grid))]
        first = functools.reduce(lambda a, b: a & b, [i == 0 for i in ids])
        last = functools.reduce(lambda a, b: a & b, [i == g - 1 for i, g in zip(ids, grid)])
        pl.when(first)(start)
        core(*ins, *outs, scr[0])
        pl.when(last)(wait)

    return body, [HBM_SPEC] * k, [HBM_SPEC] * k, _exchange_out(sends, broadcast), _exchange_sems(k), tuple(sends)


def _take(v, split, j):
    if split is None:
        return v
    if split[0] == "lane":
        return v[:, j * split[1]:(j + 1) * split[1]]
    if split[0] == "lead":
        return v[j * split[1]:(j + 1) * split[1]]
    return v[j]


def _heads(vals, specs, hb):
    return [v if s[-1] is None else jnp.stack([_take(v, s[-1], j) for j in range(hb)]) for v, s in zip(vals, specs)]


def _over_heads(chunk_fn, hb, seqs, hparams, batched):
    seq_ax = [None if s[3] is None else 0 for s in seqs]
    hp_ax = [None if s[3] is None else 0 for s in hparams]
    if batched:
        return jax.vmap(chunk_fn, in_axes=(seq_ax, hp_ax, None, 0))

    def looped(seq, hp, sp, st):
        pick = lambda vals, axes, j: [v if a is None else v[j] for v, a in zip(vals, axes)]
        res = [chunk_fn(pick(seq, seq_ax, j), pick(hp, hp_ax, j), sp, st[j]) for j in range(hb)]
        pile = lambda parts: jnp.concatenate([p[None] for p in parts], axis=0)
        return tuple(pile(o) for o in zip(*[r[0] for r in res])), pile([r[1] for r in res])

    return looped


def _where(split, j):
    if split[0] == "lane":
        return (slice(None), slice(j * split[1], (j + 1) * split[1]))
    if split[0] == "lead":
        return (slice(j * split[1], (j + 1) * split[1]),)
    return (j,)


def scan_fwd(name, chunk_fn, nblk, hb, nc, seqs, hparams, sparams, state_shape, outs, batched, side=None):
    ns, nhp, nsp, no = len(seqs), len(hparams), len(sparams), len(outs)

    def core(*refs):
        seq_r, hp_r, sp_r = refs[:ns], refs[ns:ns + nhp], refs[ns + nhp:ns + nhp + nsp]
        out_r = refs[ns + nhp + nsp:ns + nhp + nsp + no]
        st_out, st_scr = refs[-2], refs[-1]

        @pl.when(pl.program_id(1) == 0)
        def _():
            st_scr[...] = jnp.zeros_like(st_scr)

        seq_v = [x[...].astype(F32) for x in seq_r]
        hp_v = [x[...] for x in hp_r]
        sp_v = [x[...] for x in sp_r]
        st = st_scr[...]
        st_out[...] = st
        heads = _over_heads(chunk_fn, hb, seqs, hparams, batched)
        o, st_new = heads(_heads(seq_v, seqs, hb), _heads(hp_v, hparams, hb), sp_v, st)
        for ref, spec, val in zip(out_r, outs, o):
            for j in range(hb):
                ref[_where(spec[4], j)] = val[j].astype(ref.dtype)
        st_scr[...] = st_new

    nst = len(state_shape)
    body, s_in, s_out, s_shape, s_scr, s_args = _with_side(core, ns + nhp + nsp, no + 1, side, (nblk, nc))
    res = pl.pallas_call(
        body, name=name, grid=(nblk, nc),
        in_specs=([pl.BlockSpec(bs, im) for (_, bs, im, _) in seqs]
                  + [pl.BlockSpec(bs, lambda h, n, _im=im: _im(h)) for (_, bs, im, _) in hparams]
                  + [_whole(p) for p in sparams] + s_in),
        out_specs=([pl.BlockSpec(bs, im) for (_, _, bs, im, _) in outs]
                   + [pl.BlockSpec((hb, None) + tuple(state_shape), lambda h, n: (h, n) + (0,) * nst)] + s_out),
        out_shape=([jax.ShapeDtypeStruct(fs, dt) for (fs, dt, _, _, _) in outs]
                   + [jax.ShapeDtypeStruct((nblk * hb, nc) + tuple(state_shape), F32)] + s_shape),
        scratch_shapes=[pltpu.VMEM((hb,) + tuple(state_shape), F32)] + s_scr,
        compiler_params=_params(("arbitrary", "arbitrary"), side is not None),
    )(*[x[0] for x in seqs], *[x[0] for x in hparams], *sparams, *s_args)
    return res[:no], res[no], res[no + 1:]


def scan_bwd(name, chunk_fn, nblk, hb, nc, seqs, hparams, sparams, state_shape, states, douts, dseqs, batched, side=None):
    ns, nhp, nsp, no = len(seqs), len(hparams), len(sparams), len(douts)
    nst = len(state_shape)
    buf_of = [i for i, sp in enumerate(dseqs) if len(sp) > 5 and sp[5] is not None]
    bufs = [dseqs[i][5] for i in buf_of]

    def core(*refs):
        seq_r, hp_r, sp_r = refs[:ns], refs[ns:ns + nhp], refs[ns + nhp:ns + nhp + nsp]
        base = ns + nhp + nsp
        st_r = refs[base]
        do_r = refs[base + 1:base + 1 + no]
        base += 1 + no + len(bufs)
        ds_r, dhp_r, dsp_r = refs[base:base + ns], refs[base + ns:base + ns + nhp], refs[base + ns + nhp:base + ns + nhp + nsp]
        dst_scr = refs[-1]
        h, n = pl.program_id(0), pl.program_id(1)

        @pl.when(n == 0)
        def _():
            dst_scr[...] = jnp.zeros_like(dst_scr)
            for ref in dhp_r:
                ref[...] = jnp.zeros_like(ref)

        @pl.when((n == 0) & (h == 0))
        def _():
            for ref in dsp_r:
                ref[...] = jnp.zeros_like(ref)

        seq_v = [x[...].astype(F32) for x in seq_r]
        hp_v = [x[...] for x in hp_r]
        sp_v = [x[...] for x in sp_r]
        do_v = [x[...].astype(F32) for x in do_r]
        prim = (_heads(seq_v, seqs, hb), _heads(hp_v, hparams, hb), sp_v, st_r[...])
        _, vjp = jax.vjp(_over_heads(chunk_fn, hb, seqs, hparams, batched), *prim)
        ds, dhp, dsp, dst = vjp((tuple(_heads(do_v, douts, hb)), dst_scr[...]))
        for ref, spec, val in zip(ds_r, dseqs, ds):
            if spec[4] is None:
                ref[...] = val.astype(ref.dtype)
            else:
                for j in range(hb):
                    ref[_where(spec[4], j)] = val[j].astype(ref.dtype)
        for ref, spec, val in zip(dhp_r, hparams, dhp):
            for j in range(hb):
                ref[_where(spec[3], j)] += val[j]
        for ref, val in zip(dsp_r, dsp):
            ref[...] += val
        dst_scr[...] = dst

    n_in, n_out = ns + nhp + nsp + 1 + no + len(bufs), ns + nhp + nsp
    body, s_in, s_out, s_shape, s_scr, s_args = _with_side(core, n_in, n_out, side, (nblk, nc))
    res = pl.pallas_call(
        body, name=name, grid=(nblk, nc),
        in_specs=([pl.BlockSpec(bs, _flip(im, nc)) for (_, bs, im, _) in seqs]
                  + [pl.BlockSpec(bs, lambda h, n, _im=im: _im(h)) for (_, bs, im, _) in hparams]
                  + [_whole(p) for p in sparams]
                  + [pl.BlockSpec((hb, None) + tuple(state_shape), lambda h, n: (h, nc - 1 - n) + (0,) * nst)]
                  + [pl.BlockSpec(bs, _flip(im, nc)) for (_, bs, im, _) in douts]
                  + [pl.BlockSpec(memory_space=pl.ANY)] * len(bufs) + s_in),
        out_specs=([pl.BlockSpec(sp[2], _flip(sp[3], nc)) for sp in dseqs]
                   + [pl.BlockSpec(bs, lambda h, n, _im=im: _im(h)) for (_, bs, im, _) in hparams]
                   + [_whole(p) for p in sparams] + s_out),
        out_shape=([jax.ShapeDtypeStruct(sp[0], sp[1]) for sp in dseqs]
                   + [jax.ShapeDtypeStruct(x[0].shape, F32) for x in hparams]
                   + [jax.ShapeDtypeStruct(p.shape, F32) for p in sparams] + s_shape),
        scratch_shapes=[pltpu.VMEM((hb,) + tuple(state_shape), F32)] + s_scr,
        input_output_aliases={n_in - len(bufs) + k: i for k, i in enumerate(buf_of)},
        compiler_params=_params(("arbitrary", "arbitrary"), side is not None),
    )(*[x[0] for x in seqs], *[x[0] for x in hparams], *sparams, states, *[x[0] for x in douts], *bufs, *s_args)
    return res[:ns], res[ns:ns + nhp], res[ns + nhp:n_out], res[n_out:]


def _shift_down(x, n, rows):
    if n == 0:
        return x
    return jnp.where(rows >= n, pltpu.roll(x, n, 0), 0.0)


def _shift_up(x, n, rows):
    if n == 0:
        return x
    s = x.shape[0]
    return jnp.where(rows < s - n, pltpu.roll(x, s - n, 0), 0.0)


def conv_fwd(name, x, col0, w, b):
    s, cw = x.shape[0], w.shape[1]

    def body(x_ref, w_ref, b_ref, o_ref):
        xv = x_ref[...]
        rows = _iota(xv.shape, 0)
        u = jnp.broadcast_to(b_ref[...], xv.shape)
        for j in range(CONV_K):
            u = u + w_ref[j:j + 1, :] * _shift_down(xv, CONV_K - 1 - j, rows)
        o_ref[...] = _silu(u)

    return pl.pallas_call(
        body, name=name, grid=(cw // LANES,),
        in_specs=[pl.BlockSpec((s, LANES), lambda j: (0, col0 + j)),
                  pl.BlockSpec((CONV_K, LANES), lambda j: (0, j)),
                  pl.BlockSpec((1, LANES), lambda j: (0, j))],
        out_specs=pl.BlockSpec((s, LANES), lambda j: (0, j)),
        out_shape=jax.ShapeDtypeStruct((s, cw), F32),
        compiler_params=_params(("parallel",)),
    )(x, w, b)


def conv_bwd(name, x, col0, w, b, dout, into):
    s, cw = x.shape[0], w.shape[1]

    def body(x_ref, w_ref, b_ref, g_ref, into_ref, dx_ref, dw_ref, db_ref):
        xv = x_ref[...]
        rows = _iota(xv.shape, 0)
        sh = [_shift_down(xv, CONV_K - 1 - j, rows) for j in range(CONV_K)]
        u = jnp.broadcast_to(b_ref[...], xv.shape)
        for j in range(CONV_K):
            u = u + w_ref[j:j + 1, :] * sh[j]
        sg = _sigmoid(u)
        du = g_ref[...] * (sg * (1.0 + u * (1.0 - sg)))
        dx = jnp.zeros_like(xv)
        for j in range(CONV_K):
            dx = dx + w_ref[j:j + 1, :] * _shift_up(du, CONV_K - 1 - j, rows)
            dw_ref[j:j + 1, :] = jnp.sum(du * sh[j], axis=0, keepdims=True)
        dx_ref[...] = dx.astype(dx_ref.dtype)
        db_ref[...] = jnp.sum(du, axis=0, keepdims=True)

    return pl.pallas_call(
        body, name=name, grid=(cw // LANES,),
        in_specs=[pl.BlockSpec((s, LANES), lambda j: (0, col0 + j)),
                  pl.BlockSpec((CONV_K, LANES), lambda j: (0, j)),
                  pl.BlockSpec((1, LANES), lambda j: (0, j)),
                  pl.BlockSpec((s, LANES), lambda j: (0, j)),
                  pl.BlockSpec(memory_space=pl.ANY)],
        out_specs=[pl.BlockSpec((s, LANES), lambda j: (0, col0 + j)),
                   pl.BlockSpec((CONV_K, LANES), lambda j: (0, j)),
                   pl.BlockSpec((1, LANES), lambda j: (0, j))],
        out_shape=[jax.ShapeDtypeStruct(into.shape, into.dtype), jax.ShapeDtypeStruct((CONV_K, cw), F32),
                   jax.ShapeDtypeStruct((1, cw), F32)],
        input_output_aliases={4: 0},
        compiler_params=_params(("parallel",)),
    )(x, w, b, dout, into)


def exchange(name, sends, broadcast):
    nop = len(sends)

    def body(*refs):
        start, wait = _exchange_ops(refs[:nop], refs[nop:2 * nop], *refs[2 * nop:], broadcast)
        start()
        wait()

    return pl.pallas_call(
        body, name=name,
        in_specs=[HBM_SPEC] * nop, out_specs=[HBM_SPEC] * nop,
        out_shape=_exchange_out(sends, broadcast), scratch_shapes=_exchange_sems(nop),
        compiler_params=pltpu.CompilerParams(has_side_effects=True),
    )(*sends)


HBM_SPEC = pl.BlockSpec(memory_space=pltpu.HBM)


def _exchange_out(sends, broadcast):
    return [jax.ShapeDtypeStruct((N_DEV,) + tuple(t.shape if broadcast else t.shape[1:]), t.dtype) for t in sends]


def _exchange_sems(nop):
    return [pltpu.SemaphoreType.DMA((nop * (N_DEV - 1),)), pltpu.SemaphoreType.DMA((nop * (N_DEV - 1),)),
            pltpu.SemaphoreType.DMA((nop,))]


def _gather_ops(send_refs, recv_refs, send_sems, recv_sems, local_sems):
    nop = len(send_refs)
    x, y, c = lax.axis_index("x"), lax.axis_index("y"), lax.axis_index("c")
    me, other = (x, y, c), (x, y, 1 - c)
    chips = [(1 - x, y), (x, 1 - y), (1 - x, 1 - y)]
    slab = lambda dev: 4 * dev[0] + 2 * dev[1] + dev[2]

    def copy(i, k, block, to, passed_on=False):
        return pltpu.make_async_remote_copy(
            src_ref=recv_refs[i].at[slab(block)] if passed_on else send_refs[i], dst_ref=recv_refs[i].at[slab(block)],
            send_sem=send_sems.at[i * (N_DEV - 1) + k], recv_sem=recv_sems.at[i * (N_DEV - 1) + k],
            device_id=to, device_id_type=pl.DeviceIdType.MESH)

    def local(i):
        return pltpu.make_async_copy(send_refs[i], recv_refs[i].at[slab(me)], local_sems.at[i])

    def start():
        for i in range(nop):
            local(i).start()
            copy(i, 0, me, other).start()
        for j, chip in enumerate(chips):
            for i in range(nop):
                copy(i, 1 + j, me, chip + (c,)).start()

    def wait():
        for j, chip in enumerate(chips):
            for i in range(nop):
                copy(i, 1 + j, chip + (c,), me).wait_recv()
                copy(i, 4 + j, chip + (c,), other, passed_on=True).start()
        for i in range(nop):
            copy(i, 0, other, me).wait_recv()
        for j, chip in enumerate(chips):
            for i in range(nop):
                copy(i, 4 + j, chip + (1 - c,), me, passed_on=True).wait_recv()
        for i in range(nop):
            copy(i, 0, me, other).wait_send()
            for j, chip in enumerate(chips):
                copy(i, 1 + j, me, chip + (c,)).wait_send()
                copy(i, 4 + j, chip + (c,), other, passed_on=True).wait_send()
            local(i).wait()

    return start, wait


def _exchange_ops(send_refs, recv_refs, send_sems, recv_sems, local_sems, broadcast):
    if broadcast:
        return _gather_ops(send_refs, recv_refs, send_sems, recv_sems, local_sems)
    nop = len(send_refs)
    x, y, c = lax.axis_index("x"), lax.axis_index("y"), lax.axis_index("c")
    me = 4 * x + 2 * y + c
    peers = []
    for k in range(1, N_DEV):
        px = 1 - x if (k >> 2) & 1 else x
        py = 1 - y if (k >> 1) & 1 else y
        pc = 1 - c if k & 1 else c
        peers.append(((px, py, pc), 4 * px + 2 * py + pc))

    def src(i, peer):
        return send_refs[i].at[peer]

    def remote(i, k, arrival):
        dev, peer = peers[k]
        return pltpu.make_async_remote_copy(
            src_ref=src(i, peer), dst_ref=recv_refs[i].at[peer if arrival else me],
            send_sem=send_sems.at[i * (N_DEV - 1) + k], recv_sem=recv_sems.at[i * (N_DEV - 1) + k],
            device_id=dev, device_id_type=pl.DeviceIdType.MESH)

    def local(i):
        return pltpu.make_async_copy(src(i, me), recv_refs[i].at[me], local_sems.at[i])

    def start():
        for i in range(nop):
            local(i).start()
        for k in range(N_DEV - 1):
            for i in range(nop):
                remote(i, k, False).start()

    def wait():
        for k in range(N_DEV - 1):
            for i in range(nop):
                remote(i, k, True).wait_recv()
        for k in range(N_DEV - 1):
            for i in range(nop):
                remote(i, k, False).wait_send()
        for i in range(nop):
            local(i).wait()

    return start, wait


def adamw_sum(name, parts, w, m, v, layer=None, into=None):
    rws, cols = w.shape[-2:]
    nsum = parts.shape[0]
    tr = _pick(rws, (256, 128, 64, 32, 16, 8))
    c1 = 1.0 / (1.0 - ADAM_B1 ** ADAM_STEP)
    c2 = 1.0 / (1.0 - ADAM_B2 ** ADAM_STEP)

    def body(p_ref, w_ref, m_ref, v_ref, *rest):
        g_ref, d_ref, nm_ref, nv_ref = rest[-4:]
        g = p_ref[0]
        for j in range(1, nsum):
            g = g + p_ref[j]
        nm = ADAM_B1 * m_ref[...] + (1.0 - ADAM_B1) * g
        nv = ADAM_B2 * v_ref[...] + (1.0 - ADAM_B2) * (g * g)
        g_ref[...] = g
        nm_ref[...] = nm
        nv_ref[...] = nv
        d_ref[...] = -ADAM_LR * ((nm * c1) / (jnp.sqrt(nv * c2) + ADAM_EPS) + ADAM_WD * w_ref[...])

    if layer is None:
        blk = pl.BlockSpec((tr, cols), lambda i: (i, 0))
    else:
        blk = pl.BlockSpec((None, tr, cols), lambda i: (layer, i, 0))
    if into is None and layer is not None:
        into = [lax.empty(w.shape, F32) for _ in range(4)]
    extra = list(into) if into else []
    return pl.pallas_call(
        body, name=name, grid=(rws // tr,),
        in_specs=([pl.BlockSpec((nsum, tr, cols), lambda i: (0, i, 0)), blk, blk, blk]
                  + [pl.BlockSpec(memory_space=pl.ANY)] * len(extra)),
        out_specs=[blk, blk, blk, blk],
        out_shape=[jax.ShapeDtypeStruct(w.shape, F32)] * 4,
        input_output_aliases={4 + k: k for k in range(len(extra))},
        compiler_params=_params(("parallel",)),
    )(parts, w, m, v, *extra)


def ada_fwd(name, c_all, w, b):
    nl = w.shape[0]

    def body(c_ref, w_ref, b_ref, o_ref):
        ca = _silu(c_ref[...])
        for l in range(nl):
            o_ref[l] = mm_nn(ca, w_ref[l]) + b_ref[l]

    return pl.pallas_call(
        body, name=name,
        out_shape=jax.ShapeDtypeStruct((nl, c_all.shape[0], w.shape[2]), F32),
        compiler_params=pltpu.CompilerParams(vmem_limit_bytes=VMEM_LIMIT),
    )(c_all, w, b)


def ada_bwd(name, c_all, dmod):
    nl = dmod.shape[0]

    def body(c_ref, g_ref, o_ref):
        ca = _silu(c_ref[...])
        for l in range(nl):
            o_ref[l] = mm_tn(ca, g_ref[l])

    return pl.pallas_call(
        body, name=name,
        out_shape=jax.ShapeDtypeStruct((nl, c_all.shape[1], dmod.shape[2]), F32),
        compiler_params=pltpu.CompilerParams(vmem_limit_bytes=VMEM_LIMIT),
    )(c_all, dmod)


def lower_bounds_fn(rows, params):
    (lg,), _ = rows, params
    nl = lg.shape[0]
    mx = jnp.max(lg, axis=0, keepdims=True)
    e = jnp.exp(lg - mx)
    p = e / jnp.sum(e, axis=0, keepdims=True)
    layer = _iota((nl, 1), 0)
    acc = jnp.zeros_like(p)
    for j in range(1, nl):
        pj = jnp.sum(jnp.where(layer == j, p, 0.0), axis=0, keepdims=True)
        acc = acc + jnp.where(layer >= j, 1.0, 0.0) * pj
    return (acc,)


def loss_call(name, x, tgt, nw, tm):
    s, d = x.shape

    def body(x_ref, t_ref, w_ref, l_ref, dx_ref, dw_ref):
        def f(xv, wv):
            err = _rms(xv, wv) - t_ref[...]
            return jnp.sum(0.5 * jnp.mean(err * err, axis=-1, keepdims=True), axis=0, keepdims=True)

        val, vjp = jax.vjp(f, x_ref[...], w_ref[...])
        dx, dw = vjp(jnp.ones_like(val))

        @pl.when(pl.program_id(0) == 0)
        def _():
            l_ref[...] = jnp.zeros_like(l_ref)
            dw_ref[...] = jnp.zeros_like(dw_ref)

        l_ref[...] += jnp.broadcast_to(val, l_ref.shape)
        dw_ref[...] += dw
        dx_ref[...] = dx

    row = pl.BlockSpec((tm, d), lambda i: (i, 0))
    return pl.pallas_call(
        body, name=name, grid=(s // tm,),
        in_specs=[row, row, _whole(nw)],
        out_specs=[pl.BlockSpec((8, LANES), lambda i: (0, 0)), row, _whole(nw)],
        out_shape=[jax.ShapeDtypeStruct((8, LANES), F32), jax.ShapeDtypeStruct((s, d), F32),
                   jax.ShapeDtypeStruct(nw.shape, F32)],
        compiler_params=_params(("arbitrary",)),
    )(x, tgt, nw)


class Dims:
    def __init__(self, s, d, ffn):
        self.s, self.d, self.ffn = s, d, ffn
        self.mix = 3 * d // 4
        self.nh = self.mix // HEAD
        self.ssm_heads = self.mix // SSM_P
        self.pairs = self.mix // (2 * SSM_P)
        self.nc = s // CHUNK
        self.conv_ssm = self.mix + 4 * HEAD
        self.conv_w = self.conv_ssm + 3 * self.mix
        self.o_gates = 4 * self.mix
        self.o_sz = self.o_gates + 3 * d
        self.o_gz = self.o_sz + self.mix
        self.o_conv = self.o_gz + self.mix
        self.o_small = self.o_conv + self.conv_w
        used = self.o_small + LANES
        self.np = -(-used // 1280) * 1280
        self.tm = _pick(s, (256, 128, 64))
        mix, nh = self.mix, self.nh
        self.in_sizes = (mix, mix, mix, mix, mix, self.conv_ssm, self.ssm_heads, 3 * mix, mix, nh, nh, 3 * d)
        self.in_width = sum(self.in_sizes)


def w_in_tables(dm, nshard):
    off = np.cumsum((0,) + dm.in_sizes)
    hq, hf, hi, hg, sz, sxbc, sdt, gqkv, gz, gb, ga, gates = (np.arange(off[i], off[i + 1]) for i in range(12))
    hgrn = np.stack([t.reshape(dm.nh, HEAD) for t in (hq, hf, hi, hg)], axis=1).reshape(-1)
    perm = np.concatenate([hgrn, gates, sz, gz, gqkv, sxbc, sdt, gb, ga])
    perm = np.concatenate([perm, np.full(dm.np - perm.size, -1)])
    shard = dm.in_width // nshard
    wpad = -(-shard // GATHER_TILE) * GATHER_TILE
    fwd = np.where(perm >= 0, (perm // shard) * wpad + perm % shard, -1)[None]
    inv = np.zeros(dm.in_width, np.int64)
    inv[perm[perm >= 0]] = np.nonzero(perm >= 0)[0]
    bwd = np.full((nshard, wpad), -1)
    bwd[:, :shard] = inv.reshape(nshard, shard)
    return fwd.astype(np.int32), bwd.astype(np.int32)


def _small_views(dm, small):
    t = small.T
    col = lambda a: a[:, :, None]
    row = lambda a: a.reshape(a.shape[0], dm.nc, 1, CHUNK)
    a, b = dm.ssm_heads, dm.ssm_heads + dm.nh
    sdt, gb, ga = t[:a], t[a:b], t[b:b + dm.nh]
    return col(sdt), row(sdt), col(gb), col(ga), row(ga)


def _scan_specs(dm, proj, conv_out, views, lp, dproj=None):
    dt_col, dt_row, gb_col, ga_col, ga_row = views
    mixb, nh = dm.mix // LANES, dm.nh
    s, mix = dm.s, dm.mix
    lane = ("lane", LANES)
    hb = HGRN_HEADS_PER_STEP
    hw = (CHUNK, hb * LANES)
    hgrn = dict(
        nblk=nh // hb, hb=hb, fn=hgrn_chunk, batched=False, state=(HEAD, HEAD),
        seqs=[(proj, (CHUNK, hb * 4 * HEAD), lambda h, n: (n, h), ("lane", 4 * HEAD))],
        hparams=[(lp["lb"], (1, hb * HEAD), lambda h: (0, h), lane)],
        sparams=[lp["hgrn_norm"]],
        dseqs=[((s, dm.np), BF16, (CHUNK, hb * 4 * HEAD), lambda h, n: (n, h), ("lane", 4 * HEAD), dproj)],
        io=(hw, lambda h, n: (n, h), lane))
    ppg = dm.pairs // 2
    qb = 3 * mixb
    gw = (CHUNK, ppg * LANES)
    group = ("lane", ppg * LANES)
    pcol = ((2 * ppg, CHUNK, 1), lambda g, n: (g, n, 0), ("lead", 2 * ppg))
    prow = ((2 * ppg, None, 1, CHUNK), lambda g, n: (g, n, 0, 0), ("lead", 2 * ppg))
    ppar = ((2 * ppg, 1, 1), lambda g: (g, 0, 0), ("lead", 2 * ppg))
    bc = lambda first: ((CHUNK, LANES), lambda g, n: (n, first + g), None)
    ssd = dict(
        nblk=2, hb=1, fn=ssd_chunk, batched=False, state=(HEAD, ppg * LANES),
        seqs=[(conv_out, gw, lambda g, n: (n, qb // ppg + g), group), (conv_out,) + bc(qb + mixb), (conv_out,) + bc(qb + mixb + 2),
              (dt_col,) + pcol, (dt_row,) + prow],
        hparams=[(lp["ssm_dt_bias"],) + ppar, (lp["ssm_a_log"],) + ppar],
        sparams=[],
        dseqs=[((s, mix), F32, gw, lambda g, n: (n, g), group), ((s, 2 * LANES), F32) + bc(0), ((s, 2 * LANES), F32) + bc(0),
               (dt_col.shape, F32) + pcol, (dt_row.shape, F32) + prow],
        io=(gw, lambda g, n: (n, g), group))
    hb = GDN_HEADS_PER_STEP
    hw = (CHUNK, hb * LANES)
    cq, cgz = 0, dm.o_gz // LANES
    assert nh % hb == 0 and cgz % hb == 0 and qb % ppg == 0
    hcol = ((hb, CHUNK, 1), lambda h, n: (h, n, 0), ("idx",))
    hrow = ((hb, None, 1, CHUNK), lambda h, n: (h, n, 0, 0), ("idx",))
    hpar = ((hb, 1, 1), lambda h: (h, 0, 0), ("idx",))
    at = lambda first: (hw, lambda h, n: (n, first // hb + h), lane)
    gdn = dict(
        nblk=nh // hb, hb=hb, fn=gdn_chunk, batched=True, state=(HEAD, HEAD),
        seqs=[(conv_out,) + at(cq), (conv_out,) + at(cq + nh), (conv_out,) + at(cq + 2 * nh), (proj,) + at(cgz),
              (gb_col,) + hcol, (ga_col,) + hcol, (ga_row,) + hrow],
        hparams=[(lp["gdn_dt_bias"],) + hpar, (lp["gdn_a_log"],) + hpar],
        sparams=[lp["gdn_norm"]],
        dseqs=[((s, mix), F32) + at(0), ((s, mix), F32) + at(0), ((s, mix), F32) + at(0), ((s, dm.np), BF16) + at(cgz) + (dproj,),
               (gb_col.shape, F32) + hcol, (ga_col.shape, F32) + hcol, (ga_row.shape, F32) + hrow],
        io=(hw, lambda h, n: (n, h), lane))
    return hgrn, ssd, gdn


def _run_scan_fwd(dm, name, sp, side=None):
    out = ((dm.s, dm.mix), F32) + sp["io"]
    (y,), states, arrived = scan_fwd(name, sp["fn"], sp["nblk"], sp["hb"], dm.nc, sp["seqs"], sp["hparams"],
                                     sp["sparams"], sp["state"], [out], sp["batched"], side)
    return y, states, arrived


def _run_scan_bwd(dm, name, sp, states, dy, side=None):
    return scan_bwd(name, sp["fn"], sp["nblk"], sp["hb"], dm.nc, sp["seqs"], sp["hparams"], sp["sparams"], sp["state"],
                    states, [(dy,) + sp["io"]], sp["dseqs"], sp["batched"], side)


SHARE_FWD = ((4, 1), (2,), (0,), (3,))
SHARE_BWD = ((0,), (4, 2), (3,), (1,))


def _share_out(side, share):
    if side is None:
        return None, None, None, None
    s, broadcast = side
    return tuple(([s[i] for i in idx], broadcast) for idx in share)


def _collect(share, *got):
    if not got[0]:
        return None
    out = [None] * len(GATHERED)
    for idx, arrived in zip(share, got):
        for i, t in zip(idx, arrived):
            out[i] = t
    return out


def layer_fwd(dm, l, x, lp, side=None):
    tm, d, mix = dm.tm, dm.d, dm.mix
    tag = f"l{l}_"
    (h,) = rowstage_fwd(tag + "norm1", normmod_fn, [(x, d, 0)], [lp["norm_mix"], lp["sc1"], lp["sh1"]], [(d, BF16)], tm)
    side_h, side_s, side_g, side_m = _share_out(side, SHARE_FWD)
    if side:
        proj, got_m = matmul(tag + "proj", h, lp["w_in"], "nn", F32, side_m)
    else:
        proj, got_m = matmul(tag + "proj", h, lp["w_in"], "nn", F32), None
    conv_out = conv_fwd(tag + "conv", proj, dm.o_conv // LANES, lp["conv_w"], lp["conv_b"])
    small = proj[:, dm.o_small:dm.o_small + LANES]
    views = _small_views(dm, small)
    hg, sd, gd = _scan_specs(dm, proj, conv_out, views, lp)
    yh, st_h, got_h = _run_scan_fwd(dm, tag + "hgrn", hg, side_h)
    y_ssd, st_s, got_s = _run_scan_fwd(dm, tag + "ssd", sd, side_s)
    yg, st_g, got_g = _run_scan_fwd(dm, tag + "gdn", gd, side_g)
    arrived = _collect(SHARE_FWD, got_h, got_s, got_g, got_m)
    (ys,) = rowstage_fwd(tag + "ssmpost", ssmpost_fn,
                         [(y_ssd, mix, 0), (conv_out, mix, 3), (proj, mix, dm.o_sz // mix)],
                         [lp["ssm_d_exp"], lp["ssm_norm"]], [(mix, F32)], tm)
    (merged,) = rowstage_fwd(tag + "merge", merge_fn, [(yh, mix, 0), (ys, mix, 0), (yg, mix, 0), (proj, 3 * d, 1)],
                             [lp["b_merge"], lp["w_branch"]], [(d, BF16)], tm)
    (x1,) = rowstage_fwd(tag + "outproj", outproj_fn, [(merged, d, 0), (x, d, 0)], [lp["g1"], lp["w_out"]], [(d, F32)], tm)
    (h2,) = rowstage_fwd(tag + "norm2", normmod_fn, [(x1, d, 0)], [lp["norm_ffn"], lp["sc2"], lp["sh2"]], [(d, BF16)], tm)
    gu = bmatmul(tag + "ffn_in", h2, lp["w_ffn_in"], "nn", BF16, True)
    gu = gu.reshape((2, gu.shape[0] // 2) + gu.shape[1:])
    act = swiglu3_fwd(tag + "swiglu", gu, tm)
    o2 = bmatmul(tag + "ffn_out", act, lp["w_ffn_out"], "nn", F32, False)
    (x2,) = rowstage_fwd(tag + "resid", resid_fn, [(x1, d, 0), (o2, d, 0)], [lp["g2"]], [(d, F32)], tm)
    saved = dict(x=x, h=h, proj=proj, conv_out=conv_out, views=views, yh=yh, y_ssd=y_ssd, yg=yg, ys=ys,
                 st_h=st_h, st_s=st_s, st_g=st_g, merged=merged, x1=x1, h2=h2, gu=gu, act=act, o2=o2)
    return x2, saved, arrived


def layer_bwd(dm, l, dx2, lp, sv, side=None, own=False):
    tm, d, mix, s = dm.tm, dm.d, dm.mix, dm.s
    tag = f"l{l}_b_"
    g = {}
    (dx1_a, do2), (g["g2"],) = rowstage_bwd(tag + "resid", resid_fn, [(sv["x1"], d, 0), (sv["o2"], d, 0)], [lp["g2"]],
                                            [dx2], [F32, BF16], tm)
    dact = bmatmul(tag + "ffn_out_dx", do2, lp["w_ffn_out"], "nt", BF16, True)
    g["w_ffn_out"] = bmatmul(tag + "ffn_out_dw", sv["act"], do2, "tn", F32, True)
    dgu = swiglu3_bwd(tag + "swiglu", sv["gu"], dact, tm)
    dgu = dgu.reshape((-1,) + dgu.shape[2:])
    dh2 = bmatmul(tag + "ffn_in_dx", dgu, lp["w_ffn_in"], "nt", BF16, False)
    g["w_ffn_in"] = bmatmul(tag + "ffn_in_dw", sv["h2"], dgu, "tn", F32, True)
    (dx1,), (g["norm_ffn"], g["sc2"], g["sh2"]) = rowstage_bwd(
        tag + "norm2", normmod_fn, [(sv["x1"], d, 0)], [lp["norm_ffn"], lp["sc2"], lp["sh2"]], [dh2], [F32], tm,
        adds={0: dx1_a})
    (dmerged, dx_a), (g["g1"], g["w_out"]) = rowstage_bwd(
        tag + "outproj", outproj_fn, [(sv["merged"], d, 0), (sv["x"], d, 0)], [lp["g1"], lp["w_out"]], [dx1],
        [BF16, F32], tm)
    proj, conv_out = sv["proj"], sv["conv_out"]
    dproj = lax.empty((s, dm.np), BF16)
    (dyh, dys, dyg, dproj), (g["b_merge"], g["w_branch"]) = rowstage_bwd(
        tag + "merge", merge_fn, [(sv["yh"], mix, 0), (sv["ys"], mix, 0), (sv["yg"], mix, 0), (proj, 3 * d, 1)],
        [lp["b_merge"], lp["w_branch"]], [dmerged], [F32, F32, F32, BF16], tm, into={3: (dproj, 1)})
    (dy_ssd, dxs_a, dproj), (g["ssm_d_exp"], g["ssm_norm"]) = rowstage_bwd(
        tag + "ssmpost", ssmpost_fn, [(sv["y_ssd"], mix, 0), (conv_out, mix, 3), (proj, mix, dm.o_sz // mix)],
        [lp["ssm_d_exp"], lp["ssm_norm"]], [dys], [F32, F32, BF16], tm, into={2: (dproj, dm.o_sz // mix)})
    side_h, side_s, side_g, side_m = _share_out(side, SHARE_BWD)
    hg, sd, _ = _scan_specs(dm, proj, conv_out, sv["views"], lp, dproj)
    (dproj,), (g["lb"],), (g["hgrn_norm"],), got_h = _run_scan_bwd(dm, tag + "hgrn", hg, sv["st_h"], dyh, side_h)
    gd = _scan_specs(dm, proj, conv_out, sv["views"], lp, dproj)[2]
    (dxs_b, dbp, dcp, d_dt_col, d_dt_row), (g["ssm_dt_bias"], g["ssm_a_log"]), _, got_s = _run_scan_bwd(
        dm, tag + "ssd", sd, sv["st_s"], dy_ssd, side_s)
    (dq, dk, dv, dproj, d_gb_col, d_ga_col, d_ga_row), (g["gdn_dt_bias"], g["gdn_a_log"]), (g["gdn_norm"],), got_g = _run_scan_bwd(
        dm, tag + "gdn", gd, sv["st_g"], dyg, side_g)
    dconv =jnp.concatenate([dq, dk, dv, dxs_a + dxs_b, dbp, dcp], axis=1)
    dproj, g["conv_w"], g["conv_b"] = conv_bwd(tag + "conv", proj, dm.o_conv // LANES, lp["conv_w"], lp["conv_b"], dconv,
                                               dproj)
    unrow = lambda t: t.reshape(t.shape[0], s).T
    dsmall = jnp.concatenate([d_dt_col[:, :, 0].T + unrow(d_dt_row), d_gb_col[:, :, 0].T,
                              d_ga_col[:, :, 0].T + unrow(d_ga_row)], axis=1)
    tail = jnp.pad(dsmall.astype(BF16), ((0, 0), (0, dm.np - dm.o_small - dsmall.shape[1])))
    dproj = lax.dynamic_update_slice(dproj, tail, (0, dm.o_small))
    beside_dx = list(side_m[0]) if side else []
    if own:
        s_wb, s_wout, s_wf, s_wfo = small_shards(dm, g)
        beside_dx = beside_dx + [s_wf]
        g["w_in"], (got_wb, got_wout, got_wfo) = matmul(tag + "proj_dw", sv["h"], dproj, "tn", F32,
                                                        ([s_wb, s_wout, s_wfo], False))
    else:
        g["w_in"] = matmul(tag + "proj_dw", sv["h"], dproj, "tn", F32)
    if beside_dx:
        dh, got_dx = matmul(tag + "proj_dx", dproj, lp["w_in"], "nt", BF16, (beside_dx, False))
    else:
        dh, got_dx = matmul(tag + "proj_dx", dproj, lp["w_in"], "nt", BF16), []
    arrived = _collect(SHARE_BWD, got_h, got_s, got_g, got_dx[:1]) if side else None
    (dx,), (g["norm_mix"], g["sc1"], g["sh1"]) = rowstage_bwd(
        tag + "norm1", normmod_fn, [(sv["x"], d, 0)], [lp["norm_mix"], lp["sc1"], lp["sh1"]], [dh], [F32], tm,
        adds={0: dx_a})
    if own:
        return dx, g, arrived, [got_wb, got_wout, got_dx[-1], got_wfo]
    return dx, g, arrived


WEIGHTS = ("w_ada", "b_ada", "norm_mix", "norm_ffn", "w_in", "b_merge", "hgrn_lb_logits", "hgrn_norm", "ssm_conv_w",
           "ssm_conv_b", "ssm_dt_bias", "ssm_a_log", "ssm_d", "ssm_norm", "gdn_conv_w", "gdn_dt_bias", "gdn_a_log",
           "gdn_norm", "w_branch", "w_out", "w_ffn_in", "w_ffn_out", "norm_final")
GATHERED = ("w_in", "w_branch", "w_out", "w_ffn_in", "w_ffn_out")
PACKET = ("b_ada", "norm_mix", "norm_ffn", "b_merge", "hgrn_norm", "ssm_conv_b", "ssm_dt_bias", "ssm_a_log", "ssm_d",
          "ssm_norm", "gdn_dt_bias", "gdn_a_log", "gdn_norm", "norm_final")
MISC = ("hgrn_lb_logits", "ssm_conv_w", "gdn_conv_w")


def _pack(arrs, dtype, row_mult, lead=0):
    flat = jnp.concatenate([t.reshape(t.shape[:lead] + (-1,)).astype(dtype) for t in arrs], axis=lead)
    n = flat.shape[-1]
    unit = row_mult * LANES
    tot = -(-n // unit) * unit
    flat = jnp.pad(flat, [(0, 0)] * lead + [(0, tot - n)])
    return flat.reshape(flat.shape[:lead] + (tot // LANES, LANES))


def _unpack(packed, shapes, lead=0):
    flat = packed.reshape(packed.shape[:lead] + (-1,))
    out, off = [], 0
    for shp in shapes:
        n = int(np.prod(shp))
        out.append(flat[..., off:off + n].reshape(flat.shape[:lead] + tuple(shp)))
        off += n
    return out


def _shard2d(t):
    return t.reshape((-1, t.shape[-1]))


def weights_from_shards(dm, l, got, idx):
    w_in, wb, w_out, wf, wfo = got
    d, mix = dm.d, dm.mix
    return dict(
        w_in=colgather(f"l{l}_w_in", w_in, idx, dm.np, BF16)[0],
        w_branch=wb.reshape(N_DEV, 3, mix, d // N_DEV).transpose(1, 2, 0, 3).reshape(3, mix, d),
        w_out=w_out.reshape(d, d), w_ffn_in=wf, w_ffn_out=wfo.reshape(N_DEV // 2, -1, d))


def small_shards(dm, g):
    d, mix = dm.d, dm.mix
    return [g["w_branch"].reshape(3, mix, N_DEV, d // N_DEV).transpose(2, 0, 1, 3).reshape(N_DEV, 3 * mix, d // N_DEV),
            g["w_out"].reshape(N_DEV, d // N_DEV, d), g["w_ffn_in"], g["w_ffn_out"].reshape(N_DEV, -1, d)]


def w_in_shards(dm, l, g, idx):
    return colgather(f"l{l}_g_w_in", g["w_in"][None], idx, dm.in_width // N_DEV, F32)


def layer_params(dm, l, full, small, mod_l, lb_l):
    d, mix = dm.d, dm.mix
    row = lambda t: t.reshape(1, -1)
    head = lambda t: t.reshape(-1, 1, 1)
    sh1, sc1, g1, sh2, sc2, g2 = (row(mod_l[i * d:(i + 1) * d]) for i in range(6))
    conv_b = jnp.concatenate([jnp.zeros((3 * mix,), F32), small["ssm_conv_b"][l]])
    return dict(
        w_in=full["w_in"], w_branch=full["w_branch"], w_out=full["w_out"],
        w_ffn_in=full["w_ffn_in"], w_ffn_out=full["w_ffn_out"],
        norm_mix=row(small["norm_mix"][l]), norm_ffn=row(small["norm_ffn"][l]), b_merge=row(small["b_merge"][l]),
        hgrn_norm=row(small["hgrn_norm"][l]), lb=row(lb_l),
        conv_w=jnp.concatenate([small["gdn_conv_w"][l], small["ssm_conv_w"][l]], axis=1), conv_b=row(conv_b),
        ssm_dt_bias=head(small["ssm_dt_bias"][l]), ssm_a_log=head(small["ssm_a_log"][l]),
        ssm_d_exp=row(jnp.repeat(small["ssm_d"][l], SSM_P)), ssm_norm=row(small["ssm_norm"][l]),
        gdn_dt_bias=head(small["gdn_dt_bias"][l]), gdn_a_log=head(small["gdn_a_log"][l]), gdn_norm=row(small["gdn_norm"][l]),
        sh1=sh1, sc1=sc1, g1=g1, sh2=sh2, sc2=sc2, g2=g2)


def layer_grads(dm, g):
    cs = 3 * dm.mix
    out = dict(
        w_in=g["w_in"], w_branch=g["w_branch"], w_out=g["w_out"], w_ffn_in=g["w_ffn_in"],
        w_ffn_out=g["w_ffn_out"], norm_mix=g["norm_mix"][0], norm_ffn=g["norm_ffn"][0], b_merge=g["b_merge"][0],
        hgrn_norm=g["hgrn_norm"][0], ssm_conv_w=g["conv_w"][:, cs:], gdn_conv_w=g["conv_w"][:, :cs],
        ssm_conv_b=g["conv_b"][0, cs:], ssm_dt_bias=g["ssm_dt_bias"][:, 0, 0], ssm_a_log=g["ssm_a_log"][:, 0, 0],
        ssm_d=g["ssm_d_exp"].reshape(dm.ssm_heads, SSM_P).sum(axis=1), ssm_norm=g["ssm_norm"][0],
        gdn_dt_bias=g["gdn_dt_bias"][:, 0, 0], gdn_a_log=g["gdn_a_log"][:, 0, 0], gdn_norm=g["gdn_norm"][0])
    dmod = jnp.concatenate([g[k][0] for k in ("sh1", "sc1", "g1", "sh2", "sc2", "g2")])
    return out, dmod, g["lb"][0]


def local_step(dm, nl, x, tgt, norm_final, params_of, gather_of=None, scatter_of=None):
    arrived = exchange("gather_w0", gather_of(0), True) if gather_of else None
    lps, saved = [], []
    for l in range(nl):
        lps.append(params_of(l, arrived))
        side = (gather_of(l + 1), True) if gather_of and l + 1 < nl else None
        x, sv, arrived = layer_fwd(dm, l, x, lps[l], side)
        saved.append(sv)
    loss, dx, dnf = loss_call("loss", x, tgt, norm_final, dm.tm)
    grads, parts, side = [None] * nl, [None] * nl, None
    for l in reversed(range(nl)):
        if scatter_of and l == 0:
            dx, grads[l], got, own = layer_bwd(dm, l, dx, lps[l], saved[l], side, own=True)
            parts[0] = list(exchange("scatter_g0", [scatter_of(0, grads[0])], False)) + own
        else:
            dx, grads[l], got = layer_bwd(dm, l, dx, lps[l], saved[l], side)
        if side is not None:
            parts[l + 1] = got
        side = ([scatter_of(l, grads[l])] + small_shards(dm, grads[l]), False) if scatter_of and l > 0 else None
    return loss, dx, dnf, grads, parts


def kernel(x, c, w_ada, b_ada, norm_mix, norm_ffn, w_in, b_merge, hgrn_lb_logits, hgrn_norm, ssm_conv_w, ssm_conv_b, ssm_dt_bias, ssm_a_log, ssm_d, ssm_norm, gdn_conv_w, gdn_dt_bias, gdn_a_log, gdn_norm, w_branch, w_out, w_ffn_in, w_ffn_out, norm_final, loss_target, m_w_ada, m_b_ada, m_norm_mix, m_norm_ffn, m_w_in, m_b_merge, m_hgrn_lb_logits, m_hgrn_norm, m_ssm_conv_w, m_ssm_conv_b, m_ssm_dt_bias, m_ssm_a_log, m_ssm_d, m_ssm_norm, m_gdn_conv_w, m_gdn_dt_bias, m_gdn_a_log, m_gdn_norm, m_w_branch, m_w_out, m_w_ffn_in, m_w_ffn_out, m_norm_final, v_w_ada, v_b_ada, v_norm_mix, v_norm_ffn, v_w_in, v_b_merge, v_hgrn_lb_logits, v_hgrn_norm, v_ssm_conv_w, v_ssm_conv_b, v_ssm_dt_bias, v_ssm_a_log, v_ssm_d, v_ssm_norm, v_gdn_conv_w, v_gdn_dt_bias, v_gdn_a_log, v_gdn_norm, v_w_branch, v_w_out, v_w_ffn_in, v_w_ffn_out, v_norm_final):
    a = dict(locals())
    x, tgt = a["x"][0], a["loss_target"][0]
    s, d = x.shape
    nl = a["w_ada"].shape[0]
    dm = Dims(s, d, a["w_ffn_out"].shape[1] * N_DEV)
    me = 4 * lax.axis_index("x") + 2 * lax.axis_index("y") + lax.axis_index("c")

    first = [a["c"], a["ssm_conv_w"], a["gdn_conv_w"]]
    c_all, scw, gcw = _unpack(exchange("gather_c", [_pack(first, F32, 8)], True)[0], [t.shape for t in first], lead=1)
    small = dict(a, ssm_conv_w=scw.transpose(1, 2, 0, 3).reshape(scw.shape[1:3] + (-1,)),
                 gdn_conv_w=gcw.transpose(1, 2, 0, 3).reshape(gcw.shape[1:3] + (-1,)))
    c_pad = jnp.zeros((LANES, d), F32).at[:N_DEV].set(c_all.reshape(N_DEV, d))
    ncol = a["w_ada"].shape[2]
    b_mine = lax.dynamic_slice(a["b_ada"], (0, me * ncol), (nl, ncol))[:, None, :]
    mod_part = ada_fwd("ada_fwd", c_pad, a["w_ada"], b_mine)[:, :N_DEV, :]
    (mod,) = exchange("a2a_mod", [mod_part.transpose(1, 0, 2)], False)
    mod = mod.transpose(1, 0, 2).reshape(nl, N_DEV * ncol)
    (lb,) = rowstage_fwd("lower_bounds", lower_bounds_fn, [(a["hgrn_lb_logits"], dm.mix, 0)], [], [(dm.mix, F32)], nl)

    idx_fwd, idx_bwd = w_in_tables(dm, N_DEV)
    loss, dx, dnf, grads, parts = local_step(
        dm, nl, x, tgt, a["norm_final"].reshape(1, d),
        params_of=lambda l, got: layer_params(dm, l, weights_from_shards(dm, l, got, idx_fwd), small, mod[l], lb[l]),
        gather_of=lambda l: [_shard2d(a[n][l]).astype(BF16) for n in GATHERED],
        scatter_of=lambda l, g: w_in_shards(dm, l, g, idx_bwd))

    per_layer = [layer_grads(dm, g) for g in grads]
    res = {}
    for i, n in enumerate(GATHERED):
        wmv = [a[q + n].reshape((nl, -1, a[n].shape[-1])) for q in ("", "m_", "v_")]
        outs = None
        for l in range(nl):
            outs = adamw_sum(f"adamw_l{l}_{n}", parts[l][i], *wmv, layer=l, into=outs)
        for kind, o in zip(("grad", "delta", "new_m", "new_v"), outs):
            res[(kind, n)] = o.reshape(a[n].shape)

    stackg = lambda n: jnp.stack([pl_[0][n] for pl_ in per_layer])
    dmod = jnp.stack([pl_[1] for pl_ in per_layer])
    dlb = jnp.stack([pl_[2] for pl_ in per_layer])
    pk_g = [dmod if n == "b_ada" else dnf if n == "norm_final" else stackg(n) for n in PACKET]
    extra = [dlb, stackg("ssm_conv_w"), stackg("gdn_conv_w"), loss[0, :1]]
    pk_shapes = [t.shape for t in pk_g + extra]
    zeros = [jnp.zeros(t.shape, F32) for t in extra]
    (parts,) = exchange("gather_small", [_pack(pk_g + extra, F32, 8)], True)
    outs = adamw_sum("adamw_small", parts, *[_pack([a[p + n] for n in PACKET] + zeros, F32, 8) for p in ("", "m_", "v_")])
    for kind, o in zip(("grad", "delta", "new_m", "new_v"), outs):
        un = _unpack(o, pk_shapes)
        for n, t in zip(PACKET, un):
            res[(kind, n)] = t.reshape(a[n].shape)
        if kind == "grad":
            dlb_sum, g_scw, g_gcw, loss_sum = un[len(PACKET):]

    (g_lb,), _ = rowstage_bwd("lower_bounds_b", lower_bounds_fn, [(a["hgrn_lb_logits"], dm.mix, 0)], [], [dlb_sum], [F32], nl)
    mine = lambda t, n: lax.dynamic_slice_in_dim(t, me * a[n].shape[-1], a[n].shape[-1], axis=t.ndim - 1)
    (dmod_cols,) = exchange("a2a_dmod", [dmod.reshape(nl, N_DEV, ncol).transpose(1, 0, 2)], False)
    dmod_pad = jnp.zeros((nl, LANES, ncol), F32).at[:, :N_DEV].set(dmod_cols.transpose(1, 0, 2))
    g_w_ada = ada_bwd("ada_bwd", c_pad, dmod_pad)
    outs = adamw_sum("adamw_w_ada", g_w_ada.reshape(1, nl * d, ncol), *[a[q + "w_ada"].reshape(nl * d, ncol) for q in ("", "m_", "v_")])
    for kind, o in zip(("grad", "delta", "new_m", "new_v"), outs):
        res[(kind, "w_ada")] = o.reshape(nl, d, ncol)
    g_misc = [g_lb, mine(g_scw, "ssm_conv_w"), mine(g_gcw, "gdn_conv_w")]
    outs = adamw_sum("adamw_misc", _pack(g_misc, F32, 8)[None], *[_pack([a[q + n] for n in MISC], F32, 8) for q in ("", "m_", "v_")])
    for kind, o in zip(("grad", "delta", "new_m", "new_v"), outs):
        for n, t in zip(MISC, _unpack(o, [a[n].shape for n in MISC])):
            res[(kind, n)] = t

    out = [loss_sum.reshape(()), dx[None]]
    for kind in ("grad", "delta", "new_m", "new_v"):
        out += [res[(kind, n)] for n in WEIGHTS]
    return tuple(out)
```

```python
import functools
import math

import numpy as np
import jax
import jax.numpy as jnp
from jax import lax
from jax.experimental import pallas as pl
from jax.experimental.pallas import tpu as pltpu

F32 = jnp.float32
BF16 = jnp.bfloat16

N_DEV = 8
CHUNK = 64
SUB = 8
HGRN_HEADS_PER_STEP = 6
GDN_HEADS_PER_STEP = 6
HEAD = 128
SSM_P = 64
CONV_K = 4
F_MIN = 1e-30
NORM_EPS = 1e-6
LANES = 128
GATHER_TILE = 256
VMEM_LIMIT = 56 * 1024 * 1024

ADAM_LR = 0.001
ADAM_B1 = 0.9
ADAM_B2 = 0.999
ADAM_EPS = 1e-08
ADAM_WD = 0.01
ADAM_STEP = 10


def _dg(a, b, ca, cb):
    return lax.dot_general(a.astype(BF16), b.astype(BF16), (((ca,), (cb,)), ((), ())),
                           preferred_element_type=F32)


def _split3(x):
    x1 = x.astype(BF16)
    r = x - x1.astype(F32)
    x2 = r.astype(BF16)
    x3 = (r - x2.astype(F32)).astype(BF16)
    return x1, x2, x3


def _hdg(a, b, ca, cb):
    a1, a2, _ = _split3(a)
    b1, b2, _ = _split3(b)
    dn = (((ca,), (cb,)), ((), ()))
    d = lambda p, q: lax.dot_general(p, q, dn, preferred_element_type=F32)
    return (d(a2, b1) + d(a1, b2)) + d(a1, b1)


def _dot_family(prim):
    @jax.custom_vjp
    def nn(a, b):
        return prim(a, b, 1, 0)

    @jax.custom_vjp
    def nt(a, b):
        return prim(a, b, 1, 1)

    @jax.custom_vjp
    def tn(a, b):
        return prim(a, b, 0, 0)

    nn.defvjp(lambda a, b: (nn(a, b), (a, b)), lambda r, g: (nt(g, r[1]), tn(r[0], g)))
    nt.defvjp(lambda a, b: (nt(a, b), (a, b)), lambda r, g: (nn(g, r[1]), tn(g, r[0])))
    tn.defvjp(lambda a, b: (tn(a, b), (a, b)), lambda r, g: (nt(r[1], g), nn(r[0], g)))
    return nn, nt, tn


mm_nn, mm_nt, mm_tn = _dot_family(_dg)
hd_nn, hd_nt, hd_tn = _dot_family(_hdg)


def _iota(shape, dim):
    return lax.broadcasted_iota(jnp.int32, shape, dim)


def _scan_rows(x, reverse):
    n = x.shape[0]
    rows = _iota(x.shape, 0)
    k = 1
    while k < n:
        if reverse:
            x = x + jnp.where(rows < n - k, pltpu.roll(x, n - k, 0), 0.0)
        else:
            x = x + jnp.where(rows >= k, pltpu.roll(x, k, 0), 0.0)
        k *= 2
    return x


@jax.custom_vjp
def cumsum_rows(x):
    return _scan_rows(x, False)


cumsum_rows.defvjp(lambda x: (_scan_rows(x, False), None), lambda _, g: (_scan_rows(g, True),))


def _sigmoid(x):
    return jax.nn.sigmoid(x)


def _silu(x):
    return x * jax.nn.sigmoid(x)


def _softplus(x):
    e = jnp.exp(-jnp.abs(x))
    small = e * (1.0 - e * (0.5 - e * (1.0 / 3.0)))
    return jnp.maximum(x, 0.0) + jnp.where(e < 1e-3, small, jnp.log(1.0 + e))


def _masked_exp(diff, mask):
    return jnp.where(mask, jnp.exp(jnp.where(mask, diff, 0.0)), 0.0)


def _rms(x, w):
    return x * lax.rsqrt(jnp.mean(x * x, axis=-1, keepdims=True) + NORM_EPS) * w


def _cum_col_row(lg_col, lg_row):
    c = lg_col.shape[0]
    r, s = _iota((c, c), 0), _iota((c, c), 1)
    cum_col = jnp.sum(jnp.where(s <= r, jnp.broadcast_to(lg_row, (c, c)), 0.0), axis=1, keepdims=True)
    cum_row = jnp.sum(jnp.where(r <= s, jnp.broadcast_to(lg_col, (c, c)), 0.0), axis=0, keepdims=True)
    total = jnp.sum(lg_col, axis=0, keepdims=True)
    return cum_col, cum_row, total


def hgrn_chunk(seq, hp, sp, st):
    (blk,), (lb,), (nw,) = seq, hp, sp
    c = blk.shape[0]
    q_raw, f_raw, v, g_raw = (blk[:, i * HEAD:(i + 1) * HEAD] for i in range(4))
    q = _silu(q_raw)
    f = lb + (1.0 - lb) * _sigmoid(f_raw)
    logf = jnp.log(jnp.maximum(f, F_MIN))
    k = (1.0 - lb) * _sigmoid(-f_raw)
    b = cumsum_rows(logf)
    o_inter = mm_nt(q * jnp.exp(b), st)
    nsub = c // SUB
    wide = (SUB, SUB, HEAD)
    er = _iota((SUB * SUB, SUB), 0)
    e_t = (er // SUB == _iota((SUB * SUB, SUB), 1)).astype(F32)
    pr = _iota((SUB * SUB, 1), 0)
    pmask = (pr % SUB) <= (pr // SUB)
    er64 = _iota((SUB * SUB, c), 0)
    ec64 = _iota((SUB * SUB, c), 1)
    rows_c = _iota((c, 1), 0)
    row = lambda a, i: jnp.sum(jnp.where(rows_c == i, a, 0.0), axis=0, keepdims=True)
    def sub_chunk(qi, ki, bi, bref, first, place):
        qb = jnp.broadcast_to(qi[:, None, :], wide).reshape(SUB * SUB, HEAD)
        kb = jnp.broadcast_to(ki[None, :, :], wide).reshape(SUB * SUB, HEAD)
        bd = (bi[:, None, :] - bi[None, :, :]).reshape(SUB * SUB, HEAD)
        sc_col = jnp.sum(qb * kb * _masked_exp(bd, pmask), axis=1, keepdims=True)
        sc = mm_tn(e_t, sc_col * place)
        sc = sc + mm_nt(qi * jnp.exp(bi - bref), k * _masked_exp(bref - b, rows_c < first))
        return mm_nn(sc, v)

    firsts = [SUB * i for i in range(nsub)]
    pile = lambda parts: jnp.concatenate([p[None] for p in parts], axis=0)
    cut = lambda a: a.reshape(nsub, SUB, HEAD)
    brefs = pile([row(b, f) for f in firsts])
    starts = pile([jnp.full((1, 1), f, jnp.int32) for f in firsts])
    places = pile([(ec64 == (er64 % SUB) + f).astype(F32) for f in firsts])
    o_intra = jax.vmap(sub_chunk)(cut(q), cut(k), cut(b), brefs, starts, places)
    o = o_inter + o_intra.reshape(c, HEAD)
    bend = row(b, c - 1)
    st_new = st * jnp.exp(bend) + mm_tn(v, k * jnp.exp(bend - b))
    y = _rms(o, nw) * _silu(g_raw)
    return (y,), st_new


def ssd_chunk(seq, hp, sp, st):
    xs, bm, cm, dtc, dtr = seq
    dt_bias, a_log = hp
    c, width = xs.shape
    nheads = width // SSM_P
    head_of = _iota((1, width), 1) // SSM_P
    r, s = _iota((c, c), 0), _iota((c, c), 1)
    g = mm_nt(cm, bm)
    dt_l, cum_l, end_l, scores = 0.0, 0.0, 0.0, []
    for i in range(nheads):
        neg_a = -jnp.exp(a_log[i])
        dt_col = _softplus(dtc[i] + dt_bias[i])
        dt_row = _softplus(dtr[i] + dt_bias[i])
        cum_col, cum_row, total = _cum_col_row(neg_a * dt_col, neg_a * dt_row)
        mine = head_of == i
        dt_l = dt_l + jnp.where(mine, dt_col, 0.0)
        cum_l = cum_l + jnp.where(mine, cum_col, 0.0)
        end_l = end_l + jnp.where(mine, total, 0.0)
        scores.append(g * _masked_exp(cum_col - cum_row, s <= r))
    xdt = xs * dt_l
    stacked = mm_nn(jnp.concatenate(scores, axis=0), xdt)
    y_intra = 0.0
    for i in range(nheads):
        y_intra = y_intra + jnp.where(head_of == i, stacked[i * c:(i + 1) * c], 0.0)
    y_inter = mm_nn(cm, st) * jnp.exp(cum_l)
    st_new = st * jnp.exp(end_l) + mm_tn(bm, xdt * jnp.exp(end_l - cum_l))
    return (y_intra + y_inter,), st_new


def _neumann_inverse(a):
    n = a.shape[0]
    eye = (_iota((n, n), 0) == _iota((n, n), 1)).astype(F32)
    p = -a
    t = eye + p
    for _ in range(int(math.log2(n)) - 1):
        p = _hdg(p, p, 1, 0)
        t = t + _hdg(t, p, 1, 0)
    return t


@jax.custom_vjp
def inv_unit_lower(a):
    return _neumann_inverse(a)


def _inv_fwd(a):
    t = _neumann_inverse(a)
    return t, t


inv_unit_lower.defvjp(_inv_fwd, lambda t, g: (-hd_nt(hd_tn(t, g), t),))


def gdn_chunk(seq, hp, sp, st):
    q_raw, k_raw, v, z, gbc, gac, gar = seq
    dt_bias, a_log = hp
    (nw,) = sp
    c = v.shape[0]
    r, s = _iota((c, c), 0), _iota((c, c), 1)
    q = q_raw * lax.rsqrt(jnp.sum(q_raw * q_raw, axis=-1, keepdims=True) + NORM_EPS) * (HEAD ** -0.5)
    k = k_raw * lax.rsqrt(jnp.sum(k_raw * k_raw, axis=-1, keepdims=True) + NORM_EPS)
    beta = _sigmoid(gbc)
    neg_a = -jnp.exp(a_log)
    cum, cum_row, total = _cum_col_row(neg_a * _softplus(gac + dt_bias), neg_a * _softplus(gar + dt_bias))
    decay = _masked_exp(cum - cum_row, s <= r)
    kk = mm_nt(k, k)
    a_low = jnp.where(s < r, beta * kk * decay, 0.0)
    sol = hd_nn(inv_unit_lower(a_low), jnp.concatenate([v * beta, k * (beta * jnp.exp(cum))], axis=1))
    u_base, w_corr = sol[:, :HEAD], sol[:, HEAD:]
    qk = mm_nt(q, k) * decay
    u = u_base - mm_nn(w_corr, st)
    o = mm_nn(q * jnp.exp(cum), st) + mm_nn(qk, u)
    st_new = jnp.exp(total) * st + mm_tn(k * jnp.exp(total - cum), u)
    y = _rms(o, nw) * _silu(z)
    return (y,), st_new


def normmod_fn(rows, params):
    (x,), (nw, sc, sh) = rows, params
    return (_rms(x, nw) * (1.0 + sc) + sh,)


def ssmpost_fn(rows, params):
    (y, xs, z), (d_exp, nw) = rows, params
    y = (y + d_exp * xs) * _silu(z)
    gw = y.shape[1] // 2
    return (jnp.concatenate([_rms(y[:, :gw], nw[:, :gw]), _rms(y[:, gw:], nw[:, gw:])], axis=1),)


def merge_fn(rows, params):
    (yh, ys, yg, gl), (bm, wb) = rows, params
    d = wb.shape[2]
    gates = _sigmoid(gl + bm)
    out = 0.0
    for n, y in enumerate((yh, ys, yg)):
        out = out + gates[:, n * d:(n + 1) * d] * mm_nn(y, wb[n])
    return (out,)


def outproj_fn(rows, params):
    (m, x), (g1, w) = rows, params
    return (x + (1.0 + g1) * mm_nn(m, w),)


def resid_fn(rows, params):
    (x, o), (g2,) = rows, params
    return (x + (1.0 + g2) * o,)


def _params(sem, side_effects=False):
    return pltpu.CompilerParams(dimension_semantics=sem, vmem_limit_bytes=VMEM_LIMIT, has_side_effects=side_effects)


def _whole(a):
    nd = a.ndim
    return pl.BlockSpec(a.shape, lambda *_: (0,) * nd)


def _pick(n, cands):
    for c in cands:
        if n % c == 0:
            return c
    return n


def matmul(name, a, b, mode, out_dtype, side=None):
    if mode == "nn":
        (m, k), n = a.shape, b.shape[1]
    elif mode == "nt":
        (m, k), n = a.shape, b.shape[0]
    else:
        (k, m), n = a.shape, b.shape[1]
    tm = _pick(m, (1024, 512, 256, 128))
    tn = _pick(n, (1280, 1024, 1408, 768, 512, 384, 256, 128))
    tk = _pick(k, (1024, 1280, 1408, 768, 512, 256, 128))
    if mode == "tn":
        tm = _pick(m, (1024, 768, 512, 256, 128))
        tk = _pick(k, (1024, 512, 256, 128))
    nk = k // tk
    ca, cb = {"nn": (1, 0), "nt": (1, 1), "tn": (0, 0)}[mode]

    def core(a_ref, b_ref, o_ref, acc_ref):
        kk = pl.program_id(2)

        @pl.when(kk == 0)
        def _():
            acc_ref[...] = jnp.zeros_like(acc_ref)

        acc_ref[...] += _dg(a_ref[...], b_ref[...], ca, cb)

        @pl.when(kk == nk - 1)
        def _():
            o_ref[...] = acc_ref[...].astype(o_ref.dtype)

    a_spec = (pl.BlockSpec((tk, tm), lambda i, j, q: (q, i)) if mode == "tn"
              else pl.BlockSpec((tm, tk), lambda i, j, q: (i, q)))
    b_spec = (pl.BlockSpec((tn, tk), lambda i, j, q: (j, q)) if mode == "nt"
              else pl.BlockSpec((tk, tn), lambda i, j, q: (q, j)))
    grid = (m // tm, n // tn, nk)
    body, s_in, s_out, s_shape, s_scr, s_args = _with_side(core, 2, 1, side, grid)
    sem = ("arbitrary",) * 3 if side else ("parallel", "parallel", "arbitrary")
    res = pl.pallas_call(
        body, name=name, grid=grid,
        in_specs=[a_spec, b_spec] + s_in,
        out_specs=[pl.BlockSpec((tm, tn), lambda i, j, q: (i, j))] + s_out,
        out_shape=[jax.ShapeDtypeStruct((m, n), out_dtype)] + s_shape,
        scratch_shapes=[pltpu.VMEM((tm, tn), F32)] + s_scr,
        compiler_params=_params(sem, side is not None),
    )(a, b, *s_args)
    return (res[0], res[1:]) if side else res[0]


def bmatmul(name, a, b, mode, out_dtype, out_batched):
    ab, bb = a.ndim == 3, b.ndim == 3
    nb = a.shape[0] if ab else b.shape[0]
    a2, b2 = a.shape[-2:], b.shape[-2:]
    if mode == "nn":
        (m, k), n = a2, b2[1]
    elif mode == "nt":
        (m, k), n = a2, b2[0]
    else:
        (k, m), n = a2, b2[1]
    tm = _pick(m, (1024, 512, 256, 128))
    tn = _pick(n, (1024, 512, 256, 128))
    tk = _pick(k, (1024, 512, 256, 128))
    nk = k // tk
    ca, cb = {"nn": (1, 0), "nt": (1, 1), "tn": (0, 0)}[mode]
    ids = (lambda g: g) if out_batched else (lambda g: (g[2], g[0], g[1], g[3]))
    grid = (nb, m // tm, n // tn, nk) if out_batched else (m // tm, n // tn, nb, nk)

    def a_map(*g):
        bi, i, j, q = ids(g)
        idx = (q, i) if mode == "tn" else (i, q)
        return (bi,) + idx if ab else idx

    def b_map(*g):
        bi, i, j, q = ids(g)
        idx = (j, q) if mode == "nt" else (q, j)
        return (bi,) + idx if bb else idx

    def o_map(*g):
        bi, i, j, q = ids(g)
        return (bi, i, j) if out_batched else (i, j)

    def body(a_ref, b_ref, o_ref, acc_ref):
        bi, _, _, q = ids(tuple(pl.program_id(d) for d in range(4)))
        first = (q == 0) if out_batched else (q == 0) & (bi == 0)
        last = (q == nk - 1) if out_batched else (q == nk - 1) & (bi == nb - 1)

        @pl.when(first)
        def _():
            acc_ref[...] = jnp.zeros_like(acc_ref)

        acc_ref[...] += _dg(a_ref[...], b_ref[...], ca, cb)

        @pl.when(last)
        def _():
            o_ref[...] = acc_ref[...].astype(o_ref.dtype)

    a_blk = (tk, tm) if mode == "tn" else (tm, tk)
    b_blk = (tn, tk) if mode == "nt" else (tk, tn)
    return pl.pallas_call(
        body, name=name, grid=grid,
        in_specs=[pl.BlockSpec(((None,) if ab else ()) + a_blk, a_map), pl.BlockSpec(((None,) if bb else ()) + b_blk, b_map)],
        out_specs=pl.BlockSpec(((None,) if out_batched else ()) + (tm, tn), o_map),
        out_shape=jax.ShapeDtypeStruct(((nb,) if out_batched else ()) + (m, n), out_dtype),
        scratch_shapes=[pltpu.VMEM((tm, tn), F32)],
        compiler_params=_params(("parallel", "parallel", "arbitrary", "arbitrary")),
    )(a, b)


def colgather(name, src, idx, dst_w, out_dtype):
    nsrc, rows, w = src.shape
    tw = GATHER_TILE
    nbs = -(-w // tw)
    ne = idx.shape[0]
    nbd = idx.shape[1] // tw
    tiles = [sorted(set((idx[e, t * tw:(t + 1) * tw][idx[e, t * tw:(t + 1) * tw] >= 0] // tw).tolist()))
             for e in range(ne) for t in range(nbd)]
    nslot = max(1, max(len(t) for t in tiles))
    tbl = np.full((ne * nbd, nslot), -1, np.int32)
    for i, t in enumerate(tiles):
        tbl[i, :len(t)] = t
    exact3 = src.dtype == F32

    def body(tbl_ref, idx_ref, src_ref, o_ref, acc_ref):
        ti, si = pl.program_id(0), pl.program_id(1)

        @pl.when(si == 0)
        def _():
            acc_ref[...] = jnp.zeros_like(acc_ref)

        t = tbl_ref[ti * nslot + si]

        @pl.when(t >= 0)
        def _():
            onehot = ((_iota((tw, tw), 0) + t * tw) == idx_ref[...]).astype(BF16)
            col = _iota((1, tw), 1) + (t % nbs) * tw
            xv = jnp.where(col < w, src_ref[...], jnp.zeros((), src_ref.dtype))
            d = lambda p: lax.dot_general(p, onehot, (((1,), (0,)), ((), ())), preferred_element_type=F32)
            if exact3:
                x1, x2, x3 = _split3(xv)
                acc_ref[...] += (d(x3) + d(x2)) + d(x1)
            else:
                acc_ref[...] += d(xv)

        @pl.when(si == nslot - 1)
        def _():
            o_ref[...] = acc_ref[...].astype(o_ref.dtype)

    def src_map(ti, si, tbl_ref):
        t = jnp.maximum(tbl_ref[ti * nslot + si], 0)
        return (t // nbs, 0, t % nbs)

    grid_spec = pltpu.PrefetchScalarGridSpec(
        num_scalar_prefetch=1, grid=(ne * nbd, nslot),
        in_specs=[pl.BlockSpec((None, 1, tw), lambda ti, si, tbl_ref: (ti // nbd, 0, ti % nbd)),
                  pl.BlockSpec((None, rows, tw), src_map)],
        out_specs=pl.BlockSpec((None, rows, tw), lambda ti, si, tbl_ref: (ti // nbd, 0, ti % nbd)),
        scratch_shapes=[pltpu.VMEM((rows, tw), F32)])
    return pl.pallas_call(
        body, name=name, grid_spec=grid_spec,
        out_shape=jax.ShapeDtypeStruct((ne, rows, dst_w), out_dtype),
        compiler_params=_params(("parallel", "arbitrary")),
    )(jnp.asarray(tbl.reshape(-1)), jnp.asarray(idx.reshape(ne, 1, nbd * tw).astype(np.int32)), src)


def swiglu3_fwd(name, gu, tm):
    _, nb, s, w = gu.shape

    def body(x_ref, o_ref):
        o_ref[...] = (_silu(x_ref[0].astype(F32)) * x_ref[1].astype(F32)).astype(o_ref.dtype)

    return pl.pallas_call(
        body, name=name, grid=(nb, s // tm),
        in_specs=[pl.BlockSpec((2, None, tm, w), lambda b, i: (0, b, i, 0))],
        out_specs=pl.BlockSpec((None, tm, w), lambda b, i: (b, i, 0)),
        out_shape=jax.ShapeDtypeStruct((nb, s, w), BF16),
        compiler_params=_params(("parallel", "parallel")),
    )(gu)


def swiglu3_bwd(name, gu, dact, tm):
    _, nb, s, w = gu.shape

    def body(x_ref, g_ref, o_ref):
        _, vjp = jax.vjp(lambda a, b: _silu(a) * b, x_ref[0].astype(F32), x_ref[1].astype(F32))
        dg, du = vjp(g_ref[...].astype(F32))
        o_ref[0] = dg.astype(o_ref.dtype)
        o_ref[1] = du.astype(o_ref.dtype)

    return pl.pallas_call(
        body, name=name, grid=(nb, s // tm),
        in_specs=[pl.BlockSpec((2, None, tm, w), lambda b, i: (0, b, i, 0)),
                  pl.BlockSpec((None, tm, w), lambda b, i: (b, i, 0))],
        out_specs=pl.BlockSpec((2, None, tm, w), lambda b, i: (0, b, i, 0)),
        out_shape=jax.ShapeDtypeStruct(gu.shape, BF16),
        compiler_params=_params(("parallel", "parallel")),
    )(gu, dact)


def _row_specs(rows, tm):
    return [pl.BlockSpec((tm, w), lambda i, _c=c: (i, _c)) for (_, w, c) in rows]


def rowstage_fwd(name, fn, rows, params, outs, tm):
    s = rows[0][0].shape[0]
    nr, npar = len(rows), len(params)

    def body(*refs):
        r = [x[...].astype(F32) for x in refs[:nr]]
        p = [x[...].astype(F32) for x in refs[nr:nr + npar]]
        for ref, val in zip(refs[nr + npar:], fn(r, p)):
            ref[...] = val.astype(ref.dtype)

    res = pl.pallas_call(
        body, name=name, grid=(s // tm,),
        in_specs=_row_specs(rows, tm) + [_whole(p) for p in params],
        out_specs=[pl.BlockSpec((tm, w), lambda i: (i, 0)) for (w, _) in outs],
        out_shape=[jax.ShapeDtypeStruct((s, w), dt) for (w, dt) in outs],
        compiler_params=_params(("parallel",)),
    )(*[r[0] for r in rows], *params)
    return res


def rowstage_bwd(name, fn, rows, params, douts, drow_dtypes, tm, adds=None, into=None):
    s = rows[0][0].shape[0]
    nr, npar, no = len(rows), len(params), len(douts)
    adds = adds or {}
    add_idx = sorted(adds)
    na = len(add_idx)
    into = into or {}
    into_idx = sorted(into)
    nb = len(into_idx)

    def body(*refs):
        r = [x[...].astype(F32) for x in refs[:nr]]
        p = [x[...].astype(F32) for x in refs[nr:nr + npar]]
        g = [x[...].astype(F32) for x in refs[nr + npar:nr + npar + no]]
        a_refs = refs[nr + npar + no:nr + npar + no + na]
        dr_refs = refs[nr + npar + no + na + nb:nr + npar + no + na + nb + nr]
        dp_refs = refs[nr + npar + no + na + nb + nr:]
        _, vjp = jax.vjp(lambda r_, p_: tuple(fn(r_, p_)), r, p)
        dr, dp = vjp(tuple(g))
        for j, (ref, val) in enumerate(zip(dr_refs, dr)):
            if j in adds:
                val = val + a_refs[add_idx.index(j)][...].astype(F32)
            ref[...] = val.astype(ref.dtype)

        @pl.when(pl.program_id(0) == 0)
        def _():
            for ref in dp_refs:
                ref[...] = jnp.zeros_like(ref)

        for ref, val in zip(dp_refs, dp):
            ref[...] += val

    res = pl.pallas_call(
        body, name=name, grid=(s // tm,),
        in_specs=(_row_specs(rows, tm) + [_whole(p) for p in params]
                  + [pl.BlockSpec((tm, d.shape[1]), lambda i: (i, 0)) for d in douts]
                  + [pl.BlockSpec((tm, rows[j][1]), lambda i: (i, 0)) for j in add_idx]
                  + [pl.BlockSpec(memory_space=pl.ANY)] * nb),
        out_specs=([pl.BlockSpec((tm, w), lambda i, _c=(into[j][1] if j in into else 0): (i, _c))
                    for j, (_, w, _) in enumerate(rows)] + [_whole(p) for p in params]),
        out_shape=([jax.ShapeDtypeStruct(into[j][0].shape if j in into else (s, w), dt)
                    for j, ((_, w, _), dt) in enumerate(zip(rows, drow_dtypes))]
                   + [jax.ShapeDtypeStruct(p.shape, F32) for p in params]),
        input_output_aliases={nr + npar + no + na + k: j for k, j in enumerate(into_idx)},
        compiler_params=_params(("arbitrary",)),
    )(*[r[0] for r in rows], *params, *douts, *[adds[j] for j in add_idx], *[into[j][0] for j in into_idx])
    return res[:nr], res[nr:]


def _flip(index_map, nc):
    return lambda h, n: index_map(h, nc - 1 - n)


def _with_side(core, n_in, n_out, side, grid):
    if side is None:
        return core, [], [], [], [], ()
    sends, broadcast = side
    k = len(sends)

    def body(*refs):
        ins, snd = refs[:n_in], refs[n_in:n_in + k]
        outs, rcv = refs[n_in + k:n_in + k + n_out], refs[n_in + k + n_out:n_in + 2 * k + n_out]
        scr = refs[n_in + 2 * k + n_out:]
        start, wait = _exchange_ops(snd, rcv, *scr[1:], broadcast)
        ids = [pl.program_id(d) for d in range(len(grid))]
        first = functools.reduce(lambda a, b: a & b, [i == 0 for i in ids])
        last = functools.reduce(lambda a, b: a & b, [i == g - 1 for i, g in zip(ids, grid)])
        pl.when(first)(start)
        core(*ins, *outs, scr[0])
        pl.when(last)(wait)

    return body, [HBM_SPEC] * k, [HBM_SPEC] * k, _exchange_out(sends, broadcast), _exchange_sems(k), tuple(sends)


def _take(v, split, j):
    if split is None:
        return v
    if split[0] == "lane":
        return v[:, j * split[1]:(j + 1) * split[1]]
    if split[0] == "lead":
        return v[j * split[1]:(j + 1) * split[1]]
    return v[j]


def _heads(vals, specs, hb):
    return [v if s[-1] is None else jnp.stack([_take(v, s[-1], j) for j in range(hb)]) for v, s in zip(vals, specs)]


def _over_heads(chunk_fn, hb, seqs, hparams, batched):
    seq_ax = [None if s[3] is None else 0 for s in seqs]
    hp_ax = [None if s[3] is None else 0 for s in hparams]
    if batched:
        return jax.vmap(chunk_fn, in_axes=(seq_ax, hp_ax, None, 0))

    def looped(seq, hp, sp, st):
        pick = lambda vals, axes, j: [v if a is None else v[j] for v, a in zip(vals, axes)]
        res = [chunk_fn(pick(seq, seq_ax, j), pick(hp, hp_ax, j), sp, st[j]) for j in range(hb)]
        pile = lambda parts: jnp.concatenate([p[None] for p in parts], axis=0)
        return tuple(pile(o) for o in zip(*[r[0] for r in res])), pile([r[1] for r in res])

    return looped


def _where(split, j):
    if split[0] == "lane":
        return (slice(None), slice(j * split[1], (j + 1) * split[1]))
    if split[0] == "lead":
        return (slice(j * split[1], (j + 1) * split[1]),)
    return (j,)


def scan_fwd(name, chunk_fn, nblk, hb, nc, seqs, hparams, sparams, state_shape, outs, batched, side=None):
    ns, nhp, nsp, no = len(seqs), len(hparams), len(sparams), len(outs)

    def core(*refs):
        seq_r, hp_r, sp_r = refs[:ns], refs[ns:ns + nhp], refs[ns + nhp:ns + nhp + nsp]
        out_r = refs[ns + nhp + nsp:ns + nhp + nsp + no]
        st_out, st_scr = refs[-2], refs[-1]

        @pl.when(pl.program_id(1) == 0)
        def _():
            st_scr[...] = jnp.zeros_like(st_scr)

        seq_v = [x[...].astype(F32) for x in seq_r]
        hp_v = [x[...] for x in hp_r]
        sp_v = [x[...] for x in sp_r]
        st = st_scr[...]
        st_out[...] = st
        heads = _over_heads(chunk_fn, hb, seqs, hparams, batched)
        o, st_new = heads(_heads(seq_v, seqs, hb), _heads(hp_v, hparams, hb), sp_v, st)
        for ref, spec, val in zip(out_r, outs, o):
            for j in range(hb):
                ref[_where(spec[4], j)] = val[j].astype(ref.dtype)
        st_scr[...] = st_new

    nst = len(state_shape)
    body, s_in, s_out, s_shape, s_scr, s_args = _with_side(core, ns + nhp + nsp, no + 1, side, (nblk, nc))
    res = pl.pallas_call(
        body, name=name, grid=(nblk, nc),
        in_specs=([pl.BlockSpec(bs, im) for (_, bs, im, _) in seqs]
                  + [pl.BlockSpec(bs, lambda h, n, _im=im: _im(h)) for (_, bs, im, _) in hparams]
                  + [_whole(p) for p in sparams] + s_in),
        out_specs=([pl.BlockSpec(bs, im) for (_, _, bs, im, _) in outs]
                   + [pl.BlockSpec((hb, None) + tuple(state_shape), lambda h, n: (h, n) + (0,) * nst)] + s_out),
        out_shape=([jax.ShapeDtypeStruct(fs, dt) for (fs, dt, _, _, _) in outs]
                   + [jax.ShapeDtypeStruct((nblk * hb, nc) + tuple(state_shape), F32)] + s_shape),
        scratch_shapes=[pltpu.VMEM((hb,) + tuple(state_shape), F32)] + s_scr,
        compiler_params=_params(("arbitrary", "arbitrary"), side is not None),
    )(*[x[0] for x in seqs], *[x[0] for x in hparams], *sparams, *s_args)
    return res[:no], res[no], res[no + 1:]


def scan_bwd(name, chunk_fn, nblk, hb, nc, seqs, hparams, sparams, state_shape, states, douts, dseqs, batched, side=None):
    ns, nhp, nsp, no = len(seqs), len(hparams), len(sparams), len(douts)
    nst = len(state_shape)
    buf_of = [i for i, sp in enumerate(dseqs) if len(sp) > 5 and sp[5] is not None]
    bufs = [dseqs[i][5] for i in buf_of]

    def core(*refs):
        seq_r, hp_r, sp_r = refs[:ns], refs[ns:ns + nhp], refs[ns + nhp:ns + nhp + nsp]
        base = ns + nhp + nsp
        st_r = refs[base]
        do_r = refs[base + 1:base + 1 + no]
        base += 1 + no + len(bufs)
        ds_r, dhp_r, dsp_r = refs[base:base + ns], refs[base + ns:base + ns + nhp], refs[base + ns + nhp:base + ns + nhp + nsp]
        dst_scr = refs[-1]
        h, n = pl.program_id(0), pl.program_id(1)

        @pl.when(n == 0)
        def _():
            dst_scr[...] = jnp.zeros_like(dst_scr)
            for ref in dhp_r:
                ref[...] = jnp.zeros_like(ref)

        @pl.when((n == 0) & (h == 0))
        def _():
            for ref in dsp_r:
                ref[...] = jnp.zeros_like(ref)

        seq_v = [x[...].astype(F32) for x in seq_r]
        hp_v = [x[...] for x in hp_r]
        sp_v = [x[...] for x in sp_r]
        do_v = [x[...].astype(F32) for x in do_r]
        prim = (_heads(seq_v, seqs, hb), _heads(hp_v, hparams, hb), sp_v, st_r[...])
        _, vjp = jax.vjp(_over_heads(chunk_fn, hb, seqs, hparams, batched), *prim)
        ds, dhp, dsp, dst = vjp((tuple(_heads(do_v, douts, hb)), dst_scr[...]))
        for ref, spec, val in zip(ds_r, dseqs, ds):
            if spec[4] is None:
                ref[...] = val.astype(ref.dtype)
            else:
                for j in range(hb):
                    ref[_where(spec[4], j)] = val[j].astype(ref.dtype)
        for ref, spec, val in zip(dhp_r, hparams, dhp):
            for j in range(hb):
                ref[_where(spec[3], j)] += val[j]
        for ref, val in zip(dsp_r, dsp):
            ref[...] += val
        dst_scr[...] = dst

    n_in, n_out = ns + nhp + nsp + 1 + no + len(bufs), ns + nhp + nsp
    body, s_in, s_out, s_shape, s_scr, s_args = _with_side(core, n_in, n_out, side, (nblk, nc))
    res = pl.pallas_call(
        body, name=name, grid=(nblk, nc),
        in_specs=([pl.BlockSpec(bs, _flip(im, nc)) for (_, bs, im, _) in seqs]
                  + [pl.BlockSpec(bs, lambda h, n, _im=im: _im(h)) for (_, bs, im, _) in hparams]
                  + [_whole(p) for p in sparams]
                  + [pl.BlockSpec((hb, None) + tuple(state_shape), lambda h, n: (h, nc - 1 - n) + (0,) * nst)]
                  + [pl.BlockSpec(bs, _flip(im, nc)) for (_, bs, im, _) in douts]
                  + [pl.BlockSpec(memory_space=pl.ANY)] * len(bufs) + s_in),
        out_specs=([pl.BlockSpec(sp[2], _flip(sp[3], nc)) for sp in dseqs]
                   + [pl.BlockSpec(bs, lambda h, n, _im=im: _im(h)) for (_, bs, im, _) in hparams]
                   + [_whole(p) for p in sparams] + s_out),
        out_shape=([jax.ShapeDtypeStruct(sp[0], sp[1]) for sp in dseqs]
                   + [jax.ShapeDtypeStruct(x[0].shape, F32) for x in hparams]
                   + [jax.ShapeDtypeStruct(p.shape, F32) for p in sparams] + s_shape),
        scratch_shapes=[pltpu.VMEM((hb,) + tuple(state_shape), F32)] + s_scr,
        input_output_aliases={n_in - len(bufs) + k: i for k, i in enumerate(buf_of)},
        compiler_params=_params(("arbitrary", "arbitrary"), side is not None),
    )(*[x[0] for x in seqs], *[x[0] for x in hparams], *sparams, states, *[x[0] for x in douts], *bufs, *s_args)
    return res[:ns], res[ns:ns + nhp], res[ns + nhp:n_out], res[n_out:]


def _shift_down(x, n, rows):
    if n == 0:
        return x
    return jnp.where(rows >= n, pltpu.roll(x, n, 0), 0.0)


def _shift_up(x, n, rows):
    if n == 0:
        return x
    s = x.shape[0]
    return jnp.where(rows < s - n, pltpu.roll(x, s - n, 0), 0.0)


def conv_fwd(name, x, col0, w, b):
    s, cw = x.shape[0], w.shape[1]

    def body(x_ref, w_ref, b_ref, o_ref):
        xv = x_ref[...]
        rows = _iota(xv.shape, 0)
        u = jnp.broadcast_to(b_ref[...], xv.shape)
        for j in range(CONV_K):
            u = u + w_ref[j:j + 1, :] * _shift_down(xv, CONV_K - 1 - j, rows)
        o_ref[...] = _silu(u)

    return pl.pallas_call(
        body, name=name, grid=(cw // LANES,),
        in_specs=[pl.BlockSpec((s, LANES), lambda j: (0, col0 + j)),
                  pl.BlockSpec((CONV_K, LANES), lambda j: (0, j)),
                  pl.BlockSpec((1, LANES), lambda j: (0, j))],
        out_specs=pl.BlockSpec((s, LANES), lambda j: (0, j)),
        out_shape=jax.ShapeDtypeStruct((s, cw), F32),
        compiler_params=_params(("parallel",)),
    )(x, w, b)


def conv_bwd(name, x, col0, w, b, dout, into):
    s, cw = x.shape[0], w.shape[1]

    def body(x_ref, w_ref, b_ref, g_ref, into_ref, dx_ref, dw_ref, db_ref):
        xv = x_ref[...]
        rows = _iota(xv.shape, 0)
        sh = [_shift_down(xv, CONV_K - 1 - j, rows) for j in range(CONV_K)]
        u = jnp.broadcast_to(b_ref[...], xv.shape)
        for j in range(CONV_K):
            u = u + w_ref[j:j + 1, :] * sh[j]
        sg = _sigmoid(u)
        du = g_ref[...] * (sg * (1.0 + u * (1.0 - sg)))
        dx = jnp.zeros_like(xv)
        for j in range(CONV_K):
            dx = dx + w_ref[j:j + 1, :] * _shift_up(du, CONV_K - 1 - j, rows)
            dw_ref[j:j + 1, :] = jnp.sum(du * sh[j], axis=0, keepdims=True)
        dx_ref[...] = dx.astype(dx_ref.dtype)
        db_ref[...] = jnp.sum(du, axis=0, keepdims=True)

    return pl.pallas_call(
        body, name=name, grid=(cw // LANES,),
        in_specs=[pl.BlockSpec((s, LANES), lambda j: (0, col0 + j)),
                  pl.BlockSpec((CONV_K, LANES), lambda j: (0, j)),
                  pl.BlockSpec((1, LANES), lambda j: (0, j)),
                  pl.BlockSpec((s, LANES), lambda j: (0, j)),
                  pl.BlockSpec(memory_space=pl.ANY)],
        out_specs=[pl.BlockSpec((s, LANES), lambda j: (0, col0 + j)),
                   pl.BlockSpec((CONV_K, LANES), lambda j: (0, j)),
                   pl.BlockSpec((1, LANES), lambda j: (0, j))],
        out_shape=[jax.ShapeDtypeStruct(into.shape, into.dtype), jax.ShapeDtypeStruct((CONV_K, cw), F32),
                   jax.ShapeDtypeStruct((1, cw), F32)],
        input_output_aliases={4: 0},
        compiler_params=_params(("parallel",)),
    )(x, w, b, dout, into)


def exchange(name, sends, broadcast):
    nop = len(sends)

    def body(*refs):
        start, wait = _exchange_ops(refs[:nop], refs[nop:2 * nop], *refs[2 * nop:], broadcast)
        start()
        wait()

    return pl.pallas_call(
        body, name=name,
        in_specs=[HBM_SPEC] * nop, out_specs=[HBM_SPEC] * nop,
        out_shape=_exchange_out(sends, broadcast), scratch_shapes=_exchange_sems(nop),
        compiler_params=pltpu.CompilerParams(has_side_effects=True),
    )(*sends)


HBM_SPEC = pl.BlockSpec(memory_space=pltpu.HBM)


def _exchange_out(sends, broadcast):
    return [jax.ShapeDtypeStruct((N_DEV,) + tuple(t.shape if broadcast else t.shape[1:]), t.dtype) for t in sends]


def _exchange_sems(nop):
    return [pltpu.SemaphoreType.DMA((nop * (N_DEV - 1),)), pltpu.SemaphoreType.DMA((nop * (N_DEV - 1),)),
            pltpu.SemaphoreType.DMA((nop,))]


def _gather_ops(send_refs, recv_refs, send_sems, recv_sems, local_sems):
    nop = len(send_refs)
    x, y, c = lax.axis_index("x"), lax.axis_index("y"), lax.axis_index("c")
    me, other = (x, y, c), (x, y, 1 - c)
    chips = [(1 - x, y), (x, 1 - y), (1 - x, 1 - y)]
    slab = lambda dev: 4 * dev[0] + 2 * dev[1] + dev[2]

    def copy(i, k, block, to, passed_on=False):
        return pltpu.make_async_remote_copy(
            src_ref=recv_refs[i].at[slab(block)] if passed_on else send_refs[i], dst_ref=recv_refs[i].at[slab(block)],
            send_sem=send_sems.at[i * (N_DEV - 1) + k], recv_sem=recv_sems.at[i * (N_DEV - 1) + k],
            device_id=to, device_id_type=pl.DeviceIdType.MESH)

    def local(i):
        return pltpu.make_async_copy(send_refs[i], recv_refs[i].at[slab(me)], local_sems.at[i])

    def start():
        for i in range(nop):
            local(i).start()
            copy(i, 0, me, other).start()
        for j, chip in enumerate(chips):
            for i in range(nop):
                copy(i, 1 + j, me, chip + (c,)).start()

    def wait():
        for j, chip in enumerate(chips):
            for i in range(nop):
                copy(i, 1 + j, chip + (c,), me).wait_recv()
                copy(i, 4 + j, chip + (c,), other, passed_on=True).start()
        for i in range(nop):
            copy(i, 0, other, me).wait_recv()
        for j, chip in enumerate(chips):
            for i in range(nop):
                copy(i, 4 + j, chip + (1 - c,), me, passed_on=True).wait_recv()
        for i in range(nop):
            copy(i, 0, me, other).wait_send()
            for j, chip in enumerate(chips):
                copy(i, 1 + j, me, chip + (c,)).wait_send()
                copy(i, 4 + j, chip + (c,), other, passed_on=True).wait_send()
            local(i).wait()

    return start, wait


def _exchange_ops(send_refs, recv_refs, send_sems, recv_sems, local_sems, broadcast):
    if broadcast:
        return _gather_ops(send_refs, recv_refs, send_sems, recv_sems, local_sems)
    nop = len(send_refs)
    x, y, c = lax.axis_index("x"), lax.axis_index("y"), lax.axis_index("c")
    me = 4 * x + 2 * y + c
    peers = []
    for k in range(1, N_DEV):
        px = 1 - x if (k >> 2) & 1 else x
        py = 1 - y if (k >> 1) & 1 else y
        pc = 1 - c if k & 1 else c
        peers.append(((px, py, pc), 4 * px + 2 * py + pc))

    def src(i, peer):
        return send_refs[i].at[peer]

    def remote(i, k, arrival):
        dev, peer = peers[k]
        return pltpu.make_async_remote_copy(
            src_ref=src(i, peer), dst_ref=recv_refs[i].at[peer if arrival else me],
            send_sem=send_sems.at[i * (N_DEV - 1) + k], recv_sem=recv_sems.at[i * (N_DEV - 1) + k],
            device_id=dev, device_id_type=pl.DeviceIdType.MESH)

    def local(i):
        return pltpu.make_async_copy(src(i, me), recv_refs[i].at[me], local_sems.at[i])

    def start():
        for i in range(nop):
            local(i).start()
        for k in range(N_DEV - 1):
            for i in range(nop):
                remote(i, k, False).start()

    def wait():
        for k in range(N_DEV - 1):
            for i in range(nop):
                remote(i, k, True).wait_recv()
        for k in range(N_DEV - 1):
            for i in range(nop):
                remote(i, k, False).wait_send()
        for i in range(nop):
            local(i).wait()

    return start, wait


def adamw_sum(name, parts, w, m, v, layer=None, into=None):
    rws, cols = w.shape[-2:]
    nsum = parts.shape[0]
    tr = _pick(rws, (256, 128, 64, 32, 16, 8))
    c1 = 1.0 / (1.0 - ADAM_B1 ** ADAM_STEP)
    c2 = 1.0 / (1.0 - ADAM_B2 ** ADAM_STEP)

    def body(p_ref, w_ref, m_ref, v_ref, *rest):
        g_ref, d_ref, nm_ref, nv_ref = rest[-4:]
        g = p_ref[0]
        for j in range(1, nsum):
            g = g + p_ref[j]
        nm = ADAM_B1 * m_ref[...] + (1.0 - ADAM_B1) * g
        nv = ADAM_B2 * v_ref[...] + (1.0 - ADAM_B2) * (g * g)
        g_ref[...] = g
        nm_ref[...] = nm
        nv_ref[...] = nv
        d_ref[...] = -ADAM_LR * ((nm * c1) / (jnp.sqrt(nv * c2) + ADAM_EPS) + ADAM_WD * w_ref[...])

    if layer is None:
        blk = pl.BlockSpec((tr, cols), lambda i: (i, 0))
    else:
        blk = pl.BlockSpec((None, tr, cols), lambda i: (layer, i, 0))
    if into is None and layer is not None:
        into = [lax.empty(w.shape, F32) for _ in range(4)]
    extra = list(into) if into else []
    return pl.pallas_call(
        body, name=name, grid=(rws // tr,),
        in_specs=([pl.BlockSpec((nsum, tr, cols), lambda i: (0, i, 0)), blk, blk, blk]
                  + [pl.BlockSpec(memory_space=pl.ANY)] * len(extra)),
        out_specs=[blk, blk, blk, blk],
        out_shape=[jax.ShapeDtypeStruct(w.shape, F32)] * 4,
        input_output_aliases={4 + k: k for k in range(len(extra))},
        compiler_params=_params(("parallel",)),
    )(parts, w, m, v, *extra)


def ada_fwd(name, c_all, w, b):
    nl = w.shape[0]

    def body(c_ref, w_ref, b_ref, o_ref):
        ca = _silu(c_ref[...])
        for l in range(nl):
            o_ref[l] = mm_nn(ca, w_ref[l]) + b_ref[l]

    return pl.pallas_call(
        body, name=name,
        out_shape=jax.ShapeDtypeStruct((nl, c_all.shape[0], w.shape[2]), F32),
        compiler_params=pltpu.CompilerParams(vmem_limit_bytes=VMEM_LIMIT),
    )(c_all, w, b)


def ada_bwd(name, c_all, dmod):
    nl = dmod.shape[0]

    def body(c_ref, g_ref, o_ref):
        ca = _silu(c_ref[...])
        for l in range(nl):
            o_ref[l] = mm_tn(ca, g_ref[l])

    return pl.pallas_call(
        body, name=name,
        out_shape=jax.ShapeDtypeStruct((nl, c_all.shape[1], dmod.shape[2]), F32),
        compiler_params=pltpu.CompilerParams(vmem_limit_bytes=VMEM_LIMIT),
    )(c_all, dmod)


def lower_bounds_fn(rows, params):
    (lg,), _ = rows, params
    nl = lg.shape[0]
    mx = jnp.max(lg, axis=0, keepdims=True)
    e = jnp.exp(lg - mx)
    p = e / jnp.sum(e, axis=0, keepdims=True)
    layer = _iota((nl, 1), 0)
    acc = jnp.zeros_like(p)
    for j in range(1, nl):
        pj = jnp.sum(jnp.where(layer == j, p, 0.0), axis=0, keepdims=True)
        acc = acc + jnp.where(layer >= j, 1.0, 0.0) * pj
    return (acc,)


def loss_call(name, x, tgt, nw, tm):
    s, d = x.shape

    def body(x_ref, t_ref, w_ref, l_ref, dx_ref, dw_ref):
        def f(xv, wv):
            err = _rms(xv, wv) - t_ref[...]
            return jnp.sum(0.5 * jnp.mean(err * err, axis=-1, keepdims=True), axis=0, keepdims=True)

        val, vjp = jax.vjp(f, x_ref[...], w_ref[...])
        dx, dw = vjp(jnp.ones_like(val))

        @pl.when(pl.program_id(0) == 0)
        def _():
            l_ref[...] = jnp.zeros_like(l_ref)
            dw_ref[...] = jnp.zeros_like(dw_ref)

        l_ref[...] += jnp.broadcast_to(val, l_ref.shape)
        dw_ref[...] += dw
        dx_ref[...] = dx

    row = pl.BlockSpec((tm, d), lambda i: (i, 0))
    return pl.pallas_call(
        body, name=name, grid=(s // tm,),
        in_specs=[row, row, _whole(nw)],
        out_specs=[pl.BlockSpec((8, LANES), lambda i: (0, 0)), row, _whole(nw)],
        out_shape=[jax.ShapeDtypeStruct((8, LANES), F32), jax.ShapeDtypeStruct((s, d), F32),
                   jax.ShapeDtypeStruct(nw.shape, F32)],
        compiler_params=_params(("arbitrary",)),
    )(x, tgt, nw)


class Dims:
    def __init__(self, s, d, ffn):
        self.s, self.d, self.ffn = s, d, ffn
        self.mix = 3 * d // 4
        self.nh = self.mix // HEAD
        self.ssm_heads = self.mix // SSM_P
        self.pairs = self.mix // (2 * SSM_P)
        self.nc = s // CHUNK
        self.conv_ssm = self.mix + 4 * HEAD
        self.conv_w = self.conv_ssm + 3 * self.mix
        self.o_gates = 4 * self.mix
        self.o_sz = self.o_gates + 3 * d
        self.o_gz = self.o_sz + self.mix
        self.o_conv = self.o_gz + self.mix
        self.o_small = self.o_conv + self.conv_w
        used = self.o_small + LANES
        self.np = -(-used // 1280) * 1280
        self.tm = _pick(s, (256, 128, 64))
        mix, nh = self.mix, self.nh
        self.in_sizes = (mix, mix, mix, mix, mix, self.conv_ssm, self.ssm_heads, 3 * mix, mix, nh, nh, 3 * d)
        self.in_width = sum(self.in_sizes)


def w_in_tables(dm, nshard):
    off = np.cumsum((0,) + dm.in_sizes)
    hq, hf, hi, hg, sz, sxbc, sdt, gqkv, gz, gb, ga, gates = (np.arange(off[i], off[i + 1]) for i in range(12))
    hgrn = np.stack([t.reshape(dm.nh, HEAD) for t in (hq, hf, hi, hg)], axis=1).reshape(-1)
    perm = np.concatenate([hgrn, gates, sz, gz, gqkv, sxbc, sdt, gb, ga])
    perm = np.concatenate([perm, np.full(dm.np - perm.size, -1)])
    shard = dm.in_width // nshard
    wpad = -(-shard // GATHER_TILE) * GATHER_TILE
    fwd = np.where(perm >= 0, (perm // shard) * wpad + perm % shard, -1)[None]
    inv = np.zeros(dm.in_width, np.int64)
    inv[perm[perm >= 0]] = np.nonzero(perm >= 0)[0]
    bwd = np.full((nshard, wpad), -1)
    bwd[:, :shard] = inv.reshape(nshard, shard)
    return fwd.astype(np.int32), bwd.astype(np.int32)


def _small_views(dm, small):
    t = small.T
    col = lambda a: a[:, :, None]
    row = lambda a: a.reshape(a.shape[0], dm.nc, 1, CHUNK)
    a, b = dm.ssm_heads, dm.ssm_heads + dm.nh
    sdt, gb, ga = t[:a], t[a:b], t[b:b + dm.nh]
    return col(sdt), row(sdt), col(gb), col(ga), row(ga)


def _scan_specs(dm, proj, conv_out, views, lp, dproj=None):
    dt_col, dt_row, gb_col, ga_col, ga_row = views
    mixb, nh = dm.mix // LANES, dm.nh
    s, mix = dm.s, dm.mix
    lane = ("lane", LANES)
    hb = HGRN_HEADS_PER_STEP
    hw = (CHUNK, hb * LANES)
    hgrn = dict(
        nblk=nh // hb, hb=hb, fn=hgrn_chunk, batched=False, state=(HEAD, HEAD),
        seqs=[(proj, (CHUNK, hb * 4 * HEAD), lambda h, n: (n, h), ("lane", 4 * HEAD))],
        hparams=[(lp["lb"], (1, hb * HEAD), lambda h: (0, h), lane)],
        sparams=[lp["hgrn_norm"]],
        dseqs=[((s, dm.np), BF16, (CHUNK, hb * 4 * HEAD), lambda h, n: (n, h), ("lane", 4 * HEAD), dproj)],
        io=(hw, lambda h, n: (n, h), lane))
    ppg = dm.pairs // 2
    qb = 3 * mixb
    gw = (CHUNK, ppg * LANES)
    group = ("lane", ppg * LANES)
    pcol = ((2 * ppg, CHUNK, 1), lambda g, n: (g, n, 0), ("lead", 2 * ppg))
    prow = ((2 * ppg, None, 1, CHUNK), lambda g, n: (g, n, 0, 0), ("lead", 2 * ppg))
    ppar = ((2 * ppg, 1, 1), lambda g: (g, 0, 0), ("lead", 2 * ppg))
    bc = lambda first: ((CHUNK, LANES), lambda g, n: (n, first + g), None)
    ssd = dict(
        nblk=2, hb=1, fn=ssd_chunk, batched=False, state=(HEAD, ppg * LANES),
        seqs=[(conv_out, gw, lambda g, n: (n, qb // ppg + g), group), (conv_out,) + bc(qb + mixb), (conv_out,) + bc(qb + mixb + 2),
              (dt_col,) + pcol, (dt_row,) + prow],
        hparams=[(lp["ssm_dt_bias"],) + ppar, (lp["ssm_a_log"],) + ppar],
        sparams=[],
        dseqs=[((s, mix), F32, gw, lambda g, n: (n, g), group), ((s, 2 * LANES), F32) + bc(0), ((s, 2 * LANES), F32) + bc(0),
               (dt_col.shape, F32) + pcol, (dt_row.shape, F32) + prow],
        io=(gw, lambda g, n: (n, g), group))
    hb = GDN_HEADS_PER_STEP
    hw = (CHUNK, hb * LANES)
    cq, cgz = 0, dm.o_gz // LANES
    assert nh % hb == 0 and cgz % hb == 0 and qb % ppg == 0
    hcol = ((hb, CHUNK, 1), lambda h, n: (h, n, 0), ("idx",))
    hrow = ((hb, None, 1, CHUNK), lambda h, n: (h, n, 0, 0), ("idx",))
    hpar = ((hb, 1, 1), lambda h: (h, 0, 0), ("idx",))
    at = lambda first: (hw, lambda h, n: (n, first // hb + h), lane)
    gdn = dict(
        nblk=nh // hb, hb=hb, fn=gdn_chunk, batched=True, state=(HEAD, HEAD),
        seqs=[(conv_out,) + at(cq), (conv_out,) + at(cq + nh), (conv_out,) + at(cq + 2 * nh), (proj,) + at(cgz),
              (gb_col,) + hcol, (ga_col,) + hcol, (ga_row,) + hrow],
        hparams=[(lp["gdn_dt_bias"],) + hpar, (lp["gdn_a_log"],) + hpar],
        sparams=[lp["gdn_norm"]],
        dseqs=[((s, mix), F32) + at(0), ((s, mix), F32) + at(0), ((s, mix), F32) + at(0), ((s, dm.np), BF16) + at(cgz) + (dproj,),
               (gb_col.shape, F32) + hcol, (ga_col.shape, F32) + hcol, (ga_row.shape, F32) + hrow],
        io=(hw, lambda h, n: (n, h), lane))
    return hgrn, ssd, gdn


def _run_scan_fwd(dm, name, sp, side=None):
    out = ((dm.s, dm.mix), F32) + sp["io"]
    (y,), states, arrived = scan_fwd(name, sp["fn"], sp["nblk"], sp["hb"], dm.nc, sp["seqs"], sp["hparams"],
                                     sp["sparams"], sp["state"], [out], sp["batched"], side)
    return y, states, arrived


def _run_scan_bwd(dm, name, sp, states, dy, side=None):
    return scan_bwd(name, sp["fn"], sp["nblk"], sp["hb"], dm.nc, sp["seqs"], sp["hparams"], sp["sparams"], sp["state"],
                    states, [(dy,) + sp["io"]], sp["dseqs"], sp["batched"], side)


SHARE_FWD = ((4, 1), (2,), (0,), (3,))
SHARE_BWD = ((0,), (4, 2), (3,), (1,))


def _share_out(side, share):
    if side is None:
        return None, None, None, None
    s, broadcast = side
    return tuple(([s[i] for i in idx], broadcast) for idx in share)


def _collect(share, *got):
    if not got[0]:
        return None
    out = [None] * len(GATHERED)
    for idx, arrived in zip(share, got):
        for i, t in zip(idx, arrived):
            out[i] = t
    return out


def layer_fwd(dm, l, x, lp, side=None):
    tm, d, mix = dm.tm, dm.d, dm.mix
    tag = f"l{l}_"
    (h,) = rowstage_fwd(tag + "norm1", normmod_fn, [(x, d, 0)], [lp["norm_mix"], lp["sc1"], lp["sh1"]], [(d, BF16)], tm)
    side_h, side_s, side_g, side_m = _share_out(side, SHARE_FWD)
    if side:
        proj, got_m = matmul(tag + "proj", h, lp["w_in"], "nn", F32, side_m)
    else:
        proj, got_m = matmul(tag + "proj", h, lp["w_in"], "nn", F32), None
    conv_out = conv_fwd(tag + "conv", proj, dm.o_conv // LANES, lp["conv_w"], lp["conv_b"])
    small = proj[:, dm.o_small:dm.o_small + LANES]
    views = _small_views(dm, small)
    hg, sd, gd = _scan_specs(dm, proj, conv_out, views, lp)
    yh, st_h, got_h = _run_scan_fwd(dm, tag + "hgrn", hg, side_h)
    y_ssd, st_s, got_s = _run_scan_fwd(dm, tag + "ssd", sd, side_s)
    yg, st_g, got_g = _run_scan_fwd(dm, tag + "gdn", gd, side_g)
    arrived = _collect(SHARE_FWD, got_h, got_s, got_g, got_m)
    (ys,) = rowstage_fwd(tag + "ssmpost", ssmpost_fn,
                         [(y_ssd, mix, 0), (conv_out, mix, 3), (proj, mix, dm.o_sz // mix)],
                         [lp["ssm_d_exp"], lp["ssm_norm"]], [(mix, F32)], tm)
    (merged,) = rowstage_fwd(tag + "merge", merge_fn, [(yh, mix, 0), (ys, mix, 0), (yg, mix, 0), (proj, 3 * d, 1)],
                             [lp["b_merge"], lp["w_branch"]], [(d, BF16)], tm)
    (x1,) = rowstage_fwd(tag + "outproj", outproj_fn, [(merged, d, 0), (x, d, 0)], [lp["g1"], lp["w_out"]], [(d, F32)], tm)
    (h2,) = rowstage_fwd(tag + "norm2", normmod_fn, [(x1, d, 0)], [lp["norm_ffn"], lp["sc2"], lp["sh2"]], [(d, BF16)], tm)
    gu = bmatmul(tag + "ffn_in", h2, lp["w_ffn_in"], "nn", BF16, True)
    gu = gu.reshape((2, gu.shape[0] // 2) + gu.shape[1:])
    act = swiglu3_fwd(tag + "swiglu", gu, tm)
    o2 = bmatmul(tag + "ffn_out", act, lp["w_ffn_out"], "nn", F32, False)
    (x2,) = rowstage_fwd(tag + "resid", resid_fn, [(x1, d, 0), (o2, d, 0)], [lp["g2"]], [(d, F32)], tm)
    saved = dict(x=x, h=h, proj=proj, conv_out=conv_out, views=views, yh=yh, y_ssd=y_ssd, yg=yg, ys=ys,
                 st_h=st_h, st_s=st_s, st_g=st_g, merged=merged, x1=x1, h2=h2, gu=gu, act=act, o2=o2)
    return x2, saved, arrived


def layer_bwd(dm, l, dx2, lp, sv, side=None, own=False):
    tm, d, mix, s = dm.tm, dm.d, dm.mix, dm.s
    tag = f"l{l}_b_"
    g = {}
    (dx1_a, do2), (g["g2"],) = rowstage_bwd(tag + "resid", resid_fn, [(sv["x1"], d, 0), (sv["o2"], d, 0)], [lp["g2"]],
                                            [dx2], [F32, BF16], tm)
    dact = bmatmul(tag + "ffn_out_dx", do2, lp["w_ffn_out"], "nt", BF16, True)
    g["w_ffn_out"] = bmatmul(tag + "ffn_out_dw", sv["act"], do2, "tn", F32, True)
    dgu = swiglu3_bwd(tag + "swiglu", sv["gu"], dact, tm)
    dgu = dgu.reshape((-1,) + dgu.shape[2:])
    dh2 = bmatmul(tag + "ffn_in_dx", dgu, lp["w_ffn_in"], "nt", BF16, False)
    g["w_ffn_in"] = bmatmul(tag + "ffn_in_dw", sv["h2"], dgu, "tn", F32, True)
    (dx1,), (g["norm_ffn"], g["sc2"], g["sh2"]) = rowstage_bwd(
        tag + "norm2", normmod_fn, [(sv["x1"], d, 0)], [lp["norm_ffn"], lp["sc2"], lp["sh2"]], [dh2], [F32], tm,
        adds={0: dx1_a})
    (dmerged, dx_a), (g["g1"], g["w_out"]) = rowstage_bwd(
        tag + "outproj", outproj_fn, [(sv["merged"], d, 0), (sv["x"], d, 0)], [lp["g1"], lp["w_out"]], [dx1],
        [BF16, F32], tm)
    proj, conv_out = sv["proj"], sv["conv_out"]
    dproj = lax.empty((s, dm.np), BF16)
    (dyh, dys, dyg, dproj), (g["b_merge"], g["w_branch"]) = rowstage_bwd(
        tag + "merge", merge_fn, [(sv["yh"], mix, 0), (sv["ys"], mix, 0), (sv["yg"], mix, 0), (proj, 3 * d, 1)],
        [lp["b_merge"], lp["w_branch"]], [dmerged], [F32, F32, F32, BF16], tm, into={3: (dproj, 1)})
    (dy_ssd, dxs_a, dproj), (g["ssm_d_exp"], g["ssm_norm"]) = rowstage_bwd(
        tag + "ssmpost", ssmpost_fn, [(sv["y_ssd"], mix, 0), (conv_out, mix, 3), (proj, mix, dm.o_sz // mix)],
        [lp["ssm_d_exp"], lp["ssm_norm"]], [dys], [F32, F32, BF16], tm, into={2: (dproj, dm.o_sz // mix)})
    side_h, side_s, side_g, side_m = _share_out(side, SHARE_BWD)
    hg, sd, _ = _scan_specs(dm, proj, conv_out, sv["views"], lp, dproj)
    (dproj,), (g["lb"],), (g["hgrn_norm"],), got_h = _run_scan_bwd(dm, tag + "hgrn", hg, sv["st_h"], dyh, side_h)
    gd = _scan_specs(dm, proj, conv_out, sv["views"], lp, dproj)[2]
    (dxs_b, dbp, dcp, d_dt_col, d_dt_row), (g["ssm_dt_bias"], g["ssm_a_log"]), _, got_s = _run_scan_bwd(
        dm, tag + "ssd", sd, sv["st_s"], dy_ssd, side_s)
    (dq, dk, dv, dproj, d_gb_col, d_ga_col, d_ga_row), (g["gdn_dt_bias"], g["gdn_a_log"]), (g["gdn_norm"],), got_g = _run_scan_bwd(
        dm, tag + "gdn", gd, sv["st_g"], dyg, side_g)
    dconv =jnp.concatenate([dq, dk, dv, dxs_a + dxs_b, dbp, dcp], axis=1)
    dproj, g["conv_w"], g["conv_b"] = conv_bwd(tag + "conv", proj, dm.o_conv // LANES, lp["conv_w"], lp["conv_b"], dconv,
                                               dproj)
    unrow = lambda t: t.reshape(t.shape[0], s).T
    dsmall = jnp.concatenate([d_dt_col[:, :, 0].T + unrow(d_dt_row), d_gb_col[:, :, 0].T,
                              d_ga_col[:, :, 0].T + unrow(d_ga_row)], axis=1)
    tail = jnp.pad(dsmall.astype(BF16), ((0, 0), (0, dm.np - dm.o_small - dsmall.shape[1])))
    dproj = lax.dynamic_update_slice(dproj, tail, (0, dm.o_small))
    beside_dx = list(side_m[0]) if side else []
    if own:
        s_wb, s_wout, s_wf, s_wfo = small_shards(dm, g)
        beside_dx = beside_dx + [s_wf]
        g["w_in"], (got_wb, got_wout, got_wfo) = matmul(tag + "proj_dw", sv["h"], dproj, "tn", F32,
                                                        ([s_wb, s_wout, s_wfo], False))
    else:
        g["w_in"] = matmul(tag + "proj_dw", sv["h"], dproj, "tn", F32)
    if beside_dx:
        dh, got_dx = matmul(tag + "proj_dx", dproj, lp["w_in"], "nt", BF16, (beside_dx, False))
    else:
        dh, got_dx = matmul(tag + "proj_dx", dproj, lp["w_in"], "nt", BF16), []
    arrived = _collect(SHARE_BWD, got_h, got_s, got_g, got_dx[:1]) if side else None
    (dx,), (g["norm_mix"], g["sc1"], g["sh1"]) = rowstage_bwd(
        tag + "norm1", normmod_fn, [(sv["x"], d, 0)], [lp["norm_mix"], lp["sc1"], lp["sh1"]], [dh], [F32], tm,
        adds={0: dx_a})
    if own:
        return dx, g, arrived, [got_wb, got_wout, got_dx[-1], got_wfo]
    return dx, g, arrived


WEIGHTS = ("w_ada", "b_ada", "norm_mix", "norm_ffn", "w_in", "b_merge", "hgrn_lb_logits", "hgrn_norm", "ssm_conv_w",
           "ssm_conv_b", "ssm_dt_bias", "ssm_a_log", "ssm_d", "ssm_norm", "gdn_conv_w", "gdn_dt_bias", "gdn_a_log",
           "gdn_norm", "w_branch", "w_out", "w_ffn_in", "w_ffn_out", "norm_final")
GATHERED = ("w_in", "w_branch", "w_out", "w_ffn_in", "w_ffn_out")
PACKET = ("b_ada", "norm_mix", "norm_ffn", "b_merge", "hgrn_norm", "ssm_conv_b", "ssm_dt_bias", "ssm_a_log", "ssm_d",
          "ssm_norm", "gdn_dt_bias", "gdn_a_log", "gdn_norm", "norm_final")
MISC = ("hgrn_lb_logits", "ssm_conv_w", "gdn_conv_w")


def _pack(arrs, dtype, row_mult, lead=0):
    flat = jnp.concatenate([t.reshape(t.shape[:lead] + (-1,)).astype(dtype) for t in arrs], axis=lead)
    n = flat.shape[-1]
    unit = row_mult * LANES
    tot = -(-n // unit) * unit
    flat = jnp.pad(flat, [(0, 0)] * lead + [(0, tot - n)])
    return flat.reshape(flat.shape[:lead] + (tot // LANES, LANES))


def _unpack(packed, shapes, lead=0):
    flat = packed.reshape(packed.shape[:lead] + (-1,))
    out, off = [], 0
    for shp in shapes:
        n = int(np.prod(shp))
        out.append(flat[..., off:off + n].reshape(flat.shape[:lead] + tuple(shp)))
        off += n
    return out


def _shard2d(t):
    return t.reshape((-1, t.shape[-1]))


def weights_from_shards(dm, l, got, idx):
    w_in, wb, w_out, wf, wfo = got
    d, mix = dm.d, dm.mix
    return dict(
        w_in=colgather(f"l{l}_w_in", w_in, idx, dm.np, BF16)[0],
        w_branch=wb.reshape(N_DEV, 3, mix, d // N_DEV).transpose(1, 2, 0, 3).reshape(3, mix, d),
        w_out=w_out.reshape(d, d), w_ffn_in=wf, w_ffn_out=wfo.reshape(N_DEV // 2, -1, d))


def small_shards(dm, g):
    d, mix = dm.d, dm.mix
    return [g["w_branch"].reshape(3, mix, N_DEV, d // N_DEV).transpose(2, 0, 1, 3).reshape(N_DEV, 3 * mix, d // N_DEV),
            g["w_out"].reshape(N_DEV, d // N_DEV, d), g["w_ffn_in"], g["w_ffn_out"].reshape(N_DEV, -1, d)]


def w_in_shards(dm, l, g, idx):
    return colgather(f"l{l}_g_w_in", g["w_in"][None], idx, dm.in_width // N_DEV, F32)


def layer_params(dm, l, full, small, mod_l, lb_l):
    d, mix = dm.d, dm.mix
    row = lambda t: t.reshape(1, -1)
    head = lambda t: t.reshape(-1, 1, 1)
    sh1, sc1, g1, sh2, sc2, g2 = (row(mod_l[i * d:(i + 1) * d]) for i in range(6))
    conv_b = jnp.concatenate([jnp.zeros((3 * mix,), F32), small["ssm_conv_b"][l]])
    return dict(
        w_in=full["w_in"], w_branch=full["w_branch"], w_out=full["w_out"],
        w_ffn_in=full["w_ffn_in"], w_ffn_out=full["w_ffn_out"],
        norm_mix=row(small["norm_mix"][l]), norm_ffn=row(small["norm_ffn"][l]), b_merge=row(small["b_merge"][l]),
        hgrn_norm=row(small["hgrn_norm"][l]), lb=row(lb_l),
        conv_w=jnp.concatenate([small["gdn_conv_w"][l], small["ssm_conv_w"][l]], axis=1), conv_b=row(conv_b),
        ssm_dt_bias=head(small["ssm_dt_bias"][l]), ssm_a_log=head(small["ssm_a_log"][l]),
        ssm_d_exp=row(jnp.repeat(small["ssm_d"][l], SSM_P)), ssm_norm=row(small["ssm_norm"][l]),
        gdn_dt_bias=head(small["gdn_dt_bias"][l]), gdn_a_log=head(small["gdn_a_log"][l]), gdn_norm=row(small["gdn_norm"][l]),
        sh1=sh1, sc1=sc1, g1=g1, sh2=sh2, sc2=sc2, g2=g2)


def layer_grads(dm, g):
    cs = 3 * dm.mix
    out = dict(
        w_in=g["w_in"], w_branch=g["w_branch"], w_out=g["w_out"], w_ffn_in=g["w_ffn_in"],
        w_ffn_out=g["w_ffn_out"], norm_mix=g["norm_mix"][0], norm_ffn=g["norm_ffn"][0], b_merge=g["b_merge"][0],
        hgrn_norm=g["hgrn_norm"][0], ssm_conv_w=g["conv_w"][:, cs:], gdn_conv_w=g["conv_w"][:, :cs],
        ssm_conv_b=g["conv_b"][0, cs:], ssm_dt_bias=g["ssm_dt_bias"][:, 0, 0], ssm_a_log=g["ssm_a_log"][:, 0, 0],
        ssm_d=g["ssm_d_exp"].reshape(dm.ssm_heads, SSM_P).sum(axis=1), ssm_norm=g["ssm_norm"][0],
        gdn_dt_bias=g["gdn_dt_bias"][:, 0, 0], gdn_a_log=g["gdn_a_log"][:, 0, 0], gdn_norm=g["gdn_norm"][0])
    dmod = jnp.concatenate([g[k][0] for k in ("sh1", "sc1", "g1", "sh2", "sc2", "g2")])
    return out, dmod, g["lb"][0]


def local_step(dm, nl, x, tgt, norm_final, params_of, gather_of=None, scatter_of=None):
    arrived = exchange("gather_w0", gather_of(0), True) if gather_of else None
    lps, saved = [], []
    for l in range(nl):
        lps.append(params_of(l, arrived))
        side = (gather_of(l + 1), True) if gather_of and l + 1 < nl else None
        x, sv, arrived = layer_fwd(dm, l, x, lps[l], side)
        saved.append(sv)
    loss, dx, dnf = loss_call("loss", x, tgt, norm_final, dm.tm)
    grads, parts, side = [None] * nl, [None] * nl, None
    for l in reversed(range(nl)):
        if scatter_of and l == 0:
            dx, grads[l], got, own = layer_bwd(dm, l, dx, lps[l], saved[l], side, own=True)
            parts[0] = list(exchange("scatter_g0", [scatter_of(0, grads[0])], False)) + own
        else:
            dx, grads[l], got = layer_bwd(dm, l, dx, lps[l], saved[l], side)
        if side is not None:
            parts[l + 1] = got
        side = ([scatter_of(l, grads[l])] + small_shards(dm, grads[l]), False) if scatter_of and l > 0 else None
    return loss, dx, dnf, grads, parts


def kernel(x, c, w_ada, b_ada, norm_mix, norm_ffn, w_in, b_merge, hgrn_lb_logits, hgrn_norm, ssm_conv_w, ssm_conv_b, ssm_dt_bias, ssm_a_log, ssm_d, ssm_norm, gdn_conv_w, gdn_dt_bias, gdn_a_log, gdn_norm, w_branch, w_out, w_ffn_in, w_ffn_out, norm_final, loss_target, m_w_ada, m_b_ada, m_norm_mix, m_norm_ffn, m_w_in, m_b_merge, m_hgrn_lb_logits, m_hgrn_norm, m_ssm_conv_w, m_ssm_conv_b, m_ssm_dt_bias, m_ssm_a_log, m_ssm_d, m_ssm_norm, m_gdn_conv_w, m_gdn_dt_bias, m_gdn_a_log, m_gdn_norm, m_w_branch, m_w_out, m_w_ffn_in, m_w_ffn_out, m_norm_final, v_w_ada, v_b_ada, v_norm_mix, v_norm_ffn, v_w_in, v_b_merge, v_hgrn_lb_logits, v_hgrn_norm, v_ssm_conv_w, v_ssm_conv_b, v_ssm_dt_bias, v_ssm_a_log, v_ssm_d, v_ssm_norm, v_gdn_conv_w, v_gdn_dt_bias, v_gdn_a_log, v_gdn_norm, v_w_branch, v_w_out, v_w_ffn_in, v_w_ffn_out, v_norm_final):
    a = dict(locals())
    x, tgt = a["x"][0], a["loss_target"][0]
    s, d = x.shape
    nl = a["w_ada"].shape[0]
    dm = Dims(s, d, a["w_ffn_out"].shape[1] * N_DEV)
    me = 4 * lax.axis_index("x") + 2 * lax.axis_index("y") + lax.axis_index("c")

    first = [a["c"], a["ssm_conv_w"], a["gdn_conv_w"]]
    c_all, scw, gcw = _unpack(exchange("gather_c", [_pack(first, F32, 8)], True)[0], [t.shape for t in first], lead=1)
    small = dict(a, ssm_conv_w=scw.transpose(1, 2, 0, 3).reshape(scw.shape[1:3] + (-1,)),
                 gdn_conv_w=gcw.transpose(1, 2, 0, 3).reshape(gcw.shape[1:3] + (-1,)))
    c_pad = jnp.zeros((LANES, d), F32).at[:N_DEV].set(c_all.reshape(N_DEV, d))
    ncol = a["w_ada"].shape[2]
    b_mine = lax.dynamic_slice(a["b_ada"], (0, me * ncol), (nl, ncol))[:, None, :]
    mod_part = ada_fwd("ada_fwd", c_pad, a["w_ada"], b_mine)[:, :N_DEV, :]
    (mod,) = exchange("a2a_mod", [mod_part.transpose(1, 0, 2)], False)
    mod = mod.transpose(1, 0, 2).reshape(nl, N_DEV * ncol)
    (lb,) = rowstage_fwd("lower_bounds", lower_bounds_fn, [(a["hgrn_lb_logits"], dm.mix, 0)], [], [(dm.mix, F32)], nl)

    idx_fwd, idx_bwd = w_in_tables(dm, N_DEV)
    loss, dx, dnf, grads, parts = local_step(
        dm, nl, x, tgt, a["norm_final"].reshape(1, d),
        params_of=lambda l, got: layer_params(dm, l, weights_from_shards(dm, l, got, idx_fwd), small, mod[l], lb[l]),
        gather_of=lambda l: [_shard2d(a[n][l]).astype(BF16) for n in GATHERED],
        scatter_of=lambda l, g: w_in_shards(dm, l, g, idx_bwd))

    per_layer = [layer_grads(dm, g) for g in grads]
    res = {}
    for i, n in enumerate(GATHERED):
        wmv = [a[q + n].reshape((nl, -1, a[n].shape[-1])) for q in ("", "m_", "v_")]
        outs = None
        for l in range(nl):
            outs = adamw_sum(f"adamw_l{l}_{n}", parts[l][i], *wmv, layer=l, into=outs)
        for kind, o in zip(("grad", "delta", "new_m", "new_v"), outs):
            res[(kind, n)] = o.reshape(a[n].shape)

    stackg = lambda n: jnp.stack([pl_[0][n] for pl_ in per_layer])
    dmod = jnp.stack([pl_[1] for pl_ in per_layer])
    dlb = jnp.stack([pl_[2] for pl_ in per_layer])
    pk_g = [dmod if n == "b_ada" else dnf if n == "norm_final" else stackg(n) for n in PACKET]
    extra = [dlb, stackg("ssm_conv_w"), stackg("gdn_conv_w"), loss[0, :1]]
    pk_shapes = [t.shape for t in pk_g + extra]
    zeros = [jnp.zeros(t.shape, F32) for t in extra]
    (parts,) = exchange("gather_small", [_pack(pk_g + extra, F32, 8)], True)
    outs = adamw_sum("adamw_small", parts, *[_pack([a[p + n] for n in PACKET] + zeros, F32, 8) for p in ("", "m_", "v_")])
    for kind, o in zip(("grad", "delta", "new_m", "new_v"), outs):
        un = _unpack(o, pk_shapes)
        for n, t in zip(PACKET, un):
            res[(kind, n)] = t.reshape(a[n].shape)
        if kind == "grad":
            dlb_sum, g_scw, g_gcw, loss_sum = un[len(PACKET):]

    (g_lb,), _ = rowstage_bwd("lower_bounds_b", lower_bounds_fn, [(a["hgrn_lb_logits"], dm.mix, 0)], [], [dlb_sum], [F32], nl)
    mine = lambda t, n: lax.dynamic_slice_in_dim(t, me * a[n].shape[-1], a[n].shape[-1], axis=t.ndim - 1)
    (dmod_cols,) = exchange("a2a_dmod", [dmod.reshape(nl, N_DEV, ncol).transpose(1, 0, 2)], False)
    dmod_pad = jnp.zeros((nl, LANES, ncol), F32).at[:, :N_DEV].set(dmod_cols.transpose(1, 0, 2))
    g_w_ada = ada_bwd("ada_bwd", c_pad, dmod_pad)
    outs = adamw_sum("adamw_w_ada", g_w_ada.reshape(1, nl * d, ncol), *[a[q + "w_ada"].reshape(nl * d, ncol) for q in ("", "m_", "v_")])
    for kind, o in zip(("grad", "delta", "new_m", "new_v"), outs):
        res[(kind, "w_ada")] = o.reshape(nl, d, ncol)
    g_misc = [g_lb, mine(g_scw, "ssm_conv_w"), mine(g_gcw, "gdn_conv_w")]
    outs = adamw_sum("adamw_misc", _pack(g_misc, F32, 8)[None], *[_pack([a[q + n] for n in MISC], F32, 8) for q in ("", "m_", "v_")])
    for kind, o in zip(("grad", "delta", "new_m", "new_v"), outs):
        for n, t in zip(MISC, _unpack(o, [a[n].shape for n in MISC])):
            res[(kind, n)] = t

    out = [loss_sum.reshape(()), dx[None]]
    for kind in ("grad", "delta", "new_m", "new_v"):
        out += [res[(kind, n)] for n in WEIGHTS]
    return tuple(out)
```

```python
import functools
import math

import numpy as np
import jax
import jax.numpy as jnp
from jax import lax
from jax.experimental import pallas as pl
from jax.experimental.pallas import tpu as pltpu

F32 = jnp.float32
BF16 = jnp.bfloat16

N_DEV = 8
CHUNK = 64
SUB = 8
HGRN_HEADS_PER_STEP = 6
GDN_HEADS_PER_STEP = 6
HEAD = 128
SSM_P = 64
CONV_K = 4
F_MIN = 1e-30
NORM_EPS = 1e-6
LANES = 128
GATHER_TILE = 256
VMEM_LIMIT = 56 * 1024 * 1024

ADAM_LR = 0.001
ADAM_B1 = 0.9
ADAM_B2 = 0.999
ADAM_EPS = 1e-08
ADAM_WD = 0.01
ADAM_STEP = 10


def _dg(a, b, ca, cb):
    return lax.dot_general(a.astype(BF16), b.astype(BF16), (((ca,), (cb,)), ((), ())),
                           preferred_element_type=F32)


def _split3(x):
    x1 = x.astype(BF16)
    r = x - x1.astype(F32)
    x2 = r.astype(BF16)
    x3 = (r - x2.astype(F32)).astype(BF16)
    return x1, x2, x3


def _hdg(a, b, ca, cb):
    a1, a2, _ = _split3(a)
    b1, b2, _ = _split3(b)
    dn = (((ca,), (cb,)), ((), ()))
    d = lambda p, q: lax.dot_general(p, q, dn, preferred_element_type=F32)
    return (d(a2, b1) + d(a1, b2)) + d(a1, b1)


def _dot_family(prim):
    @jax.custom_vjp
    def nn(a, b):
        return prim(a, b, 1, 0)

    @jax.custom_vjp
    def nt(a, b):
        return prim(a, b, 1, 1)

    @jax.custom_vjp
    def tn(a, b):
        return prim(a, b, 0, 0)

    nn.defvjp(lambda a, b: (nn(a, b), (a, b)), lambda r, g: (nt(g, r[1]), tn(r[0], g)))
    nt.defvjp(lambda a, b: (nt(a, b), (a, b)), lambda r, g: (nn(g, r[1]), tn(g, r[0])))
    tn.defvjp(lambda a, b: (tn(a, b), (a, b)), lambda r, g: (nt(r[1], g), nn(r[0], g)))
    return nn, nt, tn


mm_nn, mm_nt, mm_tn = _dot_family(_dg)
hd_nn, hd_nt, hd_tn = _dot_family(_hdg)


def _iota(shape, dim):
    return lax.broadcasted_iota(jnp.int32, shape, dim)


def _scan_rows(x, reverse):
    n = x.shape[0]
    rows = _iota(x.shape, 0)
    k = 1
    while k < n:
        if reverse:
            x = x + jnp.where(rows < n - k, pltpu.roll(x, n - k, 0), 0.0)
        else:
            x = x + jnp.where(rows >= k, pltpu.roll(x, k, 0), 0.0)
        k *= 2
    return x


@jax.custom_vjp
def cumsum_rows(x):
    return _scan_rows(x, False)


cumsum_rows.defvjp(lambda x: (_scan_rows(x, False), None), lambda _, g: (_scan_rows(g, True),))


def _sigmoid(x):
    return jax.nn.sigmoid(x)


def _silu(x):
    return x * jax.nn.sigmoid(x)


def _softplus(x):
    e = jnp.exp(-jnp.abs(x))
    small = e * (1.0 - e * (0.5 - e * (1.0 / 3.0)))
    return jnp.maximum(x, 0.0) + jnp.where(e < 1e-3, small, jnp.log(1.0 + e))


def _masked_exp(diff, mask):
    return jnp.where(mask, jnp.exp(jnp.where(mask, diff, 0.0)), 0.0)


def _rms(x, w):
    return x * lax.rsqrt(jnp.mean(x * x, axis=-1, keepdims=True) + NORM_EPS) * w


def _cum_col_row(lg_col, lg_row):
    c = lg_col.shape[0]
    r, s = _iota((c, c), 0), _iota((c, c), 1)
    cum_col = jnp.sum(jnp.where(s <= r, jnp.broadcast_to(lg_row, (c, c)), 0.0), axis=1, keepdims=True)
    cum_row = jnp.sum(jnp.where(r <= s, jnp.broadcast_to(lg_col, (c, c)), 0.0), axis=0, keepdims=True)
    total = jnp.sum(lg_col, axis=0, keepdims=True)
    return cum_col, cum_row, total


def hgrn_chunk(seq, hp, sp, st):
    (blk,), (lb,), (nw,) = seq, hp, sp
    c = blk.shape[0]
    q_raw, f_raw, v, g_raw = (blk[:, i * HEAD:(i + 1) * HEAD] for i in range(4))
    q = _silu(q_raw)
    f = lb + (1.0 - lb) * _sigmoid(f_raw)
    logf = jnp.log(jnp.maximum(f, F_MIN))
    k = (1.0 - lb) * _sigmoid(-f_raw)
    b = cumsum_rows(logf)
    o_inter = mm_nt(q * jnp.exp(b), st)
    nsub = c // SUB
    wide = (SUB, SUB, HEAD)
    er = _iota((SUB * SUB, SUB), 0)
    e_t = (er // SUB == _iota((SUB * SUB, SUB), 1)).astype(F32)
    pr = _iota((SUB * SUB, 1), 0)
    pmask = (pr % SUB) <= (pr // SUB)
    er64 = _iota((SUB * SUB, c), 0)
    ec64 = _iota((SUB * SUB, c), 1)
    rows_c = _iota((c, 1), 0)
    row = lambda a, i: jnp.sum(jnp.where(rows_c == i, a, 0.0), axis=0, keepdims=True)
    def sub_chunk(qi, ki, bi, bref, first, place):
        qb = jnp.broadcast_to(qi[:, None, :], wide).reshape(SUB * SUB, HEAD)
        kb = jnp.broadcast_to(ki[None, :, :], wide).reshape(SUB * SUB, HEAD)
        bd = (bi[:, None, :] - bi[None, :, :]).reshape(SUB * SUB, HEAD)
        sc_col = jnp.sum(qb * kb * _masked_exp(bd, pmask), axis=1, keepdims=True)
        sc = mm_tn(e_t, sc_col * place)
        sc = sc + mm_nt(qi * jnp.exp(bi - bref), k * _masked_exp(bref - b, rows_c < first))
        return mm_nn(sc, v)

    firsts = [SUB * i for i in range(nsub)]
    pile = lambda parts: jnp.concatenate([p[None] for p in parts], axis=0)
    cut = lambda a: a.reshape(nsub, SUB, HEAD)
    brefs = pile([row(b, f) for f in firsts])
    starts = pile([jnp.full((1, 1), f, jnp.int32) for f in firsts])
    places = pile([(ec64 == (er64 % SUB) + f).astype(F32) for f in firsts])
    o_intra = jax.vmap(sub_chunk)(cut(q), cut(k), cut(b), brefs, starts, places)
    o = o_inter + o_intra.reshape(c, HEAD)
    bend = row(b, c - 1)
    st_new = st * jnp.exp(bend) + mm_tn(v, k * jnp.exp(bend - b))
    y = _rms(o, nw) * _silu(g_raw)
    return (y,), st_new


def ssd_chunk(seq, hp, sp, st):
    xs, bm, cm, dtc, dtr = seq
    dt_bias, a_log = hp
    c, width = xs.shape
    nheads = width // SSM_P
    head_of = _iota((1, width), 1) // SSM_P
    r, s = _iota((c, c), 0), _iota((c, c), 1)
    g = mm_nt(cm, bm)
    dt_l, cum_l, end_l, scores = 0.0, 0.0, 0.0, []
    for i in range(nheads):
        neg_a = -jnp.exp(a_log[i])
        dt_col = _softplus(dtc[i] + dt_bias[i])
        dt_row = _softplus(dtr[i] + dt_bias[i])
        cum_col, cum_row, total = _cum_col_row(neg_a * dt_col, neg_a * dt_row)
        mine = head_of == i
        dt_l = dt_l + jnp.where(mine, dt_col, 0.0)
        cum_l = cum_l + jnp.where(mine, cum_col, 0.0)
        end_l = end_l + jnp.where(mine, total, 0.0)
        scores.append(g * _masked_exp(cum_col - cum_row, s <= r))
    xdt = xs * dt_l
    stacked = mm_nn(jnp.concatenate(scores, axis=0), xdt)
    y_intra = 0.0
    for i in range(nheads):
        y_intra = y_intra + jnp.where(head_of == i, stacked[i * c:(i + 1) * c], 0.0)
    y_inter = mm_nn(cm, st) * jnp.exp(cum_l)
    st_new = st * jnp.exp(end_l) + mm_tn(bm, xdt * jnp.exp(end_l - cum_l))
    return (y_intra + y_inter,), st_new


def _neumann_inverse(a):
    n = a.shape[0]
    eye = (_iota((n, n), 0) == _iota((n, n), 1)).astype(F32)
    p = -a
    t = eye + p
    for _ in range(int(math.log2(n)) - 1):
        p = _hdg(p, p, 1, 0)
        t = t + _hdg(t, p, 1, 0)
    return t


@jax.custom_vjp
def inv_unit_lower(a):
    return _neumann_inverse(a)


def _inv_fwd(a):
    t = _neumann_inverse(a)
    return t, t


inv_unit_lower.defvjp(_inv_fwd, lambda t, g: (-hd_nt(hd_tn(t, g), t),))


def gdn_chunk(seq, hp, sp, st):
    q_raw, k_raw, v, z, gbc, gac, gar = seq
    dt_bias, a_log = hp
    (nw,) = sp
    c = v.shape[0]
    r, s = _iota((c, c), 0), _iota((c, c), 1)
    q = q_raw * lax.rsqrt(jnp.sum(q_raw * q_raw, axis=-1, keepdims=True) + NORM_EPS) * (HEAD ** -0.5)
    k = k_raw * lax.rsqrt(jnp.sum(k_raw * k_raw, axis=-1, keepdims=True) + NORM_EPS)
    beta = _sigmoid(gbc)
    neg_a = -jnp.exp(a_log)
    cum, cum_row, total = _cum_col_row(neg_a * _softplus(gac + dt_bias), neg_a * _softplus(gar + dt_bias))
    decay = _masked_exp(cum - cum_row, s <= r)
    kk = mm_nt(k, k)
    a_low = jnp.where(s < r, beta * kk * decay, 0.0)
    sol = hd_nn(inv_unit_lower(a_low), jnp.concatenate([v * beta, k * (beta * jnp.exp(cum))], axis=1))
    u_base, w_corr = sol[:, :HEAD], sol[:, HEAD:]
    qk = mm_nt(q, k) * decay
    u = u_base - mm_nn(w_corr, st)
    o = mm_nn(q * jnp.exp(cum), st) + mm_nn(qk, u)
    st_new = jnp.exp(total) * st + mm_tn(k * jnp.exp(total - cum), u)
    y = _rms(o, nw) * _silu(z)
    return (y,), st_new


def normmod_fn(rows, params):
    (x,), (nw, sc, sh) = rows, params
    return (_rms(x, nw) * (1.0 + sc) + sh,)


def ssmpost_fn(rows, params):
    (y, xs, z), (d_exp, nw) = rows, params
    y = (y + d_exp * xs) * _silu(z)
    gw = y.shape[1] // 2
    return (jnp.concatenate([_rms(y[:, :gw], nw[:, :gw]), _rms(y[:, gw:], nw[:, gw:])], axis=1),)


def merge_fn(rows, params):
    (yh, ys, yg, gl), (bm, wb) = rows, params
    d = wb.shape[2]
    gates = _sigmoid(gl + bm)
    out = 0.0
    for n, y in enumerate((yh, ys, yg)):
        out = out + gates[:, n * d:(n + 1) * d] * mm_nn(y, wb[n])
    return (out,)


def outproj_fn(rows, params):
    (m, x), (g1, w) = rows, params
    return (x + (1.0 + g1) * mm_nn(m, w),)


def resid_fn(rows, params):
    (x, o), (g2,) = rows, params
    return (x + (1.0 + g2) * o,)


def _params(sem, side_effects=False):
    return pltpu.CompilerParams(dimension_semantics=sem, vmem_limit_bytes=VMEM_LIMIT, has_side_effects=side_effects)


def _whole(a):
    nd = a.ndim
    return pl.BlockSpec(a.shape, lambda *_: (0,) * nd)


def _pick(n, cands):
    for c in cands:
        if n % c == 0:
            return c
    return n


def matmul(name, a, b, mode, out_dtype, side=None):
    if mode == "nn":
        (m, k), n = a.shape, b.shape[1]
    elif mode == "nt":
        (m, k), n = a.shape, b.shape[0]
    else:
        (k, m), n = a.shape, b.shape[1]
    tm = _pick(m, (1024, 512, 256, 128))
    tn = _pick(n, (1280, 1024, 1408, 768, 512, 384, 256, 128))
    tk = _pick(k, (1024, 1280, 1408, 768, 512, 256, 128))
    if mode == "tn":
        tm = _pick(m, (1024, 768, 512, 256, 128))
        tk = _pick(k, (1024, 512, 256, 128))
    nk = k // tk
    ca, cb = {"nn": (1, 0), "nt": (1, 1), "tn": (0, 0)}[mode]

    def core(a_ref, b_ref, o_ref, acc_ref):
        kk = pl.program_id(2)

        @pl.when(kk == 0)
        def _():
            acc_ref[...] = jnp.zeros_like(acc_ref)

        acc_ref[...] += _dg(a_ref[...], b_ref[...], ca, cb)

        @pl.when(kk == nk - 1)
        def _():
            o_ref[...] = acc_ref[...].astype(o_ref.dtype)

    a_spec = (pl.BlockSpec((tk, tm), lambda i, j, q: (q, i)) if mode == "tn"
              else pl.BlockSpec((tm, tk), lambda i, j, q: (i, q)))
    b_spec = (pl.BlockSpec((tn, tk), lambda i, j, q: (j, q)) if mode == "nt"
              else pl.BlockSpec((tk, tn), lambda i, j, q: (q, j)))
    grid = (m // tm, n // tn, nk)
    body, s_in, s_out, s_shape, s_scr, s_args = _with_side(core, 2, 1, side, grid)
    sem = ("arbitrary",) * 3 if side else ("parallel", "parallel", "arbitrary")
    res = pl.pallas_call(
        body, name=name, grid=grid,
        in_specs=[a_spec, b_spec] + s_in,
        out_specs=[pl.BlockSpec((tm, tn), lambda i, j, q: (i, j))] + s_out,
        out_shape=[jax.ShapeDtypeStruct((m, n), out_dtype)] + s_shape,
        scratch_shapes=[pltpu.VMEM((tm, tn), F32)] + s_scr,
        compiler_params=_params(sem, side is not None),
    )(a, b, *s_args)
    return (res[0], res[1:]) if side else res[0]


def bmatmul(name, a, b, mode, out_dtype, out_batched):
    ab, bb = a.ndim == 3, b.ndim == 3
    nb = a.shape[0] if ab else b.shape[0]
    a2, b2 = a.shape[-2:], b.shape[-2:]
    if mode == "nn":
        (m, k), n = a2, b2[1]
    elif mode == "nt":
        (m, k), n = a2, b2[0]
    else:
        (k, m), n = a2, b2[1]
    tm = _pick(m, (1024, 512, 256, 128))
    tn = _pick(n, (1024, 512, 256, 128))
    tk = _pick(k, (1024, 512, 256, 128))
    nk = k // tk
    ca, cb = {"nn": (1, 0), "nt": (1, 1), "tn": (0, 0)}[mode]
    ids = (lambda g: g) if out_batched else (lambda g: (g[2], g[0], g[1], g[3]))
    grid = (nb, m // tm, n // tn, nk) if out_batched else (m // tm, n // tn, nb, nk)

    def a_map(*g):
        bi, i, j, q = ids(g)
        idx = (q, i) if mode == "tn" else (i, q)
        return (bi,) + idx if ab else idx

    def b_map(*g):
        bi, i, j, q = ids(g)
        idx = (j, q) if mode == "nt" else (q, j)
        return (bi,) + idx if bb else idx

    def o_map(*g):
        bi, i, j, q = ids(g)
        return (bi, i, j) if out_batched else (i, j)

    def body(a_ref, b_ref, o_ref, acc_ref):
        bi, _, _, q = ids(tuple(pl.program_id(d) for d in range(4)))
        first = (q == 0) if out_batched else (q == 0) & (bi == 0)
        last = (q == nk - 1) if out_batched else (q == nk - 1) & (bi == nb - 1)

        @pl.when(first)
        def _():
            acc_ref[...] = jnp.zeros_like(acc_ref)

        acc_ref[...] += _dg(a_ref[...], b_ref[...], ca, cb)

        @pl.when(last)
        def _():
            o_ref[...] = acc_ref[...].astype(o_ref.dtype)

    a_blk = (tk, tm) if mode == "tn" else (tm, tk)
    b_blk = (tn, tk) if mode == "nt" else (tk, tn)
    return pl.pallas_call(
        body, name=name, grid=grid,
        in_specs=[pl.BlockSpec(((None,) if ab else ()) + a_blk, a_map), pl.BlockSpec(((None,) if bb else ()) + b_blk, b_map)],
        out_specs=pl.BlockSpec(((None,) if out_batched else ()) + (tm, tn), o_map),
        out_shape=jax.ShapeDtypeStruct(((nb,) if out_batched else ()) + (m, n), out_dtype),
        scratch_shapes=[pltpu.VMEM((tm, tn), F32)],
        compiler_params=_params(("parallel", "parallel", "arbitrary", "arbitrary")),
    )(a, b)


def colgather(name, src, idx, dst_w, out_dtype):
    nsrc, rows, w = src.shape
    tw = GATHER_TILE
    nbs = -(-w // tw)
    ne = idx.shape[0]
    nbd = idx.shape[1] // tw
    tiles = [sorted(set((idx[e, t * tw:(t + 1) * tw][idx[e, t * tw:(t + 1) * tw] >= 0] // tw).tolist()))
             for e in range(ne) for t in range(nbd)]
    nslot = max(1, max(len(t) for t in tiles))
    tbl = np.full((ne * nbd, nslot), -1, np.int32)
    for i, t in enumerate(tiles):
        tbl[i, :len(t)] = t
    exact3 = src.dtype == F32

    def body(tbl_ref, idx_ref, src_ref, o_ref, acc_ref):
        ti, si = pl.program_id(0), pl.program_id(1)

        @pl.when(si == 0)
        def _():
            acc_ref[...] = jnp.zeros_like(acc_ref)

        t = tbl_ref[ti * nslot + si]

        @pl.when(t >= 0)
        def _():
            onehot = ((_iota((tw, tw), 0) + t * tw) == idx_ref[...]).astype(BF16)
            col = _iota((1, tw), 1) + (t % nbs) * tw
            xv = jnp.where(col < w, src_ref[...], jnp.zeros((), src_ref.dtype))
            d = lambda p: lax.dot_general(p, onehot, (((1,), (0,)), ((), ())), preferred_element_type=F32)
            if exact3:
                x1, x2, x3 = _split3(xv)
                acc_ref[...] += (d(x3) + d(x2)) + d(x1)
            else:
                acc_ref[...] += d(xv)

        @pl.when(si == nslot - 1)
        def _():
            o_ref[...] = acc_ref[...].astype(o_ref.dtype)

    def src_map(ti, si, tbl_ref):
        t = jnp.maximum(tbl_ref[ti * nslot + si], 0)
        return (t // nbs, 0, t % nbs)

    grid_spec = pltpu.PrefetchScalarGridSpec(
        num_scalar_prefetch=1, grid=(ne * nbd, nslot),
        in_specs=[pl.BlockSpec((None, 1, tw), lambda ti, si, tbl_ref: (ti // nbd, 0, ti % nbd)),
                  pl.BlockSpec((None, rows, tw), src_map)],
        out_specs=pl.BlockSpec((None, rows, tw), lambda ti, si, tbl_ref: (ti // nbd, 0, ti % nbd)),
        scratch_shapes=[pltpu.VMEM((rows, tw), F32)])
    return pl.pallas_call(
        body, name=name, grid_spec=grid_spec,
        out_shape=jax.ShapeDtypeStruct((ne, rows, dst_w), out_dtype),
        compiler_params=_params(("parallel", "arbitrary")),
    )(jnp.asarray(tbl.reshape(-1)), jnp.asarray(idx.reshape(ne, 1, nbd * tw).astype(np.int32)), src)


def swiglu3_fwd(name, gu, tm):
    _, nb, s, w = gu.shape

    def body(x_ref, o_ref):
        o_ref[...] = (_silu(x_ref[0].astype(F32)) * x_ref[1].astype(F32)).astype(o_ref.dtype)

    return pl.pallas_call(
        body, name=name, grid=(nb, s // tm),
        in_specs=[pl.BlockSpec((2, None, tm, w), lambda b, i: (0, b, i, 0))],
        out_specs=pl.BlockSpec((None, tm, w), lambda b, i: (b, i, 0)),
        out_shape=jax.ShapeDtypeStruct((nb, s, w), BF16),
        compiler_params=_params(("parallel", "parallel")),
    )(gu)


def swiglu3_bwd(name, gu, dact, tm):
    _, nb, s, w = gu.shape

    def body(x_ref, g_ref, o_ref):
        _, vjp = jax.vjp(lambda a, b: _silu(a) * b, x_ref[0].astype(F32), x_ref[1].astype(F32))
        dg, du = vjp(g_ref[...].astype(F32))
        o_ref[0] = dg.astype(o_ref.dtype)
        o_ref[1] = du.astype(o_ref.dtype)

    return pl.pallas_call(
        body, name=name, grid=(nb, s // tm),
        in_specs=[pl.BlockSpec((2, None, tm, w), lambda b, i: (0, b, i, 0)),
                  pl.BlockSpec((None, tm, w), lambda b, i: (b, i, 0))],
        out_specs=pl.BlockSpec((2, None, tm, w), lambda b, i: (0, b, i, 0)),
        out_shape=jax.ShapeDtypeStruct(gu.shape, BF16),
        compiler_params=_params(("parallel", "parallel")),
    )(gu, dact)


def _row_specs(rows, tm):
    return [pl.BlockSpec((tm, w), lambda i, _c=c: (i, _c)) for (_, w, c) in rows]


def rowstage_fwd(name, fn, rows, params, outs, tm):
    s = rows[0][0].shape[0]
    nr, npar = len(rows), len(params)

    def body(*refs):
        r = [x[...].astype(F32) for x in refs[:nr]]
        p = [x[...].astype(F32) for x in refs[nr:nr + npar]]
        for ref, val in zip(refs[nr + npar:], fn(r, p)):
            ref[...] = val.astype(ref.dtype)

    res = pl.pallas_call(
        body, name=name, grid=(s // tm,),
        in_specs=_row_specs(rows, tm) + [_whole(p) for p in params],
        out_specs=[pl.BlockSpec((tm, w), lambda i: (i, 0)) for (w, _) in outs],
        out_shape=[jax.ShapeDtypeStruct((s, w), dt) for (w, dt) in outs],
        compiler_params=_params(("parallel",)),
    )(*[r[0] for r in rows], *params)
    return res


def rowstage_bwd(name, fn, rows, params, douts, drow_dtypes, tm, adds=None, into=None):
    s = rows[0][0].shape[0]
    nr, npar, no = len(rows), len(params), len(douts)
    adds = adds or {}
    add_idx = sorted(adds)
    na = len(add_idx)
    into = into or {}
    into_idx = sorted(into)
    nb = len(into_idx)

    def body(*refs):
        r = [x[...].astype(F32) for x in refs[:nr]]
        p = [x[...].astype(F32) for x in refs[nr:nr + npar]]
        g = [x[...].astype(F32) for x in refs[nr + npar:nr + npar + no]]
        a_refs = refs[nr + npar + no:nr + npar + no + na]
        dr_refs = refs[nr + npar + no + na + nb:nr + npar + no + na + nb + nr]
        dp_refs = refs[nr + npar + no + na + nb + nr:]
        _, vjp = jax.vjp(lambda r_, p_: tuple(fn(r_, p_)), r, p)
        dr, dp = vjp(tuple(g))
        for j, (ref, val) in enumerate(zip(dr_refs, dr)):
            if j in adds:
                val = val + a_refs[add_idx.index(j)][...].astype(F32)
            ref[...] = val.astype(ref.dtype)

        @pl.when(pl.program_id(0) == 0)
        def _():
            for ref in dp_refs:
                ref[...] = jnp.zeros_like(ref)

        for ref, val in zip(dp_refs, dp):
            ref[...] += val

    res = pl.pallas_call(
        body, name=name, grid=(s // tm,),
        in_specs=(_row_specs(rows, tm) + [_whole(p) for p in params]
                  + [pl.BlockSpec((tm, d.shape[1]), lambda i: (i, 0)) for d in douts]
                  + [pl.BlockSpec((tm, rows[j][1]), lambda i: (i, 0)) for j in add_idx]
                  + [pl.BlockSpec(memory_space=pl.ANY)] * nb),
        out_specs=([pl.BlockSpec((tm, w), lambda i, _c=(into[j][1] if j in into else 0): (i, _c))
                    for j, (_, w, _) in enumerate(rows)] + [_whole(p) for p in params]),
        out_shape=([jax.ShapeDtypeStruct(into[j][0].shape if j in into else (s, w), dt)
                    for j, ((_, w, _), dt) in enumerate(zip(rows, drow_dtypes))]
                   + [jax.ShapeDtypeStruct(p.shape, F32) for p in params]),
        input_output_aliases={nr + npar + no + na + k: j for k, j in enumerate(into_idx)},
        compiler_params=_params(("arbitrary",)),
    )(*[r[0] for r in rows], *params, *douts, *[adds[j] for j in add_idx], *[into[j][0] for j in into_idx])
    return res[:nr], res[nr:]


def _flip(index_map, nc):
    return lambda h, n: index_map(h, nc - 1 - n)


def _with_side(core, n_in, n_out, side, grid):
    if side is None:
        return core, [], [], [], [], ()
    sends, broadcast = side
    k = len(sends)

    def body(*refs):
        ins, snd = refs[:n_in], refs[n_in:n_in + k]
        outs, rcv = refs[n_in + k:n_in + k + n_out], refs[n_in + k + n_out:n_in + 2 * k + n_out]
        scr = refs[n_in + 2 * k + n_out:]
        start, wait = _exchange_ops(snd, rcv, *scr[1:], broadcast)
        ids = [pl.program_id(d) for d in range(len(grid))]
        first = functools.reduce(lambda a, b: a & b, [i == 0 for i in ids])
        last = functools.reduce(lambda a, b: a & b, [i == g - 1 for i, g in zip(ids, grid)])
        pl.when(first)(start)
        core(*ins, *outs, scr[0])
        pl.when(last)(wait)

    return body, [HBM_SPEC] * k, [HBM_SPEC] * k, _exchange_out(sends, broadcast), _exchange_sems(k), tuple(sends)


def _take(v, split, j):
    if split is None:
        return v
    if split[0] == "lane":
        return v[:, j * split[1]:(j + 1) * split[1]]
    if split[0] == "lead":
        return v[j * split[1]:(j + 1) * split[1]]
    return v[j]


def _heads(vals, specs, hb):
    return [v if s[-1] is None else jnp.stack([_take(v, s[-1], j) for j in range(hb)]) for v, s in zip(vals, specs)]


def _over_heads(chunk_fn, hb, seqs, hparams, batched):
    seq_ax = [None if s[3] is None else 0 for s in seqs]
    hp_ax = [None if s[3] is None else 0 for s in hparams]
    if batched:
        return jax.vmap(chunk_fn, in_axes=(seq_ax, hp_ax, None, 0))

    def looped(seq, hp, sp, st):
        pick = lambda vals, axes, j: [v if a is None else v[j] for v, a in zip(vals, axes)]
        res = [chunk_fn(pick(seq, seq_ax, j), pick(hp, hp_ax, j), sp, st[j]) for j in range(hb)]
        pile = lambda parts: jnp.concatenate([p[None] for p in parts], axis=0)
        return tuple(pile(o) for o in zip(*[r[0] for r in res])), pile([r[1] for r in res])

    return looped


def _where(split, j):
    if split[0] == "lane":
        return (slice(None), slice(j * split[1], (j + 1) * split[1]))
    if split[0] == "lead":
        return (slice(j * split[1], (j + 1) * split[1]),)
    return (j,)


def scan_fwd(name, chunk_fn, nblk, hb, nc, seqs, hparams, sparams, state_shape, outs, batched, side=None):
    ns, nhp, nsp, no = len(seqs), len(hparams), len(sparams), len(outs)

    def core(*refs):
        seq_r, hp_r, sp_r = refs[:ns], refs[ns:ns + nhp], refs[ns + nhp:ns + nhp + nsp]
        out_r = refs[ns + nhp + nsp:ns + nhp + nsp + no]
        st_out, st_scr = refs[-2], refs[-1]

        @pl.when(pl.program_id(1) == 0)
        def _():
            st_scr[...] = jnp.zeros_like(st_scr)

        seq_v = [x[...].astype(F32) for x in seq_r]
        hp_v = [x[...] for x in hp_r]
        sp_v = [x[...] for x in sp_r]
        st = st_scr[...]
        st_out[...] = st
        heads = _over_heads(chunk_fn, hb, seqs, hparams, batched)
        o, st_new = heads(_heads(seq_v, seqs, hb), _heads(hp_v, hparams, hb), sp_v, st)
        for ref, spec, val in zip(out_r, outs, o):
            for j in range(hb):
                ref[_where(spec[4], j)] = val[j].astype(ref.dtype)
        st_scr[...] = st_new

    nst = len(state_shape)
    body, s_in, s_out, s_shape, s_scr, s_args = _with_side(core, ns + nhp + nsp, no + 1, side, (nblk, nc))
    res = pl.pallas_call(
        body, name=name, grid=(nblk, nc),
        in_specs=([pl.BlockSpec(bs, im) for (_, bs, im, _) in seqs]
                  + [pl.BlockSpec(bs, lambda h, n, _im=im: _im(h)) for (_, bs, im, _) in hparams]
                  + [_whole(p) for p in sparams] + s_in),
        out_specs=([pl.BlockSpec(bs, im) for (_, _, bs, im, _) in outs]
                   + [pl.BlockSpec((hb, None) + tuple(state_shape), lambda h, n: (h, n) + (0,) * nst)] + s_out),
        out_shape=([jax.ShapeDtypeStruct(fs, dt) for (fs, dt, _, _, _) in outs]
                   + [jax.ShapeDtypeStruct((nblk * hb, nc) + tuple(state_shape), F32)] + s_shape),
        scratch_shapes=[pltpu.VMEM((hb,) + tuple(state_shape), F32)] + s_scr,
        compiler_params=_params(("arbitrary", "arbitrary"), side is not None),
    )(*[x[0] for x in seqs], *[x[0] for x in hparams], *sparams, *s_args)
    return res[:no], res[no], res[no + 1:]


def scan_bwd(name, chunk_fn, nblk, hb, nc, seqs, hparams, sparams, state_shape, states, douts, dseqs, batched, side=None):
    ns, nhp, nsp, no = len(seqs), len(hparams), len(sparams), len(douts)
    nst = len(state_shape)
    buf_of = [i for i, sp in enumerate(dseqs) if len(sp) > 5 and sp[5] is not None]
    bufs = [dseqs[i][5] for i in buf_of]

    def core(*refs):
        seq_r, hp_r, sp_r = refs[:ns], refs[ns:ns + nhp], refs[ns + nhp:ns + nhp + nsp]
        base = ns + nhp + nsp
        st_r = refs[base]
        do_r = refs[base + 1:base + 1 + no]
        base += 1 + no + len(bufs)
        ds_r, dhp_r, dsp_r = refs[base:base + ns], refs[base + ns:base + ns + nhp], refs[base + ns + nhp:base + ns + nhp + nsp]
        dst_scr = refs[-1]
        h, n = pl.program_id(0), pl.program_id(1)

        @pl.when(n == 0)
        def _():
            dst_scr[...] = jnp.zeros_like(dst_scr)
            for ref in dhp_r:
                ref[...] = jnp.zeros_like(ref)

        @pl.when((n == 0) & (h == 0))
        def _():
            for ref in dsp_r:
                ref[...] = jnp.zeros_like(ref)

        seq_v = [x[...].astype(F32) for x in seq_r]
        hp_v = [x[...] for x in hp_r]
        sp_v = [x[...] for x in sp_r]
        do_v = [x[...].astype(F32) for x in do_r]
        prim = (_heads(seq_v, seqs, hb), _heads(hp_v, hparams, hb), sp_v, st_r[...])
        _, vjp = jax.vjp(_over_heads(chunk_fn, hb, seqs, hparams, batched), *prim)
        ds, dhp, dsp, dst = vjp((tuple(_heads(do_v, douts, hb)), dst_scr[...]))
        for ref, spec, val in zip(ds_r, dseqs, ds):
            if spec[4] is None:
                ref[...] = val.astype(ref.dtype)
            else:
                for j in range(hb):
                    ref[_where(spec[4], j)] = val[j].astype(ref.dtype)
        for ref, spec, val in zip(dhp_r, hparams, dhp):
            for j in range(hb):
                ref[_where(spec[3], j)] += val[j]
        for ref, val in zip(dsp_r, dsp):
            ref[...] += val
        dst_scr[...] = dst

    n_in, n_out = ns + nhp + nsp + 1 + no + len(bufs), ns + nhp + nsp
    body, s_in, s_out, s_shape, s_scr, s_args = _with_side(core, n_in, n_out, side, (nblk, nc))
    res = pl.pallas_call(
        body, name=name, grid=(nblk, nc),
        in_specs=([pl.BlockSpec(bs, _flip(im, nc)) for (_, bs, im, _) in seqs]
                  + [pl.BlockSpec(bs, lambda h, n, _im=im: _im(h)) for (_, bs, im, _) in hparams]
                  + [_whole(p) for p in sparams]
                  + [pl.BlockSpec((hb, None) + tuple(state_shape), lambda h, n: (h, nc - 1 - n) + (0,) * nst)]
                  + [pl.BlockSpec(bs, _flip(im, nc)) for (_, bs, im, _) in douts]
                  + [pl.BlockSpec(memory_space=pl.ANY)] * len(bufs) + s_in),
        out_specs=([pl.BlockSpec(sp[2], _flip(sp[3], nc)) for sp in dseqs]
                   + [pl.BlockSpec(bs, lambda h, n, _im=im: _im(h)) for (_, bs, im, _) in hparams]
                   + [_whole(p) for p in sparams] + s_out),
        out_shape=([jax.ShapeDtypeStruct(sp[0], sp[1]) for sp in dseqs]
                   + [jax.ShapeDtypeStruct(x[0].shape, F32) for x in hparams]
                   + [jax.ShapeDtypeStruct(p.shape, F32) for p in sparams] + s_shape),
        scratch_shapes=[pltpu.VMEM((hb,) + tuple(state_shape), F32)] + s_scr,
        input_output_aliases={n_in - len(bufs) + k: i for k, i in enumerate(buf_of)},
        compiler_params=_params(("arbitrary", "arbitrary"), side is not None),
    )(*[x[0] for x in seqs], *[x[0] for x in hparams], *sparams, states, *[x[0] for x in douts], *bufs, *s_args)
    return res[:ns], res[ns:ns + nhp], res[ns + nhp:n_out], res[n_out:]


def _shift_down(x, n, rows):
    if n == 0:
        return x
    return jnp.where(rows >= n, pltpu.roll(x, n, 0), 0.0)


def _shift_up(x, n, rows):
    if n == 0:
        return x
    s = x.shape[0]
    return jnp.where(rows < s - n, pltpu.roll(x, s - n, 0), 0.0)


def conv_fwd(name, x, col0, w, b):
    s, cw = x.shape[0], w.shape[1]

    def body(x_ref, w_ref, b_ref, o_ref):
        xv = x_ref[...]
        rows = _iota(xv.shape, 0)
        u = jnp.broadcast_to(b_ref[...], xv.shape)
        for j in range(CONV_K):
            u = u + w_ref[j:j + 1, :] * _shift_down(xv, CONV_K - 1 - j, rows)
        o_ref[...] = _silu(u)

    return pl.pallas_call(
        body, name=name, grid=(cw // LANES,),
        in_specs=[pl.BlockSpec((s, LANES), lambda j: (0, col0 + j)),
                  pl.BlockSpec((CONV_K, LANES), lambda j: (0, j)),
                  pl.BlockSpec((1, LANES), lambda j: (0, j))],
        out_specs=pl.BlockSpec((s, LANES), lambda j: (0, j)),
        out_shape=jax.ShapeDtypeStruct((s, cw), F32),
        compiler_params=_params(("parallel",)),
    )(x, w, b)


def conv_bwd(name, x, col0, w, b, dout, into):
    s, cw = x.shape[0], w.shape[1]

    def body(x_ref, w_ref, b_ref, g_ref, into_ref, dx_ref, dw_ref, db_ref):
        xv = x_ref[...]
        rows = _iota(xv.shape, 0)
        sh = [_shift_down(xv, CONV_K - 1 - j, rows) for j in range(CONV_K)]
        u = jnp.broadcast_to(b_ref[...], xv.shape)
        for j in range(CONV_K):
            u = u + w_ref[j:j + 1, :] * sh[j]
        sg = _sigmoid(u)
        du = g_ref[...] * (sg * (1.0 + u * (1.0 - sg)))
        dx = jnp.zeros_like(xv)
        for j in range(CONV_K):
            dx = dx + w_ref[j:j + 1, :] * _shift_up(du, CONV_K - 1 - j, rows)
            dw_ref[j:j + 1, :] = jnp.sum(du * sh[j], axis=0, keepdims=True)
        dx_ref[...] = dx.astype(dx_ref.dtype)
        db_ref[...] = jnp.sum(du, axis=0, keepdims=True)

    return pl.pallas_call(
        body, name=name, grid=(cw // LANES,),
        in_specs=[pl.BlockSpec((s, LANES), lambda j: (0, col0 + j)),
                  pl.BlockSpec((CONV_K, LANES), lambda j: (0, j)),
                  pl.BlockSpec((1, LANES), lambda j: (0, j)),
                  pl.BlockSpec((s, LANES), lambda j: (0, j)),
                  pl.BlockSpec(memory_space=pl.ANY)],
        out_specs=[pl.BlockSpec((s, LANES), lambda j: (0, col0 + j)),
                   pl.BlockSpec((CONV_K, LANES), lambda j: (0, j)),
                   pl.BlockSpec((1, LANES), lambda j: (0, j))],
        out_shape=[jax.ShapeDtypeStruct(into.shape, into.dtype), jax.ShapeDtypeStruct((CONV_K, cw), F32),
                   jax.ShapeDtypeStruct((1, cw), F32)],
        input_output_aliases={4: 0},
        compiler_params=_params(("parallel",)),
    )(x, w, b, dout, into)


def exchange(name, sends, broadcast):
    nop = len(sends)

    def body(*refs):
        start, wait = _exchange_ops(refs[:nop], refs[nop:2 * nop], *refs[2 * nop:], broadcast)
        start()
        wait()

    return pl.pallas_call(
        body, name=name,
        in_specs=[HBM_SPEC] * nop, out_specs=[HBM_SPEC] * nop,
        out_shape=_exchange_out(sends, broadcast), scratch_shapes=_exchange_sems(nop),
        compiler_params=pltpu.CompilerParams(has_side_effects=True),
    )(*sends)


HBM_SPEC = pl.BlockSpec(memory_space=pltpu.HBM)


def _exchange_out(sends, broadcast):
    return [jax.ShapeDtypeStruct((N_DEV,) + tuple(t.shape if broadcast else t.shape[1:]), t.dtype) for t in sends]


def _exchange_sems(nop):
    return [pltpu.SemaphoreType.DMA((nop * (N_DEV - 1),)), pltpu.SemaphoreType.DMA((nop * (N_DEV - 1),)),
            pltpu.SemaphoreType.DMA((nop,))]


def _gather_ops(send_refs, recv_refs, send_sems, recv_sems, local_sems):
    nop = len(send_refs)
    x, y, c = lax.axis_index("x"), lax.axis_index("y"), lax.axis_index("c")
    me, other = (x, y, c), (x, y, 1 - c)
    chips = [(1 - x, y), (x, 1 - y), (1 - x, 1 - y)]
    slab = lambda dev: 4 * dev[0] + 2 * dev[1] + dev[2]

    def copy(i, k, block, to, passed_on=False):
        return pltpu.make_async_remote_copy(
            src_ref=recv_refs[i].at[slab(block)] if passed_on else send_refs[i], dst_ref=recv_refs[i].at[slab(block)],
            send_sem=send_sems.at[i * (N_DEV - 1) + k], recv_sem=recv_sems.at[i * (N_DEV - 1) + k],
            device_id=to, device_id_type=pl.DeviceIdType.MESH)

    def local(i):
        return pltpu.make_async_copy(send_refs[i], recv_refs[i].at[slab(me)], local_sems.at[i])

    def start():
        for i in range(nop):
            local(i).start()
            copy(i, 0, me, other).start()
        for j, chip in enumerate(chips):
            for i in range(nop):
                copy(i, 1 + j, me, chip + (c,)).start()

    def wait():
        for j, chip in enumerate(chips):
            for i in range(nop):
                copy(i, 1 + j, chip + (c,), me).wait_recv()
                copy(i, 4 + j, chip + (c,), other, passed_on=True).start()
        for i in range(nop):
            copy(i, 0, other, me).wait_recv()
        for j, chip in enumerate(chips):
            for i in range(nop):
                copy(i, 4 + j, chip + (1 - c,), me, passed_on=True).wait_recv()
        for i in range(nop):
            copy(i, 0, me, other).wait_send()
            for j, chip in enumerate(chips):
                copy(i, 1 + j, me, chip + (c,)).wait_send()
                copy(i, 4 + j, chip + (c,), other, passed_on=True).wait_send()
            local(i).wait()

    return start, wait


def _exchange_ops(send_refs, recv_refs, send_sems, recv_sems, local_sems, broadcast):
    if broadcast:
        return _gather_ops(send_refs, recv_refs, send_sems, recv_sems, local_sems)
    nop = len(send_refs)
    x, y, c = lax.axis_index("x"), lax.axis_index("y"), lax.axis_index("c")
    me = 4 * x + 2 * y + c
    peers = []
    for k in range(1, N_DEV):
        px = 1 - x if (k >> 2) & 1 else x
        py = 1 - y if (k >> 1) & 1 else y
        pc = 1 - c if k & 1 else c
        peers.append(((px, py, pc), 4 * px + 2 * py + pc))

    def src(i, peer):
        return send_refs[i].at[peer]

    def remote(i, k, arrival):
        dev, peer = peers[k]
        return pltpu.make_async_remote_copy(
            src_ref=src(i, peer), dst_ref=recv_refs[i].at[peer if arrival else me],
            send_sem=send_sems.at[i * (N_DEV - 1) + k], recv_sem=recv_sems.at[i * (N_DEV - 1) + k],
            device_id=dev, device_id_type=pl.DeviceIdType.MESH)

    def local(i):
        return pltpu.make_async_copy(src(i, me), recv_refs[i].at[me], local_sems.at[i])

    def start():
        for i in range(nop):
            local(i).start()
        for k in range(N_DEV - 1):
            for i in range(nop):
                remote(i, k, False).start()

    def wait():
        for k in range(N_DEV - 1):
            for i in range(nop):
                remote(i, k, True).wait_recv()
        for k in range(N_DEV - 1):
            for i in range(nop):
                remote(i, k, False).wait_send()
        for i in range(nop):
            local(i).wait()

    return start, wait


def adamw_sum(name, parts, w, m, v, layer=None, into=None):
    rws, cols = w.shape[-2:]
    nsum = parts.shape[0]
    tr = _pick(rws, (256, 128, 64, 32, 16, 8))
    c1 = 1.0 / (1.0 - ADAM_B1 ** ADAM_STEP)
    c2 = 1.0 / (1.0 - ADAM_B2 ** ADAM_STEP)

    def body(p_ref, w_ref, m_ref, v_ref, *rest):
        g_ref, d_ref, nm_ref, nv_ref = rest[-4:]
        g = p_ref[0]
        for j in range(1, nsum):
            g = g + p_ref[j]
        nm = ADAM_B1 * m_ref[...] + (1.0 - ADAM_B1) * g
        nv = ADAM_B2 * v_ref[...] + (1.0 - ADAM_B2) * (g * g)
        g_ref[...] = g
        nm_ref[...] = nm
        nv_ref[...] = nv
        d_ref[...] = -ADAM_LR * ((nm * c1) / (jnp.sqrt(nv * c2) + ADAM_EPS) + ADAM_WD * w_ref[...])

    if layer is None:
        blk = pl.BlockSpec((tr, cols), lambda i: (i, 0))
    else:
        blk = pl.BlockSpec((None, tr, cols), lambda i: (layer, i, 0))
    if into is None and layer is not None:
        into = [lax.empty(w.shape, F32) for _ in range(4)]
    extra = list(into) if into else []
    return pl.pallas_call(
        body, name=name, grid=(rws // tr,),
        in_specs=([pl.BlockSpec((nsum, tr, cols), lambda i: (0, i, 0)), blk, blk, blk]
                  + [pl.BlockSpec(memory_space=pl.ANY)] * len(extra)),
        out_specs=[blk, blk, blk, blk],
        out_shape=[jax.ShapeDtypeStruct(w.shape, F32)] * 4,
        input_output_aliases={4 + k: k for k in range(len(extra))},
        compiler_params=_params(("parallel",)),
    )(parts, w, m, v, *extra)


def ada_fwd(name, c_all, w, b):
    nl = w.shape[0]

    def body(c_ref, w_ref, b_ref, o_ref):
        ca = _silu(c_ref[...])
        for l in range(nl):
            o_ref[l] = mm_nn(ca, w_ref[l]) + b_ref[l]

    return pl.pallas_call(
        body, name=name,
        out_shape=jax.ShapeDtypeStruct((nl, c_all.shape[0], w.shape[2]), F32),
        compiler_params=pltpu.CompilerParams(vmem_limit_bytes=VMEM_LIMIT),
    )(c_all, w, b)


def ada_bwd(name, c_all, dmod):
    nl = dmod.shape[0]

    def body(c_ref, g_ref, o_ref):
        ca = _silu(c_ref[...])
        for l in range(nl):
            o_ref[l] = mm_tn(ca, g_ref[l])

    return pl.pallas_call(
        body, name=name,
        out_shape=jax.ShapeDtypeStruct((nl, c_all.shape[1], dmod.shape[2]), F32),
        compiler_params=pltpu.CompilerParams(vmem_limit_bytes=VMEM_LIMIT),
    )(c_all, dmod)


def lower_bounds_fn(rows, params):
    (lg,), _ = rows, params
    nl = lg.shape[0]
    mx = jnp.max(lg, axis=0, keepdims=True)
    e = jnp.exp(lg - mx)
    p = e / jnp.sum(e, axis=0, keepdims=True)
    layer = _iota((nl, 1), 0)
    acc = jnp.zeros_like(p)
    for j in range(1, nl):
        pj = jnp.sum(jnp.where(layer == j, p, 0.0), axis=0, keepdims=True)
        acc = acc + jnp.where(layer >= j, 1.0, 0.0) * pj
    return (acc,)


def loss_call(name, x, tgt, nw, tm):
    s, d = x.shape

    def body(x_ref, t_ref, w_ref, l_ref, dx_ref, dw_ref):
        def f(xv, wv):
            err = _rms(xv, wv) - t_ref[...]
            return jnp.sum(0.5 * jnp.mean(err * err, axis=-1, keepdims=True), axis=0, keepdims=True)

        val, vjp = jax.vjp(f, x_ref[...], w_ref[...])
        dx, dw = vjp(jnp.ones_like(val))

        @pl.when(pl.program_id(0) == 0)
        def _():
            l_ref[...] = jnp.zeros_like(l_ref)
            dw_ref[...] = jnp.zeros_like(dw_ref)

        l_ref[...] += jnp.broadcast_to(val, l_ref.shape)
        dw_ref[...] += dw
        dx_ref[...] = dx

    row = pl.BlockSpec((tm, d), lambda i: (i, 0))
    return pl.pallas_call(
        body, name=name, grid=(s // tm,),
        in_specs=[row, row, _whole(nw)],
        out_specs=[pl.BlockSpec((8, LANES), lambda i: (0, 0)), row, _whole(nw)],
        out_shape=[jax.ShapeDtypeStruct((8, LANES), F32), jax.ShapeDtypeStruct((s, d), F32),
                   jax.ShapeDtypeStruct(nw.shape, F32)],
        compiler_params=_params(("arbitrary",)),
    )(x, tgt, nw)


class Dims:
    def __init__(self, s, d, ffn):
        self.s, self.d, self.ffn = s, d, ffn
        self.mix = 3 * d // 4
        self.nh = self.mix // HEAD
        self.ssm_heads = self.mix // SSM_P
        self.pairs = self.mix // (2 * SSM_P)
        self.nc = s // CHUNK
        self.conv_ssm = self.mix + 4 * HEAD
        self.conv_w = self.conv_ssm + 3 * self.mix
        self.o_gates = 4 * self.mix
        self.o_sz = self.o_gates + 3 * d
        self.o_gz = self.o_sz + self.mix
        self.o_conv = self.o_gz + self.mix
        self.o_small = self.o_conv + self.conv_w
        used = self.o_small + LANES
        self.np = -(-used // 1280) * 1280
        self.tm = _pick(s, (256, 128, 64))
        self.tm_wide = _pick(s, (1024, 512, 256, 128, 64))
        mix, nh = self.mix, self.nh
        self.in_sizes = (mix, mix, mix, mix, mix, self.conv_ssm, self.ssm_heads, 3 * mix, mix, nh, nh, 3 * d)
        self.in_width = sum(self.in_sizes)


def w_in_tables(dm, nshard):
    off = np.cumsum((0,) + dm.in_sizes)
    hq, hf, hi, hg, sz, sxbc, sdt, gqkv, gz, gb, ga, gates = (np.arange(off[i], off[i + 1]) for i in range(12))
    hgrn = np.stack([t.reshape(dm.nh, HEAD) for t in (hq, hf, hi, hg)], axis=1).reshape(-1)
    perm = np.concatenate([hgrn, gates, sz, gz, gqkv, sxbc, sdt, gb, ga])
    perm = np.concatenate([perm, np.full(dm.np - perm.size, -1)])
    shard = dm.in_width // nshard
    wpad = -(-shard // GATHER_TILE) * GATHER_TILE
    fwd = np.where(perm >= 0, (perm // shard) * wpad + perm % shard, -1)[None]
    inv = np.zeros(dm.in_width, np.int64)
    inv[perm[perm >= 0]] = np.nonzero(perm >= 0)[0]
    bwd = np.full((nshard, wpad), -1)
    bwd[:, :shard] = inv.reshape(nshard, shard)
    return fwd.astype(np.int32), bwd.astype(np.int32)


def _small_views(dm, small):
    t = small.T
    col = lambda a: a[:, :, None]
    row = lambda a: a.reshape(a.shape[0], dm.nc, 1, CHUNK)
    a, b = dm.ssm_heads, dm.ssm_heads + dm.nh
    sdt, gb, ga = t[:a], t[a:b], t[b:b + dm.nh]
    return col(sdt), row(sdt), col(gb), col(ga), row(ga)


def _scan_specs(dm, proj, conv_out, views, lp, dproj=None):
    dt_col, dt_row, gb_col, ga_col, ga_row = views
    mixb, nh = dm.mix // LANES, dm.nh
    s, mix = dm.s, dm.mix
    lane = ("lane", LANES)
    hb = HGRN_HEADS_PER_STEP
    hw = (CHUNK, hb * LANES)
    hgrn = dict(
        nblk=nh // hb, hb=hb, fn=hgrn_chunk, batched=False, state=(HEAD, HEAD),
        seqs=[(proj, (CHUNK, hb * 4 * HEAD), lambda h, n: (n, h), ("lane", 4 * HEAD))],
        hparams=[(lp["lb"], (1, hb * HEAD), lambda h: (0, h), lane)],
        sparams=[lp["hgrn_norm"]],
        dseqs=[((s, dm.np), BF16, (CHUNK, hb * 4 * HEAD), lambda h, n: (n, h), ("lane", 4 * HEAD), dproj)],
        io=(hw, lambda h, n: (n, h), lane))
    ppg = dm.pairs // 2
    qb = 3 * mixb
    gw = (CHUNK, ppg * LANES)
    group = ("lane", ppg * LANES)
    pcol = ((2 * ppg, CHUNK, 1), lambda g, n: (g, n, 0), ("lead", 2 * ppg))
    prow = ((2 * ppg, None, 1, CHUNK), lambda g, n: (g, n, 0, 0), ("lead", 2 * ppg))
    ppar = ((2 * ppg, 1, 1), lambda g: (g, 0, 0), ("lead", 2 * ppg))
    bc = lambda first: ((CHUNK, LANES), lambda g, n: (n, first + g), None)
    ssd = dict(
        nblk=2, hb=1, fn=ssd_chunk, batched=False, state=(HEAD, ppg * LANES),
        seqs=[(conv_out, gw, lambda g, n: (n, qb // ppg + g), group), (conv_out,) + bc(qb + mixb), (conv_out,) + bc(qb + mixb + 2),
              (dt_col,) + pcol, (dt_row,) + prow],
        hparams=[(lp["ssm_dt_bias"],) + ppar, (lp["ssm_a_log"],) + ppar],
        sparams=[],
        dseqs=[((s, mix), F32, gw, lambda g, n: (n, g), group), ((s, 2 * LANES), F32) + bc(0), ((s, 2 * LANES), F32) + bc(0),
               (dt_col.shape, F32) + pcol, (dt_row.shape, F32) + prow],
        io=(gw, lambda g, n: (n, g), group))
    hb = GDN_HEADS_PER_STEP
    hw = (CHUNK, hb * LANES)
    cq, cgz = 0, dm.o_gz // LANES
    assert nh % hb == 0 and cgz % hb == 0 and qb % ppg == 0
    hcol = ((hb, CHUNK, 1), lambda h, n: (h, n, 0), ("idx",))
    hrow = ((hb, None, 1, CHUNK), lambda h, n: (h, n, 0, 0), ("idx",))
    hpar = ((hb, 1, 1), lambda h: (h, 0, 0), ("idx",))
    at = lambda first: (hw, lambda h, n: (n, first // hb + h), lane)
    gdn = dict(
        nblk=nh // hb, hb=hb, fn=gdn_chunk, batched=True, state=(HEAD, HEAD),
        seqs=[(conv_out,) + at(cq), (conv_out,) + at(cq + nh), (conv_out,) + at(cq + 2 * nh), (proj,) + at(cgz),
              (gb_col,) + hcol, (ga_col,) + hcol, (ga_row,) + hrow],
        hparams=[(lp["gdn_dt_bias"],) + hpar, (lp["gdn_a_log"],) + hpar],
        sparams=[lp["gdn_norm"]],
        dseqs=[((s, mix), F32) + at(0), ((s, mix), F32) + at(0), ((s, mix), F32) + at(0), ((s, dm.np), BF16) + at(cgz) + (dproj,),
               (gb_col.shape, F32) + hcol, (ga_col.shape, F32) + hcol, (ga_row.shape, F32) + hrow],
        io=(hw, lambda h, n: (n, h), lane))
    return hgrn, ssd, gdn


def _run_scan_fwd(dm, name, sp, side=None):
    out = ((dm.s, dm.mix), F32) + sp["io"]
    (y,), states, arrived = scan_fwd(name, sp["fn"], sp["nblk"], sp["hb"], dm.nc, sp["seqs"], sp["hparams"],
                                     sp["sparams"], sp["state"], [out], sp["batched"], side)
    return y, states, arrived


def _run_scan_bwd(dm, name, sp, states, dy, side=None):
    return scan_bwd(name, sp["fn"], sp["nblk"], sp["hb"], dm.nc, sp["seqs"], sp["hparams"], sp["sparams"], sp["state"],
                    states, [(dy,) + sp["io"]], sp["dseqs"], sp["batched"], side)


SHARE_FWD = ((4, 1), (2,), (0,), (3,))
SHARE_BWD = ((0,), (4, 2), (3,), (1,))


def _share_out(side, share):
    if side is None:
        return None, None, None, None
    s, broadcast = side
    return tuple(([s[i] for i in idx], broadcast) for idx in share)


def _collect(share, *got):
    if not got[0]:
        return None
    out = [None] * len(GATHERED)
    for idx, arrived in zip(share, got):
        for i, t in zip(idx, arrived):
            out[i] = t
    return out


def layer_fwd(dm, l, x, lp, side=None):
    tm, d, mix = dm.tm, dm.d, dm.mix
    tag = f"l{l}_"
    (h,) = rowstage_fwd(tag + "norm1", normmod_fn, [(x, d, 0)], [lp["norm_mix"], lp["sc1"], lp["sh1"]], [(d, BF16)], tm)
    side_h, side_s, side_g, side_m = _share_out(side, SHARE_FWD)
    if side:
        proj, got_m = matmul(tag + "proj", h, lp["w_in"], "nn", F32, side_m)
    else:
        proj, got_m = matmul(tag + "proj", h, lp["w_in"], "nn", F32), None
    conv_out = conv_fwd(tag + "conv", proj, dm.o_conv // LANES, lp["conv_w"], lp["conv_b"])
    small = proj[:, dm.o_small:dm.o_small + LANES]
    views = _small_views(dm, small)
    hg, sd, gd = _scan_specs(dm, proj, conv_out, views, lp)
    yh, st_h, got_h = _run_scan_fwd(dm, tag + "hgrn", hg, side_h)
    y_ssd, st_s, got_s = _run_scan_fwd(dm, tag + "ssd", sd, side_s)
    yg, st_g, got_g = _run_scan_fwd(dm, tag + "gdn", gd, side_g)
    arrived = _collect(SHARE_FWD, got_h, got_s, got_g, got_m)
    (ys,) = rowstage_fwd(tag + "ssmpost", ssmpost_fn,
                         [(y_ssd, mix, 0), (conv_out, mix, 3), (proj, mix, dm.o_sz // mix)],
                         [lp["ssm_d_exp"], lp["ssm_norm"]], [(mix, F32)], tm)
    (merged,) = rowstage_fwd(tag + "merge", merge_fn, [(yh, mix, 0), (ys, mix, 0), (yg, mix, 0), (proj, 3 * d, 1)],
                             [lp["b_merge"], lp["w_branch"]], [(d, BF16)], tm)
    (x1,) = rowstage_fwd(tag + "outproj", outproj_fn, [(merged, d, 0), (x, d, 0)], [lp["g1"], lp["w_out"]], [(d, F32)], tm)
    (h2,) = rowstage_fwd(tag + "norm2", normmod_fn, [(x1, d, 0)], [lp["norm_ffn"], lp["sc2"], lp["sh2"]], [(d, BF16)], tm)
    gu = bmatmul(tag + "ffn_in", h2, lp["w_ffn_in"], "nn", BF16, True)
    gu = gu.reshape((2, gu.shape[0] // 2) + gu.shape[1:])
    act = swiglu3_fwd(tag + "swiglu", gu, dm.tm_wide)
    o2 = bmatmul(tag + "ffn_out", act, lp["w_ffn_out"], "nn", F32, False)
    (x2,) = rowstage_fwd(tag + "resid", resid_fn, [(x1, d, 0), (o2, d, 0)], [lp["g2"]], [(d, F32)], tm)
    saved = dict(x=x, h=h, proj=proj, conv_out=conv_out, views=views, yh=yh, y_ssd=y_ssd, yg=yg, ys=ys,
                 st_h=st_h, st_s=st_s, st_g=st_g, merged=merged, x1=x1, h2=h2, gu=gu, act=act, o2=o2)
    return x2, saved, arrived


def layer_bwd(dm, l, dx2, lp, sv, side=None, own=False):
    tm, d, mix, s = dm.tm, dm.d, dm.mix, dm.s
    tag = f"l{l}_b_"
    g = {}
    (dx1_a, do2), (g["g2"],) = rowstage_bwd(tag + "resid", resid_fn, [(sv["x1"], d, 0), (sv["o2"], d, 0)], [lp["g2"]],
                                            [dx2], [F32, BF16], tm)
    dact = bmatmul(tag + "ffn_out_dx", do2, lp["w_ffn_out"], "nt", BF16, True)
    g["w_ffn_out"] = bmatmul(tag + "ffn_out_dw", sv["act"], do2, "tn", F32, True)
    dgu = swiglu3_bwd(tag + "swiglu", sv["gu"], dact, dm.tm_wide)
    dgu = dgu.reshape((-1,) + dgu.shape[2:])
    dh2 = bmatmul(tag + "ffn_in_dx", dgu, lp["w_ffn_in"], "nt", BF16, False)
    g["w_ffn_in"] = bmatmul(tag + "ffn_in_dw", sv["h2"], dgu, "tn", F32, True)
    (dx1,), (g["norm_ffn"], g["sc2"], g["sh2"]) = rowstage_bwd(
        tag + "norm2", normmod_fn, [(sv["x1"], d, 0)], [lp["norm_ffn"], lp["sc2"], lp["sh2"]], [dh2], [F32], tm,
        adds={0: dx1_a})
    (dmerged, dx_a), (g["g1"], g["w_out"]) = rowstage_bwd(
        tag + "outproj", outproj_fn, [(sv["merged"], d, 0), (sv["x"], d, 0)], [lp["g1"], lp["w_out"]], [dx1],
        [BF16, F32], tm)
    proj, conv_out = sv["proj"], sv["conv_out"]
    dproj = lax.empty((s, dm.np), BF16)
    (dyh, dys, dyg, dproj), (g["b_merge"], g["w_branch"]) = rowstage_bwd(
        tag + "merge", merge_fn, [(sv["yh"], mix, 0), (sv["ys"], mix, 0), (sv["yg"], mix, 0), (proj, 3 * d, 1)],
        [lp["b_merge"], lp["w_branch"]], [dmerged], [F32, F32, F32, BF16], tm, into={3: (dproj, 1)})
    (dy_ssd, dxs_a, dproj), (g["ssm_d_exp"], g["ssm_norm"]) = rowstage_bwd(
        tag + "ssmpost", ssmpost_fn, [(sv["y_ssd"], mix, 0), (conv_out, mix, 3), (proj, mix, dm.o_sz // mix)],
        [lp["ssm_d_exp"], lp["ssm_norm"]], [dys], [F32, F32, BF16], tm, into={2: (dproj, dm.o_sz // mix)})
    side_h, side_s, side_g, side_m = _share_out(side, SHARE_BWD)
    hg, sd, _ = _scan_specs(dm, proj, conv_out, sv["views"], lp, dproj)
    (dproj,), (g["lb"],), (g["hgrn_norm"],), got_h = _run_scan_bwd(dm, tag + "hgrn", hg, sv["st_h"], dyh, side_h)
    gd = _scan_specs(dm, proj, conv_out, sv["views"], lp, dproj)[2]
    (dxs_b, dbp, dcp, d_dt_col, d_dt_row), (g["ssm_dt_bias"], g["ssm_a_log"]), _, got_s = _run_scan_bwd(
        dm, tag + "ssd", sd, sv["st_s"], dy_ssd, side_s)
    (dq, dk, dv, dproj, d_gb_col, d_ga_col, d_ga_row), (g["gdn_dt_bias"], g["gdn_a_log"]), (g["gdn_norm"],), got_g = _run_scan_bwd(
        dm, tag + "gdn", gd, sv["st_g"], dyg, side_g)
    dconv =jnp.concatenate([dq, dk, dv, dxs_a + dxs_b, dbp, dcp], axis=1)
    dproj, g["conv_w"], g["conv_b"] = conv_bwd(tag + "conv", proj, dm.o_conv // LANES, lp["conv_w"], lp["conv_b"], dconv,
                                               dproj)
    unrow = lambda t: t.reshape(t.shape[0], s).T
    dsmall = jnp.concatenate([d_dt_col[:, :, 0].T + unrow(d_dt_row), d_gb_col[:, :, 0].T,
                              d_ga_col[:, :, 0].T + unrow(d_ga_row)], axis=1)
    tail = jnp.pad(dsmall.astype(BF16), ((0, 0), (0, dm.np - dm.o_small - dsmall.shape[1])))
    dproj = lax.dynamic_update_slice(dproj, tail, (0, dm.o_small))
    beside_dx = list(side_m[0]) if side else []
    if own:
        s_wb, s_wout, s_wf, s_wfo = small_shards(dm, g)
        beside_dx = beside_dx + [s_wf]
        g["w_in"], (got_wb, got_wout, got_wfo) = matmul(tag + "proj_dw", sv["h"], dproj, "tn", F32,
                                                        ([s_wb, s_wout, s_wfo], False))
    else:
        g["w_in"] = matmul(tag + "proj_dw", sv["h"], dproj, "tn", F32)
    if beside_dx:
        dh, got_dx = matmul(tag + "proj_dx", dproj, lp["w_in"], "nt", BF16, (beside_dx, False))
    else:
        dh, got_dx = matmul(tag + "proj_dx", dproj, lp["w_in"], "nt", BF16), []
    arrived = _collect(SHARE_BWD, got_h, got_s, got_g, got_dx[:1]) if side else None
    (dx,), (g["norm_mix"], g["sc1"], g["sh1"]) = rowstage_bwd(
        tag + "norm1", normmod_fn, [(sv["x"], d, 0)], [lp["norm_mix"], lp["sc1"], lp["sh1"]], [dh], [F32], tm,
        adds={0: dx_a})
    if own:
        return dx, g, arrived, [got_wb, got_wout, got_dx[-1], got_wfo]
    return dx, g, arrived


WEIGHTS = ("w_ada", "b_ada", "norm_mix", "norm_ffn", "w_in", "b_merge", "hgrn_lb_logits", "hgrn_norm", "ssm_conv_w",
           "ssm_conv_b", "ssm_dt_bias", "ssm_a_log", "ssm_d", "ssm_norm", "gdn_conv_w", "gdn_dt_bias", "gdn_a_log",
           "gdn_norm", "w_branch", "w_out", "w_ffn_in", "w_ffn_out", "norm_final")
GATHERED = ("w_in", "w_branch", "w_out", "w_ffn_in", "w_ffn_out")
PACKET = ("b_ada", "norm_mix", "norm_ffn", "b_merge", "hgrn_norm", "ssm_conv_b", "ssm_dt_bias", "ssm_a_log", "ssm_d",
          "ssm_norm", "gdn_dt_bias", "gdn_a_log", "gdn_norm", "norm_final")
MISC = ("hgrn_lb_logits", "ssm_conv_w", "gdn_conv_w")


def _pack(arrs, dtype, row_mult, lead=0):
    flat = jnp.concatenate([t.reshape(t.shape[:lead] + (-1,)).astype(dtype) for t in arrs], axis=lead)
    n = flat.shape[-1]
    unit = row_mult * LANES
    tot = -(-n // unit) * unit
    flat = jnp.pad(flat, [(0, 0)] * lead + [(0, tot - n)])
    return flat.reshape(flat.shape[:lead] + (tot // LANES, LANES))


def _unpack(packed, shapes, lead=0):
    flat = packed.reshape(packed.shape[:lead] + (-1,))
    out, off = [], 0
    for shp in shapes:
        n = int(np.prod(shp))
        out.append(flat[..., off:off + n].reshape(flat.shape[:lead] + tuple(shp)))
        off += n
    return out


def _shard2d(t):
    return t.reshape((-1, t.shape[-1]))


def weights_from_shards(dm, l, got, idx):
    w_in, wb, w_out, wf, wfo = got
    d, mix = dm.d, dm.mix
    return dict(
        w_in=colgather(f"l{l}_w_in", w_in, idx, dm.np, BF16)[0],
        w_branch=wb.reshape(N_DEV, 3, mix, d // N_DEV).transpose(1, 2, 0, 3).reshape(3, mix, d),
        w_out=w_out.reshape(d, d), w_ffn_in=wf, w_ffn_out=wfo.reshape(N_DEV // 2, -1, d))


def small_shards(dm, g):
    d, mix = dm.d, dm.mix
    return [g["w_branch"].reshape(3, mix, N_DEV, d // N_DEV).transpose(2, 0, 1, 3).reshape(N_DEV, 3 * mix, d // N_DEV),
            g["w_out"].reshape(N_DEV, d // N_DEV, d), g["w_ffn_in"], g["w_ffn_out"].reshape(N_DEV, -1, d)]


def w_in_shards(dm, l, g, idx):
    return colgather(f"l{l}_g_w_in", g["w_in"][None], idx, dm.in_width // N_DEV, F32)


def layer_params(dm, l, full, small, mod_l, lb_l):
    d, mix = dm.d, dm.mix
    row = lambda t: t.reshape(1, -1)
    head = lambda t: t.reshape(-1, 1, 1)
    sh1, sc1, g1, sh2, sc2, g2 = (row(mod_l[i * d:(i + 1) * d]) for i in range(6))
    conv_b = jnp.concatenate([jnp.zeros((3 * mix,), F32), small["ssm_conv_b"][l]])
    return dict(
        w_in=full["w_in"], w_branch=full["w_branch"], w_out=full["w_out"],
        w_ffn_in=full["w_ffn_in"], w_ffn_out=full["w_ffn_out"],
        norm_mix=row(small["norm_mix"][l]), norm_ffn=row(small["norm_ffn"][l]), b_merge=row(small["b_merge"][l]),
        hgrn_norm=row(small["hgrn_norm"][l]), lb=row(lb_l),
        conv_w=jnp.concatenate([small["gdn_conv_w"][l], small["ssm_conv_w"][l]], axis=1), conv_b=row(conv_b),
        ssm_dt_bias=head(small["ssm_dt_bias"][l]), ssm_a_log=head(small["ssm_a_log"][l]),
        ssm_d_exp=row(jnp.repeat(small["ssm_d"][l], SSM_P)), ssm_norm=row(small["ssm_norm"][l]),
        gdn_dt_bias=head(small["gdn_dt_bias"][l]), gdn_a_log=head(small["gdn_a_log"][l]), gdn_norm=row(small["gdn_norm"][l]),
        sh1=sh1, sc1=sc1, g1=g1, sh2=sh2, sc2=sc2, g2=g2)


def layer_grads(dm, g):
    cs = 3 * dm.mix
    out = dict(
        w_in=g["w_in"], w_branch=g["w_branch"], w_out=g["w_out"], w_ffn_in=g["w_ffn_in"],
        w_ffn_out=g["w_ffn_out"], norm_mix=g["norm_mix"][0], norm_ffn=g["norm_ffn"][0], b_merge=g["b_merge"][0],
        hgrn_norm=g["hgrn_norm"][0], ssm_conv_w=g["conv_w"][:, cs:], gdn_conv_w=g["conv_w"][:, :cs],
        ssm_conv_b=g["conv_b"][0, cs:], ssm_dt_bias=g["ssm_dt_bias"][:, 0, 0], ssm_a_log=g["ssm_a_log"][:, 0, 0],
        ssm_d=g["ssm_d_exp"].reshape(dm.ssm_heads, SSM_P).sum(axis=1), ssm_norm=g["ssm_norm"][0],
        gdn_dt_bias=g["gdn_dt_bias"][:, 0, 0], gdn_a_log=g["gdn_a_log"][:, 0, 0], gdn_norm=g["gdn_norm"][0])
    dmod = jnp.concatenate([g[k][0] for k in ("sh1", "sc1", "g1", "sh2", "sc2", "g2")])
    return out, dmod, g["lb"][0]


def local_step(dm, nl, x, tgt, norm_final, params_of, gather_of=None, scatter_of=None):
    arrived = exchange("gather_w0", gather_of(0), True) if gather_of else None
    lps, saved = [], []
    for l in range(nl):
        lps.append(params_of(l, arrived))
        side = (gather_of(l + 1), True) if gather_of and l + 1 < nl else None
        x, sv, arrived = layer_fwd(dm, l, x, lps[l], side)
        saved.append(sv)
    loss, dx, dnf = loss_call("loss", x, tgt, norm_final, dm.tm)
    grads, parts, side = [None] * nl, [None] * nl, None
    for l in reversed(range(nl)):
        if scatter_of and l == 0:
            dx, grads[l], got, own = layer_bwd(dm, l, dx, lps[l], saved[l], side, own=True)
            parts[0] = list(exchange("scatter_g0", [scatter_of(0, grads[0])], False)) + own
        else:
            dx, grads[l], got = layer_bwd(dm, l, dx, lps[l], saved[l], side)
        if side is not None:
            parts[l + 1] = got
        side = ([scatter_of(l, grads[l])] + small_shards(dm, grads[l]), False) if scatter_of and l > 0 else None
    return loss, dx, dnf, grads, parts


def kernel(x, c, w_ada, b_ada, norm_mix, norm_ffn, w_in, b_merge, hgrn_lb_logits, hgrn_norm, ssm_conv_w, ssm_conv_b, ssm_dt_bias, ssm_a_log, ssm_d, ssm_norm, gdn_conv_w, gdn_dt_bias, gdn_a_log, gdn_norm, w_branch, w_out, w_ffn_in, w_ffn_out, norm_final, loss_target, m_w_ada, m_b_ada, m_norm_mix, m_norm_ffn, m_w_in, m_b_merge, m_hgrn_lb_logits, m_hgrn_norm, m_ssm_conv_w, m_ssm_conv_b, m_ssm_dt_bias, m_ssm_a_log, m_ssm_d, m_ssm_norm, m_gdn_conv_w, m_gdn_dt_bias, m_gdn_a_log, m_gdn_norm, m_w_branch, m_w_out, m_w_ffn_in, m_w_ffn_out, m_norm_final, v_w_ada, v_b_ada, v_norm_mix, v_norm_ffn, v_w_in, v_b_merge, v_hgrn_lb_logits, v_hgrn_norm, v_ssm_conv_w, v_ssm_conv_b, v_ssm_dt_bias, v_ssm_a_log, v_ssm_d, v_ssm_norm, v_gdn_conv_w, v_gdn_dt_bias, v_gdn_a_log, v_gdn_norm, v_w_branch, v_w_out, v_w_ffn_in, v_w_ffn_out, v_norm_final):
    a = dict(locals())
    x, tgt = a["x"][0], a["loss_target"][0]
    s, d = x.shape
    nl = a["w_ada"].shape[0]
    dm = Dims(s, d, a["w_ffn_out"].shape[1] * N_DEV)
    me = 4 * lax.axis_index("x") + 2 * lax.axis_index("y") + lax.axis_index("c")

    first = [a["c"], a["ssm_conv_w"], a["gdn_conv_w"]]
    c_all, scw, gcw = _unpack(exchange("gather_c", [_pack(first, F32, 8)], True)[0], [t.shape for t in first], lead=1)
    small = dict(a, ssm_conv_w=scw.transpose(1, 2, 0, 3).reshape(scw.shape[1:3] + (-1,)),
                 gdn_conv_w=gcw.transpose(1, 2, 0, 3).reshape(gcw.shape[1:3] + (-1,)))
    c_pad = jnp.zeros((LANES, d), F32).at[:N_DEV].set(c_all.reshape(N_DEV, d))
    ncol = a["w_ada"].shape[2]
    b_mine = lax.dynamic_slice(a["b_ada"], (0, me * ncol), (nl, ncol))[:, None, :]
    mod_part = ada_fwd("ada_fwd", c_pad, a["w_ada"], b_mine)[:, :N_DEV, :]
    (mod,) = exchange("a2a_mod", [mod_part.transpose(1, 0, 2)], False)
    mod = mod.transpose(1, 0, 2).reshape(nl, N_DEV * ncol)
    (lb,) = rowstage_fwd("lower_bounds", lower_bounds_fn, [(a["hgrn_lb_logits"], dm.mix, 0)], [], [(dm.mix, F32)], nl)

    idx_fwd, idx_bwd = w_in_tables(dm, N_DEV)
    loss, dx, dnf, grads, parts = local_step(
        dm, nl, x, tgt, a["norm_final"].reshape(1, d),
        params_of=lambda l, got: layer_params(dm, l, weights_from_shards(dm, l, got, idx_fwd), small, mod[l], lb[l]),
        gather_of=lambda l: [_shard2d(a[n][l]).astype(BF16) for n in GATHERED],
        scatter_of=lambda l, g: w_in_shards(dm, l, g, idx_bwd))

    per_layer = [layer_grads(dm, g) for g in grads]
    res = {}
    for i, n in enumerate(GATHERED):
        wmv = [a[q + n].reshape((nl, -1, a[n].shape[-1])) for q in ("", "m_", "v_")]
        outs = None
        for l in range(nl):
            outs = adamw_sum(f"adamw_l{l}_{n}", parts[l][i], *wmv, layer=l, into=outs)
        for kind, o in zip(("grad", "delta", "new_m", "new_v"), outs):
            res[(kind, n)] = o.reshape(a[n].shape)

    stackg = lambda n: jnp.stack([pl_[0][n] for pl_ in per_layer])
    dmod = jnp.stack([pl_[1] for pl_ in per_layer])
    dlb = jnp.stack([pl_[2] for pl_ in per_layer])
    pk_g = [dmod if n == "b_ada" else dnf if n == "norm_final" else stackg(n) for n in PACKET]
    extra = [dlb, stackg("ssm_conv_w"), stackg("gdn_conv_w"), loss[0, :1]]
    pk_shapes = [t.shape for t in pk_g + extra]
    zeros = [jnp.zeros(t.shape, F32) for t in extra]
    (parts,) = exchange("gather_small", [_pack(pk_g + extra, F32, 8)], True)
    outs = adamw_sum("adamw_small", parts, *[_pack([a[p + n] for n in PACKET] + zeros, F32, 8) for p in ("", "m_", "v_")])
    for kind, o in zip(("grad", "delta", "new_m", "new_v"), outs):
        un = _unpack(o, pk_shapes)
        for n, t in zip(PACKET, un):
            res[(kind, n)] = t.reshape(a[n].shape)
        if kind == "grad":
            dlb_sum, g_scw, g_gcw, loss_sum = un[len(PACKET):]

    (g_lb,), _ = rowstage_bwd("lower_bounds_b", lower_bounds_fn, [(a["hgrn_lb_logits"], dm.mix, 0)], [], [dlb_sum], [F32], nl)
    mine = lambda t, n: lax.dynamic_slice_in_dim(t, me * a[n].shape[-1], a[n].shape[-1], axis=t.ndim - 1)
    (dmod_cols,) = exchange("a2a_dmod", [dmod.reshape(nl, N_DEV, ncol).transpose(1, 0, 2)], False)
    dmod_pad = jnp.zeros((nl, LANES, ncol), F32).at[:, :N_DEV].set(dmod_cols.transpose(1, 0, 2))
    g_w_ada = ada_bwd("ada_bwd", c_pad, dmod_pad)
    outs = adamw_sum("adamw_w_ada", g_w_ada.reshape(1, nl * d, ncol), *[a[q + "w_ada"].reshape(nl * d, ncol) for q in ("", "m_", "v_")])
    for kind, o in zip(("grad", "delta", "new_m", "new_v"), outs):
        res[(kind, "w_ada")] = o.reshape(nl, d, ncol)
    g_misc = [g_lb, mine(g_scw, "ssm_conv_w"), mine(g_gcw, "gdn_conv_w")]
    outs = adamw_sum("adamw_misc", _pack(g_misc, F32, 8)[None], *[_pack([a[q + n] for n in MISC], F32, 8) for q in ("", "m_", "v_")])
    for kind, o in zip(("grad", "delta", "new_m", "new_v"), outs):
        for n, t in zip(MISC, _unpack(o, [a[n].shape for n in MISC])):
            res[(kind, n)] = t

    out = [loss_sum.reshape(()), dx[None]]
    for kind in ("grad", "delta", "new_m", "new_v"):
        out += [res[(kind, n)] for n in WEIGHTS]
    return tuple(out)
```
